```python
import math
import jax, jax.numpy as jnp
from jax import lax
import numpy as np

D_MODEL = 1024
BATCH = 4
SEQ = 4096
DEPTH = 1

HEAD_DIM = 64
N_DIFF_HEADS = 4
DIFF_V_DIM = 2 * HEAD_DIM
N_SWA_HEADS = 8
N_SWA_KV = 2
SWA_GROUP = N_SWA_HEADS // N_SWA_KV
WINDOW = 128
BLOCK = 128
N_META = 16
N_BUCKETS = 32
MAX_DISTANCE = 128
N_BIAS_HEADS = N_DIFF_HEADS + N_SWA_HEADS
N_GROUPS = 4
EXPERTS_PER_GROUP = 8
N_EXPERTS = N_GROUPS * EXPERTS_PER_GROUP
TOP_K = 2
D_EXPERT = 512
MOE_BLOCK = 128
EPS = 1e-6
NEG = -1e30

DIFF_QK = N_DIFF_HEADS * 2 * HEAD_DIM
DIFF_V = N_DIFF_HEADS * DIFF_V_DIM
SWA_Q = N_SWA_HEADS * HEAD_DIM
SWA_KV = N_SWA_KV * HEAD_DIM
D_IN = 2 * DIFF_QK + DIFF_V + SWA_Q + 2 * SWA_KV
D_MIX = DIFF_V + SWA_Q

kernel_name = "hymba_diffattn_swasink_hmoe"


def rmsnorm(x, g):
    xf = x.astype(jnp.float32)
    y = xf * lax.rsqrt(jnp.mean(xf * xf, axis=-1, keepdims=True) + EPS)
    return (y * g.astype(jnp.float32)).astype(x.dtype)


def t5_bucket(dist):
    n = jnp.maximum(dist, 0)
    max_exact = N_BUCKETS // 2
    nf = jnp.maximum(n, 1).astype(jnp.float32)
    large = max_exact + (jnp.log(nf / max_exact) / math.log(MAX_DISTANCE / max_exact)
                         * (N_BUCKETS - max_exact)).astype(jnp.int32)
    large = jnp.minimum(large, N_BUCKETS - 1)
    return jnp.where(n < max_exact, n, large)


def diff_attention(q, k, v, q_gain, k_gain, lam_q1, lam_k1, lam_q2, lam_k2, subln_gain,
                   bias_table, pos, lambda_init):
    B, Lp, H, _, d = q.shape
    nb = Lp // BLOCK
    q = rmsnorm(q, q_gain)
    k = rmsnorm(k, k_gain)
    lam = (jnp.exp(jnp.sum(lam_q1.astype(jnp.float32) * lam_k1.astype(jnp.float32)))
           - jnp.exp(jnp.sum(lam_q2.astype(jnp.float32) * lam_k2.astype(jnp.float32)))
           + lambda_init)
    scale = d ** -0.5
    qb = q.reshape(B, nb, BLOCK, H, 2, d).transpose(1, 0, 2, 3, 4, 5)

    def block(args):
        qblk, qpos = args
        s = jnp.einsum('bqhmd,bkhmd->bhmqk', qblk, k).astype(jnp.float32) * scale
        dist = qpos[:, None] - pos[None, :]
        bias = bias_table[t5_bucket(dist)].transpose(2, 0, 1)
        s = s + bias[None, :, None].astype(jnp.float32)
        mask = (dist >= 0) & (pos[None, :] >= 0)
        s = jnp.where(mask, s, NEG)
        p = jax.nn.softmax(s, axis=-1)
        a = p[:, :, 0] - lam * p[:, :, 1]
        return jnp.einsum('bhqk,bkhe->bqhe', a.astype(v.dtype), v)

    o = lax.map(block, (qb, pos.reshape(nb, BLOCK)))
    o = o.transpose(1, 0, 2, 3, 4).reshape(B, Lp, H, DIFF_V_DIM)
    o = rmsnorm(o, subln_gain) * (1.0 - lambda_init)
    return o.reshape(B, Lp, H * DIFF_V_DIM)


def swa_sink_attention(q, k, v, q_gain, k_gain, sinks, bias_table, pos, n_pad):
    B, Lp, Hq, d = q.shape
    nb = Lp // BLOCK
    q = rmsnorm(q, q_gain)
    k = rmsnorm(k, k_gain)
    qb = q.reshape(B, nb, BLOCK, N_SWA_KV, SWA_GROUP, d)

    def keys(t):
        tb = t.reshape(B, nb, BLOCK, N_SWA_KV, d)
        prev = jnp.pad(tb[:, :-1], ((0, 0), (1, 0), (0, 0), (0, 0), (0, 0)))
        meta = jnp.broadcast_to(t[:, None, n_pad:n_pad + N_META], (B, nb, N_META, N_SWA_KV, d))
        return jnp.concatenate([meta, prev, tb], axis=2)

    kb, vb = keys(k), keys(v)
    posb = pos.reshape(nb, BLOCK)
    prev_pos = jnp.pad(posb[:-1], ((1, 0), (0, 0)), constant_values=-1)
    meta_pos = jnp.broadcast_to(pos[n_pad:n_pad + N_META], (nb, N_META))
    kpos = jnp.concatenate([meta_pos, prev_pos, posb], axis=1)
    dist = posb[:, :, None] - kpos[:, None, :]
    is_meta = (jnp.arange(kpos.shape[1]) < N_META)[None, None, :]
    band = (dist < WINDOW) & (kpos[:, None, :] >= N_META)
    mask = (dist >= 0) & (is_meta | band)
    bias = bias_table[t5_bucket(dist)]
    bias = bias.transpose(0, 3, 1, 2).reshape(nb, N_SWA_KV, SWA_GROUP, BLOCK, kpos.shape[1])
    s = jnp.einsum('bnqgrd,bnkgd->bngrqk', qb, kb).astype(jnp.float32) * (d ** -0.5)
    s = s + bias[None].astype(jnp.float32)
    s = jnp.where(mask[None, :, None, None], s, NEG)
    sink = jnp.broadcast_to(sinks.astype(jnp.float32).reshape(1, 1, N_SWA_KV, SWA_GROUP, 1, 1),
                            s.shape[:-1] + (1,))
    p = jax.nn.softmax(jnp.concatenate([s, sink], axis=-1), axis=-1)[..., :-1]
    o = jnp.einsum('bngrqk,bnkgd->bnqgrd', p.astype(v.dtype), vb)
    return o.reshape(B, Lp, Hq * d)


def hier_moe(x, w_group, b_group, w_router, b_router, w_gate, w_up, w_down):
    B, Lp, D = x.shape
    N = B * Lp
    t = x.reshape(N, D)
    gl = (t @ w_group).astype(jnp.float32) + b_group.astype(jnp.float32)
    g = jnp.argmax(gl, axis=-1)
    g_w = jnp.take_along_axis(jax.nn.softmax(gl, axis=-1), g[:, None], axis=1)
    el = ((t @ w_router).astype(jnp.float32) + b_router.astype(jnp.float32)).reshape(
        N, N_GROUPS, EXPERTS_PER_GROUP)
    el = jnp.take_along_axis(el, g[:, None, None], axis=1)[:, 0]
    top_l, top_j = lax.top_k(el, TOP_K)
    gates = (jax.nn.softmax(top_l, axis=-1) * g_w).reshape(-1)
    eid = (g[:, None] * EXPERTS_PER_GROUP + top_j).reshape(-1).astype(jnp.int32)
    NK = N * TOP_K
    tok = (jnp.arange(NK, dtype=jnp.int32) // TOP_K)

    order = jnp.argsort(eid)
    se = eid[order]
    counts = jnp.bincount(eid, length=N_EXPERTS)
    start = jnp.cumsum(counts) - counts
    padded = (counts + MOE_BLOCK - 1) // MOE_BLOCK * MOE_BLOCK
    pend = jnp.cumsum(padded)
    pstart = pend - padded
    dest = pstart[se] + jnp.arange(NK, dtype=jnp.int32) - start[se]
    n_blocks = -(-(NK + N_EXPERTS * (MOE_BLOCK - 1)) // MOE_BLOCK)
    P = n_blocks * MOE_BLOCK
    tok_buf = jnp.full((P,), N, jnp.int32).at[dest].set(tok[order])
    gate_buf = jnp.zeros((P,), jnp.float32).at[dest].set(gates[order])
    blk_e = jnp.minimum(jnp.searchsorted(pend, jnp.arange(n_blocks) * MOE_BLOCK, side='right'),
                        N_EXPERTS - 1)
    t_pad = jnp.concatenate([t, jnp.zeros((1, D), t.dtype)], axis=0)
    xs = t_pad[tok_buf].reshape(n_blocks, MOE_BLOCK, D)

    def expert_block(args):
        xb, e = args
        hdn = jax.nn.silu(xb @ w_gate[e]) * (xb @ w_up[e])
        return hdn @ w_down[e]

    ys = lax.map(expert_block, (xs, blk_e)).reshape(P, D)
    ys = ys * gate_buf[:, None].astype(ys.dtype)
    out = jax.ops.segment_sum(ys, tok_buf, num_segments=N + 1)[:N]
    return out.reshape(B, Lp, D)


def setup_inputs(seed: int = 0) -> dict:
    key = jax.random.key(seed)
    ks = jax.random.split(key, 24)
    f = jnp.float32

    def nrm(k, shape, s):
        return jax.random.normal(k, shape, f) * s

    return {
        "x": nrm(ks[0], (BATCH, SEQ, D_MODEL), 1.0),
        "meta_tokens": nrm(ks[1], (N_META, D_MODEL), 1.0),
        "rel_bias": nrm(ks[2], (N_BUCKETS, N_BIAS_HEADS), 0.5),
        "norm1_gain": 1.0 + nrm(ks[3], (DEPTH, D_MODEL), 0.1),
        "w_in": nrm(ks[4], (DEPTH, D_MODEL, D_IN), D_MODEL ** -0.5),
        "diff_q_gain": 1.0 + nrm(ks[5], (DEPTH, HEAD_DIM), 0.1),
        "diff_k_gain": 1.0 + nrm(ks[6], (DEPTH, HEAD_DIM), 0.1),
        "lam_q1": nrm(ks[7], (DEPTH, HEAD_DIM), 0.1),
        "lam_k1": nrm(ks[8], (DEPTH, HEAD_DIM), 0.1),
        "lam_q2": nrm(ks[9], (DEPTH, HEAD_DIM), 0.1),
        "lam_k2": nrm(ks[10], (DEPTH, HEAD_DIM), 0.1),
        "diff_subln_gain": 1.0 + nrm(ks[11], (DEPTH, DIFF_V_DIM), 0.1),
        "swa_q_gain": 1.0 + nrm(ks[12], (DEPTH, HEAD_DIM), 0.1),
        "swa_k_gain": 1.0 + nrm(ks[13], (DEPTH, HEAD_DIM), 0.1),
        "swa_sinks": nrm(ks[14], (DEPTH, N_SWA_HEADS), 1.0),
        "w_out": nrm(ks[15], (DEPTH, D_MIX, D_MODEL), D_MIX ** -0.5),
        "norm2_gain": 1.0 + nrm(ks[16], (DEPTH, D_MODEL), 0.1),
        "w_group": nrm(ks[17], (DEPTH, D_MODEL, N_GROUPS), D_MODEL ** -0.5),
        "b_group": nrm(ks[18], (DEPTH, N_GROUPS), 0.01),
        "w_router": nrm(ks[19], (DEPTH, D_MODEL, N_EXPERTS), D_MODEL ** -0.5),
        "b_router": nrm(ks[20], (DEPTH, N_EXPERTS), 0.01),
        "w_gate": nrm(ks[21], (DEPTH, N_EXPERTS, D_MODEL, D_EXPERT), D_MODEL ** -0.5),
        "w_up": nrm(ks[22], (DEPTH, N_EXPERTS, D_MODEL, D_EXPERT), D_MODEL ** -0.5),
        "w_down": nrm(ks[23], (DEPTH, N_EXPERTS, D_EXPERT, D_MODEL), D_EXPERT ** -0.5),
    }


def reference(x, meta_tokens, rel_bias, norm1_gain, w_in, diff_q_gain, diff_k_gain,
              lam_q1, lam_k1, lam_q2, lam_k2, diff_subln_gain, swa_q_gain, swa_k_gain,
              swa_sinks, w_out, norm2_gain, w_group, b_group, w_router, b_router,
              w_gate, w_up, w_down):
    B, S, D = x.shape
    L = S + N_META
    nb = -(-L // BLOCK)
    n_pad = nb * BLOCK - L
    meta = jnp.broadcast_to(meta_tokens[None].astype(x.dtype), (B, N_META, D))
    h = jnp.concatenate([jnp.zeros((B, n_pad, D), x.dtype), meta, x], axis=1)
    Lp = h.shape[1]
    pos = jnp.arange(Lp, dtype=jnp.int32) - n_pad
    split_at = [DIFF_QK, 2 * DIFF_QK, 2 * DIFF_QK + DIFF_V,
                2 * DIFF_QK + DIFF_V + SWA_Q, 2 * DIFF_QK + DIFF_V + SWA_Q + SWA_KV]
    for layer in range(DEPTH):
        lambda_init = 0.8 - 0.6 * math.exp(-0.3 * layer)
        a = rmsnorm(h, norm1_gain[layer])
        proj = a @ w_in[layer]
        dq, dk, dv, sq, sk, sv = jnp.split(proj, split_at, axis=-1)
        o_diff = diff_attention(
            dq.reshape(B, Lp, N_DIFF_HEADS, 2, HEAD_DIM),
            dk.reshape(B, Lp, N_DIFF_HEADS, 2, HEAD_DIM),
            dv.reshape(B, Lp, N_DIFF_HEADS, DIFF_V_DIM),
            diff_q_gain[layer], diff_k_gain[layer],
            lam_q1[layer], lam_k1[layer], lam_q2[layer], lam_k2[layer],
            diff_subln_gain[layer], rel_bias[:, :N_DIFF_HEADS], pos, lambda_init)
        o_swa = swa_sink_attention(
            sq.reshape(B, Lp, N_SWA_HEADS, HEAD_DIM),
            sk.reshape(B, Lp, N_SWA_KV, HEAD_DIM),
            sv.reshape(B, Lp, N_SWA_KV, HEAD_DIM),
            swa_q_gain[layer], swa_k_gain[layer], swa_sinks[layer],
            rel_bias[:, N_DIFF_HEADS:], pos, n_pad)
        mix = jnp.concatenate([o_diff, o_swa], axis=-1)
        h = h + mix @ w_out[layer]
        h = h + hier_moe(rmsnorm(h, norm2_gain[layer]), w_group[layer], b_group[layer],
                         w_router[layer], b_router[layer], w_gate[layer], w_up[layer],
                         w_down[layer])
    return h[:, n_pad + N_META:]
```

```python
import functools
import math

import numpy as np
import jax
import jax.numpy as jnp
from jax import lax
from jax.experimental import pallas as pl
from jax.experimental.pallas import tpu as pltpu

F32 = jnp.float32
BF16 = jnp.bfloat16

HEAD_DIM = 64
N_DIFF_HEADS = 4
N_SWA_HEADS = 8
N_SWA_KV = 2
BLOCK = 128
N_META = 16
N_BUCKETS = 32
MAX_DISTANCE = 128
N_GROUPS = 4
EXPERTS_PER_GROUP = 8
N_EXPERTS = N_GROUPS * EXPERTS_PER_GROUP
D_EXPERT = 512
EPS = 1e-6
NEG = -1e30

LANES = 128
MXU_DIM = 256
VMEM_LIMIT = 48 * 1024 * 1024

TM = 256
TQ = 256
CQ = 512
EB = 256

C_DQ, C_DK, C_DV, C_SQ, C_VZ, C_KD, C_END = 0, 512, 1024, 1536, 2048, 2560, 2816
NORM_GROUPS = (0, 1, 2, 3, 6, 7, 10)


def _cparams(sem):
    return pltpu.CompilerParams(dimension_semantics=sem, vmem_limit_bytes=VMEM_LIMIT)


def _t5_bucket_np(dist):
    n = np.maximum(dist, 0)
    max_exact = N_BUCKETS // 2
    nf = np.maximum(n, 1).astype(np.float32)
    large = max_exact + (np.log(nf / np.float32(max_exact)) / np.float32(math.log(MAX_DISTANCE / max_exact))
                         * np.float32(N_BUCKETS - max_exact)).astype(np.int32)
    large = np.minimum(large, N_BUCKETS - 1)
    return np.where(n < max_exact, n, large)


def _bias_tables(rel_bias, tq):
    nd = 2 * BLOCK
    buckets = _t5_bucket_np(np.arange(nd))
    assert (buckets[MAX_DISTANCE:] == N_BUCKETS - 1).all()
    tbl = rel_bias.astype(F32)[buckets]
    r = np.arange(BLOCK)[:, None]
    c = np.arange(BLOCK)[None, :]
    d_own = r - c
    d_prev = BLOCK + r - c
    far = tbl[nd - 1]

    def take(dist):
        t = tbl[np.clip(dist, 0, nd - 1)]
        return jnp.moveaxis(t, -1, 0)

    hd = slice(0, N_DIFF_HEADS)
    far_d = far[hd][:, None, None]
    d0 = jnp.where(d_own[None] >= 0, take(d_own)[hd] - far_d, NEG)
    d1 = take(d_prev)[hd] - far_d
    dblk = jnp.stack([d0, d1], axis=1)
    rq = np.arange(tq)[:, None]
    cm = np.arange(LANES)[None, :]
    d_meta = N_META + rq - cm
    bm0 = jnp.where((cm < N_META)[None], take(d_meta)[hd] - far_d, NEG)

    hs = slice(N_DIFF_HEADS, N_DIFF_HEADS + N_SWA_HEADS)
    far_s = far[hs][:, None, None]
    d_meta_s = N_META + r - cm
    meta_first = jnp.where((cm < N_META)[None], take(d_meta_s)[hs], NEG)
    meta_rest = jnp.where((cm < N_META)[None], jnp.broadcast_to(far_s, (N_SWA_HEADS, BLOCK, LANES)), NEG)
    prev_rest = jnp.where((c > r)[None], take(d_prev)[hs], NEG)
    prev_first = jnp.full((N_SWA_HEADS, BLOCK, BLOCK), NEG, F32)
    own = jnp.where((d_own >= 0)[None], take(d_own)[hs], NEG)
    bt = jnp.stack([jnp.concatenate([meta_first, prev_first, own], axis=-1),
                    jnp.concatenate([meta_rest, prev_rest, own], axis=-1)], axis=0)
    return dblk.astype(F32), bm0.astype(F32), bt.astype(F32)


def _proj_kernel(x_ref, g1_ref, w_ref, bd_ref, gain_ref, o_ref):
    x = x_ref[...]
    a = x * lax.rsqrt(jnp.mean(x * x, axis=-1, keepdims=True) + EPS) * g1_ref[...]
    p = jnp.dot(a.astype(BF16), w_ref[...], preferred_element_type=F32)
    bd = bd_ref[...]
    for j in range(C_END // MXU_DIM):
        sl = slice(j * MXU_DIM, (j + 1) * MXU_DIM)
        pj = p[:, sl]
        if j in NORM_GROUPS:
            ms = jnp.dot((pj * pj).astype(BF16), bd, preferred_element_type=F32)
            pj = pj * lax.rsqrt(ms + EPS) * gain_ref[:, sl]
        o_ref[:, sl] = pj.astype(BF16)


def _proj(x2, g1, w_aug, bd, gain, tm):
    n = x2.shape[0]
    return pl.pallas_call(
        _proj_kernel,
        out_shape=jax.ShapeDtypeStruct((n, C_END), BF16),
        grid=(n // tm,),
        in_specs=[
            pl.BlockSpec((tm, x2.shape[1]), lambda i: (i, 0)),
            pl.BlockSpec(g1.shape, lambda i: (0, 0)),
            pl.BlockSpec(w_aug.shape, lambda i: (0, 0)),
            pl.BlockSpec(bd.shape, lambda i: (0, 0)),
            pl.BlockSpec(gain.shape, lambda i: (0, 0)),
        ],
        out_specs=pl.BlockSpec((tm, C_END), lambda i: (i, 0)),
        compiler_params=_cparams(("parallel",)),
        name="proj",
    )(x2, g1, w_aug, bd, gain)


def _diff_kernel(q_ref, k_ref, v_ref, km_ref, vm_ref, d_ref, bm0_ref, lamv_ref, gain_ref, o_ref,
                 bias_ref, m_ref, l_ref, acc_ref, *, lambda_init):
    qi = pl.program_id(2)
    tq = q_ref.shape[0]
    nb = tq // BLOCK

    @pl.when(qi == 0)
    def _build_bias():
        d0 = d_ref[0, 0]
        d1 = d_ref[0, 1]
        for a in range(nb):
            for b in range(nb):
                rs, cs = slice(a * BLOCK, (a + 1) * BLOCK), slice(b * BLOCK, (b + 1) * BLOCK)
                if a == b:
                    blk = d0
                elif a == b + 1:
                    blk = d1
                elif a > b:
                    blk = jnp.zeros((BLOCK, BLOCK), F32)
                else:
                    blk = jnp.full((BLOCK, BLOCK), NEG, F32)
                bias_ref[0, rs, cs] = blk
                bias_ref[1, rs, cs] = d1 if (a == 0 and b == nb - 1) else jnp.zeros((BLOCK, BLOCK), F32)

    q = q_ref[...]
    lane = lax.broadcasted_iota(jnp.int32, q.shape, 1)
    zero = jnp.zeros_like(q)
    qs = jnp.concatenate([jnp.where(lane < HEAD_DIM, q, zero),
                          jnp.where(lane >= HEAD_DIM, q, zero)], axis=0)

    m_ref[...] = jnp.full(m_ref.shape, NEG, F32)
    l_ref[...] = jnp.zeros(l_ref.shape, F32)
    acc_ref[...] = jnp.zeros(acc_ref.shape, F32)

    def step(kblk, vblk, bias):
        s = lax.dot_general(qs, kblk, (((1,), (1,)), ((), ())), preferred_element_type=F32)
        tk = s.shape[1]
        if bias is not None:
            s = (s.reshape(2, tq, tk) + bias[None]).reshape(2 * tq, tk)
        m_prev = m_ref[...]
        m_new = jnp.maximum(m_prev, jnp.max(s, axis=1, keepdims=True))
        alpha = jnp.exp(m_prev - m_new)
        p = jnp.exp(s - jnp.tile(m_new, (1, tk // LANES)))
        l_ref[...] = alpha * l_ref[...] + jnp.sum(p, axis=1, keepdims=True)
        acc_ref[...] = alpha * acc_ref[...] + jnp.dot(p.astype(BF16), vblk, preferred_element_type=F32)
        m_ref[...] = m_new

    lane_m = lax.broadcasted_iota(jnp.int32, (tq, LANES), 1)
    meta_rest = jnp.where(lane_m < N_META, 0.0, NEG).astype(F32)
    step(km_ref[...], vm_ref[...], jnp.where(qi == 0, bm0_ref[0], meta_rest))

    def far_body(j, carry):
        r0 = pl.multiple_of(j * tq, tq)
        step(k_ref[pl.ds(r0, tq), :], v_ref[pl.ds(r0, tq), :], None)
        return carry
    lax.fori_loop(0, jnp.maximum(qi - 1, 0), far_body, 0)

    @pl.when(qi >= 1)
    def _left():
        r0 = pl.multiple_of((qi - 1) * tq, tq)
        step(k_ref[pl.ds(r0, tq), :], v_ref[pl.ds(r0, tq), :], bias_ref[1])

    r0 = pl.multiple_of(qi * tq, tq)
    step(k_ref[pl.ds(r0, tq), :], v_ref[pl.ds(r0, tq), :], bias_ref[0])

    lv = lamv_ref[...]
    lam = (jnp.exp(jnp.sum(lv[0:1] * lv[1:2], axis=-1, keepdims=True))
           - jnp.exp(jnp.sum(lv[2:3] * lv[3:4], axis=-1, keepdims=True)) + lambda_init)
    o = acc_ref[...] / l_ref[...]
    d = o[:tq] - lam * o[tq:]
    y = d * lax.rsqrt(jnp.mean(d * d, axis=-1, keepdims=True) + EPS) * gain_ref[...]
    o_ref[...] = (y * (1.0 - lambda_init)).astype(BF16)


def _diff_attention(qkv, km, vm, dblk, bm0, lamv, gain, batch, seq, lambda_init):
    nq = seq // TQ
    kern = functools.partial(_diff_kernel, lambda_init=lambda_init)
    return pl.pallas_call(
        kern,
        out_shape=jax.ShapeDtypeStruct((batch * seq, N_DIFF_HEADS * LANES), BF16),
        grid=(batch, N_DIFF_HEADS, nq),
        in_specs=[
            pl.BlockSpec((TQ, LANES), lambda b, h, i: (b * nq + i, C_DQ // LANES + h)),
            pl.BlockSpec((seq, LANES), lambda b, h, i: (b, C_DK // LANES + h)),
            pl.BlockSpec((seq, LANES), lambda b, h, i: (b, C_DV // LANES + h)),
            pl.BlockSpec((LANES, LANES), lambda b, h, i: (0, h)),
            pl.BlockSpec((LANES, LANES), lambda b, h, i: (0, h)),
            pl.BlockSpec((1, 2, BLOCK, BLOCK), lambda b, h, i: (h, 0, 0, 0)),
            pl.BlockSpec((1, TQ, LANES), lambda b, h, i: (h, 0, 0)),
            pl.BlockSpec(lamv.shape, lambda b, h, i: (0, 0)),
            pl.BlockSpec(gain.shape, lambda b, h, i: (0, 0)),
        ],
        out_specs=pl.BlockSpec((TQ, LANES), lambda b, h, i: (b * nq + i, h)),
        scratch_shapes=[
            pltpu.VMEM((2, TQ, TQ), F32),
            pltpu.VMEM((2 * TQ, LANES), F32),
            pltpu.VMEM((2 * TQ, LANES), F32),
            pltpu.VMEM((2 * TQ, LANES), F32),
        ],
        compiler_params=_cparams(("parallel", "parallel", "arbitrary")),
        name="diff_attention",
    )(qkv, qkv, qkv, km, vm, dblk, bm0, lamv, gain)


def _swa_kernel(sink_ref, q_ref, kd_ref, vz_ref, kdm_ref, vzm_ref, bt_ref, o_ref):
    ci = pl.program_id(1)
    nblk = q_ref.shape[0] // BLOCK
    lane = lax.broadcasted_iota(jnp.int32, (BLOCK, LANES), 1)
    row2 = lax.broadcasted_iota(jnp.int32, (2 * BLOCK, 1), 0)

    def block_body(n, carry):
        gblk = ci * nblk + n
        first = jnp.where(gblk == 0, 0, 1)
        r_own = pl.multiple_of(gblk * BLOCK, BLOCK)
        r_prev = pl.multiple_of(jnp.maximum(gblk - 1, 0) * BLOCK, BLOCK)
        r_q = pl.multiple_of(n * BLOCK, BLOCK)
        for g in range(N_SWA_KV):
            kcat = jnp.concatenate([kdm_ref[:, g * LANES:(g + 1) * LANES],
                                    kd_ref[pl.ds(r_prev, BLOCK), g * LANES:(g + 1) * LANES],
                                    kd_ref[pl.ds(r_own, BLOCK), g * LANES:(g + 1) * LANES]], axis=0)
            for u in range(2):
                cs = slice((2 * g + u) * LANES, (2 * g + u + 1) * LANES)
                h0 = 4 * g + 2 * u
                qp = q_ref[pl.ds(r_q, BLOCK), cs]
                zero = jnp.zeros_like(qp)
                qs = jnp.concatenate([jnp.where(lane < HEAD_DIM, qp, zero),
                                      jnp.where(lane >= HEAD_DIM, qp, zero)], axis=0)
                s = lax.dot_general(qs, kcat, (((1,), (1,)), ((), ())), preferred_element_type=F32)
                s = s + jnp.concatenate([bt_ref[first, h0], bt_ref[first, h0 + 1]], axis=0)
                sink = jnp.where(row2 < BLOCK, sink_ref[h0:h0 + 1, 0:1], sink_ref[h0 + 1:h0 + 2, 0:1])
                m = jnp.maximum(jnp.max(s, axis=1, keepdims=True), sink)
                p = jnp.exp(s - m)
                den = jnp.sum(p, axis=1, keepdims=True) + jnp.exp(sink - m)
                pb = p.astype(BF16)
                inv = 1.0 / den
                o = None
                for par in range(2):
                    vs = slice((2 * g + par) * LANES, (2 * g + par + 1) * LANES)
                    vcat = jnp.concatenate([vzm_ref[:, vs],
                                            vz_ref[pl.ds(r_prev, BLOCK), vs],
                                            vz_ref[pl.ds(r_own, BLOCK), vs]], axis=0)
                    rs = slice(par * BLOCK, (par + 1) * BLOCK)
                    t = jnp.dot(pb[rs], vcat, preferred_element_type=F32) * inv[rs]
                    o = t if o is None else o + t
                o_ref[pl.ds(r_q, BLOCK), cs] = o.astype(BF16)
        return carry

    lax.fori_loop(0, nblk, block_body, 0)


def _swa_attention(sinks, qkv, kdm, vzm, bt, batch, seq):
    nc = seq // CQ
    sinkv = jnp.broadcast_to(sinks.reshape(N_SWA_HEADS, 1), (N_SWA_HEADS, LANES))
    return pl.pallas_call(
        _swa_kernel,
        out_shape=jax.ShapeDtypeStruct((batch * seq, N_SWA_HEADS * HEAD_DIM), BF16),
        grid=(batch, nc),
        in_specs=[
            pl.BlockSpec(sinkv.shape, lambda b, c: (0, 0)),
            pl.BlockSpec((CQ, 512), lambda b, c: (b * nc + c, C_SQ // 512)),
            pl.BlockSpec((seq, 256), lambda b, c: (b, C_KD // 256)),
            pl.BlockSpec((seq, 512), lambda b, c: (b, C_VZ // 512)),
            pl.BlockSpec(kdm.shape, lambda b, c: (0, 0)),
            pl.BlockSpec(vzm.shape, lambda b, c: (0, 0)),
            pl.BlockSpec(bt.shape, lambda b, c: (0, 0, 0, 0)),
        ],
        out_specs=pl.BlockSpec((CQ, 512), lambda b, c: (b * nc + c, 0)),
        compiler_params=_cparams(("parallel", "arbitrary")),
        name="swa_attention",
    )(sinkv, qkv, qkv, qkv, kdm, vzm, bt)


def _outproj_kernel(x_ref, md_ref, ms_ref, wo_ref, g2_ref, wr_ref, br_ref,
                    h_ref, hp_ref, rt_ref, cnt_ref, c_ref):
    i = pl.program_id(0)
    tm = x_ref.shape[0]
    half = md_ref.shape[1]

    @pl.when(i == 0)
    def _init():
        c_ref[...] = jnp.zeros(c_ref.shape, F32)

    h = (x_ref[...]
         + jnp.dot(md_ref[...], wo_ref[:half, :], preferred_element_type=F32)
         + jnp.dot(ms_ref[...], wo_ref[half:, :], preferred_element_type=F32))
    h_ref[...] = h
    hn = h * lax.rsqrt(jnp.mean(h * h, axis=-1, keepdims=True) + EPS) * g2_ref[...]
    hb = hn.astype(BF16)

    bits = pltpu.bitcast(hb.astype(F32), jnp.uint32)
    dh = bits.shape[1] // 2
    hp_ref[...] = (bits[:, dh:] & jnp.uint32(0xFFFF0000)) | (bits[:, :dh] >> 16)

    lg = jnp.dot(hb, wr_ref[...], preferred_element_type=F32) + br_ref[...]
    lane_i = lax.broadcasted_iota(jnp.int32, lg.shape, 1)
    lane = lane_i.astype(F32)
    big = float(4 * LANES)
    is_g = (lane_i >= N_EXPERTS) & (lane_i < N_EXPERTS + N_GROUPS)
    glm = jnp.where(is_g, lg, -jnp.inf)
    gmax = jnp.max(glm, axis=1, keepdims=True)
    gidx = jnp.min(jnp.where(glm == gmax, lane, big), axis=1, keepdims=True) - N_EXPERTS
    gsum = jnp.sum(jnp.where(is_g, jnp.exp(lg - gmax), 0.0), axis=1, keepdims=True)
    g_w = 1.0 / gsum
    lane_grp = (lane_i >> 3).astype(F32)
    in_grp = (lane_i < N_EXPERTS) & (lane_grp == gidx)
    el = jnp.where(in_grp, lg, -jnp.inf)
    t1 = jnp.max(el, axis=1, keepdims=True)
    j1 = jnp.min(jnp.where(el == t1, lane, big), axis=1, keepdims=True)
    el2 = jnp.where(lane == j1, -jnp.inf, el)
    t2 = jnp.max(el2, axis=1, keepdims=True)
    j2 = jnp.min(jnp.where(el2 == t2, lane, big), axis=1, keepdims=True)
    e2 = jnp.exp(t2 - t1)
    den = 1.0 + e2
    gate1 = g_w / den
    gate2 = g_w * e2 / den

    o1 = lane == j1
    o2 = lane == j2
    onehot = jnp.where(o1 | o2, 1.0, 0.0).astype(BF16)
    rr = lax.broadcasted_iota(jnp.int32, (tm, tm), 0)
    cc = lax.broadcasted_iota(jnp.int32, (tm, tm), 1)
    lower = jnp.where(rr > cc, 1.0, 0.0).astype(BF16)
    tot = jnp.dot(lower, onehot, preferred_element_type=F32) + c_ref[0:1, :]
    r1 = jnp.sum(jnp.where(o1, tot, 0.0), axis=1, keepdims=True)
    r2 = jnp.sum(jnp.where(o2, tot, 0.0), axis=1, keepdims=True)
    c_new = c_ref[...] + jnp.sum(onehot.astype(F32), axis=0, keepdims=True)
    c_ref[...] = c_new
    code1 = j1 * 65536.0 + r1
    code2 = j2 * 65536.0 + r2
    rt_ref[...] = jnp.where(lane_i == 0, gate1,
                            jnp.where(lane_i == 1, gate2,
                                      jnp.where(lane_i == 2, code1,
                                                jnp.where(lane_i == 3, code2, 0.0))))

    @pl.when(i == pl.num_programs(0) - 1)
    def _fin():
        cnt_ref[...] = c_new


def _outproj(x2, mixd, mixs, wo, g2, wr, br):
    n, d = x2.shape
    return pl.pallas_call(
        _outproj_kernel,
        out_shape=(jax.ShapeDtypeStruct((n, d), F32),
                   jax.ShapeDtypeStruct((n, d // 2), jnp.uint32),
                   jax.ShapeDtypeStruct((n, LANES), F32),
                   jax.ShapeDtypeStruct((8, LANES), F32)),
        grid=(n // TM,),
        in_specs=[
            pl.BlockSpec((TM, d), lambda i: (i, 0)),
            pl.BlockSpec((TM, mixd.shape[1]), lambda i: (i, 0)),
            pl.BlockSpec((TM, mixs.shape[1]), lambda i: (i, 0)),
            pl.BlockSpec(wo.shape, lambda i: (0, 0)),
            pl.BlockSpec(g2.shape, lambda i: (0, 0)),
            pl.BlockSpec(wr.shape, lambda i: (0, 0)),
            pl.BlockSpec(br.shape, lambda i: (0, 0)),
        ],
        out_specs=(pl.BlockSpec((TM, d), lambda i: (i, 0)),
                   pl.BlockSpec((TM, d // 2), lambda i: (i, 0)),
                   pl.BlockSpec((TM, LANES), lambda i: (i, 0)),
                   pl.BlockSpec((8, LANES), lambda i: (0, 0))),
        scratch_shapes=[pltpu.VMEM((8, LANES), F32)],
        compiler_params=_cparams(("arbitrary",)),
        name="outproj_router",
    )(x2, mixd, mixs, wo, g2, wr, br)


def _slot_row(codes_ref, ps_ref, idx):
    code = codes_ref[0, 0, idx]
    return ps_ref[code >> 16] + (code & 0xFFFF)


def _dispatch_kernel(ps_ref, codes_ref, hp_ref, xs_in_ref, xs_ref, sem):
    del xs_in_ref
    i = pl.program_id(0)
    nt = pl.num_programs(0)
    tm = codes_ref.shape[2] // 2

    def row_copy(src_row, dst_row, slot):
        return pltpu.make_async_copy(hp_ref.at[pl.ds(src_row, 1)], xs_ref.at[pl.ds(dst_row, 1)], sem.at[slot])

    def issue(t, carry):
        for k in range(2):
            row_copy(i * tm + t, _slot_row(codes_ref, ps_ref, 2 * t + k), i % 2).start()
        return carry
    lax.fori_loop(0, tm, issue, 0)

    def drain(slot):
        def body(t, carry):
            for _ in range(2):
                row_copy(0, 0, slot).wait()
            return carry
        lax.fori_loop(0, tm, body, 0)

    @pl.when(i > 0)
    def _():
        drain((i + 1) % 2)

    @pl.when(i == nt - 1)
    def _():
        drain(i % 2)


def _dispatch(pstart, codes3, hp, xs0):
    n = hp.shape[0]
    return pl.pallas_call(
        _dispatch_kernel,
        out_shape=jax.ShapeDtypeStruct(xs0.shape, xs0.dtype),
        grid_spec=pltpu.PrefetchScalarGridSpec(
            num_scalar_prefetch=1,
            grid=(n // TM,),
            in_specs=[
                pl.BlockSpec((1, 1, 2 * TM), lambda i, ps: (i, 0, 0), memory_space=pltpu.SMEM),
                pl.BlockSpec(memory_space=pl.ANY),
                pl.BlockSpec(memory_space=pl.ANY),
            ],
            out_specs=pl.BlockSpec(memory_space=pl.ANY),
            scratch_shapes=[pltpu.SemaphoreType.DMA((2,))],
        ),
        input_output_aliases={3: 0},
        compiler_params=_cparams(("arbitrary",)),
        name="dispatch",
    )(pstart, codes3, hp, xs0)


def _experts_kernel(be_ref, na_ref, xs_ref, wg_ref, wu_ref, wd_ref, ys_ref, wgb, wub, wdb):
    b = pl.program_id(0)

    @pl.when(b < na_ref[0])
    def _():
        e = be_ref[b]
        changed = jnp.logical_or(b == 0, be_ref[jnp.maximum(b - 1, 0)] != e)

        @pl.when(changed)
        def _cast():
            wgb[...] = wg_ref[0].astype(BF16)
            wub[...] = wu_ref[0].astype(BF16)
            wdb[...] = wd_ref[0].astype(BF16)

        w = xs_ref[...]
        x_lo = pltpu.bitcast(w << 16, F32).astype(BF16)
        x_hi = pltpu.bitcast(w & jnp.uint32(0xFFFF0000), F32).astype(BF16)
        dh = w.shape[1]
        g = (jnp.dot(x_lo, wgb[:dh, :], preferred_element_type=F32)
             + jnp.dot(x_hi, wgb[dh:, :], preferred_element_type=F32))
        u = (jnp.dot(x_lo, wub[:dh, :], preferred_element_type=F32)
             + jnp.dot(x_hi, wub[dh:, :], preferred_element_type=F32))
        hdn = g * (1.0 / (1.0 + jnp.exp(-g))) * u
        ys_ref[...] = jnp.dot(hdn.astype(BF16), wdb[...], preferred_element_type=F32)

    @pl.when(b >= na_ref[0])
    def _():
        ys_ref[...] = jnp.zeros(ys_ref.shape, F32)


def _experts(blk_e, n_act, xs, w_gate, w_up, w_down):
    p, dh = xs.shape
    d = 2 * dh
    de = w_gate.shape[2]

    def row_map(b, be, na):
        return (jnp.minimum(b, na[0] - 1), 0)

    def w_map(b, be, na):
        return (be[jnp.minimum(b, na[0] - 1)], 0, 0)

    return pl.pallas_call(
        _experts_kernel,
        out_shape=jax.ShapeDtypeStruct((p, d), F32),
        grid_spec=pltpu.PrefetchScalarGridSpec(
            num_scalar_prefetch=2,
            grid=(p // EB,),
            in_specs=[
                pl.BlockSpec((EB, dh), row_map),
                pl.BlockSpec((1, d, de), w_map),
                pl.BlockSpec((1, d, de), w_map),
                pl.BlockSpec((1, de, d), w_map),
            ],
            out_specs=pl.BlockSpec((EB, d), lambda b, be, na: (b, 0)),
            scratch_shapes=[pltpu.VMEM((d, de), BF16), pltpu.VMEM((d, de), BF16), pltpu.VMEM((de, d), BF16)],
        ),
        compiler_params=_cparams(("arbitrary",)),
        name="experts",
    )(blk_e, n_act, xs, w_gate, w_up, w_down)


def _combine_kernel(ps_ref, cur_ref, nxt_ref, ys_ref, h_ref, rt_ref, o_ref, ybuf, sem):
    i = pl.program_id(0)
    nt = pl.num_programs(0)
    tm = h_ref.shape[0]

    def row_copy(src_row, dst_row, slot):
        return pltpu.make_async_copy(ys_ref.at[pl.ds(src_row, 1)], ybuf.at[slot, pl.ds(dst_row, 1)], sem.at[slot])

    def issue(codes_ref, slot):
        def body(t, carry):
            for k in range(2):
                row_copy(_slot_row(codes_ref, ps_ref, 2 * t + k), k * tm + t, slot).start()
            return carry
        lax.fori_loop(0, tm, body, 0)

    @pl.when(i == 0)
    def _():
        issue(cur_ref, 0)

    @pl.when(i + 1 < nt)
    def _():
        issue(nxt_ref, (i + 1) % 2)

    slot = i % 2

    def drain(t, carry):
        for _ in range(2):
            row_copy(0, 0, slot).wait()
        return carry
    lax.fori_loop(0, tm, drain, 0)

    rt = rt_ref[...]
    y = ybuf[slot]
    o_ref[...] = h_ref[...] + rt[:, 0:1] * y[:tm] + rt[:, 1:2] * y[tm:]


def _combine(pstart, codes3, ys, h1, rt):
    n, d = h1.shape
    nt = n // TM
    return pl.pallas_call(
        _combine_kernel,
        out_shape=jax.ShapeDtypeStruct((n, d), F32),
        grid_spec=pltpu.PrefetchScalarGridSpec(
            num_scalar_prefetch=1,
            grid=(nt,),
            in_specs=[
                pl.BlockSpec((1, 1, 2 * TM), lambda i, ps: (i, 0, 0), memory_space=pltpu.SMEM),
                pl.BlockSpec((1, 1, 2 * TM), lambda i, ps: (jnp.minimum(i + 1, nt - 1), 0, 0),
                             memory_space=pltpu.SMEM),
                pl.BlockSpec(memory_space=pl.ANY),
                pl.BlockSpec((TM, d), lambda i, ps: (i, 0)),
                pl.BlockSpec((TM, LANES), lambda i, ps: (i, 0)),
            ],
            out_specs=pl.BlockSpec((TM, d), lambda i, ps: (i, 0)),
            scratch_shapes=[pltpu.VMEM((2, 2 * TM, d), F32), pltpu.SemaphoreType.DMA((2,))],
        ),
        compiler_params=_cparams(("arbitrary",)),
        name="combine",
    )(pstart, codes3, codes3, ys, h1, rt)


def _augmented_w_in(w_in):
    sk0 = C_SQ + N_SWA_HEADS * HEAD_DIM
    sv0 = sk0 + N_SWA_KV * HEAD_DIM
    z = jnp.zeros((w_in.shape[0], HEAD_DIM), w_in.dtype)
    vz, kd = [], []
    for g in range(N_SWA_KV):
        k = w_in[:, sk0 + g * HEAD_DIM: sk0 + (g + 1) * HEAD_DIM]
        v = w_in[:, sv0 + g * HEAD_DIM: sv0 + (g + 1) * HEAD_DIM]
        kd += [k, k]
        vz += [v, z, z, v]
    return jnp.concatenate([w_in[:, :C_VZ]] + vz + kd, axis=1)


def kernel(x, meta_tokens, rel_bias, norm1_gain, w_in, diff_q_gain, diff_k_gain, lam_q1, lam_k1, lam_q2, lam_k2, diff_subln_gain, swa_q_gain, swa_k_gain, swa_sinks, w_out, norm2_gain, w_group, b_group, w_router, b_router, w_gate, w_up, w_down):
    batch, seq, d = x.shape
    depth = w_in.shape[0]
    n = batch * seq
    assert seq % CQ == 0 and seq % TQ == 0 and n % TM == 0 and d == 1024
    assert meta_tokens.shape[0] == N_META
    assert depth == 1, "the meta-token rows of the residual stream are not carried across layers"

    h = x.reshape(n, d)
    dblk, bm0, bt = _bias_tables(rel_bias, TQ)
    scale = HEAD_DIM ** -0.5
    bd = jnp.asarray(np.kron(np.eye(MXU_DIM // HEAD_DIM), np.full((HEAD_DIM, HEAD_DIM), 1.0 / HEAD_DIM)), BF16)
    ones = jnp.ones((HEAD_DIM,), F32)
    lower_pad = N_EXPERTS + N_GROUPS

    for layer in range(depth):
        lambda_init = 0.8 - 0.6 * math.exp(-0.3 * layer)
        w_aug = _augmented_w_in(w_in[layer]).astype(BF16)
        gain = jnp.concatenate([
            jnp.tile(diff_q_gain[layer] * scale, 2 * N_DIFF_HEADS),
            jnp.tile(diff_k_gain[layer], 2 * N_DIFF_HEADS),
            jnp.tile(ones, 2 * N_DIFF_HEADS),
            jnp.tile(swa_q_gain[layer] * scale, N_SWA_HEADS),
            jnp.tile(ones, 4 * N_SWA_KV),
            jnp.tile(swa_k_gain[layer], 2 * N_SWA_KV)]).reshape(1, C_END).astype(F32)
        g1 = norm1_gain[layer].reshape(1, d).astype(F32)

        qkv = _proj(h, g1, w_aug, bd, gain, TM)
        qkv_meta = _proj(meta_tokens.astype(F32), g1, w_aug, bd, gain, N_META)
        meta_pad = jnp.pad(qkv_meta, ((0, LANES - N_META), (0, 0)))

        lamv = jnp.pad(jnp.stack([lam_q1[layer], lam_k1[layer], lam_q2[layer], lam_k2[layer]]).astype(F32),
                       ((0, 4), (0, LANES - HEAD_DIM)))
        mixd = _diff_attention(qkv, meta_pad[:, C_DK:C_DV], meta_pad[:, C_DV:C_SQ], dblk, bm0, lamv,
                               diff_subln_gain[layer].reshape(1, LANES).astype(F32), batch, seq, lambda_init)
        mixs = _swa_attention(swa_sinks[layer].astype(F32), qkv, meta_pad[:, C_KD:C_END],
                              meta_pad[:, C_VZ:C_KD], bt, batch, seq)

        wr = jnp.pad(jnp.concatenate([w_router[layer], w_group[layer]], axis=1),
                     ((0, 0), (0, LANES - lower_pad))).astype(BF16)
        br = jnp.pad(jnp.concatenate([b_router[layer], b_group[layer]]), (0, LANES - lower_pad)).reshape(1, LANES)
        h1, hp, rt, cnt = _outproj(h, mixd, mixs, w_out[layer].astype(BF16),
                                   norm2_gain[layer].reshape(1, d).astype(F32), wr, br.astype(F32))

        counts = cnt[0, :N_EXPERTS].astype(jnp.int32)
        nblk_e = (counts + EB - 1) // EB
        blk_end = jnp.cumsum(nblk_e)
        pstart = ((blk_end - nblk_e) * EB).astype(jnp.int32)
        n_blocks = -(-(2 * n + N_EXPERTS * (EB - 1)) // EB)
        blk_e = jnp.minimum(jnp.searchsorted(blk_end, jnp.arange(n_blocks), side='right'),
                            N_EXPERTS - 1).astype(jnp.int32)
        n_act = blk_end[-1:].astype(jnp.int32)
        codes3 = rt[:, 2:4].astype(jnp.int32).reshape(n // TM, 1, 2 * TM)

        xs = _dispatch(pstart, codes3, hp, jnp.zeros((n_blocks * EB, d // 2), jnp.uint32))
        ys = _experts(blk_e, n_act, xs, w_gate[layer], w_up[layer], w_down[layer])
        h = _combine(pstart, codes3, ys, h1, rt)
    return h.reshape(batch, seq, d)
```

```python
import functools
import math

import numpy as np
import jax
import jax.numpy as jnp
from jax import lax
from jax.experimental import pallas as pl
from jax.experimental.pallas import tpu as pltpu

F32 = jnp.float32
BF16 = jnp.bfloat16

HEAD_DIM = 64
N_DIFF_HEADS = 4
N_SWA_HEADS = 8
N_SWA_KV = 2
BLOCK = 128
N_META = 16
N_BUCKETS = 32
MAX_DISTANCE = 128
N_GROUPS = 4
EXPERTS_PER_GROUP = 8
N_EXPERTS = N_GROUPS * EXPERTS_PER_GROUP
D_EXPERT = 512
EPS = 1e-6
NEG = -1e30

LANES = 128
MXU_DIM = 256
VMEM_LIMIT = 48 * 1024 * 1024

TM = 256
TQ = 256
CQ = 512
EB = 256
ROW_ALIGN = 8
CHUNK = 16
SROWS = -(-(2 * TM + N_EXPERTS * (CHUNK - 1)) // MXU_DIM) * MXU_DIM

C_DQ, C_DK, C_DV, C_SQ, C_VZ, C_KD, C_END = 0, 512, 1024, 1536, 2048, 2560, 2816
NORM_GROUPS = (0, 1, 2, 3, 6, 7, 10)


def _cparams(sem):
    return pltpu.CompilerParams(dimension_semantics=sem, vmem_limit_bytes=VMEM_LIMIT)


def _t5_bucket_np(dist):
    n = np.maximum(dist, 0)
    max_exact = N_BUCKETS // 2
    nf = np.maximum(n, 1).astype(np.float32)
    large = max_exact + (np.log(nf / np.float32(max_exact)) / np.float32(math.log(MAX_DISTANCE / max_exact))
                         * np.float32(N_BUCKETS - max_exact)).astype(np.int32)
    large = np.minimum(large, N_BUCKETS - 1)
    return np.where(n < max_exact, n, large)


def _bias_tables(rel_bias, tq):
    nd = 2 * BLOCK
    buckets = _t5_bucket_np(np.arange(nd))
    assert (buckets[MAX_DISTANCE:] == N_BUCKETS - 1).all()
    rb = rel_bias.astype(F32)
    r = np.arange(BLOCK)[:, None]
    c = np.arange(BLOCK)[None, :]
    d_own = r - c
    d_prev = BLOCK + r - c
    far = rb[N_BUCKETS - 1]

    def take(dist):
        idx = jnp.asarray(buckets[np.clip(dist, 0, nd - 1)], jnp.int32)[None]
        out = jnp.zeros((rb.shape[1],) + dist.shape, F32)
        for b in range(N_BUCKETS):
            out = jnp.where(idx == b, rb[b].reshape((-1,) + (1,) * dist.ndim), out)
        return out

    hd = slice(0, N_DIFF_HEADS)
    far_d = far[hd][:, None, None]
    d0 = jnp.where(d_own[None] >= 0, take(d_own)[hd] - far_d, NEG)
    d1 = take(d_prev)[hd] - far_d
    dblk = jnp.stack([d0, d1], axis=1)
    rq = np.arange(tq)[:, None]
    cm = np.arange(LANES)[None, :]
    d_meta = N_META + rq - cm
    bm0 = jnp.where((cm < N_META)[None], take(d_meta)[hd] - far_d, NEG)

    hs = slice(N_DIFF_HEADS, N_DIFF_HEADS + N_SWA_HEADS)
    far_s = far[hs][:, None, None]
    d_meta_s = N_META + r - cm
    meta_first = jnp.where((cm < N_META)[None], take(d_meta_s)[hs], NEG)
    meta_rest = jnp.where((cm < N_META)[None], jnp.broadcast_to(far_s, (N_SWA_HEADS, BLOCK, LANES)), NEG)
    prev_rest = jnp.where((c > r)[None], take(d_prev)[hs], NEG)
    prev_first = jnp.full((N_SWA_HEADS, BLOCK, BLOCK), NEG, F32)
    own = jnp.where((d_own >= 0)[None], take(d_own)[hs], NEG)
    bt = jnp.stack([jnp.concatenate([meta_first, prev_first, own], axis=-1),
                    jnp.concatenate([meta_rest, prev_rest, own], axis=-1)], axis=0)
    return dblk.astype(F32), bm0.astype(F32), bt.astype(F32)


def _proj_kernel(x_ref, g1_ref, w_ref, bd_ref, gain_ref, o_ref):
    x = x_ref[...]
    a = x * lax.rsqrt(jnp.mean(x * x, axis=-1, keepdims=True) + EPS) * g1_ref[...]
    p = jnp.dot(a.astype(BF16), w_ref[...], preferred_element_type=F32)
    bd = bd_ref[...]
    for j in range(C_END // MXU_DIM):
        sl = slice(j * MXU_DIM, (j + 1) * MXU_DIM)
        pj = p[:, sl]
        if j in NORM_GROUPS:
            ms = jnp.dot((pj * pj).astype(BF16), bd, preferred_element_type=F32)
            pj = pj * lax.rsqrt(ms + EPS) * gain_ref[:, sl]
        o_ref[:, sl] = pj.astype(BF16)


def _proj(x2, g1, w_aug, bd, gain, tm):
    n = x2.shape[0]
    return pl.pallas_call(
        _proj_kernel,
        out_shape=jax.ShapeDtypeStruct((n, C_END), BF16),
        grid=(n // tm,),
        in_specs=[
            pl.BlockSpec((tm, x2.shape[1]), lambda i: (i, 0)),
            pl.BlockSpec(g1.shape, lambda i: (0, 0)),
            pl.BlockSpec(w_aug.shape, lambda i: (0, 0)),
            pl.BlockSpec(bd.shape, lambda i: (0, 0)),
            pl.BlockSpec(gain.shape, lambda i: (0, 0)),
        ],
        out_specs=pl.BlockSpec((tm, C_END), lambda i: (i, 0)),
        compiler_params=_cparams(("parallel",)),
        name="proj",
    )(x2, g1, w_aug, bd, gain)


def _diff_kernel(q_ref, k_ref, v_ref, km_ref, vm_ref, d_ref, bm0_ref, lamv_ref, gain_ref, o_ref,
                 bias_ref, m_ref, l_ref, acc_ref, *, lambda_init):
    qi = pl.program_id(2)
    tq = q_ref.shape[0]
    nb = tq // BLOCK

    @pl.when(qi == 0)
    def _build_bias():
        d0 = d_ref[0, 0]
        d1 = d_ref[0, 1]
        for a in range(nb):
            for b in range(nb):
                rs, cs = slice(a * BLOCK, (a + 1) * BLOCK), slice(b * BLOCK, (b + 1) * BLOCK)
                if a == b:
                    blk = d0
                elif a == b + 1:
                    blk = d1
                elif a > b:
                    blk = jnp.zeros((BLOCK, BLOCK), F32)
                else:
                    blk = jnp.full((BLOCK, BLOCK), NEG, F32)
                bias_ref[0, rs, cs] = blk
                bias_ref[1, rs, cs] = d1 if (a == 0 and b == nb - 1) else jnp.zeros((BLOCK, BLOCK), F32)

    q = q_ref[...]
    lane = lax.broadcasted_iota(jnp.int32, q.shape, 1)
    zero = jnp.zeros_like(q)
    qs = jnp.concatenate([jnp.where(lane < HEAD_DIM, q, zero),
                          jnp.where(lane >= HEAD_DIM, q, zero)], axis=0)

    m_ref[...] = jnp.full(m_ref.shape, NEG, F32)
    l_ref[...] = jnp.zeros(l_ref.shape, F32)
    acc_ref[...] = jnp.zeros(acc_ref.shape, F32)

    def step(kblk, vblk, bias):
        s = lax.dot_general(qs, kblk, (((1,), (1,)), ((), ())), preferred_element_type=F32)
        tk = s.shape[1]
        if bias is not None:
            s = (s.reshape(2, tq, tk) + bias[None]).reshape(2 * tq, tk)
        m_prev = m_ref[...]
        m_new = jnp.maximum(m_prev, jnp.max(s, axis=1, keepdims=True))
        alpha = jnp.exp(m_prev - m_new)
        p = jnp.exp(s - jnp.tile(m_new, (1, tk // LANES)))
        l_ref[...] = alpha * l_ref[...] + jnp.sum(p, axis=1, keepdims=True)
        acc_ref[...] = alpha * acc_ref[...] + jnp.dot(p.astype(BF16), vblk, preferred_element_type=F32)
        m_ref[...] = m_new

    lane_m = lax.broadcasted_iota(jnp.int32, (tq, LANES), 1)
    meta_rest = jnp.where(lane_m < N_META, 0.0, NEG).astype(F32)
    step(km_ref[...], vm_ref[...], jnp.where(qi == 0, bm0_ref[0], meta_rest))

    def far_body(j, carry):
        r0 = pl.multiple_of(j * tq, tq)
        step(k_ref[pl.ds(r0, tq), :], v_ref[pl.ds(r0, tq), :], None)
        return carry
    lax.fori_loop(0, jnp.maximum(qi - 1, 0), far_body, 0)

    @pl.when(qi >= 1)
    def _left():
        r0 = pl.multiple_of((qi - 1) * tq, tq)
        step(k_ref[pl.ds(r0, tq), :], v_ref[pl.ds(r0, tq), :], bias_ref[1])

    r0 = pl.multiple_of(qi * tq, tq)
    step(k_ref[pl.ds(r0, tq), :], v_ref[pl.ds(r0, tq), :], bias_ref[0])

    lv = lamv_ref[...]
    lam = (jnp.exp(jnp.sum(lv[0:1] * lv[1:2], axis=-1, keepdims=True))
           - jnp.exp(jnp.sum(lv[2:3] * lv[3:4], axis=-1, keepdims=True)) + lambda_init)
    o = acc_ref[...] / l_ref[...]
    d = o[:tq] - lam * o[tq:]
    y = d * lax.rsqrt(jnp.mean(d * d, axis=-1, keepdims=True) + EPS) * gain_ref[...]
    o_ref[...] = (y * (1.0 - lambda_init)).astype(BF16)


def _diff_attention(qkv, km, vm, dblk, bm0, lamv, gain, batch, seq, lambda_init):
    nq = seq // TQ
    kern = functools.partial(_diff_kernel, lambda_init=lambda_init)
    return pl.pallas_call(
        kern,
        out_shape=jax.ShapeDtypeStruct((batch * seq, N_DIFF_HEADS * LANES), BF16),
        grid=(batch, N_DIFF_HEADS, nq),
        in_specs=[
            pl.BlockSpec((TQ, LANES), lambda b, h, i: (b * nq + i, C_DQ // LANES + h)),
            pl.BlockSpec((seq, LANES), lambda b, h, i: (b, C_DK // LANES + h)),
            pl.BlockSpec((seq, LANES), lambda b, h, i: (b, C_DV // LANES + h)),
            pl.BlockSpec((LANES, LANES), lambda b, h, i: (0, h)),
            pl.BlockSpec((LANES, LANES), lambda b, h, i: (0, h)),
            pl.BlockSpec((1, 2, BLOCK, BLOCK), lambda b, h, i: (h, 0, 0, 0)),
            pl.BlockSpec((1, TQ, LANES), lambda b, h, i: (h, 0, 0)),
            pl.BlockSpec(lamv.shape, lambda b, h, i: (0, 0)),
            pl.BlockSpec(gain.shape, lambda b, h, i: (0, 0)),
        ],
        out_specs=pl.BlockSpec((TQ, LANES), lambda b, h, i: (b * nq + i, h)),
        scratch_shapes=[
            pltpu.VMEM((2, TQ, TQ), F32),
            pltpu.VMEM((2 * TQ, LANES), F32),
            pltpu.VMEM((2 * TQ, LANES), F32),
            pltpu.VMEM((2 * TQ, LANES), F32),
        ],
        compiler_params=_cparams(("parallel", "parallel", "arbitrary")),
        name="diff_attention",
    )(qkv, qkv, qkv, km, vm, dblk, bm0, lamv, gain)


def _swa_kernel(sink_ref, q_ref, kd_ref, vz_ref, kdm_ref, vzm_ref, bt_ref, o_ref):
    ci = pl.program_id(1)
    nblk = q_ref.shape[0] // BLOCK
    lane = lax.broadcasted_iota(jnp.int32, (BLOCK, LANES), 1)
    row2 = lax.broadcasted_iota(jnp.int32, (2 * BLOCK, 1), 0)

    def block_body(n, carry):
        gblk = ci * nblk + n
        first = jnp.where(gblk == 0, 0, 1)
        r_own = pl.multiple_of(gblk * BLOCK, BLOCK)
        r_prev = pl.multiple_of(jnp.maximum(gblk - 1, 0) * BLOCK, BLOCK)
        r_q = pl.multiple_of(n * BLOCK, BLOCK)
        for g in range(N_SWA_KV):
            kcat = jnp.concatenate([kdm_ref[:, g * LANES:(g + 1) * LANES],
                                    kd_ref[pl.ds(r_prev, BLOCK), g * LANES:(g + 1) * LANES],
                                    kd_ref[pl.ds(r_own, BLOCK), g * LANES:(g + 1) * LANES]], axis=0)
            for u in range(2):
                cs = slice((2 * g + u) * LANES, (2 * g + u + 1) * LANES)
                h0 = 4 * g + 2 * u
                qp = q_ref[pl.ds(r_q, BLOCK), cs]
                zero = jnp.zeros_like(qp)
                qs = jnp.concatenate([jnp.where(lane < HEAD_DIM, qp, zero),
                                      jnp.where(lane >= HEAD_DIM, qp, zero)], axis=0)
                s = lax.dot_general(qs, kcat, (((1,), (1,)), ((), ())), preferred_element_type=F32)
                s = s + jnp.concatenate([bt_ref[first, h0], bt_ref[first, h0 + 1]], axis=0)
                sink = jnp.where(row2 < BLOCK, sink_ref[h0:h0 + 1, 0:1], sink_ref[h0 + 1:h0 + 2, 0:1])
                m = jnp.maximum(jnp.max(s, axis=1, keepdims=True), sink)
                p = jnp.exp(s - m)
                den = jnp.sum(p, axis=1, keepdims=True) + jnp.exp(sink - m)
                pb = p.astype(BF16)
                inv = 1.0 / den
                o = None
                for par in range(2):
                    vs = slice((2 * g + par) * LANES, (2 * g + par + 1) * LANES)
                    vcat = jnp.concatenate([vzm_ref[:, vs],
                                            vz_ref[pl.ds(r_prev, BLOCK), vs],
                                            vz_ref[pl.ds(r_own, BLOCK), vs]], axis=0)
                    rs = slice(par * BLOCK, (par + 1) * BLOCK)
                    t = jnp.dot(pb[rs], vcat, preferred_element_type=F32) * inv[rs]
                    o = t if o is None else o + t
                o_ref[pl.ds(r_q, BLOCK), cs] = o.astype(BF16)
        return carry

    lax.fori_loop(0, nblk, block_body, 0)


def _swa_attention(sinks, qkv, kdm, vzm, bt, batch, seq):
    nc = seq // CQ
    sinkv = jnp.broadcast_to(sinks.reshape(N_SWA_HEADS, 1), (N_SWA_HEADS, LANES))
    return pl.pallas_call(
        _swa_kernel,
        out_shape=jax.ShapeDtypeStruct((batch * seq, N_SWA_HEADS * HEAD_DIM), BF16),
        grid=(batch, nc),
        in_specs=[
            pl.BlockSpec(sinkv.shape, lambda b, c: (0, 0)),
            pl.BlockSpec((CQ, 512), lambda b, c: (b * nc + c, C_SQ // 512)),
            pl.BlockSpec((seq, 256), lambda b, c: (b, C_KD // 256)),
            pl.BlockSpec((seq, 512), lambda b, c: (b, C_VZ // 512)),
            pl.BlockSpec(kdm.shape, lambda b, c: (0, 0)),
            pl.BlockSpec(vzm.shape, lambda b, c: (0, 0)),
            pl.BlockSpec(bt.shape, lambda b, c: (0, 0, 0, 0)),
        ],
        out_specs=pl.BlockSpec((CQ, 512), lambda b, c: (b * nc + c, 0)),
        compiler_params=_cparams(("parallel", "arbitrary")),
        name="swa_attention",
    )(sinkv, qkv, qkv, qkv, kdm, vzm, bt)


def _outproj_kernel(x_ref, md_ref, ms_ref, wo_ref, g2_ref, wr_ref, br_ref,
                    h_ref, hb_ref, rt_ref, ti_ref, cnt_ref, c_ref):
    i = pl.program_id(0)
    tm = x_ref.shape[0]
    half = md_ref.shape[1]

    @pl.when(i == 0)
    def _init():
        c_ref[...] = jnp.zeros(c_ref.shape, F32)

    h = (x_ref[...]
         + jnp.dot(md_ref[...], wo_ref[:half, :], preferred_element_type=F32)
         + jnp.dot(ms_ref[...], wo_ref[half:, :], preferred_element_type=F32))
    h_ref[...] = h
    hn = h * lax.rsqrt(jnp.mean(h * h, axis=-1, keepdims=True) + EPS) * g2_ref[...]
    hb = hn.astype(BF16)
    hb_ref[...] = hb

    lg = jnp.dot(hb, wr_ref[...], preferred_element_type=F32) + br_ref[...]
    lane_i = lax.broadcasted_iota(jnp.int32, lg.shape, 1)
    lane = lane_i.astype(F32)
    big = float(4 * LANES)
    is_g = (lane_i >= N_EXPERTS) & (lane_i < N_EXPERTS + N_GROUPS)
    glm = jnp.where(is_g, lg, -jnp.inf)
    gmax = jnp.max(glm, axis=1, keepdims=True)
    gidx = jnp.min(jnp.where(glm == gmax, lane, big), axis=1, keepdims=True) - N_EXPERTS
    gsum = jnp.sum(jnp.where(is_g, jnp.exp(lg - gmax), 0.0), axis=1, keepdims=True)
    g_w = 1.0 / gsum
    lane_grp = (lane_i >> 3).astype(F32)
    in_grp = (lane_i < N_EXPERTS) & (lane_grp == gidx)
    el = jnp.where(in_grp, lg, -jnp.inf)
    t1 = jnp.max(el, axis=1, keepdims=True)
    j1 = jnp.min(jnp.where(el == t1, lane, big), axis=1, keepdims=True)
    el2 = jnp.where(lane == j1, -jnp.inf, el)
    t2 = jnp.max(el2, axis=1, keepdims=True)
    j2 = jnp.min(jnp.where(el2 == t2, lane, big), axis=1, keepdims=True)
    e2 = jnp.exp(t2 - t1)
    den = 1.0 + e2
    gate1 = g_w / den
    gate2 = g_w * e2 / den

    o1 = lane == j1
    o2 = lane == j2
    onehot = jnp.where(o1 | o2, 1.0, 0.0).astype(BF16)
    rr = lax.broadcasted_iota(jnp.int32, (tm, tm), 0)
    cc = lax.broadcasted_iota(jnp.int32, (tm, tm), 1)
    lower = jnp.where(rr > cc, 1.0, 0.0).astype(BF16)
    pfx = jnp.dot(lower, onehot, preferred_element_type=F32)
    cnt_tile = jnp.sum(onehot.astype(F32), axis=0, keepdims=True)
    nch = jnp.floor((cnt_tile + (CHUNK - 1)) * (1.0 / CHUNK))
    er = lax.broadcasted_iota(jnp.int32, (LANES, LANES), 0)
    ec = lax.broadcasted_iota(jnp.int32, (LANES, LANES), 1)
    before = jnp.where(er < ec, 1.0, 0.0).astype(BF16)
    cbase = CHUNK * jnp.dot(jnp.broadcast_to(nch, (8, LANES)).astype(BF16), before,
                            preferred_element_type=F32)[0:1]
    at = pfx + cbase
    pos1 = jnp.sum(jnp.where(o1, at, 0.0), axis=1, keepdims=True)
    pos2 = jnp.sum(jnp.where(o2, at, 0.0), axis=1, keepdims=True)
    rt_ref[...] = jnp.where(lane_i == 0, gate1,
                            jnp.where(lane_i == 1, gate2,
                                      jnp.where(lane_i == 2, pos1,
                                                jnp.where(lane_i == 3, pos2, 0.0))))
    c_old = c_ref[...]
    c_new = c_old + jnp.floor((cnt_tile + (ROW_ALIGN - 1)) * (1.0 / ROW_ALIGN)) * ROW_ALIGN
    c_ref[...] = c_new
    row8 = lax.broadcasted_iota(jnp.int32, (8, LANES), 0)
    ti_ref[...] = jnp.where(row8 == 0, cnt_tile, jnp.where(row8 == 1, c_old, 0.0))

    @pl.when(i == pl.num_programs(0) - 1)
    def _fin():
        cnt_ref[...] = c_new


def _outproj(x2, mixd, mixs, wo, g2, wr, br):
    n, d = x2.shape
    return pl.pallas_call(
        _outproj_kernel,
        out_shape=(jax.ShapeDtypeStruct((n, d), F32),
                   jax.ShapeDtypeStruct((n, d), BF16),
                   jax.ShapeDtypeStruct((n, LANES), F32),
                   jax.ShapeDtypeStruct((n // TM * 8, LANES), F32),
                   jax.ShapeDtypeStruct((8, LANES), F32)),
        grid=(n // TM,),
        in_specs=[
            pl.BlockSpec((TM, d), lambda i: (i, 0)),
            pl.BlockSpec((TM, mixd.shape[1]), lambda i: (i, 0)),
            pl.BlockSpec((TM, mixs.shape[1]), lambda i: (i, 0)),
            pl.BlockSpec(wo.shape, lambda i: (0, 0)),
            pl.BlockSpec(g2.shape, lambda i: (0, 0)),
            pl.BlockSpec(wr.shape, lambda i: (0, 0)),
            pl.BlockSpec(br.shape, lambda i: (0, 0)),
        ],
        out_specs=(pl.BlockSpec((TM, d), lambda i: (i, 0)),
                   pl.BlockSpec((TM, d), lambda i: (i, 0)),
                   pl.BlockSpec((TM, LANES), lambda i: (i, 0)),
                   pl.BlockSpec((8, LANES), lambda i: (i, 0)),
                   pl.BlockSpec((8, LANES), lambda i: (0, 0))),
        scratch_shapes=[pltpu.VMEM((8, LANES), F32)],
        compiler_params=_cparams(("arbitrary",)),
        name="outproj_router",
    )(x2, mixd, mixs, wo, g2, wr, br)


def _for_each_chunk(runs_ref, fn):
    def per_expert(e, sorted_row):
        start = runs_ref[0, 0, e]
        nch = runs_ref[0, 0, N_EXPERTS + e]

        def per_chunk(c, carry):
            fn(pl.multiple_of(start + c * CHUNK, ROW_ALIGN), pl.multiple_of(sorted_row + c * CHUNK, CHUNK))
            return carry
        lax.fori_loop(0, nch, per_chunk, 0)
        return sorted_row + nch * CHUNK
    lax.fori_loop(0, N_EXPERTS, per_expert, 0)


def _dispatch_kernel(cur_ref, prv_ref, hb_ref, rt_ref, xs_in_ref, xs_ref, sbuf, sem):
    del xs_in_ref
    i = pl.program_id(0)
    nt = pl.num_programs(0)
    slot = i % 2
    tm, d = hb_ref.shape

    pos_t = jnp.transpose(rt_ref[...])
    srow = lax.broadcasted_iota(jnp.int32, (SROWS, tm), 0).astype(F32)
    sel = jnp.where(srow == pos_t[2:3, :], 1.0, jnp.where(srow == pos_t[3:4, :], 1.0, 0.0)).astype(BF16)
    srt = jnp.dot(sel, hb_ref[...], preferred_element_type=F32)
    bits = pltpu.bitcast(srt, jnp.uint32)
    sbuf[slot] = (bits[:, d // 2:] & jnp.uint32(0xFFFF0000)) | (bits[:, :d // 2] >> 16)

    def chunk_copy(run_row, sorted_row, sl):
        return pltpu.make_async_copy(sbuf.at[sl, pl.ds(sorted_row, CHUNK)], xs_ref.at[pl.ds(run_row, CHUNK)],
                                     sem.at[sl])

    def drain(runs_ref, sl):
        def body(c, carry):
            chunk_copy(0, 0, sl).wait()
            return carry
        lax.fori_loop(0, runs_ref[0, 0, 2 * N_EXPERTS], body, 0)

    @pl.when(i > 0)
    def _():
        drain(prv_ref, 1 - slot)

    _for_each_chunk(cur_ref, lambda run_row, sorted_row: chunk_copy(run_row, sorted_row, slot).start())

    @pl.when(i == nt - 1)
    def _():
        drain(cur_ref, slot)


def _dispatch(runs, hb, rt, xs0):
    n, d = hb.shape
    return pl.pallas_call(
        _dispatch_kernel,
        out_shape=jax.ShapeDtypeStruct(xs0.shape, xs0.dtype),
        grid=(n // TM,),
        in_specs=[
            pl.BlockSpec((1, 1, LANES), lambda i: (i, 0, 0), memory_space=pltpu.SMEM),
            pl.BlockSpec((1, 1, LANES), lambda i: (jnp.maximum(i - 1, 0), 0, 0), memory_space=pltpu.SMEM),
            pl.BlockSpec((TM, d), lambda i: (i, 0)),
            pl.BlockSpec((TM, LANES), lambda i: (i, 0)),
            pl.BlockSpec(memory_space=pl.ANY),
        ],
        out_specs=pl.BlockSpec(memory_space=pl.ANY),
        scratch_shapes=[pltpu.VMEM((2, SROWS, d // 2), jnp.uint32), pltpu.SemaphoreType.DMA((2,))],
        input_output_aliases={4: 0},
        compiler_params=_cparams(("arbitrary",)),
        name="dispatch",
    )(runs, runs, hb, rt, xs0)


def _experts_kernel(be_ref, na_ref, xs_ref, wg_ref, wu_ref, wd_ref, ys_ref, wgb, wub, wdb):
    b = pl.program_id(0)

    @pl.when(b < na_ref[0])
    def _():
        e = be_ref[b]
        changed = jnp.logical_or(b == 0, be_ref[jnp.maximum(b - 1, 0)] != e)

        @pl.when(changed)
        def _cast():
            wgb[...] = wg_ref[0].astype(BF16)
            wub[...] = wu_ref[0].astype(BF16)
            wdb[...] = wd_ref[0].astype(BF16)

        w = xs_ref[...]
        x_lo = pltpu.bitcast(w << 16, F32).astype(BF16)
        x_hi = pltpu.bitcast(w & jnp.uint32(0xFFFF0000), F32).astype(BF16)
        dh = w.shape[1]
        g = (jnp.dot(x_lo, wgb[:dh, :], preferred_element_type=F32)
             + jnp.dot(x_hi, wgb[dh:, :], preferred_element_type=F32))
        u = (jnp.dot(x_lo, wub[:dh, :], preferred_element_type=F32)
             + jnp.dot(x_hi, wub[dh:, :], preferred_element_type=F32))
        hdn = g * (1.0 / (1.0 + jnp.exp(-g))) * u
        ys_ref[...] = jnp.dot(hdn.astype(BF16), wdb[...], preferred_element_type=F32)

    @pl.when(b >= na_ref[0])
    def _():
        ys_ref[...] = jnp.zeros(ys_ref.shape, F32)


def _experts(blk_e, n_act, xs, w_gate, w_up, w_down):
    p, dh = xs.shape
    d = 2 * dh
    de = w_gate.shape[2]

    def row_map(b, be, na):
        return (jnp.minimum(b, na[0] - 1), 0)

    def w_map(b, be, na):
        return (be[jnp.minimum(b, na[0] - 1)], 0, 0)

    return pl.pallas_call(
        _experts_kernel,
        out_shape=jax.ShapeDtypeStruct((p, d), F32),
        grid_spec=pltpu.PrefetchScalarGridSpec(
            num_scalar_prefetch=2,
            grid=(p // EB,),
            in_specs=[
                pl.BlockSpec((EB, dh), row_map),
                pl.BlockSpec((1, d, de), w_map),
                pl.BlockSpec((1, d, de), w_map),
                pl.BlockSpec((1, de, d), w_map),
            ],
            out_specs=pl.BlockSpec((EB, d), lambda b, be, na: (b, 0)),
            scratch_shapes=[pltpu.VMEM((d, de), BF16), pltpu.VMEM((d, de), BF16), pltpu.VMEM((de, d), BF16)],
        ),
        compiler_params=_cparams(("arbitrary",)),
        name="experts",
    )(blk_e, n_act, xs, w_gate, w_up, w_down)


def _combine_kernel(cur_ref, nxt_ref, ys_ref, h_ref, rt_ref, o_ref, ybuf, sem):
    i = pl.program_id(0)
    nt = pl.num_programs(0)
    slot = i % 2
    tm = h_ref.shape[0]

    def chunk_copy(run_row, sorted_row, sl):
        return pltpu.make_async_copy(ys_ref.at[pl.ds(run_row, CHUNK)], ybuf.at[sl, pl.ds(sorted_row, CHUNK)],
                                     sem.at[sl])

    @pl.when(i == 0)
    def _():
        ybuf[...] = jnp.zeros(ybuf.shape, F32)
        _for_each_chunk(cur_ref, lambda run_row, sorted_row: chunk_copy(run_row, sorted_row, 0).start())

    @pl.when(i + 1 < nt)
    def _():
        _for_each_chunk(nxt_ref, lambda run_row, sorted_row: chunk_copy(run_row, sorted_row, 1 - slot).start())

    def drain(c, carry):
        chunk_copy(0, 0, slot).wait()
        return carry
    lax.fori_loop(0, cur_ref[0, 0, 2 * N_EXPERTS], drain, 0)

    rt = rt_ref[...]
    yb = ybuf[slot].astype(BF16)
    col = lax.broadcasted_iota(jnp.int32, (tm, SROWS), 1).astype(F32)
    w1 = jnp.where(col == rt[:, 2:3], 1.0, 0.0).astype(BF16)
    w2 = jnp.where(col == rt[:, 3:4], 1.0, 0.0).astype(BF16)
    o_ref[...] = (h_ref[...]
                  + rt[:, 0:1] * jnp.dot(w1, yb, preferred_element_type=F32)
                  + rt[:, 1:2] * jnp.dot(w2, yb, preferred_element_type=F32))


def _combine(runs, ys, h1, rt):
    n, d = h1.shape
    nt = n // TM
    return pl.pallas_call(
        _combine_kernel,
        out_shape=jax.ShapeDtypeStruct((n, d), F32),
        grid=(nt,),
        in_specs=[
            pl.BlockSpec((1, 1, LANES), lambda i: (i, 0, 0), memory_space=pltpu.SMEM),
            pl.BlockSpec((1, 1, LANES), lambda i: (jnp.minimum(i + 1, nt - 1), 0, 0), memory_space=pltpu.SMEM),
            pl.BlockSpec(memory_space=pl.ANY),
            pl.BlockSpec((TM, d), lambda i: (i, 0)),
            pl.BlockSpec((TM, LANES), lambda i: (i, 0)),
        ],
        out_specs=pl.BlockSpec((TM, d), lambda i: (i, 0)),
        scratch_shapes=[pltpu.VMEM((2, SROWS, d), F32), pltpu.SemaphoreType.DMA((2,))],
        compiler_params=_cparams(("arbitrary",)),
        name="combine",
    )(runs, runs, ys, h1, rt)


def _augmented_w_in(w_in):
    sk0 = C_SQ + N_SWA_HEADS * HEAD_DIM
    sv0 = sk0 + N_SWA_KV * HEAD_DIM
    z = jnp.zeros((w_in.shape[0], HEAD_DIM), w_in.dtype)
    vz, kd = [], []
    for g in range(N_SWA_KV):
        k = w_in[:, sk0 + g * HEAD_DIM: sk0 + (g + 1) * HEAD_DIM]
        v = w_in[:, sv0 + g * HEAD_DIM: sv0 + (g + 1) * HEAD_DIM]
        kd += [k, k]
        vz += [v, z, z, v]
    return jnp.concatenate([w_in[:, :C_VZ]] + vz + kd, axis=1)


def kernel(x, meta_tokens, rel_bias, norm1_gain, w_in, diff_q_gain, diff_k_gain, lam_q1, lam_k1, lam_q2, lam_k2, diff_subln_gain, swa_q_gain, swa_k_gain, swa_sinks, w_out, norm2_gain, w_group, b_group, w_router, b_router, w_gate, w_up, w_down):
    batch, seq, d = x.shape
    depth = w_in.shape[0]
    n = batch * seq
    assert seq % CQ == 0 and seq % TQ == 0 and n % TM == 0 and d == 1024
    assert meta_tokens.shape[0] == N_META
    assert depth == 1, "the meta-token rows of the residual stream are not carried across layers"

    h = x.reshape(n, d)
    dblk, bm0, bt = _bias_tables(rel_bias, TQ)
    scale = HEAD_DIM ** -0.5
    bd = jnp.asarray(np.kron(np.eye(MXU_DIM // HEAD_DIM), np.full((HEAD_DIM, HEAD_DIM), 1.0 / HEAD_DIM)), BF16)
    ones = jnp.ones((HEAD_DIM,), F32)
    lower_pad = N_EXPERTS + N_GROUPS

    for layer in range(depth):
        lambda_init = 0.8 - 0.6 * math.exp(-0.3 * layer)
        w_aug = _augmented_w_in(w_in[layer]).astype(BF16)
        gain = jnp.concatenate([
            jnp.tile(diff_q_gain[layer] * scale, 2 * N_DIFF_HEADS),
            jnp.tile(diff_k_gain[layer], 2 * N_DIFF_HEADS),
            jnp.tile(ones, 2 * N_DIFF_HEADS),
            jnp.tile(swa_q_gain[layer] * scale, N_SWA_HEADS),
            jnp.tile(ones, 4 * N_SWA_KV),
            jnp.tile(swa_k_gain[layer], 2 * N_SWA_KV)]).reshape(1, C_END).astype(F32)
        g1 = norm1_gain[layer].reshape(1, d).astype(F32)

        qkv = _proj(h, g1, w_aug, bd, gain, TM)
        qkv_meta = _proj(meta_tokens.astype(F32), g1, w_aug, bd, gain, N_META)
        meta_pad = jnp.pad(qkv_meta, ((0, LANES - N_META), (0, 0)))

        lamv = jnp.pad(jnp.stack([lam_q1[layer], lam_k1[layer], lam_q2[layer], lam_k2[layer]]).astype(F32),
                       ((0, 4), (0, LANES - HEAD_DIM)))
        mixd = _diff_attention(qkv, meta_pad[:, C_DK:C_DV], meta_pad[:, C_DV:C_SQ], dblk, bm0, lamv,
                               diff_subln_gain[layer].reshape(1, LANES).astype(F32), batch, seq, lambda_init)
        mixs = _swa_attention(swa_sinks[layer].astype(F32), qkv, meta_pad[:, C_KD:C_END],
                              meta_pad[:, C_VZ:C_KD], bt, batch, seq)

        wr = jnp.pad(jnp.concatenate([w_router[layer], w_group[layer]], axis=1),
                     ((0, 0), (0, LANES - lower_pad))).astype(BF16)
        br = jnp.pad(jnp.concatenate([b_router[layer], b_group[layer]]), (0, LANES - lower_pad)).reshape(1, LANES)
        h1, hb, rt, tinfo, cnt = _outproj(h, mixd, mixs, w_out[layer].astype(BF16),
                                          norm2_gain[layer].reshape(1, d).astype(F32), wr, br.astype(F32))

        nt = n // TM
        counts = cnt[0, :N_EXPERTS].astype(jnp.int32)
        nblk_e = jnp.where(counts > 0, (counts + CHUNK - 1 + EB - 1) // EB, 0)
        blk_end = jnp.cumsum(nblk_e)
        pstart = ((blk_end - nblk_e) * EB).astype(jnp.int32)
        n_blocks = -(-(2 * n + nt * N_EXPERTS * (ROW_ALIGN - 1) + N_EXPERTS * (EB - 1 + CHUNK - 1)) // EB)
        blk_e = jnp.minimum(jnp.sum(blk_end[None, :] <= jnp.arange(n_blocks)[:, None], axis=1),
                            N_EXPERTS - 1).astype(jnp.int32)
        n_act = blk_end[-1:].astype(jnp.int32)
        ti = tinfo.reshape(nt, 8, LANES)
        run_len = ti[:, 0, :N_EXPERTS].astype(jnp.int32)
        run_start = pstart[None, :] + ti[:, 1, :N_EXPERTS].astype(jnp.int32)
        run_nch = (run_len + CHUNK - 1) // CHUNK
        runs = jnp.concatenate([run_start, run_nch, jnp.sum(run_nch, axis=1, keepdims=True),
                                jnp.zeros((nt, LANES - 2 * N_EXPERTS - 1), jnp.int32)],
                               axis=1).reshape(nt, 1, LANES)

        xs = _dispatch(runs, hb, rt, jnp.zeros((n_blocks * EB, d // 2), jnp.uint32))
        ys = _experts(blk_e, n_act, xs, w_gate[layer], w_up[layer], w_down[layer])
        h = _combine(runs, ys, h1, rt)
    return h.reshape(batch, seq, d)
```

```python
import functools
import math

import numpy as np
import jax
import jax.numpy as jnp
from jax import lax
from jax.experimental import pallas as pl
from jax.experimental.pallas import tpu as pltpu

F32 = jnp.float32
BF16 = jnp.bfloat16

HEAD_DIM = 64
N_DIFF_HEADS = 4
N_SWA_HEADS = 8
N_SWA_KV = 2
BLOCK = 128
N_META = 16
N_BUCKETS = 32
MAX_DISTANCE = 128
N_GROUPS = 4
EXPERTS_PER_GROUP = 8
N_EXPERTS = N_GROUPS * EXPERTS_PER_GROUP
D_EXPERT = 512
EPS = 1e-6
NEG = -1e30

LANES = 128
MXU_DIM = 256
VMEM_LIMIT = 48 * 1024 * 1024

TM = 256
TQ = 256
CQ = 512
EB = 256
ROW_ALIGN = 8
CHUNK = 16
SROWS = -(-(2 * TM + N_EXPERTS * (CHUNK - 1)) // MXU_DIM) * MXU_DIM

C_DQ, C_DK, C_DV, C_SQ, C_VZ, C_KD, C_END = 0, 512, 1024, 1536, 2048, 2560, 2816
NORM_GROUPS = (0, 1, 2, 3, 6, 7, 10)


def _cparams(sem):
    return pltpu.CompilerParams(dimension_semantics=sem, vmem_limit_bytes=VMEM_LIMIT)


def _t5_bucket_np(dist):
    n = np.maximum(dist, 0)
    max_exact = N_BUCKETS // 2
    nf = np.maximum(n, 1).astype(np.float32)
    large = max_exact + (np.log(nf / np.float32(max_exact)) / np.float32(math.log(MAX_DISTANCE / max_exact))
                         * np.float32(N_BUCKETS - max_exact)).astype(np.int32)
    large = np.minimum(large, N_BUCKETS - 1)
    return np.where(n < max_exact, n, large)


def _bias_tables(rel_bias, tq):
    nd = 2 * BLOCK
    buckets = _t5_bucket_np(np.arange(nd))
    assert (buckets[MAX_DISTANCE:] == N_BUCKETS - 1).all()
    rb = rel_bias.astype(F32)
    r = np.arange(BLOCK)[:, None]
    c = np.arange(BLOCK)[None, :]
    d_own = r - c
    d_prev = BLOCK + r - c
    far = rb[N_BUCKETS - 1]

    def take(dist):
        idx = jnp.asarray(buckets[np.clip(dist, 0, nd - 1)], jnp.int32)[None]
        out = jnp.zeros((rb.shape[1],) + dist.shape, F32)
        for b in range(N_BUCKETS):
            out = jnp.where(idx == b, rb[b].reshape((-1,) + (1,) * dist.ndim), out)
        return out

    hd = slice(0, N_DIFF_HEADS)
    far_d = far[hd][:, None, None]
    d0 = jnp.where(d_own[None] >= 0, take(d_own)[hd] - far_d, NEG)
    d1 = take(d_prev)[hd] - far_d
    dblk = jnp.stack([d0, d1], axis=1)
    rq = np.arange(tq)[:, None]
    cm = np.arange(LANES)[None, :]
    d_meta = N_META + rq - cm
    bm0 = jnp.where((cm < N_META)[None], take(d_meta)[hd] - far_d, NEG)

    hs = slice(N_DIFF_HEADS, N_DIFF_HEADS + N_SWA_HEADS)
    far_s = far[hs][:, None, None]
    d_meta_s = N_META + r - cm
    meta_first = jnp.where((cm < N_META)[None], take(d_meta_s)[hs], NEG)
    meta_rest = jnp.where((cm < N_META)[None], jnp.broadcast_to(far_s, (N_SWA_HEADS, BLOCK, LANES)), NEG)
    prev_rest = jnp.where((c > r)[None], take(d_prev)[hs], NEG)
    prev_first = jnp.full((N_SWA_HEADS, BLOCK, BLOCK), NEG, F32)
    own = jnp.where((d_own >= 0)[None], take(d_own)[hs], NEG)
    bt = jnp.stack([jnp.concatenate([meta_first, prev_first, own], axis=-1),
                    jnp.concatenate([meta_rest, prev_rest, own], axis=-1)], axis=0)
    return dblk.astype(F32), bm0.astype(F32), bt.astype(F32)


def _proj_kernel(x_ref, g1_ref, w_ref, bd_ref, gain_ref, o_ref):
    x = x_ref[...]
    a = x * lax.rsqrt(jnp.mean(x * x, axis=-1, keepdims=True) + EPS) * g1_ref[...]
    p = jnp.dot(a.astype(BF16), w_ref[...], preferred_element_type=F32)
    bd = bd_ref[...]
    for j in range(C_END // MXU_DIM):
        sl = slice(j * MXU_DIM, (j + 1) * MXU_DIM)
        pj = p[:, sl]
        if j in NORM_GROUPS:
            ms = jnp.dot((pj * pj).astype(BF16), bd, preferred_element_type=F32)
            pj = pj * lax.rsqrt(ms + EPS) * gain_ref[:, sl]
        o_ref[:, sl] = pj.astype(BF16)


def _proj(x2, g1, w_aug, bd, gain, tm):
    n = x2.shape[0]
    return pl.pallas_call(
        _proj_kernel,
        out_shape=jax.ShapeDtypeStruct((n, C_END), BF16),
        grid=(n // tm,),
        in_specs=[
            pl.BlockSpec((tm, x2.shape[1]), lambda i: (i, 0)),
            pl.BlockSpec(g1.shape, lambda i: (0, 0)),
            pl.BlockSpec(w_aug.shape, lambda i: (0, 0)),
            pl.BlockSpec(bd.shape, lambda i: (0, 0)),
            pl.BlockSpec(gain.shape, lambda i: (0, 0)),
        ],
        out_specs=pl.BlockSpec((tm, C_END), lambda i: (i, 0)),
        compiler_params=_cparams(("parallel",)),
        name="proj",
    )(x2, g1, w_aug, bd, gain)


def _diff_kernel(q_ref, k_ref, v_ref, km_ref, vm_ref, d_ref, bm0_ref, lamv_ref, gain_ref, o_ref,
                 bias_ref, qs_ref, s_buf, p_buf, a_buf, m_ref, acc_ref, *, lambda_init):
    qi = pl.program_id(2)
    tq = q_ref.shape[0]
    nb = tq // BLOCK
    BIAS_NONE, BIAS_LEFT, BIAS_DIAG, BIAS_META = 0, 1, 2, 3

    @pl.when(qi == 0)
    def _build_bias():
        d0 = d_ref[0, 0]
        d1 = d_ref[0, 1]
        zeros = jnp.zeros((BLOCK, BLOCK), F32)
        bias_ref[BIAS_NONE] = jnp.zeros((tq, tq), F32)
        for a in range(nb):
            for b in range(nb):
                rs, cs = slice(a * BLOCK, (a + 1) * BLOCK), slice(b * BLOCK, (b + 1) * BLOCK)
                if a == b:
                    blk = d0
                elif a == b + 1:
                    blk = d1
                elif a > b:
                    blk = zeros
                else:
                    blk = jnp.full((BLOCK, BLOCK), NEG, F32)
                bias_ref[BIAS_DIAG, rs, cs] = blk
                bias_ref[BIAS_LEFT, rs, cs] = d1 if (a == 0 and b == nb - 1) else zeros
        bias_ref[BIAS_META, :, LANES:] = jnp.full((tq, tq - LANES), NEG, F32)

    lane_m = lax.broadcasted_iota(jnp.int32, (tq, LANES), 1)
    meta_rest = jnp.where(lane_m < N_META, 0.0, NEG).astype(F32)
    bias_ref[BIAS_META, :, :LANES] = jnp.where(qi == 0, bm0_ref[0], meta_rest)

    q = q_ref[...]
    lane = lax.broadcasted_iota(jnp.int32, q.shape, 1)
    zero = jnp.zeros_like(q)
    qs_ref[...] = jnp.concatenate([jnp.where(lane < HEAD_DIM, q, zero),
                                   jnp.where(lane >= HEAD_DIM, q, zero)], axis=0)
    m_ref[...] = jnp.full(m_ref.shape, NEG, F32)
    acc_ref[...] = jnp.zeros(acc_ref.shape, F32)
    ones = jnp.ones((tq, LANES), BF16)

    def seq_row(t):
        return pl.multiple_of(jnp.maximum(t - 1, 0) * tq, tq)

    def stage_a(t, slot):
        kt = jnp.where(t == 0, km_ref[...], k_ref[pl.ds(seq_row(t), tq), :])
        s = lax.dot_general(qs_ref[...], kt, (((1,), (1,)), ((), ())), preferred_element_type=F32)
        which = jnp.where(t == 0, BIAS_META, jnp.where(t == qi + 1, BIAS_DIAG,
                                                       jnp.where(t == qi, BIAS_LEFT, BIAS_NONE)))
        s_buf[slot] = (s.reshape(2, tq, tq) + bias_ref[which][None]).reshape(2 * tq, tq)

    def stage_b(slot):
        s = s_buf[slot]
        m_prev = m_ref[...]
        m_new = jnp.maximum(m_prev, jnp.max(s, axis=1, keepdims=True))
        a_buf[slot] = jnp.exp(m_prev - m_new)
        p_buf[slot] = jnp.exp(s - jnp.tile(m_new, (1, tq // LANES))).astype(BF16)
        m_ref[...] = m_new

    def stage_c(t, slot):
        vt = jnp.where(t == 0, vm_ref[...], v_ref[pl.ds(seq_row(t), tq), :])
        pv = jnp.dot(p_buf[slot], jnp.concatenate([vt, ones], axis=1), preferred_element_type=F32)
        acc_ref[...] = jnp.tile(a_buf[slot], (1, 2)) * acc_ref[...] + pv

    stage_a(0, 0)
    stage_a(1, 1)
    stage_b(0)

    def two_steps(k, carry):
        t = 2 + 2 * k
        stage_a(t, 0)
        stage_b(1)
        stage_c(t - 2, 0)
        stage_a(t + 1, 1)
        stage_b(0)
        stage_c(t - 1, 1)
        return carry
    lax.fori_loop(0, qi // 2, two_steps, 0)

    @pl.when(qi % 2 == 1)
    def _odd_tail():
        stage_a(qi + 1, 0)
        stage_b(1)
        stage_c(qi - 1, 0)
        stage_b(0)
        stage_c(qi, 1)
        stage_c(qi + 1, 0)

    @pl.when(qi % 2 == 0)
    def _even_tail():
        stage_b(1)
        stage_c(qi, 0)
        stage_c(qi + 1, 1)

    lv = lamv_ref[...]
    lam = (jnp.exp(jnp.sum(lv[0:1] * lv[1:2], axis=-1, keepdims=True))
           - jnp.exp(jnp.sum(lv[2:3] * lv[3:4], axis=-1, keepdims=True)) + lambda_init)
    acc = acc_ref[...]
    o = acc[:, :LANES] / acc[:, LANES:]
    d = o[:tq] - lam * o[tq:]
    y = d * lax.rsqrt(jnp.mean(d * d, axis=-1, keepdims=True) + EPS) * gain_ref[...]
    o_ref[...] = (y * (1.0 - lambda_init)).astype(BF16)


def _diff_attention(qkv, km, vm, dblk, bm0, lamv, gain, batch, seq, lambda_init):
    nq = seq // TQ
    kern = functools.partial(_diff_kernel, lambda_init=lambda_init)
    return pl.pallas_call(
        kern,
        out_shape=jax.ShapeDtypeStruct((batch * seq, N_DIFF_HEADS * LANES), BF16),
        grid=(batch, N_DIFF_HEADS, nq),
        in_specs=[
            pl.BlockSpec((TQ, LANES), lambda b, h, i: (b * nq + i, C_DQ // LANES + h)),
            pl.BlockSpec((seq, LANES), lambda b, h, i: (b, C_DK // LANES + h)),
            pl.BlockSpec((seq, LANES), lambda b, h, i: (b, C_DV // LANES + h)),
            pl.BlockSpec((TQ, LANES), lambda b, h, i: (0, h)),
            pl.BlockSpec((TQ, LANES), lambda b, h, i: (0, h)),
            pl.BlockSpec((1, 2, BLOCK, BLOCK), lambda b, h, i: (h, 0, 0, 0)),
            pl.BlockSpec((1, TQ, LANES), lambda b, h, i: (h, 0, 0)),
            pl.BlockSpec(lamv.shape, lambda b, h, i: (0, 0)),
            pl.BlockSpec(gain.shape, lambda b, h, i: (0, 0)),
        ],
        out_specs=pl.BlockSpec((TQ, LANES), lambda b, h, i: (b * nq + i, h)),
        scratch_shapes=[
            pltpu.VMEM((4, TQ, TQ), F32),
            pltpu.VMEM((2 * TQ, LANES), BF16),
            pltpu.VMEM((2, 2 * TQ, TQ), F32),
            pltpu.VMEM((2, 2 * TQ, TQ), BF16),
            pltpu.VMEM((2, 2 * TQ, LANES), F32),
            pltpu.VMEM((2 * TQ, LANES), F32),
            pltpu.VMEM((2 * TQ, 2 * LANES), F32),
        ],
        compiler_params=_cparams(("parallel", "parallel", "arbitrary")),
        name="diff_attention",
    )(qkv, qkv, qkv, km, vm, dblk, bm0, lamv, gain)


def _swa_kernel(sink_ref, q_ref, kd_ref, vz_ref, kdm_ref, vzm_ref, bt_ref, o_ref):
    ci = pl.program_id(1)
    nblk = q_ref.shape[0] // BLOCK
    lane = lax.broadcasted_iota(jnp.int32, (BLOCK, LANES), 1)
    row2 = lax.broadcasted_iota(jnp.int32, (2 * BLOCK, 1), 0)

    def block_body(n, carry):
        gblk = ci * nblk + n
        first = jnp.where(gblk == 0, 0, 1)
        r_own = pl.multiple_of(gblk * BLOCK, BLOCK)
        r_prev = pl.multiple_of(jnp.maximum(gblk - 1, 0) * BLOCK, BLOCK)
        r_q = pl.multiple_of(n * BLOCK, BLOCK)
        for g in range(N_SWA_KV):
            kcat = jnp.concatenate([kdm_ref[:, g * LANES:(g + 1) * LANES],
                                    kd_ref[pl.ds(r_prev, BLOCK), g * LANES:(g + 1) * LANES],
                                    kd_ref[pl.ds(r_own, BLOCK), g * LANES:(g + 1) * LANES]], axis=0)
            for u in range(2):
                cs = slice((2 * g + u) * LANES, (2 * g + u + 1) * LANES)
                h0 = 4 * g + 2 * u
                qp = q_ref[pl.ds(r_q, BLOCK), cs]
                zero = jnp.zeros_like(qp)
                qs = jnp.concatenate([jnp.where(lane < HEAD_DIM, qp, zero),
                                      jnp.where(lane >= HEAD_DIM, qp, zero)], axis=0)
                s = lax.dot_general(qs, kcat, (((1,), (1,)), ((), ())), preferred_element_type=F32)
                s = s + jnp.concatenate([bt_ref[first, h0], bt_ref[first, h0 + 1]], axis=0)
                sink = jnp.where(row2 < BLOCK, sink_ref[h0:h0 + 1, 0:1], sink_ref[h0 + 1:h0 + 2, 0:1])
                m = jnp.maximum(jnp.max(s, axis=1, keepdims=True), sink)
                p = jnp.exp(s - m)
                den = jnp.sum(p, axis=1, keepdims=True) + jnp.exp(sink - m)
                pb = p.astype(BF16)
                inv = 1.0 / den
                o = None
                for par in range(2):
                    vs = slice((2 * g + par) * LANES, (2 * g + par + 1) * LANES)
                    vcat = jnp.concatenate([vzm_ref[:, vs],
                                            vz_ref[pl.ds(r_prev, BLOCK), vs],
                                            vz_ref[pl.ds(r_own, BLOCK), vs]], axis=0)
                    rs = slice(par * BLOCK, (par + 1) * BLOCK)
                    t = jnp.dot(pb[rs], vcat, preferred_element_type=F32) * inv[rs]
                    o = t if o is None else o + t
                o_ref[pl.ds(r_q, BLOCK), cs] = o.astype(BF16)
        return carry

    lax.fori_loop(0, nblk, block_body, 0)


def _swa_attention(sinks, qkv, kdm, vzm, bt, batch, seq):
    nc = seq // CQ
    sinkv = jnp.broadcast_to(sinks.reshape(N_SWA_HEADS, 1), (N_SWA_HEADS, LANES))
    return pl.pallas_call(
        _swa_kernel,
        out_shape=jax.ShapeDtypeStruct((batch * seq, N_SWA_HEADS * HEAD_DIM), BF16),
        grid=(batch, nc),
        in_specs=[
            pl.BlockSpec(sinkv.shape, lambda b, c: (0, 0)),
            pl.BlockSpec((CQ, 512), lambda b, c: (b * nc + c, C_SQ // 512)),
            pl.BlockSpec((seq, 256), lambda b, c: (b, C_KD // 256)),
            pl.BlockSpec((seq, 512), lambda b, c: (b, C_VZ // 512)),
            pl.BlockSpec(kdm.shape, lambda b, c: (0, 0)),
            pl.BlockSpec(vzm.shape, lambda b, c: (0, 0)),
            pl.BlockSpec(bt.shape, lambda b, c: (0, 0, 0, 0)),
        ],
        out_specs=pl.BlockSpec((CQ, 512), lambda b, c: (b * nc + c, 0)),
        compiler_params=_cparams(("parallel", "arbitrary")),
        name="swa_attention",
    )(sinkv, qkv, qkv, qkv, kdm, vzm, bt)


def _outproj_kernel(x_ref, md_ref, ms_ref, wo_ref, g2_ref, wr_ref, br_ref,
                    h_ref, hb_ref, rt_ref, ti_ref, cnt_ref, c_ref):
    i = pl.program_id(0)
    tm = x_ref.shape[0]
    half = md_ref.shape[1]

    @pl.when(i == 0)
    def _init():
        c_ref[...] = jnp.zeros(c_ref.shape, F32)

    h = (x_ref[...]
         + jnp.dot(md_ref[...], wo_ref[:half, :], preferred_element_type=F32)
         + jnp.dot(ms_ref[...], wo_ref[half:, :], preferred_element_type=F32))
    h_ref[...] = h
    hn = h * lax.rsqrt(jnp.mean(h * h, axis=-1, keepdims=True) + EPS) * g2_ref[...]
    hb = hn.astype(BF16)
    hb_ref[...] = hb

    lg = jnp.dot(hb, wr_ref[...], preferred_element_type=F32) + br_ref[...]
    lane_i = lax.broadcasted_iota(jnp.int32, lg.shape, 1)
    lane = lane_i.astype(F32)
    big = float(4 * LANES)
    is_g = (lane_i >= N_EXPERTS) & (lane_i < N_EXPERTS + N_GROUPS)
    glm = jnp.where(is_g, lg, -jnp.inf)
    gmax = jnp.max(glm, axis=1, keepdims=True)
    gidx = jnp.min(jnp.where(glm == gmax, lane, big), axis=1, keepdims=True) - N_EXPERTS
    gsum = jnp.sum(jnp.where(is_g, jnp.exp(lg - gmax), 0.0), axis=1, keepdims=True)
    g_w = 1.0 / gsum
    lane_grp = (lane_i >> 3).astype(F32)
    in_grp = (lane_i < N_EXPERTS) & (lane_grp == gidx)
    el = jnp.where(in_grp, lg, -jnp.inf)
    t1 = jnp.max(el, axis=1, keepdims=True)
    j1 = jnp.min(jnp.where(el == t1, lane, big), axis=1, keepdims=True)
    el2 = jnp.where(lane == j1, -jnp.inf, el)
    t2 = jnp.max(el2, axis=1, keepdims=True)
    j2 = jnp.min(jnp.where(el2 == t2, lane, big), axis=1, keepdims=True)
    e2 = jnp.exp(t2 - t1)
    den = 1.0 + e2
    gate1 = g_w / den
    gate2 = g_w * e2 / den

    o1 = lane == j1
    o2 = lane == j2
    onehot = jnp.where(o1 | o2, 1.0, 0.0).astype(BF16)
    rr = lax.broadcasted_iota(jnp.int32, (tm, tm), 0)
    cc = lax.broadcasted_iota(jnp.int32, (tm, tm), 1)
    lower = jnp.where(rr > cc, 1.0, 0.0).astype(BF16)
    pfx = jnp.dot(lower, onehot, preferred_element_type=F32)
    cnt_tile = jnp.sum(onehot.astype(F32), axis=0, keepdims=True)
    nch = jnp.floor((cnt_tile + (CHUNK - 1)) * (1.0 / CHUNK))
    er = lax.broadcasted_iota(jnp.int32, (LANES, LANES), 0)
    ec = lax.broadcasted_iota(jnp.int32, (LANES, LANES), 1)
    before = jnp.where(er < ec, 1.0, 0.0).astype(BF16)
    cbase = CHUNK * jnp.dot(jnp.broadcast_to(nch, (8, LANES)).astype(BF16), before,
                            preferred_element_type=F32)[0:1]
    at = pfx + cbase
    pos1 = jnp.sum(jnp.where(o1, at, 0.0), axis=1, keepdims=True)
    pos2 = jnp.sum(jnp.where(o2, at, 0.0), axis=1, keepdims=True)
    rt_ref[...] = jnp.where(lane_i == 0, gate1,
                            jnp.where(lane_i == 1, gate2,
                                      jnp.where(lane_i == 2, pos1,
                                                jnp.where(lane_i == 3, pos2, 0.0))))
    c_old = c_ref[...]
    c_new = c_old + jnp.floor((cnt_tile + (ROW_ALIGN - 1)) * (1.0 / ROW_ALIGN)) * ROW_ALIGN
    c_ref[...] = c_new
    row8 = lax.broadcasted_iota(jnp.int32, (8, LANES), 0)
    ti_ref[...] = jnp.where(row8 == 0, cnt_tile, jnp.where(row8 == 1, c_old, 0.0))

    @pl.when(i == pl.num_programs(0) - 1)
    def _fin():
        cnt_ref[...] = c_new


def _outproj(x2, mixd, mixs, wo, g2, wr, br):
    n, d = x2.shape
    return pl.pallas_call(
        _outproj_kernel,
        out_shape=(jax.ShapeDtypeStruct((n, d), F32),
                   jax.ShapeDtypeStruct((n, d), BF16),
                   jax.ShapeDtypeStruct((n, LANES), F32),
                   jax.ShapeDtypeStruct((n // TM * 8, LANES), F32),
                   jax.ShapeDtypeStruct((8, LANES), F32)),
        grid=(n // TM,),
        in_specs=[
            pl.BlockSpec((TM, d), lambda i: (i, 0)),
            pl.BlockSpec((TM, mixd.shape[1]), lambda i: (i, 0)),
            pl.BlockSpec((TM, mixs.shape[1]), lambda i: (i, 0)),
            pl.BlockSpec(wo.shape, lambda i: (0, 0)),
            pl.BlockSpec(g2.shape, lambda i: (0, 0)),
            pl.BlockSpec(wr.shape, lambda i: (0, 0)),
            pl.BlockSpec(br.shape, lambda i: (0, 0)),
        ],
        out_specs=(pl.BlockSpec((TM, d), lambda i: (i, 0)),
                   pl.BlockSpec((TM, d), lambda i: (i, 0)),
                   pl.BlockSpec((TM, LANES), lambda i: (i, 0)),
                   pl.BlockSpec((8, LANES), lambda i: (i, 0)),
                   pl.BlockSpec((8, LANES), lambda i: (0, 0))),
        scratch_shapes=[pltpu.VMEM((8, LANES), F32)],
        compiler_params=_cparams(("arbitrary",)),
        name="outproj_router",
    )(x2, mixd, mixs, wo, g2, wr, br)


def _for_each_chunk(runs_ref, fn):
    def per_expert(e, sorted_row):
        start = runs_ref[0, 0, e]
        nch = runs_ref[0, 0, N_EXPERTS + e]

        def per_chunk(c, carry):
            fn(pl.multiple_of(start + c * CHUNK, ROW_ALIGN), pl.multiple_of(sorted_row + c * CHUNK, CHUNK))
            return carry
        lax.fori_loop(0, nch, per_chunk, 0)
        return sorted_row + nch * CHUNK
    lax.fori_loop(0, N_EXPERTS, per_expert, 0)


def _dispatch_kernel(cur_ref, prv_ref, hb_ref, rt_ref, xs_in_ref, xs_ref, sbuf, sem):
    del xs_in_ref
    i = pl.program_id(0)
    nt = pl.num_programs(0)
    slot = i % 2
    tm, d = hb_ref.shape

    pos_t = jnp.transpose(rt_ref[...])
    srow = lax.broadcasted_iota(jnp.int32, (SROWS, tm), 0).astype(F32)
    sel = jnp.where(srow == pos_t[2:3, :], 1.0, jnp.where(srow == pos_t[3:4, :], 1.0, 0.0)).astype(BF16)
    srt = jnp.dot(sel, hb_ref[...], preferred_element_type=F32)
    bits = pltpu.bitcast(srt, jnp.uint32)
    sbuf[slot] = (bits[:, d // 2:] & jnp.uint32(0xFFFF0000)) | (bits[:, :d // 2] >> 16)

    def chunk_copy(run_row, sorted_row, sl):
        return pltpu.make_async_copy(sbuf.at[sl, pl.ds(sorted_row, CHUNK)], xs_ref.at[pl.ds(run_row, CHUNK)],
                                     sem.at[sl])

    def drain(runs_ref, sl):
        def body(c, carry):
            chunk_copy(0, 0, sl).wait()
            return carry
        lax.fori_loop(0, runs_ref[0, 0, 2 * N_EXPERTS], body, 0)

    @pl.when(i > 0)
    def _():
        drain(prv_ref, 1 - slot)

    _for_each_chunk(cur_ref, lambda run_row, sorted_row: chunk_copy(run_row, sorted_row, slot).start())

    @pl.when(i == nt - 1)
    def _():
        drain(cur_ref, slot)


def _dispatch(runs, hb, rt, xs0):
    n, d = hb.shape
    return pl.pallas_call(
        _dispatch_kernel,
        out_shape=jax.ShapeDtypeStruct(xs0.shape, xs0.dtype),
        grid=(n // TM,),
        in_specs=[
            pl.BlockSpec((1, 1, LANES), lambda i: (i, 0, 0), memory_space=pltpu.SMEM),
            pl.BlockSpec((1, 1, LANES), lambda i: (jnp.maximum(i - 1, 0), 0, 0), memory_space=pltpu.SMEM),
            pl.BlockSpec((TM, d), lambda i: (i, 0)),
            pl.BlockSpec((TM, LANES), lambda i: (i, 0)),
            pl.BlockSpec(memory_space=pl.ANY),
        ],
        out_specs=pl.BlockSpec(memory_space=pl.ANY),
        scratch_shapes=[pltpu.VMEM((2, SROWS, d // 2), jnp.uint32), pltpu.SemaphoreType.DMA((2,))],
        input_output_aliases={4: 0},
        compiler_params=_cparams(("arbitrary",)),
        name="dispatch",
    )(runs, runs, hb, rt, xs0)


def _experts_kernel(be_ref, na_ref, xs_ref, wg_ref, wu_ref, wd_ref, ys_ref, wgb, wub, wdb):
    b = pl.program_id(0)

    @pl.when(b < na_ref[0])
    def _():
        e = be_ref[b]
        changed = jnp.logical_or(b == 0, be_ref[jnp.maximum(b - 1, 0)] != e)

        @pl.when(changed)
        def _cast():
            wgb[...] = wg_ref[0].astype(BF16)
            wub[...] = wu_ref[0].astype(BF16)
            wdb[...] = wd_ref[0].astype(BF16)

        w = xs_ref[...]
        x_lo = pltpu.bitcast(w << 16, F32).astype(BF16)
        x_hi = pltpu.bitcast(w & jnp.uint32(0xFFFF0000), F32).astype(BF16)
        dh = w.shape[1]
        g = (jnp.dot(x_lo, wgb[:dh, :], preferred_element_type=F32)
             + jnp.dot(x_hi, wgb[dh:, :], preferred_element_type=F32))
        u = (jnp.dot(x_lo, wub[:dh, :], preferred_element_type=F32)
             + jnp.dot(x_hi, wub[dh:, :], preferred_element_type=F32))
        hdn = g * (1.0 / (1.0 + jnp.exp(-g))) * u
        ys_ref[...] = jnp.dot(hdn.astype(BF16), wdb[...], preferred_element_type=F32)

    @pl.when(b >= na_ref[0])
    def _():
        ys_ref[...] = jnp.zeros(ys_ref.shape, F32)


def _experts(blk_e, n_act, xs, w_gate, w_up, w_down):
    p, dh = xs.shape
    d = 2 * dh
    de = w_gate.shape[2]

    def row_map(b, be, na):
        return (jnp.minimum(b, na[0] - 1), 0)

    def w_map(b, be, na):
        return (be[jnp.minimum(b, na[0] - 1)], 0, 0)

    return pl.pallas_call(
        _experts_kernel,
        out_shape=jax.ShapeDtypeStruct((p, d), F32),
        grid_spec=pltpu.PrefetchScalarGridSpec(
            num_scalar_prefetch=2,
            grid=(p // EB,),
            in_specs=[
                pl.BlockSpec((EB, dh), row_map),
                pl.BlockSpec((1, d, de), w_map),
                pl.BlockSpec((1, d, de), w_map),
                pl.BlockSpec((1, de, d), w_map),
            ],
            out_specs=pl.BlockSpec((EB, d), lambda b, be, na: (b, 0)),
            scratch_shapes=[pltpu.VMEM((d, de), BF16), pltpu.VMEM((d, de), BF16), pltpu.VMEM((de, d), BF16)],
        ),
        compiler_params=_cparams(("arbitrary",)),
        name="experts",
    )(blk_e, n_act, xs, w_gate, w_up, w_down)


def _combine_kernel(cur_ref, nxt_ref, ys_ref, h_ref, rt_ref, o_ref, ybuf, sem):
    i = pl.program_id(0)
    nt = pl.num_programs(0)
    slot = i % 2
    tm = h_ref.shape[0]

    def chunk_copy(run_row, sorted_row, sl):
        return pltpu.make_async_copy(ys_ref.at[pl.ds(run_row, CHUNK)], ybuf.at[sl, pl.ds(sorted_row, CHUNK)],
                                     sem.at[sl])

    @pl.when(i == 0)
    def _():
        ybuf[...] = jnp.zeros(ybuf.shape, F32)
        _for_each_chunk(cur_ref, lambda run_row, sorted_row: chunk_copy(run_row, sorted_row, 0).start())

    @pl.when(i + 1 < nt)
    def _():
        _for_each_chunk(nxt_ref, lambda run_row, sorted_row: chunk_copy(run_row, sorted_row, 1 - slot).start())

    def drain(c, carry):
        chunk_copy(0, 0, slot).wait()
        return carry
    lax.fori_loop(0, cur_ref[0, 0, 2 * N_EXPERTS], drain, 0)

    rt = rt_ref[...]
    yb = ybuf[slot].astype(BF16)
    col = lax.broadcasted_iota(jnp.int32, (tm, SROWS), 1).astype(F32)
    w1 = jnp.where(col == rt[:, 2:3], 1.0, 0.0).astype(BF16)
    w2 = jnp.where(col == rt[:, 3:4], 1.0, 0.0).astype(BF16)
    o_ref[...] = (h_ref[...]
                  + rt[:, 0:1] * jnp.dot(w1, yb, preferred_element_type=F32)
                  + rt[:, 1:2] * jnp.dot(w2, yb, preferred_element_type=F32))


def _combine(runs, ys, h1, rt):
    n, d = h1.shape
    nt = n // TM
    return pl.pallas_call(
        _combine_kernel,
        out_shape=jax.ShapeDtypeStruct((n, d), F32),
        grid=(nt,),
        in_specs=[
            pl.BlockSpec((1, 1, LANES), lambda i: (i, 0, 0), memory_space=pltpu.SMEM),
            pl.BlockSpec((1, 1, LANES), lambda i: (jnp.minimum(i + 1, nt - 1), 0, 0), memory_space=pltpu.SMEM),
            pl.BlockSpec(memory_space=pl.ANY),
            pl.BlockSpec((TM, d), lambda i: (i, 0)),
            pl.BlockSpec((TM, LANES), lambda i: (i, 0)),
        ],
        out_specs=pl.BlockSpec((TM, d), lambda i: (i, 0)),
        scratch_shapes=[pltpu.VMEM((2, SROWS, d), F32), pltpu.SemaphoreType.DMA((2,))],
        compiler_params=_cparams(("arbitrary",)),
        name="combine",
    )(runs, runs, ys, h1, rt)


def _augmented_w_in(w_in):
    sk0 = C_SQ + N_SWA_HEADS * HEAD_DIM
    sv0 = sk0 + N_SWA_KV * HEAD_DIM
    z = jnp.zeros((w_in.shape[0], HEAD_DIM), w_in.dtype)
    vz, kd = [], []
    for g in range(N_SWA_KV):
        k = w_in[:, sk0 + g * HEAD_DIM: sk0 + (g + 1) * HEAD_DIM]
        v = w_in[:, sv0 + g * HEAD_DIM: sv0 + (g + 1) * HEAD_DIM]
        kd += [k, k]
        vz += [v, z, z, v]
    return jnp.concatenate([w_in[:, :C_VZ]] + vz + kd, axis=1)


def kernel(x, meta_tokens, rel_bias, norm1_gain, w_in, diff_q_gain, diff_k_gain, lam_q1, lam_k1, lam_q2, lam_k2, diff_subln_gain, swa_q_gain, swa_k_gain, swa_sinks, w_out, norm2_gain, w_group, b_group, w_router, b_router, w_gate, w_up, w_down):
    batch, seq, d = x.shape
    depth = w_in.shape[0]
    n = batch * seq
    assert seq % CQ == 0 and seq % TQ == 0 and n % TM == 0 and d == 1024
    assert meta_tokens.shape[0] == N_META
    assert depth == 1, "the meta-token rows of the residual stream are not carried across layers"

    h = x.reshape(n, d)
    dblk, bm0, bt = _bias_tables(rel_bias, TQ)
    scale = HEAD_DIM ** -0.5
    bd = jnp.asarray(np.kron(np.eye(MXU_DIM // HEAD_DIM), np.full((HEAD_DIM, HEAD_DIM), 1.0 / HEAD_DIM)), BF16)
    ones = jnp.ones((HEAD_DIM,), F32)
    lower_pad = N_EXPERTS + N_GROUPS

    for layer in range(depth):
        lambda_init = 0.8 - 0.6 * math.exp(-0.3 * layer)
        w_aug = _augmented_w_in(w_in[layer]).astype(BF16)
        gain = jnp.concatenate([
            jnp.tile(diff_q_gain[layer] * scale, 2 * N_DIFF_HEADS),
            jnp.tile(diff_k_gain[layer], 2 * N_DIFF_HEADS),
            jnp.tile(ones, 2 * N_DIFF_HEADS),
            jnp.tile(swa_q_gain[layer] * scale, N_SWA_HEADS),
            jnp.tile(ones, 4 * N_SWA_KV),
            jnp.tile(swa_k_gain[layer], 2 * N_SWA_KV)]).reshape(1, C_END).astype(F32)
        g1 = norm1_gain[layer].reshape(1, d).astype(F32)

        qkv = _proj(h, g1, w_aug, bd, gain, TM)
        qkv_meta = _proj(meta_tokens.astype(F32), g1, w_aug, bd, gain, N_META)
        meta_pad = jnp.pad(qkv_meta, ((0, TQ - N_META), (0, 0)))

        lamv = jnp.pad(jnp.stack([lam_q1[layer], lam_k1[layer], lam_q2[layer], lam_k2[layer]]).astype(F32),
                       ((0, 4), (0, LANES - HEAD_DIM)))
        mixd = _diff_attention(qkv, meta_pad[:, C_DK:C_DV], meta_pad[:, C_DV:C_SQ], dblk, bm0, lamv,
                               diff_subln_gain[layer].reshape(1, LANES).astype(F32), batch, seq, lambda_init)
        mixs = _swa_attention(swa_sinks[layer].astype(F32), qkv, meta_pad[:BLOCK, C_KD:C_END],
                              meta_pad[:BLOCK, C_VZ:C_KD], bt, batch, seq)

        wr = jnp.pad(jnp.concatenate([w_router[layer], w_group[layer]], axis=1),
                     ((0, 0), (0, LANES - lower_pad))).astype(BF16)
        br = jnp.pad(jnp.concatenate([b_router[layer], b_group[layer]]), (0, LANES - lower_pad)).reshape(1, LANES)
        h1, hb, rt, tinfo, cnt = _outproj(h, mixd, mixs, w_out[layer].astype(BF16),
                                          norm2_gain[layer].reshape(1, d).astype(F32), wr, br.astype(F32))

        nt = n // TM
        counts = cnt[0, :N_EXPERTS].astype(jnp.int32)
        nblk_e = jnp.where(counts > 0, (counts + CHUNK - 1 + EB - 1) // EB, 0)
        blk_end = jnp.cumsum(nblk_e)
        pstart = ((blk_end - nblk_e) * EB).astype(jnp.int32)
        n_blocks = -(-(2 * n + nt * N_EXPERTS * (ROW_ALIGN - 1) + N_EXPERTS * (EB - 1 + CHUNK - 1)) // EB)
        blk_e = jnp.minimum(jnp.sum(blk_end[None, :] <= jnp.arange(n_blocks)[:, None], axis=1),
                            N_EXPERTS - 1).astype(jnp.int32)
        n_act = blk_end[-1:].astype(jnp.int32)
        ti = tinfo.reshape(nt, 8, LANES)
        run_len = ti[:, 0, :N_EXPERTS].astype(jnp.int32)
        run_start = pstart[None, :] + ti[:, 1, :N_EXPERTS].astype(jnp.int32)
        run_nch = (run_len + CHUNK - 1) // CHUNK
        runs = jnp.concatenate([run_start, run_nch, jnp.sum(run_nch, axis=1, keepdims=True),
                                jnp.zeros((nt, LANES - 2 * N_EXPERTS - 1), jnp.int32)],
                               axis=1).reshape(nt, 1, LANES)

        xs = _dispatch(runs, hb, rt, jnp.zeros((n_blocks * EB, d // 2), jnp.uint32))
        ys = _experts(blk_e, n_act, xs, w_gate[layer], w_up[layer], w_down[layer])
        h = _combine(runs, ys, h1, rt)
    return h.reshape(batch, seq, d)
```

```python
import functools
import math

import numpy as np
import jax
import jax.numpy as jnp
from jax import lax
from jax.experimental import pallas as pl
from jax.experimental.pallas import tpu as pltpu

F32 = jnp.float32
BF16 = jnp.bfloat16

HEAD_DIM = 64
N_DIFF_HEADS = 4
N_SWA_HEADS = 8
N_SWA_KV = 2
BLOCK = 128
N_META = 16
N_BUCKETS = 32
MAX_DISTANCE = 128
N_GROUPS = 4
EXPERTS_PER_GROUP = 8
N_EXPERTS = N_GROUPS * EXPERTS_PER_GROUP
D_EXPERT = 512
EPS = 1e-6
NEG = -1e30

LANES = 128
MXU_DIM = 256
VMEM_LIMIT = 48 * 1024 * 1024

TM = 256
TQ = 256
CQ = 512
EB = 256
ROW_ALIGN = 8
CHUNK = 16
SROWS = -(-(2 * TM + N_EXPERTS * (CHUNK - 1)) // MXU_DIM) * MXU_DIM

C_DQ, C_DK, C_DV, C_SQ, C_VZ, C_KD, C_END = 0, 512, 1024, 1536, 2048, 2560, 2816
NORM_GROUPS = (0, 1, 2, 3, 6, 7, 10)


def _cparams(sem):
    return pltpu.CompilerParams(dimension_semantics=sem, vmem_limit_bytes=VMEM_LIMIT)


def _t5_bucket_np(dist):
    n = np.maximum(dist, 0)
    max_exact = N_BUCKETS // 2
    nf = np.maximum(n, 1).astype(np.float32)
    large = max_exact + (np.log(nf / np.float32(max_exact)) / np.float32(math.log(MAX_DISTANCE / max_exact))
                         * np.float32(N_BUCKETS - max_exact)).astype(np.int32)
    large = np.minimum(large, N_BUCKETS - 1)
    return np.where(n < max_exact, n, large)


def _bias_tables(rel_bias, tq):
    nd = 2 * BLOCK
    buckets = _t5_bucket_np(np.arange(nd))
    assert (buckets[MAX_DISTANCE:] == N_BUCKETS - 1).all()
    rb = rel_bias.astype(F32)
    r = np.arange(BLOCK)[:, None]
    c = np.arange(BLOCK)[None, :]
    d_own = r - c
    d_prev = BLOCK + r - c
    far = rb[N_BUCKETS - 1]

    def take(dist):
        idx = jnp.asarray(buckets[np.clip(dist, 0, nd - 1)], jnp.int32)[None]
        out = jnp.zeros((rb.shape[1],) + dist.shape, F32)
        for b in range(N_BUCKETS):
            out = jnp.where(idx == b, rb[b].reshape((-1,) + (1,) * dist.ndim), out)
        return out

    hd = slice(0, N_DIFF_HEADS)
    far_d = far[hd][:, None, None]
    d0 = jnp.where(d_own[None] >= 0, take(d_own)[hd] - far_d, NEG)
    d1 = take(d_prev)[hd] - far_d
    dblk = jnp.stack([d0, d1], axis=1)
    rq = np.arange(tq)[:, None]
    cm = np.arange(LANES)[None, :]
    d_meta = N_META + rq - cm
    bm0 = jnp.where((cm < N_META)[None], take(d_meta)[hd] - far_d, NEG)

    hs = slice(N_DIFF_HEADS, N_DIFF_HEADS + N_SWA_HEADS)
    far_s = far[hs][:, None, None]
    d_meta_s = N_META + r - cm
    meta_first = jnp.where((cm < N_META)[None], take(d_meta_s)[hs], NEG)
    meta_rest = jnp.where((cm < N_META)[None], jnp.broadcast_to(far_s, (N_SWA_HEADS, BLOCK, LANES)), NEG)
    prev_rest = jnp.where((c > r)[None], take(d_prev)[hs], NEG)
    prev_first = jnp.full((N_SWA_HEADS, BLOCK, BLOCK), NEG, F32)
    own = jnp.where((d_own >= 0)[None], take(d_own)[hs], NEG)
    bt = jnp.stack([jnp.concatenate([meta_first, prev_first, own], axis=-1),
                    jnp.concatenate([meta_rest, prev_rest, own], axis=-1)], axis=0)
    return dblk.astype(F32), bm0.astype(F32), bt.astype(F32)


def _proj_kernel(x_ref, g1_ref, w_ref, bd_ref, gain_ref, o_ref):
    x = x_ref[...]
    a = x * lax.rsqrt(jnp.mean(x * x, axis=-1, keepdims=True) + EPS) * g1_ref[...]
    p = jnp.dot(a.astype(BF16), w_ref[...], preferred_element_type=F32)
    bd = bd_ref[...]
    for j in range(C_END // MXU_DIM):
        sl = slice(j * MXU_DIM, (j + 1) * MXU_DIM)
        pj = p[:, sl]
        if j in NORM_GROUPS:
            ms = jnp.dot((pj * pj).astype(BF16), bd, preferred_element_type=F32)
            pj = pj * lax.rsqrt(ms + EPS) * gain_ref[:, sl]
        o_ref[:, sl] = pj.astype(BF16)


def _proj(x2, g1, w_aug, bd, gain, tm):
    n = x2.shape[0]
    return pl.pallas_call(
        _proj_kernel,
        out_shape=jax.ShapeDtypeStruct((n, C_END), BF16),
        grid=(n // tm,),
        in_specs=[
            pl.BlockSpec((tm, x2.shape[1]), lambda i: (i, 0)),
            pl.BlockSpec(g1.shape, lambda i: (0, 0)),
            pl.BlockSpec(w_aug.shape, lambda i: (0, 0)),
            pl.BlockSpec(bd.shape, lambda i: (0, 0)),
            pl.BlockSpec(gain.shape, lambda i: (0, 0)),
        ],
        out_specs=pl.BlockSpec((tm, C_END), lambda i: (i, 0)),
        compiler_params=_cparams(("parallel",)),
        name="proj",
    )(x2, g1, w_aug, bd, gain)


def _diff_kernel(qi_tab, t_tab, q_ref, k_ref, v_ref, km_ref, vm_ref, d_ref, bm0_ref, lamv_ref, gain_ref, o_ref,
                 bias_ref, qs_ref, s_buf, p_buf, a_buf, m_ref, acc_ref, *, lambda_init, n_steps):
    tq = TQ
    nq = q_ref.shape[0] // tq
    nb = tq // BLOCK
    BIAS_NONE, BIAS_LEFT, BIAS_DIAG, BIAS_META, BIAS_META0 = 0, 1, 2, 3, 4

    d0 = d_ref[0, 0]
    d1 = d_ref[0, 1]
    zeros = jnp.zeros((BLOCK, BLOCK), F32)
    bias_ref[BIAS_NONE] = jnp.zeros((tq, tq), F32)
    for a in range(nb):
        for b in range(nb):
            rs, cs = slice(a * BLOCK, (a + 1) * BLOCK), slice(b * BLOCK, (b + 1) * BLOCK)
            if a == b:
                blk = d0
            elif a == b + 1:
                blk = d1
            elif a > b:
                blk = zeros
            else:
                blk = jnp.full((BLOCK, BLOCK), NEG, F32)
            bias_ref[BIAS_DIAG, rs, cs] = blk
            bias_ref[BIAS_LEFT, rs, cs] = d1 if (a == 0 and b == nb - 1) else zeros
    lane_m = lax.broadcasted_iota(jnp.int32, (tq, LANES), 1)
    past_meta = jnp.full((tq, tq - LANES), NEG, F32)
    bias_ref[BIAS_META, :, :LANES] = jnp.where(lane_m < N_META, 0.0, NEG).astype(F32)
    bias_ref[BIAS_META, :, LANES:] = past_meta
    bias_ref[BIAS_META0, :, :LANES] = bm0_ref[0]
    bias_ref[BIAS_META0, :, LANES:] = past_meta

    lane = lax.broadcasted_iota(jnp.int32, (tq, LANES), 1)
    for i in range(nq):
        q = q_ref[i * tq:(i + 1) * tq, :]
        zero = jnp.zeros_like(q)
        qs_ref[i] = jnp.concatenate([jnp.where(lane < HEAD_DIM, q, zero),
                                     jnp.where(lane >= HEAD_DIM, q, zero)], axis=0)
    acc_ref[...] = jnp.zeros(acc_ref.shape, F32)
    m_ref[...] = jnp.full(m_ref.shape, NEG, F32)
    ones = jnp.ones((tq, LANES), BF16)
    lv = lamv_ref[...]
    lam = (jnp.exp(jnp.sum(lv[0:1] * lv[1:2], axis=-1, keepdims=True))
           - jnp.exp(jnp.sum(lv[2:3] * lv[3:4], axis=-1, keepdims=True)) + lambda_init)

    def seq_row(t):
        return pl.multiple_of(jnp.maximum(t - 1, 0) * tq, tq)

    def stage_a(n, slot):
        qi, t = qi_tab[n], t_tab[n]
        kt = jnp.where(t == 0, km_ref[...], k_ref[pl.ds(seq_row(t), tq), :])
        s = lax.dot_general(qs_ref[qi], kt, (((1,), (1,)), ((), ())), preferred_element_type=F32)
        which = jnp.where(t == 0, jnp.where(qi == 0, BIAS_META0, BIAS_META),
                          jnp.where(t == qi + 1, BIAS_DIAG, jnp.where(t == qi, BIAS_LEFT, BIAS_NONE)))
        s_buf[slot] = (s.reshape(2, tq, tq) + bias_ref[which][None]).reshape(2 * tq, tq)

    def stage_b(n, slot):
        s = s_buf[slot]
        m_prev = jnp.where(t_tab[n] == 0, NEG, m_ref[...])
        m_new = jnp.maximum(m_prev, jnp.max(s, axis=1, keepdims=True))
        a_buf[slot] = jnp.exp(m_prev - m_new)
        p_buf[slot] = jnp.exp(s - jnp.tile(m_new, (1, tq // LANES))).astype(BF16)
        m_ref[...] = m_new

    def stage_c(n, slot):
        qi, t = qi_tab[n], t_tab[n]
        par = qi % 2
        vt = jnp.where(t == 0, vm_ref[...], v_ref[pl.ds(seq_row(t), tq), :])
        pv = jnp.dot(p_buf[slot], jnp.concatenate([vt, ones], axis=1), preferred_element_type=F32)
        acc_ref[par] = jnp.tile(a_buf[slot], (1, 2)) * acc_ref[par] + pv

    def finish(n):
        qi, t = qi_tab[n], t_tab[n]

        @pl.when(t == qi + 1)
        def _():
            acc = acc_ref[qi % 2]
            o = acc[:, :LANES] / acc[:, LANES:]
            d = o[:tq] - lam * o[tq:]
            y = d * lax.rsqrt(jnp.mean(d * d, axis=-1, keepdims=True) + EPS) * gain_ref[...]
            o_ref[pl.ds(pl.multiple_of(qi * tq, tq), tq), :] = (y * (1.0 - lambda_init)).astype(BF16)

    stage_a(0, 0)
    stage_a(1, 1)
    stage_b(0, 0)

    def two_steps(k, carry):
        n = 2 * k
        stage_a(n + 2, 0)
        stage_b(n + 1, 1)
        stage_c(n, 0)
        stage_a(n + 3, 1)
        stage_b(n + 2, 0)
        stage_c(n + 1, 1)
        finish(n)
        finish(n + 1)
        return carry
    lax.fori_loop(0, n_steps // 2 - 1, two_steps, 0)

    stage_b(n_steps - 1, 1)
    stage_c(n_steps - 2, 0)
    stage_c(n_steps - 1, 1)
    finish(n_steps - 2)
    finish(n_steps - 1)


def _diff_attention(qkv, km, vm, dblk, bm0, lamv, gain, batch, seq, lambda_init):
    nq = seq // TQ
    steps = [(qi, t) for qi in range(nq) for t in range(qi + 2)]
    assert len(steps) % 2 == 0
    qi_tab = jnp.asarray([s[0] for s in steps], jnp.int32)
    t_tab = jnp.asarray([s[1] for s in steps], jnp.int32)
    kern = functools.partial(_diff_kernel, lambda_init=lambda_init, n_steps=len(steps))
    return pl.pallas_call(
        kern,
        out_shape=jax.ShapeDtypeStruct((batch * seq, N_DIFF_HEADS * LANES), BF16),
        grid_spec=pltpu.PrefetchScalarGridSpec(
            num_scalar_prefetch=2,
            grid=(batch, N_DIFF_HEADS),
            in_specs=[
                pl.BlockSpec((seq, LANES), lambda b, h, *_: (b, C_DQ // LANES + h)),
                pl.BlockSpec((seq, LANES), lambda b, h, *_: (b, C_DK // LANES + h)),
                pl.BlockSpec((seq, LANES), lambda b, h, *_: (b, C_DV // LANES + h)),
                pl.BlockSpec((TQ, LANES), lambda b, h, *_: (0, h)),
                pl.BlockSpec((TQ, LANES), lambda b, h, *_: (0, h)),
                pl.BlockSpec((1, 2, BLOCK, BLOCK), lambda b, h, *_: (h, 0, 0, 0)),
                pl.BlockSpec((1, TQ, LANES), lambda b, h, *_: (h, 0, 0)),
                pl.BlockSpec(lamv.shape, lambda b, h, *_: (0, 0)),
                pl.BlockSpec(gain.shape, lambda b, h, *_: (0, 0)),
            ],
            out_specs=pl.BlockSpec((seq, LANES), lambda b, h, *_: (b, h)),
            scratch_shapes=[
                pltpu.VMEM((5, TQ, TQ), F32),
                pltpu.VMEM((nq, 2 * TQ, LANES), BF16),
                pltpu.VMEM((2, 2 * TQ, TQ), F32),
                pltpu.VMEM((2, 2 * TQ, TQ), BF16),
                pltpu.VMEM((2, 2 * TQ, LANES), F32),
                pltpu.VMEM((2 * TQ, LANES), F32),
                pltpu.VMEM((2, 2 * TQ, 2 * LANES), F32),
            ],
        ),
        compiler_params=_cparams(("parallel", "parallel")),
        name="diff_attention",
    )(qi_tab, t_tab, qkv, qkv, qkv, km, vm, dblk, bm0, lamv, gain)


def _swa_kernel(sink_ref, q_ref, kd_ref, vz_ref, kdm_ref, vzm_ref, bt_ref, o_ref):
    ci = pl.program_id(1)
    nblk = q_ref.shape[0] // BLOCK
    lane = lax.broadcasted_iota(jnp.int32, (BLOCK, LANES), 1)
    row2 = lax.broadcasted_iota(jnp.int32, (2 * BLOCK, 1), 0)

    def block_body(n, carry):
        gblk = ci * nblk + n
        first = jnp.where(gblk == 0, 0, 1)
        r_own = pl.multiple_of(gblk * BLOCK, BLOCK)
        r_prev = pl.multiple_of(jnp.maximum(gblk - 1, 0) * BLOCK, BLOCK)
        r_q = pl.multiple_of(n * BLOCK, BLOCK)
        for g in range(N_SWA_KV):
            kcat = jnp.concatenate([kdm_ref[:, g * LANES:(g + 1) * LANES],
                                    kd_ref[pl.ds(r_prev, BLOCK), g * LANES:(g + 1) * LANES],
                                    kd_ref[pl.ds(r_own, BLOCK), g * LANES:(g + 1) * LANES]], axis=0)
            for u in range(2):
                cs = slice((2 * g + u) * LANES, (2 * g + u + 1) * LANES)
                h0 = 4 * g + 2 * u
                qp = q_ref[pl.ds(r_q, BLOCK), cs]
                zero = jnp.zeros_like(qp)
                qs = jnp.concatenate([jnp.where(lane < HEAD_DIM, qp, zero),
                                      jnp.where(lane >= HEAD_DIM, qp, zero)], axis=0)
                s = lax.dot_general(qs, kcat, (((1,), (1,)), ((), ())), preferred_element_type=F32)
                s = s + jnp.concatenate([bt_ref[first, h0], bt_ref[first, h0 + 1]], axis=0)
                sink = jnp.where(row2 < BLOCK, sink_ref[h0:h0 + 1, 0:1], sink_ref[h0 + 1:h0 + 2, 0:1])
                m = jnp.maximum(jnp.max(s, axis=1, keepdims=True), sink)
                p = jnp.exp(s - m)
                den = jnp.sum(p, axis=1, keepdims=True) + jnp.exp(sink - m)
                pb = p.astype(BF16)
                inv = 1.0 / den
                o = None
                for par in range(2):
                    vs = slice((2 * g + par) * LANES, (2 * g + par + 1) * LANES)
                    vcat = jnp.concatenate([vzm_ref[:, vs],
                                            vz_ref[pl.ds(r_prev, BLOCK), vs],
                                            vz_ref[pl.ds(r_own, BLOCK), vs]], axis=0)
                    rs = slice(par * BLOCK, (par + 1) * BLOCK)
                    t = jnp.dot(pb[rs], vcat, preferred_element_type=F32) * inv[rs]
                    o = t if o is None else o + t
                o_ref[pl.ds(r_q, BLOCK), cs] = o.astype(BF16)
        return carry

    lax.fori_loop(0, nblk, block_body, 0)


def _swa_attention(sinks, qkv, kdm, vzm, bt, batch, seq):
    nc = seq // CQ
    sinkv = jnp.broadcast_to(sinks.reshape(N_SWA_HEADS, 1), (N_SWA_HEADS, LANES))
    return pl.pallas_call(
        _swa_kernel,
        out_shape=jax.ShapeDtypeStruct((batch * seq, N_SWA_HEADS * HEAD_DIM), BF16),
        grid=(batch, nc),
        in_specs=[
            pl.BlockSpec(sinkv.shape, lambda b, c: (0, 0)),
            pl.BlockSpec((CQ, 512), lambda b, c: (b * nc + c, C_SQ // 512)),
            pl.BlockSpec((seq, 256), lambda b, c: (b, C_KD // 256)),
            pl.BlockSpec((seq, 512), lambda b, c: (b, C_VZ // 512)),
            pl.BlockSpec(kdm.shape, lambda b, c: (0, 0)),
            pl.BlockSpec(vzm.shape, lambda b, c: (0, 0)),
            pl.BlockSpec(bt.shape, lambda b, c: (0, 0, 0, 0)),
        ],
        out_specs=pl.BlockSpec((CQ, 512), lambda b, c: (b * nc + c, 0)),
        compiler_params=_cparams(("parallel", "arbitrary")),
        name="swa_attention",
    )(sinkv, qkv, qkv, qkv, kdm, vzm, bt)


def _outproj_kernel(x_ref, md_ref, ms_ref, wo_ref, g2_ref, wr_ref, br_ref,
                    h_ref, hb_ref, rt_ref, ti_ref, cnt_ref, c_ref):
    i = pl.program_id(0)
    tm = x_ref.shape[0]
    half = md_ref.shape[1]

    @pl.when(i == 0)
    def _init():
        c_ref[...] = jnp.zeros(c_ref.shape, F32)

    h = (x_ref[...]
         + jnp.dot(md_ref[...], wo_ref[:half, :], preferred_element_type=F32)
         + jnp.dot(ms_ref[...], wo_ref[half:, :], preferred_element_type=F32))
    h_ref[...] = h
    hn = h * lax.rsqrt(jnp.mean(h * h, axis=-1, keepdims=True) + EPS) * g2_ref[...]
    hb = hn.astype(BF16)
    hb_ref[...] = hb

    lg = jnp.dot(hb, wr_ref[...], preferred_element_type=F32) + br_ref[...]
    lane_i = lax.broadcasted_iota(jnp.int32, lg.shape, 1)
    lane = lane_i.astype(F32)
    big = float(4 * LANES)
    is_g = (lane_i >= N_EXPERTS) & (lane_i < N_EXPERTS + N_GROUPS)
    glm = jnp.where(is_g, lg, -jnp.inf)
    gmax = jnp.max(glm, axis=1, keepdims=True)
    gidx = jnp.min(jnp.where(glm == gmax, lane, big), axis=1, keepdims=True) - N_EXPERTS
    gsum = jnp.sum(jnp.where(is_g, jnp.exp(lg - gmax), 0.0), axis=1, keepdims=True)
    g_w = 1.0 / gsum
    lane_grp = (lane_i >> 3).astype(F32)
    in_grp = (lane_i < N_EXPERTS) & (lane_grp == gidx)
    el = jnp.where(in_grp, lg, -jnp.inf)
    t1 = jnp.max(el, axis=1, keepdims=True)
    j1 = jnp.min(jnp.where(el == t1, lane, big), axis=1, keepdims=True)
    el2 = jnp.where(lane == j1, -jnp.inf, el)
    t2 = jnp.max(el2, axis=1, keepdims=True)
    j2 = jnp.min(jnp.where(el2 == t2, lane, big), axis=1, keepdims=True)
    e2 = jnp.exp(t2 - t1)
    den = 1.0 + e2
    gate1 = g_w / den
    gate2 = g_w * e2 / den

    o1 = lane == j1
    o2 = lane == j2
    onehot = jnp.where(o1 | o2, 1.0, 0.0).astype(BF16)
    rr = lax.broadcasted_iota(jnp.int32, (tm, tm), 0)
    cc = lax.broadcasted_iota(jnp.int32, (tm, tm), 1)
    lower = jnp.where(rr > cc, 1.0, 0.0).astype(BF16)
    pfx = jnp.dot(lower, onehot, preferred_element_type=F32)
    cnt_tile = jnp.sum(onehot.astype(F32), axis=0, keepdims=True)
    nch = jnp.floor((cnt_tile + (CHUNK - 1)) * (1.0 / CHUNK))
    er = lax.broadcasted_iota(jnp.int32, (LANES, LANES), 0)
    ec = lax.broadcasted_iota(jnp.int32, (LANES, LANES), 1)
    before = jnp.where(er < ec, 1.0, 0.0).astype(BF16)
    cbase = CHUNK * jnp.dot(jnp.broadcast_to(nch, (8, LANES)).astype(BF16), before,
                            preferred_element_type=F32)[0:1]
    at = pfx + cbase
    pos1 = jnp.sum(jnp.where(o1, at, 0.0), axis=1, keepdims=True)
    pos2 = jnp.sum(jnp.where(o2, at, 0.0), axis=1, keepdims=True)
    rt_ref[...] = jnp.where(lane_i == 0, gate1,
                            jnp.where(lane_i == 1, gate2,
                                      jnp.where(lane_i == 2, pos1,
                                                jnp.where(lane_i == 3, pos2, 0.0))))
    c_old = c_ref[...]
    c_new = c_old + jnp.floor((cnt_tile + (ROW_ALIGN - 1)) * (1.0 / ROW_ALIGN)) * ROW_ALIGN
    c_ref[...] = c_new
    row8 = lax.broadcasted_iota(jnp.int32, (8, LANES), 0)
    ti_ref[...] = jnp.where(row8 == 0, cnt_tile, jnp.where(row8 == 1, c_old, 0.0))

    @pl.when(i == pl.num_programs(0) - 1)
    def _fin():
        cnt_ref[...] = c_new


def _outproj(x2, mixd, mixs, wo, g2, wr, br):
    n, d = x2.shape
    return pl.pallas_call(
        _outproj_kernel,
        out_shape=(jax.ShapeDtypeStruct((n, d), F32),
                   jax.ShapeDtypeStruct((n, d), BF16),
                   jax.ShapeDtypeStruct((n, LANES), F32),
                   jax.ShapeDtypeStruct((n // TM * 8, LANES), F32),
                   jax.ShapeDtypeStruct((8, LANES), F32)),
        grid=(n // TM,),
        in_specs=[
            pl.BlockSpec((TM, d), lambda i: (i, 0)),
            pl.BlockSpec((TM, mixd.shape[1]), lambda i: (i, 0)),
            pl.BlockSpec((TM, mixs.shape[1]), lambda i: (i, 0)),
            pl.BlockSpec(wo.shape, lambda i: (0, 0)),
            pl.BlockSpec(g2.shape, lambda i: (0, 0)),
            pl.BlockSpec(wr.shape, lambda i: (0, 0)),
            pl.BlockSpec(br.shape, lambda i: (0, 0)),
        ],
        out_specs=(pl.BlockSpec((TM, d), lambda i: (i, 0)),
                   pl.BlockSpec((TM, d), lambda i: (i, 0)),
                   pl.BlockSpec((TM, LANES), lambda i: (i, 0)),
                   pl.BlockSpec((8, LANES), lambda i: (i, 0)),
                   pl.BlockSpec((8, LANES), lambda i: (0, 0))),
        scratch_shapes=[pltpu.VMEM((8, LANES), F32)],
        compiler_params=_cparams(("arbitrary",)),
        name="outproj_router",
    )(x2, mixd, mixs, wo, g2, wr, br)


def _for_each_chunk(runs_ref, fn):
    def per_expert(e, sorted_row):
        start = runs_ref[0, 0, e]
        nch = runs_ref[0, 0, N_EXPERTS + e]

        def per_chunk(c, carry):
            fn(pl.multiple_of(start + c * CHUNK, ROW_ALIGN), pl.multiple_of(sorted_row + c * CHUNK, CHUNK))
            return carry
        lax.fori_loop(0, nch, per_chunk, 0)
        return sorted_row + nch * CHUNK
    lax.fori_loop(0, N_EXPERTS, per_expert, 0)


def _dispatch_kernel(cur_ref, prv_ref, hb_ref, rt_ref, xs_in_ref, xs_ref, sbuf, sem):
    del xs_in_ref
    i = pl.program_id(0)
    nt = pl.num_programs(0)
    slot = i % 2
    tm, d = hb_ref.shape

    pos_t = jnp.transpose(rt_ref[...])
    srow = lax.broadcasted_iota(jnp.int32, (SROWS, tm), 0).astype(F32)
    sel = jnp.where(srow == pos_t[2:3, :], 1.0, jnp.where(srow == pos_t[3:4, :], 1.0, 0.0)).astype(BF16)
    srt = jnp.dot(sel, hb_ref[...], preferred_element_type=F32)
    bits = pltpu.bitcast(srt, jnp.uint32)
    sbuf[slot] = (bits[:, d // 2:] & jnp.uint32(0xFFFF0000)) | (bits[:, :d // 2] >> 16)

    def chunk_copy(run_row, sorted_row, sl):
        return pltpu.make_async_copy(sbuf.at[sl, pl.ds(sorted_row, CHUNK)], xs_ref.at[pl.ds(run_row, CHUNK)],
                                     sem.at[sl])

    def drain(runs_ref, sl):
        def body(c, carry):
            chunk_copy(0, 0, sl).wait()
            return carry
        lax.fori_loop(0, runs_ref[0, 0, 2 * N_EXPERTS], body, 0)

    @pl.when(i > 0)
    def _():
        drain(prv_ref, 1 - slot)

    _for_each_chunk(cur_ref, lambda run_row, sorted_row: chunk_copy(run_row, sorted_row, slot).start())

    @pl.when(i == nt - 1)
    def _():
        drain(cur_ref, slot)


def _dispatch(runs, hb, rt, xs0):
    n, d = hb.shape
    return pl.pallas_call(
        _dispatch_kernel,
        out_shape=jax.ShapeDtypeStruct(xs0.shape, xs0.dtype),
        grid=(n // TM,),
        in_specs=[
            pl.BlockSpec((1, 1, LANES), lambda i: (i, 0, 0), memory_space=pltpu.SMEM),
            pl.BlockSpec((1, 1, LANES), lambda i: (jnp.maximum(i - 1, 0), 0, 0), memory_space=pltpu.SMEM),
            pl.BlockSpec((TM, d), lambda i: (i, 0)),
            pl.BlockSpec((TM, LANES), lambda i: (i, 0)),
            pl.BlockSpec(memory_space=pl.ANY),
        ],
        out_specs=pl.BlockSpec(memory_space=pl.ANY),
        scratch_shapes=[pltpu.VMEM((2, SROWS, d // 2), jnp.uint32), pltpu.SemaphoreType.DMA((2,))],
        input_output_aliases={4: 0},
        compiler_params=_cparams(("arbitrary",)),
        name="dispatch",
    )(runs, runs, hb, rt, xs0)


def _experts_kernel(be_ref, na_ref, xs_ref, wg_ref, wu_ref, wd_ref, ys_ref, wgb, wub, wdb):
    b = pl.program_id(0)

    @pl.when(b < na_ref[0])
    def _():
        e = be_ref[b]
        changed = jnp.logical_or(b == 0, be_ref[jnp.maximum(b - 1, 0)] != e)

        @pl.when(changed)
        def _cast():
            wgb[...] = wg_ref[0].astype(BF16)
            wub[...] = wu_ref[0].astype(BF16)
            wdb[...] = wd_ref[0].astype(BF16)

        w = xs_ref[...]
        x_lo = pltpu.bitcast(w << 16, F32).astype(BF16)
        x_hi = pltpu.bitcast(w & jnp.uint32(0xFFFF0000), F32).astype(BF16)
        dh = w.shape[1]
        g = (jnp.dot(x_lo, wgb[:dh, :], preferred_element_type=F32)
             + jnp.dot(x_hi, wgb[dh:, :], preferred_element_type=F32))
        u = (jnp.dot(x_lo, wub[:dh, :], preferred_element_type=F32)
             + jnp.dot(x_hi, wub[dh:, :], preferred_element_type=F32))
        hdn = g * (1.0 / (1.0 + jnp.exp(-g))) * u
        ys_ref[...] = jnp.dot(hdn.astype(BF16), wdb[...], preferred_element_type=F32)

    @pl.when(b >= na_ref[0])
    def _():
        ys_ref[...] = jnp.zeros(ys_ref.shape, F32)


def _experts(blk_e, n_act, xs, w_gate, w_up, w_down):
    p, dh = xs.shape
    d = 2 * dh
    de = w_gate.shape[2]

    def row_map(b, be, na):
        return (jnp.minimum(b, na[0] - 1), 0)

    def w_map(b, be, na):
        return (be[jnp.minimum(b, na[0] - 1)], 0, 0)

    return pl.pallas_call(
        _experts_kernel,
        out_shape=jax.ShapeDtypeStruct((p, d), F32),
        grid_spec=pltpu.PrefetchScalarGridSpec(
            num_scalar_prefetch=2,
            grid=(p // EB,),
            in_specs=[
                pl.BlockSpec((EB, dh), row_map),
                pl.BlockSpec((1, d, de), w_map),
                pl.BlockSpec((1, d, de), w_map),
                pl.BlockSpec((1, de, d), w_map),
            ],
            out_specs=pl.BlockSpec((EB, d), lambda b, be, na: (b, 0)),
            scratch_shapes=[pltpu.VMEM((d, de), BF16), pltpu.VMEM((d, de), BF16), pltpu.VMEM((de, d), BF16)],
        ),
        compiler_params=_cparams(("arbitrary",)),
        name="experts",
    )(blk_e, n_act, xs, w_gate, w_up, w_down)


def _combine_kernel(cur_ref, nxt_ref, ys_ref, h_ref, rt_ref, o_ref, ybuf, sem):
    i = pl.program_id(0)
    nt = pl.num_programs(0)
    slot = i % 2
    tm = h_ref.shape[0]

    def chunk_copy(run_row, sorted_row, sl):
        return pltpu.make_async_copy(ys_ref.at[pl.ds(run_row, CHUNK)], ybuf.at[sl, pl.ds(sorted_row, CHUNK)],
                                     sem.at[sl])

    @pl.when(i == 0)
    def _():
        ybuf[...] = jnp.zeros(ybuf.shape, F32)
        _for_each_chunk(cur_ref, lambda run_row, sorted_row: chunk_copy(run_row, sorted_row, 0).start())

    @pl.when(i + 1 < nt)
    def _():
        _for_each_chunk(nxt_ref, lambda run_row, sorted_row: chunk_copy(run_row, sorted_row, 1 - slot).start())

    def drain(c, carry):
        chunk_copy(0, 0, slot).wait()
        return carry
    lax.fori_loop(0, cur_ref[0, 0, 2 * N_EXPERTS], drain, 0)

    rt = rt_ref[...]
    yb = ybuf[slot].astype(BF16)
    col = lax.broadcasted_iota(jnp.int32, (tm, SROWS), 1).astype(F32)
    w1 = jnp.where(col == rt[:, 2:3], 1.0, 0.0).astype(BF16)
    w2 = jnp.where(col == rt[:, 3:4], 1.0, 0.0).astype(BF16)
    o_ref[...] = (h_ref[...]
                  + rt[:, 0:1] * jnp.dot(w1, yb, preferred_element_type=F32)
                  + rt[:, 1:2] * jnp.dot(w2, yb, preferred_element_type=F32))


def _combine(runs, ys, h1, rt):
    n, d = h1.shape
    nt = n // TM
    return pl.pallas_call(
        _combine_kernel,
        out_shape=jax.ShapeDtypeStruct((n, d), F32),
        grid=(nt,),
        in_specs=[
            pl.BlockSpec((1, 1, LANES), lambda i: (i, 0, 0), memory_space=pltpu.SMEM),
            pl.BlockSpec((1, 1, LANES), lambda i: (jnp.minimum(i + 1, nt - 1), 0, 0), memory_space=pltpu.SMEM),
            pl.BlockSpec(memory_space=pl.ANY),
            pl.BlockSpec((TM, d), lambda i: (i, 0)),
            pl.BlockSpec((TM, LANES), lambda i: (i, 0)),
        ],
        out_specs=pl.BlockSpec((TM, d), lambda i: (i, 0)),
        scratch_shapes=[pltpu.VMEM((2, SROWS, d), F32), pltpu.SemaphoreType.DMA((2,))],
        compiler_params=_cparams(("arbitrary",)),
        name="combine",
    )(runs, runs, ys, h1, rt)


def _augmented_w_in(w_in):
    sk0 = C_SQ + N_SWA_HEADS * HEAD_DIM
    sv0 = sk0 + N_SWA_KV * HEAD_DIM
    z = jnp.zeros((w_in.shape[0], HEAD_DIM), w_in.dtype)
    vz, kd = [], []
    for g in range(N_SWA_KV):
        k = w_in[:, sk0 + g * HEAD_DIM: sk0 + (g + 1) * HEAD_DIM]
        v = w_in[:, sv0 + g * HEAD_DIM: sv0 + (g + 1) * HEAD_DIM]
        kd += [k, k]
        vz += [v, z, z, v]
    return jnp.concatenate([w_in[:, :C_VZ]] + vz + kd, axis=1)


def kernel(x, meta_tokens, rel_bias, norm1_gain, w_in, diff_q_gain, diff_k_gain, lam_q1, lam_k1, lam_q2, lam_k2, diff_subln_gain, swa_q_gain, swa_k_gain, swa_sinks, w_out, norm2_gain, w_group, b_group, w_router, b_router, w_gate, w_up, w_down):
    batch, seq, d = x.shape
    depth = w_in.shape[0]
    n = batch * seq
    assert seq % CQ == 0 and seq % TQ == 0 and n % TM == 0 and d == 1024
    assert meta_tokens.shape[0] == N_META
    assert depth == 1, "the meta-token rows of the residual stream are not carried across layers"

    h = x.reshape(n, d)
    dblk, bm0, bt = _bias_tables(rel_bias, TQ)
    scale = HEAD_DIM ** -0.5
    bd = jnp.asarray(np.kron(np.eye(MXU_DIM // HEAD_DIM), np.full((HEAD_DIM, HEAD_DIM), 1.0 / HEAD_DIM)), BF16)
    ones = jnp.ones((HEAD_DIM,), F32)
    lower_pad = N_EXPERTS + N_GROUPS

    for layer in range(depth):
        lambda_init = 0.8 - 0.6 * math.exp(-0.3 * layer)
        w_aug = _augmented_w_in(w_in[layer]).astype(BF16)
        gain = jnp.concatenate([
            jnp.tile(diff_q_gain[layer] * scale, 2 * N_DIFF_HEADS),
            jnp.tile(diff_k_gain[layer], 2 * N_DIFF_HEADS),
            jnp.tile(ones, 2 * N_DIFF_HEADS),
            jnp.tile(swa_q_gain[layer] * scale, N_SWA_HEADS),
            jnp.tile(ones, 4 * N_SWA_KV),
            jnp.tile(swa_k_gain[layer], 2 * N_SWA_KV)]).reshape(1, C_END).astype(F32)
        g1 = norm1_gain[layer].reshape(1, d).astype(F32)

        qkv = _proj(h, g1, w_aug, bd, gain, TM)
        qkv_meta = _proj(meta_tokens.astype(F32), g1, w_aug, bd, gain, N_META)
        meta_pad = jnp.pad(qkv_meta, ((0, TQ - N_META), (0, 0)))

        lamv = jnp.pad(jnp.stack([lam_q1[layer], lam_k1[layer], lam_q2[layer], lam_k2[layer]]).astype(F32),
                       ((0, 4), (0, LANES - HEAD_DIM)))
        mixd = _diff_attention(qkv, meta_pad[:, C_DK:C_DV], meta_pad[:, C_DV:C_SQ], dblk, bm0, lamv,
                               diff_subln_gain[layer].reshape(1, LANES).astype(F32), batch, seq, lambda_init)
        mixs = _swa_attention(swa_sinks[layer].astype(F32), qkv, meta_pad[:BLOCK, C_KD:C_END],
                              meta_pad[:BLOCK, C_VZ:C_KD], bt, batch, seq)

        wr = jnp.pad(jnp.concatenate([w_router[layer], w_group[layer]], axis=1),
                     ((0, 0), (0, LANES - lower_pad))).astype(BF16)
        br = jnp.pad(jnp.concatenate([b_router[layer], b_group[layer]]), (0, LANES - lower_pad)).reshape(1, LANES)
        h1, hb, rt, tinfo, cnt = _outproj(h, mixd, mixs, w_out[layer].astype(BF16),
                                          norm2_gain[layer].reshape(1, d).astype(F32), wr, br.astype(F32))

        nt = n // TM
        counts = cnt[0, :N_EXPERTS].astype(jnp.int32)
        nblk_e = jnp.where(counts > 0, (counts + CHUNK - 1 + EB - 1) // EB, 0)
        blk_end = jnp.cumsum(nblk_e)
        pstart = ((blk_end - nblk_e) * EB).astype(jnp.int32)
        n_blocks = -(-(2 * n + nt * N_EXPERTS * (ROW_ALIGN - 1) + N_EXPERTS * (EB - 1 + CHUNK - 1)) // EB)
        blk_e = jnp.minimum(jnp.sum(blk_end[None, :] <= jnp.arange(n_blocks)[:, None], axis=1),
                            N_EXPERTS - 1).astype(jnp.int32)
        n_act = blk_end[-1:].astype(jnp.int32)
        ti = tinfo.reshape(nt, 8, LANES)
        run_len = ti[:, 0, :N_EXPERTS].astype(jnp.int32)
        run_start = pstart[None, :] + ti[:, 1, :N_EXPERTS].astype(jnp.int32)
        run_nch = (run_len + CHUNK - 1) // CHUNK
        runs = jnp.concatenate([run_start, run_nch, jnp.sum(run_nch, axis=1, keepdims=True),
                                jnp.zeros((nt, LANES - 2 * N_EXPERTS - 1), jnp.int32)],
                               axis=1).reshape(nt, 1, LANES)

        xs = _dispatch(runs, hb, rt, jnp.zeros((n_blocks * EB, d // 2), jnp.uint32))
        ys = _experts(blk_e, n_act, xs, w_gate[layer], w_up[layer], w_down[layer])
        h = _combine(runs, ys, h1, rt)
    return h.reshape(batch, seq, d)
```

```python
import functools
import math

import numpy as np
import jax
import jax.numpy as jnp
from jax import lax
from jax.experimental import pallas as pl
from jax.experimental.pallas import tpu as pltpu

F32 = jnp.float32
BF16 = jnp.bfloat16

HEAD_DIM = 64
N_DIFF_HEADS = 4
N_SWA_HEADS = 8
N_SWA_KV = 2
BLOCK = 128
N_META = 16
N_BUCKETS = 32
MAX_DISTANCE = 128
N_GROUPS = 4
EXPERTS_PER_GROUP = 8
N_EXPERTS = N_GROUPS * EXPERTS_PER_GROUP
D_EXPERT = 512
EPS = 1e-6
NEG = -1e30

LANES = 128
MXU_DIM = 256
VMEM_LIMIT = 48 * 1024 * 1024

TM = 256
TQ = 256
ONES_ROWS = 16
CQ = 512
EB = 256
ROW_ALIGN = 8
CHUNK = 16
SROWS = -(-(2 * TM + N_EXPERTS * (CHUNK - 1)) // MXU_DIM) * MXU_DIM

C_DQ, C_DK, C_DV, C_SQ, C_VZ, C_KD, C_END = 0, 512, 1024, 1536, 2048, 2560, 2816
NORM_GROUPS = (0, 1, 2, 3, 6, 7, 10)


def _cparams(sem):
    return pltpu.CompilerParams(dimension_semantics=sem, vmem_limit_bytes=VMEM_LIMIT)


def _t5_bucket_np(dist):
    n = np.maximum(dist, 0)
    max_exact = N_BUCKETS // 2
    nf = np.maximum(n, 1).astype(np.float32)
    large = max_exact + (np.log(nf / np.float32(max_exact)) / np.float32(math.log(MAX_DISTANCE / max_exact))
                         * np.float32(N_BUCKETS - max_exact)).astype(np.int32)
    large = np.minimum(large, N_BUCKETS - 1)
    return np.where(n < max_exact, n, large)


def _bias_tables(rel_bias, tq):
    nd = 2 * BLOCK
    buckets = _t5_bucket_np(np.arange(nd))
    assert (buckets[MAX_DISTANCE:] == N_BUCKETS - 1).all()
    rb = rel_bias.astype(F32)
    r = np.arange(BLOCK)[:, None]
    c = np.arange(BLOCK)[None, :]
    d_own = r - c
    d_prev = BLOCK + r - c
    far = rb[N_BUCKETS - 1]

    def take(dist):
        idx = jnp.asarray(buckets[np.clip(dist, 0, nd - 1)], jnp.int32)[None]
        out = jnp.zeros((rb.shape[1],) + dist.shape, F32)
        for b in range(N_BUCKETS):
            out = jnp.where(idx == b, rb[b].reshape((-1,) + (1,) * dist.ndim), out)
        return out

    hd = slice(0, N_DIFF_HEADS)
    far_d = far[hd][:, None, None]
    d0 = jnp.where(d_own[None] >= 0, take(d_own)[hd] - far_d, NEG)
    d1 = take(d_prev)[hd] - far_d
    dblk = jnp.stack([d0, d1], axis=1)
    rq = np.arange(tq)[:, None]
    cm = np.arange(LANES)[None, :]
    d_meta = N_META + rq - cm
    bm0 = jnp.where((cm < N_META)[None], take(d_meta)[hd] - far_d, NEG)

    hs = slice(N_DIFF_HEADS, N_DIFF_HEADS + N_SWA_HEADS)
    far_s = far[hs][:, None, None]
    d_meta_s = N_META + r - cm
    meta_first = jnp.where((cm < N_META)[None], take(d_meta_s)[hs], NEG)
    meta_rest = jnp.where((cm < N_META)[None], jnp.broadcast_to(far_s, (N_SWA_HEADS, BLOCK, LANES)), NEG)
    prev_rest = jnp.where((c > r)[None], take(d_prev)[hs], NEG)
    prev_first = jnp.full((N_SWA_HEADS, BLOCK, BLOCK), NEG, F32)
    own = jnp.where((d_own >= 0)[None], take(d_own)[hs], NEG)
    bt = jnp.stack([jnp.concatenate([meta_first, prev_first, own], axis=-1),
                    jnp.concatenate([meta_rest, prev_rest, own], axis=-1)], axis=0)
    return dblk.astype(F32), bm0.astype(F32), bt.astype(F32)


def _proj_kernel(x_ref, g1_ref, w_ref, bd_ref, gain_ref, o_ref):
    x = x_ref[...]
    a = x * lax.rsqrt(jnp.mean(x * x, axis=-1, keepdims=True) + EPS) * g1_ref[...]
    p = jnp.dot(a.astype(BF16), w_ref[...], preferred_element_type=F32)
    bd = bd_ref[...]
    for j in range(C_END // MXU_DIM):
        sl = slice(j * MXU_DIM, (j + 1) * MXU_DIM)
        pj = p[:, sl]
        if j in NORM_GROUPS:
            ms = jnp.dot((pj * pj).astype(BF16), bd, preferred_element_type=F32)
            pj = pj * lax.rsqrt(ms + EPS) * gain_ref[:, sl]
        o_ref[:, sl] = pj.astype(BF16)


def _proj(x2, g1, w_aug, bd, gain, tm):
    n = x2.shape[0]
    return pl.pallas_call(
        _proj_kernel,
        out_shape=jax.ShapeDtypeStruct((n, C_END), BF16),
        grid=(n // tm,),
        in_specs=[
            pl.BlockSpec((tm, x2.shape[1]), lambda i: (i, 0)),
            pl.BlockSpec(g1.shape, lambda i: (0, 0)),
            pl.BlockSpec(w_aug.shape, lambda i: (0, 0)),
            pl.BlockSpec(bd.shape, lambda i: (0, 0)),
            pl.BlockSpec(gain.shape, lambda i: (0, 0)),
        ],
        out_specs=pl.BlockSpec((tm, C_END), lambda i: (i, 0)),
        compiler_params=_cparams(("parallel",)),
        name="proj",
    )(x2, g1, w_aug, bd, gain)


def _diff_kernel(qi_tab, t_tab, q_ref, k_ref, v_ref, km_ref, vm_ref, d_ref, bm0_ref, lamv_ref, gain_ref, o_ref,
                 bias_ref, qs_ref, vt_ref, s_buf, p_buf, a_buf, m_ref, acc_ref, *, lambda_init, n_steps):
    tq = TQ
    nq = q_ref.shape[0] // tq
    nb = tq // BLOCK
    BIAS_NONE, BIAS_LEFT, BIAS_DIAG, BIAS_META, BIAS_META0 = 0, 1, 2, 3, 4

    d0 = d_ref[0, 0]
    d1 = d_ref[0, 1]
    zeros = jnp.zeros((BLOCK, BLOCK), F32)
    bias_ref[BIAS_NONE] = jnp.zeros((tq, tq), F32)
    for a in range(nb):
        for b in range(nb):
            rs, cs = slice(a * BLOCK, (a + 1) * BLOCK), slice(b * BLOCK, (b + 1) * BLOCK)
            if a == b:
                blk = d0
            elif b == a + 1:
                blk = d1
            elif b > a:
                blk = zeros
            else:
                blk = jnp.full((BLOCK, BLOCK), NEG, F32)
            bias_ref[BIAS_DIAG, rs, cs] = blk
            bias_ref[BIAS_LEFT, rs, cs] = d1 if (b == 0 and a == nb - 1) else zeros
    row_m = lax.broadcasted_iota(jnp.int32, (tq, tq), 0)
    bias_ref[BIAS_META] = jnp.where(row_m < N_META, 0.0, NEG).astype(F32)
    bias_ref[BIAS_META0, :LANES, :] = bm0_ref[0]
    bias_ref[BIAS_META0, LANES:, :] = jnp.full((tq - LANES, tq), NEG, F32)

    lane = lax.broadcasted_iota(jnp.int32, (tq, LANES), 1)
    for i in range(nq):
        rows = slice(i * tq, (i + 1) * tq)
        q = q_ref[rows, :]
        zero = jnp.zeros_like(q)
        qs_ref[i] = jnp.concatenate([jnp.where(lane < HEAD_DIM, q, zero),
                                     jnp.where(lane >= HEAD_DIM, q, zero)], axis=0)
        vt_ref[i, :LANES, :] = jnp.transpose(v_ref[rows, :].astype(F32)).astype(BF16)
    vt_ref[nq, :LANES, :] = jnp.transpose(vm_ref[...].astype(F32)).astype(BF16)
    vt_ref[:, LANES:, :] = jnp.ones((nq + 1, ONES_ROWS, tq), BF16)
    acc_ref[...] = jnp.zeros(acc_ref.shape, F32)
    m_ref[...] = jnp.full(m_ref.shape, NEG, F32)
    lv = lamv_ref[...]
    lam = (jnp.exp(jnp.sum(lv[0:1] * lv[1:2], axis=-1, keepdims=True))
           - jnp.exp(jnp.sum(lv[2:3] * lv[3:4], axis=-1, keepdims=True)) + lambda_init)

    def seq_row(t):
        return pl.multiple_of(jnp.maximum(t - 1, 0) * tq, tq)

    def stage_a(n, slot):
        qi, t = qi_tab[n], t_tab[n]
        kt = jnp.where(t == 0, km_ref[...], k_ref[pl.ds(seq_row(t), tq), :])
        s = lax.dot_general(kt, qs_ref[qi], (((1,), (1,)), ((), ())), preferred_element_type=F32)
        which = jnp.where(t == 0, jnp.where(qi == 0, BIAS_META0, BIAS_META),
                          jnp.where(t == qi + 1, BIAS_DIAG, jnp.where(t == qi, BIAS_LEFT, BIAS_NONE)))
        s_buf[slot] = s + jnp.tile(bias_ref[which], (1, 2))

    def stage_b(n, slot):
        s = s_buf[slot]
        m_prev = jnp.where(t_tab[n] == 0, NEG, m_ref[...])
        m_new = jnp.maximum(m_prev, jnp.max(s, axis=0, keepdims=True))
        a_buf[slot] = jnp.exp(m_prev - m_new)
        p_buf[slot] = jnp.exp(s - m_new[0:1]).astype(BF16)
        m_ref[...] = m_new

    def stage_c(n, slot):
        qi, t = qi_tab[n], t_tab[n]
        par = qi % 2
        vt = vt_ref[jnp.where(t == 0, nq, t - 1)]
        pv = jnp.dot(vt, p_buf[slot], preferred_element_type=F32)
        acc_ref[par] = a_buf[slot][0:1] * acc_ref[par] + pv

    def finish(n):
        qi, t = qi_tab[n], t_tab[n]

        @pl.when(t == qi + 1)
        def _():
            acc = acc_ref[qi % 2]
            o = acc[:LANES] / acc[LANES:LANES + 1]
            d = o[:, :tq] - lam * o[:, tq:]
            y = d * lax.rsqrt(jnp.mean(d * d, axis=0, keepdims=True) + EPS) * jnp.tile(gain_ref[...], (1, tq // LANES))
            y = jnp.transpose(y * (1.0 - lambda_init))
            o_ref[pl.ds(pl.multiple_of(qi * tq, tq), tq), :] = y.astype(BF16)

    stage_a(0, 0)
    stage_a(1, 1)
    stage_b(0, 0)

    def steps(n, count):
        for j in range(count):
            stage_a(n + j + 2, j % 2)
            stage_b(n + j + 1, (j + 1) % 2)
            stage_c(n + j, j % 2)
        for j in range(count):
            finish(n + j)

    unroll = 4
    n_blocks = (n_steps - 2) // unroll

    def block(k, carry):
        steps(unroll * k, unroll)
        return carry
    lax.fori_loop(0, n_blocks, block, 0)
    steps(unroll * n_blocks, n_steps - 2 - unroll * n_blocks)

    stage_b(n_steps - 1, 1)
    stage_c(n_steps - 2, 0)
    stage_c(n_steps - 1, 1)
    finish(n_steps - 2)
    finish(n_steps - 1)


def _diff_attention(qkv, km, vm, dblk, bm0, lamv, gain, batch, seq, lambda_init):
    nq = seq // TQ
    steps = [(qi, t) for qi in range(nq) for t in range(qi + 2)]
    assert len(steps) % 2 == 0
    qi_tab = jnp.asarray([s[0] for s in steps], jnp.int32)
    t_tab = jnp.asarray([s[1] for s in steps], jnp.int32)
    kern = functools.partial(_diff_kernel, lambda_init=lambda_init, n_steps=len(steps))
    return pl.pallas_call(
        kern,
        out_shape=jax.ShapeDtypeStruct((batch * seq, N_DIFF_HEADS * LANES), BF16),
        grid_spec=pltpu.PrefetchScalarGridSpec(
            num_scalar_prefetch=2,
            grid=(batch, N_DIFF_HEADS),
            in_specs=[
                pl.BlockSpec((seq, LANES), lambda b, h, *_: (b, C_DQ // LANES + h)),
                pl.BlockSpec((seq, LANES), lambda b, h, *_: (b, C_DK // LANES + h)),
                pl.BlockSpec((seq, LANES), lambda b, h, *_: (b, C_DV // LANES + h)),
                pl.BlockSpec((TQ, LANES), lambda b, h, *_: (0, h)),
                pl.BlockSpec((TQ, LANES), lambda b, h, *_: (0, h)),
                pl.BlockSpec((1, 2, BLOCK, BLOCK), lambda b, h, *_: (h, 0, 0, 0)),
                pl.BlockSpec((1, LANES, TQ), lambda b, h, *_: (h, 0, 0)),
                pl.BlockSpec(lamv.shape, lambda b, h, *_: (0, 0)),
                pl.BlockSpec((LANES, LANES), lambda b, h, *_: (0, 0)),
            ],
            out_specs=pl.BlockSpec((seq, LANES), lambda b, h, *_: (b, h)),
            scratch_shapes=[
                pltpu.VMEM((5, TQ, TQ), F32),
                pltpu.VMEM((nq, 2 * TQ, LANES), BF16),
                pltpu.VMEM((nq + 1, LANES + ONES_ROWS, TQ), BF16),
                pltpu.VMEM((2, TQ, 2 * TQ), F32),
                pltpu.VMEM((2, TQ, 2 * TQ), BF16),
                pltpu.VMEM((2, 8, 2 * TQ), F32),
                pltpu.VMEM((8, 2 * TQ), F32),
                pltpu.VMEM((2, LANES + ONES_ROWS, 2 * TQ), F32),
            ],
        ),
        compiler_params=_cparams(("parallel", "parallel")),
        name="diff_attention",
    )(qi_tab, t_tab, qkv, qkv, qkv, km, vm, jnp.swapaxes(dblk, -1, -2), jnp.swapaxes(bm0, -1, -2), lamv,
      jnp.broadcast_to(gain.reshape(LANES, 1), (LANES, LANES)))


def _swa_kernel(sink_ref, q_ref, kd_ref, vz_ref, kdm_ref, vzm_ref, bt_ref, o_ref):
    ci = pl.program_id(1)
    nblk = q_ref.shape[0] // BLOCK
    lane = lax.broadcasted_iota(jnp.int32, (BLOCK, LANES), 1)
    row2 = lax.broadcasted_iota(jnp.int32, (2 * BLOCK, 1), 0)

    def block_body(n, carry):
        gblk = ci * nblk + n
        first = jnp.where(gblk == 0, 0, 1)
        r_own = pl.multiple_of(gblk * BLOCK, BLOCK)
        r_prev = pl.multiple_of(jnp.maximum(gblk - 1, 0) * BLOCK, BLOCK)
        r_q = pl.multiple_of(n * BLOCK, BLOCK)
        for g in range(N_SWA_KV):
            kcat = jnp.concatenate([kdm_ref[:, g * LANES:(g + 1) * LANES],
                                    kd_ref[pl.ds(r_prev, BLOCK), g * LANES:(g + 1) * LANES],
                                    kd_ref[pl.ds(r_own, BLOCK), g * LANES:(g + 1) * LANES]], axis=0)
            for u in range(2):
                cs = slice((2 * g + u) * LANES, (2 * g + u + 1) * LANES)
                h0 = 4 * g + 2 * u
                qp = q_ref[pl.ds(r_q, BLOCK), cs]
                zero = jnp.zeros_like(qp)
                qs = jnp.concatenate([jnp.where(lane < HEAD_DIM, qp, zero),
                                      jnp.where(lane >= HEAD_DIM, qp, zero)], axis=0)
                s = lax.dot_general(qs, kcat, (((1,), (1,)), ((), ())), preferred_element_type=F32)
                s = s + jnp.concatenate([bt_ref[first, h0], bt_ref[first, h0 + 1]], axis=0)
                sink = jnp.where(row2 < BLOCK, sink_ref[h0:h0 + 1, 0:1], sink_ref[h0 + 1:h0 + 2, 0:1])
                m = jnp.maximum(jnp.max(s, axis=1, keepdims=True), sink)
                p = jnp.exp(s - m)
                den = jnp.sum(p, axis=1, keepdims=True) + jnp.exp(sink - m)
                pb = p.astype(BF16)
                inv = 1.0 / den
                o = None
                for par in range(2):
                    vs = slice((2 * g + par) * LANES, (2 * g + par + 1) * LANES)
                    vcat = jnp.concatenate([vzm_ref[:, vs],
                                            vz_ref[pl.ds(r_prev, BLOCK), vs],
                                            vz_ref[pl.ds(r_own, BLOCK), vs]], axis=0)
                    rs = slice(par * BLOCK, (par + 1) * BLOCK)
                    t = jnp.dot(pb[rs], vcat, preferred_element_type=F32) * inv[rs]
                    o = t if o is None else o + t
                o_ref[pl.ds(r_q, BLOCK), cs] = o.astype(BF16)
        return carry

    lax.fori_loop(0, nblk, block_body, 0)


def _swa_attention(sinks, qkv, kdm, vzm, bt, batch, seq):
    nc = seq // CQ
    sinkv = jnp.broadcast_to(sinks.reshape(N_SWA_HEADS, 1), (N_SWA_HEADS, LANES))
    return pl.pallas_call(
        _swa_kernel,
        out_shape=jax.ShapeDtypeStruct((batch * seq, N_SWA_HEADS * HEAD_DIM), BF16),
        grid=(batch, nc),
        in_specs=[
            pl.BlockSpec(sinkv.shape, lambda b, c: (0, 0)),
            pl.BlockSpec((CQ, 512), lambda b, c: (b * nc + c, C_SQ // 512)),
            pl.BlockSpec((seq, 256), lambda b, c: (b, C_KD // 256)),
            pl.BlockSpec((seq, 512), lambda b, c: (b, C_VZ // 512)),
            pl.BlockSpec(kdm.shape, lambda b, c: (0, 0)),
            pl.BlockSpec(vzm.shape, lambda b, c: (0, 0)),
            pl.BlockSpec(bt.shape, lambda b, c: (0, 0, 0, 0)),
        ],
        out_specs=pl.BlockSpec((CQ, 512), lambda b, c: (b * nc + c, 0)),
        compiler_params=_cparams(("parallel", "arbitrary")),
        name="swa_attention",
    )(sinkv, qkv, qkv, qkv, kdm, vzm, bt)


def _outproj_kernel(x_ref, md_ref, ms_ref, wo_ref, g2_ref, wr_ref, br_ref,
                    h_ref, hb_ref, rt_ref, ti_ref, cnt_ref, c_ref):
    i = pl.program_id(0)
    tm = x_ref.shape[0]
    half = md_ref.shape[1]

    @pl.when(i == 0)
    def _init():
        c_ref[...] = jnp.zeros(c_ref.shape, F32)

    h = (x_ref[...]
         + jnp.dot(md_ref[...], wo_ref[:half, :], preferred_element_type=F32)
         + jnp.dot(ms_ref[...], wo_ref[half:, :], preferred_element_type=F32))
    h_ref[...] = h
    hn = h * lax.rsqrt(jnp.mean(h * h, axis=-1, keepdims=True) + EPS) * g2_ref[...]
    hb = hn.astype(BF16)
    hb_ref[...] = hb

    lg = jnp.dot(hb, wr_ref[...], preferred_element_type=F32) + br_ref[...]
    lane_i = lax.broadcasted_iota(jnp.int32, lg.shape, 1)
    lane = lane_i.astype(F32)
    big = float(4 * LANES)
    is_g = (lane_i >= N_EXPERTS) & (lane_i < N_EXPERTS + N_GROUPS)
    glm = jnp.where(is_g, lg, -jnp.inf)
    gmax = jnp.max(glm, axis=1, keepdims=True)
    gidx = jnp.min(jnp.where(glm == gmax, lane, big), axis=1, keepdims=True) - N_EXPERTS
    gsum = jnp.sum(jnp.where(is_g, jnp.exp(lg - gmax), 0.0), axis=1, keepdims=True)
    g_w = 1.0 / gsum
    lane_grp = (lane_i >> 3).astype(F32)
    in_grp = (lane_i < N_EXPERTS) & (lane_grp == gidx)
    el = jnp.where(in_grp, lg, -jnp.inf)
    t1 = jnp.max(el, axis=1, keepdims=True)
    j1 = jnp.min(jnp.where(el == t1, lane, big), axis=1, keepdims=True)
    el2 = jnp.where(lane == j1, -jnp.inf, el)
    t2 = jnp.max(el2, axis=1, keepdims=True)
    j2 = jnp.min(jnp.where(el2 == t2, lane, big), axis=1, keepdims=True)
    e2 = jnp.exp(t2 - t1)
    den = 1.0 + e2
    gate1 = g_w / den
    gate2 = g_w * e2 / den

    o1 = lane == j1
    o2 = lane == j2
    onehot = jnp.where(o1 | o2, 1.0, 0.0).astype(BF16)
    rr = lax.broadcasted_iota(jnp.int32, (tm, tm), 0)
    cc = lax.broadcasted_iota(jnp.int32, (tm, tm), 1)
    lower = jnp.where(rr > cc, 1.0, 0.0).astype(BF16)
    pfx = jnp.dot(lower, onehot, preferred_element_type=F32)
    cnt_tile = jnp.sum(onehot.astype(F32), axis=0, keepdims=True)
    nch = jnp.floor((cnt_tile + (CHUNK - 1)) * (1.0 / CHUNK))
    er = lax.broadcasted_iota(jnp.int32, (LANES, LANES), 0)
    ec = lax.broadcasted_iota(jnp.int32, (LANES, LANES), 1)
    before = jnp.where(er < ec, 1.0, 0.0).astype(BF16)
    cbase = CHUNK * jnp.dot(jnp.broadcast_to(nch, (8, LANES)).astype(BF16), before,
                            preferred_element_type=F32)[0:1]
    at = pfx + cbase
    pos1 = jnp.sum(jnp.where(o1, at, 0.0), axis=1, keepdims=True)
    pos2 = jnp.sum(jnp.where(o2, at, 0.0), axis=1, keepdims=True)
    rt_ref[...] = jnp.where(lane_i == 0, gate1,
                            jnp.where(lane_i == 1, gate2,
                                      jnp.where(lane_i == 2, pos1,
                                                jnp.where(lane_i == 3, pos2, 0.0))))
    c_old = c_ref[...]
    c_new = c_old + jnp.floor((cnt_tile + (ROW_ALIGN - 1)) * (1.0 / ROW_ALIGN)) * ROW_ALIGN
    c_ref[...] = c_new
    row8 = lax.broadcasted_iota(jnp.int32, (8, LANES), 0)
    ti_ref[...] = jnp.where(row8 == 0, cnt_tile, jnp.where(row8 == 1, c_old, 0.0))

    @pl.when(i == pl.num_programs(0) - 1)
    def _fin():
        cnt_ref[...] = c_new


def _outproj(x2, mixd, mixs, wo, g2, wr, br):
    n, d = x2.shape
    return pl.pallas_call(
        _outproj_kernel,
        out_shape=(jax.ShapeDtypeStruct((n, d), F32),
                   jax.ShapeDtypeStruct((n, d), BF16),
                   jax.ShapeDtypeStruct((n, LANES), F32),
                   jax.ShapeDtypeStruct((n // TM * 8, LANES), F32),
                   jax.ShapeDtypeStruct((8, LANES), F32)),
        grid=(n // TM,),
        in_specs=[
            pl.BlockSpec((TM, d), lambda i: (i, 0)),
            pl.BlockSpec((TM, mixd.shape[1]), lambda i: (i, 0)),
            pl.BlockSpec((TM, mixs.shape[1]), lambda i: (i, 0)),
            pl.BlockSpec(wo.shape, lambda i: (0, 0)),
            pl.BlockSpec(g2.shape, lambda i: (0, 0)),
            pl.BlockSpec(wr.shape, lambda i: (0, 0)),
            pl.BlockSpec(br.shape, lambda i: (0, 0)),
        ],
        out_specs=(pl.BlockSpec((TM, d), lambda i: (i, 0)),
                   pl.BlockSpec((TM, d), lambda i: (i, 0)),
                   pl.BlockSpec((TM, LANES), lambda i: (i, 0)),
                   pl.BlockSpec((8, LANES), lambda i: (i, 0)),
                   pl.BlockSpec((8, LANES), lambda i: (0, 0))),
        scratch_shapes=[pltpu.VMEM((8, LANES), F32)],
        compiler_params=_cparams(("arbitrary",)),
        name="outproj_router",
    )(x2, mixd, mixs, wo, g2, wr, br)


def _for_each_chunk(runs_ref, fn):
    def per_expert(e, sorted_row):
        start = runs_ref[0, 0, e]
        nch = runs_ref[0, 0, N_EXPERTS + e]

        def per_chunk(c, carry):
            fn(pl.multiple_of(start + c * CHUNK, ROW_ALIGN), pl.multiple_of(sorted_row + c * CHUNK, CHUNK))
            return carry
        lax.fori_loop(0, nch, per_chunk, 0)
        return sorted_row + nch * CHUNK
    lax.fori_loop(0, N_EXPERTS, per_expert, 0)


def _dispatch_kernel(cur_ref, prv_ref, hb_ref, rt_ref, xs_in_ref, xs_ref, sbuf, sem):
    del xs_in_ref
    i = pl.program_id(0)
    nt = pl.num_programs(0)
    slot = i % 2
    tm, d = hb_ref.shape

    pos_t = jnp.transpose(rt_ref[...])
    srow = lax.broadcasted_iota(jnp.int32, (SROWS, tm), 0).astype(F32)
    sel = jnp.where(srow == pos_t[2:3, :], 1.0, jnp.where(srow == pos_t[3:4, :], 1.0, 0.0)).astype(BF16)
    srt = jnp.dot(sel, hb_ref[...], preferred_element_type=F32)
    bits = pltpu.bitcast(srt, jnp.uint32)
    sbuf[slot] = (bits[:, d // 2:] & jnp.uint32(0xFFFF0000)) | (bits[:, :d // 2] >> 16)

    def chunk_copy(run_row, sorted_row, sl):
        return pltpu.make_async_copy(sbuf.at[sl, pl.ds(sorted_row, CHUNK)], xs_ref.at[pl.ds(run_row, CHUNK)],
                                     sem.at[sl])

    def drain(runs_ref, sl):
        def body(c, carry):
            chunk_copy(0, 0, sl).wait()
            return carry
        lax.fori_loop(0, runs_ref[0, 0, 2 * N_EXPERTS], body, 0)

    @pl.when(i > 0)
    def _():
        drain(prv_ref, 1 - slot)

    _for_each_chunk(cur_ref, lambda run_row, sorted_row: chunk_copy(run_row, sorted_row, slot).start())

    @pl.when(i == nt - 1)
    def _():
        drain(cur_ref, slot)


def _dispatch(runs, hb, rt, xs0):
    n, d = hb.shape
    return pl.pallas_call(
        _dispatch_kernel,
        out_shape=jax.ShapeDtypeStruct(xs0.shape, xs0.dtype),
        grid=(n // TM,),
        in_specs=[
            pl.BlockSpec((1, 1, LANES), lambda i: (i, 0, 0), memory_space=pltpu.SMEM),
            pl.BlockSpec((1, 1, LANES), lambda i: (jnp.maximum(i - 1, 0), 0, 0), memory_space=pltpu.SMEM),
            pl.BlockSpec((TM, d), lambda i: (i, 0)),
            pl.BlockSpec((TM, LANES), lambda i: (i, 0)),
            pl.BlockSpec(memory_space=pl.ANY),
        ],
        out_specs=pl.BlockSpec(memory_space=pl.ANY),
        scratch_shapes=[pltpu.VMEM((2, SROWS, d // 2), jnp.uint32), pltpu.SemaphoreType.DMA((2,))],
        input_output_aliases={4: 0},
        compiler_params=_cparams(("arbitrary",)),
        name="dispatch",
    )(runs, runs, hb, rt, xs0)


def _experts_kernel(be_ref, na_ref, xs_ref, wg_ref, wu_ref, wd_ref, ys_ref, wgb, wub, wdb):
    b = pl.program_id(0)

    @pl.when(b < na_ref[0])
    def _():
        e = be_ref[b]
        changed = jnp.logical_or(b == 0, be_ref[jnp.maximum(b - 1, 0)] != e)

        @pl.when(changed)
        def _cast():
            wgb[...] = wg_ref[0].astype(BF16)
            wub[...] = wu_ref[0].astype(BF16)
            wdb[...] = wd_ref[0].astype(BF16)

        w = xs_ref[...]
        x_lo = pltpu.bitcast(w << 16, F32).astype(BF16)
        x_hi = pltpu.bitcast(w & jnp.uint32(0xFFFF0000), F32).astype(BF16)
        dh = w.shape[1]
        g = (jnp.dot(x_lo, wgb[:dh, :], preferred_element_type=F32)
             + jnp.dot(x_hi, wgb[dh:, :], preferred_element_type=F32))
        u = (jnp.dot(x_lo, wub[:dh, :], preferred_element_type=F32)
             + jnp.dot(x_hi, wub[dh:, :], preferred_element_type=F32))
        hdn = g * (1.0 / (1.0 + jnp.exp(-g))) * u
        ys_ref[...] = jnp.dot(hdn.astype(BF16), wdb[...], preferred_element_type=F32)

    @pl.when(b >= na_ref[0])
    def _():
        ys_ref[...] = jnp.zeros(ys_ref.shape, F32)


def _experts(blk_e, n_act, xs, w_gate, w_up, w_down):
    p, dh = xs.shape
    d = 2 * dh
    de = w_gate.shape[2]

    def row_map(b, be, na):
        return (jnp.minimum(b, na[0] - 1), 0)

    def w_map(b, be, na):
        return (be[jnp.minimum(b, na[0] - 1)], 0, 0)

    return pl.pallas_call(
        _experts_kernel,
        out_shape=jax.ShapeDtypeStruct((p, d), F32),
        grid_spec=pltpu.PrefetchScalarGridSpec(
            num_scalar_prefetch=2,
            grid=(p // EB,),
            in_specs=[
                pl.BlockSpec((EB, dh), row_map),
                pl.BlockSpec((1, d, de), w_map),
                pl.BlockSpec((1, d, de), w_map),
                pl.BlockSpec((1, de, d), w_map),
            ],
            out_specs=pl.BlockSpec((EB, d), lambda b, be, na: (b, 0)),
            scratch_shapes=[pltpu.VMEM((d, de), BF16), pltpu.VMEM((d, de), BF16), pltpu.VMEM((de, d), BF16)],
        ),
        compiler_params=_cparams(("arbitrary",)),
        name="experts",
    )(blk_e, n_act, xs, w_gate, w_up, w_down)


def _combine_kernel(cur_ref, nxt_ref, ys_ref, h_ref, rt_ref, o_ref, ybuf, sem):
    i = pl.program_id(0)
    nt = pl.num_programs(0)
    slot = i % 2
    tm = h_ref.shape[0]

    def chunk_copy(run_row, sorted_row, sl):
        return pltpu.make_async_copy(ys_ref.at[pl.ds(run_row, CHUNK)], ybuf.at[sl, pl.ds(sorted_row, CHUNK)],
                                     sem.at[sl])

    @pl.when(i == 0)
    def _():
        ybuf[...] = jnp.zeros(ybuf.shape, F32)
        _for_each_chunk(cur_ref, lambda run_row, sorted_row: chunk_copy(run_row, sorted_row, 0).start())

    @pl.when(i + 1 < nt)
    def _():
        _for_each_chunk(nxt_ref, lambda run_row, sorted_row: chunk_copy(run_row, sorted_row, 1 - slot).start())

    def drain(c, carry):
        chunk_copy(0, 0, slot).wait()
        return carry
    lax.fori_loop(0, cur_ref[0, 0, 2 * N_EXPERTS], drain, 0)

    rt = rt_ref[...]
    yb = ybuf[slot].astype(BF16)
    col = lax.broadcasted_iota(jnp.int32, (tm, SROWS), 1).astype(F32)
    w1 = jnp.where(col == rt[:, 2:3], 1.0, 0.0).astype(BF16)
    w2 = jnp.where(col == rt[:, 3:4], 1.0, 0.0).astype(BF16)
    o_ref[...] = (h_ref[...]
                  + rt[:, 0:1] * jnp.dot(w1, yb, preferred_element_type=F32)
                  + rt[:, 1:2] * jnp.dot(w2, yb, preferred_element_type=F32))


def _combine(runs, ys, h1, rt):
    n, d = h1.shape
    nt = n // TM
    return pl.pallas_call(
        _combine_kernel,
        out_shape=jax.ShapeDtypeStruct((n, d), F32),
        grid=(nt,),
        in_specs=[
            pl.BlockSpec((1, 1, LANES), lambda i: (i, 0, 0), memory_space=pltpu.SMEM),
            pl.BlockSpec((1, 1, LANES), lambda i: (jnp.minimum(i + 1, nt - 1), 0, 0), memory_space=pltpu.SMEM),
            pl.BlockSpec(memory_space=pl.ANY),
            pl.BlockSpec((TM, d), lambda i: (i, 0)),
            pl.BlockSpec((TM, LANES), lambda i: (i, 0)),
        ],
        out_specs=pl.BlockSpec((TM, d), lambda i: (i, 0)),
        scratch_shapes=[pltpu.VMEM((2, SROWS, d), F32), pltpu.SemaphoreType.DMA((2,))],
        compiler_params=_cparams(("arbitrary",)),
        name="combine",
    )(runs, runs, ys, h1, rt)


def _augmented_w_in(w_in):
    sk0 = C_SQ + N_SWA_HEADS * HEAD_DIM
    sv0 = sk0 + N_SWA_KV * HEAD_DIM
    z = jnp.zeros((w_in.shape[0], HEAD_DIM), w_in.dtype)
    vz, kd = [], []
    for g in range(N_SWA_KV):
        k = w_in[:, sk0 + g * HEAD_DIM: sk0 + (g + 1) * HEAD_DIM]
        v = w_in[:, sv0 + g * HEAD_DIM: sv0 + (g + 1) * HEAD_DIM]
        kd += [k, k]
        vz += [v, z, z, v]
    return jnp.concatenate([w_in[:, :C_VZ]] + vz + kd, axis=1)


def kernel(x, meta_tokens, rel_bias, norm1_gain, w_in, diff_q_gain, diff_k_gain, lam_q1, lam_k1, lam_q2, lam_k2, diff_subln_gain, swa_q_gain, swa_k_gain, swa_sinks, w_out, norm2_gain, w_group, b_group, w_router, b_router, w_gate, w_up, w_down):
    batch, seq, d = x.shape
    depth = w_in.shape[0]
    n = batch * seq
    assert seq % CQ == 0 and seq % TQ == 0 and n % TM == 0 and d == 1024
    assert meta_tokens.shape[0] == N_META
    assert depth == 1, "the meta-token rows of the residual stream are not carried across layers"

    h = x.reshape(n, d)
    dblk, bm0, bt = _bias_tables(rel_bias, TQ)
    scale = HEAD_DIM ** -0.5
    bd = jnp.asarray(np.kron(np.eye(MXU_DIM // HEAD_DIM), np.full((HEAD_DIM, HEAD_DIM), 1.0 / HEAD_DIM)), BF16)
    ones = jnp.ones((HEAD_DIM,), F32)
    lower_pad = N_EXPERTS + N_GROUPS

    for layer in range(depth):
        lambda_init = 0.8 - 0.6 * math.exp(-0.3 * layer)
        w_aug = _augmented_w_in(w_in[layer]).astype(BF16)
        gain = jnp.concatenate([
            jnp.tile(diff_q_gain[layer] * scale, 2 * N_DIFF_HEADS),
            jnp.tile(diff_k_gain[layer], 2 * N_DIFF_HEADS),
            jnp.tile(ones, 2 * N_DIFF_HEADS),
            jnp.tile(swa_q_gain[layer] * scale, N_SWA_HEADS),
            jnp.tile(ones, 4 * N_SWA_KV),
            jnp.tile(swa_k_gain[layer], 2 * N_SWA_KV)]).reshape(1, C_END).astype(F32)
        g1 = norm1_gain[layer].reshape(1, d).astype(F32)

        qkv = _proj(h, g1, w_aug, bd, gain, TM)
        qkv_meta = _proj(meta_tokens.astype(F32), g1, w_aug, bd, gain, N_META)
        meta_pad = jnp.pad(qkv_meta, ((0, TQ - N_META), (0, 0)))

        lamv = jnp.pad(jnp.stack([lam_q1[layer], lam_k1[layer], lam_q2[layer], lam_k2[layer]]).astype(F32),
                       ((0, 4), (0, LANES - HEAD_DIM)))
        mixd = _diff_attention(qkv, meta_pad[:, C_DK:C_DV], meta_pad[:, C_DV:C_SQ], dblk, bm0, lamv,
                               diff_subln_gain[layer].reshape(1, LANES).astype(F32), batch, seq, lambda_init)
        mixs = _swa_attention(swa_sinks[layer].astype(F32), qkv, meta_pad[:BLOCK, C_KD:C_END],
                              meta_pad[:BLOCK, C_VZ:C_KD], bt, batch, seq)

        wr = jnp.pad(jnp.concatenate([w_router[layer], w_group[layer]], axis=1),
                     ((0, 0), (0, LANES - lower_pad))).astype(BF16)
        br = jnp.pad(jnp.concatenate([b_router[layer], b_group[layer]]), (0, LANES - lower_pad)).reshape(1, LANES)
        h1, hb, rt, tinfo, cnt = _outproj(h, mixd, mixs, w_out[layer].astype(BF16),
                                          norm2_gain[layer].reshape(1, d).astype(F32), wr, br.astype(F32))

        nt = n // TM
        counts = cnt[0, :N_EXPERTS].astype(jnp.int32)
        nblk_e = jnp.where(counts > 0, (counts + CHUNK - 1 + EB - 1) // EB, 0)
        blk_end = jnp.cumsum(nblk_e)
        pstart = ((blk_end - nblk_e) * EB).astype(jnp.int32)
        n_blocks = -(-(2 * n + nt * N_EXPERTS * (ROW_ALIGN - 1) + N_EXPERTS * (EB - 1 + CHUNK - 1)) // EB)
        blk_e = jnp.minimum(jnp.sum(blk_end[None, :] <= jnp.arange(n_blocks)[:, None], axis=1),
                            N_EXPERTS - 1).astype(jnp.int32)
        n_act = blk_end[-1:].astype(jnp.int32)
        ti = tinfo.reshape(nt, 8, LANES)
        run_len = ti[:, 0, :N_EXPERTS].astype(jnp.int32)
        run_start = pstart[None, :] + ti[:, 1, :N_EXPERTS].astype(jnp.int32)
        run_nch = (run_len + CHUNK - 1) // CHUNK
        runs = jnp.concatenate([run_start, run_nch, jnp.sum(run_nch, axis=1, keepdims=True),
                                jnp.zeros((nt, LANES - 2 * N_EXPERTS - 1), jnp.int32)],
                               axis=1).reshape(nt, 1, LANES)

        xs = _dispatch(runs, hb, rt, jnp.zeros((n_blocks * EB, d // 2), jnp.uint32))
        ys = _experts(blk_e, n_act, xs, w_gate[layer], w_up[layer], w_down[layer])
        h = _combine(runs, ys, h1, rt)
    return h.reshape(batch, seq, d)
```

```python
import functools
import math

import numpy as np
import jax
import jax.numpy as jnp
from jax import lax
from jax.experimental import pallas as pl
from jax.experimental.pallas import tpu as pltpu

F32 = jnp.float32
BF16 = jnp.bfloat16

HEAD_DIM = 64
N_DIFF_HEADS = 4
N_SWA_HEADS = 8
N_SWA_KV = 2
BLOCK = 128
N_META = 16
N_BUCKETS = 32
MAX_DISTANCE = 128
N_GROUPS = 4
EXPERTS_PER_GROUP = 8
N_EXPERTS = N_GROUPS * EXPERTS_PER_GROUP
D_EXPERT = 512
EPS = 1e-6
NEG = -1e30

LANES = 128
MXU_DIM = 256
VMEM_LIMIT = 48 * 1024 * 1024

TM = 256
TQ = 256
ONES_ROWS = 16
CQ = 512
EB = 256
ROW_ALIGN = 8
CHUNK = 16
SROWS = -(-(2 * TM + N_EXPERTS * (ROW_ALIGN - 1)) // MXU_DIM) * MXU_DIM

C_DQ, C_DK, C_DV, C_SQ, C_VZ, C_KD, C_END = 0, 512, 1024, 1536, 2048, 2560, 2816
NORM_GROUPS = (0, 1, 2, 3, 6, 7, 10)


def _cparams(sem):
    return pltpu.CompilerParams(dimension_semantics=sem, vmem_limit_bytes=VMEM_LIMIT)


def _t5_bucket_np(dist):
    n = np.maximum(dist, 0)
    max_exact = N_BUCKETS // 2
    nf = np.maximum(n, 1).astype(np.float32)
    large = max_exact + (np.log(nf / np.float32(max_exact)) / np.float32(math.log(MAX_DISTANCE / max_exact))
                         * np.float32(N_BUCKETS - max_exact)).astype(np.int32)
    large = np.minimum(large, N_BUCKETS - 1)
    return np.where(n < max_exact, n, large)


def _bias_tables(rel_bias, tq):
    nd = 2 * BLOCK
    buckets = _t5_bucket_np(np.arange(nd))
    assert (buckets[MAX_DISTANCE:] == N_BUCKETS - 1).all()
    rb = rel_bias.astype(F32)
    r = np.arange(BLOCK)[:, None]
    c = np.arange(BLOCK)[None, :]
    d_own = r - c
    d_prev = BLOCK + r - c
    far = rb[N_BUCKETS - 1]

    def take(dist):
        idx = jnp.asarray(buckets[np.clip(dist, 0, nd - 1)], jnp.int32)[None]
        out = jnp.zeros((rb.shape[1],) + dist.shape, F32)
        for b in range(N_BUCKETS):
            out = jnp.where(idx == b, rb[b].reshape((-1,) + (1,) * dist.ndim), out)
        return out

    hd = slice(0, N_DIFF_HEADS)
    far_d = far[hd][:, None, None]
    d0 = jnp.where(d_own[None] >= 0, take(d_own)[hd] - far_d, NEG)
    d1 = take(d_prev)[hd] - far_d
    dblk = jnp.stack([d0, d1], axis=1)
    rq = np.arange(tq)[:, None]
    cm = np.arange(LANES)[None, :]
    d_meta = N_META + rq - cm
    bm0 = jnp.where((cm < N_META)[None], take(d_meta)[hd] - far_d, NEG)

    hs = slice(N_DIFF_HEADS, N_DIFF_HEADS + N_SWA_HEADS)
    far_s = far[hs][:, None, None]
    d_meta_s = N_META + r - cm
    meta_first = jnp.where((cm < N_META)[None], take(d_meta_s)[hs], NEG)
    meta_rest = jnp.where((cm < N_META)[None], jnp.broadcast_to(far_s, (N_SWA_HEADS, BLOCK, LANES)), NEG)
    prev_rest = jnp.where((c > r)[None], take(d_prev)[hs], NEG)
    prev_first = jnp.full((N_SWA_HEADS, BLOCK, BLOCK), NEG, F32)
    own = jnp.where((d_own >= 0)[None], take(d_own)[hs], NEG)
    bt = jnp.stack([jnp.concatenate([meta_first, prev_first, own], axis=-1),
                    jnp.concatenate([meta_rest, prev_rest, own], axis=-1)], axis=0)
    return dblk.astype(F32), bm0.astype(F32), bt.astype(F32)


def _proj_kernel(x_ref, g1_ref, w_ref, bd_ref, gain_ref, o_ref):
    x = x_ref[...]
    a = x * lax.rsqrt(jnp.mean(x * x, axis=-1, keepdims=True) + EPS) * g1_ref[...]
    p = jnp.dot(a.astype(BF16), w_ref[...], preferred_element_type=F32)
    bd = bd_ref[...]
    for j in range(C_END // MXU_DIM):
        sl = slice(j * MXU_DIM, (j + 1) * MXU_DIM)
        pj = p[:, sl]
        if j in NORM_GROUPS:
            ms = jnp.dot((pj * pj).astype(BF16), bd, preferred_element_type=F32)
            pj = pj * lax.rsqrt(ms + EPS) * gain_ref[:, sl]
        o_ref[:, sl] = pj.astype(BF16)


def _proj(x2, g1, w_aug, bd, gain, tm):
    n = x2.shape[0]
    return pl.pallas_call(
        _proj_kernel,
        out_shape=jax.ShapeDtypeStruct((n, C_END), BF16),
        grid=(n // tm,),
        in_specs=[
            pl.BlockSpec((tm, x2.shape[1]), lambda i: (i, 0)),
            pl.BlockSpec(g1.shape, lambda i: (0, 0)),
            pl.BlockSpec(w_aug.shape, lambda i: (0, 0)),
            pl.BlockSpec(bd.shape, lambda i: (0, 0)),
            pl.BlockSpec(gain.shape, lambda i: (0, 0)),
        ],
        out_specs=pl.BlockSpec((tm, C_END), lambda i: (i, 0)),
        compiler_params=_cparams(("parallel",)),
        name="proj",
    )(x2, g1, w_aug, bd, gain)


def _diff_kernel(qi_tab, t_tab, q_ref, k_ref, v_ref, km_ref, vm_ref, d_ref, bm0_ref, lamv_ref, gain_ref, o_ref,
                 bias_ref, qs_ref, vt_ref, s_buf, p_buf, a_buf, m_ref, acc_ref, *, lambda_init, n_steps):
    tq = TQ
    nq = q_ref.shape[0] // tq
    nb = tq // BLOCK
    BIAS_NONE, BIAS_LEFT, BIAS_DIAG, BIAS_META, BIAS_META0 = 0, 1, 2, 3, 4

    d0 = d_ref[0, 0]
    d1 = d_ref[0, 1]
    zeros = jnp.zeros((BLOCK, BLOCK), F32)
    bias_ref[BIAS_NONE] = jnp.zeros((tq, tq), F32)
    for a in range(nb):
        for b in range(nb):
            rs, cs = slice(a * BLOCK, (a + 1) * BLOCK), slice(b * BLOCK, (b + 1) * BLOCK)
            if a == b:
                blk = d0
            elif b == a + 1:
                blk = d1
            elif b > a:
                blk = zeros
            else:
                blk = jnp.full((BLOCK, BLOCK), NEG, F32)
            bias_ref[BIAS_DIAG, rs, cs] = blk
            bias_ref[BIAS_LEFT, rs, cs] = d1 if (b == 0 and a == nb - 1) else zeros
    row_m = lax.broadcasted_iota(jnp.int32, (tq, tq), 0)
    bias_ref[BIAS_META] = jnp.where(row_m < N_META, 0.0, NEG).astype(F32)
    bias_ref[BIAS_META0, :LANES, :] = bm0_ref[0]
    bias_ref[BIAS_META0, LANES:, :] = jnp.full((tq - LANES, tq), NEG, F32)

    lane = lax.broadcasted_iota(jnp.int32, (tq, LANES), 1)
    for i in range(nq):
        rows = slice(i * tq, (i + 1) * tq)
        q = q_ref[rows, :]
        zero = jnp.zeros_like(q)
        qs_ref[i] = jnp.concatenate([jnp.where(lane < HEAD_DIM, q, zero),
                                     jnp.where(lane >= HEAD_DIM, q, zero)], axis=0)
        vt_ref[i, :LANES, :] = jnp.transpose(v_ref[rows, :].astype(F32)).astype(BF16)
    vt_ref[nq, :LANES, :] = jnp.transpose(vm_ref[...].astype(F32)).astype(BF16)
    vt_ref[:, LANES:, :] = jnp.ones((nq + 1, ONES_ROWS, tq), BF16)
    acc_ref[...] = jnp.zeros(acc_ref.shape, F32)
    m_ref[...] = jnp.full(m_ref.shape, NEG, F32)
    lv = lamv_ref[...]
    lam = (jnp.exp(jnp.sum(lv[0:1] * lv[1:2], axis=-1, keepdims=True))
           - jnp.exp(jnp.sum(lv[2:3] * lv[3:4], axis=-1, keepdims=True)) + lambda_init)

    def seq_row(t):
        return pl.multiple_of(jnp.maximum(t - 1, 0) * tq, tq)

    def stage_a(n, slot):
        qi, t = qi_tab[n], t_tab[n]
        kt = jnp.where(t == 0, km_ref[...], k_ref[pl.ds(seq_row(t), tq), :])
        s = lax.dot_general(kt, qs_ref[qi], (((1,), (1,)), ((), ())), preferred_element_type=F32)
        which = jnp.where(t == 0, jnp.where(qi == 0, BIAS_META0, BIAS_META),
                          jnp.where(t == qi + 1, BIAS_DIAG, jnp.where(t == qi, BIAS_LEFT, BIAS_NONE)))
        s_buf[slot] = s + jnp.tile(bias_ref[which], (1, 2))

    def stage_b(n, slot):
        s = s_buf[slot]
        m_prev = jnp.where(t_tab[n] == 0, NEG, m_ref[...])
        m_new = jnp.maximum(m_prev, jnp.max(s, axis=0, keepdims=True))
        a_buf[slot] = jnp.exp(m_prev - m_new)
        p_buf[slot] = jnp.exp(s - m_new[0:1]).astype(BF16)
        m_ref[...] = m_new

    def stage_c(n, slot):
        qi, t = qi_tab[n], t_tab[n]
        par = qi % 2
        vt = vt_ref[jnp.where(t == 0, nq, t - 1)]
        pv = jnp.dot(vt, p_buf[slot], preferred_element_type=F32)
        acc_ref[par] = a_buf[slot][0:1] * acc_ref[par] + pv

    def finish(n):
        qi, t = qi_tab[n], t_tab[n]

        @pl.when(t == qi + 1)
        def _():
            acc = acc_ref[qi % 2]
            o = acc[:LANES] / acc[LANES:LANES + 1]
            d = o[:, :tq] - lam * o[:, tq:]
            y = d * lax.rsqrt(jnp.mean(d * d, axis=0, keepdims=True) + EPS) * jnp.tile(gain_ref[...], (1, tq // LANES))
            y = jnp.transpose(y * (1.0 - lambda_init))
            o_ref[pl.ds(pl.multiple_of(qi * tq, tq), tq), :] = y.astype(BF16)

    stage_a(0, 0)
    stage_a(1, 1)
    stage_b(0, 0)

    def steps(n, count):
        for j in range(count):
            stage_a(n + j + 2, j % 2)
            stage_b(n + j + 1, (j + 1) % 2)
            stage_c(n + j, j % 2)
        for j in range(count):
            finish(n + j)

    unroll = 4
    n_blocks = (n_steps - 2) // unroll

    def block(k, carry):
        steps(unroll * k, unroll)
        return carry
    lax.fori_loop(0, n_blocks, block, 0)
    steps(unroll * n_blocks, n_steps - 2 - unroll * n_blocks)

    stage_b(n_steps - 1, 1)
    stage_c(n_steps - 2, 0)
    stage_c(n_steps - 1, 1)
    finish(n_steps - 2)
    finish(n_steps - 1)


def _diff_attention(qkv, km, vm, dblk, bm0, lamv, gain, batch, seq, lambda_init):
    nq = seq // TQ
    steps = [(qi, t) for qi in range(nq) for t in range(qi + 2)]
    assert len(steps) % 2 == 0
    qi_tab = jnp.asarray([s[0] for s in steps], jnp.int32)
    t_tab = jnp.asarray([s[1] for s in steps], jnp.int32)
    kern = functools.partial(_diff_kernel, lambda_init=lambda_init, n_steps=len(steps))
    return pl.pallas_call(
        kern,
        out_shape=jax.ShapeDtypeStruct((batch * seq, N_DIFF_HEADS * LANES), BF16),
        grid_spec=pltpu.PrefetchScalarGridSpec(
            num_scalar_prefetch=2,
            grid=(batch, N_DIFF_HEADS),
            in_specs=[
                pl.BlockSpec((seq, LANES), lambda b, h, *_: (b, C_DQ // LANES + h)),
                pl.BlockSpec((seq, LANES), lambda b, h, *_: (b, C_DK // LANES + h)),
                pl.BlockSpec((seq, LANES), lambda b, h, *_: (b, C_DV // LANES + h)),
                pl.BlockSpec((TQ, LANES), lambda b, h, *_: (0, h)),
                pl.BlockSpec((TQ, LANES), lambda b, h, *_: (0, h)),
                pl.BlockSpec((1, 2, BLOCK, BLOCK), lambda b, h, *_: (h, 0, 0, 0)),
                pl.BlockSpec((1, LANES, TQ), lambda b, h, *_: (h, 0, 0)),
                pl.BlockSpec(lamv.shape, lambda b, h, *_: (0, 0)),
                pl.BlockSpec((LANES, LANES), lambda b, h, *_: (0, 0)),
            ],
            out_specs=pl.BlockSpec((seq, LANES), lambda b, h, *_: (b, h)),
            scratch_shapes=[
                pltpu.VMEM((5, TQ, TQ), F32),
                pltpu.VMEM((nq, 2 * TQ, LANES), BF16),
                pltpu.VMEM((nq + 1, LANES + ONES_ROWS, TQ), BF16),
                pltpu.VMEM((2, TQ, 2 * TQ), F32),
                pltpu.VMEM((2, TQ, 2 * TQ), BF16),
                pltpu.VMEM((2, 8, 2 * TQ), F32),
                pltpu.VMEM((8, 2 * TQ), F32),
                pltpu.VMEM((2, LANES + ONES_ROWS, 2 * TQ), F32),
            ],
        ),
        compiler_params=_cparams(("parallel", "parallel")),
        name="diff_attention",
    )(qi_tab, t_tab, qkv, qkv, qkv, km, vm, jnp.swapaxes(dblk, -1, -2), jnp.swapaxes(bm0, -1, -2), lamv,
      jnp.broadcast_to(gain.reshape(LANES, 1), (LANES, LANES)))


def _swa_kernel(sink_ref, q_ref, kd_ref, vz_ref, kdm_ref, vzm_ref, bt_ref, o_ref):
    ci = pl.program_id(1)
    nblk = q_ref.shape[0] // BLOCK
    lane = lax.broadcasted_iota(jnp.int32, (BLOCK, LANES), 1)
    row2 = lax.broadcasted_iota(jnp.int32, (2 * BLOCK, 1), 0)

    def block_body(n, carry):
        gblk = ci * nblk + n
        first = jnp.where(gblk == 0, 0, 1)
        r_own = pl.multiple_of(gblk * BLOCK, BLOCK)
        r_prev = pl.multiple_of(jnp.maximum(gblk - 1, 0) * BLOCK, BLOCK)
        r_q = pl.multiple_of(n * BLOCK, BLOCK)
        for g in range(N_SWA_KV):
            kcat = jnp.concatenate([kdm_ref[:, g * LANES:(g + 1) * LANES],
                                    kd_ref[pl.ds(r_prev, BLOCK), g * LANES:(g + 1) * LANES],
                                    kd_ref[pl.ds(r_own, BLOCK), g * LANES:(g + 1) * LANES]], axis=0)
            for u in range(2):
                cs = slice((2 * g + u) * LANES, (2 * g + u + 1) * LANES)
                h0 = 4 * g + 2 * u
                qp = q_ref[pl.ds(r_q, BLOCK), cs]
                zero = jnp.zeros_like(qp)
                qs = jnp.concatenate([jnp.where(lane < HEAD_DIM, qp, zero),
                                      jnp.where(lane >= HEAD_DIM, qp, zero)], axis=0)
                s = lax.dot_general(qs, kcat, (((1,), (1,)), ((), ())), preferred_element_type=F32)
                s = s + jnp.concatenate([bt_ref[first, h0], bt_ref[first, h0 + 1]], axis=0)
                sink = jnp.where(row2 < BLOCK, sink_ref[h0:h0 + 1, 0:1], sink_ref[h0 + 1:h0 + 2, 0:1])
                m = jnp.maximum(jnp.max(s, axis=1, keepdims=True), sink)
                p = jnp.exp(s - m)
                den = jnp.sum(p, axis=1, keepdims=True) + jnp.exp(sink - m)
                pb = p.astype(BF16)
                inv = 1.0 / den
                o = None
                for par in range(2):
                    vs = slice((2 * g + par) * LANES, (2 * g + par + 1) * LANES)
                    vcat = jnp.concatenate([vzm_ref[:, vs],
                                            vz_ref[pl.ds(r_prev, BLOCK), vs],
                                            vz_ref[pl.ds(r_own, BLOCK), vs]], axis=0)
                    rs = slice(par * BLOCK, (par + 1) * BLOCK)
                    t = jnp.dot(pb[rs], vcat, preferred_element_type=F32) * inv[rs]
                    o = t if o is None else o + t
                o_ref[pl.ds(r_q, BLOCK), cs] = o.astype(BF16)
        return carry

    lax.fori_loop(0, nblk, block_body, 0)


def _swa_attention(sinks, qkv, kdm, vzm, bt, batch, seq):
    nc = seq // CQ
    sinkv = jnp.broadcast_to(sinks.reshape(N_SWA_HEADS, 1), (N_SWA_HEADS, LANES))
    return pl.pallas_call(
        _swa_kernel,
        out_shape=jax.ShapeDtypeStruct((batch * seq, N_SWA_HEADS * HEAD_DIM), BF16),
        grid=(batch, nc),
        in_specs=[
            pl.BlockSpec(sinkv.shape, lambda b, c: (0, 0)),
            pl.BlockSpec((CQ, 512), lambda b, c: (b * nc + c, C_SQ // 512)),
            pl.BlockSpec((seq, 256), lambda b, c: (b, C_KD // 256)),
            pl.BlockSpec((seq, 512), lambda b, c: (b, C_VZ // 512)),
            pl.BlockSpec(kdm.shape, lambda b, c: (0, 0)),
            pl.BlockSpec(vzm.shape, lambda b, c: (0, 0)),
            pl.BlockSpec(bt.shape, lambda b, c: (0, 0, 0, 0)),
        ],
        out_specs=pl.BlockSpec((CQ, 512), lambda b, c: (b * nc + c, 0)),
        compiler_params=_cparams(("parallel", "arbitrary")),
        name="swa_attention",
    )(sinkv, qkv, qkv, qkv, kdm, vzm, bt)


def _outproj_kernel(x_ref, md_ref, ms_ref, wo_ref, g2_ref, wr_ref, br_ref,
                    h_ref, hb_ref, rt_ref, ti_ref, cnt_ref, c_ref):
    i = pl.program_id(0)
    tm = x_ref.shape[0]
    half = md_ref.shape[1]

    @pl.when(i == 0)
    def _init():
        c_ref[...] = jnp.zeros(c_ref.shape, F32)

    h = (x_ref[...]
         + jnp.dot(md_ref[...], wo_ref[:half, :], preferred_element_type=F32)
         + jnp.dot(ms_ref[...], wo_ref[half:, :], preferred_element_type=F32))
    h_ref[...] = h
    hn = h * lax.rsqrt(jnp.mean(h * h, axis=-1, keepdims=True) + EPS) * g2_ref[...]
    hb = hn.astype(BF16)
    hb_ref[...] = hb

    lg = jnp.dot(hb, wr_ref[...], preferred_element_type=F32) + br_ref[...]
    lane_i = lax.broadcasted_iota(jnp.int32, lg.shape, 1)
    lane = lane_i.astype(F32)
    big = float(4 * LANES)
    is_g = (lane_i >= N_EXPERTS) & (lane_i < N_EXPERTS + N_GROUPS)
    glm = jnp.where(is_g, lg, -jnp.inf)
    gmax = jnp.max(glm, axis=1, keepdims=True)
    gidx = jnp.min(jnp.where(glm == gmax, lane, big), axis=1, keepdims=True) - N_EXPERTS
    gsum = jnp.sum(jnp.where(is_g, jnp.exp(lg - gmax), 0.0), axis=1, keepdims=True)
    g_w = 1.0 / gsum
    lane_grp = (lane_i >> 3).astype(F32)
    in_grp = (lane_i < N_EXPERTS) & (lane_grp == gidx)
    el = jnp.where(in_grp, lg, -jnp.inf)
    t1 = jnp.max(el, axis=1, keepdims=True)
    j1 = jnp.min(jnp.where(el == t1, lane, big), axis=1, keepdims=True)
    el2 = jnp.where(lane == j1, -jnp.inf, el)
    t2 = jnp.max(el2, axis=1, keepdims=True)
    j2 = jnp.min(jnp.where(el2 == t2, lane, big), axis=1, keepdims=True)
    e2 = jnp.exp(t2 - t1)
    den = 1.0 + e2
    gate1 = g_w / den
    gate2 = g_w * e2 / den

    o1 = lane == j1
    o2 = lane == j2
    onehot = jnp.where(o1 | o2, 1.0, 0.0).astype(BF16)
    rr = lax.broadcasted_iota(jnp.int32, (tm, tm), 0)
    cc = lax.broadcasted_iota(jnp.int32, (tm, tm), 1)
    lower = jnp.where(rr > cc, 1.0, 0.0).astype(BF16)
    pfx = jnp.dot(lower, onehot, preferred_element_type=F32)
    cnt_tile = jnp.sum(onehot.astype(F32), axis=0, keepdims=True)
    groups = jnp.floor((cnt_tile + (ROW_ALIGN - 1)) * (1.0 / ROW_ALIGN))
    er = lax.broadcasted_iota(jnp.int32, (LANES, LANES), 0)
    ec = lax.broadcasted_iota(jnp.int32, (LANES, LANES), 1)
    before = jnp.where(er < ec, 1.0, 0.0).astype(BF16)
    cbase = ROW_ALIGN * jnp.dot(jnp.broadcast_to(groups, (8, LANES)).astype(BF16), before,
                                preferred_element_type=F32)[0:1]
    at = pfx + cbase
    pos1 = jnp.sum(jnp.where(o1, at, 0.0), axis=1, keepdims=True)
    pos2 = jnp.sum(jnp.where(o2, at, 0.0), axis=1, keepdims=True)
    rt_ref[...] = jnp.where(lane_i == 0, gate1,
                            jnp.where(lane_i == 1, gate2,
                                      jnp.where(lane_i == 2, pos1,
                                                jnp.where(lane_i == 3, pos2, 0.0))))
    c_old = c_ref[...]
    c_new = c_old + groups * ROW_ALIGN
    c_ref[...] = c_new
    row8 = lax.broadcasted_iota(jnp.int32, (8, LANES), 0)
    ti_ref[...] = jnp.where(row8 == 0, cnt_tile, jnp.where(row8 == 1, c_old, 0.0))

    @pl.when(i == pl.num_programs(0) - 1)
    def _fin():
        cnt_ref[...] = c_new


def _outproj(x2, mixd, mixs, wo, g2, wr, br):
    n, d = x2.shape
    return pl.pallas_call(
        _outproj_kernel,
        out_shape=(jax.ShapeDtypeStruct((n, d), F32),
                   jax.ShapeDtypeStruct((n, d), BF16),
                   jax.ShapeDtypeStruct((n, LANES), F32),
                   jax.ShapeDtypeStruct((n // TM * 8, LANES), F32),
                   jax.ShapeDtypeStruct((8, LANES), F32)),
        grid=(n // TM,),
        in_specs=[
            pl.BlockSpec((TM, d), lambda i: (i, 0)),
            pl.BlockSpec((TM, mixd.shape[1]), lambda i: (i, 0)),
            pl.BlockSpec((TM, mixs.shape[1]), lambda i: (i, 0)),
            pl.BlockSpec(wo.shape, lambda i: (0, 0)),
            pl.BlockSpec(g2.shape, lambda i: (0, 0)),
            pl.BlockSpec(wr.shape, lambda i: (0, 0)),
            pl.BlockSpec(br.shape, lambda i: (0, 0)),
        ],
        out_specs=(pl.BlockSpec((TM, d), lambda i: (i, 0)),
                   pl.BlockSpec((TM, d), lambda i: (i, 0)),
                   pl.BlockSpec((TM, LANES), lambda i: (i, 0)),
                   pl.BlockSpec((8, LANES), lambda i: (i, 0)),
                   pl.BlockSpec((8, LANES), lambda i: (0, 0))),
        scratch_shapes=[pltpu.VMEM((8, LANES), F32)],
        compiler_params=_cparams(("arbitrary",)),
        name="outproj_router",
    )(x2, mixd, mixs, wo, g2, wr, br)


def _for_each_chunk(runs_ref, fn):
    def per_expert(e, sorted_row):
        start = runs_ref[0, 0, e]
        groups = runs_ref[0, 0, N_EXPERTS + e]
        whole = groups // (CHUNK // ROW_ALIGN)

        def per_chunk(c, carry):
            fn(pl.multiple_of(start + c * CHUNK, ROW_ALIGN), pl.multiple_of(sorted_row + c * CHUNK, ROW_ALIGN),
               CHUNK)
            return carry
        lax.fori_loop(0, whole, per_chunk, 0)

        @pl.when(groups % (CHUNK // ROW_ALIGN) == 1)
        def _():
            fn(pl.multiple_of(start + whole * CHUNK, ROW_ALIGN),
               pl.multiple_of(sorted_row + whole * CHUNK, ROW_ALIGN), ROW_ALIGN)
        return sorted_row + groups * ROW_ALIGN
    lax.fori_loop(0, N_EXPERTS, per_expert, 0)


def _wait_chunks(runs_ref, make_copy):
    for k, rows in enumerate((CHUNK, ROW_ALIGN)):
        def body(c, carry, rows=rows):
            make_copy(rows).wait()
            return carry
        lax.fori_loop(0, runs_ref[0, 0, 2 * N_EXPERTS + k], body, 0)


def _dispatch_kernel(zf_ref, cur_ref, prv_ref, hb_ref, rt_ref, xs_ref, sbuf, zbuf, sem, zsem):
    i = pl.program_id(0)
    nt = pl.num_programs(0)
    slot = i % 2
    tm, d = hb_ref.shape

    @pl.when(i == 0)
    def _():
        zbuf[...] = jnp.zeros(zbuf.shape, zbuf.dtype)

        def zero_copy(b):
            return pltpu.make_async_copy(zbuf, xs_ref.at[pl.ds(pl.multiple_of(b * EB, EB), EB)], zsem)

        def start(b, carry):
            @pl.when(zf_ref[b] == 1)
            def _():
                zero_copy(b).start()
            return carry
        lax.fori_loop(0, zf_ref.shape[0], start, 0)

        def wait(b, carry):
            @pl.when(zf_ref[b] == 1)
            def _():
                zero_copy(b).wait()
            return carry
        lax.fori_loop(0, zf_ref.shape[0], wait, 0)

    pos_t = jnp.transpose(rt_ref[...])
    srow = lax.broadcasted_iota(jnp.int32, (SROWS, tm), 0).astype(F32)
    sel = jnp.where(srow == pos_t[2:3, :], 1.0, jnp.where(srow == pos_t[3:4, :], 1.0, 0.0)).astype(BF16)
    srt = jnp.dot(sel, hb_ref[...], preferred_element_type=F32)
    bits = pltpu.bitcast(srt, jnp.uint32)
    sbuf[slot] = (bits[:, d // 2:] & jnp.uint32(0xFFFF0000)) | (bits[:, :d // 2] >> 16)

    def chunk_copy(run_row, sorted_row, rows, sl):
        return pltpu.make_async_copy(sbuf.at[sl, pl.ds(sorted_row, rows)], xs_ref.at[pl.ds(run_row, rows)],
                                     sem.at[sl])

    _for_each_chunk(cur_ref, lambda run_row, sorted_row, rows: chunk_copy(run_row, sorted_row, rows, slot).start())

    @pl.when(i > 0)
    def _():
        _wait_chunks(prv_ref, lambda rows: chunk_copy(0, 0, rows, 1 - slot))

    @pl.when(i == nt - 1)
    def _():
        _wait_chunks(cur_ref, lambda rows: chunk_copy(0, 0, rows, slot))


def _dispatch(zero_blocks, runs, hb, rt, n_rows):
    n, d = hb.shape
    return pl.pallas_call(
        _dispatch_kernel,
        out_shape=jax.ShapeDtypeStruct((n_rows, d // 2), jnp.uint32),
        grid_spec=pltpu.PrefetchScalarGridSpec(
            num_scalar_prefetch=1,
            grid=(n // TM,),
            in_specs=[
                pl.BlockSpec((1, 1, LANES), lambda i, zf: (i, 0, 0), memory_space=pltpu.SMEM),
                pl.BlockSpec((1, 1, LANES), lambda i, zf: (jnp.maximum(i - 1, 0), 0, 0), memory_space=pltpu.SMEM),
                pl.BlockSpec((TM, d), lambda i, zf: (i, 0)),
                pl.BlockSpec((TM, LANES), lambda i, zf: (i, 0)),
            ],
            out_specs=pl.BlockSpec(memory_space=pl.ANY),
            scratch_shapes=[pltpu.VMEM((2, SROWS, d // 2), jnp.uint32), pltpu.VMEM((EB, d // 2), jnp.uint32),
                            pltpu.SemaphoreType.DMA((2,)), pltpu.SemaphoreType.DMA(())],
        ),
        compiler_params=_cparams(("arbitrary",)),
        name="dispatch",
    )(zero_blocks, runs, runs, hb, rt)


def _experts_kernel(be_ref, na_ref, nxt_ref, xs_ref, wg_hbm, wu_hbm, wd_hbm, ys_ref,
                    wgf, wuf, wdf, wgb, wub, wdb, sem):
    b = pl.program_id(0)

    def weight_copies(e):
        return (pltpu.make_async_copy(wg_hbm.at[e], wgf, sem.at[0]),
                pltpu.make_async_copy(wu_hbm.at[e], wuf, sem.at[1]),
                pltpu.make_async_copy(wd_hbm.at[e], wdf, sem.at[2]))

    @pl.when(b == 0)
    def _():
        for c in weight_copies(be_ref[0]):
            c.start()

    @pl.when(b < na_ref[0])
    def _():
        e = be_ref[b]
        changed = jnp.logical_or(b == 0, be_ref[jnp.maximum(b - 1, 0)] != e)

        @pl.when(changed)
        def _load():
            for c in weight_copies(e):
                c.wait()
            wgb[...] = wgf[...].astype(BF16)
            wub[...] = wuf[...].astype(BF16)
            wdb[...] = wdf[...].astype(BF16)
            nxt = nxt_ref[e]

            @pl.when(nxt >= 0)
            def _():
                for c in weight_copies(nxt):
                    c.start()

        w = xs_ref[...]
        x_lo = pltpu.bitcast(w << 16, F32).astype(BF16)
        x_hi = pltpu.bitcast(w & jnp.uint32(0xFFFF0000), F32).astype(BF16)
        dh = w.shape[1]
        g = (jnp.dot(x_lo, wgb[:dh, :], preferred_element_type=F32)
             + jnp.dot(x_hi, wgb[dh:, :], preferred_element_type=F32))
        u = (jnp.dot(x_lo, wub[:dh, :], preferred_element_type=F32)
             + jnp.dot(x_hi, wub[dh:, :], preferred_element_type=F32))
        hdn = g * (1.0 / (1.0 + jnp.exp(-g))) * u
        ys_ref[...] = jnp.dot(hdn.astype(BF16), wdb[...], preferred_element_type=F32)

    @pl.when(b >= na_ref[0])
    def _():
        ys_ref[...] = jnp.zeros(ys_ref.shape, F32)


def _experts(blk_e, n_act, nxt_e, xs, w_gate, w_up, w_down):
    p, dh = xs.shape
    d = 2 * dh
    de = w_gate.shape[2]

    def row_map(b, be, na, nx):
        return (jnp.minimum(b, na[0] - 1), 0)

    return pl.pallas_call(
        _experts_kernel,
        out_shape=jax.ShapeDtypeStruct((p, d), F32),
        grid_spec=pltpu.PrefetchScalarGridSpec(
            num_scalar_prefetch=3,
            grid=(p // EB,),
            in_specs=[
                pl.BlockSpec((EB, dh), row_map),
                pl.BlockSpec(memory_space=pl.ANY),
                pl.BlockSpec(memory_space=pl.ANY),
                pl.BlockSpec(memory_space=pl.ANY),
            ],
            out_specs=pl.BlockSpec((EB, d), lambda b, be, na, nx: (b, 0)),
            scratch_shapes=[pltpu.VMEM((d, de), F32), pltpu.VMEM((d, de), F32), pltpu.VMEM((de, d), F32),
                            pltpu.VMEM((d, de), BF16), pltpu.VMEM((d, de), BF16), pltpu.VMEM((de, d), BF16),
                            pltpu.SemaphoreType.DMA((3,))],
        ),
        compiler_params=_cparams(("arbitrary",)),
        name="experts",
    )(blk_e, n_act, nxt_e, xs, w_gate, w_up, w_down)


def _combine_kernel(cur_ref, nxt_ref, ys_ref, h_ref, rt_ref, o_ref, ybuf, sem):
    i = pl.program_id(0)
    nt = pl.num_programs(0)
    slot = i % 2
    tm = h_ref.shape[0]

    def chunk_copy(run_row, sorted_row, rows, sl):
        return pltpu.make_async_copy(ys_ref.at[pl.ds(run_row, rows)], ybuf.at[sl, pl.ds(sorted_row, rows)],
                                     sem.at[sl])

    @pl.when(i == 0)
    def _():
        ybuf[...] = jnp.zeros(ybuf.shape, F32)
        _for_each_chunk(cur_ref, lambda run_row, sorted_row, rows: chunk_copy(run_row, sorted_row, rows, 0).start())

    @pl.when(i + 1 < nt)
    def _():
        _for_each_chunk(nxt_ref,
                        lambda run_row, sorted_row, rows: chunk_copy(run_row, sorted_row, rows, 1 - slot).start())

    _wait_chunks(cur_ref, lambda rows: chunk_copy(0, 0, rows, slot))

    rt = rt_ref[...]
    yb = ybuf[slot].astype(BF16)
    col = lax.broadcasted_iota(jnp.int32, (tm, SROWS), 1).astype(F32)
    w1 = jnp.where(col == rt[:, 2:3], 1.0, 0.0).astype(BF16)
    w2 = jnp.where(col == rt[:, 3:4], 1.0, 0.0).astype(BF16)
    o_ref[...] = (h_ref[...]
                  + rt[:, 0:1] * jnp.dot(w1, yb, preferred_element_type=F32)
                  + rt[:, 1:2] * jnp.dot(w2, yb, preferred_element_type=F32))


def _combine(runs, ys, h1, rt):
    n, d = h1.shape
    nt = n // TM
    return pl.pallas_call(
        _combine_kernel,
        out_shape=jax.ShapeDtypeStruct((n, d), F32),
        grid=(nt,),
        in_specs=[
            pl.BlockSpec((1, 1, LANES), lambda i: (i, 0, 0), memory_space=pltpu.SMEM),
            pl.BlockSpec((1, 1, LANES), lambda i: (jnp.minimum(i + 1, nt - 1), 0, 0), memory_space=pltpu.SMEM),
            pl.BlockSpec(memory_space=pl.ANY),
            pl.BlockSpec((TM, d), lambda i: (i, 0)),
            pl.BlockSpec((TM, LANES), lambda i: (i, 0)),
        ],
        out_specs=pl.BlockSpec((TM, d), lambda i: (i, 0)),
        scratch_shapes=[pltpu.VMEM((2, SROWS, d), F32), pltpu.SemaphoreType.DMA((2,))],
        compiler_params=_cparams(("arbitrary",)),
        name="combine",
    )(runs, runs, ys, h1, rt)


def _augmented_w_in(w_in):
    sk0 = C_SQ + N_SWA_HEADS * HEAD_DIM
    sv0 = sk0 + N_SWA_KV * HEAD_DIM
    z = jnp.zeros((w_in.shape[0], HEAD_DIM), w_in.dtype)
    vz, kd = [], []
    for g in range(N_SWA_KV):
        k = w_in[:, sk0 + g * HEAD_DIM: sk0 + (g + 1) * HEAD_DIM]
        v = w_in[:, sv0 + g * HEAD_DIM: sv0 + (g + 1) * HEAD_DIM]
        kd += [k, k]
        vz += [v, z, z, v]
    return jnp.concatenate([w_in[:, :C_VZ]] + vz + kd, axis=1)


def kernel(x, meta_tokens, rel_bias, norm1_gain, w_in, diff_q_gain, diff_k_gain, lam_q1, lam_k1, lam_q2, lam_k2, diff_subln_gain, swa_q_gain, swa_k_gain, swa_sinks, w_out, norm2_gain, w_group, b_group, w_router, b_router, w_gate, w_up, w_down):
    batch, seq, d = x.shape
    depth = w_in.shape[0]
    n = batch * seq
    assert seq % CQ == 0 and seq % TQ == 0 and n % TM == 0 and d == 1024
    assert meta_tokens.shape[0] == N_META
    assert depth == 1, "the meta-token rows of the residual stream are not carried across layers"

    h = x.reshape(n, d)
    dblk, bm0, bt = _bias_tables(rel_bias, TQ)
    scale = HEAD_DIM ** -0.5
    bd = jnp.asarray(np.kron(np.eye(MXU_DIM // HEAD_DIM), np.full((HEAD_DIM, HEAD_DIM), 1.0 / HEAD_DIM)), BF16)
    ones = jnp.ones((HEAD_DIM,), F32)
    lower_pad = N_EXPERTS + N_GROUPS

    for layer in range(depth):
        lambda_init = 0.8 - 0.6 * math.exp(-0.3 * layer)
        w_aug = _augmented_w_in(w_in[layer]).astype(BF16)
        gain = jnp.concatenate([
            jnp.tile(diff_q_gain[layer] * scale, 2 * N_DIFF_HEADS),
            jnp.tile(diff_k_gain[layer], 2 * N_DIFF_HEADS),
            jnp.tile(ones, 2 * N_DIFF_HEADS),
            jnp.tile(swa_q_gain[layer] * scale, N_SWA_HEADS),
            jnp.tile(ones, 4 * N_SWA_KV),
            jnp.tile(swa_k_gain[layer], 2 * N_SWA_KV)]).reshape(1, C_END).astype(F32)
        g1 = norm1_gain[layer].reshape(1, d).astype(F32)

        qkv = _proj(h, g1, w_aug, bd, gain, TM)
        qkv_meta = _proj(meta_tokens.astype(F32), g1, w_aug, bd, gain, N_META)
        meta_pad = jnp.pad(qkv_meta, ((0, TQ - N_META), (0, 0)))

        lamv = jnp.pad(jnp.stack([lam_q1[layer], lam_k1[layer], lam_q2[layer], lam_k2[layer]]).astype(F32),
                       ((0, 4), (0, LANES - HEAD_DIM)))
        mixd = _diff_attention(qkv, meta_pad[:, C_DK:C_DV], meta_pad[:, C_DV:C_SQ], dblk, bm0, lamv,
                               diff_subln_gain[layer].reshape(1, LANES).astype(F32), batch, seq, lambda_init)
        mixs = _swa_attention(swa_sinks[layer].astype(F32), qkv, meta_pad[:BLOCK, C_KD:C_END],
                              meta_pad[:BLOCK, C_VZ:C_KD], bt, batch, seq)

        wr = jnp.pad(jnp.concatenate([w_router[layer], w_group[layer]], axis=1),
                     ((0, 0), (0, LANES - lower_pad))).astype(BF16)
        br = jnp.pad(jnp.concatenate([b_router[layer], b_group[layer]]), (0, LANES - lower_pad)).reshape(1, LANES)
        h1, hb, rt, tinfo, cnt = _outproj(h, mixd, mixs, w_out[layer].astype(BF16),
                                          norm2_gain[layer].reshape(1, d).astype(F32), wr, br.astype(F32))

        nt = n // TM
        counts = cnt[0, :N_EXPERTS].astype(jnp.int32)
        nblk_e = (counts + EB - 1) // EB
        blk_end = jnp.cumsum(nblk_e)
        pstart = ((blk_end - nblk_e) * EB).astype(jnp.int32)
        n_blocks = -(-(2 * n + nt * N_EXPERTS * (ROW_ALIGN - 1) + N_EXPERTS * (EB - 1)) // EB)
        blk_ids = jnp.arange(n_blocks)
        blk_e = jnp.minimum(jnp.sum(blk_end[None, :] <= blk_ids[:, None], axis=1), N_EXPERTS - 1).astype(jnp.int32)
        n_act = blk_end[-1:].astype(jnp.int32)
        is_last = jnp.any((blk_end[None, :] == blk_ids[:, None] + 1) & (nblk_e[None, :] > 0), axis=1)
        zero_blocks = ((blk_ids >= n_act[0]) | is_last).astype(jnp.int32)
        ti = tinfo.reshape(nt, 8, LANES)
        run_len = ti[:, 0, :N_EXPERTS].astype(jnp.int32)
        run_start = pstart[None, :] + ti[:, 1, :N_EXPERTS].astype(jnp.int32)
        run_groups = (run_len + ROW_ALIGN - 1) // ROW_ALIGN
        per_chunk = CHUNK // ROW_ALIGN
        runs = jnp.concatenate([run_start, run_groups,
                                jnp.sum(run_groups // per_chunk, axis=1, keepdims=True),
                                jnp.sum(run_groups % per_chunk, axis=1, keepdims=True),
                                jnp.zeros((nt, LANES - 2 * N_EXPERTS - 2), jnp.int32)],
                               axis=1).reshape(nt, 1, LANES)

        xs = _dispatch(zero_blocks, runs, hb, rt, n_blocks * EB)
        own = jnp.where(nblk_e > 0, jnp.arange(N_EXPERTS), N_EXPERTS)
        later = jnp.concatenate([lax.cummin(own[::-1])[::-1][1:], jnp.full((1,), N_EXPERTS)])
        nxt_e = jnp.where(later < N_EXPERTS, later, -1).astype(jnp.int32)
        ys = _experts(blk_e, n_act, nxt_e, xs, w_gate[layer], w_up[layer], w_down[layer])
        h = _combine(runs, ys, h1, rt)
    return h.reshape(batch, seq, d)
```

```python
import functools
import math

import numpy as np
import jax
import jax.numpy as jnp
from jax import lax
from jax.experimental import pallas as pl
from jax.experimental.pallas import tpu as pltpu

F32 = jnp.float32
BF16 = jnp.bfloat16

HEAD_DIM = 64
N_DIFF_HEADS = 4
N_SWA_HEADS = 8
N_SWA_KV = 2
BLOCK = 128
N_META = 16
N_BUCKETS = 32
MAX_DISTANCE = 128
N_GROUPS = 4
EXPERTS_PER_GROUP = 8
N_EXPERTS = N_GROUPS * EXPERTS_PER_GROUP
D_EXPERT = 512
EPS = 1e-6
NEG = -1e30

LANES = 128
MXU_DIM = 256
VMEM_LIMIT = 48 * 1024 * 1024

TP = 512
TM = 256
TQ = 256
ONES_ROWS = 16
CQ = 512
EB = 256
ROW_ALIGN = 8
CHUNK = 16
SROWS = -(-(2 * TM + N_EXPERTS * (ROW_ALIGN - 1)) // MXU_DIM) * MXU_DIM

C_DQ, C_DK, C_DV, C_SQ, C_SK, C_SV, C_END = 0, 512, 1024, 1536, 2048, 2176, 2304
NORM_GROUPS = (0, 1, 2, 3, 6, 7, 8)


def _cparams(sem):
    return pltpu.CompilerParams(dimension_semantics=sem, vmem_limit_bytes=VMEM_LIMIT)


def _t5_bucket_np(dist):
    n = np.maximum(dist, 0)
    max_exact = N_BUCKETS // 2
    nf = np.maximum(n, 1).astype(np.float32)
    large = max_exact + (np.log(nf / np.float32(max_exact)) / np.float32(math.log(MAX_DISTANCE / max_exact))
                         * np.float32(N_BUCKETS - max_exact)).astype(np.int32)
    large = np.minimum(large, N_BUCKETS - 1)
    return np.where(n < max_exact, n, large)


def _bias_tables(rel_bias, tq):
    nd = 2 * BLOCK
    buckets = _t5_bucket_np(np.arange(nd))
    assert (buckets[MAX_DISTANCE:] == N_BUCKETS - 1).all()
    rb = rel_bias.astype(F32)
    r = np.arange(BLOCK)[:, None]
    c = np.arange(BLOCK)[None, :]
    d_own = r - c
    d_prev = BLOCK + r - c
    far = rb[N_BUCKETS - 1]

    def take(dist):
        idx = jnp.asarray(buckets[np.clip(dist, 0, nd - 1)], jnp.int32)[None]
        out = jnp.zeros((rb.shape[1],) + dist.shape, F32)
        for b in range(N_BUCKETS):
            out = jnp.where(idx == b, rb[b].reshape((-1,) + (1,) * dist.ndim), out)
        return out

    hd = slice(0, N_DIFF_HEADS)
    far_d = far[hd][:, None, None]
    d0 = jnp.where(d_own[None] >= 0, take(d_own)[hd] - far_d, NEG)
    d1 = take(d_prev)[hd] - far_d
    dblk = jnp.stack([d0, d1], axis=1)
    rq = np.arange(tq)[:, None]
    cm = np.arange(LANES)[None, :]
    d_meta = N_META + rq - cm
    bm0 = jnp.where((cm < N_META)[None], take(d_meta)[hd] - far_d, NEG)

    hs = slice(N_DIFF_HEADS, N_DIFF_HEADS + N_SWA_HEADS)
    far_s = far[hs][:, None, None]
    d_meta_s = N_META + r - cm
    meta_first = jnp.where((cm < N_META)[None], take(d_meta_s)[hs], NEG)
    meta_rest = jnp.where((cm < N_META)[None], jnp.broadcast_to(far_s, (N_SWA_HEADS, BLOCK, LANES)), NEG)
    prev_rest = jnp.where((c > r)[None], take(d_prev)[hs], NEG)
    prev_first = jnp.full((N_SWA_HEADS, BLOCK, BLOCK), NEG, F32)
    own = jnp.where((d_own >= 0)[None], take(d_own)[hs], NEG)
    bt = jnp.stack([jnp.concatenate([meta_first, prev_first, own], axis=-1),
                    jnp.concatenate([meta_rest, prev_rest, own], axis=-1)], axis=0)
    return dblk.astype(F32), bm0.astype(F32), bt.astype(F32)


def _proj_kernel(x_ref, g1_ref, w_ref, bd_ref, gain_ref, nmask_ref, o_ref):
    x = x_ref[...]
    a = x * lax.rsqrt(jnp.mean(x * x, axis=-1, keepdims=True) + EPS) * g1_ref[...]
    p = jnp.dot(a.astype(BF16), w_ref[...], preferred_element_type=F32)
    bd = bd_ref[...]
    for j in range(C_END // MXU_DIM):
        sl = slice(j * MXU_DIM, (j + 1) * MXU_DIM)
        pj = p[:, sl]
        if j in NORM_GROUPS:
            ms = jnp.dot((pj * pj).astype(BF16), bd, preferred_element_type=F32)
            pj = jnp.where(nmask_ref[:, sl] != 0.0, pj * lax.rsqrt(ms + EPS) * gain_ref[:, sl], pj)
        o_ref[:, sl] = pj.astype(BF16)


def _proj(x2, g1, w, bd, gain, nmask, tm):
    n = x2.shape[0]
    return pl.pallas_call(
        _proj_kernel,
        out_shape=jax.ShapeDtypeStruct((n, C_END), BF16),
        grid=(n // tm,),
        in_specs=[
            pl.BlockSpec((tm, x2.shape[1]), lambda i: (i, 0)),
            pl.BlockSpec(g1.shape, lambda i: (0, 0)),
            pl.BlockSpec(w.shape, lambda i: (0, 0)),
            pl.BlockSpec(bd.shape, lambda i: (0, 0)),
            pl.BlockSpec(gain.shape, lambda i: (0, 0)),
            pl.BlockSpec(nmask.shape, lambda i: (0, 0)),
        ],
        out_specs=pl.BlockSpec((tm, C_END), lambda i: (i, 0)),
        compiler_params=_cparams(("parallel",)),
        name="proj",
    )(x2, g1, w, bd, gain, nmask)


def _diff_kernel(qi_tab, t_tab, q_ref, k_ref, v_ref, km_ref, vm_ref, d_ref, bm0_ref, lamv_ref, gain_ref, o_ref,
                 bias_ref, qs_ref, vt_ref, s_buf, p_buf, a_buf, m_ref, acc_ref, *, lambda_init, n_steps):
    tq = TQ
    nq = q_ref.shape[0] // tq
    nb = tq // BLOCK
    BIAS_NONE, BIAS_LEFT, BIAS_DIAG, BIAS_META, BIAS_META0 = 0, 1, 2, 3, 4

    d0 = d_ref[0, 0]
    d1 = d_ref[0, 1]
    zeros = jnp.zeros((BLOCK, BLOCK), F32)
    bias_ref[BIAS_NONE] = jnp.zeros((tq, tq), F32)
    for a in range(nb):
        for b in range(nb):
            rs, cs = slice(a * BLOCK, (a + 1) * BLOCK), slice(b * BLOCK, (b + 1) * BLOCK)
            if a == b:
                blk = d0
            elif b == a + 1:
                blk = d1
            elif b > a:
                blk = zeros
            else:
                blk = jnp.full((BLOCK, BLOCK), NEG, F32)
            bias_ref[BIAS_DIAG, rs, cs] = blk
            bias_ref[BIAS_LEFT, rs, cs] = d1 if (b == 0 and a == nb - 1) else zeros
    row_m = lax.broadcasted_iota(jnp.int32, (tq, tq), 0)
    bias_ref[BIAS_META] = jnp.where(row_m < N_META, 0.0, NEG).astype(F32)
    bias_ref[BIAS_META0, :LANES, :] = bm0_ref[0]
    bias_ref[BIAS_META0, LANES:, :] = jnp.full((tq - LANES, tq), NEG, F32)

    lane = lax.broadcasted_iota(jnp.int32, (tq, LANES), 1)
    for i in range(nq):
        rows = slice(i * tq, (i + 1) * tq)
        q = q_ref[rows, :]
        zero = jnp.zeros_like(q)
        qs_ref[i] = jnp.concatenate([jnp.where(lane < HEAD_DIM, q, zero),
                                     jnp.where(lane >= HEAD_DIM, q, zero)], axis=0)
        vt_ref[i, :LANES, :] = jnp.transpose(v_ref[rows, :].astype(F32)).astype(BF16)
    vt_ref[nq, :LANES, :] = jnp.transpose(vm_ref[...].astype(F32)).astype(BF16)
    vt_ref[:, LANES:, :] = jnp.ones((nq + 1, ONES_ROWS, tq), BF16)
    acc_ref[...] = jnp.zeros(acc_ref.shape, F32)
    m_ref[...] = jnp.full(m_ref.shape, NEG, F32)
    lv = lamv_ref[...]
    lam = (jnp.exp(jnp.sum(lv[0:1] * lv[1:2], axis=-1, keepdims=True))
           - jnp.exp(jnp.sum(lv[2:3] * lv[3:4], axis=-1, keepdims=True)) + lambda_init)

    def seq_row(t):
        return pl.multiple_of(jnp.maximum(t - 1, 0) * tq, tq)

    def stage_a(n, slot):
        qi, t = qi_tab[n], t_tab[n]
        kt = jnp.where(t == 0, km_ref[...], k_ref[pl.ds(seq_row(t), tq), :])
        s = lax.dot_general(kt, qs_ref[qi], (((1,), (1,)), ((), ())), preferred_element_type=F32)
        which = jnp.where(t == 0, jnp.where(qi == 0, BIAS_META0, BIAS_META),
                          jnp.where(t == qi + 1, BIAS_DIAG, jnp.where(t == qi, BIAS_LEFT, BIAS_NONE)))
        s_buf[slot] = s + jnp.tile(bias_ref[which], (1, 2))

    def stage_b(n, slot):
        s = s_buf[slot]
        m_prev = jnp.where(t_tab[n] == 0, NEG, m_ref[...])
        m_new = jnp.maximum(m_prev, jnp.max(s, axis=0, keepdims=True))
        a_buf[slot] = jnp.exp(m_prev - m_new)
        p_buf[slot] = jnp.exp(s - m_new[0:1]).astype(BF16)
        m_ref[...] = m_new

    def stage_c(n, slot):
        qi, t = qi_tab[n], t_tab[n]
        par = qi % 2
        vt = vt_ref[jnp.where(t == 0, nq, t - 1)]
        pv = jnp.dot(vt, p_buf[slot], preferred_element_type=F32)
        acc_ref[par] = a_buf[slot][0:1] * acc_ref[par] + pv

    def finish(n):
        qi, t = qi_tab[n], t_tab[n]

        @pl.when(t == qi + 1)
        def _():
            acc = acc_ref[qi % 2]
            o = acc[:LANES] / acc[LANES:LANES + 1]
            d = o[:, :tq] - lam * o[:, tq:]
            y = d * lax.rsqrt(jnp.mean(d * d, axis=0, keepdims=True) + EPS) * jnp.tile(gain_ref[...], (1, tq // LANES))
            y = jnp.transpose(y * (1.0 - lambda_init))
            o_ref[pl.ds(pl.multiple_of(qi * tq, tq), tq), :] = y.astype(BF16)

    stage_a(0, 0)
    stage_a(1, 1)
    stage_b(0, 0)

    def steps(n, count):
        for j in range(count):
            stage_a(n + j + 2, j % 2)
            stage_b(n + j + 1, (j + 1) % 2)
            stage_c(n + j, j % 2)
        for j in range(count):
            finish(n + j)

    unroll = 4
    n_blocks = (n_steps - 2) // unroll

    def block(k, carry):
        steps(unroll * k, unroll)
        return carry
    lax.fori_loop(0, n_blocks, block, 0)
    steps(unroll * n_blocks, n_steps - 2 - unroll * n_blocks)

    stage_b(n_steps - 1, 1)
    stage_c(n_steps - 2, 0)
    stage_c(n_steps - 1, 1)
    finish(n_steps - 2)
    finish(n_steps - 1)


def _diff_attention(qkv, km, vm, dblk, bm0, lamv, gain, batch, seq, lambda_init):
    nq = seq // TQ
    steps = [(qi, t) for qi in range(nq) for t in range(qi + 2)]
    assert len(steps) % 2 == 0
    qi_tab = jnp.asarray([s[0] for s in steps], jnp.int32)
    t_tab = jnp.asarray([s[1] for s in steps], jnp.int32)
    kern = functools.partial(_diff_kernel, lambda_init=lambda_init, n_steps=len(steps))
    return pl.pallas_call(
        kern,
        out_shape=jax.ShapeDtypeStruct((batch * seq, N_DIFF_HEADS * LANES), BF16),
        grid_spec=pltpu.PrefetchScalarGridSpec(
            num_scalar_prefetch=2,
            grid=(batch, N_DIFF_HEADS),
            in_specs=[
                pl.BlockSpec((seq, LANES), lambda b, h, *_: (b, C_DQ // LANES + h)),
                pl.BlockSpec((seq, LANES), lambda b, h, *_: (b, C_DK // LANES + h)),
                pl.BlockSpec((seq, LANES), lambda b, h, *_: (b, C_DV // LANES + h)),
                pl.BlockSpec((TQ, LANES), lambda b, h, *_: (0, h)),
                pl.BlockSpec((TQ, LANES), lambda b, h, *_: (0, h)),
                pl.BlockSpec((1, 2, BLOCK, BLOCK), lambda b, h, *_: (h, 0, 0, 0)),
                pl.BlockSpec((1, LANES, TQ), lambda b, h, *_: (h, 0, 0)),
                pl.BlockSpec(lamv.shape, lambda b, h, *_: (0, 0)),
                pl.BlockSpec((LANES, LANES), lambda b, h, *_: (0, 0)),
            ],
            out_specs=pl.BlockSpec((seq, LANES), lambda b, h, *_: (b, h)),
            scratch_shapes=[
                pltpu.VMEM((5, TQ, TQ), F32),
                pltpu.VMEM((nq, 2 * TQ, LANES), BF16),
                pltpu.VMEM((nq + 1, LANES + ONES_ROWS, TQ), BF16),
                pltpu.VMEM((2, TQ, 2 * TQ), F32),
                pltpu.VMEM((2, TQ, 2 * TQ), BF16),
                pltpu.VMEM((2, 8, 2 * TQ), F32),
                pltpu.VMEM((8, 2 * TQ), F32),
                pltpu.VMEM((2, LANES + ONES_ROWS, 2 * TQ), F32),
            ],
        ),
        compiler_params=_cparams(("parallel", "parallel")),
        name="diff_attention",
    )(qi_tab, t_tab, qkv, qkv, qkv, km, vm, jnp.swapaxes(dblk, -1, -2), jnp.swapaxes(bm0, -1, -2), lamv,
      jnp.broadcast_to(gain.reshape(LANES, 1), (LANES, LANES)))


def _swa_kernel(sink_ref, q_ref, k_ref, v_ref, km_ref, vm_ref, bt_ref, o_ref, kd_ref, vt_ref,
                s_scr, p_scr, inv_scr):
    ci = pl.program_id(1)
    nblk = q_ref.shape[0] // BLOCK
    nkb = k_ref.shape[0] // BLOCK
    lane = lax.broadcasted_iota(jnp.int32, (BLOCK, LANES), 1)

    def both_halves(k):
        k0, k1 = k[:, :HEAD_DIM], k[:, HEAD_DIM:]
        return jnp.concatenate([k0, k0, k1, k1], axis=1)

    @pl.when(ci == 0)
    def _per_batch():
        def body(j, carry):
            rows = pl.ds(pl.multiple_of(j * BLOCK, BLOCK), BLOCK)
            kd_ref[j] = both_halves(k_ref[rows, :])
            vt_ref[j] = jnp.transpose(v_ref[rows, :].astype(F32)).astype(BF16)
            return carry
        lax.fori_loop(0, nkb, body, 0)
        kd_ref[nkb] = both_halves(km_ref[...])
        vt_ref[nkb] = jnp.transpose(vm_ref[...].astype(F32)).astype(BF16)

    def block_body(n, carry):
        gblk = ci * nblk + n
        first = jnp.where(gblk == 0, 0, 1)
        prev = jnp.maximum(gblk - 1, 0)
        r_q = pl.multiple_of(n * BLOCK, BLOCK)
        pairs = [(g, u) for g in range(N_SWA_KV) for u in range(2)]
        for c, (g, u) in enumerate(pairs):
            ks = slice(g * LANES, (g + 1) * LANES)
            kcat = jnp.concatenate([kd_ref[nkb, :, ks], kd_ref[prev, :, ks], kd_ref[gblk, :, ks]], axis=0)
            h0 = 4 * g + 2 * u
            qp = q_ref[pl.ds(r_q, BLOCK), (2 * g + u) * LANES:(2 * g + u + 1) * LANES]
            zero = jnp.zeros_like(qp)
            qs = jnp.concatenate([jnp.where(lane < HEAD_DIM, qp, zero),
                                  jnp.where(lane >= HEAD_DIM, qp, zero)], axis=0)
            s = lax.dot_general(kcat, qs, (((1,), (1,)), ((), ())), preferred_element_type=F32)
            s_scr[c] = s + jnp.concatenate([bt_ref[first, h0], bt_ref[first, h0 + 1]], axis=1)
        for c, (g, u) in enumerate(pairs):
            h0 = 4 * g + 2 * u
            s = s_scr[c]
            sink = jnp.concatenate([sink_ref[h0:h0 + 1, :], sink_ref[h0 + 1:h0 + 2, :]], axis=1)
            m = jnp.maximum(jnp.max(s, axis=0, keepdims=True), sink)
            p = jnp.exp(s - m)
            p_scr[c] = p.astype(BF16)
            inv_scr[c] = jnp.broadcast_to(1.0 / (jnp.sum(p, axis=0, keepdims=True) + jnp.exp(sink - m)),
                                          inv_scr.shape[1:])
        for c, (g, u) in enumerate(pairs):
            vs = slice(g * HEAD_DIM, (g + 1) * HEAD_DIM)
            vcat = jnp.concatenate([vt_ref[nkb, vs, :], vt_ref[prev, vs, :], vt_ref[gblk, vs, :]], axis=1)
            o = jnp.dot(vcat, p_scr[c], preferred_element_type=F32) * inv_scr[c][0:1]
            ot = jnp.transpose(o)
            o_ref[pl.ds(r_q, BLOCK), (2 * g + u) * LANES:(2 * g + u + 1) * LANES] = (
                jnp.concatenate([ot[:BLOCK], ot[BLOCK:]], axis=1).astype(BF16))
        return carry

    lax.fori_loop(0, nblk, block_body, 0)


def _swa_attention(sinks, qkv, km, vm, bt, batch, seq):
    nc = seq // CQ
    nkb = seq // BLOCK
    sinkv = jnp.broadcast_to(sinks.reshape(N_SWA_HEADS, 1), (N_SWA_HEADS, LANES))
    return pl.pallas_call(
        _swa_kernel,
        out_shape=jax.ShapeDtypeStruct((batch * seq, N_SWA_HEADS * HEAD_DIM), BF16),
        grid=(batch, nc),
        in_specs=[
            pl.BlockSpec(sinkv.shape, lambda b, c: (0, 0)),
            pl.BlockSpec((CQ, 512), lambda b, c: (b * nc + c, C_SQ // 512)),
            pl.BlockSpec((seq, LANES), lambda b, c: (b, C_SK // LANES)),
            pl.BlockSpec((seq, LANES), lambda b, c: (b, C_SV // LANES)),
            pl.BlockSpec(km.shape, lambda b, c: (0, 0)),
            pl.BlockSpec(vm.shape, lambda b, c: (0, 0)),
            pl.BlockSpec(bt.shape, lambda b, c: (0, 0, 0, 0)),
        ],
        out_specs=pl.BlockSpec((CQ, 512), lambda b, c: (b * nc + c, 0)),
        scratch_shapes=[pltpu.VMEM((nkb + 1, BLOCK, 2 * LANES), BF16),
                        pltpu.VMEM((nkb + 1, LANES, BLOCK), BF16),
                        pltpu.VMEM((4, 3 * BLOCK, 2 * BLOCK), F32),
                        pltpu.VMEM((4, 3 * BLOCK, 2 * BLOCK), BF16),
                        pltpu.VMEM((4, 8, 2 * BLOCK), F32)],
        compiler_params=_cparams(("parallel", "arbitrary")),
        name="swa_attention",
    )(sinkv, qkv, qkv, qkv, km, vm, bt)


def _outproj_kernel(x_ref, md_ref, ms_ref, wo_ref, g2_ref, wr_ref, br_ref,
                    h_ref, hb_ref, rt_ref, ti_ref, cnt_ref, c_ref):
    i = pl.program_id(0)
    tm = x_ref.shape[0]
    half = md_ref.shape[1]

    @pl.when(i == 0)
    def _init():
        c_ref[...] = jnp.zeros(c_ref.shape, F32)

    h = (x_ref[...]
         + jnp.dot(md_ref[...], wo_ref[:half, :], preferred_element_type=F32)
         + jnp.dot(ms_ref[...], wo_ref[half:, :], preferred_element_type=F32))
    h_ref[...] = h
    hn = h * lax.rsqrt(jnp.mean(h * h, axis=-1, keepdims=True) + EPS) * g2_ref[...]
    hb = hn.astype(BF16)
    hb_ref[...] = hb

    lg = jnp.dot(hb, wr_ref[...], preferred_element_type=F32) + br_ref[...]
    lane_i = lax.broadcasted_iota(jnp.int32, lg.shape, 1)
    lane = lane_i.astype(F32)
    big = float(4 * LANES)
    is_g = (lane_i >= N_EXPERTS) & (lane_i < N_EXPERTS + N_GROUPS)
    glm = jnp.where(is_g, lg, -jnp.inf)
    gmax = jnp.max(glm, axis=1, keepdims=True)
    gidx = jnp.min(jnp.where(glm == gmax, lane, big), axis=1, keepdims=True) - N_EXPERTS
    gsum = jnp.sum(jnp.where(is_g, jnp.exp(lg - gmax), 0.0), axis=1, keepdims=True)
    g_w = 1.0 / gsum
    lane_grp = (lane_i >> 3).astype(F32)
    in_grp = (lane_i < N_EXPERTS) & (lane_grp == gidx)
    el = jnp.where(in_grp, lg, -jnp.inf)
    t1 = jnp.max(el, axis=1, keepdims=True)
    j1 = jnp.min(jnp.where(el == t1, lane, big), axis=1, keepdims=True)
    el2 = jnp.where(lane == j1, -jnp.inf, el)
    t2 = jnp.max(el2, axis=1, keepdims=True)
    j2 = jnp.min(jnp.where(el2 == t2, lane, big), axis=1, keepdims=True)
    e2 = jnp.exp(t2 - t1)
    den = 1.0 + e2
    gate1 = g_w / den
    gate2 = g_w * e2 / den

    o1 = lane == j1
    o2 = lane == j2
    onehot = jnp.where(o1 | o2, 1.0, 0.0).astype(BF16)
    rr = lax.broadcasted_iota(jnp.int32, (tm, tm), 0)
    cc = lax.broadcasted_iota(jnp.int32, (tm, tm), 1)
    lower = jnp.where(rr > cc, 1.0, 0.0).astype(BF16)
    pfx = jnp.dot(lower, onehot, preferred_element_type=F32)
    cnt_tile = jnp.sum(onehot.astype(F32), axis=0, keepdims=True)
    groups = jnp.floor((cnt_tile + (ROW_ALIGN - 1)) * (1.0 / ROW_ALIGN))
    er = lax.broadcasted_iota(jnp.int32, (LANES, LANES), 0)
    ec = lax.broadcasted_iota(jnp.int32, (LANES, LANES), 1)
    before = jnp.where(er < ec, 1.0, 0.0).astype(BF16)
    cbase = ROW_ALIGN * jnp.dot(jnp.broadcast_to(groups, (8, LANES)).astype(BF16), before,
                                preferred_element_type=F32)[0:1]
    at = pfx + cbase
    pos1 = jnp.sum(jnp.where(o1, at, 0.0), axis=1, keepdims=True)
    pos2 = jnp.sum(jnp.where(o2, at, 0.0), axis=1, keepdims=True)
    rt_ref[...] = jnp.where(lane_i == 0, gate1,
                            jnp.where(lane_i == 1, gate2,
                                      jnp.where(lane_i == 2, pos1,
                                                jnp.where(lane_i == 3, pos2, 0.0))))
    c_old = c_ref[...]
    c_new = c_old + groups * ROW_ALIGN
    c_ref[...] = c_new
    row8 = lax.broadcasted_iota(jnp.int32, (8, LANES), 0)
    ti_ref[...] = jnp.where(row8 == 0, cnt_tile, jnp.where(row8 == 1, c_old, 0.0))

    @pl.when(i == pl.num_programs(0) - 1)
    def _fin():
        cnt_ref[...] = c_new


def _outproj(x2, mixd, mixs, wo, g2, wr, br):
    n, d = x2.shape
    return pl.pallas_call(
        _outproj_kernel,
        out_shape=(jax.ShapeDtypeStruct((n, d), F32),
                   jax.ShapeDtypeStruct((n, d), BF16),
                   jax.ShapeDtypeStruct((n, LANES), F32),
                   jax.ShapeDtypeStruct((n // TM * 8, LANES), F32),
                   jax.ShapeDtypeStruct((8, LANES), F32)),
        grid=(n // TM,),
        in_specs=[
            pl.BlockSpec((TM, d), lambda i: (i, 0)),
            pl.BlockSpec((TM, mixd.shape[1]), lambda i: (i, 0)),
            pl.BlockSpec((TM, mixs.shape[1]), lambda i: (i, 0)),
            pl.BlockSpec(wo.shape, lambda i: (0, 0)),
            pl.BlockSpec(g2.shape, lambda i: (0, 0)),
            pl.BlockSpec(wr.shape, lambda i: (0, 0)),
            pl.BlockSpec(br.shape, lambda i: (0, 0)),
        ],
        out_specs=(pl.BlockSpec((TM, d), lambda i: (i, 0)),
                   pl.BlockSpec((TM, d), lambda i: (i, 0)),
                   pl.BlockSpec((TM, LANES), lambda i: (i, 0)),
                   pl.BlockSpec((8, LANES), lambda i: (i, 0)),
                   pl.BlockSpec((8, LANES), lambda i: (0, 0))),
        scratch_shapes=[pltpu.VMEM((8, LANES), F32)],
        compiler_params=_cparams(("arbitrary",)),
        name="outproj_router",
    )(x2, mixd, mixs, wo, g2, wr, br)


def _for_each_chunk(runs_ref, fn):
    def per_expert(e, sorted_row):
        start = runs_ref[0, 0, e]
        groups = runs_ref[0, 0, N_EXPERTS + e]
        whole = groups // (CHUNK // ROW_ALIGN)

        def per_chunk(c, carry):
            fn(pl.multiple_of(start + c * CHUNK, ROW_ALIGN), pl.multiple_of(sorted_row + c * CHUNK, ROW_ALIGN),
               CHUNK)
            return carry
        lax.fori_loop(0, whole, per_chunk, 0)

        @pl.when(groups % (CHUNK // ROW_ALIGN) == 1)
        def _():
            fn(pl.multiple_of(start + whole * CHUNK, ROW_ALIGN),
               pl.multiple_of(sorted_row + whole * CHUNK, ROW_ALIGN), ROW_ALIGN)
        return sorted_row + groups * ROW_ALIGN
    lax.fori_loop(0, N_EXPERTS, per_expert, 0)


def _wait_chunks(runs_ref, make_copy):
    for k, rows in enumerate((CHUNK, ROW_ALIGN)):
        def body(c, carry, rows=rows):
            make_copy(rows).wait()
            return carry
        lax.fori_loop(0, runs_ref[0, 0, 2 * N_EXPERTS + k], body, 0)


def _dispatch_kernel(zf_ref, cur_ref, prv_ref, hb_ref, rt_ref, xs_ref, sbuf, zbuf, sem, zsem):
    i = pl.program_id(0)
    nt = pl.num_programs(0)
    slot = i % 2
    tm, d = hb_ref.shape

    @pl.when(i == 0)
    def _():
        zbuf[...] = jnp.zeros(zbuf.shape, zbuf.dtype)

        def zero_copy(b):
            return pltpu.make_async_copy(zbuf, xs_ref.at[pl.ds(pl.multiple_of(b * EB, EB), EB)], zsem)

        def start(b, carry):
            @pl.when(zf_ref[b] == 1)
            def _():
                zero_copy(b).start()
            return carry
        lax.fori_loop(0, zf_ref.shape[0], start, 0)

        def wait(b, carry):
            @pl.when(zf_ref[b] == 1)
            def _():
                zero_copy(b).wait()
            return carry
        lax.fori_loop(0, zf_ref.shape[0], wait, 0)

    pos_t = jnp.transpose(rt_ref[...])
    srow = lax.broadcasted_iota(jnp.int32, (SROWS, tm), 0).astype(F32)
    sel = jnp.where(srow == pos_t[2:3, :], 1.0, jnp.where(srow == pos_t[3:4, :], 1.0, 0.0)).astype(BF16)
    srt = jnp.dot(sel, hb_ref[...], preferred_element_type=F32)
    bits = pltpu.bitcast(srt, jnp.uint32)
    sbuf[slot] = (bits[:, d // 2:] & jnp.uint32(0xFFFF0000)) | (bits[:, :d // 2] >> 16)

    def chunk_copy(run_row, sorted_row, rows, sl):
        return pltpu.make_async_copy(sbuf.at[sl, pl.ds(sorted_row, rows)], xs_ref.at[pl.ds(run_row, rows)],
                                     sem.at[sl])

    _for_each_chunk(cur_ref, lambda run_row, sorted_row, rows: chunk_copy(run_row, sorted_row, rows, slot).start())

    @pl.when(i > 0)
    def _():
        _wait_chunks(prv_ref, lambda rows: chunk_copy(0, 0, rows, 1 - slot))

    @pl.when(i == nt - 1)
    def _():
        _wait_chunks(cur_ref, lambda rows: chunk_copy(0, 0, rows, slot))


def _dispatch(zero_blocks, runs, hb, rt, n_rows):
    n, d = hb.shape
    return pl.pallas_call(
        _dispatch_kernel,
        out_shape=jax.ShapeDtypeStruct((n_rows, d // 2), jnp.uint32),
        grid_spec=pltpu.PrefetchScalarGridSpec(
            num_scalar_prefetch=1,
            grid=(n // TM,),
            in_specs=[
                pl.BlockSpec((1, 1, LANES), lambda i, zf: (i, 0, 0), memory_space=pltpu.SMEM),
                pl.BlockSpec((1, 1, LANES), lambda i, zf: (jnp.maximum(i - 1, 0), 0, 0), memory_space=pltpu.SMEM),
                pl.BlockSpec((TM, d), lambda i, zf: (i, 0)),
                pl.BlockSpec((TM, LANES), lambda i, zf: (i, 0)),
            ],
            out_specs=pl.BlockSpec(memory_space=pl.ANY),
            scratch_shapes=[pltpu.VMEM((2, SROWS, d // 2), jnp.uint32), pltpu.VMEM((EB, d // 2), jnp.uint32),
                            pltpu.SemaphoreType.DMA((2,)), pltpu.SemaphoreType.DMA(())],
        ),
        compiler_params=_cparams(("arbitrary",)),
        name="dispatch",
    )(zero_blocks, runs, runs, hb, rt)


def _experts_kernel(be_ref, na_ref, nxt_ref, xs_ref, wg_hbm, wu_hbm, wd_hbm, ys_ref,
                    wgf, wuf, wdf, wgb, wub, wdb, sem):
    b = pl.program_id(0)

    def weight_copies(e):
        return (pltpu.make_async_copy(wg_hbm.at[e], wgf, sem.at[0]),
                pltpu.make_async_copy(wu_hbm.at[e], wuf, sem.at[1]),
                pltpu.make_async_copy(wd_hbm.at[e], wdf, sem.at[2]))

    @pl.when(b == 0)
    def _():
        for c in weight_copies(be_ref[0]):
            c.start()

    @pl.when(b < na_ref[0])
    def _():
        e = be_ref[b]
        changed = jnp.logical_or(b == 0, be_ref[jnp.maximum(b - 1, 0)] != e)

        @pl.when(changed)
        def _load():
            for c in weight_copies(e):
                c.wait()
            wgb[...] = wgf[...].astype(BF16)
            wub[...] = wuf[...].astype(BF16)
            wdb[...] = wdf[...].astype(BF16)
            nxt = nxt_ref[e]

            @pl.when(nxt >= 0)
            def _():
                for c in weight_copies(nxt):
                    c.start()

        w = xs_ref[...]
        x_lo = pltpu.bitcast(w << 16, F32).astype(BF16)
        x_hi = pltpu.bitcast(w & jnp.uint32(0xFFFF0000), F32).astype(BF16)
        dh = w.shape[1]
        g = (jnp.dot(x_lo, wgb[:dh, :], preferred_element_type=F32)
             + jnp.dot(x_hi, wgb[dh:, :], preferred_element_type=F32))
        u = (jnp.dot(x_lo, wub[:dh, :], preferred_element_type=F32)
             + jnp.dot(x_hi, wub[dh:, :], preferred_element_type=F32))
        hdn = g * (1.0 / (1.0 + jnp.exp(-g))) * u
        y = jnp.dot(hdn.astype(BF16), wdb[...], preferred_element_type=F32)
        bits = pltpu.bitcast(y.astype(BF16).astype(F32), jnp.uint32)
        ys_ref[...] = (bits[:, dh:] & jnp.uint32(0xFFFF0000)) | (bits[:, :dh] >> 16)

    @pl.when(b >= na_ref[0])
    def _():
        ys_ref[...] = jnp.zeros(ys_ref.shape, ys_ref.dtype)


def _experts(blk_e, n_act, nxt_e, xs, w_gate, w_up, w_down):
    p, dh = xs.shape
    d = 2 * dh
    de = w_gate.shape[2]

    def row_map(b, be, na, nx):
        return (jnp.minimum(b, na[0] - 1), 0)

    return pl.pallas_call(
        _experts_kernel,
        out_shape=jax.ShapeDtypeStruct((p, dh), jnp.uint32),
        grid_spec=pltpu.PrefetchScalarGridSpec(
            num_scalar_prefetch=3,
            grid=(p // EB,),
            in_specs=[
                pl.BlockSpec((EB, dh), row_map),
                pl.BlockSpec(memory_space=pl.ANY),
                pl.BlockSpec(memory_space=pl.ANY),
                pl.BlockSpec(memory_space=pl.ANY),
            ],
            out_specs=pl.BlockSpec((EB, dh), lambda b, be, na, nx: (b, 0)),
            scratch_shapes=[pltpu.VMEM((d, de), F32), pltpu.VMEM((d, de), F32), pltpu.VMEM((de, d), F32),
                            pltpu.VMEM((d, de), BF16), pltpu.VMEM((d, de), BF16), pltpu.VMEM((de, d), BF16),
                            pltpu.SemaphoreType.DMA((3,))],
        ),
        compiler_params=_cparams(("arbitrary",)),
        name="experts",
    )(blk_e, n_act, nxt_e, xs, w_gate, w_up, w_down)


def _combine_kernel(cur_ref, nxt_ref, ys_ref, h_ref, rt_ref, o_ref, ybuf, sem):
    i = pl.program_id(0)
    nt = pl.num_programs(0)
    slot = i % 2
    tm = h_ref.shape[0]

    def chunk_copy(run_row, sorted_row, rows, sl):
        return pltpu.make_async_copy(ys_ref.at[pl.ds(run_row, rows)], ybuf.at[sl, pl.ds(sorted_row, rows)],
                                     sem.at[sl])

    @pl.when(i == 0)
    def _():
        ybuf[...] = jnp.zeros(ybuf.shape, ybuf.dtype)
        _for_each_chunk(cur_ref, lambda run_row, sorted_row, rows: chunk_copy(run_row, sorted_row, rows, 0).start())

    @pl.when(i + 1 < nt)
    def _():
        _for_each_chunk(nxt_ref,
                        lambda run_row, sorted_row, rows: chunk_copy(run_row, sorted_row, rows, 1 - slot).start())

    _wait_chunks(cur_ref, lambda rows: chunk_copy(0, 0, rows, slot))

    rt = rt_ref[...]
    w = ybuf[slot]
    dh = w.shape[1]
    y_lo = pltpu.bitcast(w << 16, F32).astype(BF16)
    y_hi = pltpu.bitcast(w & jnp.uint32(0xFFFF0000), F32).astype(BF16)
    col = lax.broadcasted_iota(jnp.int32, (tm, SROWS), 1).astype(F32)
    w1 = jnp.where(col == rt[:, 2:3], 1.0, 0.0).astype(BF16)
    w2 = jnp.where(col == rt[:, 3:4], 1.0, 0.0).astype(BF16)
    for half, yb in ((slice(0, dh), y_lo), (slice(dh, 2 * dh), y_hi)):
        o_ref[:, half] = (h_ref[:, half]
                          + rt[:, 0:1] * jnp.dot(w1, yb, preferred_element_type=F32)
                          + rt[:, 1:2] * jnp.dot(w2, yb, preferred_element_type=F32))


def _combine(runs, ys, h1, rt):
    n, d = h1.shape
    nt = n // TM
    return pl.pallas_call(
        _combine_kernel,
        out_shape=jax.ShapeDtypeStruct((n, d), F32),
        grid=(nt,),
        in_specs=[
            pl.BlockSpec((1, 1, LANES), lambda i: (i, 0, 0), memory_space=pltpu.SMEM),
            pl.BlockSpec((1, 1, LANES), lambda i: (jnp.minimum(i + 1, nt - 1), 0, 0), memory_space=pltpu.SMEM),
            pl.BlockSpec(memory_space=pl.ANY),
            pl.BlockSpec((TM, d), lambda i: (i, 0)),
            pl.BlockSpec((TM, LANES), lambda i: (i, 0)),
        ],
        out_specs=pl.BlockSpec((TM, d), lambda i: (i, 0)),
        scratch_shapes=[pltpu.VMEM((2, SROWS, d // 2), jnp.uint32), pltpu.SemaphoreType.DMA((2,))],
        compiler_params=_cparams(("arbitrary",)),
        name="combine",
    )(runs, runs, ys, h1, rt)


def kernel(x, meta_tokens, rel_bias, norm1_gain, w_in, diff_q_gain, diff_k_gain, lam_q1, lam_k1, lam_q2, lam_k2, diff_subln_gain, swa_q_gain, swa_k_gain, swa_sinks, w_out, norm2_gain, w_group, b_group, w_router, b_router, w_gate, w_up, w_down):
    batch, seq, d = x.shape
    depth = w_in.shape[0]
    n = batch * seq
    assert seq % CQ == 0 and seq % TQ == 0 and n % TM == 0 and d == 1024
    assert meta_tokens.shape[0] == N_META
    assert depth == 1, "the meta-token rows of the residual stream are not carried across layers"

    h = x.reshape(n, d)
    dblk, bm0, bt = _bias_tables(rel_bias, TQ)
    scale = HEAD_DIM ** -0.5
    bd = jnp.asarray(np.kron(np.eye(MXU_DIM // HEAD_DIM), np.full((HEAD_DIM, HEAD_DIM), 1.0 / HEAD_DIM)), BF16)
    ones = jnp.ones((HEAD_DIM,), F32)
    lower_pad = N_EXPERTS + N_GROUPS

    for layer in range(depth):
        lambda_init = 0.8 - 0.6 * math.exp(-0.3 * layer)
        w_cat = w_in[layer].astype(BF16)
        gain = jnp.concatenate([
            jnp.tile(diff_q_gain[layer] * scale, 2 * N_DIFF_HEADS),
            jnp.tile(diff_k_gain[layer], 2 * N_DIFF_HEADS),
            jnp.tile(ones, 2 * N_DIFF_HEADS),
            jnp.tile(swa_q_gain[layer] * scale, N_SWA_HEADS),
            jnp.tile(swa_k_gain[layer], N_SWA_KV),
            jnp.tile(ones, N_SWA_KV)]).reshape(1, C_END).astype(F32)
        nmask = np.zeros((1, C_END), np.float32)
        nmask[:, C_DQ:C_DV] = 1.0
        nmask[:, C_SQ:C_SV] = 1.0
        nmask = jnp.asarray(nmask)
        g1 = norm1_gain[layer].reshape(1, d).astype(F32)

        qkv = _proj(h, g1, w_cat, bd, gain, nmask, TP)
        qkv_meta = _proj(meta_tokens.astype(F32), g1, w_cat, bd, gain, nmask, N_META)
        meta_pad = jnp.pad(qkv_meta, ((0, TQ - N_META), (0, 0)))

        lamv = jnp.pad(jnp.stack([lam_q1[layer], lam_k1[layer], lam_q2[layer], lam_k2[layer]]).astype(F32),
                       ((0, 4), (0, LANES - HEAD_DIM)))
        mixd = _diff_attention(qkv, meta_pad[:, C_DK:C_DV], meta_pad[:, C_DV:C_SQ], dblk, bm0, lamv,
                               diff_subln_gain[layer].reshape(1, LANES).astype(F32), batch, seq, lambda_init)
        mixs = _swa_attention(swa_sinks[layer].astype(F32), qkv, meta_pad[:BLOCK, C_SK:C_SV],
                              meta_pad[:BLOCK, C_SV:C_END], jnp.swapaxes(bt, -1, -2), batch, seq)

        wr = jnp.pad(jnp.concatenate([w_router[layer], w_group[layer]], axis=1),
                     ((0, 0), (0, LANES - lower_pad))).astype(BF16)
        br = jnp.pad(jnp.concatenate([b_router[layer], b_group[layer]]), (0, LANES - lower_pad)).reshape(1, LANES)
        h1, hb, rt, tinfo, cnt = _outproj(h, mixd, mixs, w_out[layer].astype(BF16),
                                          norm2_gain[layer].reshape(1, d).astype(F32), wr, br.astype(F32))

        nt = n // TM
        counts = cnt[0, :N_EXPERTS].astype(jnp.int32)
        nblk_e = (counts + EB - 1) // EB
        blk_end = jnp.cumsum(nblk_e)
        pstart = ((blk_end - nblk_e) * EB).astype(jnp.int32)
        n_blocks = -(-(2 * n + nt * N_EXPERTS * (ROW_ALIGN - 1) + N_EXPERTS * (EB - 1)) // EB)
        blk_ids = jnp.arange(n_blocks)
        blk_e = jnp.minimum(jnp.sum(blk_end[None, :] <= blk_ids[:, None], axis=1), N_EXPERTS - 1).astype(jnp.int32)
        n_act = blk_end[-1:].astype(jnp.int32)
        is_last = jnp.any((blk_end[None, :] == blk_ids[:, None] + 1) & (nblk_e[None, :] > 0), axis=1)
        zero_blocks = ((blk_ids >= n_act[0]) | is_last).astype(jnp.int32)
        ti = tinfo.reshape(nt, 8, LANES)
        run_len = ti[:, 0, :N_EXPERTS].astype(jnp.int32)
        run_start = pstart[None, :] + ti[:, 1, :N_EXPERTS].astype(jnp.int32)
        run_groups = (run_len + ROW_ALIGN - 1) // ROW_ALIGN
        per_chunk = CHUNK // ROW_ALIGN
        runs = jnp.concatenate([run_start, run_groups,
                                jnp.sum(run_groups // per_chunk, axis=1, keepdims=True),
                                jnp.sum(run_groups % per_chunk, axis=1, keepdims=True),
                                jnp.zeros((nt, LANES - 2 * N_EXPERTS - 2), jnp.int32)],
                               axis=1).reshape(nt, 1, LANES)

        xs = _dispatch(zero_blocks, runs, hb, rt, n_blocks * EB)
        own = jnp.where(nblk_e > 0, jnp.arange(N_EXPERTS), N_EXPERTS)
        later = jnp.concatenate([lax.cummin(own[::-1])[::-1][1:], jnp.full((1,), N_EXPERTS)])
        nxt_e = jnp.where(later < N_EXPERTS, later, -1).astype(jnp.int32)
        ys = _experts(blk_e, n_act, nxt_e, xs, w_gate[layer], w_up[layer], w_down[layer])
        h = _combine(runs, ys, h1, rt)
    return h.reshape(batch, seq, d)
```

```python
import functools
import math

import numpy as np
import jax
import jax.numpy as jnp
from jax import lax
from jax.experimental import pallas as pl
from jax.experimental.pallas import tpu as pltpu

F32 = jnp.float32
BF16 = jnp.bfloat16

HEAD_DIM = 64
N_DIFF_HEADS = 4
N_SWA_HEADS = 8
N_SWA_KV = 2
BLOCK = 128
N_META = 16
N_BUCKETS = 32
MAX_DISTANCE = 128
N_GROUPS = 4
EXPERTS_PER_GROUP = 8
N_EXPERTS = N_GROUPS * EXPERTS_PER_GROUP
D_EXPERT = 512
EPS = 1e-6
NEG = -1e30

LANES = 128
MXU_DIM = 256
VMEM_LIMIT = 48 * 1024 * 1024

TP = 512
TM = 256
TO = 512
TQ = 256
ONES_ROWS = 16
ACC_BUFS = 4
CQ = 512
EB = 256
ROW_ALIGN = 8
CHUNK = 16
SROWS = -(-(2 * TM + N_EXPERTS * (ROW_ALIGN - 1)) // MXU_DIM) * MXU_DIM

C_DQ, C_DK, C_DV, C_SQ, C_SK, C_SV, C_END = 0, 512, 1024, 1536, 2048, 2176, 2304
NORM_GROUPS = (0, 1, 2, 3, 6, 7, 8)


def _cparams(sem):
    return pltpu.CompilerParams(dimension_semantics=sem, vmem_limit_bytes=VMEM_LIMIT)


def _t5_bucket_np(dist):
    n = np.maximum(dist, 0)
    max_exact = N_BUCKETS // 2
    nf = np.maximum(n, 1).astype(np.float32)
    large = max_exact + (np.log(nf / np.float32(max_exact)) / np.float32(math.log(MAX_DISTANCE / max_exact))
                         * np.float32(N_BUCKETS - max_exact)).astype(np.int32)
    large = np.minimum(large, N_BUCKETS - 1)
    return np.where(n < max_exact, n, large)


def _bias_tables(rel_bias, tq):
    nd = 2 * BLOCK
    buckets = _t5_bucket_np(np.arange(nd))
    assert (buckets[MAX_DISTANCE:] == N_BUCKETS - 1).all()
    rb = rel_bias.astype(F32)
    r = np.arange(BLOCK)[:, None]
    c = np.arange(BLOCK)[None, :]
    d_own = r - c
    d_prev = BLOCK + r - c
    far = rb[N_BUCKETS - 1]

    def take(dist):
        idx = jnp.asarray(buckets[np.clip(dist, 0, nd - 1)], jnp.int32)[None]
        out = jnp.zeros((rb.shape[1],) + dist.shape, F32)
        for b in range(N_BUCKETS):
            out = jnp.where(idx == b, rb[b].reshape((-1,) + (1,) * dist.ndim), out)
        return out

    hd = slice(0, N_DIFF_HEADS)
    far_d = far[hd][:, None, None]
    d0 = jnp.where(d_own[None] >= 0, take(d_own)[hd] - far_d, NEG)
    d1 = take(d_prev)[hd] - far_d
    dblk = jnp.stack([d0, d1], axis=1)
    rq = np.arange(tq)[:, None]
    cm = np.arange(LANES)[None, :]
    d_meta = N_META + rq - cm
    bm0 = jnp.where((cm < N_META)[None], take(d_meta)[hd] - far_d, NEG)

    hs = slice(N_DIFF_HEADS, N_DIFF_HEADS + N_SWA_HEADS)
    far_s = far[hs][:, None, None]
    d_meta_s = N_META + r - cm
    meta_first = jnp.where((cm < N_META)[None], take(d_meta_s)[hs], NEG)
    meta_rest = jnp.where((cm < N_META)[None], jnp.broadcast_to(far_s, (N_SWA_HEADS, BLOCK, LANES)), NEG)
    prev_rest = jnp.where((c > r)[None], take(d_prev)[hs], NEG)
    prev_first = jnp.full((N_SWA_HEADS, BLOCK, BLOCK), NEG, F32)
    own = jnp.where((d_own >= 0)[None], take(d_own)[hs], NEG)
    bt = jnp.stack([jnp.concatenate([meta_first, prev_first, own], axis=-1),
                    jnp.concatenate([meta_rest, prev_rest, own], axis=-1)], axis=0)
    return dblk.astype(F32), bm0.astype(F32), bt.astype(F32)


def _proj_kernel(x_ref, g1_ref, w_ref, bd_ref, gain_ref, nmask_ref, o_ref):
    x = x_ref[...]
    a = x * lax.rsqrt(jnp.mean(x * x, axis=-1, keepdims=True) + EPS) * g1_ref[...]
    p = jnp.dot(a.astype(BF16), w_ref[...], preferred_element_type=F32)
    bd = bd_ref[...]
    for j in range(C_END // MXU_DIM):
        sl = slice(j * MXU_DIM, (j + 1) * MXU_DIM)
        pj = p[:, sl]
        if j in NORM_GROUPS:
            ms = jnp.dot((pj * pj).astype(BF16), bd, preferred_element_type=F32)
            pj = jnp.where(nmask_ref[:, sl] != 0.0, pj * lax.rsqrt(ms + EPS) * gain_ref[:, sl], pj)
        o_ref[:, sl] = pj.astype(BF16)


def _proj(x2, g1, w, bd, gain, nmask, tm):
    n = x2.shape[0]
    return pl.pallas_call(
        _proj_kernel,
        out_shape=jax.ShapeDtypeStruct((n, C_END), BF16),
        grid=(n // tm,),
        in_specs=[
            pl.BlockSpec((tm, x2.shape[1]), lambda i: (i, 0)),
            pl.BlockSpec(g1.shape, lambda i: (0, 0)),
            pl.BlockSpec(w.shape, lambda i: (0, 0)),
            pl.BlockSpec(bd.shape, lambda i: (0, 0)),
            pl.BlockSpec(gain.shape, lambda i: (0, 0)),
            pl.BlockSpec(nmask.shape, lambda i: (0, 0)),
        ],
        out_specs=pl.BlockSpec((tm, C_END), lambda i: (i, 0)),
        compiler_params=_cparams(("parallel",)),
        name="proj",
    )(x2, g1, w, bd, gain, nmask)


def _diff_kernel(qi_tab, t_tab, q_ref, k_ref, v_ref, km_ref, vm_ref, d_ref, bm0_ref, lamv_ref, gain_ref, o_ref,
                 bias_ref, qs_ref, vt_ref, s_buf, p_buf, a_buf, m_ref, acc_ref, *, lambda_init, n_steps):
    tq = TQ
    nq = q_ref.shape[0] // tq
    nb = tq // BLOCK
    BIAS_NONE, BIAS_LEFT, BIAS_DIAG, BIAS_META, BIAS_META0 = 0, 1, 2, 3, 4

    d0 = d_ref[0, 0]
    d1 = d_ref[0, 1]
    zeros = jnp.zeros((BLOCK, BLOCK), F32)
    bias_ref[BIAS_NONE] = jnp.zeros((tq, tq), F32)
    for a in range(nb):
        for b in range(nb):
            rs, cs = slice(a * BLOCK, (a + 1) * BLOCK), slice(b * BLOCK, (b + 1) * BLOCK)
            if a == b:
                blk = d0
            elif b == a + 1:
                blk = d1
            elif b > a:
                blk = zeros
            else:
                blk = jnp.full((BLOCK, BLOCK), NEG, F32)
            bias_ref[BIAS_DIAG, rs, cs] = blk
            bias_ref[BIAS_LEFT, rs, cs] = d1 if (b == 0 and a == nb - 1) else zeros
    row_m = lax.broadcasted_iota(jnp.int32, (tq, tq), 0)
    bias_ref[BIAS_META] = jnp.where(row_m < N_META, 0.0, NEG).astype(F32)
    bias_ref[BIAS_META0, :LANES, :] = bm0_ref[0]
    bias_ref[BIAS_META0, LANES:, :] = jnp.full((tq - LANES, tq), NEG, F32)

    lane = lax.broadcasted_iota(jnp.int32, (tq, LANES), 1)
    for i in range(nq):
        rows = slice(i * tq, (i + 1) * tq)
        q = q_ref[rows, :]
        zero = jnp.zeros_like(q)
        qs_ref[i] = jnp.concatenate([jnp.where(lane < HEAD_DIM, q, zero),
                                     jnp.where(lane >= HEAD_DIM, q, zero)], axis=0)
        vt_ref[i, :LANES, :] = jnp.transpose(v_ref[rows, :].astype(F32)).astype(BF16)
    vt_ref[nq, :LANES, :] = jnp.transpose(vm_ref[...].astype(F32)).astype(BF16)
    vt_ref[:, LANES:, :] = jnp.ones((nq + 1, ONES_ROWS, tq), BF16)
    acc_ref[...] = jnp.zeros(acc_ref.shape, F32)
    m_ref[...] = jnp.full(m_ref.shape, NEG, F32)
    lv = lamv_ref[...]
    lam = (jnp.exp(jnp.sum(lv[0:1] * lv[1:2], axis=-1, keepdims=True))
           - jnp.exp(jnp.sum(lv[2:3] * lv[3:4], axis=-1, keepdims=True)) + lambda_init)

    def seq_row(t):
        return pl.multiple_of(jnp.maximum(t - 1, 0) * tq, tq)

    def stage_a(n, slot):
        qi, t = qi_tab[n], t_tab[n]
        kt = jnp.where(t == 0, km_ref[...], k_ref[pl.ds(seq_row(t), tq), :])
        s = lax.dot_general(kt, qs_ref[qi], (((1,), (1,)), ((), ())), preferred_element_type=F32)
        which = jnp.where(t == 0, jnp.where(qi == 0, BIAS_META0, BIAS_META),
                          jnp.where(t == qi + 1, BIAS_DIAG, jnp.where(t == qi, BIAS_LEFT, BIAS_NONE)))
        s_buf[slot] = s + jnp.tile(bias_ref[which], (1, 2))

    def stage_b(n, slot):
        s = s_buf[slot]
        m_prev = jnp.where(t_tab[n] == 0, NEG, m_ref[...])
        m_new = jnp.maximum(m_prev, jnp.max(s, axis=0, keepdims=True))
        a_buf[slot] = jnp.exp(m_prev - m_new)
        p_buf[slot] = jnp.exp(s - m_new[0:1]).astype(BF16)
        m_ref[...] = m_new

    def stage_c(n, slot):
        qi, t = qi_tab[n], t_tab[n]
        par = qi % ACC_BUFS
        vt = vt_ref[jnp.where(t == 0, nq, t - 1)]
        pv = jnp.dot(vt, p_buf[slot], preferred_element_type=F32)
        acc_ref[par] = a_buf[slot][0:1] * acc_ref[par] + pv

    def finish(n):
        qi, t = qi_tab[n], t_tab[n]

        @pl.when(t == qi + 1)
        def _():
            acc = acc_ref[qi % ACC_BUFS]
            o = acc[:LANES] / acc[LANES:LANES + 1]
            d = o[:, :tq] - lam * o[:, tq:]
            y = d * lax.rsqrt(jnp.mean(d * d, axis=0, keepdims=True) + EPS) * jnp.tile(gain_ref[...], (1, tq // LANES))
            y = jnp.transpose(y * (1.0 - lambda_init))
            o_ref[pl.ds(pl.multiple_of(qi * tq, tq), tq), :] = y.astype(BF16)

    stage_a(0, 0)
    stage_a(1, 1)
    stage_b(0, 0)

    def steps(n, count):
        for j in range(count):
            stage_a(n + j + 2, j % 2)
            stage_b(n + j + 1, (j + 1) % 2)
            stage_c(n + j, j % 2)
        for j in range(count):
            finish(n + j)

    unroll = 8
    assert unroll <= 3 * (ACC_BUFS - 1) + 1
    n_blocks = (n_steps - 2) // unroll

    def block(k, carry):
        steps(unroll * k, unroll)
        return carry
    lax.fori_loop(0, n_blocks, block, 0)
    steps(unroll * n_blocks, n_steps - 2 - unroll * n_blocks)

    stage_b(n_steps - 1, 1)
    stage_c(n_steps - 2, 0)
    stage_c(n_steps - 1, 1)
    finish(n_steps - 2)
    finish(n_steps - 1)


def _diff_attention(qkv, km, vm, dblk, bm0, lamv, gain, batch, seq, lambda_init):
    nq = seq // TQ
    steps = [(qi, t) for qi in range(nq) for t in range(qi + 2)]
    assert len(steps) % 2 == 0
    qi_tab = jnp.asarray([s[0] for s in steps], jnp.int32)
    t_tab = jnp.asarray([s[1] for s in steps], jnp.int32)
    kern = functools.partial(_diff_kernel, lambda_init=lambda_init, n_steps=len(steps))
    return pl.pallas_call(
        kern,
        out_shape=jax.ShapeDtypeStruct((batch * seq, N_DIFF_HEADS * LANES), BF16),
        grid_spec=pltpu.PrefetchScalarGridSpec(
            num_scalar_prefetch=2,
            grid=(batch, N_DIFF_HEADS),
            in_specs=[
                pl.BlockSpec((seq, LANES), lambda b, h, *_: (b, C_DQ // LANES + h)),
                pl.BlockSpec((seq, LANES), lambda b, h, *_: (b, C_DK // LANES + h)),
                pl.BlockSpec((seq, LANES), lambda b, h, *_: (b, C_DV // LANES + h)),
                pl.BlockSpec((TQ, LANES), lambda b, h, *_: (0, h)),
                pl.BlockSpec((TQ, LANES), lambda b, h, *_: (0, h)),
                pl.BlockSpec((1, 2, BLOCK, BLOCK), lambda b, h, *_: (h, 0, 0, 0)),
                pl.BlockSpec((1, LANES, TQ), lambda b, h, *_: (h, 0, 0)),
                pl.BlockSpec(lamv.shape, lambda b, h, *_: (0, 0)),
                pl.BlockSpec((LANES, LANES), lambda b, h, *_: (0, 0)),
            ],
            out_specs=pl.BlockSpec((seq, LANES), lambda b, h, *_: (b, h)),
            scratch_shapes=[
                pltpu.VMEM((5, TQ, TQ), F32),
                pltpu.VMEM((nq, 2 * TQ, LANES), BF16),
                pltpu.VMEM((nq + 1, LANES + ONES_ROWS, TQ), BF16),
                pltpu.VMEM((2, TQ, 2 * TQ), F32),
                pltpu.VMEM((2, TQ, 2 * TQ), BF16),
                pltpu.VMEM((2, 8, 2 * TQ), F32),
                pltpu.VMEM((8, 2 * TQ), F32),
                pltpu.VMEM((ACC_BUFS, LANES + ONES_ROWS, 2 * TQ), F32),
            ],
        ),
        compiler_params=_cparams(("parallel", "parallel")),
        name="diff_attention",
    )(qi_tab, t_tab, qkv, qkv, qkv, km, vm, jnp.swapaxes(dblk, -1, -2), jnp.swapaxes(bm0, -1, -2), lamv,
      jnp.broadcast_to(gain.reshape(LANES, 1), (LANES, LANES)))


def _swa_kernel(sink_ref, q_ref, k_ref, v_ref, km_ref, vm_ref, bt_ref, o_ref, kd_ref, vt_ref,
                s_scr, p_scr, inv_scr):
    ci = pl.program_id(1)
    nblk = q_ref.shape[0] // BLOCK
    nkb = k_ref.shape[0] // BLOCK
    lane = lax.broadcasted_iota(jnp.int32, (BLOCK, LANES), 1)

    def both_halves(k):
        k0, k1 = k[:, :HEAD_DIM], k[:, HEAD_DIM:]
        return jnp.concatenate([k0, k0, k1, k1], axis=1)

    @pl.when(ci == 0)
    def _per_batch():
        def body(j, carry):
            rows = pl.ds(pl.multiple_of(j * BLOCK, BLOCK), BLOCK)
            kd_ref[j] = both_halves(k_ref[rows, :])
            vt_ref[j] = jnp.transpose(v_ref[rows, :].astype(F32)).astype(BF16)
            return carry
        lax.fori_loop(0, nkb, body, 0)
        kd_ref[nkb] = both_halves(km_ref[...])
        vt_ref[nkb] = jnp.transpose(vm_ref[...].astype(F32)).astype(BF16)

    def block_body(n, carry):
        gblk = ci * nblk + n
        first = jnp.where(gblk == 0, 0, 1)
        prev = jnp.maximum(gblk - 1, 0)
        r_q = pl.multiple_of(n * BLOCK, BLOCK)
        pairs = [(g, u) for g in range(N_SWA_KV) for u in range(2)]
        for c, (g, u) in enumerate(pairs):
            ks = slice(g * LANES, (g + 1) * LANES)
            kcat = jnp.concatenate([kd_ref[nkb, :, ks], kd_ref[prev, :, ks], kd_ref[gblk, :, ks]], axis=0)
            h0 = 4 * g + 2 * u
            qp = q_ref[pl.ds(r_q, BLOCK), (2 * g + u) * LANES:(2 * g + u + 1) * LANES]
            zero = jnp.zeros_like(qp)
            qs = jnp.concatenate([jnp.where(lane < HEAD_DIM, qp, zero),
                                  jnp.where(lane >= HEAD_DIM, qp, zero)], axis=0)
            s = lax.dot_general(kcat, qs, (((1,), (1,)), ((), ())), preferred_element_type=F32)
            s_scr[c] = s + jnp.concatenate([bt_ref[first, h0], bt_ref[first, h0 + 1]], axis=1)
        for c, (g, u) in enumerate(pairs):
            h0 = 4 * g + 2 * u
            s = s_scr[c]
            sink = jnp.concatenate([sink_ref[h0:h0 + 1, :], sink_ref[h0 + 1:h0 + 2, :]], axis=1)
            m = jnp.maximum(jnp.max(s, axis=0, keepdims=True), sink)
            p = jnp.exp(s - m)
            p_scr[c] = p.astype(BF16)
            inv_scr[c] = jnp.broadcast_to(1.0 / (jnp.sum(p, axis=0, keepdims=True) + jnp.exp(sink - m)),
                                          inv_scr.shape[1:])
        for c, (g, u) in enumerate(pairs):
            vs = slice(g * HEAD_DIM, (g + 1) * HEAD_DIM)
            vcat = jnp.concatenate([vt_ref[nkb, vs, :], vt_ref[prev, vs, :], vt_ref[gblk, vs, :]], axis=1)
            o = jnp.dot(vcat, p_scr[c], preferred_element_type=F32) * inv_scr[c][0:1]
            ot = jnp.transpose(o)
            o_ref[pl.ds(r_q, BLOCK), (2 * g + u) * LANES:(2 * g + u + 1) * LANES] = (
                jnp.concatenate([ot[:BLOCK], ot[BLOCK:]], axis=1).astype(BF16))
        return carry

    lax.fori_loop(0, nblk, block_body, 0)


def _swa_attention(sinks, qkv, km, vm, bt, batch, seq):
    nc = seq // CQ
    nkb = seq // BLOCK
    sinkv = jnp.broadcast_to(sinks.reshape(N_SWA_HEADS, 1), (N_SWA_HEADS, LANES))
    return pl.pallas_call(
        _swa_kernel,
        out_shape=jax.ShapeDtypeStruct((batch * seq, N_SWA_HEADS * HEAD_DIM), BF16),
        grid=(batch, nc),
        in_specs=[
            pl.BlockSpec(sinkv.shape, lambda b, c: (0, 0)),
            pl.BlockSpec((CQ, 512), lambda b, c: (b * nc + c, C_SQ // 512)),
            pl.BlockSpec((seq, LANES), lambda b, c: (b, C_SK // LANES)),
            pl.BlockSpec((seq, LANES), lambda b, c: (b, C_SV // LANES)),
            pl.BlockSpec(km.shape, lambda b, c: (0, 0)),
            pl.BlockSpec(vm.shape, lambda b, c: (0, 0)),
            pl.BlockSpec(bt.shape, lambda b, c: (0, 0, 0, 0)),
        ],
        out_specs=pl.BlockSpec((CQ, 512), lambda b, c: (b * nc + c, 0)),
        scratch_shapes=[pltpu.VMEM((nkb + 1, BLOCK, 2 * LANES), BF16),
                        pltpu.VMEM((nkb + 1, LANES, BLOCK), BF16),
                        pltpu.VMEM((4, 3 * BLOCK, 2 * BLOCK), F32),
                        pltpu.VMEM((4, 3 * BLOCK, 2 * BLOCK), BF16),
                        pltpu.VMEM((4, 8, 2 * BLOCK), F32)],
        compiler_params=_cparams(("parallel", "arbitrary")),
        name="swa_attention",
    )(sinkv, qkv, qkv, qkv, km, vm, bt)


def _outproj_kernel(x_ref, md_ref, ms_ref, wo_ref, g2_ref, wr_ref, br_ref,
                    h_ref, hb_ref, rt_ref, ti_ref, cnt_ref, c_ref):
    i = pl.program_id(0)

    @pl.when(i == 0)
    def _init():
        c_ref[...] = jnp.zeros(c_ref.shape, F32)

    for sub in range(x_ref.shape[0] // TM):
        _route_tile(slice(sub * TM, (sub + 1) * TM), slice(sub * 8, (sub + 1) * 8),
                    x_ref, md_ref, ms_ref, wo_ref, g2_ref, wr_ref, br_ref, h_ref, hb_ref, rt_ref, ti_ref, c_ref)

    @pl.when(i == pl.num_programs(0) - 1)
    def _fin():
        cnt_ref[...] = c_ref[...]


def _route_tile(rows, ti_rows, x_ref, md_ref, ms_ref, wo_ref, g2_ref, wr_ref, br_ref,
                h_ref, hb_ref, rt_ref, ti_ref, c_ref):
    tm = TM
    half = md_ref.shape[1]
    h = (x_ref[rows, :]
         + jnp.dot(md_ref[rows, :], wo_ref[:half, :], preferred_element_type=F32)
         + jnp.dot(ms_ref[rows, :], wo_ref[half:, :], preferred_element_type=F32))
    h_ref[rows, :] = h
    hn = h * lax.rsqrt(jnp.mean(h * h, axis=-1, keepdims=True) + EPS) * g2_ref[...]
    hb = hn.astype(BF16)
    hb_ref[rows, :] = hb

    lg = jnp.dot(hb, wr_ref[...], preferred_element_type=F32) + br_ref[...]
    lane_i = lax.broadcasted_iota(jnp.int32, lg.shape, 1)
    lane = lane_i.astype(F32)
    big = float(4 * LANES)
    is_g = (lane_i >= N_EXPERTS) & (lane_i < N_EXPERTS + N_GROUPS)
    glm = jnp.where(is_g, lg, -jnp.inf)
    gmax = jnp.max(glm, axis=1, keepdims=True)
    gidx = jnp.min(jnp.where(glm == gmax, lane, big), axis=1, keepdims=True) - N_EXPERTS
    gsum = jnp.sum(jnp.where(is_g, jnp.exp(lg - gmax), 0.0), axis=1, keepdims=True)
    g_w = 1.0 / gsum
    lane_grp = (lane_i >> 3).astype(F32)
    in_grp = (lane_i < N_EXPERTS) & (lane_grp == gidx)
    el = jnp.where(in_grp, lg, -jnp.inf)
    t1 = jnp.max(el, axis=1, keepdims=True)
    j1 = jnp.min(jnp.where(el == t1, lane, big), axis=1, keepdims=True)
    el2 = jnp.where(lane == j1, -jnp.inf, el)
    t2 = jnp.max(el2, axis=1, keepdims=True)
    j2 = jnp.min(jnp.where(el2 == t2, lane, big), axis=1, keepdims=True)
    e2 = jnp.exp(t2 - t1)
    den = 1.0 + e2
    gate1 = g_w / den
    gate2 = g_w * e2 / den

    o1 = lane == j1
    o2 = lane == j2
    onehot = jnp.where(o1 | o2, 1.0, 0.0).astype(BF16)
    rr = lax.broadcasted_iota(jnp.int32, (tm, tm), 0)
    cc = lax.broadcasted_iota(jnp.int32, (tm, tm), 1)
    lower = jnp.where(rr > cc, 1.0, 0.0).astype(BF16)
    pfx = jnp.dot(lower, onehot, preferred_element_type=F32)
    cnt_tile = jnp.sum(onehot.astype(F32), axis=0, keepdims=True)
    groups = jnp.floor((cnt_tile + (ROW_ALIGN - 1)) * (1.0 / ROW_ALIGN))
    er = lax.broadcasted_iota(jnp.int32, (LANES, LANES), 0)
    ec = lax.broadcasted_iota(jnp.int32, (LANES, LANES), 1)
    before = jnp.where(er < ec, 1.0, 0.0).astype(BF16)
    cbase = ROW_ALIGN * jnp.dot(jnp.broadcast_to(groups, (8, LANES)).astype(BF16), before,
                                preferred_element_type=F32)[0:1]
    at = pfx + cbase
    pos1 = jnp.sum(jnp.where(o1, at, 0.0), axis=1, keepdims=True)
    pos2 = jnp.sum(jnp.where(o2, at, 0.0), axis=1, keepdims=True)
    rt_ref[rows, :] = jnp.where(lane_i == 0, gate1,
                                jnp.where(lane_i == 1, gate2,
                                          jnp.where(lane_i == 2, pos1,
                                                    jnp.where(lane_i == 3, pos2, 0.0))))
    c_old = c_ref[...]
    c_ref[...] = c_old + groups * ROW_ALIGN
    row8 = lax.broadcasted_iota(jnp.int32, (8, LANES), 0)
    ti_ref[ti_rows, :] = jnp.where(row8 == 0, cnt_tile, jnp.where(row8 == 1, c_old, 0.0))


def _outproj(x2, mixd, mixs, wo, g2, wr, br):
    n, d = x2.shape
    sub = TO // TM
    return pl.pallas_call(
        _outproj_kernel,
        out_shape=(jax.ShapeDtypeStruct((n, d), F32),
                   jax.ShapeDtypeStruct((n, d), BF16),
                   jax.ShapeDtypeStruct((n, LANES), F32),
                   jax.ShapeDtypeStruct((n // TM * 8, LANES), F32),
                   jax.ShapeDtypeStruct((8, LANES), F32)),
        grid=(n // TO,),
        in_specs=[
            pl.BlockSpec((TO, d), lambda i: (i, 0)),
            pl.BlockSpec((TO, mixd.shape[1]), lambda i: (i, 0)),
            pl.BlockSpec((TO, mixs.shape[1]), lambda i: (i, 0)),
            pl.BlockSpec(wo.shape, lambda i: (0, 0)),
            pl.BlockSpec(g2.shape, lambda i: (0, 0)),
            pl.BlockSpec(wr.shape, lambda i: (0, 0)),
            pl.BlockSpec(br.shape, lambda i: (0, 0)),
        ],
        out_specs=(pl.BlockSpec((TO, d), lambda i: (i, 0)),
                   pl.BlockSpec((TO, d), lambda i: (i, 0)),
                   pl.BlockSpec((TO, LANES), lambda i: (i, 0)),
                   pl.BlockSpec((8 * sub, LANES), lambda i: (i, 0)),
                   pl.BlockSpec((8, LANES), lambda i: (0, 0))),
        scratch_shapes=[pltpu.VMEM((8, LANES), F32)],
        compiler_params=_cparams(("arbitrary",)),
        name="outproj_router",
    )(x2, mixd, mixs, wo, g2, wr, br)


def _for_each_chunk(runs_ref, fn):
    def per_expert(e, sorted_row, priority):
        start = runs_ref[0, 0, e]
        groups = runs_ref[0, 0, N_EXPERTS + e]
        whole = groups // (CHUNK // ROW_ALIGN)

        def per_chunk(c, carry):
            fn(pl.multiple_of(start + c * CHUNK, ROW_ALIGN), pl.multiple_of(sorted_row + c * CHUNK, ROW_ALIGN),
               CHUNK, priority)
            return carry
        lax.fori_loop(0, whole, per_chunk, 0)

        @pl.when(groups % (CHUNK // ROW_ALIGN) == 1)
        def _():
            fn(pl.multiple_of(start + whole * CHUNK, ROW_ALIGN),
               pl.multiple_of(sorted_row + whole * CHUNK, ROW_ALIGN), ROW_ALIGN, priority)
        return sorted_row + groups * ROW_ALIGN

    def expert_pair(e2, sorted_row):
        return per_expert(2 * e2 + 1, per_expert(2 * e2, sorted_row, 0), 1)
    lax.fori_loop(0, N_EXPERTS // 2, expert_pair, 0)


def _wait_chunks(runs_ref, make_copy):
    for k, rows in enumerate((CHUNK, ROW_ALIGN)):
        def body(c, carry, rows=rows):
            make_copy(rows).wait()
            return carry
        lax.fori_loop(0, runs_ref[0, 0, 2 * N_EXPERTS + k], body, 0)


def _dispatch_kernel(zf_ref, cur_ref, prv_ref, hb_ref, rt_ref, xs_ref, sbuf, zbuf, sem, zsem):
    i = pl.program_id(0)
    nt = pl.num_programs(0)
    slot = i % 2
    tm, d = hb_ref.shape

    @pl.when(i == 0)
    def _():
        zbuf[...] = jnp.zeros(zbuf.shape, zbuf.dtype)

        def zero_copy(b):
            return pltpu.make_async_copy(zbuf, xs_ref.at[pl.ds(pl.multiple_of(b * EB, EB), EB)], zsem)

        def start(b, carry):
            @pl.when(zf_ref[b] == 1)
            def _():
                zero_copy(b).start()
            return carry
        lax.fori_loop(0, zf_ref.shape[0], start, 0)

        def wait(b, carry):
            @pl.when(zf_ref[b] == 1)
            def _():
                zero_copy(b).wait()
            return carry
        lax.fori_loop(0, zf_ref.shape[0], wait, 0)

    pos_t = jnp.transpose(rt_ref[...])
    srow = lax.broadcasted_iota(jnp.int32, (SROWS, tm), 0).astype(F32)
    sel = jnp.where(srow == pos_t[2:3, :], 1.0, jnp.where(srow == pos_t[3:4, :], 1.0, 0.0)).astype(BF16)
    srt = jnp.dot(sel, hb_ref[...], preferred_element_type=F32)
    bits = pltpu.bitcast(srt, jnp.uint32)
    sbuf[slot] = (bits[:, d // 2:] & jnp.uint32(0xFFFF0000)) | (bits[:, :d // 2] >> 16)

    def chunk_copy(run_row, sorted_row, rows, sl):
        return pltpu.make_async_copy(sbuf.at[sl, pl.ds(sorted_row, rows)], xs_ref.at[pl.ds(run_row, rows)],
                                     sem.at[sl])

    _for_each_chunk(cur_ref, lambda run_row, sorted_row, rows, priority:
                    chunk_copy(run_row, sorted_row, rows, slot).start(priority=priority))

    @pl.when(i > 0)
    def _():
        _wait_chunks(prv_ref, lambda rows: chunk_copy(0, 0, rows, 1 - slot))

    @pl.when(i == nt - 1)
    def _():
        _wait_chunks(cur_ref, lambda rows: chunk_copy(0, 0, rows, slot))


def _dispatch(zero_blocks, runs, hb, rt, n_rows):
    n, d = hb.shape
    return pl.pallas_call(
        _dispatch_kernel,
        out_shape=jax.ShapeDtypeStruct((n_rows, d // 2), jnp.uint32),
        grid_spec=pltpu.PrefetchScalarGridSpec(
            num_scalar_prefetch=1,
            grid=(n // TM,),
            in_specs=[
                pl.BlockSpec((1, 1, LANES), lambda i, zf: (i, 0, 0), memory_space=pltpu.SMEM),
                pl.BlockSpec((1, 1, LANES), lambda i, zf: (jnp.maximum(i - 1, 0), 0, 0), memory_space=pltpu.SMEM),
                pl.BlockSpec((TM, d), lambda i, zf: (i, 0)),
                pl.BlockSpec((TM, LANES), lambda i, zf: (i, 0)),
            ],
            out_specs=pl.BlockSpec(memory_space=pl.ANY),
            scratch_shapes=[pltpu.VMEM((2, SROWS, d // 2), jnp.uint32), pltpu.VMEM((EB, d // 2), jnp.uint32),
                            pltpu.SemaphoreType.DMA((2,)), pltpu.SemaphoreType.DMA(())],
        ),
        compiler_params=_cparams(("arbitrary",)),
        name="dispatch",
    )(zero_blocks, runs, runs, hb, rt)


def _experts_kernel(be_ref, na_ref, nxt_ref, xs_ref, wg_hbm, wu_hbm, wd_hbm, ys_ref,
                    wgf, wuf, wdf, wgb, wub, wdb, sem):
    b = pl.program_id(0)

    def weight_copies(e):
        return (pltpu.make_async_copy(wg_hbm.at[e], wgf, sem.at[0]),
                pltpu.make_async_copy(wu_hbm.at[e], wuf, sem.at[1]),
                pltpu.make_async_copy(wd_hbm.at[e], wdf, sem.at[2]))

    @pl.when(b == 0)
    def _():
        for c in weight_copies(be_ref[0]):
            c.start()

    @pl.when(b < na_ref[0])
    def _():
        e = be_ref[b]
        changed = jnp.logical_or(b == 0, be_ref[jnp.maximum(b - 1, 0)] != e)

        @pl.when(changed)
        def _load():
            for c in weight_copies(e):
                c.wait()
            wgb[...] = wgf[...].astype(BF16)
            wub[...] = wuf[...].astype(BF16)
            wdb[...] = wdf[...].astype(BF16)
            nxt = nxt_ref[e]

            @pl.when(nxt >= 0)
            def _():
                for c in weight_copies(nxt):
                    c.start()

        w = xs_ref[...]
        x_lo = pltpu.bitcast(w << 16, F32).astype(BF16)
        x_hi = pltpu.bitcast(w & jnp.uint32(0xFFFF0000), F32).astype(BF16)
        dh = w.shape[1]
        g = (jnp.dot(x_lo, wgb[:dh, :], preferred_element_type=F32)
             + jnp.dot(x_hi, wgb[dh:, :], preferred_element_type=F32))
        u = (jnp.dot(x_lo, wub[:dh, :], preferred_element_type=F32)
             + jnp.dot(x_hi, wub[dh:, :], preferred_element_type=F32))
        hdn = g * (1.0 / (1.0 + jnp.exp(-g))) * u
        y = jnp.dot(hdn.astype(BF16), wdb[...], preferred_element_type=F32)
        bits = pltpu.bitcast(y.astype(BF16).astype(F32), jnp.uint32)
        ys_ref[...] = (bits[:, dh:] & jnp.uint32(0xFFFF0000)) | (bits[:, :dh] >> 16)

    @pl.when(b >= na_ref[0])
    def _():
        ys_ref[...] = jnp.zeros(ys_ref.shape, ys_ref.dtype)


def _experts(blk_e, n_act, nxt_e, xs, w_gate, w_up, w_down):
    p, dh = xs.shape
    d = 2 * dh
    de = w_gate.shape[2]

    def row_map(b, be, na, nx):
        return (jnp.minimum(b, na[0] - 1), 0)

    return pl.pallas_call(
        _experts_kernel,
        out_shape=jax.ShapeDtypeStruct((p, dh), jnp.uint32),
        grid_spec=pltpu.PrefetchScalarGridSpec(
            num_scalar_prefetch=3,
            grid=(p // EB,),
            in_specs=[
                pl.BlockSpec((EB, dh), row_map),
                pl.BlockSpec(memory_space=pl.ANY),
                pl.BlockSpec(memory_space=pl.ANY),
                pl.BlockSpec(memory_space=pl.ANY),
            ],
            out_specs=pl.BlockSpec((EB, dh), lambda b, be, na, nx: (b, 0)),
            scratch_shapes=[pltpu.VMEM((d, de), F32), pltpu.VMEM((d, de), F32), pltpu.VMEM((de, d), F32),
                            pltpu.VMEM((d, de), BF16), pltpu.VMEM((d, de), BF16), pltpu.VMEM((de, d), BF16),
                            pltpu.SemaphoreType.DMA((3,))],
        ),
        compiler_params=_cparams(("arbitrary",)),
        name="experts",
    )(blk_e, n_act, nxt_e, xs, w_gate, w_up, w_down)


def _combine_kernel(cur_ref, nxt_ref, ys_ref, h_ref, rt_ref, o_ref, ybuf, sem):
    i = pl.program_id(0)
    nt = pl.num_programs(0)
    slot = i % 2
    tm = h_ref.shape[0]

    def chunk_copy(run_row, sorted_row, rows, sl):
        return pltpu.make_async_copy(ys_ref.at[pl.ds(run_row, rows)], ybuf.at[sl, pl.ds(sorted_row, rows)],
                                     sem.at[sl])

    @pl.when(i == 0)
    def _():
        ybuf[...] = jnp.zeros(ybuf.shape, ybuf.dtype)
        _for_each_chunk(cur_ref, lambda run_row, sorted_row, rows, priority:
                        chunk_copy(run_row, sorted_row, rows, 0).start(priority=priority))

    @pl.when(i + 1 < nt)
    def _():
        _for_each_chunk(nxt_ref, lambda run_row, sorted_row, rows, priority:
                        chunk_copy(run_row, sorted_row, rows, 1 - slot).start(priority=priority))

    _wait_chunks(cur_ref, lambda rows: chunk_copy(0, 0, rows, slot))

    rt = rt_ref[...]
    w = ybuf[slot]
    dh = w.shape[1]
    y_lo = pltpu.bitcast(w << 16, F32).astype(BF16)
    y_hi = pltpu.bitcast(w & jnp.uint32(0xFFFF0000), F32).astype(BF16)
    col = lax.broadcasted_iota(jnp.int32, (tm, SROWS), 1).astype(F32)
    w1 = jnp.where(col == rt[:, 2:3], 1.0, 0.0).astype(BF16)
    w2 = jnp.where(col == rt[:, 3:4], 1.0, 0.0).astype(BF16)
    for half, yb in ((slice(0, dh), y_lo), (slice(dh, 2 * dh), y_hi)):
        o_ref[:, half] = (h_ref[:, half]
                          + rt[:, 0:1] * jnp.dot(w1, yb, preferred_element_type=F32)
                          + rt[:, 1:2] * jnp.dot(w2, yb, preferred_element_type=F32))


def _combine(runs, ys, h1, rt):
    n, d = h1.shape
    nt = n // TM
    return pl.pallas_call(
        _combine_kernel,
        out_shape=jax.ShapeDtypeStruct((n, d), F32),
        grid=(nt,),
        in_specs=[
            pl.BlockSpec((1, 1, LANES), lambda i: (i, 0, 0), memory_space=pltpu.SMEM),
            pl.BlockSpec((1, 1, LANES), lambda i: (jnp.minimum(i + 1, nt - 1), 0, 0), memory_space=pltpu.SMEM),
            pl.BlockSpec(memory_space=pl.ANY),
            pl.BlockSpec((TM, d), lambda i: (i, 0)),
            pl.BlockSpec((TM, LANES), lambda i: (i, 0)),
        ],
        out_specs=pl.BlockSpec((TM, d), lambda i: (i, 0)),
        scratch_shapes=[pltpu.VMEM((2, SROWS, d // 2), jnp.uint32), pltpu.SemaphoreType.DMA((2,))],
        compiler_params=_cparams(("arbitrary",)),
        name="combine",
    )(runs, runs, ys, h1, rt)


def kernel(x, meta_tokens, rel_bias, norm1_gain, w_in, diff_q_gain, diff_k_gain, lam_q1, lam_k1, lam_q2, lam_k2, diff_subln_gain, swa_q_gain, swa_k_gain, swa_sinks, w_out, norm2_gain, w_group, b_group, w_router, b_router, w_gate, w_up, w_down):
    batch, seq, d = x.shape
    depth = w_in.shape[0]
    n = batch * seq
    assert seq % CQ == 0 and seq % TQ == 0 and n % TM == 0 and d == 1024
    assert meta_tokens.shape[0] == N_META
    assert depth == 1, "the meta-token rows of the residual stream are not carried across layers"

    h = x.reshape(n, d)
    dblk, bm0, bt = _bias_tables(rel_bias, TQ)
    scale = HEAD_DIM ** -0.5
    bd = jnp.asarray(np.kron(np.eye(MXU_DIM // HEAD_DIM), np.full((HEAD_DIM, HEAD_DIM), 1.0 / HEAD_DIM)), BF16)
    ones = jnp.ones((HEAD_DIM,), F32)
    lower_pad = N_EXPERTS + N_GROUPS

    for layer in range(depth):
        lambda_init = 0.8 - 0.6 * math.exp(-0.3 * layer)
        w_cat = w_in[layer].astype(BF16)
        gain = jnp.concatenate([
            jnp.tile(diff_q_gain[layer] * scale, 2 * N_DIFF_HEADS),
            jnp.tile(diff_k_gain[layer], 2 * N_DIFF_HEADS),
            jnp.tile(ones, 2 * N_DIFF_HEADS),
            jnp.tile(swa_q_gain[layer] * scale, N_SWA_HEADS),
            jnp.tile(swa_k_gain[layer], N_SWA_KV),
            jnp.tile(ones, N_SWA_KV)]).reshape(1, C_END).astype(F32)
        nmask = np.zeros((1, C_END), np.float32)
        nmask[:, C_DQ:C_DV] = 1.0
        nmask[:, C_SQ:C_SV] = 1.0
        nmask = jnp.asarray(nmask)
        g1 = norm1_gain[layer].reshape(1, d).astype(F32)

        qkv = _proj(h, g1, w_cat, bd, gain, nmask, TP)
        qkv_meta = _proj(meta_tokens.astype(F32), g1, w_cat, bd, gain, nmask, N_META)
        meta_pad = jnp.pad(qkv_meta, ((0, TQ - N_META), (0, 0)))

        lamv = jnp.pad(jnp.stack([lam_q1[layer], lam_k1[layer], lam_q2[layer], lam_k2[layer]]).astype(F32),
                       ((0, 4), (0, LANES - HEAD_DIM)))
        mixd = _diff_attention(qkv, meta_pad[:, C_DK:C_DV], meta_pad[:, C_DV:C_SQ], dblk, bm0, lamv,
                               diff_subln_gain[layer].reshape(1, LANES).astype(F32), batch, seq, lambda_init)
        mixs = _swa_attention(swa_sinks[layer].astype(F32), qkv, meta_pad[:BLOCK, C_SK:C_SV],
                              meta_pad[:BLOCK, C_SV:C_END], jnp.swapaxes(bt, -1, -2), batch, seq)

        wr = jnp.pad(jnp.concatenate([w_router[layer], w_group[layer]], axis=1),
                     ((0, 0), (0, LANES - lower_pad))).astype(BF16)
        br = jnp.pad(jnp.concatenate([b_router[layer], b_group[layer]]), (0, LANES - lower_pad)).reshape(1, LANES)
        h1, hb, rt, tinfo, cnt = _outproj(h, mixd, mixs, w_out[layer].astype(BF16),
                                          norm2_gain[layer].reshape(1, d).astype(F32), wr, br.astype(F32))

        nt = n // TM
        counts = cnt[0, :N_EXPERTS].astype(jnp.int32)
        nblk_e = (counts + EB - 1) // EB
        blk_end = jnp.cumsum(nblk_e)
        pstart = ((blk_end - nblk_e) * EB).astype(jnp.int32)
        n_blocks = -(-(2 * n + nt * N_EXPERTS * (ROW_ALIGN - 1) + N_EXPERTS * (EB - 1)) // EB)
        blk_ids = jnp.arange(n_blocks)
        blk_e = jnp.minimum(jnp.sum(blk_end[None, :] <= blk_ids[:, None], axis=1), N_EXPERTS - 1).astype(jnp.int32)
        n_act = blk_end[-1:].astype(jnp.int32)
        is_last = jnp.any((blk_end[None, :] == blk_ids[:, None] + 1) & (nblk_e[None, :] > 0), axis=1)
        zero_blocks = ((blk_ids >= n_act[0]) | is_last).astype(jnp.int32)
        ti = tinfo.reshape(nt, 8, LANES)
        run_len = ti[:, 0, :N_EXPERTS].astype(jnp.int32)
        run_start = pstart[None, :] + ti[:, 1, :N_EXPERTS].astype(jnp.int32)
        run_groups = (run_len + ROW_ALIGN - 1) // ROW_ALIGN
        per_chunk = CHUNK // ROW_ALIGN
        runs = jnp.concatenate([run_start, run_groups,
                                jnp.sum(run_groups // per_chunk, axis=1, keepdims=True),
                                jnp.sum(run_groups % per_chunk, axis=1, keepdims=True),
                                jnp.zeros((nt, LANES - 2 * N_EXPERTS - 2), jnp.int32)],
                               axis=1).reshape(nt, 1, LANES)

        xs = _dispatch(zero_blocks, runs, hb, rt, n_blocks * EB)
        own = jnp.where(nblk_e > 0, jnp.arange(N_EXPERTS), N_EXPERTS)
        later = jnp.concatenate([lax.cummin(own[::-1])[::-1][1:], jnp.full((1,), N_EXPERTS)])
        nxt_e = jnp.where(later < N_EXPERTS, later, -1).astype(jnp.int32)
        ys = _experts(blk_e, n_act, nxt_e, xs, w_gate[layer], w_up[layer], w_down[layer])
        h = _combine(runs, ys, h1, rt)
    return h.reshape(batch, seq, d)
```

```python
import functools
import math

import numpy as np
import jax
import jax.numpy as jnp
from jax import lax
from jax.experimental import pallas as pl
from jax.experimental.pallas import tpu as pltpu

F32 = jnp.float32
BF16 = jnp.bfloat16

HEAD_DIM = 64
N_DIFF_HEADS = 4
N_SWA_HEADS = 8
N_SWA_KV = 2
BLOCK = 128
N_META = 16
N_BUCKETS = 32
MAX_DISTANCE = 128
N_GROUPS = 4
EXPERTS_PER_GROUP = 8
N_EXPERTS = N_GROUPS * EXPERTS_PER_GROUP
D_EXPERT = 512
EPS = 1e-6
NEG = -1e30
LOG2E = math.log2(math.e)

LANES = 128
MXU_DIM = 256
VMEM_LIMIT = 48 * 1024 * 1024

TP = 512
TM = 256
TO = 512
TQ = 256
ONES_ROWS = 16
ACC_BUFS = 4
CQ = 512
EB = 512
ROW_ALIGN = 8
CHUNK = 16
SROWS = -(-(2 * TM + N_EXPERTS * (ROW_ALIGN - 1)) // MXU_DIM) * MXU_DIM

C_DQ, C_DK, C_DV, C_SQ, C_SK, C_SV, C_END = 0, 512, 1024, 1536, 2048, 2176, 2304
NORM_GROUPS = (0, 1, 2, 3, 6, 7, 8)


def _cparams(sem):
    return pltpu.CompilerParams(dimension_semantics=sem, vmem_limit_bytes=VMEM_LIMIT)


def _t5_bucket_np(dist):
    n = np.maximum(dist, 0)
    max_exact = N_BUCKETS // 2
    nf = np.maximum(n, 1).astype(np.float32)
    large = max_exact + (np.log(nf / np.float32(max_exact)) / np.float32(math.log(MAX_DISTANCE / max_exact))
                         * np.float32(N_BUCKETS - max_exact)).astype(np.int32)
    large = np.minimum(large, N_BUCKETS - 1)
    return np.where(n < max_exact, n, large)


def _bias_tables(rel_bias, tq):
    nd = 2 * BLOCK
    buckets = _t5_bucket_np(np.arange(nd))
    assert (buckets[MAX_DISTANCE:] == N_BUCKETS - 1).all()
    rb = rel_bias.astype(F32)
    r = np.arange(BLOCK)[:, None]
    c = np.arange(BLOCK)[None, :]
    d_own = r - c
    d_prev = BLOCK + r - c
    far = rb[N_BUCKETS - 1]

    def take(dist):
        idx = jnp.asarray(buckets[np.clip(dist, 0, nd - 1)], jnp.int32)[None]
        out = jnp.zeros((rb.shape[1],) + dist.shape, F32)
        for b in range(N_BUCKETS):
            out = jnp.where(idx == b, rb[b].reshape((-1,) + (1,) * dist.ndim), out)
        return out

    hd = slice(0, N_DIFF_HEADS)
    far_d = far[hd][:, None, None]
    d0 = jnp.where(d_own[None] >= 0, take(d_own)[hd] - far_d, NEG)
    d1 = take(d_prev)[hd] - far_d
    dblk = jnp.stack([d0, d1], axis=1)
    rq = np.arange(tq)[:, None]
    cm = np.arange(LANES)[None, :]
    d_meta = N_META + rq - cm
    bm0 = jnp.where((cm < N_META)[None], take(d_meta)[hd] - far_d, NEG)

    hs = slice(N_DIFF_HEADS, N_DIFF_HEADS + N_SWA_HEADS)
    far_s = far[hs][:, None, None]
    d_meta_s = N_META + r - cm
    meta_first = jnp.where((cm < N_META)[None], take(d_meta_s)[hs], NEG)
    meta_rest = jnp.where((cm < N_META)[None], jnp.broadcast_to(far_s, (N_SWA_HEADS, BLOCK, LANES)), NEG)
    prev_rest = jnp.where((c > r)[None], take(d_prev)[hs], NEG)
    prev_first = jnp.full((N_SWA_HEADS, BLOCK, BLOCK), NEG, F32)
    own = jnp.where((d_own >= 0)[None], take(d_own)[hs], NEG)
    bt = jnp.stack([jnp.concatenate([meta_first, prev_first, own], axis=-1),
                    jnp.concatenate([meta_rest, prev_rest, own], axis=-1)], axis=0)
    return dblk.astype(F32), bm0.astype(F32), bt.astype(F32)


def _proj_kernel(x_ref, g1_ref, w_ref, bd_ref, gain_ref, nmask_ref, o_ref):
    x = x_ref[...]
    a = x * lax.rsqrt(jnp.mean(x * x, axis=-1, keepdims=True) + EPS) * g1_ref[...]
    p = jnp.dot(a.astype(BF16), w_ref[...], preferred_element_type=F32)
    bd = bd_ref[...]
    for j in range(C_END // MXU_DIM):
        sl = slice(j * MXU_DIM, (j + 1) * MXU_DIM)
        pj = p[:, sl]
        if j in NORM_GROUPS:
            ms = jnp.dot((pj * pj).astype(BF16), bd, preferred_element_type=F32)
            pj = jnp.where(nmask_ref[:, sl] != 0.0, pj * lax.rsqrt(ms + EPS) * gain_ref[:, sl], pj)
        o_ref[:, sl] = pj.astype(BF16)


def _proj(x2, g1, w, bd, gain, nmask, tm):
    n = x2.shape[0]
    return pl.pallas_call(
        _proj_kernel,
        out_shape=jax.ShapeDtypeStruct((n, C_END), BF16),
        grid=(n // tm,),
        in_specs=[
            pl.BlockSpec((tm, x2.shape[1]), lambda i: (i, 0)),
            pl.BlockSpec(g1.shape, lambda i: (0, 0)),
            pl.BlockSpec(w.shape, lambda i: (0, 0)),
            pl.BlockSpec(bd.shape, lambda i: (0, 0)),
            pl.BlockSpec(gain.shape, lambda i: (0, 0)),
            pl.BlockSpec(nmask.shape, lambda i: (0, 0)),
        ],
        out_specs=pl.BlockSpec((tm, C_END), lambda i: (i, 0)),
        compiler_params=_cparams(("parallel",)),
        name="proj",
    )(x2, g1, w, bd, gain, nmask)


def _diff_kernel(qi_tab, t_tab, q_ref, k_ref, v_ref, km_ref, vm_ref, d_ref, bm0_ref, lamv_ref, gain_ref, o_ref,
                 bias_ref, qs_ref, vt_ref, s_buf, p_buf, a_buf, m_ref, acc_ref, *, lambda_init, n_steps):
    tq = TQ
    nq = q_ref.shape[0] // tq
    nb = tq // BLOCK
    BIAS_NONE, BIAS_LEFT, BIAS_DIAG, BIAS_META, BIAS_META0 = 0, 1, 2, 3, 4

    d0 = d_ref[0, 0] * LOG2E
    d1 = d_ref[0, 1] * LOG2E
    zeros = jnp.zeros((BLOCK, BLOCK), F32)
    bias_ref[BIAS_NONE] = jnp.zeros((tq, tq), F32)
    for a in range(nb):
        for b in range(nb):
            rs, cs = slice(a * BLOCK, (a + 1) * BLOCK), slice(b * BLOCK, (b + 1) * BLOCK)
            if a == b:
                blk = d0
            elif b == a + 1:
                blk = d1
            elif b > a:
                blk = zeros
            else:
                blk = jnp.full((BLOCK, BLOCK), NEG, F32)
            bias_ref[BIAS_DIAG, rs, cs] = blk
            bias_ref[BIAS_LEFT, rs, cs] = d1 if (b == 0 and a == nb - 1) else zeros
    row_m = lax.broadcasted_iota(jnp.int32, (tq, tq), 0)
    bias_ref[BIAS_META] = jnp.where(row_m < N_META, 0.0, NEG).astype(F32)
    bias_ref[BIAS_META0, :LANES, :] = bm0_ref[0] * LOG2E
    bias_ref[BIAS_META0, LANES:, :] = jnp.full((tq - LANES, tq), NEG, F32)

    lane = lax.broadcasted_iota(jnp.int32, (tq, LANES), 1)
    for i in range(nq):
        rows = slice(i * tq, (i + 1) * tq)
        q = q_ref[rows, :].astype(F32)
        qs_ref[i] = jnp.transpose(jnp.concatenate([jnp.where(lane < HEAD_DIM, q, 0.0),
                                                   jnp.where(lane >= HEAD_DIM, q, 0.0)], axis=0)).astype(BF16)
        vt_ref[i, :LANES, :] = jnp.transpose(v_ref[rows, :].astype(F32)).astype(BF16)
    vt_ref[nq, :LANES, :] = jnp.transpose(vm_ref[...].astype(F32)).astype(BF16)
    vt_ref[:, LANES:, :] = jnp.ones((nq + 1, ONES_ROWS, tq), BF16)
    acc_ref[...] = jnp.zeros(acc_ref.shape, F32)
    m_ref[...] = jnp.full(m_ref.shape, NEG, F32)
    lv = lamv_ref[...]
    lam = (jnp.exp(jnp.sum(lv[0:1] * lv[1:2], axis=-1, keepdims=True))
           - jnp.exp(jnp.sum(lv[2:3] * lv[3:4], axis=-1, keepdims=True)) + lambda_init)

    def seq_row(t):
        return pl.multiple_of(jnp.maximum(t - 1, 0) * tq, tq)

    def stage_a(n, slot):
        qi, t = qi_tab[n], t_tab[n]
        kt = jnp.where(t == 0, km_ref[...], k_ref[pl.ds(seq_row(t), tq), :])
        s = jnp.dot(kt, qs_ref[qi], preferred_element_type=F32)
        which = jnp.where(t == 0, jnp.where(qi == 0, BIAS_META0, BIAS_META),
                          jnp.where(t == qi + 1, BIAS_DIAG, jnp.where(t == qi, BIAS_LEFT, BIAS_NONE)))
        s_buf[slot] = s + jnp.tile(bias_ref[which], (1, 2))

    def stage_b(n, slot):
        s = s_buf[slot]
        m_prev = jnp.where(t_tab[n] == 0, NEG, m_ref[...])
        m_new = jnp.maximum(m_prev, jnp.max(s, axis=0, keepdims=True))
        a_buf[slot] = jnp.exp2(m_prev - m_new)
        p_buf[slot] = jnp.exp2(s - m_new[0:1]).astype(BF16)
        m_ref[...] = m_new

    def stage_c(n, slot):
        qi, t = qi_tab[n], t_tab[n]
        par = qi % ACC_BUFS
        vt = vt_ref[jnp.where(t == 0, nq, t - 1)]
        pv = jnp.dot(vt, p_buf[slot], preferred_element_type=F32)
        acc_ref[par] = a_buf[slot][0:1] * acc_ref[par] + pv

    def finish(n):
        qi, t = qi_tab[n], t_tab[n]

        @pl.when(t == qi + 1)
        def _():
            acc = acc_ref[qi % ACC_BUFS]
            o = acc[:LANES] / acc[LANES:LANES + 1]
            d = o[:, :tq] - lam * o[:, tq:]
            y = d * lax.rsqrt(jnp.mean(d * d, axis=0, keepdims=True) + EPS) * jnp.tile(gain_ref[...], (1, tq // LANES))
            y = jnp.transpose(y * (1.0 - lambda_init))
            o_ref[pl.ds(pl.multiple_of(qi * tq, tq), tq), :] = y.astype(BF16)

    stage_a(0, 0)
    stage_a(1, 1)
    stage_b(0, 0)

    def steps(n, count):
        for j in range(count):
            stage_a(n + j + 2, j % 2)
            stage_b(n + j + 1, (j + 1) % 2)
            stage_c(n + j, j % 2)
        for j in range(count):
            finish(n + j)

    unroll = 8
    assert unroll <= 3 * (ACC_BUFS - 1) + 1
    n_blocks = (n_steps - 2) // unroll

    def block(k, carry):
        steps(unroll * k, unroll)
        return carry
    lax.fori_loop(0, n_blocks, block, 0)
    steps(unroll * n_blocks, n_steps - 2 - unroll * n_blocks)

    stage_b(n_steps - 1, 1)
    stage_c(n_steps - 2, 0)
    stage_c(n_steps - 1, 1)
    finish(n_steps - 2)
    finish(n_steps - 1)


def _diff_attention(qkv, km, vm, dblk, bm0, lamv, gain, batch, seq, lambda_init):
    nq = seq // TQ
    steps = [(qi, t) for qi in range(nq) for t in range(qi + 2)]
    assert len(steps) % 2 == 0
    qi_tab = jnp.asarray([s[0] for s in steps], jnp.int32)
    t_tab = jnp.asarray([s[1] for s in steps], jnp.int32)
    kern = functools.partial(_diff_kernel, lambda_init=lambda_init, n_steps=len(steps))
    return pl.pallas_call(
        kern,
        out_shape=jax.ShapeDtypeStruct((batch * seq, N_DIFF_HEADS * LANES), BF16),
        grid_spec=pltpu.PrefetchScalarGridSpec(
            num_scalar_prefetch=2,
            grid=(batch, N_DIFF_HEADS),
            in_specs=[
                pl.BlockSpec((seq, LANES), lambda b, h, *_: (b, C_DQ // LANES + h)),
                pl.BlockSpec((seq, LANES), lambda b, h, *_: (b, C_DK // LANES + h)),
                pl.BlockSpec((seq, LANES), lambda b, h, *_: (b, C_DV // LANES + h)),
                pl.BlockSpec((TQ, LANES), lambda b, h, *_: (0, h)),
                pl.BlockSpec((TQ, LANES), lambda b, h, *_: (0, h)),
                pl.BlockSpec((1, 2, BLOCK, BLOCK), lambda b, h, *_: (h, 0, 0, 0)),
                pl.BlockSpec((1, LANES, TQ), lambda b, h, *_: (h, 0, 0)),
                pl.BlockSpec(lamv.shape, lambda b, h, *_: (0, 0)),
                pl.BlockSpec((LANES, LANES), lambda b, h, *_: (0, 0)),
            ],
            out_specs=pl.BlockSpec((seq, LANES), lambda b, h, *_: (b, h)),
            scratch_shapes=[
                pltpu.VMEM((5, TQ, TQ), F32),
                pltpu.VMEM((nq, LANES, 2 * TQ), BF16),
                pltpu.VMEM((nq + 1, LANES + ONES_ROWS, TQ), BF16),
                pltpu.VMEM((2, TQ, 2 * TQ), F32),
                pltpu.VMEM((2, TQ, 2 * TQ), BF16),
                pltpu.VMEM((2, 8, 2 * TQ), F32),
                pltpu.VMEM((8, 2 * TQ), F32),
                pltpu.VMEM((ACC_BUFS, LANES + ONES_ROWS, 2 * TQ), F32),
            ],
        ),
        compiler_params=_cparams(("parallel", "parallel")),
        name="diff_attention",
    )(qi_tab, t_tab, qkv, qkv, qkv, km, vm, jnp.swapaxes(dblk, -1, -2), jnp.swapaxes(bm0, -1, -2), lamv,
      jnp.broadcast_to(gain.reshape(LANES, 1), (LANES, LANES)))


def _swa_kernel(sink_ref, q_ref, k_ref, v_ref, km_ref, vm_ref, bt_ref, o_ref, kd_ref, vt_ref,
                s_scr, p_scr, inv_scr):
    ci = pl.program_id(1)
    nblk = q_ref.shape[0] // BLOCK
    nkb = k_ref.shape[0] // BLOCK
    lane = lax.broadcasted_iota(jnp.int32, (BLOCK, LANES), 1)

    def both_halves(k):
        k0, k1 = k[:, :HEAD_DIM], k[:, HEAD_DIM:]
        return jnp.concatenate([k0, k0, k1, k1], axis=1)

    @pl.when(ci == 0)
    def _per_batch():
        def body(j, carry):
            rows = pl.ds(pl.multiple_of(j * BLOCK, BLOCK), BLOCK)
            kd_ref[j] = both_halves(k_ref[rows, :])
            vt_ref[j] = jnp.transpose(v_ref[rows, :].astype(F32)).astype(BF16)
            return carry
        lax.fori_loop(0, nkb, body, 0)
        kd_ref[nkb] = both_halves(km_ref[...])
        vt_ref[nkb] = jnp.transpose(vm_ref[...].astype(F32)).astype(BF16)

    def block_body(n, carry):
        gblk = ci * nblk + n
        first = jnp.where(gblk == 0, 0, 1)
        prev = jnp.maximum(gblk - 1, 0)
        r_q = pl.multiple_of(n * BLOCK, BLOCK)
        pairs = [(g, u) for g in range(N_SWA_KV) for u in range(2)]
        for c, (g, u) in enumerate(pairs):
            ks = slice(g * LANES, (g + 1) * LANES)
            kcat = jnp.concatenate([kd_ref[nkb, :, ks], kd_ref[prev, :, ks], kd_ref[gblk, :, ks]], axis=0)
            h0 = 4 * g + 2 * u
            qp = q_ref[pl.ds(r_q, BLOCK), (2 * g + u) * LANES:(2 * g + u + 1) * LANES].astype(F32)
            qs = jnp.transpose(jnp.concatenate([jnp.where(lane < HEAD_DIM, qp, 0.0),
                                                jnp.where(lane >= HEAD_DIM, qp, 0.0)], axis=0)).astype(BF16)
            s = jnp.dot(kcat, qs, preferred_element_type=F32)
            s_scr[c] = s + jnp.concatenate([bt_ref[first, h0], bt_ref[first, h0 + 1]], axis=1)
        for c, (g, u) in enumerate(pairs):
            h0 = 4 * g + 2 * u
            s = s_scr[c]
            sink = jnp.concatenate([sink_ref[h0:h0 + 1, :], sink_ref[h0 + 1:h0 + 2, :]], axis=1)
            m = jnp.maximum(jnp.max(s, axis=0, keepdims=True), sink)
            p = jnp.exp(s - m)
            p_scr[c] = p.astype(BF16)
            inv_scr[c] = jnp.broadcast_to(1.0 / (jnp.sum(p, axis=0, keepdims=True) + jnp.exp(sink - m)),
                                          inv_scr.shape[1:])
        for c, (g, u) in enumerate(pairs):
            vs = slice(g * HEAD_DIM, (g + 1) * HEAD_DIM)
            vcat = jnp.concatenate([vt_ref[nkb, vs, :], vt_ref[prev, vs, :], vt_ref[gblk, vs, :]], axis=1)
            o = jnp.dot(vcat, p_scr[c], preferred_element_type=F32) * inv_scr[c][0:1]
            ot = jnp.transpose(o)
            o_ref[pl.ds(r_q, BLOCK), (2 * g + u) * LANES:(2 * g + u + 1) * LANES] = (
                jnp.concatenate([ot[:BLOCK], ot[BLOCK:]], axis=1).astype(BF16))
        return carry

    lax.fori_loop(0, nblk, block_body, 0)


def _swa_attention(sinks, qkv, km, vm, bt, batch, seq):
    nc = seq // CQ
    nkb = seq // BLOCK
    sinkv = jnp.broadcast_to(sinks.reshape(N_SWA_HEADS, 1), (N_SWA_HEADS, LANES))
    return pl.pallas_call(
        _swa_kernel,
        out_shape=jax.ShapeDtypeStruct((batch * seq, N_SWA_HEADS * HEAD_DIM), BF16),
        grid=(batch, nc),
        in_specs=[
            pl.BlockSpec(sinkv.shape, lambda b, c: (0, 0)),
            pl.BlockSpec((CQ, 512), lambda b, c: (b * nc + c, C_SQ // 512)),
            pl.BlockSpec((seq, LANES), lambda b, c: (b, C_SK // LANES)),
            pl.BlockSpec((seq, LANES), lambda b, c: (b, C_SV // LANES)),
            pl.BlockSpec(km.shape, lambda b, c: (0, 0)),
            pl.BlockSpec(vm.shape, lambda b, c: (0, 0)),
            pl.BlockSpec(bt.shape, lambda b, c: (0, 0, 0, 0)),
        ],
        out_specs=pl.BlockSpec((CQ, 512), lambda b, c: (b * nc + c, 0)),
        scratch_shapes=[pltpu.VMEM((nkb + 1, BLOCK, 2 * LANES), BF16),
                        pltpu.VMEM((nkb + 1, LANES, BLOCK), BF16),
                        pltpu.VMEM((4, 3 * BLOCK, 2 * BLOCK), F32),
                        pltpu.VMEM((4, 3 * BLOCK, 2 * BLOCK), BF16),
                        pltpu.VMEM((4, 8, 2 * BLOCK), F32)],
        compiler_params=_cparams(("parallel", "arbitrary")),
        name="swa_attention",
    )(sinkv, qkv, qkv, qkv, km, vm, bt)


def _outproj_kernel(x_ref, md_ref, ms_ref, wo_ref, g2_ref, wr_ref, br_ref,
                    h_ref, hb_ref, rt_ref, ti_ref, cnt_ref, c_ref):
    i = pl.program_id(0)

    @pl.when(i == 0)
    def _init():
        c_ref[...] = jnp.zeros(c_ref.shape, F32)

    for sub in range(x_ref.shape[0] // TM):
        _route_tile(slice(sub * TM, (sub + 1) * TM), slice(sub * 8, (sub + 1) * 8),
                    x_ref, md_ref, ms_ref, wo_ref, g2_ref, wr_ref, br_ref, h_ref, hb_ref, rt_ref, ti_ref, c_ref)

    @pl.when(i == pl.num_programs(0) - 1)
    def _fin():
        cnt_ref[...] = c_ref[...]


def _route_tile(rows, ti_rows, x_ref, md_ref, ms_ref, wo_ref, g2_ref, wr_ref, br_ref,
                h_ref, hb_ref, rt_ref, ti_ref, c_ref):
    tm = TM
    half = md_ref.shape[1]
    h = (x_ref[rows, :]
         + jnp.dot(md_ref[rows, :], wo_ref[:half, :], preferred_element_type=F32)
         + jnp.dot(ms_ref[rows, :], wo_ref[half:, :], preferred_element_type=F32))
    h_ref[rows, :] = h
    hn = h * lax.rsqrt(jnp.mean(h * h, axis=-1, keepdims=True) + EPS) * g2_ref[...]
    hb = hn.astype(BF16)
    hb_ref[rows, :] = hb

    lg = jnp.dot(hb, wr_ref[...], preferred_element_type=F32) + br_ref[...]
    lane_i = lax.broadcasted_iota(jnp.int32, lg.shape, 1)
    lane = lane_i.astype(F32)
    big = float(4 * LANES)
    is_g = (lane_i >= N_EXPERTS) & (lane_i < N_EXPERTS + N_GROUPS)
    glm = jnp.where(is_g, lg, -jnp.inf)
    gmax = jnp.max(glm, axis=1, keepdims=True)
    gidx = jnp.min(jnp.where(glm == gmax, lane, big), axis=1, keepdims=True) - N_EXPERTS
    gsum = jnp.sum(jnp.where(is_g, jnp.exp(lg - gmax), 0.0), axis=1, keepdims=True)
    g_w = 1.0 / gsum
    lane_grp = (lane_i >> 3).astype(F32)
    in_grp = (lane_i < N_EXPERTS) & (lane_grp == gidx)
    el = jnp.where(in_grp, lg, -jnp.inf)
    t1 = jnp.max(el, axis=1, keepdims=True)
    j1 = jnp.min(jnp.where(el == t1, lane, big), axis=1, keepdims=True)
    el2 = jnp.where(lane == j1, -jnp.inf, el)
    t2 = jnp.max(el2, axis=1, keepdims=True)
    j2 = jnp.min(jnp.where(el2 == t2, lane, big), axis=1, keepdims=True)
    e2 = jnp.exp(t2 - t1)
    den = 1.0 + e2
    gate1 = g_w / den
    gate2 = g_w * e2 / den

    o1 = lane == j1
    o2 = lane == j2
    onehot = jnp.where(o1 | o2, 1.0, 0.0).astype(BF16)
    rr = lax.broadcasted_iota(jnp.int32, (tm, tm), 0)
    cc = lax.broadcasted_iota(jnp.int32, (tm, tm), 1)
    lower = jnp.where(rr > cc, 1.0, 0.0).astype(BF16)
    pfx = jnp.dot(lower, onehot, preferred_element_type=F32)
    cnt_tile = jnp.sum(onehot.astype(F32), axis=0, keepdims=True)
    groups = jnp.floor((cnt_tile + (ROW_ALIGN - 1)) * (1.0 / ROW_ALIGN))
    er = lax.broadcasted_iota(jnp.int32, (LANES, LANES), 0)
    ec = lax.broadcasted_iota(jnp.int32, (LANES, LANES), 1)
    before = jnp.where(er < ec, 1.0, 0.0).astype(BF16)
    cbase = ROW_ALIGN * jnp.dot(jnp.broadcast_to(groups, (8, LANES)).astype(BF16), before,
                                preferred_element_type=F32)[0:1]
    at = pfx + cbase
    pos1 = jnp.sum(jnp.where(o1, at, 0.0), axis=1, keepdims=True)
    pos2 = jnp.sum(jnp.where(o2, at, 0.0), axis=1, keepdims=True)
    rt_ref[rows, :] = jnp.where(lane_i == 0, gate1,
                                jnp.where(lane_i == 1, gate2,
                                          jnp.where(lane_i == 2, pos1,
                                                    jnp.where(lane_i == 3, pos2, 0.0))))
    c_old = c_ref[...]
    c_ref[...] = c_old + groups * ROW_ALIGN
    row8 = lax.broadcasted_iota(jnp.int32, (8, LANES), 0)
    ti_ref[ti_rows, :] = jnp.where(row8 == 0, cnt_tile, jnp.where(row8 == 1, c_old, 0.0))


def _outproj(x2, mixd, mixs, wo, g2, wr, br):
    n, d = x2.shape
    sub = TO // TM
    return pl.pallas_call(
        _outproj_kernel,
        out_shape=(jax.ShapeDtypeStruct((n, d), F32),
                   jax.ShapeDtypeStruct((n, d), BF16),
                   jax.ShapeDtypeStruct((n, LANES), F32),
                   jax.ShapeDtypeStruct((n // TM * 8, LANES), F32),
                   jax.ShapeDtypeStruct((8, LANES), F32)),
        grid=(n // TO,),
        in_specs=[
            pl.BlockSpec((TO, d), lambda i: (i, 0)),
            pl.BlockSpec((TO, mixd.shape[1]), lambda i: (i, 0)),
            pl.BlockSpec((TO, mixs.shape[1]), lambda i: (i, 0)),
            pl.BlockSpec(wo.shape, lambda i: (0, 0)),
            pl.BlockSpec(g2.shape, lambda i: (0, 0)),
            pl.BlockSpec(wr.shape, lambda i: (0, 0)),
            pl.BlockSpec(br.shape, lambda i: (0, 0)),
        ],
        out_specs=(pl.BlockSpec((TO, d), lambda i: (i, 0)),
                   pl.BlockSpec((TO, d), lambda i: (i, 0)),
                   pl.BlockSpec((TO, LANES), lambda i: (i, 0)),
                   pl.BlockSpec((8 * sub, LANES), lambda i: (i, 0)),
                   pl.BlockSpec((8, LANES), lambda i: (0, 0))),
        scratch_shapes=[pltpu.VMEM((8, LANES), F32)],
        compiler_params=_cparams(("arbitrary",)),
        name="outproj_router",
    )(x2, mixd, mixs, wo, g2, wr, br)


def _for_each_chunk(runs_ref, fn):
    def per_expert(e, sorted_row, priority):
        start = runs_ref[0, 0, e]
        groups = runs_ref[0, 0, N_EXPERTS + e]
        whole = groups // (CHUNK // ROW_ALIGN)

        def per_chunk(c, carry):
            fn(pl.multiple_of(start + c * CHUNK, ROW_ALIGN), pl.multiple_of(sorted_row + c * CHUNK, ROW_ALIGN),
               CHUNK, priority)
            return carry
        lax.fori_loop(0, whole, per_chunk, 0)

        @pl.when(groups % (CHUNK // ROW_ALIGN) == 1)
        def _():
            fn(pl.multiple_of(start + whole * CHUNK, ROW_ALIGN),
               pl.multiple_of(sorted_row + whole * CHUNK, ROW_ALIGN), ROW_ALIGN, priority)
        return sorted_row + groups * ROW_ALIGN

    def expert_pair(e2, sorted_row):
        return per_expert(2 * e2 + 1, per_expert(2 * e2, sorted_row, 0), 1)
    lax.fori_loop(0, N_EXPERTS // 2, expert_pair, 0)


def _wait_chunks(runs_ref, make_copy):
    for k, rows in enumerate((CHUNK, ROW_ALIGN)):
        def body(c, carry, rows=rows):
            make_copy(rows).wait()
            return carry
        lax.fori_loop(0, runs_ref[0, 0, 2 * N_EXPERTS + k], body, 0)


def _dispatch_kernel(zf_ref, cur_ref, prv_ref, hb_ref, rt_ref, xs_ref, sbuf, zbuf, sem, zsem):
    i = pl.program_id(0)
    nt = pl.num_programs(0)
    slot = i % 2
    tm, d = hb_ref.shape

    @pl.when(i == 0)
    def _():
        zbuf[...] = jnp.zeros(zbuf.shape, zbuf.dtype)

        def zero_copy(b):
            return pltpu.make_async_copy(zbuf, xs_ref.at[pl.ds(pl.multiple_of(b * EB, EB), EB)], zsem)

        def start(b, carry):
            @pl.when(zf_ref[b] == 1)
            def _():
                zero_copy(b).start()
            return carry
        lax.fori_loop(0, zf_ref.shape[0], start, 0)

        def wait(b, carry):
            @pl.when(zf_ref[b] == 1)
            def _():
                zero_copy(b).wait()
            return carry
        lax.fori_loop(0, zf_ref.shape[0], wait, 0)

    pos_t = jnp.transpose(rt_ref[...])
    srow = lax.broadcasted_iota(jnp.int32, (SROWS, tm), 0).astype(F32)
    sel = jnp.where(srow == pos_t[2:3, :], 1.0, jnp.where(srow == pos_t[3:4, :], 1.0, 0.0)).astype(BF16)
    srt = jnp.dot(sel, hb_ref[...], preferred_element_type=F32)
    bits = pltpu.bitcast(srt, jnp.uint32)
    sbuf[slot] = (bits[:, d // 2:] & jnp.uint32(0xFFFF0000)) | (bits[:, :d // 2] >> 16)

    def chunk_copy(run_row, sorted_row, rows, sl):
        return pltpu.make_async_copy(sbuf.at[sl, pl.ds(sorted_row, rows)], xs_ref.at[pl.ds(run_row, rows)],
                                     sem.at[sl])

    _for_each_chunk(cur_ref, lambda run_row, sorted_row, rows, priority:
                    chunk_copy(run_row, sorted_row, rows, slot).start(priority=priority))

    @pl.when(i > 0)
    def _():
        _wait_chunks(prv_ref, lambda rows: chunk_copy(0, 0, rows, 1 - slot))

    @pl.when(i == nt - 1)
    def _():
        _wait_chunks(cur_ref, lambda rows: chunk_copy(0, 0, rows, slot))


def _dispatch(zero_blocks, runs, hb, rt, n_rows):
    n, d = hb.shape
    return pl.pallas_call(
        _dispatch_kernel,
        out_shape=jax.ShapeDtypeStruct((n_rows, d // 2), jnp.uint32),
        grid_spec=pltpu.PrefetchScalarGridSpec(
            num_scalar_prefetch=1,
            grid=(n // TM,),
            in_specs=[
                pl.BlockSpec((1, 1, LANES), lambda i, zf: (i, 0, 0), memory_space=pltpu.SMEM),
                pl.BlockSpec((1, 1, LANES), lambda i, zf: (jnp.maximum(i - 1, 0), 0, 0), memory_space=pltpu.SMEM),
                pl.BlockSpec((TM, d), lambda i, zf: (i, 0)),
                pl.BlockSpec((TM, LANES), lambda i, zf: (i, 0)),
            ],
            out_specs=pl.BlockSpec(memory_space=pl.ANY),
            scratch_shapes=[pltpu.VMEM((2, SROWS, d // 2), jnp.uint32), pltpu.VMEM((EB, d // 2), jnp.uint32),
                            pltpu.SemaphoreType.DMA((2,)), pltpu.SemaphoreType.DMA(())],
        ),
        compiler_params=_cparams(("arbitrary",)),
        name="dispatch",
    )(zero_blocks, runs, runs, hb, rt)


def _experts_kernel(be_ref, na_ref, nxt_ref, xs_ref, wg_hbm, wu_hbm, wd_hbm, ys_ref,
                    wgf, wuf, wdf, wgb, wub, wdb, sem):
    b = pl.program_id(0)

    def weight_copies(e):
        return (pltpu.make_async_copy(wg_hbm.at[e], wgf, sem.at[0]),
                pltpu.make_async_copy(wu_hbm.at[e], wuf, sem.at[1]),
                pltpu.make_async_copy(wd_hbm.at[e], wdf, sem.at[2]))

    @pl.when(b == 0)
    def _():
        for c in weight_copies(be_ref[0]):
            c.start()

    @pl.when(b < na_ref[0])
    def _():
        e = be_ref[b]
        changed = jnp.logical_or(b == 0, be_ref[jnp.maximum(b - 1, 0)] != e)

        @pl.when(changed)
        def _load():
            for c in weight_copies(e):
                c.wait()
            wgb[...] = wgf[...].astype(BF16)
            wub[...] = wuf[...].astype(BF16)
            wdb[...] = wdf[...].astype(BF16)
            nxt = nxt_ref[e]

            @pl.when(nxt >= 0)
            def _():
                for c in weight_copies(nxt):
                    c.start()

        w = xs_ref[...]
        x_lo = pltpu.bitcast(w << 16, F32).astype(BF16)
        x_hi = pltpu.bitcast(w & jnp.uint32(0xFFFF0000), F32).astype(BF16)
        dh = w.shape[1]
        g = (jnp.dot(x_lo, wgb[:dh, :], preferred_element_type=F32)
             + jnp.dot(x_hi, wgb[dh:, :], preferred_element_type=F32))
        u = (jnp.dot(x_lo, wub[:dh, :], preferred_element_type=F32)
             + jnp.dot(x_hi, wub[dh:, :], preferred_element_type=F32))
        hdn = g * (1.0 / (1.0 + jnp.exp(-g))) * u
        y = jnp.dot(hdn.astype(BF16), wdb[...], preferred_element_type=F32)
        bits = pltpu.bitcast(y.astype(BF16).astype(F32), jnp.uint32)
        ys_ref[...] = (bits[:, dh:] & jnp.uint32(0xFFFF0000)) | (bits[:, :dh] >> 16)

    @pl.when(b >= na_ref[0])
    def _():
        ys_ref[...] = jnp.zeros(ys_ref.shape, ys_ref.dtype)


def _experts(blk_e, n_act, nxt_e, xs, w_gate, w_up, w_down):
    p, dh = xs.shape
    d = 2 * dh
    de = w_gate.shape[2]

    def row_map(b, be, na, nx):
        return (jnp.minimum(b, na[0] - 1), 0)

    return pl.pallas_call(
        _experts_kernel,
        out_shape=jax.ShapeDtypeStruct((p, dh), jnp.uint32),
        grid_spec=pltpu.PrefetchScalarGridSpec(
            num_scalar_prefetch=3,
            grid=(p // EB,),
            in_specs=[
                pl.BlockSpec((EB, dh), row_map),
                pl.BlockSpec(memory_space=pl.ANY),
                pl.BlockSpec(memory_space=pl.ANY),
                pl.BlockSpec(memory_space=pl.ANY),
            ],
            out_specs=pl.BlockSpec((EB, dh), lambda b, be, na, nx: (b, 0)),
            scratch_shapes=[pltpu.VMEM((d, de), F32), pltpu.VMEM((d, de), F32), pltpu.VMEM((de, d), F32),
                            pltpu.VMEM((d, de), BF16), pltpu.VMEM((d, de), BF16), pltpu.VMEM((de, d), BF16),
                            pltpu.SemaphoreType.DMA((3,))],
        ),
        compiler_params=_cparams(("arbitrary",)),
        name="experts",
    )(blk_e, n_act, nxt_e, xs, w_gate, w_up, w_down)


def _combine_kernel(cur_ref, nxt_ref, ys_ref, h_ref, rt_ref, o_ref, ybuf, sem):
    i = pl.program_id(0)
    nt = pl.num_programs(0)
    slot = i % 2
    tm = h_ref.shape[0]

    def chunk_copy(run_row, sorted_row, rows, sl):
        return pltpu.make_async_copy(ys_ref.at[pl.ds(run_row, rows)], ybuf.at[sl, pl.ds(sorted_row, rows)],
                                     sem.at[sl])

    @pl.when(i == 0)
    def _():
        ybuf[...] = jnp.zeros(ybuf.shape, ybuf.dtype)
        _for_each_chunk(cur_ref, lambda run_row, sorted_row, rows, priority:
                        chunk_copy(run_row, sorted_row, rows, 0).start(priority=priority))

    @pl.when(i + 1 < nt)
    def _():
        _for_each_chunk(nxt_ref, lambda run_row, sorted_row, rows, priority:
                        chunk_copy(run_row, sorted_row, rows, 1 - slot).start(priority=priority))

    _wait_chunks(cur_ref, lambda rows: chunk_copy(0, 0, rows, slot))

    rt = rt_ref[...]
    w = ybuf[slot]
    dh = w.shape[1]
    y_lo = pltpu.bitcast(w << 16, F32).astype(BF16)
    y_hi = pltpu.bitcast(w & jnp.uint32(0xFFFF0000), F32).astype(BF16)
    col = lax.broadcasted_iota(jnp.int32, (tm, SROWS), 1).astype(F32)
    w1 = jnp.where(col == rt[:, 2:3], 1.0, 0.0).astype(BF16)
    w2 = jnp.where(col == rt[:, 3:4], 1.0, 0.0).astype(BF16)
    for half, yb in ((slice(0, dh), y_lo), (slice(dh, 2 * dh), y_hi)):
        o_ref[:, half] = (h_ref[:, half]
                          + rt[:, 0:1] * jnp.dot(w1, yb, preferred_element_type=F32)
                          + rt[:, 1:2] * jnp.dot(w2, yb, preferred_element_type=F32))


def _combine(runs, ys, h1, rt):
    n, d = h1.shape
    nt = n // TM
    return pl.pallas_call(
        _combine_kernel,
        out_shape=jax.ShapeDtypeStruct((n, d), F32),
        grid=(nt,),
        in_specs=[
            pl.BlockSpec((1, 1, LANES), lambda i: (i, 0, 0), memory_space=pltpu.SMEM),
            pl.BlockSpec((1, 1, LANES), lambda i: (jnp.minimum(i + 1, nt - 1), 0, 0), memory_space=pltpu.SMEM),
            pl.BlockSpec(memory_space=pl.ANY),
            pl.BlockSpec((TM, d), lambda i: (i, 0)),
            pl.BlockSpec((TM, LANES), lambda i: (i, 0)),
        ],
        out_specs=pl.BlockSpec((TM, d), lambda i: (i, 0)),
        scratch_shapes=[pltpu.VMEM((2, SROWS, d // 2), jnp.uint32), pltpu.SemaphoreType.DMA((2,))],
        compiler_params=_cparams(("arbitrary",)),
        name="combine",
    )(runs, runs, ys, h1, rt)


def kernel(x, meta_tokens, rel_bias, norm1_gain, w_in, diff_q_gain, diff_k_gain, lam_q1, lam_k1, lam_q2, lam_k2, diff_subln_gain, swa_q_gain, swa_k_gain, swa_sinks, w_out, norm2_gain, w_group, b_group, w_router, b_router, w_gate, w_up, w_down):
    batch, seq, d = x.shape
    depth = w_in.shape[0]
    n = batch * seq
    assert seq % CQ == 0 and seq % TQ == 0 and n % TM == 0 and d == 1024
    assert meta_tokens.shape[0] == N_META
    assert depth == 1, "the meta-token rows of the residual stream are not carried across layers"

    h = x.reshape(n, d)
    dblk, bm0, bt = _bias_tables(rel_bias, TQ)
    scale = HEAD_DIM ** -0.5
    bd = jnp.asarray(np.kron(np.eye(MXU_DIM // HEAD_DIM), np.full((HEAD_DIM, HEAD_DIM), 1.0 / HEAD_DIM)), BF16)
    ones = jnp.ones((HEAD_DIM,), F32)
    lower_pad = N_EXPERTS + N_GROUPS

    for layer in range(depth):
        lambda_init = 0.8 - 0.6 * math.exp(-0.3 * layer)
        w_cat = w_in[layer].astype(BF16)
        gain = jnp.concatenate([
            jnp.tile(diff_q_gain[layer] * (scale * LOG2E), 2 * N_DIFF_HEADS),
            jnp.tile(diff_k_gain[layer], 2 * N_DIFF_HEADS),
            jnp.tile(ones, 2 * N_DIFF_HEADS),
            jnp.tile(swa_q_gain[layer] * scale, N_SWA_HEADS),
            jnp.tile(swa_k_gain[layer], N_SWA_KV),
            jnp.tile(ones, N_SWA_KV)]).reshape(1, C_END).astype(F32)
        nmask = np.zeros((1, C_END), np.float32)
        nmask[:, C_DQ:C_DV] = 1.0
        nmask[:, C_SQ:C_SV] = 1.0
        nmask = jnp.asarray(nmask)
        g1 = norm1_gain[layer].reshape(1, d).astype(F32)

        qkv = _proj(h, g1, w_cat, bd, gain, nmask, TP)
        qkv_meta = _proj(meta_tokens.astype(F32), g1, w_cat, bd, gain, nmask, N_META)
        meta_pad = jnp.pad(qkv_meta, ((0, TQ - N_META), (0, 0)))

        lamv = jnp.pad(jnp.stack([lam_q1[layer], lam_k1[layer], lam_q2[layer], lam_k2[layer]]).astype(F32),
                       ((0, 4), (0, LANES - HEAD_DIM)))
        mixd = _diff_attention(qkv, meta_pad[:, C_DK:C_DV], meta_pad[:, C_DV:C_SQ], dblk, bm0, lamv,
                               diff_subln_gain[layer].reshape(1, LANES).astype(F32), batch, seq, lambda_init)
        mixs = _swa_attention(swa_sinks[layer].astype(F32), qkv, meta_pad[:BLOCK, C_SK:C_SV],
                              meta_pad[:BLOCK, C_SV:C_END], jnp.swapaxes(bt, -1, -2), batch, seq)

        wr = jnp.pad(jnp.concatenate([w_router[layer], w_group[layer]], axis=1),
                     ((0, 0), (0, LANES - lower_pad))).astype(BF16)
        br = jnp.pad(jnp.concatenate([b_router[layer], b_group[layer]]), (0, LANES - lower_pad)).reshape(1, LANES)
        h1, hb, rt, tinfo, cnt = _outproj(h, mixd, mixs, w_out[layer].astype(BF16),
                                          norm2_gain[layer].reshape(1, d).astype(F32), wr, br.astype(F32))

        nt = n // TM
        counts = cnt[0, :N_EXPERTS].astype(jnp.int32)
        nblk_e = (counts + EB - 1) // EB
        blk_end = jnp.cumsum(nblk_e)
        pstart = ((blk_end - nblk_e) * EB).astype(jnp.int32)
        n_blocks = -(-(2 * n + nt * N_EXPERTS * (ROW_ALIGN - 1) + N_EXPERTS * (EB - 1)) // EB)
        blk_ids = jnp.arange(n_blocks)
        blk_e = jnp.minimum(jnp.sum(blk_end[None, :] <= blk_ids[:, None], axis=1), N_EXPERTS - 1).astype(jnp.int32)
        n_act = blk_end[-1:].astype(jnp.int32)
        is_last = jnp.any((blk_end[None, :] == blk_ids[:, None] + 1) & (nblk_e[None, :] > 0), axis=1)
        zero_blocks = ((blk_ids >= n_act[0]) | is_last).astype(jnp.int32)
        ti = tinfo.reshape(nt, 8, LANES)
        run_len = ti[:, 0, :N_EXPERTS].astype(jnp.int32)
        run_start = pstart[None, :] + ti[:, 1, :N_EXPERTS].astype(jnp.int32)
        run_groups = (run_len + ROW_ALIGN - 1) // ROW_ALIGN
        per_chunk = CHUNK // ROW_ALIGN
        runs = jnp.concatenate([run_start, run_groups,
                                jnp.sum(run_groups // per_chunk, axis=1, keepdims=True),
                                jnp.sum(run_groups % per_chunk, axis=1, keepdims=True),
                                jnp.zeros((nt, LANES - 2 * N_EXPERTS - 2), jnp.int32)],
                               axis=1).reshape(nt, 1, LANES)

        xs = _dispatch(zero_blocks, runs, hb, rt, n_blocks * EB)
        own = jnp.where(nblk_e > 0, jnp.arange(N_EXPERTS), N_EXPERTS)
        later = jnp.concatenate([lax.cummin(own[::-1])[::-1][1:], jnp.full((1,), N_EXPERTS)])
        nxt_e = jnp.where(later < N_EXPERTS, later, -1).astype(jnp.int32)
        ys = _experts(blk_e, n_act, nxt_e, xs, w_gate[layer], w_up[layer], w_down[layer])
        h = _combine(runs, ys, h1, rt)
    return h.reshape(batch, seq, d)
```

```python
import functools
import math

import numpy as np
import jax
import jax.numpy as jnp
from jax import lax
from jax.experimental import pallas as pl
from jax.experimental.pallas import tpu as pltpu

F32 = jnp.float32
BF16 = jnp.bfloat16

HEAD_DIM = 64
N_DIFF_HEADS = 4
N_SWA_HEADS = 8
N_SWA_KV = 2
BLOCK = 128
N_META = 16
N_BUCKETS = 32
MAX_DISTANCE = 128
N_GROUPS = 4
EXPERTS_PER_GROUP = 8
N_EXPERTS = N_GROUPS * EXPERTS_PER_GROUP
D_EXPERT = 512
EPS = 1e-6
NEG = -1e30
LOG2E = math.log2(math.e)

LANES = 128
MXU_DIM = 256
VMEM_LIMIT = 48 * 1024 * 1024

TP = 512
TM = 512
TO = 512
TQ = 256
ONES_ROWS = 16
ACC_BUFS = 4
CQ = 512
EB = 512
ROW_ALIGN = 8
CHUNK = 16
SROWS = -(-(2 * TM + N_EXPERTS * (ROW_ALIGN - 1)) // MXU_DIM) * MXU_DIM

C_DQ, C_DK, C_DV, C_SQ, C_SK, C_SV, C_END = 0, 512, 1024, 1536, 2048, 2176, 2304
NORM_GROUPS = (0, 1, 2, 3, 6, 7, 8)


def _cparams(sem):
    return pltpu.CompilerParams(dimension_semantics=sem, vmem_limit_bytes=VMEM_LIMIT)


def _t5_bucket_np(dist):
    n = np.maximum(dist, 0)
    max_exact = N_BUCKETS // 2
    nf = np.maximum(n, 1).astype(np.float32)
    large = max_exact + (np.log(nf / np.float32(max_exact)) / np.float32(math.log(MAX_DISTANCE / max_exact))
                         * np.float32(N_BUCKETS - max_exact)).astype(np.int32)
    large = np.minimum(large, N_BUCKETS - 1)
    return np.where(n < max_exact, n, large)


def _bias_tables(rel_bias, tq):
    nd = 2 * BLOCK
    buckets = _t5_bucket_np(np.arange(nd))
    assert (buckets[MAX_DISTANCE:] == N_BUCKETS - 1).all()
    rb = rel_bias.astype(F32)
    r = np.arange(BLOCK)[:, None]
    c = np.arange(BLOCK)[None, :]
    d_own = r - c
    d_prev = BLOCK + r - c
    far = rb[N_BUCKETS - 1]

    def take(dist):
        idx = jnp.asarray(buckets[np.clip(dist, 0, nd - 1)], jnp.int32)[None]
        out = jnp.zeros((rb.shape[1],) + dist.shape, F32)
        for b in range(N_BUCKETS):
            out = jnp.where(idx == b, rb[b].reshape((-1,) + (1,) * dist.ndim), out)
        return out

    hd = slice(0, N_DIFF_HEADS)
    far_d = far[hd][:, None, None]
    d0 = jnp.where(d_own[None] >= 0, take(d_own)[hd] - far_d, NEG)
    d1 = take(d_prev)[hd] - far_d
    dblk = jnp.stack([d0, d1], axis=1)
    rq = np.arange(tq)[:, None]
    cm = np.arange(LANES)[None, :]
    d_meta = N_META + rq - cm
    bm0 = jnp.where((cm < N_META)[None], take(d_meta)[hd] - far_d, NEG)

    hs = slice(N_DIFF_HEADS, N_DIFF_HEADS + N_SWA_HEADS)
    far_s = far[hs][:, None, None]
    d_meta_s = N_META + r - cm
    meta_first = jnp.where((cm < N_META)[None], take(d_meta_s)[hs], NEG)
    meta_rest = jnp.where((cm < N_META)[None], jnp.broadcast_to(far_s, (N_SWA_HEADS, BLOCK, LANES)), NEG)
    prev_rest = jnp.where((c > r)[None], take(d_prev)[hs], NEG)
    prev_first = jnp.full((N_SWA_HEADS, BLOCK, BLOCK), NEG, F32)
    own = jnp.where((d_own >= 0)[None], take(d_own)[hs], NEG)
    bt = jnp.stack([jnp.concatenate([meta_first, prev_first, own], axis=-1),
                    jnp.concatenate([meta_rest, prev_rest, own], axis=-1)], axis=0)
    return dblk.astype(F32), bm0.astype(F32), bt.astype(F32)


def _proj_kernel(x_ref, g1_ref, w_ref, bd_ref, gain_ref, nmask_ref, o_ref):
    x = x_ref[...]
    a = x * lax.rsqrt(jnp.mean(x * x, axis=-1, keepdims=True) + EPS) * g1_ref[...]
    p = jnp.dot(a.astype(BF16), w_ref[...], preferred_element_type=F32)
    bd = bd_ref[...]
    for j in range(C_END // MXU_DIM):
        sl = slice(j * MXU_DIM, (j + 1) * MXU_DIM)
        pj = p[:, sl]
        if j in NORM_GROUPS:
            ms = jnp.dot((pj * pj).astype(BF16), bd, preferred_element_type=F32)
            pj = jnp.where(nmask_ref[:, sl] != 0.0, pj * lax.rsqrt(ms + EPS) * gain_ref[:, sl], pj)
        o_ref[:, sl] = pj.astype(BF16)


def _proj(x2, g1, w, bd, gain, nmask, tm):
    n = x2.shape[0]
    return pl.pallas_call(
        _proj_kernel,
        out_shape=jax.ShapeDtypeStruct((n, C_END), BF16),
        grid=(n // tm,),
        in_specs=[
            pl.BlockSpec((tm, x2.shape[1]), lambda i: (i, 0)),
            pl.BlockSpec(g1.shape, lambda i: (0, 0)),
            pl.BlockSpec(w.shape, lambda i: (0, 0)),
            pl.BlockSpec(bd.shape, lambda i: (0, 0)),
            pl.BlockSpec(gain.shape, lambda i: (0, 0)),
            pl.BlockSpec(nmask.shape, lambda i: (0, 0)),
        ],
        out_specs=pl.BlockSpec((tm, C_END), lambda i: (i, 0)),
        compiler_params=_cparams(("parallel",)),
        name="proj",
    )(x2, g1, w, bd, gain, nmask)


def _diff_kernel(qi_tab, t_tab, q_ref, k_ref, v_ref, km_ref, vm_ref, d_ref, bm0_ref, lamv_ref, gain_ref, o_ref,
                 bias_ref, qs_ref, vt_ref, s_buf, p_buf, a_buf, m_ref, acc_ref, *, lambda_init, n_steps):
    tq = TQ
    nq = q_ref.shape[0] // tq
    nb = tq // BLOCK
    BIAS_NONE, BIAS_LEFT, BIAS_DIAG, BIAS_META, BIAS_META0 = 0, 1, 2, 3, 4

    d0 = d_ref[0, 0] * LOG2E
    d1 = d_ref[0, 1] * LOG2E
    zeros = jnp.zeros((BLOCK, BLOCK), F32)
    bias_ref[BIAS_NONE] = jnp.zeros((tq, tq), F32)
    for a in range(nb):
        for b in range(nb):
            rs, cs = slice(a * BLOCK, (a + 1) * BLOCK), slice(b * BLOCK, (b + 1) * BLOCK)
            if a == b:
                blk = d0
            elif b == a + 1:
                blk = d1
            elif b > a:
                blk = zeros
            else:
                blk = jnp.full((BLOCK, BLOCK), NEG, F32)
            bias_ref[BIAS_DIAG, rs, cs] = blk
            bias_ref[BIAS_LEFT, rs, cs] = d1 if (b == 0 and a == nb - 1) else zeros
    row_m = lax.broadcasted_iota(jnp.int32, (tq, tq), 0)
    bias_ref[BIAS_META] = jnp.where(row_m < N_META, 0.0, NEG).astype(F32)
    bias_ref[BIAS_META0, :LANES, :] = bm0_ref[0] * LOG2E
    bias_ref[BIAS_META0, LANES:, :] = jnp.full((tq - LANES, tq), NEG, F32)

    lane = lax.broadcasted_iota(jnp.int32, (tq, LANES), 1)
    for i in range(nq):
        rows = slice(i * tq, (i + 1) * tq)
        q = q_ref[rows, :].astype(F32)
        qs_ref[i] = jnp.transpose(jnp.concatenate([jnp.where(lane < HEAD_DIM, q, 0.0),
                                                   jnp.where(lane >= HEAD_DIM, q, 0.0)], axis=0)).astype(BF16)
        vt_ref[i, :LANES, :] = jnp.transpose(v_ref[rows, :].astype(F32)).astype(BF16)
    vt_ref[nq, :LANES, :] = jnp.transpose(vm_ref[...].astype(F32)).astype(BF16)
    vt_ref[:, LANES:, :] = jnp.ones((nq + 1, ONES_ROWS, tq), BF16)
    acc_ref[...] = jnp.zeros(acc_ref.shape, F32)
    m_ref[...] = jnp.full(m_ref.shape, NEG, F32)
    lv = lamv_ref[...]
    lam = (jnp.exp(jnp.sum(lv[0:1] * lv[1:2], axis=-1, keepdims=True))
           - jnp.exp(jnp.sum(lv[2:3] * lv[3:4], axis=-1, keepdims=True)) + lambda_init)

    def seq_row(t):
        return pl.multiple_of(jnp.maximum(t - 1, 0) * tq, tq)

    def stage_a(n, slot):
        qi, t = qi_tab[n], t_tab[n]
        kt = jnp.where(t == 0, km_ref[...], k_ref[pl.ds(seq_row(t), tq), :])
        s = jnp.dot(kt, qs_ref[qi], preferred_element_type=F32)
        which = jnp.where(t == 0, jnp.where(qi == 0, BIAS_META0, BIAS_META),
                          jnp.where(t == qi + 1, BIAS_DIAG, jnp.where(t == qi, BIAS_LEFT, BIAS_NONE)))
        s_buf[slot] = s + jnp.tile(bias_ref[which], (1, 2))

    def stage_b(n, slot):
        s = s_buf[slot]
        m_prev = jnp.where(t_tab[n] == 0, NEG, m_ref[...])
        m_new = jnp.maximum(m_prev, jnp.max(s, axis=0, keepdims=True))
        a_buf[slot] = jnp.exp2(m_prev - m_new)
        p_buf[slot] = jnp.exp2(s - m_new[0:1]).astype(BF16)
        m_ref[...] = m_new

    def stage_c(n, slot):
        qi, t = qi_tab[n], t_tab[n]
        par = qi % ACC_BUFS
        vt = vt_ref[jnp.where(t == 0, nq, t - 1)]
        pv = jnp.dot(vt, p_buf[slot], preferred_element_type=F32)
        acc_ref[par] = a_buf[slot][0:1] * acc_ref[par] + pv

    def finish(n):
        qi, t = qi_tab[n], t_tab[n]

        @pl.when(t == qi + 1)
        def _():
            acc = acc_ref[qi % ACC_BUFS]
            o = acc[:LANES] * (1.0 / acc[LANES:LANES + 1])
            d = o[:, :tq] - lam * o[:, tq:]
            y = d * lax.rsqrt(jnp.mean(d * d, axis=0, keepdims=True) + EPS) * jnp.tile(gain_ref[...], (1, tq // LANES))
            y = jnp.transpose(y * (1.0 - lambda_init))
            o_ref[pl.ds(pl.multiple_of(qi * tq, tq), tq), :] = y.astype(BF16)

    stage_a(0, 0)
    stage_a(1, 1)
    stage_b(0, 0)

    def steps(n, count):
        for j in range(count):
            stage_a(n + j + 2, j % 2)
            stage_b(n + j + 1, (j + 1) % 2)
            stage_c(n + j, j % 2)
        for j in range(count):
            finish(n + j)

    unroll = 8
    assert unroll <= 3 * (ACC_BUFS - 1) + 1
    n_blocks = (n_steps - 2) // unroll

    def block(k, carry):
        steps(unroll * k, unroll)
        return carry
    lax.fori_loop(0, n_blocks, block, 0)
    steps(unroll * n_blocks, n_steps - 2 - unroll * n_blocks)

    stage_b(n_steps - 1, 1)
    stage_c(n_steps - 2, 0)
    stage_c(n_steps - 1, 1)
    finish(n_steps - 2)
    finish(n_steps - 1)


def _diff_attention(qkv, km, vm, dblk, bm0, lamv, gain, batch, seq, lambda_init):
    nq = seq // TQ
    steps = [(qi, t) for qi in range(nq) for t in range(qi + 2)]
    assert len(steps) % 2 == 0
    qi_tab = jnp.asarray([s[0] for s in steps], jnp.int32)
    t_tab = jnp.asarray([s[1] for s in steps], jnp.int32)
    kern = functools.partial(_diff_kernel, lambda_init=lambda_init, n_steps=len(steps))
    return pl.pallas_call(
        kern,
        out_shape=jax.ShapeDtypeStruct((batch * seq, N_DIFF_HEADS * LANES), BF16),
        grid_spec=pltpu.PrefetchScalarGridSpec(
            num_scalar_prefetch=2,
            grid=(batch, N_DIFF_HEADS),
            in_specs=[
                pl.BlockSpec((seq, LANES), lambda b, h, *_: (b, C_DQ // LANES + h)),
                pl.BlockSpec((seq, LANES), lambda b, h, *_: (b, C_DK // LANES + h)),
                pl.BlockSpec((seq, LANES), lambda b, h, *_: (b, C_DV // LANES + h)),
                pl.BlockSpec((TQ, LANES), lambda b, h, *_: (0, h)),
                pl.BlockSpec((TQ, LANES), lambda b, h, *_: (0, h)),
                pl.BlockSpec((1, 2, BLOCK, BLOCK), lambda b, h, *_: (h, 0, 0, 0)),
                pl.BlockSpec((1, LANES, TQ), lambda b, h, *_: (h, 0, 0)),
                pl.BlockSpec(lamv.shape, lambda b, h, *_: (0, 0)),
                pl.BlockSpec((LANES, LANES), lambda b, h, *_: (0, 0)),
            ],
            out_specs=pl.BlockSpec((seq, LANES), lambda b, h, *_: (b, h)),
            scratch_shapes=[
                pltpu.VMEM((5, TQ, TQ), F32),
                pltpu.VMEM((nq, LANES, 2 * TQ), BF16),
                pltpu.VMEM((nq + 1, LANES + ONES_ROWS, TQ), BF16),
                pltpu.VMEM((2, TQ, 2 * TQ), F32),
                pltpu.VMEM((2, TQ, 2 * TQ), BF16),
                pltpu.VMEM((2, 8, 2 * TQ), F32),
                pltpu.VMEM((8, 2 * TQ), F32),
                pltpu.VMEM((ACC_BUFS, LANES + ONES_ROWS, 2 * TQ), F32),
            ],
        ),
        compiler_params=_cparams(("parallel", "parallel")),
        name="diff_attention",
    )(qi_tab, t_tab, qkv, qkv, qkv, km, vm, jnp.swapaxes(dblk, -1, -2), jnp.swapaxes(bm0, -1, -2), lamv,
      jnp.broadcast_to(gain.reshape(LANES, 1), (LANES, LANES)))


def _swa_kernel(sink_ref, q_ref, k_ref, v_ref, km_ref, vm_ref, bt_ref, o_ref, kd_ref, vt_ref,
                s_scr, p_scr, inv_scr):
    ci = pl.program_id(1)
    nblk = q_ref.shape[0] // BLOCK
    nkb = k_ref.shape[0] // BLOCK
    lane = lax.broadcasted_iota(jnp.int32, (BLOCK, LANES), 1)

    def both_halves(k):
        k0, k1 = k[:, :HEAD_DIM], k[:, HEAD_DIM:]
        return jnp.concatenate([k0, k0, k1, k1], axis=1)

    @pl.when(ci == 0)
    def _per_batch():
        def body(j, carry):
            rows = pl.ds(pl.multiple_of(j * BLOCK, BLOCK), BLOCK)
            kd_ref[j] = both_halves(k_ref[rows, :])
            vt_ref[j] = jnp.transpose(v_ref[rows, :].astype(F32)).astype(BF16)
            return carry
        lax.fori_loop(0, nkb, body, 0)
        kd_ref[nkb] = both_halves(km_ref[...])
        vt_ref[nkb] = jnp.transpose(vm_ref[...].astype(F32)).astype(BF16)

    def block_body(n, carry):
        gblk = ci * nblk + n
        first = jnp.where(gblk == 0, 0, 1)
        prev = jnp.maximum(gblk - 1, 0)
        r_q = pl.multiple_of(n * BLOCK, BLOCK)
        pairs = [(g, u) for g in range(N_SWA_KV) for u in range(2)]
        for c, (g, u) in enumerate(pairs):
            ks = slice(g * LANES, (g + 1) * LANES)
            kcat = jnp.concatenate([kd_ref[nkb, :, ks], kd_ref[prev, :, ks], kd_ref[gblk, :, ks]], axis=0)
            h0 = 4 * g + 2 * u
            qp = q_ref[pl.ds(r_q, BLOCK), (2 * g + u) * LANES:(2 * g + u + 1) * LANES].astype(F32)
            qs = jnp.transpose(jnp.concatenate([jnp.where(lane < HEAD_DIM, qp, 0.0),
                                                jnp.where(lane >= HEAD_DIM, qp, 0.0)], axis=0)).astype(BF16)
            s = jnp.dot(kcat, qs, preferred_element_type=F32)
            s_scr[c] = s + jnp.concatenate([bt_ref[first, h0], bt_ref[first, h0 + 1]], axis=1)
        for c, (g, u) in enumerate(pairs):
            h0 = 4 * g + 2 * u
            s = s_scr[c]
            sink = jnp.concatenate([sink_ref[h0:h0 + 1, :], sink_ref[h0 + 1:h0 + 2, :]], axis=1)
            m = jnp.maximum(jnp.max(s, axis=0, keepdims=True), sink)
            p = jnp.exp(s - m)
            p_scr[c] = p.astype(BF16)
            inv_scr[c] = jnp.broadcast_to(1.0 / (jnp.sum(p, axis=0, keepdims=True) + jnp.exp(sink - m)),
                                          inv_scr.shape[1:])
        for c, (g, u) in enumerate(pairs):
            vs = slice(g * HEAD_DIM, (g + 1) * HEAD_DIM)
            vcat = jnp.concatenate([vt_ref[nkb, vs, :], vt_ref[prev, vs, :], vt_ref[gblk, vs, :]], axis=1)
            o = jnp.dot(vcat, p_scr[c], preferred_element_type=F32) * inv_scr[c][0:1]
            ot = jnp.transpose(o)
            o_ref[pl.ds(r_q, BLOCK), (2 * g + u) * LANES:(2 * g + u + 1) * LANES] = (
                jnp.concatenate([ot[:BLOCK], ot[BLOCK:]], axis=1).astype(BF16))
        return carry

    lax.fori_loop(0, nblk, block_body, 0)


def _swa_attention(sinks, qkv, km, vm, bt, batch, seq):
    nc = seq // CQ
    nkb = seq // BLOCK
    sinkv = jnp.broadcast_to(sinks.reshape(N_SWA_HEADS, 1), (N_SWA_HEADS, LANES))
    return pl.pallas_call(
        _swa_kernel,
        out_shape=jax.ShapeDtypeStruct((batch * seq, N_SWA_HEADS * HEAD_DIM), BF16),
        grid=(batch, nc),
        in_specs=[
            pl.BlockSpec(sinkv.shape, lambda b, c: (0, 0)),
            pl.BlockSpec((CQ, 512), lambda b, c: (b * nc + c, C_SQ // 512)),
            pl.BlockSpec((seq, LANES), lambda b, c: (b, C_SK // LANES)),
            pl.BlockSpec((seq, LANES), lambda b, c: (b, C_SV // LANES)),
            pl.BlockSpec(km.shape, lambda b, c: (0, 0)),
            pl.BlockSpec(vm.shape, lambda b, c: (0, 0)),
            pl.BlockSpec(bt.shape, lambda b, c: (0, 0, 0, 0)),
        ],
        out_specs=pl.BlockSpec((CQ, 512), lambda b, c: (b * nc + c, 0)),
        scratch_shapes=[pltpu.VMEM((nkb + 1, BLOCK, 2 * LANES), BF16),
                        pltpu.VMEM((nkb + 1, LANES, BLOCK), BF16),
                        pltpu.VMEM((4, 3 * BLOCK, 2 * BLOCK), F32),
                        pltpu.VMEM((4, 3 * BLOCK, 2 * BLOCK), BF16),
                        pltpu.VMEM((4, 8, 2 * BLOCK), F32)],
        compiler_params=_cparams(("parallel", "arbitrary")),
        name="swa_attention",
    )(sinkv, qkv, qkv, qkv, km, vm, bt)


def _outproj_kernel(x_ref, md_ref, ms_ref, wo_ref, g2_ref, wr_ref, br_ref,
                    h_ref, hb_ref, rt_ref, ti_ref, cnt_ref, c_ref):
    i = pl.program_id(0)

    @pl.when(i == 0)
    def _init():
        c_ref[...] = jnp.zeros(c_ref.shape, F32)

    for sub in range(x_ref.shape[0] // TM):
        _route_tile(slice(sub * TM, (sub + 1) * TM), slice(sub * 8, (sub + 1) * 8),
                    x_ref, md_ref, ms_ref, wo_ref, g2_ref, wr_ref, br_ref, h_ref, hb_ref, rt_ref, ti_ref, c_ref)

    @pl.when(i == pl.num_programs(0) - 1)
    def _fin():
        cnt_ref[...] = c_ref[...]


def _route_tile(rows, ti_rows, x_ref, md_ref, ms_ref, wo_ref, g2_ref, wr_ref, br_ref,
                h_ref, hb_ref, rt_ref, ti_ref, c_ref):
    tm = TM
    half = md_ref.shape[1]
    h = (x_ref[rows, :]
         + jnp.dot(md_ref[rows, :], wo_ref[:half, :], preferred_element_type=F32)
         + jnp.dot(ms_ref[rows, :], wo_ref[half:, :], preferred_element_type=F32))
    h_ref[rows, :] = h
    hn = h * lax.rsqrt(jnp.mean(h * h, axis=-1, keepdims=True) + EPS) * g2_ref[...]
    hb = hn.astype(BF16)
    hb_ref[rows, :] = hb

    lg = jnp.dot(hb, wr_ref[...], preferred_element_type=F32) + br_ref[...]
    lane_i = lax.broadcasted_iota(jnp.int32, lg.shape, 1)
    lane = lane_i.astype(F32)
    big = float(4 * LANES)
    is_g = (lane_i >= N_EXPERTS) & (lane_i < N_EXPERTS + N_GROUPS)
    glm = jnp.where(is_g, lg, -jnp.inf)
    gmax = jnp.max(glm, axis=1, keepdims=True)
    gidx = jnp.min(jnp.where(glm == gmax, lane, big), axis=1, keepdims=True) - N_EXPERTS
    gsum = jnp.sum(jnp.where(is_g, jnp.exp(lg - gmax), 0.0), axis=1, keepdims=True)
    g_w = 1.0 / gsum
    lane_grp = (lane_i >> 3).astype(F32)
    in_grp = (lane_i < N_EXPERTS) & (lane_grp == gidx)
    el = jnp.where(in_grp, lg, -jnp.inf)
    t1 = jnp.max(el, axis=1, keepdims=True)
    j1 = jnp.min(jnp.where(el == t1, lane, big), axis=1, keepdims=True)
    el2 = jnp.where(lane == j1, -jnp.inf, el)
    t2 = jnp.max(el2, axis=1, keepdims=True)
    j2 = jnp.min(jnp.where(el2 == t2, lane, big), axis=1, keepdims=True)
    e2 = jnp.exp(t2 - t1)
    den = 1.0 + e2
    gate1 = g_w / den
    gate2 = g_w * e2 / den

    o1 = lane == j1
    o2 = lane == j2
    onehot = jnp.where(o1 | o2, 1.0, 0.0).astype(BF16)
    rr = lax.broadcasted_iota(jnp.int32, (tm, tm), 0)
    cc = lax.broadcasted_iota(jnp.int32, (tm, tm), 1)
    lower = jnp.where(rr > cc, 1.0, 0.0).astype(BF16)
    pfx = jnp.dot(lower, onehot, preferred_element_type=F32)
    cnt_tile = jnp.sum(onehot.astype(F32), axis=0, keepdims=True)
    groups = jnp.floor((cnt_tile + (ROW_ALIGN - 1)) * (1.0 / ROW_ALIGN))
    er = lax.broadcasted_iota(jnp.int32, (LANES, LANES), 0)
    ec = lax.broadcasted_iota(jnp.int32, (LANES, LANES), 1)
    before = jnp.where(er < ec, 1.0, 0.0).astype(BF16)
    cbase = ROW_ALIGN * jnp.dot(jnp.broadcast_to(groups, (8, LANES)).astype(BF16), before,
                                preferred_element_type=F32)[0:1]
    at = pfx + cbase
    pos1 = jnp.sum(jnp.where(o1, at, 0.0), axis=1, keepdims=True)
    pos2 = jnp.sum(jnp.where(o2, at, 0.0), axis=1, keepdims=True)
    rt_ref[rows, :] = jnp.where(lane_i == 0, gate1,
                                jnp.where(lane_i == 1, gate2,
                                          jnp.where(lane_i == 2, pos1,
                                                    jnp.where(lane_i == 3, pos2, 0.0))))
    c_old = c_ref[...]
    c_ref[...] = c_old + groups * ROW_ALIGN
    row8 = lax.broadcasted_iota(jnp.int32, (8, LANES), 0)
    ti_ref[ti_rows, :] = jnp.where(row8 == 0, cnt_tile, jnp.where(row8 == 1, c_old, 0.0))


def _outproj(x2, mixd, mixs, wo, g2, wr, br):
    n, d = x2.shape
    sub = TO // TM
    return pl.pallas_call(
        _outproj_kernel,
        out_shape=(jax.ShapeDtypeStruct((n, d), F32),
                   jax.ShapeDtypeStruct((n, d), BF16),
                   jax.ShapeDtypeStruct((n, LANES), F32),
                   jax.ShapeDtypeStruct((n // TM * 8, LANES), F32),
                   jax.ShapeDtypeStruct((8, LANES), F32)),
        grid=(n // TO,),
        in_specs=[
            pl.BlockSpec((TO, d), lambda i: (i, 0)),
            pl.BlockSpec((TO, mixd.shape[1]), lambda i: (i, 0)),
            pl.BlockSpec((TO, mixs.shape[1]), lambda i: (i, 0)),
            pl.BlockSpec(wo.shape, lambda i: (0, 0)),
            pl.BlockSpec(g2.shape, lambda i: (0, 0)),
            pl.BlockSpec(wr.shape, lambda i: (0, 0)),
            pl.BlockSpec(br.shape, lambda i: (0, 0)),
        ],
        out_specs=(pl.BlockSpec((TO, d), lambda i: (i, 0)),
                   pl.BlockSpec((TO, d), lambda i: (i, 0)),
                   pl.BlockSpec((TO, LANES), lambda i: (i, 0)),
                   pl.BlockSpec((8 * sub, LANES), lambda i: (i, 0)),
                   pl.BlockSpec((8, LANES), lambda i: (0, 0))),
        scratch_shapes=[pltpu.VMEM((8, LANES), F32)],
        compiler_params=_cparams(("arbitrary",)),
        name="outproj_router",
    )(x2, mixd, mixs, wo, g2, wr, br)


def _for_each_chunk(runs_ref, fn):
    def per_expert(e, sorted_row, priority):
        start = runs_ref[0, 0, e]
        groups = runs_ref[0, 0, N_EXPERTS + e]
        whole = groups // (CHUNK // ROW_ALIGN)

        def per_chunk(c, carry):
            fn(pl.multiple_of(start + c * CHUNK, ROW_ALIGN), pl.multiple_of(sorted_row + c * CHUNK, ROW_ALIGN),
               CHUNK, priority)
            return carry
        lax.fori_loop(0, whole, per_chunk, 0)

        @pl.when(groups % (CHUNK // ROW_ALIGN) == 1)
        def _():
            fn(pl.multiple_of(start + whole * CHUNK, ROW_ALIGN),
               pl.multiple_of(sorted_row + whole * CHUNK, ROW_ALIGN), ROW_ALIGN, priority)
        return sorted_row + groups * ROW_ALIGN

    def expert_pair(e2, sorted_row):
        return per_expert(2 * e2 + 1, per_expert(2 * e2, sorted_row, 0), 1)
    lax.fori_loop(0, N_EXPERTS // 2, expert_pair, 0)


def _wait_chunks(runs_ref, make_copy):
    for k, rows in enumerate((CHUNK, ROW_ALIGN)):
        def body(c, carry, rows=rows):
            make_copy(rows).wait()
            return carry
        lax.fori_loop(0, runs_ref[0, 0, 2 * N_EXPERTS + k], body, 0)


def _dispatch_kernel(zf_ref, cur_ref, prv_ref, hb_ref, rt_ref, xs_ref, sbuf, zbuf, sem, zsem):
    i = pl.program_id(0)
    nt = pl.num_programs(0)
    slot = i % 2
    tm, d = hb_ref.shape

    @pl.when(i == 0)
    def _():
        zbuf[...] = jnp.zeros(zbuf.shape, zbuf.dtype)

        def zero_copy(b):
            return pltpu.make_async_copy(zbuf, xs_ref.at[pl.ds(pl.multiple_of(b * EB, EB), EB)], zsem)

        def start(b, carry):
            @pl.when(zf_ref[b] == 1)
            def _():
                zero_copy(b).start()
            return carry
        lax.fori_loop(0, zf_ref.shape[0], start, 0)

        def wait(b, carry):
            @pl.when(zf_ref[b] == 1)
            def _():
                zero_copy(b).wait()
            return carry
        lax.fori_loop(0, zf_ref.shape[0], wait, 0)

    pos_t = jnp.transpose(rt_ref[...])
    srow = lax.broadcasted_iota(jnp.int32, (SROWS, tm), 0).astype(F32)
    sel = jnp.where(srow == pos_t[2:3, :], 1.0, jnp.where(srow == pos_t[3:4, :], 1.0, 0.0)).astype(BF16)
    srt = jnp.dot(sel, hb_ref[...], preferred_element_type=F32)
    bits = pltpu.bitcast(srt, jnp.uint32)
    sbuf[slot] = (bits[:, d // 2:] & jnp.uint32(0xFFFF0000)) | (bits[:, :d // 2] >> 16)

    def chunk_copy(run_row, sorted_row, rows, sl):
        return pltpu.make_async_copy(sbuf.at[sl, pl.ds(sorted_row, rows)], xs_ref.at[pl.ds(run_row, rows)],
                                     sem.at[sl])

    _for_each_chunk(cur_ref, lambda run_row, sorted_row, rows, priority:
                    chunk_copy(run_row, sorted_row, rows, slot).start(priority=priority))

    @pl.when(i > 0)
    def _():
        _wait_chunks(prv_ref, lambda rows: chunk_copy(0, 0, rows, 1 - slot))

    @pl.when(i == nt - 1)
    def _():
        _wait_chunks(cur_ref, lambda rows: chunk_copy(0, 0, rows, slot))


def _dispatch(zero_blocks, runs, hb, rt, n_rows):
    n, d = hb.shape
    return pl.pallas_call(
        _dispatch_kernel,
        out_shape=jax.ShapeDtypeStruct((n_rows, d // 2), jnp.uint32),
        grid_spec=pltpu.PrefetchScalarGridSpec(
            num_scalar_prefetch=1,
            grid=(n // TM,),
            in_specs=[
                pl.BlockSpec((1, 1, LANES), lambda i, zf: (i, 0, 0), memory_space=pltpu.SMEM),
                pl.BlockSpec((1, 1, LANES), lambda i, zf: (jnp.maximum(i - 1, 0), 0, 0), memory_space=pltpu.SMEM),
                pl.BlockSpec((TM, d), lambda i, zf: (i, 0)),
                pl.BlockSpec((TM, LANES), lambda i, zf: (i, 0)),
            ],
            out_specs=pl.BlockSpec(memory_space=pl.ANY),
            scratch_shapes=[pltpu.VMEM((2, SROWS, d // 2), jnp.uint32), pltpu.VMEM((EB, d // 2), jnp.uint32),
                            pltpu.SemaphoreType.DMA((2,)), pltpu.SemaphoreType.DMA(())],
        ),
        compiler_params=_cparams(("arbitrary",)),
        name="dispatch",
    )(zero_blocks, runs, runs, hb, rt)


def _experts_kernel(be_ref, na_ref, nxt_ref, xs_ref, wg_hbm, wu_hbm, wd_hbm, ys_ref,
                    wgf, wuf, wdf, wgb, wub, wdb, sem):
    b = pl.program_id(0)

    def weight_copies(e):
        return (pltpu.make_async_copy(wg_hbm.at[e], wgf, sem.at[0]),
                pltpu.make_async_copy(wu_hbm.at[e], wuf, sem.at[1]),
                pltpu.make_async_copy(wd_hbm.at[e], wdf, sem.at[2]))

    @pl.when(b == 0)
    def _():
        for c in weight_copies(be_ref[0]):
            c.start()

    @pl.when(b < na_ref[0])
    def _():
        e = be_ref[b]
        changed = jnp.logical_or(b == 0, be_ref[jnp.maximum(b - 1, 0)] != e)

        @pl.when(changed)
        def _load():
            for c in weight_copies(e):
                c.wait()
            wgb[...] = wgf[...].astype(BF16)
            wub[...] = wuf[...].astype(BF16)
            wdb[...] = wdf[...].astype(BF16)
            nxt = nxt_ref[e]

            @pl.when(nxt >= 0)
            def _():
                for c in weight_copies(nxt):
                    c.start()

        w = xs_ref[...]
        x_lo = pltpu.bitcast(w << 16, F32).astype(BF16)
        x_hi = pltpu.bitcast(w & jnp.uint32(0xFFFF0000), F32).astype(BF16)
        dh = w.shape[1]
        g = (jnp.dot(x_lo, wgb[:dh, :], preferred_element_type=F32)
             + jnp.dot(x_hi, wgb[dh:, :], preferred_element_type=F32))
        u = (jnp.dot(x_lo, wub[:dh, :], preferred_element_type=F32)
             + jnp.dot(x_hi, wub[dh:, :], preferred_element_type=F32))
        hdn = g * (1.0 / (1.0 + jnp.exp(-g))) * u
        y = jnp.dot(hdn.astype(BF16), wdb[...], preferred_element_type=F32)
        bits = pltpu.bitcast(y.astype(BF16).astype(F32), jnp.uint32)
        ys_ref[...] = (bits[:, dh:] & jnp.uint32(0xFFFF0000)) | (bits[:, :dh] >> 16)

    @pl.when(b >= na_ref[0])
    def _():
        ys_ref[...] = jnp.zeros(ys_ref.shape, ys_ref.dtype)


def _experts(blk_e, n_act, nxt_e, xs, w_gate, w_up, w_down):
    p, dh = xs.shape
    d = 2 * dh
    de = w_gate.shape[2]

    def row_map(b, be, na, nx):
        return (jnp.minimum(b, na[0] - 1), 0)

    return pl.pallas_call(
        _experts_kernel,
        out_shape=jax.ShapeDtypeStruct((p, dh), jnp.uint32),
        grid_spec=pltpu.PrefetchScalarGridSpec(
            num_scalar_prefetch=3,
            grid=(p // EB,),
            in_specs=[
                pl.BlockSpec((EB, dh), row_map),
                pl.BlockSpec(memory_space=pl.ANY),
                pl.BlockSpec(memory_space=pl.ANY),
                pl.BlockSpec(memory_space=pl.ANY),
            ],
            out_specs=pl.BlockSpec((EB, dh), lambda b, be, na, nx: (b, 0)),
            scratch_shapes=[pltpu.VMEM((d, de), F32), pltpu.VMEM((d, de), F32), pltpu.VMEM((de, d), F32),
                            pltpu.VMEM((d, de), BF16), pltpu.VMEM((d, de), BF16), pltpu.VMEM((de, d), BF16),
                            pltpu.SemaphoreType.DMA((3,))],
        ),
        compiler_params=_cparams(("arbitrary",)),
        name="experts",
    )(blk_e, n_act, nxt_e, xs, w_gate, w_up, w_down)


def _combine_kernel(cur_ref, nxt_ref, ys_ref, h_ref, rt_ref, o_ref, ybuf, sem):
    i = pl.program_id(0)
    nt = pl.num_programs(0)
    slot = i % 2
    tm = h_ref.shape[0]

    def chunk_copy(run_row, sorted_row, rows, sl):
        return pltpu.make_async_copy(ys_ref.at[pl.ds(run_row, rows)], ybuf.at[sl, pl.ds(sorted_row, rows)],
                                     sem.at[sl])

    @pl.when(i == 0)
    def _():
        ybuf[...] = jnp.zeros(ybuf.shape, ybuf.dtype)
        _for_each_chunk(cur_ref, lambda run_row, sorted_row, rows, priority:
                        chunk_copy(run_row, sorted_row, rows, 0).start(priority=priority))

    @pl.when(i + 1 < nt)
    def _():
        _for_each_chunk(nxt_ref, lambda run_row, sorted_row, rows, priority:
                        chunk_copy(run_row, sorted_row, rows, 1 - slot).start(priority=priority))

    _wait_chunks(cur_ref, lambda rows: chunk_copy(0, 0, rows, slot))

    rt = rt_ref[...]
    w = ybuf[slot]
    dh = w.shape[1]
    y_lo = pltpu.bitcast(w << 16, F32).astype(BF16)
    y_hi = pltpu.bitcast(w & jnp.uint32(0xFFFF0000), F32).astype(BF16)
    col = lax.broadcasted_iota(jnp.int32, (tm, SROWS), 1).astype(F32)
    wsel = jnp.where(col == rt[:, 2:3], rt[:, 0:1], jnp.where(col == rt[:, 3:4], rt[:, 1:2], 0.0)).astype(BF16)
    for half, yb in ((slice(0, dh), y_lo), (slice(dh, 2 * dh), y_hi)):
        o_ref[:, half] = h_ref[:, half] + jnp.dot(wsel, yb, preferred_element_type=F32)


def _combine(runs, ys, h1, rt):
    n, d = h1.shape
    nt = n // TM
    return pl.pallas_call(
        _combine_kernel,
        out_shape=jax.ShapeDtypeStruct((n, d), F32),
        grid=(nt,),
        in_specs=[
            pl.BlockSpec((1, 1, LANES), lambda i: (i, 0, 0), memory_space=pltpu.SMEM),
            pl.BlockSpec((1, 1, LANES), lambda i: (jnp.minimum(i + 1, nt - 1), 0, 0), memory_space=pltpu.SMEM),
            pl.BlockSpec(memory_space=pl.ANY),
            pl.BlockSpec((TM, d), lambda i: (i, 0)),
            pl.BlockSpec((TM, LANES), lambda i: (i, 0)),
        ],
        out_specs=pl.BlockSpec((TM, d), lambda i: (i, 0)),
        scratch_shapes=[pltpu.VMEM((2, SROWS, d // 2), jnp.uint32), pltpu.SemaphoreType.DMA((2,))],
        compiler_params=_cparams(("arbitrary",)),
        name="combine",
    )(runs, runs, ys, h1, rt)


def kernel(x, meta_tokens, rel_bias, norm1_gain, w_in, diff_q_gain, diff_k_gain, lam_q1, lam_k1, lam_q2, lam_k2, diff_subln_gain, swa_q_gain, swa_k_gain, swa_sinks, w_out, norm2_gain, w_group, b_group, w_router, b_router, w_gate, w_up, w_down):
    batch, seq, d = x.shape
    depth = w_in.shape[0]
    n = batch * seq
    assert seq % CQ == 0 and seq % TQ == 0 and n % TM == 0 and d == 1024
    assert meta_tokens.shape[0] == N_META
    assert depth == 1, "the meta-token rows of the residual stream are not carried across layers"

    h = x.reshape(n, d)
    dblk, bm0, bt = _bias_tables(rel_bias, TQ)
    scale = HEAD_DIM ** -0.5
    bd = jnp.asarray(np.kron(np.eye(MXU_DIM // HEAD_DIM), np.full((HEAD_DIM, HEAD_DIM), 1.0 / HEAD_DIM)), BF16)
    ones = jnp.ones((HEAD_DIM,), F32)
    lower_pad = N_EXPERTS + N_GROUPS

    for layer in range(depth):
        lambda_init = 0.8 - 0.6 * math.exp(-0.3 * layer)
        w_cat = w_in[layer].astype(BF16)
        gain = jnp.concatenate([
            jnp.tile(diff_q_gain[layer] * (scale * LOG2E), 2 * N_DIFF_HEADS),
            jnp.tile(diff_k_gain[layer], 2 * N_DIFF_HEADS),
            jnp.tile(ones, 2 * N_DIFF_HEADS),
            jnp.tile(swa_q_gain[layer] * scale, N_SWA_HEADS),
            jnp.tile(swa_k_gain[layer], N_SWA_KV),
            jnp.tile(ones, N_SWA_KV)]).reshape(1, C_END).astype(F32)
        nmask = np.zeros((1, C_END), np.float32)
        nmask[:, C_DQ:C_DV] = 1.0
        nmask[:, C_SQ:C_SV] = 1.0
        nmask = jnp.asarray(nmask)
        g1 = norm1_gain[layer].reshape(1, d).astype(F32)

        qkv = _proj(h, g1, w_cat, bd, gain, nmask, TP)
        qkv_meta = _proj(meta_tokens.astype(F32), g1, w_cat, bd, gain, nmask, N_META)
        meta_pad = jnp.pad(qkv_meta, ((0, TQ - N_META), (0, 0)))

        lamv = jnp.pad(jnp.stack([lam_q1[layer], lam_k1[layer], lam_q2[layer], lam_k2[layer]]).astype(F32),
                       ((0, 4), (0, LANES - HEAD_DIM)))
        mixd = _diff_attention(qkv, meta_pad[:, C_DK:C_DV], meta_pad[:, C_DV:C_SQ], dblk, bm0, lamv,
                               diff_subln_gain[layer].reshape(1, LANES).astype(F32), batch, seq, lambda_init)
        mixs = _swa_attention(swa_sinks[layer].astype(F32), qkv, meta_pad[:BLOCK, C_SK:C_SV],
                              meta_pad[:BLOCK, C_SV:C_END], jnp.swapaxes(bt, -1, -2), batch, seq)

        wr = jnp.pad(jnp.concatenate([w_router[layer], w_group[layer]], axis=1),
                     ((0, 0), (0, LANES - lower_pad))).astype(BF16)
        br = jnp.pad(jnp.concatenate([b_router[layer], b_group[layer]]), (0, LANES - lower_pad)).reshape(1, LANES)
        h1, hb, rt, tinfo, cnt = _outproj(h, mixd, mixs, w_out[layer].astype(BF16),
                                          norm2_gain[layer].reshape(1, d).astype(F32), wr, br.astype(F32))

        nt = n // TM
        counts = cnt[0, :N_EXPERTS].astype(jnp.int32)
        nblk_e = (counts + EB - 1) // EB
        blk_end = jnp.cumsum(nblk_e)
        pstart = ((blk_end - nblk_e) * EB).astype(jnp.int32)
        n_blocks = -(-(2 * n + nt * N_EXPERTS * (ROW_ALIGN - 1) + N_EXPERTS * (EB - 1)) // EB)
        blk_ids = jnp.arange(n_blocks)
        blk_e = jnp.minimum(jnp.sum(blk_end[None, :] <= blk_ids[:, None], axis=1), N_EXPERTS - 1).astype(jnp.int32)
        n_act = blk_end[-1:].astype(jnp.int32)
        is_last = jnp.any((blk_end[None, :] == blk_ids[:, None] + 1) & (nblk_e[None, :] > 0), axis=1)
        zero_blocks = ((blk_ids >= n_act[0]) | is_last).astype(jnp.int32)
        ti = tinfo.reshape(nt, 8, LANES)
        run_len = ti[:, 0, :N_EXPERTS].astype(jnp.int32)
        run_start = pstart[None, :] + ti[:, 1, :N_EXPERTS].astype(jnp.int32)
        run_groups = (run_len + ROW_ALIGN - 1) // ROW_ALIGN
        per_chunk = CHUNK // ROW_ALIGN
        runs = jnp.concatenate([run_start, run_groups,
                                jnp.sum(run_groups // per_chunk, axis=1, keepdims=True),
                                jnp.sum(run_groups % per_chunk, axis=1, keepdims=True),
                                jnp.zeros((nt, LANES - 2 * N_EXPERTS - 2), jnp.int32)],
                               axis=1).reshape(nt, 1, LANES)

        xs = _dispatch(zero_blocks, runs, hb, rt, n_blocks * EB)
        own = jnp.where(nblk_e > 0, jnp.arange(N_EXPERTS), N_EXPERTS)
        later = jnp.concatenate([lax.cummin(own[::-1])[::-1][1:], jnp.full((1,), N_EXPERTS)])
        nxt_e = jnp.where(later < N_EXPERTS, later, -1).astype(jnp.int32)
        ys = _experts(blk_e, n_act, nxt_e, xs, w_gate[layer], w_up[layer], w_down[layer])
        h = _combine(runs, ys, h1, rt)
    return h.reshape(batch, seq, d)
```

```python
import functools
import math

import numpy as np
import jax
import jax.numpy as jnp
from jax import lax
from jax.experimental import pallas as pl
from jax.experimental.pallas import tpu as pltpu

F32 = jnp.float32
BF16 = jnp.bfloat16

HEAD_DIM = 64
N_DIFF_HEADS = 4
N_SWA_HEADS = 8
N_SWA_KV = 2
BLOCK = 128
N_META = 16
N_BUCKETS = 32
MAX_DISTANCE = 128
N_GROUPS = 4
EXPERTS_PER_GROUP = 8
N_EXPERTS = N_GROUPS * EXPERTS_PER_GROUP
D_EXPERT = 512
EPS = 1e-6
NEG = -1e30
LOG2E = math.log2(math.e)

LANES = 128
MXU_DIM = 256
VMEM_LIMIT = 48 * 1024 * 1024

TP = 512
TM = 512
TQ = 256
ONES_ROWS = 16
CQ = 512
EB = 512
ROW_ALIGN = 8
CHUNK = 16
SROWS = -(-(2 * TM + N_EXPERTS * (ROW_ALIGN - 1)) // MXU_DIM) * MXU_DIM

C_DQ, C_DK, C_DV, C_SQ, C_SK, C_SV, C_END = 0, 512, 1024, 1536, 2048, 2176, 2304
NORM_GROUPS = (0, 1, 2, 3, 6, 7, 8)


def _cparams(sem):
    return pltpu.CompilerParams(dimension_semantics=sem, vmem_limit_bytes=VMEM_LIMIT)


def _t5_bucket_np(dist):
    n = np.maximum(dist, 0)
    max_exact = N_BUCKETS // 2
    nf = np.maximum(n, 1).astype(np.float32)
    large = max_exact + (np.log(nf / np.float32(max_exact)) / np.float32(math.log(MAX_DISTANCE / max_exact))
                         * np.float32(N_BUCKETS - max_exact)).astype(np.int32)
    large = np.minimum(large, N_BUCKETS - 1)
    return np.where(n < max_exact, n, large)


def _bias_tables(rel_bias, tq):
    nd = 2 * BLOCK
    buckets = _t5_bucket_np(np.arange(nd))
    assert (buckets[MAX_DISTANCE:] == N_BUCKETS - 1).all()
    rb = rel_bias.astype(F32)
    r = np.arange(BLOCK)[:, None]
    c = np.arange(BLOCK)[None, :]
    d_own = r - c
    d_prev = BLOCK + r - c
    far = rb[N_BUCKETS - 1]

    def take(dist):
        idx = jnp.asarray(buckets[np.clip(dist, 0, nd - 1)], jnp.int32)[None]
        out = jnp.zeros((rb.shape[1],) + dist.shape, F32)
        for b in range(N_BUCKETS):
            out = jnp.where(idx == b, rb[b].reshape((-1,) + (1,) * dist.ndim), out)
        return out

    hd = slice(0, N_DIFF_HEADS)
    far_d = far[hd][:, None, None]
    d0 = jnp.where(d_own[None] >= 0, take(d_own)[hd] - far_d, NEG)
    d1 = take(d_prev)[hd] - far_d
    dblk = jnp.stack([d0, d1], axis=1)
    rq = np.arange(tq)[:, None]
    cm = np.arange(LANES)[None, :]
    d_meta = N_META + rq - cm
    bm0 = jnp.where((cm < N_META)[None], take(d_meta)[hd] - far_d, NEG)

    hs = slice(N_DIFF_HEADS, N_DIFF_HEADS + N_SWA_HEADS)
    far_s = far[hs][:, None, None]
    d_meta_s = N_META + r - cm
    meta_first = jnp.where((cm < N_META)[None], take(d_meta_s)[hs], NEG)
    meta_rest = jnp.where((cm < N_META)[None], jnp.broadcast_to(far_s, (N_SWA_HEADS, BLOCK, LANES)), NEG)
    prev_rest = jnp.where((c > r)[None], take(d_prev)[hs], NEG)
    prev_first = jnp.full((N_SWA_HEADS, BLOCK, BLOCK), NEG, F32)
    own = jnp.where((d_own >= 0)[None], take(d_own)[hs], NEG)
    bt = jnp.stack([jnp.concatenate([meta_first, prev_first, own], axis=-1),
                    jnp.concatenate([meta_rest, prev_rest, own], axis=-1)], axis=0)
    return dblk.astype(F32), bm0.astype(F32), bt.astype(F32)


def _proj_kernel(x_ref, g1_ref, w_ref, bd_ref, gain_ref, nmask_ref, o_ref):
    x = x_ref[...]
    a = x * lax.rsqrt(jnp.mean(x * x, axis=-1, keepdims=True) + EPS) * g1_ref[...]
    p = jnp.dot(a.astype(BF16), w_ref[...], preferred_element_type=F32)
    bd = bd_ref[...]
    for j in range(C_END // MXU_DIM):
        sl = slice(j * MXU_DIM, (j + 1) * MXU_DIM)
        pj = p[:, sl]
        if j in NORM_GROUPS:
            ms = jnp.dot((pj * pj).astype(BF16), bd, preferred_element_type=F32)
            pj = jnp.where(nmask_ref[:, sl] != 0.0, pj * lax.rsqrt(ms + EPS) * gain_ref[:, sl], pj)
        o_ref[:, sl] = pj.astype(BF16)


def _proj(x2, g1, w, bd, gain, nmask, tm):
    n = x2.shape[0]
    return pl.pallas_call(
        _proj_kernel,
        out_shape=jax.ShapeDtypeStruct((n, C_END), BF16),
        grid=(n // tm,),
        in_specs=[
            pl.BlockSpec((tm, x2.shape[1]), lambda i: (i, 0)),
            pl.BlockSpec(g1.shape, lambda i: (0, 0)),
            pl.BlockSpec(w.shape, lambda i: (0, 0)),
            pl.BlockSpec(bd.shape, lambda i: (0, 0)),
            pl.BlockSpec(gain.shape, lambda i: (0, 0)),
            pl.BlockSpec(nmask.shape, lambda i: (0, 0)),
        ],
        out_specs=pl.BlockSpec((tm, C_END), lambda i: (i, 0)),
        compiler_params=_cparams(("parallel",)),
        name="proj",
    )(x2, g1, w, bd, gain, nmask)


def _diff_kernel(qi_tab, t_tab, q_ref, k_ref, v_ref, km_ref, vm_ref, d_ref, bm0_ref, lamv_ref, gain_ref, o_ref,
                 bias_ref, qs_ref, vt_ref, s_buf, p_buf, a_buf, m_ref, acc_ref, *, lambda_init, n_steps):
    tq = TQ
    nq = q_ref.shape[0] // tq
    nb = tq // BLOCK
    BIAS_NONE, BIAS_LEFT, BIAS_DIAG, BIAS_META, BIAS_META0 = 0, 1, 2, 3, 4

    d0 = d_ref[0, 0] * LOG2E
    d1 = d_ref[0, 1] * LOG2E
    zeros = jnp.zeros((BLOCK, BLOCK), F32)
    bias_ref[BIAS_NONE] = jnp.zeros((tq, tq), F32)
    for a in range(nb):
        for b in range(nb):
            rs, cs = slice(a * BLOCK, (a + 1) * BLOCK), slice(b * BLOCK, (b + 1) * BLOCK)
            if a == b:
                blk = d0
            elif b == a + 1:
                blk = d1
            elif b > a:
                blk = zeros
            else:
                blk = jnp.full((BLOCK, BLOCK), NEG, F32)
            bias_ref[BIAS_DIAG, rs, cs] = blk
            bias_ref[BIAS_LEFT, rs, cs] = d1 if (b == 0 and a == nb - 1) else zeros
    row_m = lax.broadcasted_iota(jnp.int32, (tq, tq), 0)
    bias_ref[BIAS_META] = jnp.where(row_m < N_META, 0.0, NEG).astype(F32)
    bias_ref[BIAS_META0, :LANES, :] = bm0_ref[0] * LOG2E
    bias_ref[BIAS_META0, LANES:, :] = jnp.full((tq - LANES, tq), NEG, F32)

    lane = lax.broadcasted_iota(jnp.int32, (tq, LANES), 1)
    for i in range(nq):
        rows = slice(i * tq, (i + 1) * tq)
        q = q_ref[rows, :].astype(F32)
        qs_ref[i] = jnp.transpose(jnp.concatenate([jnp.where(lane < HEAD_DIM, q, 0.0),
                                                   jnp.where(lane >= HEAD_DIM, q, 0.0)], axis=0)).astype(BF16)
        vt_ref[i, :LANES, :] = jnp.transpose(v_ref[rows, :].astype(F32)).astype(BF16)
    vt_ref[nq, :LANES, :] = jnp.transpose(vm_ref[...].astype(F32)).astype(BF16)
    vt_ref[:, LANES:, :] = jnp.ones((nq + 1, ONES_ROWS, tq), BF16)
    acc_ref[...] = jnp.zeros(acc_ref.shape, F32)
    m_ref[...] = jnp.full(m_ref.shape, NEG, F32)
    lv = lamv_ref[...]
    lam = (jnp.exp(jnp.sum(lv[0:1] * lv[1:2], axis=-1, keepdims=True))
           - jnp.exp(jnp.sum(lv[2:3] * lv[3:4], axis=-1, keepdims=True)) + lambda_init)

    def seq_row(t):
        return pl.multiple_of(jnp.maximum(t - 1, 0) * tq, tq)

    def stage_a(n, slot):
        qi, t = qi_tab[n], t_tab[n]
        kt = jnp.where(t == 0, km_ref[...], k_ref[pl.ds(seq_row(t), tq), :])
        s = jnp.dot(kt, qs_ref[qi], preferred_element_type=F32)
        which = jnp.where(t == 0, jnp.where(qi == 0, BIAS_META0, BIAS_META),
                          jnp.where(t == qi + 1, BIAS_DIAG, jnp.where(t == qi, BIAS_LEFT, BIAS_NONE)))
        s_buf[slot] = s + jnp.tile(bias_ref[which], (1, 2))

    def stage_b(n, slot):
        s = s_buf[slot]
        m_prev = jnp.where(t_tab[n] == 0, NEG, m_ref[...])
        m_new = jnp.maximum(m_prev, jnp.max(s, axis=0, keepdims=True))
        a_buf[slot] = jnp.exp2(m_prev - m_new)
        p_buf[slot] = jnp.exp2(s - m_new[0:1]).astype(BF16)
        m_ref[...] = m_new

    def stage_c(n, slot):
        qi, t = qi_tab[n], t_tab[n]
        vt = vt_ref[jnp.where(t == 0, nq, t - 1)]
        pv = jnp.dot(vt, p_buf[slot], preferred_element_type=F32)
        acc_ref[qi] = a_buf[slot][0:1] * acc_ref[qi] + pv

    stage_a(0, 0)
    stage_a(1, 1)
    stage_b(0, 0)

    def steps(n, count):
        for j in range(count):
            stage_a(n + j + 2, j % 2)
            stage_b(n + j + 1, (j + 1) % 2)
            stage_c(n + j, j % 2)

    unroll = 8
    n_blocks = (n_steps - 2) // unroll

    def block(k, carry):
        steps(unroll * k, unroll)
        return carry
    lax.fori_loop(0, n_blocks, block, 0)
    steps(unroll * n_blocks, n_steps - 2 - unroll * n_blocks)

    stage_b(n_steps - 1, 1)
    stage_c(n_steps - 2, 0)
    stage_c(n_steps - 1, 1)

    for i in range(nq):
        acc = acc_ref[i]
        o = acc[:LANES] * (1.0 / acc[LANES:LANES + 1])
        d = o[:, :tq] - lam * o[:, tq:]
        y = d * lax.rsqrt(jnp.mean(d * d, axis=0, keepdims=True) + EPS) * jnp.tile(gain_ref[...], (1, tq // LANES))
        o_ref[i * tq:(i + 1) * tq, :] = jnp.transpose(y * (1.0 - lambda_init)).astype(BF16)


def _diff_attention(qkv, km, vm, dblk, bm0, lamv, gain, batch, seq, lambda_init):
    nq = seq // TQ
    steps = [(qi, t) for qi in range(nq) for t in range(qi + 2)]
    assert len(steps) % 2 == 0
    qi_tab = jnp.asarray([s[0] for s in steps], jnp.int32)
    t_tab = jnp.asarray([s[1] for s in steps], jnp.int32)
    kern = functools.partial(_diff_kernel, lambda_init=lambda_init, n_steps=len(steps))
    return pl.pallas_call(
        kern,
        out_shape=jax.ShapeDtypeStruct((batch * seq, N_DIFF_HEADS * LANES), BF16),
        grid_spec=pltpu.PrefetchScalarGridSpec(
            num_scalar_prefetch=2,
            grid=(batch, N_DIFF_HEADS),
            in_specs=[
                pl.BlockSpec((seq, LANES), lambda b, h, *_: (b, C_DQ // LANES + h)),
                pl.BlockSpec((seq, LANES), lambda b, h, *_: (b, C_DK // LANES + h)),
                pl.BlockSpec((seq, LANES), lambda b, h, *_: (b, C_DV // LANES + h)),
                pl.BlockSpec((TQ, LANES), lambda b, h, *_: (0, h)),
                pl.BlockSpec((TQ, LANES), lambda b, h, *_: (0, h)),
                pl.BlockSpec((1, 2, BLOCK, BLOCK), lambda b, h, *_: (h, 0, 0, 0)),
                pl.BlockSpec((1, LANES, TQ), lambda b, h, *_: (h, 0, 0)),
                pl.BlockSpec(lamv.shape, lambda b, h, *_: (0, 0)),
                pl.BlockSpec((LANES, LANES), lambda b, h, *_: (0, 0)),
            ],
            out_specs=pl.BlockSpec((seq, LANES), lambda b, h, *_: (b, h)),
            scratch_shapes=[
                pltpu.VMEM((5, TQ, TQ), F32),
                pltpu.VMEM((nq, LANES, 2 * TQ), BF16),
                pltpu.VMEM((nq + 1, LANES + ONES_ROWS, TQ), BF16),
                pltpu.VMEM((2, TQ, 2 * TQ), F32),
                pltpu.VMEM((2, TQ, 2 * TQ), BF16),
                pltpu.VMEM((2, 8, 2 * TQ), F32),
                pltpu.VMEM((8, 2 * TQ), F32),
                pltpu.VMEM((nq, LANES + ONES_ROWS, 2 * TQ), F32),
            ],
        ),
        compiler_params=_cparams(("parallel", "parallel")),
        name="diff_attention",
    )(qi_tab, t_tab, qkv, qkv, qkv, km, vm, jnp.swapaxes(dblk, -1, -2), jnp.swapaxes(bm0, -1, -2), lamv,
      jnp.broadcast_to(gain.reshape(LANES, 1), (LANES, LANES)))


def _swa_kernel(sink_ref, q_ref, k_ref, v_ref, km_ref, vm_ref, bt_ref, o_ref, kd_ref, vt_ref,
                s_scr, p_scr, inv_scr):
    ci = pl.program_id(1)
    nblk = q_ref.shape[0] // BLOCK
    nkb = k_ref.shape[0] // BLOCK
    lane = lax.broadcasted_iota(jnp.int32, (BLOCK, LANES), 1)

    def both_halves(k):
        k0, k1 = k[:, :HEAD_DIM], k[:, HEAD_DIM:]
        return jnp.concatenate([k0, k0, k1, k1], axis=1)

    @pl.when(ci == 0)
    def _per_batch():
        def body(j, carry):
            rows = pl.ds(pl.multiple_of(j * BLOCK, BLOCK), BLOCK)
            kd_ref[j] = both_halves(k_ref[rows, :])
            vt_ref[j] = jnp.transpose(v_ref[rows, :].astype(F32)).astype(BF16)
            return carry
        lax.fori_loop(0, nkb, body, 0)
        kd_ref[nkb] = both_halves(km_ref[...])
        vt_ref[nkb] = jnp.transpose(vm_ref[...].astype(F32)).astype(BF16)

    def block_body(n, carry):
        gblk = ci * nblk + n
        first = jnp.where(gblk == 0, 0, 1)
        prev = jnp.maximum(gblk - 1, 0)
        r_q = pl.multiple_of(n * BLOCK, BLOCK)
        pairs = [(g, u) for g in range(N_SWA_KV) for u in range(2)]
        for c, (g, u) in enumerate(pairs):
            ks = slice(g * LANES, (g + 1) * LANES)
            kcat = jnp.concatenate([kd_ref[nkb, :, ks], kd_ref[prev, :, ks], kd_ref[gblk, :, ks]], axis=0)
            h0 = 4 * g + 2 * u
            qp = q_ref[pl.ds(r_q, BLOCK), (2 * g + u) * LANES:(2 * g + u + 1) * LANES].astype(F32)
            qs = jnp.transpose(jnp.concatenate([jnp.where(lane < HEAD_DIM, qp, 0.0),
                                                jnp.where(lane >= HEAD_DIM, qp, 0.0)], axis=0)).astype(BF16)
            s = jnp.dot(kcat, qs, preferred_element_type=F32)
            s_scr[c] = s + jnp.concatenate([bt_ref[first, h0], bt_ref[first, h0 + 1]], axis=1)
        for c, (g, u) in enumerate(pairs):
            h0 = 4 * g + 2 * u
            s = s_scr[c]
            sink = jnp.concatenate([sink_ref[h0:h0 + 1, :], sink_ref[h0 + 1:h0 + 2, :]], axis=1)
            m = jnp.maximum(jnp.max(s, axis=0, keepdims=True), sink)
            p = jnp.exp(s - m)
            p_scr[c] = p.astype(BF16)
            inv_scr[c] = jnp.broadcast_to(1.0 / (jnp.sum(p, axis=0, keepdims=True) + jnp.exp(sink - m)),
                                          inv_scr.shape[1:])
        for c, (g, u) in enumerate(pairs):
            vs = slice(g * HEAD_DIM, (g + 1) * HEAD_DIM)
            vcat = jnp.concatenate([vt_ref[nkb, vs, :], vt_ref[prev, vs, :], vt_ref[gblk, vs, :]], axis=1)
            o = jnp.dot(vcat, p_scr[c], preferred_element_type=F32) * inv_scr[c][0:1]
            ot = jnp.transpose(o)
            o_ref[pl.ds(r_q, BLOCK), (2 * g + u) * LANES:(2 * g + u + 1) * LANES] = (
                jnp.concatenate([ot[:BLOCK], ot[BLOCK:]], axis=1).astype(BF16))
        return carry

    lax.fori_loop(0, nblk, block_body, 0)


def _swa_attention(sinks, qkv, km, vm, bt, batch, seq):
    nc = seq // CQ
    nkb = seq // BLOCK
    sinkv = jnp.broadcast_to(sinks.reshape(N_SWA_HEADS, 1), (N_SWA_HEADS, LANES))
    return pl.pallas_call(
        _swa_kernel,
        out_shape=jax.ShapeDtypeStruct((batch * seq, N_SWA_HEADS * HEAD_DIM), BF16),
        grid=(batch, nc),
        in_specs=[
            pl.BlockSpec(sinkv.shape, lambda b, c: (0, 0)),
            pl.BlockSpec((CQ, 512), lambda b, c: (b * nc + c, C_SQ // 512)),
            pl.BlockSpec((seq, LANES), lambda b, c: (b, C_SK // LANES)),
            pl.BlockSpec((seq, LANES), lambda b, c: (b, C_SV // LANES)),
            pl.BlockSpec(km.shape, lambda b, c: (0, 0)),
            pl.BlockSpec(vm.shape, lambda b, c: (0, 0)),
            pl.BlockSpec(bt.shape, lambda b, c: (0, 0, 0, 0)),
        ],
        out_specs=pl.BlockSpec((CQ, 512), lambda b, c: (b * nc + c, 0)),
        scratch_shapes=[pltpu.VMEM((nkb + 1, BLOCK, 2 * LANES), BF16),
                        pltpu.VMEM((nkb + 1, LANES, BLOCK), BF16),
                        pltpu.VMEM((4, 3 * BLOCK, 2 * BLOCK), F32),
                        pltpu.VMEM((4, 3 * BLOCK, 2 * BLOCK), BF16),
                        pltpu.VMEM((4, 8, 2 * BLOCK), F32)],
        compiler_params=_cparams(("parallel", "arbitrary")),
        name="swa_attention",
    )(sinkv, qkv, qkv, qkv, km, vm, bt)


def _outproj_kernel(x_ref, md_ref, ms_ref, wo_ref, g2_ref, wr_ref, br_ref,
                    h_ref, hb_ref, rt_ref, ti_ref, cnt_ref, c_ref, lg_ref):
    i = pl.program_id(0)

    @pl.when(i == 0)
    def _init():
        c_ref[...] = jnp.zeros(c_ref.shape, F32)
        lg_ref[...] = jnp.zeros(lg_ref.shape, F32)

    lg_prev = lg_ref[...]
    half = md_ref.shape[1]
    h = (x_ref[...]
         + jnp.dot(md_ref[...], wo_ref[:half, :], preferred_element_type=F32)
         + jnp.dot(ms_ref[...], wo_ref[half:, :], preferred_element_type=F32))
    h_ref[...] = h
    hn = h * lax.rsqrt(jnp.mean(h * h, axis=-1, keepdims=True) + EPS) * g2_ref[...]
    hb = hn.astype(BF16)
    hb_ref[...] = hb
    lg_ref[...] = jnp.dot(hb, wr_ref[...], preferred_element_type=F32) + br_ref[...]
    _route_tile(lg_prev, jnp.where(i > 0, 1.0, 0.0), rt_ref, ti_ref, c_ref)

    @pl.when(i == pl.num_programs(0) - 1)
    def _fin():
        cnt_ref[...] = c_ref[...]


def _route_tile(lg, live, rt_ref, ti_ref, c_ref):
    tm = lg.shape[0]
    lane_i = lax.broadcasted_iota(jnp.int32, lg.shape, 1)
    lane = lane_i.astype(F32)
    big = float(4 * LANES)
    is_g = (lane_i >= N_EXPERTS) & (lane_i < N_EXPERTS + N_GROUPS)
    glm = jnp.where(is_g, lg, -jnp.inf)
    gmax = jnp.max(glm, axis=1, keepdims=True)
    gidx = jnp.min(jnp.where(glm == gmax, lane, big), axis=1, keepdims=True) - N_EXPERTS
    gsum = jnp.sum(jnp.where(is_g, jnp.exp(lg - gmax), 0.0), axis=1, keepdims=True)
    g_w = 1.0 / gsum
    lane_grp = (lane_i >> 3).astype(F32)
    in_grp = (lane_i < N_EXPERTS) & (lane_grp == gidx)
    el = jnp.where(in_grp, lg, -jnp.inf)
    t1 = jnp.max(el, axis=1, keepdims=True)
    j1 = jnp.min(jnp.where(el == t1, lane, big), axis=1, keepdims=True)
    el2 = jnp.where(lane == j1, -jnp.inf, el)
    t2 = jnp.max(el2, axis=1, keepdims=True)
    j2 = jnp.min(jnp.where(el2 == t2, lane, big), axis=1, keepdims=True)
    e2 = jnp.exp(t2 - t1)
    den = 1.0 + e2
    gate1 = g_w / den
    gate2 = g_w * e2 / den

    o1 = lane == j1
    o2 = lane == j2
    onehot = jnp.where(o1 | o2, 1.0, 0.0).astype(BF16)
    rr = lax.broadcasted_iota(jnp.int32, (tm, tm), 0)
    cc = lax.broadcasted_iota(jnp.int32, (tm, tm), 1)
    lower = jnp.where(rr > cc, 1.0, 0.0).astype(BF16)
    pfx = jnp.dot(lower, onehot, preferred_element_type=F32)
    cnt_tile = jnp.sum(onehot.astype(F32), axis=0, keepdims=True)
    groups = jnp.floor((cnt_tile + (ROW_ALIGN - 1)) * (1.0 / ROW_ALIGN))
    er = lax.broadcasted_iota(jnp.int32, (LANES, LANES), 0)
    ec = lax.broadcasted_iota(jnp.int32, (LANES, LANES), 1)
    before = jnp.where(er < ec, 1.0, 0.0).astype(BF16)
    cbase = ROW_ALIGN * jnp.dot(jnp.broadcast_to(groups, (8, LANES)).astype(BF16), before,
                                preferred_element_type=F32)[0:1]
    at = pfx + cbase
    pos1 = jnp.sum(jnp.where(o1, at, 0.0), axis=1, keepdims=True)
    pos2 = jnp.sum(jnp.where(o2, at, 0.0), axis=1, keepdims=True)
    rt_ref[...] = jnp.where(lane_i == 0, gate1,
                            jnp.where(lane_i == 1, gate2,
                                      jnp.where(lane_i == 2, pos1,
                                                jnp.where(lane_i == 3, pos2, 0.0))))
    c_old = c_ref[...]
    c_ref[...] = c_old + groups * (ROW_ALIGN * live)
    row8 = lax.broadcasted_iota(jnp.int32, (8, LANES), 0)
    ti_ref[...] = jnp.where(row8 == 0, cnt_tile, jnp.where(row8 == 1, c_old, 0.0))


def _outproj(x2, mixd, mixs, wo, g2, wr, br):
    n, d = x2.shape
    nt = n // TM

    def proj_tile(i):
        return (jnp.minimum(i, nt - 1), 0)

    def route_tile(i):
        return (jnp.maximum(i - 1, 0), 0)

    return pl.pallas_call(
        _outproj_kernel,
        out_shape=(jax.ShapeDtypeStruct((n, d), F32),
                   jax.ShapeDtypeStruct((n, d), BF16),
                   jax.ShapeDtypeStruct((n, LANES), F32),
                   jax.ShapeDtypeStruct((nt * 8, LANES), F32),
                   jax.ShapeDtypeStruct((8, LANES), F32)),
        grid=(nt + 1,),
        in_specs=[
            pl.BlockSpec((TM, d), proj_tile),
            pl.BlockSpec((TM, mixd.shape[1]), proj_tile),
            pl.BlockSpec((TM, mixs.shape[1]), proj_tile),
            pl.BlockSpec(wo.shape, lambda i: (0, 0)),
            pl.BlockSpec(g2.shape, lambda i: (0, 0)),
            pl.BlockSpec(wr.shape, lambda i: (0, 0)),
            pl.BlockSpec(br.shape, lambda i: (0, 0)),
        ],
        out_specs=(pl.BlockSpec((TM, d), proj_tile),
                   pl.BlockSpec((TM, d), proj_tile),
                   pl.BlockSpec((TM, LANES), route_tile),
                   pl.BlockSpec((8, LANES), route_tile),
                   pl.BlockSpec((8, LANES), lambda i: (0, 0))),
        scratch_shapes=[pltpu.VMEM((8, LANES), F32), pltpu.VMEM((TM, LANES), F32)],
        compiler_params=_cparams(("arbitrary",)),
        name="outproj_router",
    )(x2, mixd, mixs, wo, g2, wr, br)


def _for_each_chunk(runs_ref, fn):
    def per_expert(e, sorted_row, priority):
        start = runs_ref[0, 0, e]
        groups = runs_ref[0, 0, N_EXPERTS + e]
        whole = groups // (CHUNK // ROW_ALIGN)

        def per_chunk(c, carry):
            fn(pl.multiple_of(start + c * CHUNK, ROW_ALIGN), pl.multiple_of(sorted_row + c * CHUNK, ROW_ALIGN),
               CHUNK, priority)
            return carry
        lax.fori_loop(0, whole, per_chunk, 0)

        @pl.when(groups % (CHUNK // ROW_ALIGN) == 1)
        def _():
            fn(pl.multiple_of(start + whole * CHUNK, ROW_ALIGN),
               pl.multiple_of(sorted_row + whole * CHUNK, ROW_ALIGN), ROW_ALIGN, priority)
        return sorted_row + groups * ROW_ALIGN

    def expert_pair(e2, sorted_row):
        return per_expert(2 * e2 + 1, per_expert(2 * e2, sorted_row, 0), 1)
    lax.fori_loop(0, N_EXPERTS // 2, expert_pair, 0)


def _wait_chunks(runs_ref, make_copy):
    for k, rows in enumerate((CHUNK, ROW_ALIGN)):
        def body(c, carry, rows=rows):
            make_copy(rows).wait()
            return carry
        lax.fori_loop(0, runs_ref[0, 0, 2 * N_EXPERTS + k], body, 0)


def _dispatch_kernel(zf_ref, cur_ref, prv_ref, hb_ref, rt_ref, xs_ref, sbuf, zbuf, sem, zsem):
    i = pl.program_id(0)
    nt = pl.num_programs(0)
    slot = i % 2
    tm, d = hb_ref.shape

    def for_zero_blocks(kind, fn):
        def body(b, carry):
            @pl.when(zf_ref[b] == kind)
            def _():
                fn(pltpu.make_async_copy(zbuf, xs_ref.at[pl.ds(pl.multiple_of(b * EB, EB), EB)],
                                         zsem.at[kind - 1]))
            return carry
        lax.fori_loop(0, zf_ref.shape[0], body, 0)

    @pl.when(i == 0)
    def _():
        zbuf[...] = jnp.zeros(zbuf.shape, zbuf.dtype)
        for_zero_blocks(1, lambda c: c.start())
        for_zero_blocks(2, lambda c: c.start())
        for_zero_blocks(1, lambda c: c.wait())

    pos_t = jnp.transpose(rt_ref[...])
    srow = lax.broadcasted_iota(jnp.int32, (SROWS, tm), 0).astype(F32)
    sel = jnp.where(srow == pos_t[2:3, :], 1.0, jnp.where(srow == pos_t[3:4, :], 1.0, 0.0)).astype(BF16)
    srt = jnp.dot(sel, hb_ref[...], preferred_element_type=F32)
    bits = pltpu.bitcast(srt, jnp.uint32)
    sbuf[slot] = (bits[:, d // 2:] & jnp.uint32(0xFFFF0000)) | (bits[:, :d // 2] >> 16)

    def chunk_copy(run_row, sorted_row, rows, sl):
        return pltpu.make_async_copy(sbuf.at[sl, pl.ds(sorted_row, rows)], xs_ref.at[pl.ds(run_row, rows)],
                                     sem.at[sl])

    _for_each_chunk(cur_ref, lambda run_row, sorted_row, rows, priority:
                    chunk_copy(run_row, sorted_row, rows, slot).start(priority=priority))

    @pl.when(i > 0)
    def _():
        _wait_chunks(prv_ref, lambda rows: chunk_copy(0, 0, rows, 1 - slot))

    @pl.when(i == nt - 1)
    def _():
        _wait_chunks(cur_ref, lambda rows: chunk_copy(0, 0, rows, slot))
        for_zero_blocks(2, lambda c: c.wait())


def _dispatch(zero_blocks, runs, hb, rt, n_rows):
    n, d = hb.shape
    return pl.pallas_call(
        _dispatch_kernel,
        out_shape=jax.ShapeDtypeStruct((n_rows, d // 2), jnp.uint32),
        grid_spec=pltpu.PrefetchScalarGridSpec(
            num_scalar_prefetch=1,
            grid=(n // TM,),
            in_specs=[
                pl.BlockSpec((1, 1, LANES), lambda i, zf: (i, 0, 0), memory_space=pltpu.SMEM),
                pl.BlockSpec((1, 1, LANES), lambda i, zf: (jnp.maximum(i - 1, 0), 0, 0), memory_space=pltpu.SMEM),
                pl.BlockSpec((TM, d), lambda i, zf: (i, 0)),
                pl.BlockSpec((TM, LANES), lambda i, zf: (i, 0)),
            ],
            out_specs=pl.BlockSpec(memory_space=pl.ANY),
            scratch_shapes=[pltpu.VMEM((2, SROWS, d // 2), jnp.uint32), pltpu.VMEM((EB, d // 2), jnp.uint32),
                            pltpu.SemaphoreType.DMA((2,)), pltpu.SemaphoreType.DMA((2,))],
        ),
        compiler_params=_cparams(("arbitrary",)),
        name="dispatch",
    )(zero_blocks, runs, runs, hb, rt)


def _experts_kernel(be_ref, na_ref, nxt_ref, xs_ref, wg_hbm, wu_hbm, wd_hbm, ys_ref,
                    wgf, wuf, wdf, wgb, wub, wdb, sem):
    b = pl.program_id(0)

    def weight_copies(e):
        return (pltpu.make_async_copy(wg_hbm.at[e], wgf, sem.at[0]),
                pltpu.make_async_copy(wu_hbm.at[e], wuf, sem.at[1]),
                pltpu.make_async_copy(wd_hbm.at[e], wdf, sem.at[2]))

    @pl.when(b == 0)
    def _():
        for c in weight_copies(be_ref[0]):
            c.start()

    @pl.when(b < na_ref[0])
    def _():
        e = be_ref[b]
        changed = jnp.logical_or(b == 0, be_ref[jnp.maximum(b - 1, 0)] != e)

        @pl.when(changed)
        def _load():
            for c in weight_copies(e):
                c.wait()
            wgb[...] = wgf[...].astype(BF16)
            wub[...] = wuf[...].astype(BF16)
            wdb[...] = wdf[...].astype(BF16)
            nxt = nxt_ref[e]

            @pl.when(nxt >= 0)
            def _():
                for c in weight_copies(nxt):
                    c.start()

        w = xs_ref[...]
        x_lo = pltpu.bitcast(w << 16, F32).astype(BF16)
        x_hi = pltpu.bitcast(w & jnp.uint32(0xFFFF0000), F32).astype(BF16)
        dh = w.shape[1]
        g = (jnp.dot(x_lo, wgb[:dh, :], preferred_element_type=F32)
             + jnp.dot(x_hi, wgb[dh:, :], preferred_element_type=F32))
        u = (jnp.dot(x_lo, wub[:dh, :], preferred_element_type=F32)
             + jnp.dot(x_hi, wub[dh:, :], preferred_element_type=F32))
        hdn = g * (1.0 / (1.0 + jnp.exp(-g))) * u
        y = jnp.dot(hdn.astype(BF16), wdb[...], preferred_element_type=F32)
        bits = pltpu.bitcast(y.astype(BF16).astype(F32), jnp.uint32)
        ys_ref[...] = (bits[:, dh:] & jnp.uint32(0xFFFF0000)) | (bits[:, :dh] >> 16)

    @pl.when(b >= na_ref[0])
    def _():
        ys_ref[...] = jnp.zeros(ys_ref.shape, ys_ref.dtype)


def _experts(blk_e, n_act, nxt_e, xs, w_gate, w_up, w_down):
    p, dh = xs.shape
    d = 2 * dh
    de = w_gate.shape[2]

    def row_map(b, be, na, nx):
        return (jnp.minimum(b, na[0] - 1), 0)

    return pl.pallas_call(
        _experts_kernel,
        out_shape=jax.ShapeDtypeStruct((p, dh), jnp.uint32),
        grid_spec=pltpu.PrefetchScalarGridSpec(
            num_scalar_prefetch=3,
            grid=(p // EB,),
            in_specs=[
                pl.BlockSpec((EB, dh), row_map),
                pl.BlockSpec(memory_space=pl.ANY),
                pl.BlockSpec(memory_space=pl.ANY),
                pl.BlockSpec(memory_space=pl.ANY),
            ],
            out_specs=pl.BlockSpec((EB, dh), lambda b, be, na, nx: (b, 0)),
            scratch_shapes=[pltpu.VMEM((d, de), F32), pltpu.VMEM((d, de), F32), pltpu.VMEM((de, d), F32),
                            pltpu.VMEM((d, de), BF16), pltpu.VMEM((d, de), BF16), pltpu.VMEM((de, d), BF16),
                            pltpu.SemaphoreType.DMA((3,))],
        ),
        compiler_params=_cparams(("arbitrary",)),
        name="experts",
    )(blk_e, n_act, nxt_e, xs, w_gate, w_up, w_down)


def _combine_kernel(cur_ref, nxt_ref, ys_ref, h_ref, rt_ref, o_ref, ybuf, sem):
    i = pl.program_id(0)
    nt = pl.num_programs(0)
    slot = i % 2
    tm = h_ref.shape[0]

    def chunk_copy(run_row, sorted_row, rows, sl):
        return pltpu.make_async_copy(ys_ref.at[pl.ds(run_row, rows)], ybuf.at[sl, pl.ds(sorted_row, rows)],
                                     sem.at[sl])

    @pl.when(i == 0)
    def _():
        ybuf[...] = jnp.zeros(ybuf.shape, ybuf.dtype)
        _for_each_chunk(cur_ref, lambda run_row, sorted_row, rows, priority:
                        chunk_copy(run_row, sorted_row, rows, 0).start(priority=priority))

    @pl.when(i + 1 < nt)
    def _():
        _for_each_chunk(nxt_ref, lambda run_row, sorted_row, rows, priority:
                        chunk_copy(run_row, sorted_row, rows, 1 - slot).start(priority=priority))

    _wait_chunks(cur_ref, lambda rows: chunk_copy(0, 0, rows, slot))

    rt = rt_ref[...]
    w = ybuf[slot]
    dh = w.shape[1]
    y_lo = pltpu.bitcast(w << 16, F32).astype(BF16)
    y_hi = pltpu.bitcast(w & jnp.uint32(0xFFFF0000), F32).astype(BF16)
    col = lax.broadcasted_iota(jnp.int32, (tm, SROWS), 1).astype(F32)
    wsel = jnp.where(col == rt[:, 2:3], rt[:, 0:1], jnp.where(col == rt[:, 3:4], rt[:, 1:2], 0.0)).astype(BF16)
    for half, yb in ((slice(0, dh), y_lo), (slice(dh, 2 * dh), y_hi)):
        o_ref[:, half] = h_ref[:, half] + jnp.dot(wsel, yb, preferred_element_type=F32)


def _combine(runs, ys, h1, rt):
    n, d = h1.shape
    nt = n // TM
    return pl.pallas_call(
        _combine_kernel,
        out_shape=jax.ShapeDtypeStruct((n, d), F32),
        grid=(nt,),
        in_specs=[
            pl.BlockSpec((1, 1, LANES), lambda i: (i, 0, 0), memory_space=pltpu.SMEM),
            pl.BlockSpec((1, 1, LANES), lambda i: (jnp.minimum(i + 1, nt - 1), 0, 0), memory_space=pltpu.SMEM),
            pl.BlockSpec(memory_space=pl.ANY),
            pl.BlockSpec((TM, d), lambda i: (i, 0)),
            pl.BlockSpec((TM, LANES), lambda i: (i, 0)),
        ],
        out_specs=pl.BlockSpec((TM, d), lambda i: (i, 0)),
        scratch_shapes=[pltpu.VMEM((2, SROWS, d // 2), jnp.uint32), pltpu.SemaphoreType.DMA((2,))],
        compiler_params=_cparams(("arbitrary",)),
        name="combine",
    )(runs, runs, ys, h1, rt)


def kernel(x, meta_tokens, rel_bias, norm1_gain, w_in, diff_q_gain, diff_k_gain, lam_q1, lam_k1, lam_q2, lam_k2, diff_subln_gain, swa_q_gain, swa_k_gain, swa_sinks, w_out, norm2_gain, w_group, b_group, w_router, b_router, w_gate, w_up, w_down):
    batch, seq, d = x.shape
    depth = w_in.shape[0]
    n = batch * seq
    assert seq % CQ == 0 and seq % TQ == 0 and n % TM == 0 and d == 1024
    assert meta_tokens.shape[0] == N_META
    assert depth == 1, "the meta-token rows of the residual stream are not carried across layers"

    h = x.reshape(n, d)
    dblk, bm0, bt = _bias_tables(rel_bias, TQ)
    scale = HEAD_DIM ** -0.5
    bd = jnp.asarray(np.kron(np.eye(MXU_DIM // HEAD_DIM), np.full((HEAD_DIM, HEAD_DIM), 1.0 / HEAD_DIM)), BF16)
    ones = jnp.ones((HEAD_DIM,), F32)
    lower_pad = N_EXPERTS + N_GROUPS

    for layer in range(depth):
        lambda_init = 0.8 - 0.6 * math.exp(-0.3 * layer)
        w_cat = w_in[layer].astype(BF16)
        gain = jnp.concatenate([
            jnp.tile(diff_q_gain[layer] * (scale * LOG2E), 2 * N_DIFF_HEADS),
            jnp.tile(diff_k_gain[layer], 2 * N_DIFF_HEADS),
            jnp.tile(ones, 2 * N_DIFF_HEADS),
            jnp.tile(swa_q_gain[layer] * scale, N_SWA_HEADS),
            jnp.tile(swa_k_gain[layer], N_SWA_KV),
            jnp.tile(ones, N_SWA_KV)]).reshape(1, C_END).astype(F32)
        nmask = np.zeros((1, C_END), np.float32)
        nmask[:, C_DQ:C_DV] = 1.0
        nmask[:, C_SQ:C_SV] = 1.0
        nmask = jnp.asarray(nmask)
        g1 = norm1_gain[layer].reshape(1, d).astype(F32)

        qkv = _proj(h, g1, w_cat, bd, gain, nmask, TP)
        qkv_meta = _proj(meta_tokens.astype(F32), g1, w_cat, bd, gain, nmask, N_META)
        meta_pad = jnp.pad(qkv_meta, ((0, TQ - N_META), (0, 0)))

        lamv = jnp.pad(jnp.stack([lam_q1[layer], lam_k1[layer], lam_q2[layer], lam_k2[layer]]).astype(F32),
                       ((0, 4), (0, LANES - HEAD_DIM)))
        mixd = _diff_attention(qkv, meta_pad[:, C_DK:C_DV], meta_pad[:, C_DV:C_SQ], dblk, bm0, lamv,
                               diff_subln_gain[layer].reshape(1, LANES).astype(F32), batch, seq, lambda_init)
        mixs = _swa_attention(swa_sinks[layer].astype(F32), qkv, meta_pad[:BLOCK, C_SK:C_SV],
                              meta_pad[:BLOCK, C_SV:C_END], jnp.swapaxes(bt, -1, -2), batch, seq)

        wr = jnp.pad(jnp.concatenate([w_router[layer], w_group[layer]], axis=1),
                     ((0, 0), (0, LANES - lower_pad))).astype(BF16)
        br = jnp.pad(jnp.concatenate([b_router[layer], b_group[layer]]), (0, LANES - lower_pad)).reshape(1, LANES)
        h1, hb, rt, tinfo, cnt = _outproj(h, mixd, mixs, w_out[layer].astype(BF16),
                                          norm2_gain[layer].reshape(1, d).astype(F32), wr, br.astype(F32))

        nt = n // TM
        counts = cnt[0, :N_EXPERTS].astype(jnp.int32)
        nblk_e = (counts + EB - 1) // EB
        blk_end = jnp.cumsum(nblk_e)
        pstart = ((blk_end - nblk_e) * EB).astype(jnp.int32)
        n_blocks = -(-(2 * n + nt * N_EXPERTS * (ROW_ALIGN - 1) + N_EXPERTS * (EB - 1)) // EB)
        blk_ids = jnp.arange(n_blocks)
        blk_e = jnp.minimum(jnp.sum(blk_end[None, :] <= blk_ids[:, None], axis=1), N_EXPERTS - 1).astype(jnp.int32)
        n_act = blk_end[-1:].astype(jnp.int32)
        is_last = jnp.any((blk_end[None, :] == blk_ids[:, None] + 1) & (nblk_e[None, :] > 0), axis=1)
        zero_blocks = jnp.where(blk_ids >= n_act[0], 2, jnp.where(is_last, 1, 0)).astype(jnp.int32)
        ti = tinfo.reshape(nt, 8, LANES)
        run_len = ti[:, 0, :N_EXPERTS].astype(jnp.int32)
        run_start = pstart[None, :] + ti[:, 1, :N_EXPERTS].astype(jnp.int32)
        run_groups = (run_len + ROW_ALIGN - 1) // ROW_ALIGN
        per_chunk = CHUNK // ROW_ALIGN
        runs = jnp.concatenate([run_start, run_groups,
                                jnp.sum(run_groups // per_chunk, axis=1, keepdims=True),
                                jnp.sum(run_groups % per_chunk, axis=1, keepdims=True),
                                jnp.zeros((nt, LANES - 2 * N_EXPERTS - 2), jnp.int32)],
                               axis=1).reshape(nt, 1, LANES)

        xs = _dispatch(zero_blocks, runs, hb, rt, n_blocks * EB)
        own = jnp.where(nblk_e > 0, jnp.arange(N_EXPERTS), N_EXPERTS)
        later = jnp.concatenate([lax.cummin(own[::-1])[::-1][1:], jnp.full((1,), N_EXPERTS)])
        nxt_e = jnp.where(later < N_EXPERTS, later, -1).astype(jnp.int32)
        ys = _experts(blk_e, n_act, nxt_e, xs, w_gate[layer], w_up[layer], w_down[layer])
        h = _combine(runs, ys, h1, rt)
    return h.reshape(batch, seq, d)
```

```python
import functools
import math

import numpy as np
import jax
import jax.numpy as jnp
from jax import lax
from jax.experimental import pallas as pl
from jax.experimental.pallas import tpu as pltpu

F32 = jnp.float32
BF16 = jnp.bfloat16

HEAD_DIM = 64
N_DIFF_HEADS = 4
N_SWA_HEADS = 8
N_SWA_KV = 2
BLOCK = 128
N_META = 16
N_BUCKETS = 32
MAX_DISTANCE = 128
N_GROUPS = 4
EXPERTS_PER_GROUP = 8
N_EXPERTS = N_GROUPS * EXPERTS_PER_GROUP
D_EXPERT = 512
EPS = 1e-6
NEG = -1e30
LOG2E = math.log2(math.e)

LANES = 128
MXU_DIM = 256
VMEM_LIMIT = 48 * 1024 * 1024

TP = 512
TM = 512
TQ = 256
ONES_ROWS = 16
CQ = 512
EB = 512
ROW_ALIGN = 8
CHUNK_ROWS = (32, 16, 8)
SROWS = -(-(2 * TM + N_EXPERTS * (ROW_ALIGN - 1)) // MXU_DIM) * MXU_DIM

C_DQ, C_DK, C_DV, C_SQ, C_SK, C_SV, C_END = 0, 512, 1024, 1536, 2048, 2176, 2304
NORM_GROUPS = (0, 1, 2, 3, 6, 7, 8)


def _cparams(sem):
    return pltpu.CompilerParams(dimension_semantics=sem, vmem_limit_bytes=VMEM_LIMIT)


def _t5_bucket_np(dist):
    n = np.maximum(dist, 0)
    max_exact = N_BUCKETS // 2
    nf = np.maximum(n, 1).astype(np.float32)
    large = max_exact + (np.log(nf / np.float32(max_exact)) / np.float32(math.log(MAX_DISTANCE / max_exact))
                         * np.float32(N_BUCKETS - max_exact)).astype(np.int32)
    large = np.minimum(large, N_BUCKETS - 1)
    return np.where(n < max_exact, n, large)


def _bias_tables(rel_bias, tq):
    nd = 2 * BLOCK
    buckets = _t5_bucket_np(np.arange(nd))
    assert (buckets[MAX_DISTANCE:] == N_BUCKETS - 1).all()
    rb = rel_bias.astype(F32)
    r = np.arange(BLOCK)[:, None]
    c = np.arange(BLOCK)[None, :]
    d_own = r - c
    d_prev = BLOCK + r - c
    far = rb[N_BUCKETS - 1]

    def take(dist):
        idx = jnp.asarray(buckets[np.clip(dist, 0, nd - 1)], jnp.int32)[None]
        out = jnp.zeros((rb.shape[1],) + dist.shape, F32)
        for b in range(N_BUCKETS):
            out = jnp.where(idx == b, rb[b].reshape((-1,) + (1,) * dist.ndim), out)
        return out

    hd = slice(0, N_DIFF_HEADS)
    far_d = far[hd][:, None, None]
    d0 = jnp.where(d_own[None] >= 0, take(d_own)[hd] - far_d, NEG)
    d1 = take(d_prev)[hd] - far_d
    dblk = jnp.stack([d0, d1], axis=1)
    rq = np.arange(tq)[:, None]
    cm = np.arange(LANES)[None, :]
    d_meta = N_META + rq - cm
    bm0 = jnp.where((cm < N_META)[None], take(d_meta)[hd] - far_d, NEG)

    hs = slice(N_DIFF_HEADS, N_DIFF_HEADS + N_SWA_HEADS)
    far_s = far[hs][:, None, None]
    d_meta_s = N_META + r - cm
    meta_first = jnp.where((cm < N_META)[None], take(d_meta_s)[hs], NEG)
    meta_rest = jnp.where((cm < N_META)[None], jnp.broadcast_to(far_s, (N_SWA_HEADS, BLOCK, LANES)), NEG)
    prev_rest = jnp.where((c > r)[None], take(d_prev)[hs], NEG)
    prev_first = jnp.full((N_SWA_HEADS, BLOCK, BLOCK), NEG, F32)
    own = jnp.where((d_own >= 0)[None], take(d_own)[hs], NEG)
    bt = jnp.stack([jnp.concatenate([meta_first, prev_first, own], axis=-1),
                    jnp.concatenate([meta_rest, prev_rest, own], axis=-1)], axis=0)
    return dblk.astype(F32), bm0.astype(F32), bt.astype(F32)


def _proj_kernel(x_ref, g1_ref, w_ref, bd_ref, gain_ref, nmask_ref, o_ref):
    x = x_ref[...]
    a = x * lax.rsqrt(jnp.mean(x * x, axis=-1, keepdims=True) + EPS) * g1_ref[...]
    p = jnp.dot(a.astype(BF16), w_ref[...], preferred_element_type=F32)
    bd = bd_ref[...]
    for j in range(C_END // MXU_DIM):
        sl = slice(j * MXU_DIM, (j + 1) * MXU_DIM)
        pj = p[:, sl]
        if j in NORM_GROUPS:
            ms = jnp.dot((pj * pj).astype(BF16), bd, preferred_element_type=F32)
            pj = jnp.where(nmask_ref[:, sl] != 0.0, pj * lax.rsqrt(ms + EPS) * gain_ref[:, sl], pj)
        o_ref[:, sl] = pj.astype(BF16)


def _proj(x2, g1, w, bd, gain, nmask, tm):
    n = x2.shape[0]
    return pl.pallas_call(
        _proj_kernel,
        out_shape=jax.ShapeDtypeStruct((n, C_END), BF16),
        grid=(n // tm,),
        in_specs=[
            pl.BlockSpec((tm, x2.shape[1]), lambda i: (i, 0)),
            pl.BlockSpec(g1.shape, lambda i: (0, 0)),
            pl.BlockSpec(w.shape, lambda i: (0, 0)),
            pl.BlockSpec(bd.shape, lambda i: (0, 0)),
            pl.BlockSpec(gain.shape, lambda i: (0, 0)),
            pl.BlockSpec(nmask.shape, lambda i: (0, 0)),
        ],
        out_specs=pl.BlockSpec((tm, C_END), lambda i: (i, 0)),
        compiler_params=_cparams(("parallel",)),
        name="proj",
    )(x2, g1, w, bd, gain, nmask)


def _diff_kernel(qi_tab, t_tab, q_ref, k_ref, v_ref, km_ref, vm_ref, d_ref, bm0_ref, lamv_ref, gain_ref, o_ref,
                 bias_ref, qs_ref, vt_ref, s_buf, p_buf, a_buf, m_ref, acc_ref, *, lambda_init, n_steps):
    tq = TQ
    nq = q_ref.shape[0] // tq
    nb = tq // BLOCK
    BIAS_NONE, BIAS_LEFT, BIAS_DIAG, BIAS_META, BIAS_META0 = 0, 1, 2, 3, 4

    d0 = d_ref[0, 0] * LOG2E
    d1 = d_ref[0, 1] * LOG2E
    zeros = jnp.zeros((BLOCK, BLOCK), F32)
    bias_ref[BIAS_NONE] = jnp.zeros((tq, tq), F32)
    for a in range(nb):
        for b in range(nb):
            rs, cs = slice(a * BLOCK, (a + 1) * BLOCK), slice(b * BLOCK, (b + 1) * BLOCK)
            if a == b:
                blk = d0
            elif b == a + 1:
                blk = d1
            elif b > a:
                blk = zeros
            else:
                blk = jnp.full((BLOCK, BLOCK), NEG, F32)
            bias_ref[BIAS_DIAG, rs, cs] = blk
            bias_ref[BIAS_LEFT, rs, cs] = d1 if (b == 0 and a == nb - 1) else zeros
    row_m = lax.broadcasted_iota(jnp.int32, (tq, tq), 0)
    bias_ref[BIAS_META] = jnp.where(row_m < N_META, 0.0, NEG).astype(F32)
    bias_ref[BIAS_META0, :LANES, :] = bm0_ref[0] * LOG2E
    bias_ref[BIAS_META0, LANES:, :] = jnp.full((tq - LANES, tq), NEG, F32)

    lane = lax.broadcasted_iota(jnp.int32, (tq, LANES), 1)
    for i in range(nq):
        rows = slice(i * tq, (i + 1) * tq)
        q = q_ref[rows, :].astype(F32)
        qs_ref[i] = jnp.transpose(jnp.concatenate([jnp.where(lane < HEAD_DIM, q, 0.0),
                                                   jnp.where(lane >= HEAD_DIM, q, 0.0)], axis=0)).astype(BF16)
        vt_ref[i, :LANES, :] = jnp.transpose(v_ref[rows, :].astype(F32)).astype(BF16)
    vt_ref[nq, :LANES, :] = jnp.transpose(vm_ref[...].astype(F32)).astype(BF16)
    vt_ref[:, LANES:, :] = jnp.ones((nq + 1, ONES_ROWS, tq), BF16)
    acc_ref[...] = jnp.zeros(acc_ref.shape, F32)
    m_ref[...] = jnp.full(m_ref.shape, NEG, F32)
    lv = lamv_ref[...]
    lam = (jnp.exp(jnp.sum(lv[0:1] * lv[1:2], axis=-1, keepdims=True))
           - jnp.exp(jnp.sum(lv[2:3] * lv[3:4], axis=-1, keepdims=True)) + lambda_init)

    def seq_row(t):
        return pl.multiple_of(jnp.maximum(t - 1, 0) * tq, tq)

    def stage_a(n, slot):
        qi, t = qi_tab[n], t_tab[n]
        kt = jnp.where(t == 0, km_ref[...], k_ref[pl.ds(seq_row(t), tq), :])
        s = jnp.dot(kt, qs_ref[qi], preferred_element_type=F32)
        which = jnp.where(t == 0, jnp.where(qi == 0, BIAS_META0, BIAS_META),
                          jnp.where(t == qi + 1, BIAS_DIAG, jnp.where(t == qi, BIAS_LEFT, BIAS_NONE)))
        s_buf[slot] = s + jnp.tile(bias_ref[which], (1, 2))

    def stage_b(n, slot):
        s = s_buf[slot]
        m_prev = jnp.where(t_tab[n] == 0, NEG, m_ref[...])
        m_new = jnp.maximum(m_prev, jnp.max(s, axis=0, keepdims=True))
        a_buf[slot] = jnp.exp2(m_prev - m_new)
        p_buf[slot] = jnp.exp2(s - m_new[0:1]).astype(BF16)
        m_ref[...] = m_new

    def stage_c(n, slot):
        qi, t = qi_tab[n], t_tab[n]
        vt = vt_ref[jnp.where(t == 0, nq, t - 1)]
        pv = jnp.dot(vt, p_buf[slot], preferred_element_type=F32)
        acc_ref[qi] = a_buf[slot][0:1] * acc_ref[qi] + pv

    stage_a(0, 0)
    stage_a(1, 1)
    stage_b(0, 0)

    def steps(n, count):
        for j in range(count):
            stage_a(n + j + 2, j % 2)
            stage_b(n + j + 1, (j + 1) % 2)
            stage_c(n + j, j % 2)

    unroll = 8
    n_blocks = (n_steps - 2) // unroll

    def block(k, carry):
        steps(unroll * k, unroll)
        return carry
    lax.fori_loop(0, n_blocks, block, 0)
    steps(unroll * n_blocks, n_steps - 2 - unroll * n_blocks)

    stage_b(n_steps - 1, 1)
    stage_c(n_steps - 2, 0)
    stage_c(n_steps - 1, 1)

    for i in range(nq):
        acc = acc_ref[i]
        o = acc[:LANES] * (1.0 / acc[LANES:LANES + 1])
        d = o[:, :tq] - lam * o[:, tq:]
        y = d * lax.rsqrt(jnp.mean(d * d, axis=0, keepdims=True) + EPS) * jnp.tile(gain_ref[...], (1, tq // LANES))
        o_ref[i * tq:(i + 1) * tq, :] = jnp.transpose(y * (1.0 - lambda_init)).astype(BF16)


def _diff_attention(qkv, km, vm, dblk, bm0, lamv, gain, batch, seq, lambda_init):
    nq = seq // TQ
    steps = [(qi, t) for qi in range(nq) for t in range(qi + 2)]
    assert len(steps) % 2 == 0
    qi_tab = jnp.asarray([s[0] for s in steps], jnp.int32)
    t_tab = jnp.asarray([s[1] for s in steps], jnp.int32)
    kern = functools.partial(_diff_kernel, lambda_init=lambda_init, n_steps=len(steps))
    return pl.pallas_call(
        kern,
        out_shape=jax.ShapeDtypeStruct((batch * seq, N_DIFF_HEADS * LANES), BF16),
        grid_spec=pltpu.PrefetchScalarGridSpec(
            num_scalar_prefetch=2,
            grid=(batch, N_DIFF_HEADS),
            in_specs=[
                pl.BlockSpec((seq, LANES), lambda b, h, *_: (b, C_DQ // LANES + h)),
                pl.BlockSpec((seq, LANES), lambda b, h, *_: (b, C_DK // LANES + h)),
                pl.BlockSpec((seq, LANES), lambda b, h, *_: (b, C_DV // LANES + h)),
                pl.BlockSpec((TQ, LANES), lambda b, h, *_: (0, h)),
                pl.BlockSpec((TQ, LANES), lambda b, h, *_: (0, h)),
                pl.BlockSpec((1, 2, BLOCK, BLOCK), lambda b, h, *_: (h, 0, 0, 0)),
                pl.BlockSpec((1, LANES, TQ), lambda b, h, *_: (h, 0, 0)),
                pl.BlockSpec(lamv.shape, lambda b, h, *_: (0, 0)),
                pl.BlockSpec((LANES, LANES), lambda b, h, *_: (0, 0)),
            ],
            out_specs=pl.BlockSpec((seq, LANES), lambda b, h, *_: (b, h)),
            scratch_shapes=[
                pltpu.VMEM((5, TQ, TQ), F32),
                pltpu.VMEM((nq, LANES, 2 * TQ), BF16),
                pltpu.VMEM((nq + 1, LANES + ONES_ROWS, TQ), BF16),
                pltpu.VMEM((2, TQ, 2 * TQ), F32),
                pltpu.VMEM((2, TQ, 2 * TQ), BF16),
                pltpu.VMEM((2, 8, 2 * TQ), F32),
                pltpu.VMEM((8, 2 * TQ), F32),
                pltpu.VMEM((nq, LANES + ONES_ROWS, 2 * TQ), F32),
            ],
        ),
        compiler_params=_cparams(("parallel", "parallel")),
        name="diff_attention",
    )(qi_tab, t_tab, qkv, qkv, qkv, km, vm, jnp.swapaxes(dblk, -1, -2), jnp.swapaxes(bm0, -1, -2), lamv,
      jnp.broadcast_to(gain.reshape(LANES, 1), (LANES, LANES)))


def _swa_kernel(sink_ref, q_ref, k_ref, v_ref, km_ref, vm_ref, bt_ref, o_ref, kd_ref, vt_ref,
                s_scr, p_scr, inv_scr):
    ci = pl.program_id(1)
    nblk = q_ref.shape[0] // BLOCK
    nkb = k_ref.shape[0] // BLOCK
    lane = lax.broadcasted_iota(jnp.int32, (BLOCK, LANES), 1)

    def both_halves(k):
        k0, k1 = k[:, :HEAD_DIM], k[:, HEAD_DIM:]
        return jnp.concatenate([k0, k0, k1, k1], axis=1)

    @pl.when(ci == 0)
    def _per_batch():
        def body(j, carry):
            rows = pl.ds(pl.multiple_of(j * BLOCK, BLOCK), BLOCK)
            kd_ref[j] = both_halves(k_ref[rows, :])
            vt_ref[j] = jnp.transpose(v_ref[rows, :].astype(F32)).astype(BF16)
            return carry
        lax.fori_loop(0, nkb, body, 0)
        kd_ref[nkb] = both_halves(km_ref[...])
        vt_ref[nkb] = jnp.transpose(vm_ref[...].astype(F32)).astype(BF16)

    def block_body(n, carry):
        gblk = ci * nblk + n
        first = jnp.where(gblk == 0, 0, 1)
        prev = jnp.maximum(gblk - 1, 0)
        r_q = pl.multiple_of(n * BLOCK, BLOCK)
        pairs = [(g, u) for g in range(N_SWA_KV) for u in range(2)]

        def scores(c):
            g, u = pairs[c]
            ks = slice(g * LANES, (g + 1) * LANES)
            kcat = jnp.concatenate([kd_ref[nkb, :, ks], kd_ref[prev, :, ks], kd_ref[gblk, :, ks]], axis=0)
            h0 = 4 * g + 2 * u
            qp = q_ref[pl.ds(r_q, BLOCK), (2 * g + u) * LANES:(2 * g + u + 1) * LANES].astype(F32)
            qs = jnp.transpose(jnp.concatenate([jnp.where(lane < HEAD_DIM, qp, 0.0),
                                                jnp.where(lane >= HEAD_DIM, qp, 0.0)], axis=0)).astype(BF16)
            s = jnp.dot(kcat, qs, preferred_element_type=F32)
            s_scr[c] = s + jnp.concatenate([bt_ref[first, h0], bt_ref[first, h0 + 1]], axis=1)

        def exponentials(c):
            g, u = pairs[c]
            h0 = 4 * g + 2 * u
            s = s_scr[c]
            sink = jnp.concatenate([sink_ref[h0:h0 + 1, :], sink_ref[h0 + 1:h0 + 2, :]], axis=1)
            m = jnp.maximum(jnp.max(s, axis=0, keepdims=True), sink)
            p = jnp.exp(s - m)
            p_scr[c] = p.astype(BF16)
            inv_scr[c] = jnp.broadcast_to(1.0 / (jnp.sum(p, axis=0, keepdims=True) + jnp.exp(sink - m)),
                                          inv_scr.shape[1:])

        def values(c):
            g, u = pairs[c]
            vs = slice(g * HEAD_DIM, (g + 1) * HEAD_DIM)
            vcat = jnp.concatenate([vt_ref[nkb, vs, :], vt_ref[prev, vs, :], vt_ref[gblk, vs, :]], axis=1)
            o = jnp.dot(vcat, p_scr[c], preferred_element_type=F32) * inv_scr[c][0:1]
            ot = jnp.transpose(o)
            o_ref[pl.ds(r_q, BLOCK), (2 * g + u) * LANES:(2 * g + u + 1) * LANES] = (
                jnp.concatenate([ot[:BLOCK], ot[BLOCK:]], axis=1).astype(BF16))

        for stage in (scores, exponentials, values):
            for c in range(len(pairs)):
                stage(c)
        return carry

    lax.fori_loop(0, nblk, block_body, 0)


def _swa_attention(sinks, qkv, km, vm, bt, batch, seq):
    nc = seq // CQ
    nkb = seq // BLOCK
    sinkv = jnp.broadcast_to(sinks.reshape(N_SWA_HEADS, 1), (N_SWA_HEADS, LANES))
    return pl.pallas_call(
        _swa_kernel,
        out_shape=jax.ShapeDtypeStruct((batch * seq, N_SWA_HEADS * HEAD_DIM), BF16),
        grid=(batch, nc),
        in_specs=[
            pl.BlockSpec(sinkv.shape, lambda b, c: (0, 0)),
            pl.BlockSpec((CQ, 512), lambda b, c: (b * nc + c, C_SQ // 512)),
            pl.BlockSpec((seq, LANES), lambda b, c: (b, C_SK // LANES)),
            pl.BlockSpec((seq, LANES), lambda b, c: (b, C_SV // LANES)),
            pl.BlockSpec(km.shape, lambda b, c: (0, 0)),
            pl.BlockSpec(vm.shape, lambda b, c: (0, 0)),
            pl.BlockSpec(bt.shape, lambda b, c: (0, 0, 0, 0)),
        ],
        out_specs=pl.BlockSpec((CQ, 512), lambda b, c: (b * nc + c, 0)),
        scratch_shapes=[pltpu.VMEM((nkb + 1, BLOCK, 2 * LANES), BF16),
                        pltpu.VMEM((nkb + 1, LANES, BLOCK), BF16),
                        pltpu.VMEM((4, 3 * BLOCK, 2 * BLOCK), F32),
                        pltpu.VMEM((4, 3 * BLOCK, 2 * BLOCK), BF16),
                        pltpu.VMEM((4, 8, 2 * BLOCK), F32)],
        compiler_params=_cparams(("parallel", "arbitrary")),
        name="swa_attention",
    )(sinkv, qkv, qkv, qkv, km, vm, bt)


def _outproj_kernel(x_ref, md_ref, ms_ref, wo_ref, g2_ref, wr_ref, br_ref,
                    h_ref, hb_ref, rt_ref, ti_ref, cnt_ref, c_ref, lg_ref):
    i = pl.program_id(0)

    @pl.when(i == 0)
    def _init():
        c_ref[...] = jnp.zeros(c_ref.shape, F32)
        lg_ref[...] = jnp.zeros(lg_ref.shape, F32)

    lg_prev = lg_ref[...]
    half = md_ref.shape[1]
    h = (x_ref[...]
         + jnp.dot(md_ref[...], wo_ref[:half, :], preferred_element_type=F32)
         + jnp.dot(ms_ref[...], wo_ref[half:, :], preferred_element_type=F32))
    h_ref[...] = h
    hn = h * lax.rsqrt(jnp.mean(h * h, axis=-1, keepdims=True) + EPS) * g2_ref[...]
    hb = hn.astype(BF16)
    hb_ref[...] = hb
    lg_ref[...] = jnp.dot(hb, wr_ref[...], preferred_element_type=F32) + br_ref[...]
    _route_tile(lg_prev, jnp.where(i > 0, 1.0, 0.0), rt_ref, ti_ref, c_ref)

    @pl.when(i == pl.num_programs(0) - 1)
    def _fin():
        cnt_ref[...] = c_ref[...]


def _route_tile(lg, live, rt_ref, ti_ref, c_ref):
    tm = lg.shape[0]
    lane_i = lax.broadcasted_iota(jnp.int32, lg.shape, 1)
    lane = lane_i.astype(F32)
    big = float(4 * LANES)
    is_g = (lane_i >= N_EXPERTS) & (lane_i < N_EXPERTS + N_GROUPS)
    glm = jnp.where(is_g, lg, -jnp.inf)
    gmax = jnp.max(glm, axis=1, keepdims=True)
    gidx = jnp.min(jnp.where(glm == gmax, lane, big), axis=1, keepdims=True) - N_EXPERTS
    gsum = jnp.sum(jnp.where(is_g, jnp.exp(lg - gmax), 0.0), axis=1, keepdims=True)
    g_w = 1.0 / gsum
    lane_grp = (lane_i >> 3).astype(F32)
    in_grp = (lane_i < N_EXPERTS) & (lane_grp == gidx)
    el = jnp.where(in_grp, lg, -jnp.inf)
    t1 = jnp.max(el, axis=1, keepdims=True)
    j1 = jnp.min(jnp.where(el == t1, lane, big), axis=1, keepdims=True)
    el2 = jnp.where(lane == j1, -jnp.inf, el)
    t2 = jnp.max(el2, axis=1, keepdims=True)
    j2 = jnp.min(jnp.where(el2 == t2, lane, big), axis=1, keepdims=True)
    e2 = jnp.exp(t2 - t1)
    den = 1.0 + e2
    gate1 = g_w / den
    gate2 = g_w * e2 / den

    o1 = lane == j1
    o2 = lane == j2
    onehot = jnp.where(o1 | o2, 1.0, 0.0).astype(BF16)
    rr = lax.broadcasted_iota(jnp.int32, (tm, tm), 0)
    cc = lax.broadcasted_iota(jnp.int32, (tm, tm), 1)
    lower = jnp.where(rr > cc, 1.0, 0.0).astype(BF16)
    pfx = jnp.dot(lower, onehot, preferred_element_type=F32)
    cnt_tile = jnp.sum(onehot.astype(F32), axis=0, keepdims=True)
    groups = jnp.floor((cnt_tile + (ROW_ALIGN - 1)) * (1.0 / ROW_ALIGN))
    er = lax.broadcasted_iota(jnp.int32, (LANES, LANES), 0)
    ec = lax.broadcasted_iota(jnp.int32, (LANES, LANES), 1)
    before = jnp.where(er < ec, 1.0, 0.0).astype(BF16)
    cbase = ROW_ALIGN * jnp.dot(jnp.broadcast_to(groups, (8, LANES)).astype(BF16), before,
                                preferred_element_type=F32)[0:1]
    at = pfx + cbase
    pos1 = jnp.sum(jnp.where(o1, at, 0.0), axis=1, keepdims=True)
    pos2 = jnp.sum(jnp.where(o2, at, 0.0), axis=1, keepdims=True)
    rt_ref[...] = jnp.where(lane_i == 0, gate1,
                            jnp.where(lane_i == 1, gate2,
                                      jnp.where(lane_i == 2, pos1,
                                                jnp.where(lane_i == 3, pos2, 0.0))))
    c_old = c_ref[...]
    c_ref[...] = c_old + groups * (ROW_ALIGN * live)
    row8 = lax.broadcasted_iota(jnp.int32, (8, LANES), 0)
    ti_ref[...] = jnp.where(row8 == 0, cnt_tile, jnp.where(row8 == 1, c_old, 0.0))


def _outproj(x2, mixd, mixs, wo, g2, wr, br):
    n, d = x2.shape
    nt = n // TM

    def proj_tile(i):
        return (jnp.minimum(i, nt - 1), 0)

    def route_tile(i):
        return (jnp.maximum(i - 1, 0), 0)

    return pl.pallas_call(
        _outproj_kernel,
        out_shape=(jax.ShapeDtypeStruct((n, d), F32),
                   jax.ShapeDtypeStruct((n, d), BF16),
                   jax.ShapeDtypeStruct((n, LANES), F32),
                   jax.ShapeDtypeStruct((nt * 8, LANES), F32),
                   jax.ShapeDtypeStruct((8, LANES), F32)),
        grid=(nt + 1,),
        in_specs=[
            pl.BlockSpec((TM, d), proj_tile),
            pl.BlockSpec((TM, mixd.shape[1]), proj_tile),
            pl.BlockSpec((TM, mixs.shape[1]), proj_tile),
            pl.BlockSpec(wo.shape, lambda i: (0, 0)),
            pl.BlockSpec(g2.shape, lambda i: (0, 0)),
            pl.BlockSpec(wr.shape, lambda i: (0, 0)),
            pl.BlockSpec(br.shape, lambda i: (0, 0)),
        ],
        out_specs=(pl.BlockSpec((TM, d), proj_tile),
                   pl.BlockSpec((TM, d), proj_tile),
                   pl.BlockSpec((TM, LANES), route_tile),
                   pl.BlockSpec((8, LANES), route_tile),
                   pl.BlockSpec((8, LANES), lambda i: (0, 0))),
        scratch_shapes=[pltpu.VMEM((8, LANES), F32), pltpu.VMEM((TM, LANES), F32)],
        compiler_params=_cparams(("arbitrary",)),
        name="outproj_router",
    )(x2, mixd, mixs, wo, g2, wr, br)


def _for_each_chunk(runs_ref, fn):
    big = CHUNK_ROWS[0]

    def per_expert(e, sorted_row, priority):
        start = runs_ref[0, 0, e]
        groups = runs_ref[0, 0, N_EXPERTS + e]
        whole = groups // (big // ROW_ALIGN)

        def per_chunk(c, carry):
            fn(pl.multiple_of(start + c * big, ROW_ALIGN), pl.multiple_of(sorted_row + c * big, ROW_ALIGN),
               big, priority)
            return carry
        lax.fori_loop(0, whole, per_chunk, 0)

        done = whole * big
        for rows in CHUNK_ROWS[1:]:
            has = (groups // (rows // ROW_ALIGN)) % 2

            @pl.when(has == 1)
            def _(done=done, rows=rows):
                fn(pl.multiple_of(start + done, ROW_ALIGN), pl.multiple_of(sorted_row + done, ROW_ALIGN),
                   rows, priority)
            done = done + has * rows
        return sorted_row + groups * ROW_ALIGN

    def expert_pair(e2, sorted_row):
        return per_expert(2 * e2 + 1, per_expert(2 * e2, sorted_row, 0), 1)
    lax.fori_loop(0, N_EXPERTS // 2, expert_pair, 0)


def _wait_chunks(runs_ref, make_copy):
    for k, rows in enumerate(CHUNK_ROWS):
        def body(c, carry, rows=rows):
            make_copy(rows).wait()
            return carry
        lax.fori_loop(0, runs_ref[0, 0, 2 * N_EXPERTS + k], body, 0)


def _dispatch_kernel(zf_ref, cur_ref, prv_ref, hb_ref, rt_ref, xs_ref, sbuf, zbuf, sem, zsem):
    i = pl.program_id(0)
    nt = pl.num_programs(0)
    slot = i % 2
    tm, d = hb_ref.shape

    def for_zero_blocks(kind, fn):
        def body(b, carry):
            @pl.when(zf_ref[b] == kind)
            def _():
                fn(pltpu.make_async_copy(zbuf, xs_ref.at[pl.ds(pl.multiple_of(b * EB, EB), EB)],
                                         zsem.at[kind - 1]))
            return carry
        lax.fori_loop(0, zf_ref.shape[0], body, 0)

    @pl.when(i == 0)
    def _():
        zbuf[...] = jnp.zeros(zbuf.shape, zbuf.dtype)
        for_zero_blocks(1, lambda c: c.start())
        for_zero_blocks(2, lambda c: c.start())
        for_zero_blocks(1, lambda c: c.wait())

    pos_t = jnp.transpose(rt_ref[...])
    srow = lax.broadcasted_iota(jnp.int32, (SROWS, tm), 0).astype(F32)
    sel = jnp.where(srow == pos_t[2:3, :], 1.0, jnp.where(srow == pos_t[3:4, :], 1.0, 0.0)).astype(BF16)
    srt = jnp.dot(sel, hb_ref[...], preferred_element_type=F32)
    bits = pltpu.bitcast(srt, jnp.uint32)
    sbuf[slot] = (bits[:, d // 2:] & jnp.uint32(0xFFFF0000)) | (bits[:, :d // 2] >> 16)

    def chunk_copy(run_row, sorted_row, rows, sl):
        return pltpu.make_async_copy(sbuf.at[sl, pl.ds(sorted_row, rows)], xs_ref.at[pl.ds(run_row, rows)],
                                     sem.at[sl])

    _for_each_chunk(cur_ref, lambda run_row, sorted_row, rows, priority:
                    chunk_copy(run_row, sorted_row, rows, slot).start(priority=priority))

    @pl.when(i > 0)
    def _():
        _wait_chunks(prv_ref, lambda rows: chunk_copy(0, 0, rows, 1 - slot))

    @pl.when(i == nt - 1)
    def _():
        _wait_chunks(cur_ref, lambda rows: chunk_copy(0, 0, rows, slot))
        for_zero_blocks(2, lambda c: c.wait())


def _dispatch(zero_blocks, runs, hb, rt, n_rows):
    n, d = hb.shape
    return pl.pallas_call(
        _dispatch_kernel,
        out_shape=jax.ShapeDtypeStruct((n_rows, d // 2), jnp.uint32),
        grid_spec=pltpu.PrefetchScalarGridSpec(
            num_scalar_prefetch=1,
            grid=(n // TM,),
            in_specs=[
                pl.BlockSpec((1, 1, LANES), lambda i, zf: (i, 0, 0), memory_space=pltpu.SMEM),
                pl.BlockSpec((1, 1, LANES), lambda i, zf: (jnp.maximum(i - 1, 0), 0, 0), memory_space=pltpu.SMEM),
                pl.BlockSpec((TM, d), lambda i, zf: (i, 0)),
                pl.BlockSpec((TM, LANES), lambda i, zf: (i, 0)),
            ],
            out_specs=pl.BlockSpec(memory_space=pl.ANY),
            scratch_shapes=[pltpu.VMEM((2, SROWS, d // 2), jnp.uint32), pltpu.VMEM((EB, d // 2), jnp.uint32),
                            pltpu.SemaphoreType.DMA((2,)), pltpu.SemaphoreType.DMA((2,))],
        ),
        compiler_params=_cparams(("arbitrary",)),
        name="dispatch",
    )(zero_blocks, runs, runs, hb, rt)


def _experts_kernel(be_ref, na_ref, nxt_ref, xs_ref, wg_hbm, wu_hbm, wd_hbm, ys_ref,
                    wgf, wuf, wdf, wgb, wub, wdb, sem):
    b = pl.program_id(0)

    def weight_copies(e):
        return (pltpu.make_async_copy(wg_hbm.at[e], wgf, sem.at[0]),
                pltpu.make_async_copy(wu_hbm.at[e], wuf, sem.at[1]),
                pltpu.make_async_copy(wd_hbm.at[e], wdf, sem.at[2]))

    @pl.when(b == 0)
    def _():
        for c in weight_copies(be_ref[0]):
            c.start()

    @pl.when(b < na_ref[0])
    def _():
        e = be_ref[b]
        changed = jnp.logical_or(b == 0, be_ref[jnp.maximum(b - 1, 0)] != e)

        @pl.when(changed)
        def _load():
            for c in weight_copies(e):
                c.wait()
            wgb[...] = wgf[...].astype(BF16)
            wub[...] = wuf[...].astype(BF16)
            wdb[...] = wdf[...].astype(BF16)
            nxt = nxt_ref[e]

            @pl.when(nxt >= 0)
            def _():
                for c in weight_copies(nxt):
                    c.start()

        w = xs_ref[...]
        x_lo = pltpu.bitcast(w << 16, F32).astype(BF16)
        x_hi = pltpu.bitcast(w & jnp.uint32(0xFFFF0000), F32).astype(BF16)
        dh = w.shape[1]
        g = (jnp.dot(x_lo, wgb[:dh, :], preferred_element_type=F32)
             + jnp.dot(x_hi, wgb[dh:, :], preferred_element_type=F32))
        u = (jnp.dot(x_lo, wub[:dh, :], preferred_element_type=F32)
             + jnp.dot(x_hi, wub[dh:, :], preferred_element_type=F32))
        hdn = g * (1.0 / (1.0 + jnp.exp(-g))) * u
        y = jnp.dot(hdn.astype(BF16), wdb[...], preferred_element_type=F32)
        bits = pltpu.bitcast(y.astype(BF16).astype(F32), jnp.uint32)
        ys_ref[...] = (bits[:, dh:] & jnp.uint32(0xFFFF0000)) | (bits[:, :dh] >> 16)

    @pl.when(b >= na_ref[0])
    def _():
        ys_ref[...] = jnp.zeros(ys_ref.shape, ys_ref.dtype)


def _experts(blk_e, n_act, nxt_e, xs, w_gate, w_up, w_down):
    p, dh = xs.shape
    d = 2 * dh
    de = w_gate.shape[2]

    def row_map(b, be, na, nx):
        return (jnp.minimum(b, na[0] - 1), 0)

    return pl.pallas_call(
        _experts_kernel,
        out_shape=jax.ShapeDtypeStruct((p, dh), jnp.uint32),
        grid_spec=pltpu.PrefetchScalarGridSpec(
            num_scalar_prefetch=3,
            grid=(p // EB,),
            in_specs=[
                pl.BlockSpec((EB, dh), row_map),
                pl.BlockSpec(memory_space=pl.ANY),
                pl.BlockSpec(memory_space=pl.ANY),
                pl.BlockSpec(memory_space=pl.ANY),
            ],
            out_specs=pl.BlockSpec((EB, dh), lambda b, be, na, nx: (b, 0)),
            scratch_shapes=[pltpu.VMEM((d, de), F32), pltpu.VMEM((d, de), F32), pltpu.VMEM((de, d), F32),
                            pltpu.VMEM((d, de), BF16), pltpu.VMEM((d, de), BF16), pltpu.VMEM((de, d), BF16),
                            pltpu.SemaphoreType.DMA((3,))],
        ),
        compiler_params=_cparams(("arbitrary",)),
        name="experts",
    )(blk_e, n_act, nxt_e, xs, w_gate, w_up, w_down)


def _combine_kernel(cur_ref, nxt_ref, ys_ref, h_ref, rt_ref, o_ref, ybuf, sem):
    i = pl.program_id(0)
    nt = pl.num_programs(0)
    slot = i % 2
    tm = h_ref.shape[0]

    def chunk_copy(run_row, sorted_row, rows, sl):
        return pltpu.make_async_copy(ys_ref.at[pl.ds(run_row, rows)], ybuf.at[sl, pl.ds(sorted_row, rows)],
                                     sem.at[sl])

    @pl.when(i == 0)
    def _():
        ybuf[...] = jnp.zeros(ybuf.shape, ybuf.dtype)
        _for_each_chunk(cur_ref, lambda run_row, sorted_row, rows, priority:
                        chunk_copy(run_row, sorted_row, rows, 0).start(priority=priority))

    @pl.when(i + 1 < nt)
    def _():
        _for_each_chunk(nxt_ref, lambda run_row, sorted_row, rows, priority:
                        chunk_copy(run_row, sorted_row, rows, 1 - slot).start(priority=priority))

    _wait_chunks(cur_ref, lambda rows: chunk_copy(0, 0, rows, slot))

    rt = rt_ref[...]
    w = ybuf[slot]
    dh = w.shape[1]
    y_lo = pltpu.bitcast(w << 16, F32).astype(BF16)
    y_hi = pltpu.bitcast(w & jnp.uint32(0xFFFF0000), F32).astype(BF16)
    col = lax.broadcasted_iota(jnp.int32, (tm, SROWS), 1).astype(F32)
    wsel = jnp.where(col == rt[:, 2:3], rt[:, 0:1], jnp.where(col == rt[:, 3:4], rt[:, 1:2], 0.0)).astype(BF16)
    for half, yb in ((slice(0, dh), y_lo), (slice(dh, 2 * dh), y_hi)):
        o_ref[:, half] = h_ref[:, half] + jnp.dot(wsel, yb, preferred_element_type=F32)


def _combine(runs, ys, h1, rt):
    n, d = h1.shape
    nt = n // TM
    return pl.pallas_call(
        _combine_kernel,
        out_shape=jax.ShapeDtypeStruct((n, d), F32),
        grid=(nt,),
        in_specs=[
            pl.BlockSpec((1, 1, LANES), lambda i: (i, 0, 0), memory_space=pltpu.SMEM),
            pl.BlockSpec((1, 1, LANES), lambda i: (jnp.minimum(i + 1, nt - 1), 0, 0), memory_space=pltpu.SMEM),
            pl.BlockSpec(memory_space=pl.ANY),
            pl.BlockSpec((TM, d), lambda i: (i, 0)),
            pl.BlockSpec((TM, LANES), lambda i: (i, 0)),
        ],
        out_specs=pl.BlockSpec((TM, d), lambda i: (i, 0)),
        scratch_shapes=[pltpu.VMEM((2, SROWS, d // 2), jnp.uint32), pltpu.SemaphoreType.DMA((2,))],
        compiler_params=_cparams(("arbitrary",)),
        name="combine",
    )(runs, runs, ys, h1, rt)


def kernel(x, meta_tokens, rel_bias, norm1_gain, w_in, diff_q_gain, diff_k_gain, lam_q1, lam_k1, lam_q2, lam_k2, diff_subln_gain, swa_q_gain, swa_k_gain, swa_sinks, w_out, norm2_gain, w_group, b_group, w_router, b_router, w_gate, w_up, w_down):
    batch, seq, d = x.shape
    depth = w_in.shape[0]
    n = batch * seq
    assert seq % CQ == 0 and seq % TQ == 0 and n % TM == 0 and d == 1024
    assert meta_tokens.shape[0] == N_META
    assert depth == 1, "the meta-token rows of the residual stream are not carried across layers"

    h = x.reshape(n, d)
    dblk, bm0, bt = _bias_tables(rel_bias, TQ)
    scale = HEAD_DIM ** -0.5
    bd = jnp.asarray(np.kron(np.eye(MXU_DIM // HEAD_DIM), np.full((HEAD_DIM, HEAD_DIM), 1.0 / HEAD_DIM)), BF16)
    ones = jnp.ones((HEAD_DIM,), F32)
    lower_pad = N_EXPERTS + N_GROUPS

    for layer in range(depth):
        lambda_init = 0.8 - 0.6 * math.exp(-0.3 * layer)
        w_cat = w_in[layer].astype(BF16)
        gain = jnp.concatenate([
            jnp.tile(diff_q_gain[layer] * (scale * LOG2E), 2 * N_DIFF_HEADS),
            jnp.tile(diff_k_gain[layer], 2 * N_DIFF_HEADS),
            jnp.tile(ones, 2 * N_DIFF_HEADS),
            jnp.tile(swa_q_gain[layer] * scale, N_SWA_HEADS),
            jnp.tile(swa_k_gain[layer], N_SWA_KV),
            jnp.tile(ones, N_SWA_KV)]).reshape(1, C_END).astype(F32)
        nmask = np.zeros((1, C_END), np.float32)
        nmask[:, C_DQ:C_DV] = 1.0
        nmask[:, C_SQ:C_SV] = 1.0
        nmask = jnp.asarray(nmask)
        g1 = norm1_gain[layer].reshape(1, d).astype(F32)

        qkv = _proj(h, g1, w_cat, bd, gain, nmask, TP)
        qkv_meta = _proj(meta_tokens.astype(F32), g1, w_cat, bd, gain, nmask, N_META)
        meta_pad = jnp.pad(qkv_meta, ((0, TQ - N_META), (0, 0)))

        lamv = jnp.pad(jnp.stack([lam_q1[layer], lam_k1[layer], lam_q2[layer], lam_k2[layer]]).astype(F32),
                       ((0, 4), (0, LANES - HEAD_DIM)))
        mixd = _diff_attention(qkv, meta_pad[:, C_DK:C_DV], meta_pad[:, C_DV:C_SQ], dblk, bm0, lamv,
                               diff_subln_gain[layer].reshape(1, LANES).astype(F32), batch, seq, lambda_init)
        mixs = _swa_attention(swa_sinks[layer].astype(F32), qkv, meta_pad[:BLOCK, C_SK:C_SV],
                              meta_pad[:BLOCK, C_SV:C_END], jnp.swapaxes(bt, -1, -2), batch, seq)

        wr = jnp.pad(jnp.concatenate([w_router[layer], w_group[layer]], axis=1),
                     ((0, 0), (0, LANES - lower_pad))).astype(BF16)
        br = jnp.pad(jnp.concatenate([b_router[layer], b_group[layer]]), (0, LANES - lower_pad)).reshape(1, LANES)
        h1, hb, rt, tinfo, cnt = _outproj(h, mixd, mixs, w_out[layer].astype(BF16),
                                          norm2_gain[layer].reshape(1, d).astype(F32), wr, br.astype(F32))

        nt = n // TM
        counts = cnt[0, :N_EXPERTS].astype(jnp.int32)
        nblk_e = (counts + EB - 1) // EB
        blk_end = jnp.cumsum(nblk_e)
        pstart = ((blk_end - nblk_e) * EB).astype(jnp.int32)
        n_blocks = -(-(2 * n + nt * N_EXPERTS * (ROW_ALIGN - 1) + N_EXPERTS * (EB - 1)) // EB)
        blk_ids = jnp.arange(n_blocks)
        blk_e = jnp.minimum(jnp.sum(blk_end[None, :] <= blk_ids[:, None], axis=1), N_EXPERTS - 1).astype(jnp.int32)
        n_act = blk_end[-1:].astype(jnp.int32)
        is_last = jnp.any((blk_end[None, :] == blk_ids[:, None] + 1) & (nblk_e[None, :] > 0), axis=1)
        zero_blocks = jnp.where(blk_ids >= n_act[0], 2, jnp.where(is_last, 1, 0)).astype(jnp.int32)
        ti = tinfo.reshape(nt, 8, LANES)
        run_len = ti[:, 0, :N_EXPERTS].astype(jnp.int32)
        run_start = pstart[None, :] + ti[:, 1, :N_EXPERTS].astype(jnp.int32)
        run_groups = (run_len + ROW_ALIGN - 1) // ROW_ALIGN
        n_copies = [jnp.sum(run_groups // (CHUNK_ROWS[0] // ROW_ALIGN), axis=1, keepdims=True)]
        n_copies += [jnp.sum((run_groups // (rows // ROW_ALIGN)) % 2, axis=1, keepdims=True)
                     for rows in CHUNK_ROWS[1:]]
        runs = jnp.concatenate([run_start, run_groups] + n_copies
                               + [jnp.zeros((nt, LANES - 2 * N_EXPERTS - len(CHUNK_ROWS)), jnp.int32)],
                               axis=1).reshape(nt, 1, LANES)

        xs = _dispatch(zero_blocks, runs, hb, rt, n_blocks * EB)
        own = jnp.where(nblk_e > 0, jnp.arange(N_EXPERTS), N_EXPERTS)
        later = jnp.concatenate([lax.cummin(own[::-1])[::-1][1:], jnp.full((1,), N_EXPERTS)])
        nxt_e = jnp.where(later < N_EXPERTS, later, -1).astype(jnp.int32)
        ys = _experts(blk_e, n_act, nxt_e, xs, w_gate[layer], w_up[layer], w_down[layer])
        h = _combine(runs, ys, h1, rt)
    return h.reshape(batch, seq, d)
```

```python
import functools
import math

import numpy as np
import jax
import jax.numpy as jnp
from jax import lax
from jax.experimental import pallas as pl
from jax.experimental.pallas import tpu as pltpu

F32 = jnp.float32
BF16 = jnp.bfloat16

HEAD_DIM = 64
N_DIFF_HEADS = 4
N_SWA_HEADS = 8
N_SWA_KV = 2
BLOCK = 128
N_META = 16
N_BUCKETS = 32
MAX_DISTANCE = 128
N_GROUPS = 4
EXPERTS_PER_GROUP = 8
N_EXPERTS = N_GROUPS * EXPERTS_PER_GROUP
D_EXPERT = 512
EPS = 1e-6
NEG = -1e30
LOG2E = math.log2(math.e)

LANES = 128
MXU_DIM = 256
VMEM_LIMIT = 48 * 1024 * 1024

TP = 512
TM = 512
TQ = 256
ONES_ROWS = 16
EB = 512
ROW_ALIGN = 8
CHUNK_ROWS = (32, 16, 8)
SROWS = -(-(2 * TM + N_EXPERTS * (ROW_ALIGN - 1)) // MXU_DIM) * MXU_DIM

C_DQ, C_DK, C_DV, C_SQ, C_SK, C_SV, C_END = 0, 512, 1024, 1536, 2048, 2176, 2304
NORM_GROUPS = (0, 1, 2, 3, 6, 7, 8)


def _cparams(sem):
    return pltpu.CompilerParams(dimension_semantics=sem, vmem_limit_bytes=VMEM_LIMIT)


def _t5_bucket_np(dist):
    n = np.maximum(dist, 0)
    max_exact = N_BUCKETS // 2
    nf = np.maximum(n, 1).astype(np.float32)
    large = max_exact + (np.log(nf / np.float32(max_exact)) / np.float32(math.log(MAX_DISTANCE / max_exact))
                         * np.float32(N_BUCKETS - max_exact)).astype(np.int32)
    large = np.minimum(large, N_BUCKETS - 1)
    return np.where(n < max_exact, n, large)


def _bias_tables(rel_bias, tq):
    nd = 2 * BLOCK
    buckets = _t5_bucket_np(np.arange(nd))
    assert (buckets[MAX_DISTANCE:] == N_BUCKETS - 1).all()
    rb = rel_bias.astype(F32)
    r = np.arange(BLOCK)[:, None]
    c = np.arange(BLOCK)[None, :]
    d_own = r - c
    d_prev = BLOCK + r - c
    far = rb[N_BUCKETS - 1]

    def take(dist):
        idx = jnp.asarray(buckets[np.clip(dist, 0, nd - 1)], jnp.int32)[None]
        out = jnp.zeros((rb.shape[1],) + dist.shape, F32)
        for b in range(N_BUCKETS):
            out = jnp.where(idx == b, rb[b].reshape((-1,) + (1,) * dist.ndim), out)
        return out

    hd = slice(0, N_DIFF_HEADS)
    far_d = far[hd][:, None, None]
    d0 = jnp.where(d_own[None] >= 0, take(d_own)[hd] - far_d, NEG)
    d1 = take(d_prev)[hd] - far_d
    dblk = jnp.stack([d0, d1], axis=1)
    rq = np.arange(tq)[:, None]
    cm = np.arange(LANES)[None, :]
    d_meta = N_META + rq - cm
    bm0 = jnp.where((cm < N_META)[None], take(d_meta)[hd] - far_d, NEG)

    hs = slice(N_DIFF_HEADS, N_DIFF_HEADS + N_SWA_HEADS)
    far_s = far[hs][:, None, None]
    d_meta_s = N_META + r - cm
    meta_first = jnp.where((cm < N_META)[None], take(d_meta_s)[hs], NEG)
    meta_rest = jnp.where((cm < N_META)[None], jnp.broadcast_to(far_s, (N_SWA_HEADS, BLOCK, LANES)), NEG)
    prev_rest = jnp.where((c > r)[None], take(d_prev)[hs], NEG)
    prev_first = jnp.full((N_SWA_HEADS, BLOCK, BLOCK), NEG, F32)
    own = jnp.where((d_own >= 0)[None], take(d_own)[hs], NEG)
    bt = jnp.stack([jnp.concatenate([meta_first, prev_first, own], axis=-1),
                    jnp.concatenate([meta_rest, prev_rest, own], axis=-1)], axis=0)
    return dblk.astype(F32), bm0.astype(F32), bt.astype(F32)


def _proj_kernel(x_ref, g1_ref, w_ref, bd_ref, gain_ref, nmask_ref, o_ref):
    x = x_ref[...]
    a = x * lax.rsqrt(jnp.mean(x * x, axis=-1, keepdims=True) + EPS) * g1_ref[...]
    p = jnp.dot(a.astype(BF16), w_ref[...], preferred_element_type=F32)
    bd = bd_ref[...]
    for j in range(C_END // MXU_DIM):
        sl = slice(j * MXU_DIM, (j + 1) * MXU_DIM)
        pj = p[:, sl]
        if j in NORM_GROUPS:
            ms = jnp.dot((pj * pj).astype(BF16), bd, preferred_element_type=F32)
            pj = jnp.where(nmask_ref[:, sl] != 0.0, pj * lax.rsqrt(ms + EPS) * gain_ref[:, sl], pj)
        o_ref[:, sl] = pj.astype(BF16)


def _proj(x2, g1, w, bd, gain, nmask, tm):
    n = x2.shape[0]
    return pl.pallas_call(
        _proj_kernel,
        out_shape=jax.ShapeDtypeStruct((n, C_END), BF16),
        grid=(n // tm,),
        in_specs=[
            pl.BlockSpec((tm, x2.shape[1]), lambda i: (i, 0)),
            pl.BlockSpec(g1.shape, lambda i: (0, 0)),
            pl.BlockSpec(w.shape, lambda i: (0, 0)),
            pl.BlockSpec(bd.shape, lambda i: (0, 0)),
            pl.BlockSpec(gain.shape, lambda i: (0, 0)),
            pl.BlockSpec(nmask.shape, lambda i: (0, 0)),
        ],
        out_specs=pl.BlockSpec((tm, C_END), lambda i: (i, 0)),
        compiler_params=_cparams(("parallel",)),
        name="proj",
    )(x2, g1, w, bd, gain, nmask)


def _diff_kernel(qi_tab, t_tab, q_ref, k_ref, v_ref, km_ref, vm_ref, d_ref, bm0_ref, lamv_ref, gain_ref, o_ref,
                 bias_ref, qs_ref, vt_ref, s_buf, p_buf, a_buf, m_ref, acc_ref, *, lambda_init, n_steps):
    tq = TQ
    nq = q_ref.shape[0] // tq
    nb = tq // BLOCK
    BIAS_NONE, BIAS_LEFT, BIAS_DIAG, BIAS_META, BIAS_META0 = 0, 1, 2, 3, 4

    d0 = d_ref[0, 0] * LOG2E
    d1 = d_ref[0, 1] * LOG2E
    zeros = jnp.zeros((BLOCK, BLOCK), F32)
    bias_ref[BIAS_NONE] = jnp.zeros((tq, tq), F32)
    for a in range(nb):
        for b in range(nb):
            rs, cs = slice(a * BLOCK, (a + 1) * BLOCK), slice(b * BLOCK, (b + 1) * BLOCK)
            if a == b:
                blk = d0
            elif b == a + 1:
                blk = d1
            elif b > a:
                blk = zeros
            else:
                blk = jnp.full((BLOCK, BLOCK), NEG, F32)
            bias_ref[BIAS_DIAG, rs, cs] = blk
            bias_ref[BIAS_LEFT, rs, cs] = d1 if (b == 0 and a == nb - 1) else zeros
    row_m = lax.broadcasted_iota(jnp.int32, (tq, tq), 0)
    bias_ref[BIAS_META] = jnp.where(row_m < N_META, 0.0, NEG).astype(F32)
    bias_ref[BIAS_META0, :LANES, :] = bm0_ref[0] * LOG2E
    bias_ref[BIAS_META0, LANES:, :] = jnp.full((tq - LANES, tq), NEG, F32)

    lane = lax.broadcasted_iota(jnp.int32, (tq, LANES), 1)
    for i in range(nq):
        rows = slice(i * tq, (i + 1) * tq)
        q = q_ref[rows, :].astype(F32)
        qs_ref[i] = jnp.transpose(jnp.concatenate([jnp.where(lane < HEAD_DIM, q, 0.0),
                                                   jnp.where(lane >= HEAD_DIM, q, 0.0)], axis=0)).astype(BF16)
        vt_ref[i, :LANES, :] = jnp.transpose(v_ref[rows, :].astype(F32)).astype(BF16)
    vt_ref[nq, :LANES, :] = jnp.transpose(vm_ref[...].astype(F32)).astype(BF16)
    vt_ref[:, LANES:, :] = jnp.ones((nq + 1, ONES_ROWS, tq), BF16)
    acc_ref[...] = jnp.zeros(acc_ref.shape, F32)
    m_ref[...] = jnp.full(m_ref.shape, NEG, F32)
    lv = lamv_ref[...]
    lam = (jnp.exp(jnp.sum(lv[0:1] * lv[1:2], axis=-1, keepdims=True))
           - jnp.exp(jnp.sum(lv[2:3] * lv[3:4], axis=-1, keepdims=True)) + lambda_init)

    def seq_row(t):
        return pl.multiple_of(jnp.maximum(t - 1, 0) * tq, tq)

    def stage_a(n, slot):
        qi, t = qi_tab[n], t_tab[n]
        kt = jnp.where(t == 0, km_ref[...], k_ref[pl.ds(seq_row(t), tq), :])
        s = jnp.dot(kt, qs_ref[qi], preferred_element_type=F32)
        which = jnp.where(t == 0, jnp.where(qi == 0, BIAS_META0, BIAS_META),
                          jnp.where(t == qi + 1, BIAS_DIAG, jnp.where(t == qi, BIAS_LEFT, BIAS_NONE)))
        s_buf[slot] = s + jnp.tile(bias_ref[which], (1, 2))

    def stage_b(n, slot):
        s = s_buf[slot]
        m_prev = jnp.where(t_tab[n] == 0, NEG, m_ref[...])
        m_new = jnp.maximum(m_prev, jnp.max(s, axis=0, keepdims=True))
        a_buf[slot] = jnp.exp2(m_prev - m_new)
        p_buf[slot] = jnp.exp2(s - m_new[0:1]).astype(BF16)
        m_ref[...] = m_new

    def stage_c(n, slot):
        qi, t = qi_tab[n], t_tab[n]
        vt = vt_ref[jnp.where(t == 0, nq, t - 1)]
        pv = jnp.dot(vt, p_buf[slot], preferred_element_type=F32)
        acc_ref[qi] = a_buf[slot][0:1] * acc_ref[qi] + pv

    stage_a(0, 0)
    stage_a(1, 1)
    stage_b(0, 0)

    def steps(n, count):
        for j in range(count):
            stage_a(n + j + 2, j % 2)
            stage_b(n + j + 1, (j + 1) % 2)
            stage_c(n + j, j % 2)

    unroll = 8
    n_blocks = (n_steps - 2) // unroll

    def block(k, carry):
        steps(unroll * k, unroll)
        return carry
    lax.fori_loop(0, n_blocks, block, 0)
    steps(unroll * n_blocks, n_steps - 2 - unroll * n_blocks)

    stage_b(n_steps - 1, 1)
    stage_c(n_steps - 2, 0)
    stage_c(n_steps - 1, 1)

    for i in range(nq):
        acc = acc_ref[i]
        o = acc[:LANES] * (1.0 / acc[LANES:LANES + 1])
        d = o[:, :tq] - lam * o[:, tq:]
        y = d * lax.rsqrt(jnp.mean(d * d, axis=0, keepdims=True) + EPS) * jnp.tile(gain_ref[...], (1, tq // LANES))
        o_ref[i * tq:(i + 1) * tq, :] = jnp.transpose(y * (1.0 - lambda_init)).astype(BF16)


def _diff_attention(qkv, km, vm, dblk, bm0, lamv, gain, batch, seq, lambda_init):
    nq = seq // TQ
    steps = [(qi, t) for qi in range(nq) for t in range(qi + 2)]
    assert len(steps) % 2 == 0
    qi_tab = jnp.asarray([s[0] for s in steps], jnp.int32)
    t_tab = jnp.asarray([s[1] for s in steps], jnp.int32)
    kern = functools.partial(_diff_kernel, lambda_init=lambda_init, n_steps=len(steps))
    return pl.pallas_call(
        kern,
        out_shape=jax.ShapeDtypeStruct((batch * seq, N_DIFF_HEADS * LANES), BF16),
        grid_spec=pltpu.PrefetchScalarGridSpec(
            num_scalar_prefetch=2,
            grid=(batch, N_DIFF_HEADS),
            in_specs=[
                pl.BlockSpec((seq, LANES), lambda b, h, *_: (b, C_DQ // LANES + h)),
                pl.BlockSpec((seq, LANES), lambda b, h, *_: (b, C_DK // LANES + h)),
                pl.BlockSpec((seq, LANES), lambda b, h, *_: (b, C_DV // LANES + h)),
                pl.BlockSpec((TQ, LANES), lambda b, h, *_: (0, h)),
                pl.BlockSpec((TQ, LANES), lambda b, h, *_: (0, h)),
                pl.BlockSpec((1, 2, BLOCK, BLOCK), lambda b, h, *_: (h, 0, 0, 0)),
                pl.BlockSpec((1, LANES, TQ), lambda b, h, *_: (h, 0, 0)),
                pl.BlockSpec(lamv.shape, lambda b, h, *_: (0, 0)),
                pl.BlockSpec((LANES, LANES), lambda b, h, *_: (0, 0)),
            ],
            out_specs=pl.BlockSpec((seq, LANES), lambda b, h, *_: (b, h)),
            scratch_shapes=[
                pltpu.VMEM((5, TQ, TQ), F32),
                pltpu.VMEM((nq, LANES, 2 * TQ), BF16),
                pltpu.VMEM((nq + 1, LANES + ONES_ROWS, TQ), BF16),
                pltpu.VMEM((2, TQ, 2 * TQ), F32),
                pltpu.VMEM((2, TQ, 2 * TQ), BF16),
                pltpu.VMEM((2, 8, 2 * TQ), F32),
                pltpu.VMEM((8, 2 * TQ), F32),
                pltpu.VMEM((nq, LANES + ONES_ROWS, 2 * TQ), F32),
            ],
        ),
        compiler_params=_cparams(("parallel", "parallel")),
        name="diff_attention",
    )(qi_tab, t_tab, qkv, qkv, qkv, km, vm, jnp.swapaxes(dblk, -1, -2), jnp.swapaxes(bm0, -1, -2), lamv,
      jnp.broadcast_to(gain.reshape(LANES, 1), (LANES, LANES)))


def _swa_kernel(sink_ref, q_ref, k_ref, v_ref, km_ref, vm_ref, bt_ref, o_ref, kd_ref, vt_ref,
                s_scr, p_scr, inv_scr):
    nkb = k_ref.shape[0] // BLOCK
    lane = lax.broadcasted_iota(jnp.int32, (BLOCK, LANES), 1)
    pairs = [(g, u) for g in range(N_SWA_KV) for u in range(2)]

    def both_halves(k):
        k0, k1 = k[:, :HEAD_DIM], k[:, HEAD_DIM:]
        return jnp.concatenate([k0, k0, k1, k1], axis=1)

    def prepare(j, carry):
        rows = pl.ds(pl.multiple_of(j * BLOCK, BLOCK), BLOCK)
        kd_ref[j] = both_halves(k_ref[rows, :])
        vt_ref[j] = jnp.transpose(v_ref[rows, :].astype(F32)).astype(BF16)
        return carry
    lax.fori_loop(0, nkb, prepare, 0)
    kd_ref[nkb] = both_halves(km_ref[...])
    vt_ref[nkb] = jnp.transpose(vm_ref[...].astype(F32)).astype(BF16)

    def scores(n, slot):
        first = jnp.where(n == 0, 0, 1)
        prev = jnp.maximum(n - 1, 0)
        r_q = pl.multiple_of(n * BLOCK, BLOCK)
        for c, (g, u) in enumerate(pairs):
            ks = slice(g * LANES, (g + 1) * LANES)
            kcat = jnp.concatenate([kd_ref[nkb, :, ks], kd_ref[prev, :, ks], kd_ref[n, :, ks]], axis=0)
            h0 = 4 * g + 2 * u
            qp = q_ref[pl.ds(r_q, BLOCK), (2 * g + u) * LANES:(2 * g + u + 1) * LANES].astype(F32)
            qs = jnp.transpose(jnp.concatenate([jnp.where(lane < HEAD_DIM, qp, 0.0),
                                                jnp.where(lane >= HEAD_DIM, qp, 0.0)], axis=0)).astype(BF16)
            s = jnp.dot(kcat, qs, preferred_element_type=F32)
            s_scr[slot, c] = s + jnp.concatenate([bt_ref[first, h0], bt_ref[first, h0 + 1]], axis=1)

    def exponentials(slot):
        for c, (g, u) in enumerate(pairs):
            h0 = 4 * g + 2 * u
            s = s_scr[slot, c]
            sink = jnp.concatenate([sink_ref[h0:h0 + 1, :], sink_ref[h0 + 1:h0 + 2, :]], axis=1)
            m = jnp.maximum(jnp.max(s, axis=0, keepdims=True), sink)
            p = jnp.exp(s - m)
            p_scr[slot, c] = p.astype(BF16)
            inv_scr[slot, c] = jnp.broadcast_to(1.0 / (jnp.sum(p, axis=0, keepdims=True) + jnp.exp(sink - m)),
                                                inv_scr.shape[2:])

    def values(n, slot):
        prev = jnp.maximum(n - 1, 0)
        r_q = pl.multiple_of(n * BLOCK, BLOCK)
        for c, (g, u) in enumerate(pairs):
            vs = slice(g * HEAD_DIM, (g + 1) * HEAD_DIM)
            vcat = jnp.concatenate([vt_ref[nkb, vs, :], vt_ref[prev, vs, :], vt_ref[n, vs, :]], axis=1)
            o = jnp.dot(vcat, p_scr[slot, c], preferred_element_type=F32) * inv_scr[slot, c][0:1]
            ot = jnp.transpose(o)
            o_ref[pl.ds(r_q, BLOCK), (2 * g + u) * LANES:(2 * g + u + 1) * LANES] = (
                jnp.concatenate([ot[:BLOCK], ot[BLOCK:]], axis=1).astype(BF16))

    scores(0, 0)
    scores(1, 1)
    exponentials(0)

    def two_blocks(k, carry):
        n = 2 * k
        scores(n + 2, 0)
        exponentials(1)
        values(n, 0)
        scores(n + 3, 1)
        exponentials(0)
        values(n + 1, 1)
        return carry
    lax.fori_loop(0, nkb // 2 - 1, two_blocks, 0)
    exponentials(1)
    values(nkb - 2, 0)
    values(nkb - 1, 1)


def _swa_attention(sinks, qkv, km, vm, bt, batch, seq):
    nkb = seq // BLOCK
    assert nkb % 2 == 0
    sinkv = jnp.broadcast_to(sinks.reshape(N_SWA_HEADS, 1), (N_SWA_HEADS, LANES))
    return pl.pallas_call(
        _swa_kernel,
        out_shape=jax.ShapeDtypeStruct((batch * seq, N_SWA_HEADS * HEAD_DIM), BF16),
        grid=(batch,),
        in_specs=[
            pl.BlockSpec(sinkv.shape, lambda b: (0, 0)),
            pl.BlockSpec((seq, 512), lambda b: (b, C_SQ // 512)),
            pl.BlockSpec((seq, LANES), lambda b: (b, C_SK // LANES)),
            pl.BlockSpec((seq, LANES), lambda b: (b, C_SV // LANES)),
            pl.BlockSpec(km.shape, lambda b: (0, 0)),
            pl.BlockSpec(vm.shape, lambda b: (0, 0)),
            pl.BlockSpec(bt.shape, lambda b: (0, 0, 0, 0)),
        ],
        out_specs=pl.BlockSpec((seq, 512), lambda b: (b, 0)),
        scratch_shapes=[pltpu.VMEM((nkb + 1, BLOCK, 2 * LANES), BF16),
                        pltpu.VMEM((nkb + 1, LANES, BLOCK), BF16),
                        pltpu.VMEM((2, 4, 3 * BLOCK, 2 * BLOCK), F32),
                        pltpu.VMEM((2, 4, 3 * BLOCK, 2 * BLOCK), BF16),
                        pltpu.VMEM((2, 4, 8, 2 * BLOCK), F32)],
        compiler_params=_cparams(("parallel",)),
        name="swa_attention",
    )(sinkv, qkv, qkv, qkv, km, vm, bt)


def _outproj_kernel(x_ref, md_ref, ms_ref, wo_ref, g2_ref, wr_ref, br_ref,
                    h_ref, hb_ref, rt_ref, ti_ref, cnt_ref, c_ref, lg_ref):
    i = pl.program_id(0)

    @pl.when(i == 0)
    def _init():
        c_ref[...] = jnp.zeros(c_ref.shape, F32)
        lg_ref[...] = jnp.zeros(lg_ref.shape, F32)

    lg_prev = lg_ref[...]
    half = md_ref.shape[1]
    h = (x_ref[...]
         + jnp.dot(md_ref[...], wo_ref[:half, :], preferred_element_type=F32)
         + jnp.dot(ms_ref[...], wo_ref[half:, :], preferred_element_type=F32))
    h_ref[...] = h
    hn = h * lax.rsqrt(jnp.mean(h * h, axis=-1, keepdims=True) + EPS) * g2_ref[...]
    hb = hn.astype(BF16)
    hb_ref[...] = hb
    lg_ref[...] = jnp.dot(hb, wr_ref[...], preferred_element_type=F32) + br_ref[...]
    _route_tile(lg_prev, jnp.where(i > 0, 1.0, 0.0), rt_ref, ti_ref, c_ref)

    @pl.when(i == pl.num_programs(0) - 1)
    def _fin():
        cnt_ref[...] = c_ref[...]


def _route_tile(lg, live, rt_ref, ti_ref, c_ref):
    tm = lg.shape[0]
    lane_i = lax.broadcasted_iota(jnp.int32, lg.shape, 1)
    lane = lane_i.astype(F32)
    big = float(4 * LANES)
    is_g = (lane_i >= N_EXPERTS) & (lane_i < N_EXPERTS + N_GROUPS)
    glm = jnp.where(is_g, lg, -jnp.inf)
    gmax = jnp.max(glm, axis=1, keepdims=True)
    gidx = jnp.min(jnp.where(glm == gmax, lane, big), axis=1, keepdims=True) - N_EXPERTS
    gsum = jnp.sum(jnp.where(is_g, jnp.exp(lg - gmax), 0.0), axis=1, keepdims=True)
    g_w = 1.0 / gsum
    lane_grp = (lane_i >> 3).astype(F32)
    in_grp = (lane_i < N_EXPERTS) & (lane_grp == gidx)
    el = jnp.where(in_grp, lg, -jnp.inf)
    t1 = jnp.max(el, axis=1, keepdims=True)
    j1 = jnp.min(jnp.where(el == t1, lane, big), axis=1, keepdims=True)
    el2 = jnp.where(lane == j1, -jnp.inf, el)
    t2 = jnp.max(el2, axis=1, keepdims=True)
    j2 = jnp.min(jnp.where(el2 == t2, lane, big), axis=1, keepdims=True)
    e2 = jnp.exp(t2 - t1)
    den = 1.0 + e2
    gate1 = g_w / den
    gate2 = g_w * e2 / den

    o1 = lane == j1
    o2 = lane == j2
    onehot = jnp.where(o1 | o2, 1.0, 0.0).astype(BF16)
    rr = lax.broadcasted_iota(jnp.int32, (tm, tm), 0)
    cc = lax.broadcasted_iota(jnp.int32, (tm, tm), 1)
    lower = jnp.where(rr > cc, 1.0, 0.0).astype(BF16)
    pfx = jnp.dot(lower, onehot, preferred_element_type=F32)
    cnt_tile = jnp.sum(onehot.astype(F32), axis=0, keepdims=True)
    groups = jnp.floor((cnt_tile + (ROW_ALIGN - 1)) * (1.0 / ROW_ALIGN))
    er = lax.broadcasted_iota(jnp.int32, (LANES, LANES), 0)
    ec = lax.broadcasted_iota(jnp.int32, (LANES, LANES), 1)
    before = jnp.where(er < ec, 1.0, 0.0).astype(BF16)
    cbase = ROW_ALIGN * jnp.dot(jnp.broadcast_to(groups, (8, LANES)).astype(BF16), before,
                                preferred_element_type=F32)[0:1]
    at = pfx + cbase
    pos1 = jnp.sum(jnp.where(o1, at, 0.0), axis=1, keepdims=True)
    pos2 = jnp.sum(jnp.where(o2, at, 0.0), axis=1, keepdims=True)
    rt_ref[...] = jnp.where(lane_i == 0, gate1,
                            jnp.where(lane_i == 1, gate2,
                                      jnp.where(lane_i == 2, pos1,
                                                jnp.where(lane_i == 3, pos2, 0.0))))
    c_old = c_ref[...]
    c_ref[...] = c_old + groups * (ROW_ALIGN * live)
    row8 = lax.broadcasted_iota(jnp.int32, (8, LANES), 0)
    ti_ref[...] = jnp.where(row8 == 0, cnt_tile, jnp.where(row8 == 1, c_old, 0.0))


def _outproj(x2, mixd, mixs, wo, g2, wr, br):
    n, d = x2.shape
    nt = n // TM

    def proj_tile(i):
        return (jnp.minimum(i, nt - 1), 0)

    def route_tile(i):
        return (jnp.maximum(i - 1, 0), 0)

    return pl.pallas_call(
        _outproj_kernel,
        out_shape=(jax.ShapeDtypeStruct((n, d), F32),
                   jax.ShapeDtypeStruct((n, d), BF16),
                   jax.ShapeDtypeStruct((n, LANES), F32),
                   jax.ShapeDtypeStruct((nt * 8, LANES), F32),
                   jax.ShapeDtypeStruct((8, LANES), F32)),
        grid=(nt + 1,),
        in_specs=[
            pl.BlockSpec((TM, d), proj_tile),
            pl.BlockSpec((TM, mixd.shape[1]), proj_tile),
            pl.BlockSpec((TM, mixs.shape[1]), proj_tile),
            pl.BlockSpec(wo.shape, lambda i: (0, 0)),
            pl.BlockSpec(g2.shape, lambda i: (0, 0)),
            pl.BlockSpec(wr.shape, lambda i: (0, 0)),
            pl.BlockSpec(br.shape, lambda i: (0, 0)),
        ],
        out_specs=(pl.BlockSpec((TM, d), proj_tile),
                   pl.BlockSpec((TM, d), proj_tile),
                   pl.BlockSpec((TM, LANES), route_tile),
                   pl.BlockSpec((8, LANES), route_tile),
                   pl.BlockSpec((8, LANES), lambda i: (0, 0))),
        scratch_shapes=[pltpu.VMEM((8, LANES), F32), pltpu.VMEM((TM, LANES), F32)],
        compiler_params=_cparams(("arbitrary",)),
        name="outproj_router",
    )(x2, mixd, mixs, wo, g2, wr, br)


def _for_each_chunk(runs_ref, fn):
    big = CHUNK_ROWS[0]

    def per_expert(e, sorted_row, priority):
        start = runs_ref[0, 0, e]
        groups = runs_ref[0, 0, N_EXPERTS + e]
        whole = groups // (big // ROW_ALIGN)

        def per_chunk(c, carry):
            fn(pl.multiple_of(start + c * big, ROW_ALIGN), pl.multiple_of(sorted_row + c * big, ROW_ALIGN),
               big, priority)
            return carry
        lax.fori_loop(0, whole, per_chunk, 0)

        done = whole * big
        for rows in CHUNK_ROWS[1:]:
            has = (groups // (rows // ROW_ALIGN)) % 2

            @pl.when(has == 1)
            def _(done=done, rows=rows):
                fn(pl.multiple_of(start + done, ROW_ALIGN), pl.multiple_of(sorted_row + done, ROW_ALIGN),
                   rows, priority)
            done = done + has * rows
        return sorted_row + groups * ROW_ALIGN

    def expert_pair(e2, sorted_row):
        return per_expert(2 * e2 + 1, per_expert(2 * e2, sorted_row, 0), 1)
    lax.fori_loop(0, N_EXPERTS // 2, expert_pair, 0)


def _wait_chunks(runs_ref, make_copy):
    for k, rows in enumerate(CHUNK_ROWS):
        def body(c, carry, rows=rows):
            make_copy(rows).wait()
            return carry
        lax.fori_loop(0, runs_ref[0, 0, 2 * N_EXPERTS + k], body, 0)


def _dispatch_kernel(zf_ref, cur_ref, prv_ref, hb_ref, rt_ref, xs_ref, sbuf, zbuf, sem, zsem):
    i = pl.program_id(0)
    nt = pl.num_programs(0)
    slot = i % 2
    tm, d = hb_ref.shape

    def for_zero_blocks(kind, fn):
        def body(b, carry):
            @pl.when(zf_ref[b] == kind)
            def _():
                fn(pltpu.make_async_copy(zbuf, xs_ref.at[pl.ds(pl.multiple_of(b * EB, EB), EB)],
                                         zsem.at[kind - 1]))
            return carry
        lax.fori_loop(0, zf_ref.shape[0], body, 0)

    @pl.when(i == 0)
    def _():
        zbuf[...] = jnp.zeros(zbuf.shape, zbuf.dtype)
        for_zero_blocks(1, lambda c: c.start())
        for_zero_blocks(2, lambda c: c.start())
        for_zero_blocks(1, lambda c: c.wait())

    pos_t = jnp.transpose(rt_ref[...])
    srow = lax.broadcasted_iota(jnp.int32, (SROWS, tm), 0).astype(F32)
    sel = jnp.where(srow == pos_t[2:3, :], 1.0, jnp.where(srow == pos_t[3:4, :], 1.0, 0.0)).astype(BF16)
    srt = jnp.dot(sel, hb_ref[...], preferred_element_type=F32)
    bits = pltpu.bitcast(srt, jnp.uint32)
    sbuf[slot] = (bits[:, d // 2:] & jnp.uint32(0xFFFF0000)) | (bits[:, :d // 2] >> 16)

    def chunk_copy(run_row, sorted_row, rows, sl):
        return pltpu.make_async_copy(sbuf.at[sl, pl.ds(sorted_row, rows)], xs_ref.at[pl.ds(run_row, rows)],
                                     sem.at[sl])

    _for_each_chunk(cur_ref, lambda run_row, sorted_row, rows, priority:
                    chunk_copy(run_row, sorted_row, rows, slot).start(priority=priority))

    @pl.when(i > 0)
    def _():
        _wait_chunks(prv_ref, lambda rows: chunk_copy(0, 0, rows, 1 - slot))

    @pl.when(i == nt - 1)
    def _():
        _wait_chunks(cur_ref, lambda rows: chunk_copy(0, 0, rows, slot))
        for_zero_blocks(2, lambda c: c.wait())


def _dispatch(zero_blocks, runs, hb, rt, n_rows):
    n, d = hb.shape
    return pl.pallas_call(
        _dispatch_kernel,
        out_shape=jax.ShapeDtypeStruct((n_rows, d // 2), jnp.uint32),
        grid_spec=pltpu.PrefetchScalarGridSpec(
            num_scalar_prefetch=1,
            grid=(n // TM,),
            in_specs=[
                pl.BlockSpec((1, 1, LANES), lambda i, zf: (i, 0, 0), memory_space=pltpu.SMEM),
                pl.BlockSpec((1, 1, LANES), lambda i, zf: (jnp.maximum(i - 1, 0), 0, 0), memory_space=pltpu.SMEM),
                pl.BlockSpec((TM, d), lambda i, zf: (i, 0)),
                pl.BlockSpec((TM, LANES), lambda i, zf: (i, 0)),
            ],
            out_specs=pl.BlockSpec(memory_space=pl.ANY),
            scratch_shapes=[pltpu.VMEM((2, SROWS, d // 2), jnp.uint32), pltpu.VMEM((EB, d // 2), jnp.uint32),
                            pltpu.SemaphoreType.DMA((2,)), pltpu.SemaphoreType.DMA((2,))],
        ),
        compiler_params=_cparams(("arbitrary",)),
        name="dispatch",
    )(zero_blocks, runs, runs, hb, rt)


def _experts_kernel(be_ref, na_ref, nxt_ref, xs_ref, wg_hbm, wu_hbm, wd_hbm, ys_ref,
                    wgf, wuf, wdf, wgb, wub, wdb, sem):
    b = pl.program_id(0)

    def weight_copies(e):
        return (pltpu.make_async_copy(wg_hbm.at[e], wgf, sem.at[0]),
                pltpu.make_async_copy(wu_hbm.at[e], wuf, sem.at[1]),
                pltpu.make_async_copy(wd_hbm.at[e], wdf, sem.at[2]))

    @pl.when(b == 0)
    def _():
        for c in weight_copies(be_ref[0]):
            c.start()

    @pl.when(b < na_ref[0])
    def _():
        e = be_ref[b]
        changed = jnp.logical_or(b == 0, be_ref[jnp.maximum(b - 1, 0)] != e)

        @pl.when(changed)
        def _load():
            for c in weight_copies(e):
                c.wait()
            wgb[...] = wgf[...].astype(BF16)
            wub[...] = wuf[...].astype(BF16)
            wdb[...] = wdf[...].astype(BF16)
            nxt = nxt_ref[e]

            @pl.when(nxt >= 0)
            def _():
                for c in weight_copies(nxt):
                    c.start()

        w = xs_ref[...]
        x_lo = pltpu.bitcast(w << 16, F32).astype(BF16)
        x_hi = pltpu.bitcast(w & jnp.uint32(0xFFFF0000), F32).astype(BF16)
        dh = w.shape[1]
        g = (jnp.dot(x_lo, wgb[:dh, :], preferred_element_type=F32)
             + jnp.dot(x_hi, wgb[dh:, :], preferred_element_type=F32))
        u = (jnp.dot(x_lo, wub[:dh, :], preferred_element_type=F32)
             + jnp.dot(x_hi, wub[dh:, :], preferred_element_type=F32))
        hdn = g * (1.0 / (1.0 + jnp.exp(-g))) * u
        y = jnp.dot(hdn.astype(BF16), wdb[...], preferred_element_type=F32)
        bits = pltpu.bitcast(y.astype(BF16).astype(F32), jnp.uint32)
        ys_ref[...] = (bits[:, dh:] & jnp.uint32(0xFFFF0000)) | (bits[:, :dh] >> 16)

    @pl.when(b >= na_ref[0])
    def _():
        ys_ref[...] = jnp.zeros(ys_ref.shape, ys_ref.dtype)


def _experts(blk_e, n_act, nxt_e, xs, w_gate, w_up, w_down):
    p, dh = xs.shape
    d = 2 * dh
    de = w_gate.shape[2]

    def row_map(b, be, na, nx):
        return (jnp.minimum(b, na[0] - 1), 0)

    return pl.pallas_call(
        _experts_kernel,
        out_shape=jax.ShapeDtypeStruct((p, dh), jnp.uint32),
        grid_spec=pltpu.PrefetchScalarGridSpec(
            num_scalar_prefetch=3,
            grid=(p // EB,),
            in_specs=[
                pl.BlockSpec((EB, dh), row_map),
                pl.BlockSpec(memory_space=pl.ANY),
                pl.BlockSpec(memory_space=pl.ANY),
                pl.BlockSpec(memory_space=pl.ANY),
            ],
            out_specs=pl.BlockSpec((EB, dh), lambda b, be, na, nx: (b, 0)),
            scratch_shapes=[pltpu.VMEM((d, de), F32), pltpu.VMEM((d, de), F32), pltpu.VMEM((de, d), F32),
                            pltpu.VMEM((d, de), BF16), pltpu.VMEM((d, de), BF16), pltpu.VMEM((de, d), BF16),
                            pltpu.SemaphoreType.DMA((3,))],
        ),
        compiler_params=_cparams(("arbitrary",)),
        name="experts",
    )(blk_e, n_act, nxt_e, xs, w_gate, w_up, w_down)


def _combine_kernel(cur_ref, nxt_ref, ys_ref, h_ref, rt_ref, o_ref, ybuf, sem):
    i = pl.program_id(0)
    nt = pl.num_programs(0)
    slot = i % 2
    tm = h_ref.shape[0]

    def chunk_copy(run_row, sorted_row, rows, sl):
        return pltpu.make_async_copy(ys_ref.at[pl.ds(run_row, rows)], ybuf.at[sl, pl.ds(sorted_row, rows)],
                                     sem.at[sl])

    @pl.when(i == 0)
    def _():
        ybuf[...] = jnp.zeros(ybuf.shape, ybuf.dtype)
        _for_each_chunk(cur_ref, lambda run_row, sorted_row, rows, priority:
                        chunk_copy(run_row, sorted_row, rows, 0).start(priority=priority))

    @pl.when(i + 1 < nt)
    def _():
        _for_each_chunk(nxt_ref, lambda run_row, sorted_row, rows, priority:
                        chunk_copy(run_row, sorted_row, rows, 1 - slot).start(priority=priority))

    _wait_chunks(cur_ref, lambda rows: chunk_copy(0, 0, rows, slot))

    rt = rt_ref[...]
    w = ybuf[slot]
    dh = w.shape[1]
    y_lo = pltpu.bitcast(w << 16, F32).astype(BF16)
    y_hi = pltpu.bitcast(w & jnp.uint32(0xFFFF0000), F32).astype(BF16)
    col = lax.broadcasted_iota(jnp.int32, (tm, SROWS), 1).astype(F32)
    wsel = jnp.where(col == rt[:, 2:3], rt[:, 0:1], jnp.where(col == rt[:, 3:4], rt[:, 1:2], 0.0)).astype(BF16)
    for half, yb in ((slice(0, dh), y_lo), (slice(dh, 2 * dh), y_hi)):
        o_ref[:, half] = h_ref[:, half] + jnp.dot(wsel, yb, preferred_element_type=F32)


def _combine(runs, ys, h1, rt):
    n, d = h1.shape
    nt = n // TM
    return pl.pallas_call(
        _combine_kernel,
        out_shape=jax.ShapeDtypeStruct((n, d), F32),
        grid=(nt,),
        in_specs=[
            pl.BlockSpec((1, 1, LANES), lambda i: (i, 0, 0), memory_space=pltpu.SMEM),
            pl.BlockSpec((1, 1, LANES), lambda i: (jnp.minimum(i + 1, nt - 1), 0, 0), memory_space=pltpu.SMEM),
            pl.BlockSpec(memory_space=pl.ANY),
            pl.BlockSpec((TM, d), lambda i: (i, 0)),
            pl.BlockSpec((TM, LANES), lambda i: (i, 0)),
        ],
        out_specs=pl.BlockSpec((TM, d), lambda i: (i, 0)),
        scratch_shapes=[pltpu.VMEM((2, SROWS, d // 2), jnp.uint32), pltpu.SemaphoreType.DMA((2,))],
        compiler_params=_cparams(("arbitrary",)),
        name="combine",
    )(runs, runs, ys, h1, rt)


def kernel(x, meta_tokens, rel_bias, norm1_gain, w_in, diff_q_gain, diff_k_gain, lam_q1, lam_k1, lam_q2, lam_k2, diff_subln_gain, swa_q_gain, swa_k_gain, swa_sinks, w_out, norm2_gain, w_group, b_group, w_router, b_router, w_gate, w_up, w_down):
    batch, seq, d = x.shape
    depth = w_in.shape[0]
    n = batch * seq
    assert seq % TQ == 0 and n % TM == 0 and n % TP == 0 and d == 1024
    assert meta_tokens.shape[0] == N_META
    assert depth == 1, "the meta-token rows of the residual stream are not carried across layers"

    h = x.reshape(n, d)
    dblk, bm0, bt = _bias_tables(rel_bias, TQ)
    scale = HEAD_DIM ** -0.5
    bd = jnp.asarray(np.kron(np.eye(MXU_DIM // HEAD_DIM), np.full((HEAD_DIM, HEAD_DIM), 1.0 / HEAD_DIM)), BF16)
    ones = jnp.ones((HEAD_DIM,), F32)
    lower_pad = N_EXPERTS + N_GROUPS

    for layer in range(depth):
        lambda_init = 0.8 - 0.6 * math.exp(-0.3 * layer)
        w_cat = w_in[layer].astype(BF16)
        gain = jnp.concatenate([
            jnp.tile(diff_q_gain[layer] * (scale * LOG2E), 2 * N_DIFF_HEADS),
            jnp.tile(diff_k_gain[layer], 2 * N_DIFF_HEADS),
            jnp.tile(ones, 2 * N_DIFF_HEADS),
            jnp.tile(swa_q_gain[layer] * scale, N_SWA_HEADS),
            jnp.tile(swa_k_gain[layer], N_SWA_KV),
            jnp.tile(ones, N_SWA_KV)]).reshape(1, C_END).astype(F32)
        nmask = np.zeros((1, C_END), np.float32)
        nmask[:, C_DQ:C_DV] = 1.0
        nmask[:, C_SQ:C_SV] = 1.0
        nmask = jnp.asarray(nmask)
        g1 = norm1_gain[layer].reshape(1, d).astype(F32)

        qkv = _proj(h, g1, w_cat, bd, gain, nmask, TP)
        qkv_meta = _proj(meta_tokens.astype(F32), g1, w_cat, bd, gain, nmask, N_META)
        meta_pad = jnp.pad(qkv_meta, ((0, TQ - N_META), (0, 0)))

        lamv = jnp.pad(jnp.stack([lam_q1[layer], lam_k1[layer], lam_q2[layer], lam_k2[layer]]).astype(F32),
                       ((0, 4), (0, LANES - HEAD_DIM)))
        mixd = _diff_attention(qkv, meta_pad[:, C_DK:C_DV], meta_pad[:, C_DV:C_SQ], dblk, bm0, lamv,
                               diff_subln_gain[layer].reshape(1, LANES).astype(F32), batch, seq, lambda_init)
        mixs = _swa_attention(swa_sinks[layer].astype(F32), qkv, meta_pad[:BLOCK, C_SK:C_SV],
                              meta_pad[:BLOCK, C_SV:C_END], jnp.swapaxes(bt, -1, -2), batch, seq)

        wr = jnp.pad(jnp.concatenate([w_router[layer], w_group[layer]], axis=1),
                     ((0, 0), (0, LANES - lower_pad))).astype(BF16)
        br = jnp.pad(jnp.concatenate([b_router[layer], b_group[layer]]), (0, LANES - lower_pad)).reshape(1, LANES)
        h1, hb, rt, tinfo, cnt = _outproj(h, mixd, mixs, w_out[layer].astype(BF16),
                                          norm2_gain[layer].reshape(1, d).astype(F32), wr, br.astype(F32))

        nt = n // TM
        counts = cnt[0, :N_EXPERTS].astype(jnp.int32)
        nblk_e = (counts + EB - 1) // EB
        blk_end = jnp.cumsum(nblk_e)
        pstart = ((blk_end - nblk_e) * EB).astype(jnp.int32)
        n_blocks = -(-(2 * n + nt * N_EXPERTS * (ROW_ALIGN - 1) + N_EXPERTS * (EB - 1)) // EB)
        blk_ids = jnp.arange(n_blocks)
        blk_e = jnp.minimum(jnp.sum(blk_end[None, :] <= blk_ids[:, None], axis=1), N_EXPERTS - 1).astype(jnp.int32)
        n_act = blk_end[-1:].astype(jnp.int32)
        is_last = jnp.any((blk_end[None, :] == blk_ids[:, None] + 1) & (nblk_e[None, :] > 0), axis=1)
        zero_blocks = jnp.where(blk_ids >= n_act[0], 2, jnp.where(is_last, 1, 0)).astype(jnp.int32)
        ti = tinfo.reshape(nt, 8, LANES)
        run_len = ti[:, 0, :N_EXPERTS].astype(jnp.int32)
        run_start = pstart[None, :] + ti[:, 1, :N_EXPERTS].astype(jnp.int32)
        run_groups = (run_len + ROW_ALIGN - 1) // ROW_ALIGN
        n_copies = [jnp.sum(run_groups // (CHUNK_ROWS[0] // ROW_ALIGN), axis=1, keepdims=True)]
        n_copies += [jnp.sum((run_groups // (rows // ROW_ALIGN)) % 2, axis=1, keepdims=True)
                     for rows in CHUNK_ROWS[1:]]
        runs = jnp.concatenate([run_start, run_groups] + n_copies
                               + [jnp.zeros((nt, LANES - 2 * N_EXPERTS - len(CHUNK_ROWS)), jnp.int32)],
                               axis=1).reshape(nt, 1, LANES)

        xs = _dispatch(zero_blocks, runs, hb, rt, n_blocks * EB)
        own = jnp.where(nblk_e > 0, jnp.arange(N_EXPERTS), N_EXPERTS)
        later = jnp.concatenate([lax.cummin(own[::-1])[::-1][1:], jnp.full((1,), N_EXPERTS)])
        nxt_e = jnp.where(later < N_EXPERTS, later, -1).astype(jnp.int32)
        ys = _experts(blk_e, n_act, nxt_e, xs, w_gate[layer], w_up[layer], w_down[layer])
        h = _combine(runs, ys, h1, rt)
    return h.reshape(batch, seq, d)
```

```python
import functools
import math

import numpy as np
import jax
import jax.numpy as jnp
from jax import lax
from jax.experimental import pallas as pl
from jax.experimental.pallas import tpu as pltpu

F32 = jnp.float32
BF16 = jnp.bfloat16

HEAD_DIM = 64
N_DIFF_HEADS = 4
N_SWA_HEADS = 8
N_SWA_KV = 2
BLOCK = 128
N_META = 16
N_BUCKETS = 32
MAX_DISTANCE = 128
N_GROUPS = 4
EXPERTS_PER_GROUP = 8
N_EXPERTS = N_GROUPS * EXPERTS_PER_GROUP
D_EXPERT = 512
EPS = 1e-6
NEG = -1e30
LOG2E = math.log2(math.e)

LANES = 128
MXU_DIM = 256
VMEM_LIMIT = 48 * 1024 * 1024

TP = 512
TM = 512
TQ = 256
ONES_ROWS = 16
EB = 512
ROW_ALIGN = 8
CHUNK_ROWS = (32, 16, 8)
SROWS = -(-(2 * TM + N_EXPERTS * (ROW_ALIGN - 1)) // MXU_DIM) * MXU_DIM

C_DQ, C_DK, C_DV, C_SQ, C_SK, C_SV, C_END = 0, 512, 1024, 1536, 2048, 2176, 2304
NORM_GROUPS = (0, 1, 2, 3, 6, 7, 8)


def _cparams(sem):
    return pltpu.CompilerParams(dimension_semantics=sem, vmem_limit_bytes=VMEM_LIMIT)


def _t5_bucket_np(dist):
    n = np.maximum(dist, 0)
    max_exact = N_BUCKETS // 2
    nf = np.maximum(n, 1).astype(np.float32)
    large = max_exact + (np.log(nf / np.float32(max_exact)) / np.float32(math.log(MAX_DISTANCE / max_exact))
                         * np.float32(N_BUCKETS - max_exact)).astype(np.int32)
    large = np.minimum(large, N_BUCKETS - 1)
    return np.where(n < max_exact, n, large)


def _bias_tables(rel_bias, tq):
    nd = 2 * BLOCK
    buckets = _t5_bucket_np(np.arange(nd))
    assert (buckets[MAX_DISTANCE:] == N_BUCKETS - 1).all()
    rb = rel_bias.astype(F32)
    r = np.arange(BLOCK)[:, None]
    c = np.arange(BLOCK)[None, :]
    d_own = r - c
    d_prev = BLOCK + r - c
    far = rb[N_BUCKETS - 1]

    def take(dist):
        idx = jnp.asarray(buckets[np.clip(dist, 0, nd - 1)], jnp.int32)[None]
        out = jnp.zeros((rb.shape[1],) + dist.shape, F32)
        for b in range(N_BUCKETS):
            out = jnp.where(idx == b, rb[b].reshape((-1,) + (1,) * dist.ndim), out)
        return out

    hd = slice(0, N_DIFF_HEADS)
    far_d = far[hd][:, None, None]
    d0 = jnp.where(d_own[None] >= 0, take(d_own)[hd] - far_d, NEG)
    d1 = take(d_prev)[hd] - far_d
    dblk = jnp.stack([d0, d1], axis=1)
    rq = np.arange(tq)[:, None]
    cm = np.arange(LANES)[None, :]
    d_meta = N_META + rq - cm
    bm0 = jnp.where((cm < N_META)[None], take(d_meta)[hd] - far_d, NEG)

    hs = slice(N_DIFF_HEADS, N_DIFF_HEADS + N_SWA_HEADS)
    far_s = far[hs][:, None, None]
    d_meta_s = N_META + r - cm
    meta_first = jnp.where((cm < N_META)[None], take(d_meta_s)[hs], NEG)
    meta_rest = jnp.where((cm < N_META)[None], jnp.broadcast_to(far_s, (N_SWA_HEADS, BLOCK, LANES)), NEG)
    prev_rest = jnp.where((c > r)[None], take(d_prev)[hs], NEG)
    prev_first = jnp.full((N_SWA_HEADS, BLOCK, BLOCK), NEG, F32)
    own = jnp.where((d_own >= 0)[None], take(d_own)[hs], NEG)
    bt = jnp.stack([jnp.concatenate([meta_first, prev_first, own], axis=-1),
                    jnp.concatenate([meta_rest, prev_rest, own], axis=-1)], axis=0)
    return dblk.astype(F32), bm0.astype(F32), bt.astype(F32)


def _proj_kernel(x_ref, g1_ref, w_ref, bd_ref, gain_ref, nmask_ref, o_ref):
    x = x_ref[...]
    a = x * lax.rsqrt(jnp.mean(x * x, axis=-1, keepdims=True) + EPS) * g1_ref[...]
    p = jnp.dot(a.astype(BF16), w_ref[...], preferred_element_type=F32)
    bd = bd_ref[...]
    for j in range(C_END // MXU_DIM):
        sl = slice(j * MXU_DIM, (j + 1) * MXU_DIM)
        pj = p[:, sl]
        if j in NORM_GROUPS:
            ms = jnp.dot((pj * pj).astype(BF16), bd, preferred_element_type=F32)
            pj = jnp.where(nmask_ref[:, sl] != 0.0, pj * lax.rsqrt(ms + EPS) * gain_ref[:, sl], pj)
        o_ref[:, sl] = pj.astype(BF16)


def _proj(x2, g1, w, bd, gain, nmask, tm):
    n = x2.shape[0]
    return pl.pallas_call(
        _proj_kernel,
        out_shape=jax.ShapeDtypeStruct((n, C_END), BF16),
        grid=(n // tm,),
        in_specs=[
            pl.BlockSpec((tm, x2.shape[1]), lambda i: (i, 0)),
            pl.BlockSpec(g1.shape, lambda i: (0, 0)),
            pl.BlockSpec(w.shape, lambda i: (0, 0)),
            pl.BlockSpec(bd.shape, lambda i: (0, 0)),
            pl.BlockSpec(gain.shape, lambda i: (0, 0)),
            pl.BlockSpec(nmask.shape, lambda i: (0, 0)),
        ],
        out_specs=pl.BlockSpec((tm, C_END), lambda i: (i, 0)),
        compiler_params=_cparams(("parallel",)),
        name="proj",
    )(x2, g1, w, bd, gain, nmask)


def _diff_kernel(qi_tab, t_tab, q_ref, k_ref, v_ref, km_ref, vm_ref, d_ref, bm0_ref, lamv_ref, gain_ref, o_ref,
                 bias_ref, qs_ref, kt_ref, vt_ref, s_buf, p_buf, a_buf, m_ref, acc_ref, *,
                 lambda_init, n_steps, n_far):
    tq = TQ
    nq = q_ref.shape[0] // tq
    nb = tq // BLOCK
    BIAS_NONE, BIAS_LEFT, BIAS_DIAG, BIAS_META, BIAS_META0 = 0, 1, 2, 3, 4

    d0 = d_ref[0, 0] * LOG2E
    d1 = d_ref[0, 1] * LOG2E
    zeros = jnp.zeros((BLOCK, BLOCK), F32)
    bias_ref[BIAS_NONE] = jnp.zeros((tq, tq), F32)
    for a in range(nb):
        for b in range(nb):
            rs, cs = slice(a * BLOCK, (a + 1) * BLOCK), slice(b * BLOCK, (b + 1) * BLOCK)
            if a == b:
                blk = d0
            elif b == a + 1:
                blk = d1
            elif b > a:
                blk = zeros
            else:
                blk = jnp.full((BLOCK, BLOCK), NEG, F32)
            bias_ref[BIAS_DIAG, rs, cs] = blk
            bias_ref[BIAS_LEFT, rs, cs] = d1 if (b == 0 and a == nb - 1) else zeros
    row_m = lax.broadcasted_iota(jnp.int32, (tq, tq), 0)
    bias_ref[BIAS_META] = jnp.where(row_m < N_META, 0.0, NEG).astype(F32)
    bias_ref[BIAS_META0, :LANES, :] = bm0_ref[0] * LOG2E
    bias_ref[BIAS_META0, LANES:, :] = jnp.full((tq - LANES, tq), NEG, F32)

    lane = lax.broadcasted_iota(jnp.int32, (tq, LANES), 1)
    for i in range(nq):
        rows = slice(i * tq, (i + 1) * tq)
        q = q_ref[rows, :].astype(F32)
        qs_ref[i] = jnp.transpose(jnp.concatenate([jnp.where(lane < HEAD_DIM, q, 0.0),
                                                   jnp.where(lane >= HEAD_DIM, q, 0.0)], axis=0)).astype(BF16)
        vt_ref[i, :LANES, :] = jnp.transpose(v_ref[rows, :].astype(F32)).astype(BF16)
        kt_ref[i] = k_ref[rows, :]
    vt_ref[nq, :LANES, :] = jnp.transpose(vm_ref[...].astype(F32)).astype(BF16)
    kt_ref[nq] = km_ref[...]
    vt_ref[:, LANES:, :] = jnp.ones((nq + 1, ONES_ROWS, tq), BF16)
    acc_ref[...] = jnp.zeros(acc_ref.shape, F32)
    m_ref[...] = jnp.full(m_ref.shape, NEG, F32)
    lv = lamv_ref[...]
    lam = (jnp.exp(jnp.sum(lv[0:1] * lv[1:2], axis=-1, keepdims=True))
           - jnp.exp(jnp.sum(lv[2:3] * lv[3:4], axis=-1, keepdims=True)) + lambda_init)

    def stage_a(n, slot, biased):
        qi, t = qi_tab[n], t_tab[n]
        kt = kt_ref[jnp.where(t == 0, nq, t - 1)]
        s = jnp.dot(kt, qs_ref[qi], preferred_element_type=F32)
        if biased:
            which = jnp.where(t == 0, jnp.where(qi == 0, BIAS_META0, BIAS_META),
                              jnp.where(t == qi + 1, BIAS_DIAG, jnp.where(t == qi, BIAS_LEFT, BIAS_NONE)))
            s = s + jnp.tile(bias_ref[which], (1, 2))
        s_buf[slot] = s

    def stage_b(n, slot):
        qi = qi_tab[n]
        s = s_buf[slot]
        m_prev = m_ref[qi]
        m_new = jnp.maximum(m_prev, jnp.max(s, axis=0, keepdims=True))
        a_buf[slot] = jnp.exp2(m_prev - m_new)
        p_buf[slot] = jnp.exp2(s - m_new[0:1]).astype(BF16)
        m_ref[qi] = m_new

    def stage_c(n, slot):
        qi, t = qi_tab[n], t_tab[n]
        vt = vt_ref[jnp.where(t == 0, nq, t - 1)]
        pv = jnp.dot(vt, p_buf[slot], preferred_element_type=F32)
        acc_ref[qi] = a_buf[slot][0:1] * acc_ref[qi] + pv

    def pipeline(base, count, biased):
        LEAD, SLOTS, unroll = 2, 3, 12
        assert unroll % SLOTS == 0 and count > 2 * LEAD
        for j in range(2 * LEAD):
            stage_a(base + j, j % SLOTS, biased)
            if j >= LEAD:
                stage_b(base + j - LEAD, (j - LEAD) % SLOTS)

        def steps(n, count):
            for j in range(count):
                stage_a(base + n + j + 2 * LEAD, (j + 2 * LEAD) % SLOTS, biased)
                stage_b(base + n + j + LEAD, (j + LEAD) % SLOTS)
                stage_c(base + n + j, j % SLOTS)

        n_steady = count - 2 * LEAD
        n_blocks = n_steady // unroll

        def block(k, carry):
            steps(unroll * k, unroll)
            return carry
        lax.fori_loop(0, n_blocks, block, 0)
        steps(unroll * n_blocks, n_steady - unroll * n_blocks)
        for j in range(n_steady, count):
            if j + LEAD < count:
                stage_b(base + j + LEAD, (j + LEAD) % SLOTS)
            stage_c(base + j, j % SLOTS)

    if n_far > 0:
        pipeline(0, n_far, False)
    pipeline(n_far, n_steps - n_far, True)

    for i in range(nq):
        acc = acc_ref[i]
        o = acc[:LANES] * (1.0 / acc[LANES:LANES + 1])
        d = o[:, :tq] - lam * o[:, tq:]
        y = d * lax.rsqrt(jnp.mean(d * d, axis=0, keepdims=True) + EPS) * jnp.tile(gain_ref[...], (1, tq // LANES))
        o_ref[i * tq:(i + 1) * tq, :] = jnp.transpose(y * (1.0 - lambda_init)).astype(BF16)


def _diff_attention(qkv, km, vm, dblk, bm0, lamv, gain, batch, seq, lambda_init):
    nq = seq // TQ
    far = [(qi, t) for qi in range(nq) for t in range(1, qi)]
    near = [(qi, t) for qi in range(nq) for t in range(qi + 2) if not 1 <= t < qi]
    if len(far) <= 4:
        far, near = [], [(qi, t) for qi in range(nq) for t in range(qi + 2)]
    steps = far + near
    qi_tab = jnp.asarray([s[0] for s in steps], jnp.int32)
    t_tab = jnp.asarray([s[1] for s in steps], jnp.int32)
    kern = functools.partial(_diff_kernel, lambda_init=lambda_init, n_steps=len(steps), n_far=len(far))
    return pl.pallas_call(
        kern,
        out_shape=jax.ShapeDtypeStruct((batch * seq, N_DIFF_HEADS * LANES), BF16),
        grid_spec=pltpu.PrefetchScalarGridSpec(
            num_scalar_prefetch=2,
            grid=(batch, N_DIFF_HEADS),
            in_specs=[
                pl.BlockSpec((seq, LANES), lambda b, h, *_: (b, C_DQ // LANES + h)),
                pl.BlockSpec((seq, LANES), lambda b, h, *_: (b, C_DK // LANES + h)),
                pl.BlockSpec((seq, LANES), lambda b, h, *_: (b, C_DV // LANES + h)),
                pl.BlockSpec((TQ, LANES), lambda b, h, *_: (0, h)),
                pl.BlockSpec((TQ, LANES), lambda b, h, *_: (0, h)),
                pl.BlockSpec((1, 2, BLOCK, BLOCK), lambda b, h, *_: (h, 0, 0, 0)),
                pl.BlockSpec((1, LANES, TQ), lambda b, h, *_: (h, 0, 0)),
                pl.BlockSpec(lamv.shape, lambda b, h, *_: (0, 0)),
                pl.BlockSpec((LANES, LANES), lambda b, h, *_: (0, 0)),
            ],
            out_specs=pl.BlockSpec((seq, LANES), lambda b, h, *_: (b, h)),
            scratch_shapes=[
                pltpu.VMEM((5, TQ, TQ), F32),
                pltpu.VMEM((nq, LANES, 2 * TQ), BF16),
                pltpu.VMEM((nq + 1, TQ, LANES), BF16),
                pltpu.VMEM((nq + 1, LANES + ONES_ROWS, TQ), BF16),
                pltpu.VMEM((3, TQ, 2 * TQ), F32),
                pltpu.VMEM((3, TQ, 2 * TQ), BF16),
                pltpu.VMEM((3, 8, 2 * TQ), F32),
                pltpu.VMEM((nq, 8, 2 * TQ), F32),
                pltpu.VMEM((nq, LANES + ONES_ROWS, 2 * TQ), F32),
            ],
        ),
        compiler_params=_cparams(("parallel", "parallel")),
        name="diff_attention",
    )(qi_tab, t_tab, qkv, qkv, qkv, km, vm, jnp.swapaxes(dblk, -1, -2), jnp.swapaxes(bm0, -1, -2), lamv,
      jnp.broadcast_to(gain.reshape(LANES, 1), (LANES, LANES)))


def _swa_kernel(sink_ref, q_ref, k_ref, v_ref, km_ref, vm_ref, bt_ref, o_ref, kd_ref, vt_ref,
                s_scr, p_scr, inv_scr):
    nkb = k_ref.shape[0] // BLOCK
    lane = lax.broadcasted_iota(jnp.int32, (BLOCK, LANES), 1)
    pairs = [(g, u) for g in range(N_SWA_KV) for u in range(2)]

    def both_halves(k):
        k0, k1 = k[:, :HEAD_DIM], k[:, HEAD_DIM:]
        return jnp.concatenate([k0, k0, k1, k1], axis=1)

    def prepare(j, carry):
        rows = pl.ds(pl.multiple_of(j * BLOCK, BLOCK), BLOCK)
        kd_ref[j] = both_halves(k_ref[rows, :])
        vt_ref[j] = jnp.transpose(v_ref[rows, :].astype(F32)).astype(BF16)
        return carry
    lax.fori_loop(0, nkb, prepare, 0)
    kd_ref[nkb] = both_halves(km_ref[...])
    vt_ref[nkb] = jnp.transpose(vm_ref[...].astype(F32)).astype(BF16)

    def scores(n, slot):
        first = jnp.where(n == 0, 0, 1)
        prev = jnp.maximum(n - 1, 0)
        r_q = pl.multiple_of(n * BLOCK, BLOCK)
        for c, (g, u) in enumerate(pairs):
            ks = slice(g * LANES, (g + 1) * LANES)
            kcat = jnp.concatenate([kd_ref[nkb, :, ks], kd_ref[prev, :, ks], kd_ref[n, :, ks]], axis=0)
            h0 = 4 * g + 2 * u
            qp = q_ref[pl.ds(r_q, BLOCK), (2 * g + u) * LANES:(2 * g + u + 1) * LANES].astype(F32)
            qs = jnp.transpose(jnp.concatenate([jnp.where(lane < HEAD_DIM, qp, 0.0),
                                                jnp.where(lane >= HEAD_DIM, qp, 0.0)], axis=0)).astype(BF16)
            s = jnp.dot(kcat, qs, preferred_element_type=F32)
            s_scr[slot, c] = s + jnp.concatenate([bt_ref[first, h0], bt_ref[first, h0 + 1]], axis=1)

    def exponentials(slot):
        for c, (g, u) in enumerate(pairs):
            h0 = 4 * g + 2 * u
            s = s_scr[slot, c]
            sink = jnp.concatenate([sink_ref[h0:h0 + 1, :], sink_ref[h0 + 1:h0 + 2, :]], axis=1)
            m = jnp.maximum(jnp.max(s, axis=0, keepdims=True), sink)
            p = jnp.exp(s - m)
            p_scr[slot, c] = p.astype(BF16)
            inv_scr[slot, c] = jnp.broadcast_to(1.0 / (jnp.sum(p, axis=0, keepdims=True) + jnp.exp(sink - m)),
                                                inv_scr.shape[2:])

    def values(n, slot):
        prev = jnp.maximum(n - 1, 0)
        r_q = pl.multiple_of(n * BLOCK, BLOCK)
        for c, (g, u) in enumerate(pairs):
            vs = slice(g * HEAD_DIM, (g + 1) * HEAD_DIM)
            vcat = jnp.concatenate([vt_ref[nkb, vs, :], vt_ref[prev, vs, :], vt_ref[n, vs, :]], axis=1)
            o = jnp.dot(vcat, p_scr[slot, c], preferred_element_type=F32) * inv_scr[slot, c][0:1]
            ot = jnp.transpose(o)
            o_ref[pl.ds(r_q, BLOCK), (2 * g + u) * LANES:(2 * g + u + 1) * LANES] = (
                jnp.concatenate([ot[:BLOCK], ot[BLOCK:]], axis=1).astype(BF16))

    scores(0, 0)
    scores(1, 1)
    exponentials(0)

    def two_blocks(k, carry):
        n = 2 * k
        scores(n + 2, 0)
        exponentials(1)
        values(n, 0)
        scores(n + 3, 1)
        exponentials(0)
        values(n + 1, 1)
        return carry
    lax.fori_loop(0, nkb // 2 - 1, two_blocks, 0)
    exponentials(1)
    values(nkb - 2, 0)
    values(nkb - 1, 1)


def _swa_attention(sinks, qkv, km, vm, bt, batch, seq):
    nkb = seq // BLOCK
    assert nkb % 2 == 0
    sinkv = jnp.broadcast_to(sinks.reshape(N_SWA_HEADS, 1), (N_SWA_HEADS, LANES))
    return pl.pallas_call(
        _swa_kernel,
        out_shape=jax.ShapeDtypeStruct((batch * seq, N_SWA_HEADS * HEAD_DIM), BF16),
        grid=(batch,),
        in_specs=[
            pl.BlockSpec(sinkv.shape, lambda b: (0, 0)),
            pl.BlockSpec((seq, 512), lambda b: (b, C_SQ // 512)),
            pl.BlockSpec((seq, LANES), lambda b: (b, C_SK // LANES)),
            pl.BlockSpec((seq, LANES), lambda b: (b, C_SV // LANES)),
            pl.BlockSpec(km.shape, lambda b: (0, 0)),
            pl.BlockSpec(vm.shape, lambda b: (0, 0)),
            pl.BlockSpec(bt.shape, lambda b: (0, 0, 0, 0)),
        ],
        out_specs=pl.BlockSpec((seq, 512), lambda b: (b, 0)),
        scratch_shapes=[pltpu.VMEM((nkb + 1, BLOCK, 2 * LANES), BF16),
                        pltpu.VMEM((nkb + 1, LANES, BLOCK), BF16),
                        pltpu.VMEM((2, 4, 3 * BLOCK, 2 * BLOCK), F32),
                        pltpu.VMEM((2, 4, 3 * BLOCK, 2 * BLOCK), BF16),
                        pltpu.VMEM((2, 4, 8, 2 * BLOCK), F32)],
        compiler_params=_cparams(("parallel",)),
        name="swa_attention",
    )(sinkv, qkv, qkv, qkv, km, vm, bt)


def _outproj_kernel(x_ref, md_ref, ms_ref, wo_ref, g2_ref, wr_ref, br_ref,
                    h_ref, hb_ref, rt_ref, ti_ref, cnt_ref, c_ref, lg_ref):
    i = pl.program_id(0)

    @pl.when(i == 0)
    def _init():
        c_ref[...] = jnp.zeros(c_ref.shape, F32)
        lg_ref[...] = jnp.zeros(lg_ref.shape, F32)

    lg_prev = lg_ref[...]
    half = md_ref.shape[1]
    h = (x_ref[...]
         + jnp.dot(md_ref[...], wo_ref[:half, :], preferred_element_type=F32)
         + jnp.dot(ms_ref[...], wo_ref[half:, :], preferred_element_type=F32))
    h_ref[...] = h
    hn = h * lax.rsqrt(jnp.mean(h * h, axis=-1, keepdims=True) + EPS) * g2_ref[...]
    hb = hn.astype(BF16)
    hb_ref[...] = hb
    lg_ref[...] = jnp.dot(hb, wr_ref[...], preferred_element_type=F32) + br_ref[...]
    _route_tile(lg_prev, jnp.where(i > 0, 1.0, 0.0), rt_ref, ti_ref, c_ref)

    @pl.when(i == pl.num_programs(0) - 1)
    def _fin():
        cnt_ref[...] = c_ref[...]


def _route_tile(lg, live, rt_ref, ti_ref, c_ref):
    tm = lg.shape[0]
    lane_i = lax.broadcasted_iota(jnp.int32, lg.shape, 1)
    lane = lane_i.astype(F32)
    big = float(4 * LANES)
    is_g = (lane_i >= N_EXPERTS) & (lane_i < N_EXPERTS + N_GROUPS)
    glm = jnp.where(is_g, lg, -jnp.inf)
    gmax = jnp.max(glm, axis=1, keepdims=True)
    gidx = jnp.min(jnp.where(glm == gmax, lane, big), axis=1, keepdims=True) - N_EXPERTS
    gsum = jnp.sum(jnp.where(is_g, jnp.exp(lg - gmax), 0.0), axis=1, keepdims=True)
    g_w = 1.0 / gsum
    lane_grp = (lane_i >> 3).astype(F32)
    in_grp = (lane_i < N_EXPERTS) & (lane_grp == gidx)
    el = jnp.where(in_grp, lg, -jnp.inf)
    t1 = jnp.max(el, axis=1, keepdims=True)
    j1 = jnp.min(jnp.where(el == t1, lane, big), axis=1, keepdims=True)
    el2 = jnp.where(lane == j1, -jnp.inf, el)
    t2 = jnp.max(el2, axis=1, keepdims=True)
    j2 = jnp.min(jnp.where(el2 == t2, lane, big), axis=1, keepdims=True)
    e2 = jnp.exp(t2 - t1)
    den = 1.0 + e2
    gate1 = g_w / den
    gate2 = g_w * e2 / den

    o1 = lane == j1
    o2 = lane == j2
    onehot = jnp.where(o1 | o2, 1.0, 0.0).astype(BF16)
    rr = lax.broadcasted_iota(jnp.int32, (tm, tm), 0)
    cc = lax.broadcasted_iota(jnp.int32, (tm, tm), 1)
    lower = jnp.where(rr > cc, 1.0, 0.0).astype(BF16)
    pfx = jnp.dot(lower, onehot, preferred_element_type=F32)
    cnt_tile = jnp.sum(onehot.astype(F32), axis=0, keepdims=True)
    groups = jnp.floor((cnt_tile + (ROW_ALIGN - 1)) * (1.0 / ROW_ALIGN))
    er = lax.broadcasted_iota(jnp.int32, (LANES, LANES), 0)
    ec = lax.broadcasted_iota(jnp.int32, (LANES, LANES), 1)
    before = jnp.where(er < ec, 1.0, 0.0).astype(BF16)
    cbase = ROW_ALIGN * jnp.dot(jnp.broadcast_to(groups, (8, LANES)).astype(BF16), before,
                                preferred_element_type=F32)[0:1]
    at = pfx + cbase
    pos1 = jnp.sum(jnp.where(o1, at, 0.0), axis=1, keepdims=True)
    pos2 = jnp.sum(jnp.where(o2, at, 0.0), axis=1, keepdims=True)
    rt_ref[...] = jnp.where(lane_i == 0, gate1,
                            jnp.where(lane_i == 1, gate2,
                                      jnp.where(lane_i == 2, pos1,
                                                jnp.where(lane_i == 3, pos2, 0.0))))
    c_old = c_ref[...]
    c_ref[...] = c_old + groups * (ROW_ALIGN * live)
    row8 = lax.broadcasted_iota(jnp.int32, (8, LANES), 0)
    ti_ref[...] = jnp.where(row8 == 0, cnt_tile, jnp.where(row8 == 1, c_old, 0.0))


def _outproj(x2, mixd, mixs, wo, g2, wr, br):
    n, d = x2.shape
    nt = n // TM

    def proj_tile(i):
        return (jnp.minimum(i, nt - 1), 0)

    def route_tile(i):
        return (jnp.maximum(i - 1, 0), 0)

    return pl.pallas_call(
        _outproj_kernel,
        out_shape=(jax.ShapeDtypeStruct((n, d), F32),
                   jax.ShapeDtypeStruct((n, d), BF16),
                   jax.ShapeDtypeStruct((n, LANES), F32),
                   jax.ShapeDtypeStruct((nt * 8, LANES), F32),
                   jax.ShapeDtypeStruct((8, LANES), F32)),
        grid=(nt + 1,),
        in_specs=[
            pl.BlockSpec((TM, d), proj_tile),
            pl.BlockSpec((TM, mixd.shape[1]), proj_tile),
            pl.BlockSpec((TM, mixs.shape[1]), proj_tile),
            pl.BlockSpec(wo.shape, lambda i: (0, 0)),
            pl.BlockSpec(g2.shape, lambda i: (0, 0)),
            pl.BlockSpec(wr.shape, lambda i: (0, 0)),
            pl.BlockSpec(br.shape, lambda i: (0, 0)),
        ],
        out_specs=(pl.BlockSpec((TM, d), proj_tile),
                   pl.BlockSpec((TM, d), proj_tile),
                   pl.BlockSpec((TM, LANES), route_tile),
                   pl.BlockSpec((8, LANES), route_tile),
                   pl.BlockSpec((8, LANES), lambda i: (0, 0))),
        scratch_shapes=[pltpu.VMEM((8, LANES), F32), pltpu.VMEM((TM, LANES), F32)],
        compiler_params=_cparams(("arbitrary",)),
        name="outproj_router",
    )(x2, mixd, mixs, wo, g2, wr, br)


def _for_each_chunk(runs_ref, fn):
    big = CHUNK_ROWS[0]

    def per_expert(e, sorted_row, priority):
        start = runs_ref[0, 0, e]
        groups = runs_ref[0, 0, N_EXPERTS + e]
        whole = groups // (big // ROW_ALIGN)

        def per_chunk(c, carry):
            fn(pl.multiple_of(start + c * big, ROW_ALIGN), pl.multiple_of(sorted_row + c * big, ROW_ALIGN),
               big, priority)
            return carry
        lax.fori_loop(0, whole, per_chunk, 0)

        done = whole * big
        for rows in CHUNK_ROWS[1:]:
            has = (groups // (rows // ROW_ALIGN)) % 2

            @pl.when(has == 1)
            def _(done=done, rows=rows):
                fn(pl.multiple_of(start + done, ROW_ALIGN), pl.multiple_of(sorted_row + done, ROW_ALIGN),
                   rows, priority)
            done = done + has * rows
        return sorted_row + groups * ROW_ALIGN

    def expert_pair(e2, sorted_row):
        return per_expert(2 * e2 + 1, per_expert(2 * e2, sorted_row, 0), 1)
    lax.fori_loop(0, N_EXPERTS // 2, expert_pair, 0)


def _wait_chunks(runs_ref, make_copy):
    for k, rows in enumerate(CHUNK_ROWS):
        def body(c, carry, rows=rows):
            make_copy(rows).wait()
            return carry
        lax.fori_loop(0, runs_ref[0, 0, 2 * N_EXPERTS + k], body, 0)


def _dispatch_kernel(zf_ref, cur_ref, prv_ref, hb_ref, rt_ref, xs_ref, sbuf, zbuf, sem, zsem):
    i = pl.program_id(0)
    nt = pl.num_programs(0)
    slot = i % 2
    tm, d = hb_ref.shape

    def for_zero_blocks(kind, fn):
        def body(b, carry):
            @pl.when(zf_ref[b] == kind)
            def _():
                fn(pltpu.make_async_copy(zbuf, xs_ref.at[pl.ds(pl.multiple_of(b * EB, EB), EB)],
                                         zsem.at[kind - 1]))
            return carry
        lax.fori_loop(0, zf_ref.shape[0], body, 0)

    @pl.when(i == 0)
    def _():
        zbuf[...] = jnp.zeros(zbuf.shape, zbuf.dtype)
        for_zero_blocks(1, lambda c: c.start())
        for_zero_blocks(2, lambda c: c.start())
        for_zero_blocks(1, lambda c: c.wait())

    pos_t = jnp.transpose(rt_ref[...])
    srow = lax.broadcasted_iota(jnp.int32, (SROWS, tm), 0).astype(F32)
    sel = jnp.where(srow == pos_t[2:3, :], 1.0, jnp.where(srow == pos_t[3:4, :], 1.0, 0.0)).astype(BF16)
    srt = jnp.dot(sel, hb_ref[...], preferred_element_type=F32)
    bits = pltpu.bitcast(srt, jnp.uint32)
    sbuf[slot] = (bits[:, d // 2:] & jnp.uint32(0xFFFF0000)) | (bits[:, :d // 2] >> 16)

    def chunk_copy(run_row, sorted_row, rows, sl):
        return pltpu.make_async_copy(sbuf.at[sl, pl.ds(sorted_row, rows)], xs_ref.at[pl.ds(run_row, rows)],
                                     sem.at[sl])

    _for_each_chunk(cur_ref, lambda run_row, sorted_row, rows, priority:
                    chunk_copy(run_row, sorted_row, rows, slot).start(priority=priority))

    @pl.when(i > 0)
    def _():
        _wait_chunks(prv_ref, lambda rows: chunk_copy(0, 0, rows, 1 - slot))

    @pl.when(i == nt - 1)
    def _():
        _wait_chunks(cur_ref, lambda rows: chunk_copy(0, 0, rows, slot))
        for_zero_blocks(2, lambda c: c.wait())


def _dispatch(zero_blocks, runs, hb, rt, n_rows):
    n, d = hb.shape
    return pl.pallas_call(
        _dispatch_kernel,
        out_shape=jax.ShapeDtypeStruct((n_rows, d // 2), jnp.uint32),
        grid_spec=pltpu.PrefetchScalarGridSpec(
            num_scalar_prefetch=1,
            grid=(n // TM,),
            in_specs=[
                pl.BlockSpec((1, 1, LANES), lambda i, zf: (i, 0, 0), memory_space=pltpu.SMEM),
                pl.BlockSpec((1, 1, LANES), lambda i, zf: (jnp.maximum(i - 1, 0), 0, 0), memory_space=pltpu.SMEM),
                pl.BlockSpec((TM, d), lambda i, zf: (i, 0)),
                pl.BlockSpec((TM, LANES), lambda i, zf: (i, 0)),
            ],
            out_specs=pl.BlockSpec(memory_space=pl.ANY),
            scratch_shapes=[pltpu.VMEM((2, SROWS, d // 2), jnp.uint32), pltpu.VMEM((EB, d // 2), jnp.uint32),
                            pltpu.SemaphoreType.DMA((2,)), pltpu.SemaphoreType.DMA((2,))],
        ),
        compiler_params=_cparams(("arbitrary",)),
        name="dispatch",
    )(zero_blocks, runs, runs, hb, rt)


def _experts_kernel(be_ref, na_ref, nxt_ref, xs_ref, wg_hbm, wu_hbm, wd_hbm, ys_ref,
                    wgf, wuf, wdf, wgb, wub, wdb, sem):
    b = pl.program_id(0)

    def weight_copies(e):
        return (pltpu.make_async_copy(wg_hbm.at[e], wgf, sem.at[0]),
                pltpu.make_async_copy(wu_hbm.at[e], wuf, sem.at[1]),
                pltpu.make_async_copy(wd_hbm.at[e], wdf, sem.at[2]))

    @pl.when(b == 0)
    def _():
        for c in weight_copies(be_ref[0]):
            c.start()

    @pl.when(b < na_ref[0])
    def _():
        e = be_ref[b]
        changed = jnp.logical_or(b == 0, be_ref[jnp.maximum(b - 1, 0)] != e)

        @pl.when(changed)
        def _load():
            for c in weight_copies(e):
                c.wait()
            wgb[...] = wgf[...].astype(BF16)
            wub[...] = wuf[...].astype(BF16)
            wdb[...] = wdf[...].astype(BF16)
            nxt = nxt_ref[e]

            @pl.when(nxt >= 0)
            def _():
                for c in weight_copies(nxt):
                    c.start()

        w = xs_ref[...]
        x_lo = pltpu.bitcast(w << 16, F32).astype(BF16)
        x_hi = pltpu.bitcast(w & jnp.uint32(0xFFFF0000), F32).astype(BF16)
        dh = w.shape[1]
        g = (jnp.dot(x_lo, wgb[:dh, :], preferred_element_type=F32)
             + jnp.dot(x_hi, wgb[dh:, :], preferred_element_type=F32))
        u = (jnp.dot(x_lo, wub[:dh, :], preferred_element_type=F32)
             + jnp.dot(x_hi, wub[dh:, :], preferred_element_type=F32))
        hdn = g * (1.0 / (1.0 + jnp.exp(-g))) * u
        y = jnp.dot(hdn.astype(BF16), wdb[...], preferred_element_type=F32)
        bits = pltpu.bitcast(y.astype(BF16).astype(F32), jnp.uint32)
        ys_ref[...] = (bits[:, dh:] & jnp.uint32(0xFFFF0000)) | (bits[:, :dh] >> 16)

    @pl.when(b >= na_ref[0])
    def _():
        ys_ref[...] = jnp.zeros(ys_ref.shape, ys_ref.dtype)


def _experts(blk_e, n_act, nxt_e, xs, w_gate, w_up, w_down):
    p, dh = xs.shape
    d = 2 * dh
    de = w_gate.shape[2]

    def row_map(b, be, na, nx):
        return (jnp.minimum(b, na[0] - 1), 0)

    return pl.pallas_call(
        _experts_kernel,
        out_shape=jax.ShapeDtypeStruct((p, dh), jnp.uint32),
        grid_spec=pltpu.PrefetchScalarGridSpec(
            num_scalar_prefetch=3,
            grid=(p // EB,),
            in_specs=[
                pl.BlockSpec((EB, dh), row_map),
                pl.BlockSpec(memory_space=pl.ANY),
                pl.BlockSpec(memory_space=pl.ANY),
                pl.BlockSpec(memory_space=pl.ANY),
            ],
            out_specs=pl.BlockSpec((EB, dh), lambda b, be, na, nx: (b, 0)),
            scratch_shapes=[pltpu.VMEM((d, de), F32), pltpu.VMEM((d, de), F32), pltpu.VMEM((de, d), F32),
                            pltpu.VMEM((d, de), BF16), pltpu.VMEM((d, de), BF16), pltpu.VMEM((de, d), BF16),
                            pltpu.SemaphoreType.DMA((3,))],
        ),
        compiler_params=_cparams(("arbitrary",)),
        name="experts",
    )(blk_e, n_act, nxt_e, xs, w_gate, w_up, w_down)


def _combine_kernel(cur_ref, nxt_ref, ys_ref, h_ref, rt_ref, o_ref, ybuf, sem):
    i = pl.program_id(0)
    nt = pl.num_programs(0)
    slot = i % 2
    tm = h_ref.shape[0]

    def chunk_copy(run_row, sorted_row, rows, sl):
        return pltpu.make_async_copy(ys_ref.at[pl.ds(run_row, rows)], ybuf.at[sl, pl.ds(sorted_row, rows)],
                                     sem.at[sl])

    @pl.when(i == 0)
    def _():
        ybuf[...] = jnp.zeros(ybuf.shape, ybuf.dtype)
        _for_each_chunk(cur_ref, lambda run_row, sorted_row, rows, priority:
                        chunk_copy(run_row, sorted_row, rows, 0).start(priority=priority))

    @pl.when(i + 1 < nt)
    def _():
        _for_each_chunk(nxt_ref, lambda run_row, sorted_row, rows, priority:
                        chunk_copy(run_row, sorted_row, rows, 1 - slot).start(priority=priority))

    _wait_chunks(cur_ref, lambda rows: chunk_copy(0, 0, rows, slot))

    rt = rt_ref[...]
    w = ybuf[slot]
    dh = w.shape[1]
    y_lo = pltpu.bitcast(w << 16, F32).astype(BF16)
    y_hi = pltpu.bitcast(w & jnp.uint32(0xFFFF0000), F32).astype(BF16)
    col = lax.broadcasted_iota(jnp.int32, (tm, SROWS), 1).astype(F32)
    wsel = jnp.where(col == rt[:, 2:3], rt[:, 0:1], jnp.where(col == rt[:, 3:4], rt[:, 1:2], 0.0)).astype(BF16)
    for half, yb in ((slice(0, dh), y_lo), (slice(dh, 2 * dh), y_hi)):
        o_ref[:, half] = h_ref[:, half] + jnp.dot(wsel, yb, preferred_element_type=F32)


def _combine(runs, ys, h1, rt):
    n, d = h1.shape
    nt = n // TM
    return pl.pallas_call(
        _combine_kernel,
        out_shape=jax.ShapeDtypeStruct((n, d), F32),
        grid=(nt,),
        in_specs=[
            pl.BlockSpec((1, 1, LANES), lambda i: (i, 0, 0), memory_space=pltpu.SMEM),
            pl.BlockSpec((1, 1, LANES), lambda i: (jnp.minimum(i + 1, nt - 1), 0, 0), memory_space=pltpu.SMEM),
            pl.BlockSpec(memory_space=pl.ANY),
            pl.BlockSpec((TM, d), lambda i: (i, 0)),
            pl.BlockSpec((TM, LANES), lambda i: (i, 0)),
        ],
        out_specs=pl.BlockSpec((TM, d), lambda i: (i, 0)),
        scratch_shapes=[pltpu.VMEM((2, SROWS, d // 2), jnp.uint32), pltpu.SemaphoreType.DMA((2,))],
        compiler_params=_cparams(("arbitrary",)),
        name="combine",
    )(runs, runs, ys, h1, rt)


def kernel(x, meta_tokens, rel_bias, norm1_gain, w_in, diff_q_gain, diff_k_gain, lam_q1, lam_k1, lam_q2, lam_k2, diff_subln_gain, swa_q_gain, swa_k_gain, swa_sinks, w_out, norm2_gain, w_group, b_group, w_router, b_router, w_gate, w_up, w_down):
    batch, seq, d = x.shape
    depth = w_in.shape[0]
    n = batch * seq
    assert seq % TQ == 0 and n % TM == 0 and n % TP == 0 and d == 1024
    assert meta_tokens.shape[0] == N_META
    assert depth == 1, "the meta-token rows of the residual stream are not carried across layers"

    h = x.reshape(n, d)
    dblk, bm0, bt = _bias_tables(rel_bias, TQ)
    scale = HEAD_DIM ** -0.5
    bd = jnp.asarray(np.kron(np.eye(MXU_DIM // HEAD_DIM), np.full((HEAD_DIM, HEAD_DIM), 1.0 / HEAD_DIM)), BF16)
    ones = jnp.ones((HEAD_DIM,), F32)
    lower_pad = N_EXPERTS + N_GROUPS

    for layer in range(depth):
        lambda_init = 0.8 - 0.6 * math.exp(-0.3 * layer)
        w_cat = w_in[layer].astype(BF16)
        gain = jnp.concatenate([
            jnp.tile(diff_q_gain[layer] * (scale * LOG2E), 2 * N_DIFF_HEADS),
            jnp.tile(diff_k_gain[layer], 2 * N_DIFF_HEADS),
            jnp.tile(ones, 2 * N_DIFF_HEADS),
            jnp.tile(swa_q_gain[layer] * scale, N_SWA_HEADS),
            jnp.tile(swa_k_gain[layer], N_SWA_KV),
            jnp.tile(ones, N_SWA_KV)]).reshape(1, C_END).astype(F32)
        nmask = np.zeros((1, C_END), np.float32)
        nmask[:, C_DQ:C_DV] = 1.0
        nmask[:, C_SQ:C_SV] = 1.0
        nmask = jnp.asarray(nmask)
        g1 = norm1_gain[layer].reshape(1, d).astype(F32)

        qkv = _proj(h, g1, w_cat, bd, gain, nmask, TP)
        qkv_meta = _proj(meta_tokens.astype(F32), g1, w_cat, bd, gain, nmask, N_META)
        meta_pad = jnp.pad(qkv_meta, ((0, TQ - N_META), (0, 0)))

        lamv = jnp.pad(jnp.stack([lam_q1[layer], lam_k1[layer], lam_q2[layer], lam_k2[layer]]).astype(F32),
                       ((0, 4), (0, LANES - HEAD_DIM)))
        mixd = _diff_attention(qkv, meta_pad[:, C_DK:C_DV], meta_pad[:, C_DV:C_SQ], dblk, bm0, lamv,
                               diff_subln_gain[layer].reshape(1, LANES).astype(F32), batch, seq, lambda_init)
        mixs = _swa_attention(swa_sinks[layer].astype(F32), qkv, meta_pad[:BLOCK, C_SK:C_SV],
                              meta_pad[:BLOCK, C_SV:C_END], jnp.swapaxes(bt, -1, -2), batch, seq)

        wr = jnp.pad(jnp.concatenate([w_router[layer], w_group[layer]], axis=1),
                     ((0, 0), (0, LANES - lower_pad))).astype(BF16)
        br = jnp.pad(jnp.concatenate([b_router[layer], b_group[layer]]), (0, LANES - lower_pad)).reshape(1, LANES)
        h1, hb, rt, tinfo, cnt = _outproj(h, mixd, mixs, w_out[layer].astype(BF16),
                                          norm2_gain[layer].reshape(1, d).astype(F32), wr, br.astype(F32))

        nt = n // TM
        counts = cnt[0, :N_EXPERTS].astype(jnp.int32)
        nblk_e = (counts + EB - 1) // EB
        blk_end = jnp.cumsum(nblk_e)
        pstart = ((blk_end - nblk_e) * EB).astype(jnp.int32)
        n_blocks = -(-(2 * n + nt * N_EXPERTS * (ROW_ALIGN - 1) + N_EXPERTS * (EB - 1)) // EB)
        blk_ids = jnp.arange(n_blocks)
        blk_e = jnp.minimum(jnp.sum(blk_end[None, :] <= blk_ids[:, None], axis=1), N_EXPERTS - 1).astype(jnp.int32)
        n_act = blk_end[-1:].astype(jnp.int32)
        is_last = jnp.any((blk_end[None, :] == blk_ids[:, None] + 1) & (nblk_e[None, :] > 0), axis=1)
        zero_blocks = jnp.where(blk_ids >= n_act[0], 2, jnp.where(is_last, 1, 0)).astype(jnp.int32)
        ti = tinfo.reshape(nt, 8, LANES)
        run_len = ti[:, 0, :N_EXPERTS].astype(jnp.int32)
        run_start = pstart[None, :] + ti[:, 1, :N_EXPERTS].astype(jnp.int32)
        run_groups = (run_len + ROW_ALIGN - 1) // ROW_ALIGN
        n_copies = [jnp.sum(run_groups // (CHUNK_ROWS[0] // ROW_ALIGN), axis=1, keepdims=True)]
        n_copies += [jnp.sum((run_groups // (rows // ROW_ALIGN)) % 2, axis=1, keepdims=True)
                     for rows in CHUNK_ROWS[1:]]
        runs = jnp.concatenate([run_start, run_groups] + n_copies
                               + [jnp.zeros((nt, LANES - 2 * N_EXPERTS - len(CHUNK_ROWS)), jnp.int32)],
                               axis=1).reshape(nt, 1, LANES)

        xs = _dispatch(zero_blocks, runs, hb, rt, n_blocks * EB)
        own = jnp.where(nblk_e > 0, jnp.arange(N_EXPERTS), N_EXPERTS)
        later = jnp.concatenate([lax.cummin(own[::-1])[::-1][1:], jnp.full((1,), N_EXPERTS)])
        nxt_e = jnp.where(later < N_EXPERTS, later, -1).astype(jnp.int32)
        ys = _experts(blk_e, n_act, nxt_e, xs, w_gate[layer], w_up[layer], w_down[layer])
        h = _combine(runs, ys, h1, rt)
    return h.reshape(batch, seq, d)
```

```python
import functools
import math

import numpy as np
import jax
import jax.numpy as jnp
from jax import lax
from jax.experimental import pallas as pl
from jax.experimental.pallas import tpu as pltpu

F32 = jnp.float32
BF16 = jnp.bfloat16

HEAD_DIM = 64
N_DIFF_HEADS = 4
N_SWA_HEADS = 8
N_SWA_KV = 2
BLOCK = 128
N_META = 16
N_BUCKETS = 32
MAX_DISTANCE = 128
N_GROUPS = 4
EXPERTS_PER_GROUP = 8
N_EXPERTS = N_GROUPS * EXPERTS_PER_GROUP
D_EXPERT = 512
EPS = 1e-6
NEG = -1e30
LOG2E = math.log2(math.e)

LANES = 128
MXU_DIM = 256
VMEM_LIMIT = 48 * 1024 * 1024

TP = 512
TM = 512
TQ = 256
ONES_ROWS = 16
EB = 512
ROW_ALIGN = 8
CHUNK_ROWS = (32, 16, 8)
SROWS = -(-(2 * TM + N_EXPERTS * (ROW_ALIGN - 1)) // MXU_DIM) * MXU_DIM

C_DQ, C_DK, C_DV, C_SQ, C_SK, C_SV, C_END = 0, 512, 1024, 1536, 2048, 2176, 2304
NORM_GROUPS = (0, 1, 2, 3, 6, 7, 8)


def _cparams(sem):
    return pltpu.CompilerParams(dimension_semantics=sem, vmem_limit_bytes=VMEM_LIMIT)


def _t5_bucket_np(dist):
    n = np.maximum(dist, 0)
    max_exact = N_BUCKETS // 2
    nf = np.maximum(n, 1).astype(np.float32)
    large = max_exact + (np.log(nf / np.float32(max_exact)) / np.float32(math.log(MAX_DISTANCE / max_exact))
                         * np.float32(N_BUCKETS - max_exact)).astype(np.int32)
    large = np.minimum(large, N_BUCKETS - 1)
    return np.where(n < max_exact, n, large)


def _bias_tables(rel_bias, tq):
    nd = 2 * BLOCK
    buckets = _t5_bucket_np(np.arange(nd))
    assert (buckets[MAX_DISTANCE:] == N_BUCKETS - 1).all()
    rb = rel_bias.astype(F32)
    r = np.arange(BLOCK)[:, None]
    c = np.arange(BLOCK)[None, :]
    d_own = r - c
    d_prev = BLOCK + r - c
    far = rb[N_BUCKETS - 1]

    def take(dist):
        idx = jnp.asarray(buckets[np.clip(dist, 0, nd - 1)], jnp.int32)[None]
        out = jnp.zeros((rb.shape[1],) + dist.shape, F32)
        for b in range(N_BUCKETS):
            out = jnp.where(idx == b, rb[b].reshape((-1,) + (1,) * dist.ndim), out)
        return out

    hd = slice(0, N_DIFF_HEADS)
    far_d = far[hd][:, None, None]
    d0 = jnp.where(d_own[None] >= 0, take(d_own)[hd] - far_d, NEG)
    d1 = take(d_prev)[hd] - far_d
    dblk = jnp.stack([d0, d1], axis=1)
    rq = np.arange(tq)[:, None]
    cm = np.arange(LANES)[None, :]
    d_meta = N_META + rq - cm
    bm0 = jnp.where((cm < N_META)[None], take(d_meta)[hd] - far_d, NEG)

    hs = slice(N_DIFF_HEADS, N_DIFF_HEADS + N_SWA_HEADS)
    far_s = far[hs][:, None, None]
    d_meta_s = N_META + r - cm
    meta_first = jnp.where((cm < N_META)[None], take(d_meta_s)[hs], NEG)
    meta_rest = jnp.where((cm < N_META)[None], jnp.broadcast_to(far_s, (N_SWA_HEADS, BLOCK, LANES)), NEG)
    prev_rest = jnp.where((c > r)[None], take(d_prev)[hs], NEG)
    prev_first = jnp.full((N_SWA_HEADS, BLOCK, BLOCK), NEG, F32)
    own = jnp.where((d_own >= 0)[None], take(d_own)[hs], NEG)
    bt = jnp.stack([jnp.concatenate([meta_first, prev_first, own], axis=-1),
                    jnp.concatenate([meta_rest, prev_rest, own], axis=-1)], axis=0)
    return dblk.astype(F32), bm0.astype(F32), bt.astype(F32)


def _proj_kernel(x_ref, g1_ref, w_ref, bd_ref, gain_ref, nmask_ref, o_ref):
    x = x_ref[...]
    a = x * lax.rsqrt(jnp.mean(x * x, axis=-1, keepdims=True) + EPS) * g1_ref[...]
    p = jnp.dot(a.astype(BF16), w_ref[...], preferred_element_type=F32)
    bd = bd_ref[...]
    for j in range(C_END // MXU_DIM):
        sl = slice(j * MXU_DIM, (j + 1) * MXU_DIM)
        pj = p[:, sl]
        if j in NORM_GROUPS:
            ms = jnp.dot((pj * pj).astype(BF16), bd, preferred_element_type=F32)
            pj = jnp.where(nmask_ref[:, sl] != 0.0, pj * lax.rsqrt(ms + EPS) * gain_ref[:, sl], pj)
        o_ref[:, sl] = pj.astype(BF16)


def _proj(x2, g1, w, bd, gain, nmask, tm):
    n = x2.shape[0]
    return pl.pallas_call(
        _proj_kernel,
        out_shape=jax.ShapeDtypeStruct((n, C_END), BF16),
        grid=(n // tm,),
        in_specs=[
            pl.BlockSpec((tm, x2.shape[1]), lambda i: (i, 0)),
            pl.BlockSpec(g1.shape, lambda i: (0, 0)),
            pl.BlockSpec(w.shape, lambda i: (0, 0)),
            pl.BlockSpec(bd.shape, lambda i: (0, 0)),
            pl.BlockSpec(gain.shape, lambda i: (0, 0)),
            pl.BlockSpec(nmask.shape, lambda i: (0, 0)),
        ],
        out_specs=pl.BlockSpec((tm, C_END), lambda i: (i, 0)),
        compiler_params=_cparams(("parallel",)),
        name="proj",
    )(x2, g1, w, bd, gain, nmask)


def _diff_kernel(qi_tab, t_tab, q_ref, k_ref, v_ref, km_ref, vm_ref, d_ref, bm0_ref, lamv_ref, gain_ref, o_ref,
                 bias_ref, mb_ref, qs_ref, kt_ref, vt_ref, s_buf, p_buf, a_buf, m_ref, acc_ref, *,
                 lambda_init, n_steps, n_far, n_near):
    tq = TQ
    nq = q_ref.shape[0] // tq
    nb = tq // BLOCK
    BIAS_LEFT, BIAS_DIAG = 0, 1

    d0 = d_ref[0, 0] * LOG2E
    d1 = d_ref[0, 1] * LOG2E
    zeros = jnp.zeros((BLOCK, BLOCK), F32)
    for a in range(nb):
        for b in range(nb):
            rs, cs = slice(a * BLOCK, (a + 1) * BLOCK), slice(b * BLOCK, (b + 1) * BLOCK)
            if a == b:
                blk = d0
            elif b == a + 1:
                blk = d1
            elif b > a:
                blk = zeros
            else:
                blk = jnp.full((BLOCK, BLOCK), NEG, F32)
            bias_ref[BIAS_DIAG, rs, cs] = blk
            bias_ref[BIAS_LEFT, rs, cs] = d1 if (b == 0 and a == nb - 1) else zeros
    mb_ref[0] = jnp.zeros((N_META, tq), F32)
    mb_ref[1] = bm0_ref[0, :N_META, :] * LOG2E

    lane = lax.broadcasted_iota(jnp.int32, (tq, LANES), 1)
    for i in range(nq):
        rows = slice(i * tq, (i + 1) * tq)
        q = q_ref[rows, :].astype(F32)
        qs_ref[i] = jnp.transpose(jnp.concatenate([jnp.where(lane < HEAD_DIM, q, 0.0),
                                                   jnp.where(lane >= HEAD_DIM, q, 0.0)], axis=0)).astype(BF16)
        vt_ref[i, :LANES, :] = jnp.transpose(v_ref[rows, :].astype(F32)).astype(BF16)
        kt_ref[i] = k_ref[rows, :]
    vt_ref[nq, :LANES, :] = jnp.transpose(vm_ref[...].astype(F32)).astype(BF16)
    kt_ref[nq] = km_ref[...]
    vt_ref[:, LANES:, :] = jnp.ones((nq + 1, ONES_ROWS, tq), BF16)
    acc_ref[...] = jnp.zeros(acc_ref.shape, F32)
    m_ref[...] = jnp.full(m_ref.shape, NEG, F32)
    lv = lamv_ref[...]
    lam = (jnp.exp(jnp.sum(lv[0:1] * lv[1:2], axis=-1, keepdims=True))
           - jnp.exp(jnp.sum(lv[2:3] * lv[3:4], axis=-1, keepdims=True)) + lambda_init)

    FAR, NEAR, META = 0, 1, 2

    def stage_a(n, slot, kind):
        qi, t = qi_tab[n], t_tab[n]
        if kind == META:
            s = jnp.dot(kt_ref[nq, :N_META, :], qs_ref[qi], preferred_element_type=F32)
            s_buf[slot, :N_META] = s + jnp.tile(mb_ref[jnp.where(qi == 0, 1, 0)], (1, 2))
            return
        s = jnp.dot(kt_ref[t - 1], qs_ref[qi], preferred_element_type=F32)
        if kind == NEAR:
            s = s + jnp.tile(bias_ref[jnp.where(t == qi + 1, BIAS_DIAG, BIAS_LEFT)], (1, 2))
        s_buf[slot] = s

    def stage_b(n, slot, kind):
        qi = qi_tab[n]
        rows = slice(0, N_META if kind == META else tq)
        s = s_buf[slot, rows]
        m_prev = m_ref[qi]
        m_new = jnp.maximum(m_prev, jnp.max(s, axis=0, keepdims=True))
        a_buf[slot] = jnp.exp2(m_prev - m_new)
        p_buf[slot, rows] = jnp.exp2(s - m_new[0:1]).astype(BF16)
        m_ref[qi] = m_new

    def stage_c(n, slot, kind):
        qi, t = qi_tab[n], t_tab[n]
        if kind == META:
            pv = jnp.dot(vt_ref[nq, :, :N_META], p_buf[slot, :N_META], preferred_element_type=F32)
        else:
            pv = jnp.dot(vt_ref[t - 1], p_buf[slot], preferred_element_type=F32)
        acc_ref[qi] = a_buf[slot][0:1] * acc_ref[qi] + pv

    def pipeline(base, count, kind):
        LEAD, SLOTS, unroll = 2, 3, 12
        assert unroll % SLOTS == 0
        if count <= 2 * LEAD:
            for j in range(count):
                stage_a(base + j, 0, kind)
                stage_b(base + j, 0, kind)
                stage_c(base + j, 0, kind)
            return
        for j in range(2 * LEAD):
            stage_a(base + j, j % SLOTS, kind)
            if j >= LEAD:
                stage_b(base + j - LEAD, (j - LEAD) % SLOTS, kind)

        def steps(n, count):
            for j in range(count):
                stage_a(base + n + j + 2 * LEAD, (j + 2 * LEAD) % SLOTS, kind)
                stage_b(base + n + j + LEAD, (j + LEAD) % SLOTS, kind)
                stage_c(base + n + j, j % SLOTS, kind)

        n_steady = count - 2 * LEAD
        n_blocks = n_steady // unroll

        def block(k, carry):
            steps(unroll * k, unroll)
            return carry
        lax.fori_loop(0, n_blocks, block, 0)
        steps(unroll * n_blocks, n_steady - unroll * n_blocks)
        for j in range(n_steady, count):
            if j + LEAD < count:
                stage_b(base + j + LEAD, (j + LEAD) % SLOTS, kind)
            stage_c(base + j, j % SLOTS, kind)

    pipeline(0, n_far, FAR)
    pipeline(n_far, n_near, NEAR)
    pipeline(n_far + n_near, n_steps - n_far - n_near, META)

    for i in range(nq):
        acc = acc_ref[i]
        o = acc[:LANES] * (1.0 / acc[LANES:LANES + 1])
        d = o[:, :tq] - lam * o[:, tq:]
        y = d * lax.rsqrt(jnp.mean(d * d, axis=0, keepdims=True) + EPS) * jnp.tile(gain_ref[...], (1, tq // LANES))
        o_ref[i * tq:(i + 1) * tq, :] = jnp.transpose(y * (1.0 - lambda_init)).astype(BF16)


def _diff_attention(qkv, km, vm, dblk, bm0, lamv, gain, batch, seq, lambda_init):
    nq = seq // TQ
    far = [(qi, t) for qi in range(nq) for t in range(1, qi)]
    near = [(qi, t) for qi in range(nq) for t in (qi, qi + 1) if t >= 1]
    meta = [(qi, 0) for qi in range(nq)]
    steps = far + near + meta
    qi_tab = jnp.asarray([s[0] for s in steps], jnp.int32)
    t_tab = jnp.asarray([s[1] for s in steps], jnp.int32)
    kern = functools.partial(_diff_kernel, lambda_init=lambda_init, n_steps=len(steps), n_far=len(far),
                             n_near=len(near))
    return pl.pallas_call(
        kern,
        out_shape=jax.ShapeDtypeStruct((batch * seq, N_DIFF_HEADS * LANES), BF16),
        grid_spec=pltpu.PrefetchScalarGridSpec(
            num_scalar_prefetch=2,
            grid=(batch, N_DIFF_HEADS),
            in_specs=[
                pl.BlockSpec((seq, LANES), lambda b, h, *_: (b, C_DQ // LANES + h)),
                pl.BlockSpec((seq, LANES), lambda b, h, *_: (b, C_DK // LANES + h)),
                pl.BlockSpec((seq, LANES), lambda b, h, *_: (b, C_DV // LANES + h)),
                pl.BlockSpec((TQ, LANES), lambda b, h, *_: (0, h)),
                pl.BlockSpec((TQ, LANES), lambda b, h, *_: (0, h)),
                pl.BlockSpec((1, 2, BLOCK, BLOCK), lambda b, h, *_: (h, 0, 0, 0)),
                pl.BlockSpec((1, LANES, TQ), lambda b, h, *_: (h, 0, 0)),
                pl.BlockSpec(lamv.shape, lambda b, h, *_: (0, 0)),
                pl.BlockSpec((LANES, LANES), lambda b, h, *_: (0, 0)),
            ],
            out_specs=pl.BlockSpec((seq, LANES), lambda b, h, *_: (b, h)),
            scratch_shapes=[
                pltpu.VMEM((2, TQ, TQ), F32),
                pltpu.VMEM((2, N_META, TQ), F32),
                pltpu.VMEM((nq, LANES, 2 * TQ), BF16),
                pltpu.VMEM((nq + 1, TQ, LANES), BF16),
                pltpu.VMEM((nq + 1, LANES + ONES_ROWS, TQ), BF16),
                pltpu.VMEM((3, TQ, 2 * TQ), F32),
                pltpu.VMEM((3, TQ, 2 * TQ), BF16),
                pltpu.VMEM((3, 8, 2 * TQ), F32),
                pltpu.VMEM((nq, 8, 2 * TQ), F32),
                pltpu.VMEM((nq, LANES + ONES_ROWS, 2 * TQ), F32),
            ],
        ),
        compiler_params=_cparams(("parallel", "parallel")),
        name="diff_attention",
    )(qi_tab, t_tab, qkv, qkv, qkv, km, vm, jnp.swapaxes(dblk, -1, -2), jnp.swapaxes(bm0, -1, -2), lamv,
      jnp.broadcast_to(gain.reshape(LANES, 1), (LANES, LANES)))


def _swa_kernel(sink_ref, q_ref, k_ref, v_ref, km_ref, vm_ref, bt_ref, o_ref, kd_ref, vt_ref,
                s_scr, p_scr, inv_scr):
    nkb = k_ref.shape[0] // BLOCK
    lane = lax.broadcasted_iota(jnp.int32, (BLOCK, LANES), 1)
    pairs = [(g, u) for g in range(N_SWA_KV) for u in range(2)]

    def both_halves(k):
        k0, k1 = k[:, :HEAD_DIM], k[:, HEAD_DIM:]
        return jnp.concatenate([k0, k0, k1, k1], axis=1)

    def prepare(j, carry):
        rows = pl.ds(pl.multiple_of(j * BLOCK, BLOCK), BLOCK)
        kd_ref[j] = both_halves(k_ref[rows, :])
        vt_ref[j] = jnp.transpose(v_ref[rows, :].astype(F32)).astype(BF16)
        return carry
    lax.fori_loop(0, nkb, prepare, 0)
    kd_ref[nkb] = both_halves(km_ref[...])
    vt_ref[nkb] = jnp.transpose(vm_ref[...].astype(F32)).astype(BF16)

    def scores(n, slot):
        first = jnp.where(n == 0, 0, 1)
        prev = jnp.maximum(n - 1, 0)
        r_q = pl.multiple_of(n * BLOCK, BLOCK)
        for c, (g, u) in enumerate(pairs):
            ks = slice(g * LANES, (g + 1) * LANES)
            kcat = jnp.concatenate([kd_ref[nkb, :, ks], kd_ref[prev, :, ks], kd_ref[n, :, ks]], axis=0)
            h0 = 4 * g + 2 * u
            qp = q_ref[pl.ds(r_q, BLOCK), (2 * g + u) * LANES:(2 * g + u + 1) * LANES].astype(F32)
            qs = jnp.transpose(jnp.concatenate([jnp.where(lane < HEAD_DIM, qp, 0.0),
                                                jnp.where(lane >= HEAD_DIM, qp, 0.0)], axis=0)).astype(BF16)
            s = jnp.dot(kcat, qs, preferred_element_type=F32)
            s_scr[slot, c] = s + jnp.concatenate([bt_ref[first, h0], bt_ref[first, h0 + 1]], axis=1)

    def exponentials(slot):
        for c, (g, u) in enumerate(pairs):
            h0 = 4 * g + 2 * u
            s = s_scr[slot, c]
            sink = jnp.concatenate([sink_ref[h0:h0 + 1, :], sink_ref[h0 + 1:h0 + 2, :]], axis=1)
            m = jnp.maximum(jnp.max(s, axis=0, keepdims=True), sink)
            p = jnp.exp(s - m)
            p_scr[slot, c] = p.astype(BF16)
            inv_scr[slot, c] = jnp.broadcast_to(1.0 / (jnp.sum(p, axis=0, keepdims=True) + jnp.exp(sink - m)),
                                                inv_scr.shape[2:])

    def values(n, slot):
        prev = jnp.maximum(n - 1, 0)
        r_q = pl.multiple_of(n * BLOCK, BLOCK)
        for c, (g, u) in enumerate(pairs):
            vs = slice(g * HEAD_DIM, (g + 1) * HEAD_DIM)
            vcat = jnp.concatenate([vt_ref[nkb, vs, :], vt_ref[prev, vs, :], vt_ref[n, vs, :]], axis=1)
            o = jnp.dot(vcat, p_scr[slot, c], preferred_element_type=F32) * inv_scr[slot, c][0:1]
            ot = jnp.transpose(o)
            o_ref[pl.ds(r_q, BLOCK), (2 * g + u) * LANES:(2 * g + u + 1) * LANES] = (
                jnp.concatenate([ot[:BLOCK], ot[BLOCK:]], axis=1).astype(BF16))

    scores(0, 0)
    scores(1, 1)
    exponentials(0)

    def two_blocks(k, carry):
        n = 2 * k
        scores(n + 2, 0)
        exponentials(1)
        values(n, 0)
        scores(n + 3, 1)
        exponentials(0)
        values(n + 1, 1)
        return carry
    lax.fori_loop(0, nkb // 2 - 1, two_blocks, 0)
    exponentials(1)
    values(nkb - 2, 0)
    values(nkb - 1, 1)


def _swa_attention(sinks, qkv, km, vm, bt, batch, seq):
    nkb = seq // BLOCK
    assert nkb % 2 == 0
    sinkv = jnp.broadcast_to(sinks.reshape(N_SWA_HEADS, 1), (N_SWA_HEADS, LANES))
    return pl.pallas_call(
        _swa_kernel,
        out_shape=jax.ShapeDtypeStruct((batch * seq, N_SWA_HEADS * HEAD_DIM), BF16),
        grid=(batch,),
        in_specs=[
            pl.BlockSpec(sinkv.shape, lambda b: (0, 0)),
            pl.BlockSpec((seq, 512), lambda b: (b, C_SQ // 512)),
            pl.BlockSpec((seq, LANES), lambda b: (b, C_SK // LANES)),
            pl.BlockSpec((seq, LANES), lambda b: (b, C_SV // LANES)),
            pl.BlockSpec(km.shape, lambda b: (0, 0)),
            pl.BlockSpec(vm.shape, lambda b: (0, 0)),
            pl.BlockSpec(bt.shape, lambda b: (0, 0, 0, 0)),
        ],
        out_specs=pl.BlockSpec((seq, 512), lambda b: (b, 0)),
        scratch_shapes=[pltpu.VMEM((nkb + 1, BLOCK, 2 * LANES), BF16),
                        pltpu.VMEM((nkb + 1, LANES, BLOCK), BF16),
                        pltpu.VMEM((2, 4, 3 * BLOCK, 2 * BLOCK), F32),
                        pltpu.VMEM((2, 4, 3 * BLOCK, 2 * BLOCK), BF16),
                        pltpu.VMEM((2, 4, 8, 2 * BLOCK), F32)],
        compiler_params=_cparams(("parallel",)),
        name="swa_attention",
    )(sinkv, qkv, qkv, qkv, km, vm, bt)


def _outproj_kernel(x_ref, md_ref, ms_ref, wo_ref, g2_ref, wr_ref, br_ref,
                    h_ref, hb_ref, rt_ref, ti_ref, cnt_ref, c_ref, lg_ref):
    i = pl.program_id(0)

    @pl.when(i == 0)
    def _init():
        c_ref[...] = jnp.zeros(c_ref.shape, F32)
        lg_ref[...] = jnp.zeros(lg_ref.shape, F32)

    lg_prev = lg_ref[...]
    half = md_ref.shape[1]
    h = (x_ref[...]
         + jnp.dot(md_ref[...], wo_ref[:half, :], preferred_element_type=F32)
         + jnp.dot(ms_ref[...], wo_ref[half:, :], preferred_element_type=F32))
    h_ref[...] = h
    hn = h * lax.rsqrt(jnp.mean(h * h, axis=-1, keepdims=True) + EPS) * g2_ref[...]
    hb = hn.astype(BF16)
    hb_ref[...] = hb
    lg_ref[...] = jnp.dot(hb, wr_ref[...], preferred_element_type=F32) + br_ref[...]
    _route_tile(lg_prev, jnp.where(i > 0, 1.0, 0.0), rt_ref, ti_ref, c_ref)

    @pl.when(i == pl.num_programs(0) - 1)
    def _fin():
        cnt_ref[...] = c_ref[...]


def _route_tile(lg, live, rt_ref, ti_ref, c_ref):
    tm = lg.shape[0]
    lane_i = lax.broadcasted_iota(jnp.int32, lg.shape, 1)
    lane = lane_i.astype(F32)
    big = float(4 * LANES)
    is_g = (lane_i >= N_EXPERTS) & (lane_i < N_EXPERTS + N_GROUPS)
    glm = jnp.where(is_g, lg, -jnp.inf)
    gmax = jnp.max(glm, axis=1, keepdims=True)
    gidx = jnp.min(jnp.where(glm == gmax, lane, big), axis=1, keepdims=True) - N_EXPERTS
    gsum = jnp.sum(jnp.where(is_g, jnp.exp(lg - gmax), 0.0), axis=1, keepdims=True)
    g_w = 1.0 / gsum
    lane_grp = (lane_i >> 3).astype(F32)
    in_grp = (lane_i < N_EXPERTS) & (lane_grp == gidx)
    el = jnp.where(in_grp, lg, -jnp.inf)
    t1 = jnp.max(el, axis=1, keepdims=True)
    j1 = jnp.min(jnp.where(el == t1, lane, big), axis=1, keepdims=True)
    el2 = jnp.where(lane == j1, -jnp.inf, el)
    t2 = jnp.max(el2, axis=1, keepdims=True)
    j2 = jnp.min(jnp.where(el2 == t2, lane, big), axis=1, keepdims=True)
    e2 = jnp.exp(t2 - t1)
    den = 1.0 + e2
    gate1 = g_w / den
    gate2 = g_w * e2 / den

    o1 = lane == j1
    o2 = lane == j2
    onehot = jnp.where(o1 | o2, 1.0, 0.0).astype(BF16)
    rr = lax.broadcasted_iota(jnp.int32, (tm, tm), 0)
    cc = lax.broadcasted_iota(jnp.int32, (tm, tm), 1)
    lower = jnp.where(rr > cc, 1.0, 0.0).astype(BF16)
    pfx = jnp.dot(lower, onehot, preferred_element_type=F32)
    cnt_tile = jnp.sum(onehot.astype(F32), axis=0, keepdims=True)
    groups = jnp.floor((cnt_tile + (ROW_ALIGN - 1)) * (1.0 / ROW_ALIGN))
    er = lax.broadcasted_iota(jnp.int32, (LANES, LANES), 0)
    ec = lax.broadcasted_iota(jnp.int32, (LANES, LANES), 1)
    before = jnp.where(er < ec, 1.0, 0.0).astype(BF16)
    cbase = ROW_ALIGN * jnp.dot(jnp.broadcast_to(groups, (8, LANES)).astype(BF16), before,
                                preferred_element_type=F32)[0:1]
    at = pfx + cbase
    pos1 = jnp.sum(jnp.where(o1, at, 0.0), axis=1, keepdims=True)
    pos2 = jnp.sum(jnp.where(o2, at, 0.0), axis=1, keepdims=True)
    rt_ref[...] = jnp.where(lane_i == 0, gate1,
                            jnp.where(lane_i == 1, gate2,
                                      jnp.where(lane_i == 2, pos1,
                                                jnp.where(lane_i == 3, pos2, 0.0))))
    c_old = c_ref[...]
    c_ref[...] = c_old + groups * (ROW_ALIGN * live)
    row8 = lax.broadcasted_iota(jnp.int32, (8, LANES), 0)
    ti_ref[...] = jnp.where(row8 == 0, cnt_tile, jnp.where(row8 == 1, c_old, 0.0))


def _outproj(x2, mixd, mixs, wo, g2, wr, br):
    n, d = x2.shape
    nt = n // TM

    def proj_tile(i):
        return (jnp.minimum(i, nt - 1), 0)

    def route_tile(i):
        return (jnp.maximum(i - 1, 0), 0)

    return pl.pallas_call(
        _outproj_kernel,
        out_shape=(jax.ShapeDtypeStruct((n, d), F32),
                   jax.ShapeDtypeStruct((n, d), BF16),
                   jax.ShapeDtypeStruct((n, LANES), F32),
                   jax.ShapeDtypeStruct((nt * 8, LANES), F32),
                   jax.ShapeDtypeStruct((8, LANES), F32)),
        grid=(nt + 1,),
        in_specs=[
            pl.BlockSpec((TM, d), proj_tile),
            pl.BlockSpec((TM, mixd.shape[1]), proj_tile),
            pl.BlockSpec((TM, mixs.shape[1]), proj_tile),
            pl.BlockSpec(wo.shape, lambda i: (0, 0)),
            pl.BlockSpec(g2.shape, lambda i: (0, 0)),
            pl.BlockSpec(wr.shape, lambda i: (0, 0)),
            pl.BlockSpec(br.shape, lambda i: (0, 0)),
        ],
        out_specs=(pl.BlockSpec((TM, d), proj_tile),
                   pl.BlockSpec((TM, d), proj_tile),
                   pl.BlockSpec((TM, LANES), route_tile),
                   pl.BlockSpec((8, LANES), route_tile),
                   pl.BlockSpec((8, LANES), lambda i: (0, 0))),
        scratch_shapes=[pltpu.VMEM((8, LANES), F32), pltpu.VMEM((TM, LANES), F32)],
        compiler_params=_cparams(("arbitrary",)),
        name="outproj_router",
    )(x2, mixd, mixs, wo, g2, wr, br)


def _for_each_chunk(runs_ref, fn):
    big = CHUNK_ROWS[0]

    def per_expert(e, sorted_row, priority):
        start = runs_ref[0, 0, e]
        groups = runs_ref[0, 0, N_EXPERTS + e]
        whole = groups // (big // ROW_ALIGN)

        def per_chunk(c, carry):
            fn(pl.multiple_of(start + c * big, ROW_ALIGN), pl.multiple_of(sorted_row + c * big, ROW_ALIGN),
               big, priority)
            return carry
        lax.fori_loop(0, whole, per_chunk, 0)

        done = whole * big
        for rows in CHUNK_ROWS[1:]:
            has = (groups // (rows // ROW_ALIGN)) % 2

            @pl.when(has == 1)
            def _(done=done, rows=rows):
                fn(pl.multiple_of(start + done, ROW_ALIGN), pl.multiple_of(sorted_row + done, ROW_ALIGN),
                   rows, priority)
            done = done + has * rows
        return sorted_row + groups * ROW_ALIGN

    def expert_pair(e2, sorted_row):
        return per_expert(2 * e2 + 1, per_expert(2 * e2, sorted_row, 0), 1)
    lax.fori_loop(0, N_EXPERTS // 2, expert_pair, 0)


def _wait_chunks(runs_ref, make_copy):
    for k, rows in enumerate(CHUNK_ROWS):
        def body(c, carry, rows=rows):
            make_copy(rows).wait()
            return carry
        lax.fori_loop(0, runs_ref[0, 0, 2 * N_EXPERTS + k], body, 0)


def _dispatch_kernel(zf_ref, cur_ref, prv_ref, hb_ref, rt_ref, xs_ref, sbuf, zbuf, sem, zsem):
    i = pl.program_id(0)
    nt = pl.num_programs(0)
    slot = i % 2
    tm, d = hb_ref.shape

    def for_zero_blocks(kind, fn):
        def body(b, carry):
            @pl.when(zf_ref[b] == kind)
            def _():
                fn(pltpu.make_async_copy(zbuf, xs_ref.at[pl.ds(pl.multiple_of(b * EB, EB), EB)],
                                         zsem.at[kind - 1]))
            return carry
        lax.fori_loop(0, zf_ref.shape[0], body, 0)

    @pl.when(i == 0)
    def _():
        zbuf[...] = jnp.zeros(zbuf.shape, zbuf.dtype)
        for_zero_blocks(1, lambda c: c.start())
        for_zero_blocks(2, lambda c: c.start())
        for_zero_blocks(1, lambda c: c.wait())

    pos_t = jnp.transpose(rt_ref[...])
    srow = lax.broadcasted_iota(jnp.int32, (SROWS, tm), 0).astype(F32)
    sel = jnp.where(srow == pos_t[2:3, :], 1.0, jnp.where(srow == pos_t[3:4, :], 1.0, 0.0)).astype(BF16)
    srt = jnp.dot(sel, hb_ref[...], preferred_element_type=F32)
    bits = pltpu.bitcast(srt, jnp.uint32)
    sbuf[slot] = (bits[:, d // 2:] & jnp.uint32(0xFFFF0000)) | (bits[:, :d // 2] >> 16)

    def chunk_copy(run_row, sorted_row, rows, sl):
        return pltpu.make_async_copy(sbuf.at[sl, pl.ds(sorted_row, rows)], xs_ref.at[pl.ds(run_row, rows)],
                                     sem.at[sl])

    _for_each_chunk(cur_ref, lambda run_row, sorted_row, rows, priority:
                    chunk_copy(run_row, sorted_row, rows, slot).start(priority=priority))

    @pl.when(i > 0)
    def _():
        _wait_chunks(prv_ref, lambda rows: chunk_copy(0, 0, rows, 1 - slot))

    @pl.when(i == nt - 1)
    def _():
        _wait_chunks(cur_ref, lambda rows: chunk_copy(0, 0, rows, slot))
        for_zero_blocks(2, lambda c: c.wait())


def _dispatch(zero_blocks, runs, hb, rt, n_rows):
    n, d = hb.shape
    return pl.pallas_call(
        _dispatch_kernel,
        out_shape=jax.ShapeDtypeStruct((n_rows, d // 2), jnp.uint32),
        grid_spec=pltpu.PrefetchScalarGridSpec(
            num_scalar_prefetch=1,
            grid=(n // TM,),
            in_specs=[
                pl.BlockSpec((1, 1, LANES), lambda i, zf: (i, 0, 0), memory_space=pltpu.SMEM),
                pl.BlockSpec((1, 1, LANES), lambda i, zf: (jnp.maximum(i - 1, 0), 0, 0), memory_space=pltpu.SMEM),
                pl.BlockSpec((TM, d), lambda i, zf: (i, 0)),
                pl.BlockSpec((TM, LANES), lambda i, zf: (i, 0)),
            ],
            out_specs=pl.BlockSpec(memory_space=pl.ANY),
            scratch_shapes=[pltpu.VMEM((2, SROWS, d // 2), jnp.uint32), pltpu.VMEM((EB, d // 2), jnp.uint32),
                            pltpu.SemaphoreType.DMA((2,)), pltpu.SemaphoreType.DMA((2,))],
        ),
        compiler_params=_cparams(("arbitrary",)),
        name="dispatch",
    )(zero_blocks, runs, runs, hb, rt)


def _experts_kernel(be_ref, na_ref, nxt_ref, xs_ref, wg_hbm, wu_hbm, wd_hbm, ys_ref,
                    wgf, wuf, wdf, wgb, wub, wdb, sem):
    b = pl.program_id(0)

    def weight_copies(e):
        return (pltpu.make_async_copy(wg_hbm.at[e], wgf, sem.at[0]),
                pltpu.make_async_copy(wu_hbm.at[e], wuf, sem.at[1]),
                pltpu.make_async_copy(wd_hbm.at[e], wdf, sem.at[2]))

    @pl.when(b == 0)
    def _():
        for c in weight_copies(be_ref[0]):
            c.start()

    @pl.when(b < na_ref[0])
    def _():
        e = be_ref[b]
        changed = jnp.logical_or(b == 0, be_ref[jnp.maximum(b - 1, 0)] != e)

        @pl.when(changed)
        def _load():
            for c in weight_copies(e):
                c.wait()
            wgb[...] = wgf[...].astype(BF16)
            wub[...] = wuf[...].astype(BF16)
            wdb[...] = wdf[...].astype(BF16)
            nxt = nxt_ref[e]

            @pl.when(nxt >= 0)
            def _():
                for c in weight_copies(nxt):
                    c.start()

        w = xs_ref[...]
        x_lo = pltpu.bitcast(w << 16, F32).astype(BF16)
        x_hi = pltpu.bitcast(w & jnp.uint32(0xFFFF0000), F32).astype(BF16)
        dh = w.shape[1]
        g = (jnp.dot(x_lo, wgb[:dh, :], preferred_element_type=F32)
             + jnp.dot(x_hi, wgb[dh:, :], preferred_element_type=F32))
        u = (jnp.dot(x_lo, wub[:dh, :], preferred_element_type=F32)
             + jnp.dot(x_hi, wub[dh:, :], preferred_element_type=F32))
        hdn = g * (1.0 / (1.0 + jnp.exp(-g))) * u
        y = jnp.dot(hdn.astype(BF16), wdb[...], preferred_element_type=F32)
        bits = pltpu.bitcast(y.astype(BF16).astype(F32), jnp.uint32)
        ys_ref[...] = (bits[:, dh:] & jnp.uint32(0xFFFF0000)) | (bits[:, :dh] >> 16)

    @pl.when(b >= na_ref[0])
    def _():
        ys_ref[...] = jnp.zeros(ys_ref.shape, ys_ref.dtype)


def _experts(blk_e, n_act, nxt_e, xs, w_gate, w_up, w_down):
    p, dh = xs.shape
    d = 2 * dh
    de = w_gate.shape[2]

    def row_map(b, be, na, nx):
        return (jnp.minimum(b, na[0] - 1), 0)

    return pl.pallas_call(
        _experts_kernel,
        out_shape=jax.ShapeDtypeStruct((p, dh), jnp.uint32),
        grid_spec=pltpu.PrefetchScalarGridSpec(
            num_scalar_prefetch=3,
            grid=(p // EB,),
            in_specs=[
                pl.BlockSpec((EB, dh), row_map),
                pl.BlockSpec(memory_space=pl.ANY),
                pl.BlockSpec(memory_space=pl.ANY),
                pl.BlockSpec(memory_space=pl.ANY),
            ],
            out_specs=pl.BlockSpec((EB, dh), lambda b, be, na, nx: (b, 0)),
            scratch_shapes=[pltpu.VMEM((d, de), F32), pltpu.VMEM((d, de), F32), pltpu.VMEM((de, d), F32),
                            pltpu.VMEM((d, de), BF16), pltpu.VMEM((d, de), BF16), pltpu.VMEM((de, d), BF16),
                            pltpu.SemaphoreType.DMA((3,))],
        ),
        compiler_params=_cparams(("arbitrary",)),
        name="experts",
    )(blk_e, n_act, nxt_e, xs, w_gate, w_up, w_down)


def _combine_kernel(cur_ref, nxt_ref, ys_ref, h_ref, rt_ref, o_ref, ybuf, sem):
    i = pl.program_id(0)
    nt = pl.num_programs(0)
    slot = i % 2
    tm = h_ref.shape[0]

    def chunk_copy(run_row, sorted_row, rows, sl):
        return pltpu.make_async_copy(ys_ref.at[pl.ds(run_row, rows)], ybuf.at[sl, pl.ds(sorted_row, rows)],
                                     sem.at[sl])

    @pl.when(i == 0)
    def _():
        ybuf[...] = jnp.zeros(ybuf.shape, ybuf.dtype)
        _for_each_chunk(cur_ref, lambda run_row, sorted_row, rows, priority:
                        chunk_copy(run_row, sorted_row, rows, 0).start(priority=priority))

    @pl.when(i + 1 < nt)
    def _():
        _for_each_chunk(nxt_ref, lambda run_row, sorted_row, rows, priority:
                        chunk_copy(run_row, sorted_row, rows, 1 - slot).start(priority=priority))

    _wait_chunks(cur_ref, lambda rows: chunk_copy(0, 0, rows, slot))

    rt = rt_ref[...]
    w = ybuf[slot]
    dh = w.shape[1]
    y_lo = pltpu.bitcast(w << 16, F32).astype(BF16)
    y_hi = pltpu.bitcast(w & jnp.uint32(0xFFFF0000), F32).astype(BF16)
    col = lax.broadcasted_iota(jnp.int32, (tm, SROWS), 1).astype(F32)
    wsel = jnp.where(col == rt[:, 2:3], rt[:, 0:1], jnp.where(col == rt[:, 3:4], rt[:, 1:2], 0.0)).astype(BF16)
    for half, yb in ((slice(0, dh), y_lo), (slice(dh, 2 * dh), y_hi)):
        o_ref[:, half] = h_ref[:, half] + jnp.dot(wsel, yb, preferred_element_type=F32)


def _combine(runs, ys, h1, rt):
    n, d = h1.shape
    nt = n // TM
    return pl.pallas_call(
        _combine_kernel,
        out_shape=jax.ShapeDtypeStruct((n, d), F32),
        grid=(nt,),
        in_specs=[
            pl.BlockSpec((1, 1, LANES), lambda i: (i, 0, 0), memory_space=pltpu.SMEM),
            pl.BlockSpec((1, 1, LANES), lambda i: (jnp.minimum(i + 1, nt - 1), 0, 0), memory_space=pltpu.SMEM),
            pl.BlockSpec(memory_space=pl.ANY),
            pl.BlockSpec((TM, d), lambda i: (i, 0)),
            pl.BlockSpec((TM, LANES), lambda i: (i, 0)),
        ],
        out_specs=pl.BlockSpec((TM, d), lambda i: (i, 0)),
        scratch_shapes=[pltpu.VMEM((2, SROWS, d // 2), jnp.uint32), pltpu.SemaphoreType.DMA((2,))],
        compiler_params=_cparams(("arbitrary",)),
        name="combine",
    )(runs, runs, ys, h1, rt)


def kernel(x, meta_tokens, rel_bias, norm1_gain, w_in, diff_q_gain, diff_k_gain, lam_q1, lam_k1, lam_q2, lam_k2, diff_subln_gain, swa_q_gain, swa_k_gain, swa_sinks, w_out, norm2_gain, w_group, b_group, w_router, b_router, w_gate, w_up, w_down):
    batch, seq, d = x.shape
    depth = w_in.shape[0]
    n = batch * seq
    assert seq % TQ == 0 and n % TM == 0 and n % TP == 0 and d == 1024
    assert meta_tokens.shape[0] == N_META
    assert depth == 1, "the meta-token rows of the residual stream are not carried across layers"

    h = x.reshape(n, d)
    dblk, bm0, bt = _bias_tables(rel_bias, TQ)
    scale = HEAD_DIM ** -0.5
    bd = jnp.asarray(np.kron(np.eye(MXU_DIM // HEAD_DIM), np.full((HEAD_DIM, HEAD_DIM), 1.0 / HEAD_DIM)), BF16)
    ones = jnp.ones((HEAD_DIM,), F32)
    lower_pad = N_EXPERTS + N_GROUPS

    for layer in range(depth):
        lambda_init = 0.8 - 0.6 * math.exp(-0.3 * layer)
        w_cat = w_in[layer].astype(BF16)
        gain = jnp.concatenate([
            jnp.tile(diff_q_gain[layer] * (scale * LOG2E), 2 * N_DIFF_HEADS),
            jnp.tile(diff_k_gain[layer], 2 * N_DIFF_HEADS),
            jnp.tile(ones, 2 * N_DIFF_HEADS),
            jnp.tile(swa_q_gain[layer] * scale, N_SWA_HEADS),
            jnp.tile(swa_k_gain[layer], N_SWA_KV),
            jnp.tile(ones, N_SWA_KV)]).reshape(1, C_END).astype(F32)
        nmask = np.zeros((1, C_END), np.float32)
        nmask[:, C_DQ:C_DV] = 1.0
        nmask[:, C_SQ:C_SV] = 1.0
        nmask = jnp.asarray(nmask)
        g1 = norm1_gain[layer].reshape(1, d).astype(F32)

        qkv = _proj(h, g1, w_cat, bd, gain, nmask, TP)
        qkv_meta = _proj(meta_tokens.astype(F32), g1, w_cat, bd, gain, nmask, N_META)
        meta_pad = jnp.pad(qkv_meta, ((0, TQ - N_META), (0, 0)))

        lamv = jnp.pad(jnp.stack([lam_q1[layer], lam_k1[layer], lam_q2[layer], lam_k2[layer]]).astype(F32),
                       ((0, 4), (0, LANES - HEAD_DIM)))
        mixd = _diff_attention(qkv, meta_pad[:, C_DK:C_DV], meta_pad[:, C_DV:C_SQ], dblk, bm0, lamv,
                               diff_subln_gain[layer].reshape(1, LANES).astype(F32), batch, seq, lambda_init)
        mixs = _swa_attention(swa_sinks[layer].astype(F32), qkv, meta_pad[:BLOCK, C_SK:C_SV],
                              meta_pad[:BLOCK, C_SV:C_END], jnp.swapaxes(bt, -1, -2), batch, seq)

        wr = jnp.pad(jnp.concatenate([w_router[layer], w_group[layer]], axis=1),
                     ((0, 0), (0, LANES - lower_pad))).astype(BF16)
        br = jnp.pad(jnp.concatenate([b_router[layer], b_group[layer]]), (0, LANES - lower_pad)).reshape(1, LANES)
        h1, hb, rt, tinfo, cnt = _outproj(h, mixd, mixs, w_out[layer].astype(BF16),
                                          norm2_gain[layer].reshape(1, d).astype(F32), wr, br.astype(F32))

        nt = n // TM
        counts = cnt[0, :N_EXPERTS].astype(jnp.int32)
        nblk_e = (counts + EB - 1) // EB
        blk_end = jnp.cumsum(nblk_e)
        pstart = ((blk_end - nblk_e) * EB).astype(jnp.int32)
        n_blocks = -(-(2 * n + nt * N_EXPERTS * (ROW_ALIGN - 1) + N_EXPERTS * (EB - 1)) // EB)
        blk_ids = jnp.arange(n_blocks)
        blk_e = jnp.minimum(jnp.sum(blk_end[None, :] <= blk_ids[:, None], axis=1), N_EXPERTS - 1).astype(jnp.int32)
        n_act = blk_end[-1:].astype(jnp.int32)
        is_last = jnp.any((blk_end[None, :] == blk_ids[:, None] + 1) & (nblk_e[None, :] > 0), axis=1)
        zero_blocks = jnp.where(blk_ids >= n_act[0], 2, jnp.where(is_last, 1, 0)).astype(jnp.int32)
        ti = tinfo.reshape(nt, 8, LANES)
        run_len = ti[:, 0, :N_EXPERTS].astype(jnp.int32)
        run_start = pstart[None, :] + ti[:, 1, :N_EXPERTS].astype(jnp.int32)
        run_groups = (run_len + ROW_ALIGN - 1) // ROW_ALIGN
        n_copies = [jnp.sum(run_groups // (CHUNK_ROWS[0] // ROW_ALIGN), axis=1, keepdims=True)]
        n_copies += [jnp.sum((run_groups // (rows // ROW_ALIGN)) % 2, axis=1, keepdims=True)
                     for rows in CHUNK_ROWS[1:]]
        runs = jnp.concatenate([run_start, run_groups] + n_copies
                               + [jnp.zeros((nt, LANES - 2 * N_EXPERTS - len(CHUNK_ROWS)), jnp.int32)],
                               axis=1).reshape(nt, 1, LANES)

        xs = _dispatch(zero_blocks, runs, hb, rt, n_blocks * EB)
        own = jnp.where(nblk_e > 0, jnp.arange(N_EXPERTS), N_EXPERTS)
        later = jnp.concatenate([lax.cummin(own[::-1])[::-1][1:], jnp.full((1,), N_EXPERTS)])
        nxt_e = jnp.where(later < N_EXPERTS, later, -1).astype(jnp.int32)
        ys = _experts(blk_e, n_act, nxt_e, xs, w_gate[layer], w_up[layer], w_down[layer])
        h = _combine(runs, ys, h1, rt)
    return h.reshape(batch, seq, d)
```

```python
import functools
import math

import numpy as np
import jax
import jax.numpy as jnp
from jax import lax
from jax.experimental import pallas as pl
from jax.experimental.pallas import tpu as pltpu

F32 = jnp.float32
BF16 = jnp.bfloat16

HEAD_DIM = 64
N_DIFF_HEADS = 4
N_SWA_HEADS = 8
N_SWA_KV = 2
BLOCK = 128
N_META = 16
N_BUCKETS = 32
MAX_DISTANCE = 128
N_GROUPS = 4
EXPERTS_PER_GROUP = 8
N_EXPERTS = N_GROUPS * EXPERTS_PER_GROUP
D_EXPERT = 512
EPS = 1e-6
NEG = -1e30
LOG2E = math.log2(math.e)

LANES = 128
MXU_DIM = 256
VMEM_LIMIT = 48 * 1024 * 1024

TP = 512
TM = 512
TQ = 256
ONES_ROWS = 16
EB = 512
ROW_ALIGN = 8
CHUNK_ROWS = (32, 16, 8)
SROWS = -(-(2 * TM + N_EXPERTS * (ROW_ALIGN - 1)) // MXU_DIM) * MXU_DIM

C_DQ, C_DK, C_DV, C_SQ, C_SK, C_SV, C_END = 0, 512, 1024, 1536, 2048, 2176, 2304
NORM_GROUPS = (0, 1, 2, 3, 6, 7, 8)


def _cparams(sem):
    return pltpu.CompilerParams(dimension_semantics=sem, vmem_limit_bytes=VMEM_LIMIT)


def _t5_bucket_np(dist):
    n = np.maximum(dist, 0)
    max_exact = N_BUCKETS // 2
    nf = np.maximum(n, 1).astype(np.float32)
    large = max_exact + (np.log(nf / np.float32(max_exact)) / np.float32(math.log(MAX_DISTANCE / max_exact))
                         * np.float32(N_BUCKETS - max_exact)).astype(np.int32)
    large = np.minimum(large, N_BUCKETS - 1)
    return np.where(n < max_exact, n, large)


def _bias_tables(rel_bias, tq):
    nd = 2 * BLOCK
    buckets = _t5_bucket_np(np.arange(nd))
    assert (buckets[MAX_DISTANCE:] == N_BUCKETS - 1).all()
    rb = rel_bias.astype(F32)
    r = np.arange(BLOCK)[:, None]
    c = np.arange(BLOCK)[None, :]
    d_own = r - c
    d_prev = BLOCK + r - c
    far = rb[N_BUCKETS - 1]

    def take(dist):
        idx = jnp.asarray(buckets[np.clip(dist, 0, nd - 1)], jnp.int32)[None]
        out = jnp.zeros((rb.shape[1],) + dist.shape, F32)
        for b in range(N_BUCKETS):
            out = jnp.where(idx == b, rb[b].reshape((-1,) + (1,) * dist.ndim), out)
        return out

    hd = slice(0, N_DIFF_HEADS)
    far_d = far[hd][:, None, None]
    d0 = jnp.where(d_own[None] >= 0, take(d_own)[hd] - far_d, NEG)
    d1 = take(d_prev)[hd] - far_d
    dblk = jnp.stack([d0, d1], axis=1)
    rq = np.arange(tq)[:, None]
    cm = np.arange(LANES)[None, :]
    d_meta = N_META + rq - cm
    bm0 = jnp.where((cm < N_META)[None], take(d_meta)[hd] - far_d, NEG)

    hs = slice(N_DIFF_HEADS, N_DIFF_HEADS + N_SWA_HEADS)
    far_s = far[hs][:, None, None]
    d_meta_s = N_META + r - cm
    meta_first = jnp.where((cm < N_META)[None], take(d_meta_s)[hs], NEG)
    meta_rest = jnp.where((cm < N_META)[None], jnp.broadcast_to(far_s, (N_SWA_HEADS, BLOCK, LANES)), NEG)
    prev_rest = jnp.where((c > r)[None], take(d_prev)[hs], NEG)
    prev_first = jnp.full((N_SWA_HEADS, BLOCK, BLOCK), NEG, F32)
    own = jnp.where((d_own >= 0)[None], take(d_own)[hs], NEG)
    bt = jnp.stack([jnp.concatenate([prev_first, own, meta_first[..., :N_META]], axis=-1),
                    jnp.concatenate([prev_rest, own, meta_rest[..., :N_META]], axis=-1)], axis=0)
    return dblk.astype(F32), bm0.astype(F32), bt.astype(F32)


def _proj_kernel(x_ref, g1_ref, w_ref, bd_ref, gain_ref, nmask_ref, o_ref):
    x = x_ref[...]
    a = x * lax.rsqrt(jnp.mean(x * x, axis=-1, keepdims=True) + EPS) * g1_ref[...]
    p = jnp.dot(a.astype(BF16), w_ref[...], preferred_element_type=F32)
    bd = bd_ref[...]
    for j in range(C_END // MXU_DIM):
        sl = slice(j * MXU_DIM, (j + 1) * MXU_DIM)
        pj = p[:, sl]
        if j in NORM_GROUPS:
            ms = jnp.dot((pj * pj).astype(BF16), bd, preferred_element_type=F32)
            pj = jnp.where(nmask_ref[:, sl] != 0.0, pj * lax.rsqrt(ms + EPS) * gain_ref[:, sl], pj)
        o_ref[:, sl] = pj.astype(BF16)


def _proj(x2, g1, w, bd, gain, nmask, tm):
    n = x2.shape[0]
    return pl.pallas_call(
        _proj_kernel,
        out_shape=jax.ShapeDtypeStruct((n, C_END), BF16),
        grid=(n // tm,),
        in_specs=[
            pl.BlockSpec((tm, x2.shape[1]), lambda i: (i, 0)),
            pl.BlockSpec(g1.shape, lambda i: (0, 0)),
            pl.BlockSpec(w.shape, lambda i: (0, 0)),
            pl.BlockSpec(bd.shape, lambda i: (0, 0)),
            pl.BlockSpec(gain.shape, lambda i: (0, 0)),
            pl.BlockSpec(nmask.shape, lambda i: (0, 0)),
        ],
        out_specs=pl.BlockSpec((tm, C_END), lambda i: (i, 0)),
        compiler_params=_cparams(("parallel",)),
        name="proj",
    )(x2, g1, w, bd, gain, nmask)


def _diff_kernel(qi_tab, t_tab, q_ref, k_ref, v_ref, km_ref, vm_ref, d_ref, bm0_ref, lamv_ref, gain_ref, o_ref,
                 bias_ref, mb_ref, qs_ref, kt_ref, vt_ref, s_buf, p_buf, a_buf, m_ref, acc_ref, *,
                 lambda_init, n_steps, n_far, n_near):
    tq = TQ
    nq = q_ref.shape[0] // tq
    nb = tq // BLOCK
    BIAS_LEFT, BIAS_DIAG = 0, 1

    d0 = d_ref[0, 0] * LOG2E
    d1 = d_ref[0, 1] * LOG2E
    zeros = jnp.zeros((BLOCK, BLOCK), F32)
    for a in range(nb):
        for b in range(nb):
            rs, cs = slice(a * BLOCK, (a + 1) * BLOCK), slice(b * BLOCK, (b + 1) * BLOCK)
            if a == b:
                blk = d0
            elif b == a + 1:
                blk = d1
            elif b > a:
                blk = zeros
            else:
                blk = jnp.full((BLOCK, BLOCK), NEG, F32)
            bias_ref[BIAS_DIAG, rs, cs] = blk
            bias_ref[BIAS_LEFT, rs, cs] = d1 if (b == 0 and a == nb - 1) else zeros
    mb_ref[0] = jnp.zeros((N_META, tq), F32)
    mb_ref[1] = bm0_ref[0, :N_META, :] * LOG2E

    lane = lax.broadcasted_iota(jnp.int32, (tq, LANES), 1)
    for i in range(nq):
        rows = slice(i * tq, (i + 1) * tq)
        q = q_ref[rows, :].astype(F32)
        qs_ref[i] = jnp.transpose(jnp.concatenate([jnp.where(lane < HEAD_DIM, q, 0.0),
                                                   jnp.where(lane >= HEAD_DIM, q, 0.0)], axis=0)).astype(BF16)
        vt_ref[i, :LANES, :] = jnp.transpose(v_ref[rows, :].astype(F32)).astype(BF16)
        kt_ref[i] = k_ref[rows, :]
    vt_ref[nq, :LANES, :] = jnp.transpose(vm_ref[...].astype(F32)).astype(BF16)
    kt_ref[nq] = km_ref[...]
    vt_ref[:, LANES:, :] = jnp.ones((nq + 1, ONES_ROWS, tq), BF16)
    acc_ref[...] = jnp.zeros(acc_ref.shape, F32)
    m_ref[...] = jnp.full(m_ref.shape, NEG, F32)
    lv = lamv_ref[...]
    lam = (jnp.exp(jnp.sum(lv[0:1] * lv[1:2], axis=-1, keepdims=True))
           - jnp.exp(jnp.sum(lv[2:3] * lv[3:4], axis=-1, keepdims=True)) + lambda_init)

    FAR, NEAR, META = 0, 1, 2

    def stage_a(n, slot, kind):
        qi, t = qi_tab[n], t_tab[n]
        if kind == META:
            s = jnp.dot(kt_ref[nq, :N_META, :], qs_ref[qi], preferred_element_type=F32)
            s_buf[slot, :N_META] = s + jnp.tile(mb_ref[jnp.where(qi == 0, 1, 0)], (1, 2))
            return
        s = jnp.dot(kt_ref[t - 1], qs_ref[qi], preferred_element_type=F32)
        if kind == NEAR:
            s = s + jnp.tile(bias_ref[jnp.where(t == qi + 1, BIAS_DIAG, BIAS_LEFT)], (1, 2))
        s_buf[slot] = s

    def stage_b(n, slot, kind):
        qi = qi_tab[n]
        rows = slice(0, N_META if kind == META else tq)
        s = s_buf[slot, rows]
        m_prev = m_ref[qi]
        m_new = jnp.maximum(m_prev, jnp.max(s, axis=0, keepdims=True))
        a_buf[slot] = jnp.exp2(m_prev - m_new)
        p_buf[slot, rows] = jnp.exp2(s - m_new[0:1]).astype(BF16)
        m_ref[qi] = m_new

    def stage_c(n, slot, kind):
        qi, t = qi_tab[n], t_tab[n]
        if kind == META:
            pv = jnp.dot(vt_ref[nq, :, :N_META], p_buf[slot, :N_META], preferred_element_type=F32)
        else:
            pv = jnp.dot(vt_ref[t - 1], p_buf[slot], preferred_element_type=F32)
        acc_ref[qi] = a_buf[slot][0:1] * acc_ref[qi] + pv

    def pipeline(base, count, kind):
        LEAD, SLOTS, unroll = 2, 3, 12
        assert unroll % SLOTS == 0
        if count <= 2 * LEAD:
            for j in range(count):
                stage_a(base + j, 0, kind)
                stage_b(base + j, 0, kind)
                stage_c(base + j, 0, kind)
            return
        for j in range(2 * LEAD):
            stage_a(base + j, j % SLOTS, kind)
            if j >= LEAD:
                stage_b(base + j - LEAD, (j - LEAD) % SLOTS, kind)

        def steps(n, count):
            for j in range(count):
                stage_a(base + n + j + 2 * LEAD, (j + 2 * LEAD) % SLOTS, kind)
                stage_b(base + n + j + LEAD, (j + LEAD) % SLOTS, kind)
                stage_c(base + n + j, j % SLOTS, kind)

        n_steady = count - 2 * LEAD
        n_blocks = n_steady // unroll

        def block(k, carry):
            steps(unroll * k, unroll)
            return carry
        lax.fori_loop(0, n_blocks, block, 0)
        steps(unroll * n_blocks, n_steady - unroll * n_blocks)
        for j in range(n_steady, count):
            if j + LEAD < count:
                stage_b(base + j + LEAD, (j + LEAD) % SLOTS, kind)
            stage_c(base + j, j % SLOTS, kind)

    pipeline(0, n_far, FAR)
    pipeline(n_far, n_near, NEAR)
    pipeline(n_far + n_near, n_steps - n_far - n_near, META)

    for i in range(nq):
        acc = acc_ref[i]
        o = acc[:LANES] * (1.0 / acc[LANES:LANES + 1])
        d = o[:, :tq] - lam * o[:, tq:]
        y = d * lax.rsqrt(jnp.mean(d * d, axis=0, keepdims=True) + EPS) * jnp.tile(gain_ref[...], (1, tq // LANES))
        o_ref[i * tq:(i + 1) * tq, :] = jnp.transpose(y * (1.0 - lambda_init)).astype(BF16)


def _diff_attention(qkv, km, vm, dblk, bm0, lamv, gain, batch, seq, lambda_init):
    nq = seq // TQ
    far = [(qi, t) for qi in range(nq) for t in range(1, qi)]
    near = [(qi, t) for qi in range(nq) for t in (qi, qi + 1) if t >= 1]
    meta = [(qi, 0) for qi in range(nq)]
    steps = far + near + meta
    qi_tab = jnp.asarray([s[0] for s in steps], jnp.int32)
    t_tab = jnp.asarray([s[1] for s in steps], jnp.int32)
    kern = functools.partial(_diff_kernel, lambda_init=lambda_init, n_steps=len(steps), n_far=len(far),
                             n_near=len(near))
    return pl.pallas_call(
        kern,
        out_shape=jax.ShapeDtypeStruct((batch * seq, N_DIFF_HEADS * LANES), BF16),
        grid_spec=pltpu.PrefetchScalarGridSpec(
            num_scalar_prefetch=2,
            grid=(batch, N_DIFF_HEADS),
            in_specs=[
                pl.BlockSpec((seq, LANES), lambda b, h, *_: (b, C_DQ // LANES + h)),
                pl.BlockSpec((seq, LANES), lambda b, h, *_: (b, C_DK // LANES + h)),
                pl.BlockSpec((seq, LANES), lambda b, h, *_: (b, C_DV // LANES + h)),
                pl.BlockSpec((TQ, LANES), lambda b, h, *_: (0, h)),
                pl.BlockSpec((TQ, LANES), lambda b, h, *_: (0, h)),
                pl.BlockSpec((1, 2, BLOCK, BLOCK), lambda b, h, *_: (h, 0, 0, 0)),
                pl.BlockSpec((1, LANES, TQ), lambda b, h, *_: (h, 0, 0)),
                pl.BlockSpec(lamv.shape, lambda b, h, *_: (0, 0)),
                pl.BlockSpec((LANES, LANES), lambda b, h, *_: (0, 0)),
            ],
            out_specs=pl.BlockSpec((seq, LANES), lambda b, h, *_: (b, h)),
            scratch_shapes=[
                pltpu.VMEM((2, TQ, TQ), F32),
                pltpu.VMEM((2, N_META, TQ), F32),
                pltpu.VMEM((nq, LANES, 2 * TQ), BF16),
                pltpu.VMEM((nq + 1, TQ, LANES), BF16),
                pltpu.VMEM((nq + 1, LANES + ONES_ROWS, TQ), BF16),
                pltpu.VMEM((3, TQ, 2 * TQ), F32),
                pltpu.VMEM((3, TQ, 2 * TQ), BF16),
                pltpu.VMEM((3, 8, 2 * TQ), F32),
                pltpu.VMEM((nq, 8, 2 * TQ), F32),
                pltpu.VMEM((nq, LANES + ONES_ROWS, 2 * TQ), F32),
            ],
        ),
        compiler_params=_cparams(("parallel", "parallel")),
        name="diff_attention",
    )(qi_tab, t_tab, qkv, qkv, qkv, km, vm, jnp.swapaxes(dblk, -1, -2), jnp.swapaxes(bm0, -1, -2), lamv,
      jnp.broadcast_to(gain.reshape(LANES, 1), (LANES, LANES)))


def _swa_kernel(sink_ref, q_ref, k_ref, v_ref, km_ref, vm_ref, bt_ref, o_ref, kd_ref, vt_ref,
                s_scr, p_scr, inv_scr):
    nkb = k_ref.shape[0] // BLOCK
    lane = lax.broadcasted_iota(jnp.int32, (BLOCK, LANES), 1)
    pairs = [(g, u) for g in range(N_SWA_KV) for u in range(2)]

    def both_halves(k):
        k0, k1 = k[:, :HEAD_DIM], k[:, HEAD_DIM:]
        return jnp.concatenate([k0, k0, k1, k1], axis=1)

    def prepare(j, carry):
        rows = pl.ds(pl.multiple_of(j * BLOCK, BLOCK), BLOCK)
        kd_ref[j] = both_halves(k_ref[rows, :])
        vt_ref[j] = jnp.transpose(v_ref[rows, :].astype(F32)).astype(BF16)
        return carry
    lax.fori_loop(0, nkb, prepare, 0)
    kd_ref[nkb] = both_halves(km_ref[...])
    vt_ref[nkb] = jnp.transpose(vm_ref[...].astype(F32)).astype(BF16)

    def scores(n, slot):
        first = jnp.where(n == 0, 0, 1)
        prev = jnp.maximum(n - 1, 0)
        r_q = pl.multiple_of(n * BLOCK, BLOCK)
        for c, (g, u) in enumerate(pairs):
            ks = slice(g * LANES, (g + 1) * LANES)
            kcat = jnp.concatenate([kd_ref[prev, :, ks], kd_ref[n, :, ks], kd_ref[nkb, :N_META, ks]], axis=0)
            h0 = 4 * g + 2 * u
            qp = q_ref[pl.ds(r_q, BLOCK), (2 * g + u) * LANES:(2 * g + u + 1) * LANES].astype(F32)
            qs = jnp.transpose(jnp.concatenate([jnp.where(lane < HEAD_DIM, qp, 0.0),
                                                jnp.where(lane >= HEAD_DIM, qp, 0.0)], axis=0)).astype(BF16)
            s = jnp.dot(kcat, qs, preferred_element_type=F32)
            s_scr[slot, c] = s + jnp.concatenate([bt_ref[first, h0], bt_ref[first, h0 + 1]], axis=1)

    def exponentials(slot):
        for c, (g, u) in enumerate(pairs):
            h0 = 4 * g + 2 * u
            s = s_scr[slot, c]
            sink = jnp.concatenate([sink_ref[h0:h0 + 1, :], sink_ref[h0 + 1:h0 + 2, :]], axis=1)
            m = jnp.maximum(jnp.max(s, axis=0, keepdims=True), sink)
            p = jnp.exp(s - m)
            p_scr[slot, c] = p.astype(BF16)
            inv_scr[slot, c] = jnp.broadcast_to(1.0 / (jnp.sum(p, axis=0, keepdims=True) + jnp.exp(sink - m)),
                                                inv_scr.shape[2:])

    def values(n, slot):
        prev = jnp.maximum(n - 1, 0)
        r_q = pl.multiple_of(n * BLOCK, BLOCK)
        for c, (g, u) in enumerate(pairs):
            vs = slice(g * HEAD_DIM, (g + 1) * HEAD_DIM)
            vcat = jnp.concatenate([vt_ref[prev, vs, :], vt_ref[n, vs, :]], axis=1)
            o = (jnp.dot(vcat, p_scr[slot, c, :2 * BLOCK], preferred_element_type=F32)
                 + jnp.dot(vt_ref[nkb, vs, :N_META], p_scr[slot, c, 2 * BLOCK:], preferred_element_type=F32)
                 ) * inv_scr[slot, c][0:1]
            ot = jnp.transpose(o)
            o_ref[pl.ds(r_q, BLOCK), (2 * g + u) * LANES:(2 * g + u + 1) * LANES] = (
                jnp.concatenate([ot[:BLOCK], ot[BLOCK:]], axis=1).astype(BF16))

    scores(0, 0)
    scores(1, 1)
    exponentials(0)

    def two_blocks(k, carry):
        n = 2 * k
        scores(n + 2, 0)
        exponentials(1)
        values(n, 0)
        scores(n + 3, 1)
        exponentials(0)
        values(n + 1, 1)
        return carry
    lax.fori_loop(0, nkb // 2 - 1, two_blocks, 0)
    exponentials(1)
    values(nkb - 2, 0)
    values(nkb - 1, 1)


def _swa_attention(sinks, qkv, km, vm, bt, batch, seq):
    nkb = seq // BLOCK
    assert nkb % 2 == 0
    sinkv = jnp.broadcast_to(sinks.reshape(N_SWA_HEADS, 1), (N_SWA_HEADS, LANES))
    return pl.pallas_call(
        _swa_kernel,
        out_shape=jax.ShapeDtypeStruct((batch * seq, N_SWA_HEADS * HEAD_DIM), BF16),
        grid=(batch,),
        in_specs=[
            pl.BlockSpec(sinkv.shape, lambda b: (0, 0)),
            pl.BlockSpec((seq, 512), lambda b: (b, C_SQ // 512)),
            pl.BlockSpec((seq, LANES), lambda b: (b, C_SK // LANES)),
            pl.BlockSpec((seq, LANES), lambda b: (b, C_SV // LANES)),
            pl.BlockSpec(km.shape, lambda b: (0, 0)),
            pl.BlockSpec(vm.shape, lambda b: (0, 0)),
            pl.BlockSpec(bt.shape, lambda b: (0, 0, 0, 0)),
        ],
        out_specs=pl.BlockSpec((seq, 512), lambda b: (b, 0)),
        scratch_shapes=[pltpu.VMEM((nkb + 1, BLOCK, 2 * LANES), BF16),
                        pltpu.VMEM((nkb + 1, LANES, BLOCK), BF16),
                        pltpu.VMEM((2, 4, 2 * BLOCK + N_META, 2 * BLOCK), F32),
                        pltpu.VMEM((2, 4, 2 * BLOCK + N_META, 2 * BLOCK), BF16),
                        pltpu.VMEM((2, 4, 8, 2 * BLOCK), F32)],
        compiler_params=_cparams(("parallel",)),
        name="swa_attention",
    )(sinkv, qkv, qkv, qkv, km, vm, bt)


def _outproj_kernel(x_ref, md_ref, ms_ref, wo_ref, g2_ref, wr_ref, br_ref,
                    h_ref, hb_ref, rt_ref, ti_ref, cnt_ref, c_ref, lg_ref):
    i = pl.program_id(0)

    @pl.when(i == 0)
    def _init():
        c_ref[...] = jnp.zeros(c_ref.shape, F32)
        lg_ref[...] = jnp.zeros(lg_ref.shape, F32)

    lg_prev = lg_ref[...]
    half = md_ref.shape[1]
    h = (x_ref[...]
         + jnp.dot(md_ref[...], wo_ref[:half, :], preferred_element_type=F32)
         + jnp.dot(ms_ref[...], wo_ref[half:, :], preferred_element_type=F32))
    h_ref[...] = h
    hn = h * lax.rsqrt(jnp.mean(h * h, axis=-1, keepdims=True) + EPS) * g2_ref[...]
    hb = hn.astype(BF16)
    hb_ref[...] = hb
    lg_ref[...] = jnp.dot(hb, wr_ref[...], preferred_element_type=F32) + br_ref[...]
    _route_tile(lg_prev, jnp.where(i > 0, 1.0, 0.0), rt_ref, ti_ref, c_ref)

    @pl.when(i == pl.num_programs(0) - 1)
    def _fin():
        cnt_ref[...] = c_ref[...]


def _route_tile(lg, live, rt_ref, ti_ref, c_ref):
    tm = lg.shape[0]
    lane_i = lax.broadcasted_iota(jnp.int32, lg.shape, 1)
    lane = lane_i.astype(F32)
    big = float(4 * LANES)
    is_g = (lane_i >= N_EXPERTS) & (lane_i < N_EXPERTS + N_GROUPS)
    glm = jnp.where(is_g, lg, -jnp.inf)
    gmax = jnp.max(glm, axis=1, keepdims=True)
    gidx = jnp.min(jnp.where(glm == gmax, lane, big), axis=1, keepdims=True) - N_EXPERTS
    gsum = jnp.sum(jnp.where(is_g, jnp.exp(lg - gmax), 0.0), axis=1, keepdims=True)
    g_w = 1.0 / gsum
    lane_grp = (lane_i >> 3).astype(F32)
    in_grp = (lane_i < N_EXPERTS) & (lane_grp == gidx)
    el = jnp.where(in_grp, lg, -jnp.inf)
    t1 = jnp.max(el, axis=1, keepdims=True)
    j1 = jnp.min(jnp.where(el == t1, lane, big), axis=1, keepdims=True)
    el2 = jnp.where(lane == j1, -jnp.inf, el)
    t2 = jnp.max(el2, axis=1, keepdims=True)
    j2 = jnp.min(jnp.where(el2 == t2, lane, big), axis=1, keepdims=True)
    e2 = jnp.exp(t2 - t1)
    den = 1.0 + e2
    gate1 = g_w / den
    gate2 = g_w * e2 / den

    o1 = lane == j1
    o2 = lane == j2
    onehot = jnp.where(o1 | o2, 1.0, 0.0).astype(BF16)
    rr = lax.broadcasted_iota(jnp.int32, (tm, tm), 0)
    cc = lax.broadcasted_iota(jnp.int32, (tm, tm), 1)
    lower = jnp.where(rr > cc, 1.0, 0.0).astype(BF16)
    pfx = jnp.dot(lower, onehot, preferred_element_type=F32)
    cnt_tile = jnp.sum(onehot.astype(F32), axis=0, keepdims=True)
    groups = jnp.floor((cnt_tile + (ROW_ALIGN - 1)) * (1.0 / ROW_ALIGN))
    er = lax.broadcasted_iota(jnp.int32, (LANES, LANES), 0)
    ec = lax.broadcasted_iota(jnp.int32, (LANES, LANES), 1)
    before = jnp.where(er < ec, 1.0, 0.0).astype(BF16)
    cbase = ROW_ALIGN * jnp.dot(jnp.broadcast_to(groups, (8, LANES)).astype(BF16), before,
                                preferred_element_type=F32)[0:1]
    at = pfx + cbase
    pos1 = jnp.sum(jnp.where(o1, at, 0.0), axis=1, keepdims=True)
    pos2 = jnp.sum(jnp.where(o2, at, 0.0), axis=1, keepdims=True)
    rt_ref[...] = jnp.where(lane_i == 0, gate1,
                            jnp.where(lane_i == 1, gate2,
                                      jnp.where(lane_i == 2, pos1,
                                                jnp.where(lane_i == 3, pos2, 0.0))))
    c_old = c_ref[...]
    c_ref[...] = c_old + groups * (ROW_ALIGN * live)
    row8 = lax.broadcasted_iota(jnp.int32, (8, LANES), 0)
    ti_ref[...] = jnp.where(row8 == 0, cnt_tile, jnp.where(row8 == 1, c_old, 0.0))


def _outproj(x2, mixd, mixs, wo, g2, wr, br):
    n, d = x2.shape
    nt = n // TM

    def proj_tile(i):
        return (jnp.minimum(i, nt - 1), 0)

    def route_tile(i):
        return (jnp.maximum(i - 1, 0), 0)

    return pl.pallas_call(
        _outproj_kernel,
        out_shape=(jax.ShapeDtypeStruct((n, d), F32),
                   jax.ShapeDtypeStruct((n, d), BF16),
                   jax.ShapeDtypeStruct((n, LANES), F32),
                   jax.ShapeDtypeStruct((nt * 8, LANES), F32),
                   jax.ShapeDtypeStruct((8, LANES), F32)),
        grid=(nt + 1,),
        in_specs=[
            pl.BlockSpec((TM, d), proj_tile),
            pl.BlockSpec((TM, mixd.shape[1]), proj_tile),
            pl.BlockSpec((TM, mixs.shape[1]), proj_tile),
            pl.BlockSpec(wo.shape, lambda i: (0, 0)),
            pl.BlockSpec(g2.shape, lambda i: (0, 0)),
            pl.BlockSpec(wr.shape, lambda i: (0, 0)),
            pl.BlockSpec(br.shape, lambda i: (0, 0)),
        ],
        out_specs=(pl.BlockSpec((TM, d), proj_tile),
                   pl.BlockSpec((TM, d), proj_tile),
                   pl.BlockSpec((TM, LANES), route_tile),
                   pl.BlockSpec((8, LANES), route_tile),
                   pl.BlockSpec((8, LANES), lambda i: (0, 0))),
        scratch_shapes=[pltpu.VMEM((8, LANES), F32), pltpu.VMEM((TM, LANES), F32)],
        compiler_params=_cparams(("arbitrary",)),
        name="outproj_router",
    )(x2, mixd, mixs, wo, g2, wr, br)


def _for_each_chunk(runs_ref, fn):
    big = CHUNK_ROWS[0]

    def per_expert(e, sorted_row, priority):
        start = runs_ref[0, 0, e]
        groups = runs_ref[0, 0, N_EXPERTS + e]
        whole = groups // (big // ROW_ALIGN)

        def per_chunk(c, carry):
            fn(pl.multiple_of(start + c * big, ROW_ALIGN), pl.multiple_of(sorted_row + c * big, ROW_ALIGN),
               big, priority)
            return carry
        lax.fori_loop(0, whole, per_chunk, 0)

        done = whole * big
        for rows in CHUNK_ROWS[1:]:
            has = (groups // (rows // ROW_ALIGN)) % 2

            @pl.when(has == 1)
            def _(done=done, rows=rows):
                fn(pl.multiple_of(start + done, ROW_ALIGN), pl.multiple_of(sorted_row + done, ROW_ALIGN),
                   rows, priority)
            done = done + has * rows
        return sorted_row + groups * ROW_ALIGN

    def expert_pair(e2, sorted_row):
        return per_expert(2 * e2 + 1, per_expert(2 * e2, sorted_row, 0), 1)
    lax.fori_loop(0, N_EXPERTS // 2, expert_pair, 0)


def _wait_chunks(runs_ref, make_copy):
    for k, rows in enumerate(CHUNK_ROWS):
        def body(c, carry, rows=rows):
            make_copy(rows).wait()
            return carry
        lax.fori_loop(0, runs_ref[0, 0, 2 * N_EXPERTS + k], body, 0)


def _dispatch_kernel(zf_ref, cur_ref, prv_ref, hb_ref, rt_ref, xs_ref, sbuf, zbuf, sem, zsem):
    i = pl.program_id(0)
    nt = pl.num_programs(0)
    slot = i % 2
    tm, d = hb_ref.shape

    def for_zero_blocks(kind, fn):
        def body(b, carry):
            @pl.when(zf_ref[b] == kind)
            def _():
                fn(pltpu.make_async_copy(zbuf, xs_ref.at[pl.ds(pl.multiple_of(b * EB, EB), EB)],
                                         zsem.at[kind - 1]))
            return carry
        lax.fori_loop(0, zf_ref.shape[0], body, 0)

    @pl.when(i == 0)
    def _():
        zbuf[...] = jnp.zeros(zbuf.shape, zbuf.dtype)
        for_zero_blocks(1, lambda c: c.start())
        for_zero_blocks(2, lambda c: c.start())
        for_zero_blocks(1, lambda c: c.wait())

    pos_t = jnp.transpose(rt_ref[...])
    srow = lax.broadcasted_iota(jnp.int32, (SROWS, tm), 0).astype(F32)
    sel = jnp.where(srow == pos_t[2:3, :], 1.0, jnp.where(srow == pos_t[3:4, :], 1.0, 0.0)).astype(BF16)
    srt = jnp.dot(sel, hb_ref[...], preferred_element_type=F32)
    bits = pltpu.bitcast(srt, jnp.uint32)
    sbuf[slot] = (bits[:, d // 2:] & jnp.uint32(0xFFFF0000)) | (bits[:, :d // 2] >> 16)

    def chunk_copy(run_row, sorted_row, rows, sl):
        return pltpu.make_async_copy(sbuf.at[sl, pl.ds(sorted_row, rows)], xs_ref.at[pl.ds(run_row, rows)],
                                     sem.at[sl])

    _for_each_chunk(cur_ref, lambda run_row, sorted_row, rows, priority:
                    chunk_copy(run_row, sorted_row, rows, slot).start(priority=priority))

    @pl.when(i > 0)
    def _():
        _wait_chunks(prv_ref, lambda rows: chunk_copy(0, 0, rows, 1 - slot))

    @pl.when(i == nt - 1)
    def _():
        _wait_chunks(cur_ref, lambda rows: chunk_copy(0, 0, rows, slot))
        for_zero_blocks(2, lambda c: c.wait())


def _dispatch(zero_blocks, runs, hb, rt, n_rows):
    n, d = hb.shape
    return pl.pallas_call(
        _dispatch_kernel,
        out_shape=jax.ShapeDtypeStruct((n_rows, d // 2), jnp.uint32),
        grid_spec=pltpu.PrefetchScalarGridSpec(
            num_scalar_prefetch=1,
            grid=(n // TM,),
            in_specs=[
                pl.BlockSpec((1, 1, LANES), lambda i, zf: (i, 0, 0), memory_space=pltpu.SMEM),
                pl.BlockSpec((1, 1, LANES), lambda i, zf: (jnp.maximum(i - 1, 0), 0, 0), memory_space=pltpu.SMEM),
                pl.BlockSpec((TM, d), lambda i, zf: (i, 0)),
                pl.BlockSpec((TM, LANES), lambda i, zf: (i, 0)),
            ],
            out_specs=pl.BlockSpec(memory_space=pl.ANY),
            scratch_shapes=[pltpu.VMEM((2, SROWS, d // 2), jnp.uint32), pltpu.VMEM((EB, d // 2), jnp.uint32),
                            pltpu.SemaphoreType.DMA((2,)), pltpu.SemaphoreType.DMA((2,))],
        ),
        compiler_params=_cparams(("arbitrary",)),
        name="dispatch",
    )(zero_blocks, runs, runs, hb, rt)


def _experts_kernel(be_ref, na_ref, nxt_ref, xs_ref, wg_hbm, wu_hbm, wd_hbm, ys_ref,
                    wgf, wuf, wdf, wgb, wub, wdb, sem):
    b = pl.program_id(0)

    def weight_copies(e):
        return (pltpu.make_async_copy(wg_hbm.at[e], wgf, sem.at[0]),
                pltpu.make_async_copy(wu_hbm.at[e], wuf, sem.at[1]),
                pltpu.make_async_copy(wd_hbm.at[e], wdf, sem.at[2]))

    @pl.when(b == 0)
    def _():
        for c in weight_copies(be_ref[0]):
            c.start()

    @pl.when(b < na_ref[0])
    def _():
        e = be_ref[b]
        changed = jnp.logical_or(b == 0, be_ref[jnp.maximum(b - 1, 0)] != e)

        @pl.when(changed)
        def _load():
            for c in weight_copies(e):
                c.wait()
            wgb[...] = wgf[...].astype(BF16)
            wub[...] = wuf[...].astype(BF16)
            wdb[...] = wdf[...].astype(BF16)
            nxt = nxt_ref[e]

            @pl.when(nxt >= 0)
            def _():
                for c in weight_copies(nxt):
                    c.start()

        w = xs_ref[...]
        x_lo = pltpu.bitcast(w << 16, F32).astype(BF16)
        x_hi = pltpu.bitcast(w & jnp.uint32(0xFFFF0000), F32).astype(BF16)
        dh = w.shape[1]
        g = (jnp.dot(x_lo, wgb[:dh, :], preferred_element_type=F32)
             + jnp.dot(x_hi, wgb[dh:, :], preferred_element_type=F32))
        u = (jnp.dot(x_lo, wub[:dh, :], preferred_element_type=F32)
             + jnp.dot(x_hi, wub[dh:, :], preferred_element_type=F32))
        hdn = g * (1.0 / (1.0 + jnp.exp(-g))) * u
        y = jnp.dot(hdn.astype(BF16), wdb[...], preferred_element_type=F32)
        bits = pltpu.bitcast(y.astype(BF16).astype(F32), jnp.uint32)
        ys_ref[...] = (bits[:, dh:] & jnp.uint32(0xFFFF0000)) | (bits[:, :dh] >> 16)

    @pl.when(b >= na_ref[0])
    def _():
        ys_ref[...] = jnp.zeros(ys_ref.shape, ys_ref.dtype)


def _experts(blk_e, n_act, nxt_e, xs, w_gate, w_up, w_down):
    p, dh = xs.shape
    d = 2 * dh
    de = w_gate.shape[2]

    def row_map(b, be, na, nx):
        return (jnp.minimum(b, na[0] - 1), 0)

    return pl.pallas_call(
        _experts_kernel,
        out_shape=jax.ShapeDtypeStruct((p, dh), jnp.uint32),
        grid_spec=pltpu.PrefetchScalarGridSpec(
            num_scalar_prefetch=3,
            grid=(p // EB,),
            in_specs=[
                pl.BlockSpec((EB, dh), row_map),
                pl.BlockSpec(memory_space=pl.ANY),
                pl.BlockSpec(memory_space=pl.ANY),
                pl.BlockSpec(memory_space=pl.ANY),
            ],
            out_specs=pl.BlockSpec((EB, dh), lambda b, be, na, nx: (b, 0)),
            scratch_shapes=[pltpu.VMEM((d, de), F32), pltpu.VMEM((d, de), F32), pltpu.VMEM((de, d), F32),
                            pltpu.VMEM((d, de), BF16), pltpu.VMEM((d, de), BF16), pltpu.VMEM((de, d), BF16),
                            pltpu.SemaphoreType.DMA((3,))],
        ),
        compiler_params=_cparams(("arbitrary",)),
        name="experts",
    )(blk_e, n_act, nxt_e, xs, w_gate, w_up, w_down)


def _combine_kernel(cur_ref, nxt_ref, ys_ref, h_ref, rt_ref, o_ref, ybuf, sem):
    i = pl.program_id(0)
    nt = pl.num_programs(0)
    slot = i % 2
    tm = h_ref.shape[0]

    def chunk_copy(run_row, sorted_row, rows, sl):
        return pltpu.make_async_copy(ys_ref.at[pl.ds(run_row, rows)], ybuf.at[sl, pl.ds(sorted_row, rows)],
                                     sem.at[sl])

    @pl.when(i == 0)
    def _():
        ybuf[...] = jnp.zeros(ybuf.shape, ybuf.dtype)
        _for_each_chunk(cur_ref, lambda run_row, sorted_row, rows, priority:
                        chunk_copy(run_row, sorted_row, rows, 0).start(priority=priority))

    @pl.when(i + 1 < nt)
    def _():
        _for_each_chunk(nxt_ref, lambda run_row, sorted_row, rows, priority:
                        chunk_copy(run_row, sorted_row, rows, 1 - slot).start(priority=priority))

    _wait_chunks(cur_ref, lambda rows: chunk_copy(0, 0, rows, slot))

    rt = rt_ref[...]
    w = ybuf[slot]
    dh = w.shape[1]
    y_lo = pltpu.bitcast(w << 16, F32).astype(BF16)
    y_hi = pltpu.bitcast(w & jnp.uint32(0xFFFF0000), F32).astype(BF16)
    col = lax.broadcasted_iota(jnp.int32, (tm, SROWS), 1).astype(F32)
    wsel = jnp.where(col == rt[:, 2:3], rt[:, 0:1], jnp.where(col == rt[:, 3:4], rt[:, 1:2], 0.0)).astype(BF16)
    for half, yb in ((slice(0, dh), y_lo), (slice(dh, 2 * dh), y_hi)):
        o_ref[:, half] = h_ref[:, half] + jnp.dot(wsel, yb, preferred_element_type=F32)


def _combine(runs, ys, h1, rt):
    n, d = h1.shape
    nt = n // TM
    return pl.pallas_call(
        _combine_kernel,
        out_shape=jax.ShapeDtypeStruct((n, d), F32),
        grid=(nt,),
        in_specs=[
            pl.BlockSpec((1, 1, LANES), lambda i: (i, 0, 0), memory_space=pltpu.SMEM),
            pl.BlockSpec((1, 1, LANES), lambda i: (jnp.minimum(i + 1, nt - 1), 0, 0), memory_space=pltpu.SMEM),
            pl.BlockSpec(memory_space=pl.ANY),
            pl.BlockSpec((TM, d), lambda i: (i, 0)),
            pl.BlockSpec((TM, LANES), lambda i: (i, 0)),
        ],
        out_specs=pl.BlockSpec((TM, d), lambda i: (i, 0)),
        scratch_shapes=[pltpu.VMEM((2, SROWS, d // 2), jnp.uint32), pltpu.SemaphoreType.DMA((2,))],
        compiler_params=_cparams(("arbitrary",)),
        name="combine",
    )(runs, runs, ys, h1, rt)


def kernel(x, meta_tokens, rel_bias, norm1_gain, w_in, diff_q_gain, diff_k_gain, lam_q1, lam_k1, lam_q2, lam_k2, diff_subln_gain, swa_q_gain, swa_k_gain, swa_sinks, w_out, norm2_gain, w_group, b_group, w_router, b_router, w_gate, w_up, w_down):
    batch, seq, d = x.shape
    depth = w_in.shape[0]
    n = batch * seq
    assert seq % TQ == 0 and n % TM == 0 and n % TP == 0 and d == 1024
    assert meta_tokens.shape[0] == N_META
    assert depth == 1, "the meta-token rows of the residual stream are not carried across layers"

    h = x.reshape(n, d)
    dblk, bm0, bt = _bias_tables(rel_bias, TQ)
    scale = HEAD_DIM ** -0.5
    bd = jnp.asarray(np.kron(np.eye(MXU_DIM // HEAD_DIM), np.full((HEAD_DIM, HEAD_DIM), 1.0 / HEAD_DIM)), BF16)
    ones = jnp.ones((HEAD_DIM,), F32)
    lower_pad = N_EXPERTS + N_GROUPS

    for layer in range(depth):
        lambda_init = 0.8 - 0.6 * math.exp(-0.3 * layer)
        w_cat = w_in[layer].astype(BF16)
        gain = jnp.concatenate([
            jnp.tile(diff_q_gain[layer] * (scale * LOG2E), 2 * N_DIFF_HEADS),
            jnp.tile(diff_k_gain[layer], 2 * N_DIFF_HEADS),
            jnp.tile(ones, 2 * N_DIFF_HEADS),
            jnp.tile(swa_q_gain[layer] * scale, N_SWA_HEADS),
            jnp.tile(swa_k_gain[layer], N_SWA_KV),
            jnp.tile(ones, N_SWA_KV)]).reshape(1, C_END).astype(F32)
        nmask = np.zeros((1, C_END), np.float32)
        nmask[:, C_DQ:C_DV] = 1.0
        nmask[:, C_SQ:C_SV] = 1.0
        nmask = jnp.asarray(nmask)
        g1 = norm1_gain[layer].reshape(1, d).astype(F32)

        qkv = _proj(h, g1, w_cat, bd, gain, nmask, TP)
        qkv_meta = _proj(meta_tokens.astype(F32), g1, w_cat, bd, gain, nmask, N_META)
        meta_pad = jnp.pad(qkv_meta, ((0, TQ - N_META), (0, 0)))

        lamv = jnp.pad(jnp.stack([lam_q1[layer], lam_k1[layer], lam_q2[layer], lam_k2[layer]]).astype(F32),
                       ((0, 4), (0, LANES - HEAD_DIM)))
        mixd = _diff_attention(qkv, meta_pad[:, C_DK:C_DV], meta_pad[:, C_DV:C_SQ], dblk, bm0, lamv,
                               diff_subln_gain[layer].reshape(1, LANES).astype(F32), batch, seq, lambda_init)
        mixs = _swa_attention(swa_sinks[layer].astype(F32), qkv, meta_pad[:BLOCK, C_SK:C_SV],
                              meta_pad[:BLOCK, C_SV:C_END], jnp.swapaxes(bt, -1, -2), batch, seq)

        wr = jnp.pad(jnp.concatenate([w_router[layer], w_group[layer]], axis=1),
                     ((0, 0), (0, LANES - lower_pad))).astype(BF16)
        br = jnp.pad(jnp.concatenate([b_router[layer], b_group[layer]]), (0, LANES - lower_pad)).reshape(1, LANES)
        h1, hb, rt, tinfo, cnt = _outproj(h, mixd, mixs, w_out[layer].astype(BF16),
                                          norm2_gain[layer].reshape(1, d).astype(F32), wr, br.astype(F32))

        nt = n // TM
        counts = cnt[0, :N_EXPERTS].astype(jnp.int32)
        nblk_e = (counts + EB - 1) // EB
        blk_end = jnp.cumsum(nblk_e)
        pstart = ((blk_end - nblk_e) * EB).astype(jnp.int32)
        n_blocks = -(-(2 * n + nt * N_EXPERTS * (ROW_ALIGN - 1) + N_EXPERTS * (EB - 1)) // EB)
        blk_ids = jnp.arange(n_blocks)
        blk_e = jnp.minimum(jnp.sum(blk_end[None, :] <= blk_ids[:, None], axis=1), N_EXPERTS - 1).astype(jnp.int32)
        n_act = blk_end[-1:].astype(jnp.int32)
        is_last = jnp.any((blk_end[None, :] == blk_ids[:, None] + 1) & (nblk_e[None, :] > 0), axis=1)
        zero_blocks = jnp.where(blk_ids >= n_act[0], 2, jnp.where(is_last, 1, 0)).astype(jnp.int32)
        ti = tinfo.reshape(nt, 8, LANES)
        run_len = ti[:, 0, :N_EXPERTS].astype(jnp.int32)
        run_start = pstart[None, :] + ti[:, 1, :N_EXPERTS].astype(jnp.int32)
        run_groups = (run_len + ROW_ALIGN - 1) // ROW_ALIGN
        n_copies = [jnp.sum(run_groups // (CHUNK_ROWS[0] // ROW_ALIGN), axis=1, keepdims=True)]
        n_copies += [jnp.sum((run_groups // (rows // ROW_ALIGN)) % 2, axis=1, keepdims=True)
                     for rows in CHUNK_ROWS[1:]]
        runs = jnp.concatenate([run_start, run_groups] + n_copies
                               + [jnp.zeros((nt, LANES - 2 * N_EXPERTS - len(CHUNK_ROWS)), jnp.int32)],
                               axis=1).reshape(nt, 1, LANES)

        xs = _dispatch(zero_blocks, runs, hb, rt, n_blocks * EB)
        own = jnp.where(nblk_e > 0, jnp.arange(N_EXPERTS), N_EXPERTS)
        later = jnp.concatenate([lax.cummin(own[::-1])[::-1][1:], jnp.full((1,), N_EXPERTS)])
        nxt_e = jnp.where(later < N_EXPERTS, later, -1).astype(jnp.int32)
        ys = _experts(blk_e, n_act, nxt_e, xs, w_gate[layer], w_up[layer], w_down[layer])
        h = _combine(runs, ys, h1, rt)
    return h.reshape(batch, seq, d)
```

```python
import functools
import math

import numpy as np
import jax
import jax.numpy as jnp
from jax import lax
from jax.experimental import pallas as pl
from jax.experimental.pallas import tpu as pltpu

F32 = jnp.float32
BF16 = jnp.bfloat16

HEAD_DIM = 64
N_DIFF_HEADS = 4
N_SWA_HEADS = 8
N_SWA_KV = 2
BLOCK = 128
N_META = 16
N_BUCKETS = 32
MAX_DISTANCE = 128
N_GROUPS = 4
EXPERTS_PER_GROUP = 8
N_EXPERTS = N_GROUPS * EXPERTS_PER_GROUP
D_EXPERT = 512
EPS = 1e-6
NEG = -1e30
LOG2E = math.log2(math.e)

LANES = 128
MXU_DIM = 256
VMEM_LIMIT = 48 * 1024 * 1024

TP = 1024
TM = 512
TQ = 256
ONES_ROWS = 16
EB = 512
ROW_ALIGN = 8
CHUNK_ROWS = (32, 16, 8)
SROWS = -(-(2 * TM + N_EXPERTS * (ROW_ALIGN - 1)) // MXU_DIM) * MXU_DIM

C_DQ, C_DK, C_DV, C_SQ, C_SK, C_SV, C_END = 0, 512, 1024, 1536, 2048, 2176, 2304
NORM_GROUPS = (0, 1, 2, 3, 6, 7, 8)


def _cparams(sem):
    return pltpu.CompilerParams(dimension_semantics=sem, vmem_limit_bytes=VMEM_LIMIT)


def _t5_bucket_np(dist):
    n = np.maximum(dist, 0)
    max_exact = N_BUCKETS // 2
    nf = np.maximum(n, 1).astype(np.float32)
    large = max_exact + (np.log(nf / np.float32(max_exact)) / np.float32(math.log(MAX_DISTANCE / max_exact))
                         * np.float32(N_BUCKETS - max_exact)).astype(np.int32)
    large = np.minimum(large, N_BUCKETS - 1)
    return np.where(n < max_exact, n, large)


def _bias_tables(rel_bias, tq):
    nd = 2 * BLOCK
    buckets = _t5_bucket_np(np.arange(nd))
    assert (buckets[MAX_DISTANCE:] == N_BUCKETS - 1).all()
    rb = rel_bias.astype(F32)
    r = np.arange(BLOCK)[:, None]
    c = np.arange(BLOCK)[None, :]
    d_own = r - c
    d_prev = BLOCK + r - c
    far = rb[N_BUCKETS - 1]

    def take(dist, heads):
        idx = jnp.asarray(buckets[np.clip(dist, 0, nd - 1)], jnp.int32)[None]
        out = jnp.zeros((heads.stop - heads.start,) + dist.shape, F32)
        for b in range(N_BUCKETS):
            out = jnp.where(idx == b, rb[b, heads].reshape((-1,) + (1,) * dist.ndim), out)
        return out

    hd = slice(0, N_DIFF_HEADS)
    far_d = far[hd][:, None, None]
    d0 = jnp.where(d_own[None] >= 0, take(d_own, hd) - far_d, NEG)
    d1 = take(d_prev, hd) - far_d
    dblk = jnp.stack([d0, d1], axis=1)
    rq = np.arange(tq)[:, None]
    cm = np.arange(LANES)[None, :]
    d_meta = (N_META + rq - cm)[:, :N_META]
    bm0 = jnp.pad(take(d_meta, hd) - far_d, ((0, 0), (0, 0), (0, LANES - N_META)), constant_values=NEG)

    hs = slice(N_DIFF_HEADS, N_DIFF_HEADS + N_SWA_HEADS)
    far_s = far[hs][:, None, None]
    d_meta_s = N_META + r - cm
    meta_first = jnp.where((cm < N_META)[None], take(d_meta_s, hs), NEG)
    meta_rest = jnp.where((cm < N_META)[None], jnp.broadcast_to(far_s, (N_SWA_HEADS, BLOCK, LANES)), NEG)
    prev_rest = jnp.where((c > r)[None], take(d_prev, hs), NEG)
    prev_first = jnp.full((N_SWA_HEADS, BLOCK, BLOCK), NEG, F32)
    own = jnp.where((d_own >= 0)[None], take(d_own, hs), NEG)
    bt = jnp.stack([jnp.concatenate([prev_first, own, meta_first[..., :N_META]], axis=-1),
                    jnp.concatenate([prev_rest, own, meta_rest[..., :N_META]], axis=-1)], axis=0)
    return dblk.astype(F32), bm0.astype(F32), bt.astype(F32)


def _proj_kernel(x_ref, g1_ref, w_ref, bd_ref, gain_ref, nmask_ref, o_ref):
    x = x_ref[...]
    a = x * lax.rsqrt(jnp.mean(x * x, axis=-1, keepdims=True) + EPS) * g1_ref[...]
    p = jnp.dot(a.astype(BF16), w_ref[...], preferred_element_type=F32)
    bd = bd_ref[...]
    for j in range(C_END // MXU_DIM):
        sl = slice(j * MXU_DIM, (j + 1) * MXU_DIM)
        pj = p[:, sl]
        if j in NORM_GROUPS:
            ms = jnp.dot((pj * pj).astype(BF16), bd, preferred_element_type=F32)
            pj = jnp.where(nmask_ref[:, sl] != 0.0, pj * lax.rsqrt(ms + EPS) * gain_ref[:, sl], pj)
        o_ref[:, sl] = pj.astype(BF16)


def _proj(x2, g1, w, bd, gain, nmask, tm):
    n = x2.shape[0]
    return pl.pallas_call(
        _proj_kernel,
        out_shape=jax.ShapeDtypeStruct((n, C_END), BF16),
        grid=(n // tm,),
        in_specs=[
            pl.BlockSpec((tm, x2.shape[1]), lambda i: (i, 0)),
            pl.BlockSpec(g1.shape, lambda i: (0, 0)),
            pl.BlockSpec(w.shape, lambda i: (0, 0)),
            pl.BlockSpec(bd.shape, lambda i: (0, 0)),
            pl.BlockSpec(gain.shape, lambda i: (0, 0)),
            pl.BlockSpec(nmask.shape, lambda i: (0, 0)),
        ],
        out_specs=pl.BlockSpec((tm, C_END), lambda i: (i, 0)),
        compiler_params=_cparams(("parallel",)),
        name="proj",
    )(x2, g1, w, bd, gain, nmask)


def _diff_kernel(qi_tab, t_tab, q_ref, k_ref, v_ref, km_ref, vm_ref, d_ref, bm0_ref, lamv_ref, gain_ref, o_ref,
                 bias_ref, mb_ref, qs_ref, kt_ref, vt_ref, s_buf, p_buf, a_buf, m_ref, acc_ref, *,
                 lambda_init, n_steps, n_far, n_near):
    tq = TQ
    nq = q_ref.shape[0] // tq
    nb = tq // BLOCK
    BIAS_LEFT, BIAS_DIAG = 0, 1

    d0 = d_ref[0, 0] * LOG2E
    d1 = d_ref[0, 1] * LOG2E
    zeros = jnp.zeros((BLOCK, BLOCK), F32)
    for a in range(nb):
        for b in range(nb):
            rs, cs = slice(a * BLOCK, (a + 1) * BLOCK), slice(b * BLOCK, (b + 1) * BLOCK)
            if a == b:
                blk = d0
            elif b == a + 1:
                blk = d1
            elif b > a:
                blk = zeros
            else:
                blk = jnp.full((BLOCK, BLOCK), NEG, F32)
            bias_ref[BIAS_DIAG, rs, cs] = blk
            bias_ref[BIAS_LEFT, rs, cs] = d1 if (b == 0 and a == nb - 1) else zeros
    mb_ref[0] = jnp.zeros((N_META, tq), F32)
    mb_ref[1] = bm0_ref[0, :N_META, :] * LOG2E

    lane = lax.broadcasted_iota(jnp.int32, (tq, LANES), 1)
    for i in range(nq):
        rows = slice(i * tq, (i + 1) * tq)
        q = q_ref[rows, :].astype(F32)
        qs_ref[i] = jnp.transpose(jnp.concatenate([jnp.where(lane < HEAD_DIM, q, 0.0),
                                                   jnp.where(lane >= HEAD_DIM, q, 0.0)], axis=0)).astype(BF16)
        vt_ref[i, :LANES, :] = jnp.transpose(v_ref[rows, :].astype(F32)).astype(BF16)
        kt_ref[i] = k_ref[rows, :]
    vt_ref[nq, :LANES, :] = jnp.transpose(vm_ref[...].astype(F32)).astype(BF16)
    kt_ref[nq] = km_ref[...]
    vt_ref[:, LANES:, :] = jnp.ones((nq + 1, ONES_ROWS, tq), BF16)
    acc_ref[...] = jnp.zeros(acc_ref.shape, F32)
    m_ref[...] = jnp.full(m_ref.shape, NEG, F32)
    lv = lamv_ref[...]
    lam = (jnp.exp(jnp.sum(lv[0:1] * lv[1:2], axis=-1, keepdims=True))
           - jnp.exp(jnp.sum(lv[2:3] * lv[3:4], axis=-1, keepdims=True)) + lambda_init)

    FAR, NEAR, META = 0, 1, 2

    def stage_a(n, slot, kind):
        qi, t = qi_tab[n], t_tab[n]
        if kind == META:
            s = jnp.dot(kt_ref[nq, :N_META, :], qs_ref[qi], preferred_element_type=F32)
            s_buf[slot, :N_META] = s + jnp.tile(mb_ref[jnp.where(qi == 0, 1, 0)], (1, 2))
            return
        s = jnp.dot(kt_ref[t - 1], qs_ref[qi], preferred_element_type=F32)
        if kind == NEAR:
            s = s + jnp.tile(bias_ref[jnp.where(t == qi + 1, BIAS_DIAG, BIAS_LEFT)], (1, 2))
        s_buf[slot] = s

    def stage_b(n, slot, kind):
        qi = qi_tab[n]
        rows = slice(0, N_META if kind == META else tq)
        s = s_buf[slot, rows]
        m_prev = m_ref[qi]
        m_new = jnp.maximum(m_prev, jnp.max(s, axis=0, keepdims=True))
        a_buf[slot] = jnp.exp2(m_prev - m_new)
        p_buf[slot, rows] = jnp.exp2(s - m_new[0:1]).astype(BF16)
        m_ref[qi] = m_new

    def stage_c(n, slot, kind):
        qi, t = qi_tab[n], t_tab[n]
        if kind == META:
            pv = jnp.dot(vt_ref[nq, :, :N_META], p_buf[slot, :N_META], preferred_element_type=F32)
        else:
            pv = jnp.dot(vt_ref[t - 1], p_buf[slot], preferred_element_type=F32)
        acc_ref[qi] = a_buf[slot][0:1] * acc_ref[qi] + pv

    def pipeline(base, count, kind):
        LEAD, SLOTS, unroll = 2, 3, 12
        assert unroll % SLOTS == 0
        if count <= 2 * LEAD:
            for j in range(count):
                stage_a(base + j, 0, kind)
                stage_b(base + j, 0, kind)
                stage_c(base + j, 0, kind)
            return
        for j in range(2 * LEAD):
            stage_a(base + j, j % SLOTS, kind)
            if j >= LEAD:
                stage_b(base + j - LEAD, (j - LEAD) % SLOTS, kind)

        def steps(n, count):
            for j in range(count):
                stage_a(base + n + j + 2 * LEAD, (j + 2 * LEAD) % SLOTS, kind)
                stage_b(base + n + j + LEAD, (j + LEAD) % SLOTS, kind)
                stage_c(base + n + j, j % SLOTS, kind)

        n_steady = count - 2 * LEAD
        n_blocks = n_steady // unroll

        def block(k, carry):
            steps(unroll * k, unroll)
            return carry
        lax.fori_loop(0, n_blocks, block, 0)
        steps(unroll * n_blocks, n_steady - unroll * n_blocks)
        for j in range(n_steady, count):
            if j + LEAD < count:
                stage_b(base + j + LEAD, (j + LEAD) % SLOTS, kind)
            stage_c(base + j, j % SLOTS, kind)

    pipeline(0, n_far, FAR)
    pipeline(n_far, n_near, NEAR)
    pipeline(n_far + n_near, n_steps - n_far - n_near, META)

    for i in range(nq):
        acc = acc_ref[i]
        o = acc[:LANES] * (1.0 / acc[LANES:LANES + 1])
        d = o[:, :tq] - lam * o[:, tq:]
        y = d * lax.rsqrt(jnp.mean(d * d, axis=0, keepdims=True) + EPS) * jnp.tile(gain_ref[...], (1, tq // LANES))
        o_ref[i * tq:(i + 1) * tq, :] = jnp.transpose(y * (1.0 - lambda_init)).astype(BF16)


def _diff_attention(qkv, km, vm, dblk, bm0, lamv, gain, batch, seq, lambda_init):
    nq = seq // TQ
    far = [(qi, t) for qi in range(nq) for t in range(1, qi)]
    near = [(qi, t) for qi in range(nq) for t in (qi, qi + 1) if t >= 1]
    meta = [(qi, 0) for qi in range(nq)]
    steps = far + near + meta
    qi_tab = jnp.asarray([s[0] for s in steps], jnp.int32)
    t_tab = jnp.asarray([s[1] for s in steps], jnp.int32)
    kern = functools.partial(_diff_kernel, lambda_init=lambda_init, n_steps=len(steps), n_far=len(far),
                             n_near=len(near))
    return pl.pallas_call(
        kern,
        out_shape=jax.ShapeDtypeStruct((batch * seq, N_DIFF_HEADS * LANES), BF16),
        grid_spec=pltpu.PrefetchScalarGridSpec(
            num_scalar_prefetch=2,
            grid=(batch, N_DIFF_HEADS),
            in_specs=[
                pl.BlockSpec((seq, LANES), lambda b, h, *_: (b, C_DQ // LANES + h)),
                pl.BlockSpec((seq, LANES), lambda b, h, *_: (b, C_DK // LANES + h)),
                pl.BlockSpec((seq, LANES), lambda b, h, *_: (b, C_DV // LANES + h)),
                pl.BlockSpec((TQ, LANES), lambda b, h, *_: (0, h)),
                pl.BlockSpec((TQ, LANES), lambda b, h, *_: (0, h)),
                pl.BlockSpec((1, 2, BLOCK, BLOCK), lambda b, h, *_: (h, 0, 0, 0)),
                pl.BlockSpec((1, LANES, TQ), lambda b, h, *_: (h, 0, 0)),
                pl.BlockSpec(lamv.shape, lambda b, h, *_: (0, 0)),
                pl.BlockSpec((LANES, LANES), lambda b, h, *_: (0, 0)),
            ],
            out_specs=pl.BlockSpec((seq, LANES), lambda b, h, *_: (b, h)),
            scratch_shapes=[
                pltpu.VMEM((2, TQ, TQ), F32),
                pltpu.VMEM((2, N_META, TQ), F32),
                pltpu.VMEM((nq, LANES, 2 * TQ), BF16),
                pltpu.VMEM((nq + 1, TQ, LANES), BF16),
                pltpu.VMEM((nq + 1, LANES + ONES_ROWS, TQ), BF16),
                pltpu.VMEM((3, TQ, 2 * TQ), F32),
                pltpu.VMEM((3, TQ, 2 * TQ), BF16),
                pltpu.VMEM((3, 8, 2 * TQ), F32),
                pltpu.VMEM((nq, 8, 2 * TQ), F32),
                pltpu.VMEM((nq, LANES + ONES_ROWS, 2 * TQ), F32),
            ],
        ),
        compiler_params=_cparams(("parallel", "parallel")),
        name="diff_attention",
    )(qi_tab, t_tab, qkv, qkv, qkv, km, vm, jnp.swapaxes(dblk, -1, -2), jnp.swapaxes(bm0, -1, -2), lamv,
      jnp.broadcast_to(gain.reshape(LANES, 1), (LANES, LANES)))


def _swa_kernel(sink_ref, q_ref, k_ref, v_ref, km_ref, vm_ref, bt_ref, o_ref, kd_ref, vt_ref,
                s_scr, p_scr, inv_scr):
    nkb = k_ref.shape[0] // BLOCK
    lane = lax.broadcasted_iota(jnp.int32, (BLOCK, LANES), 1)
    pairs = [(g, u) for g in range(N_SWA_KV) for u in range(2)]

    def both_halves(k):
        k0, k1 = k[:, :HEAD_DIM], k[:, HEAD_DIM:]
        return jnp.concatenate([k0, k0, k1, k1], axis=1)

    def prepare(j, carry):
        rows = pl.ds(pl.multiple_of(j * BLOCK, BLOCK), BLOCK)
        kd_ref[j] = both_halves(k_ref[rows, :])
        vt_ref[j] = jnp.transpose(v_ref[rows, :].astype(F32)).astype(BF16)
        return carry
    lax.fori_loop(0, nkb, prepare, 0)
    kd_ref[nkb] = both_halves(km_ref[...])
    vt_ref[nkb] = jnp.transpose(vm_ref[...].astype(F32)).astype(BF16)

    def scores(n, slot):
        first = jnp.where(n == 0, 0, 1)
        prev = jnp.maximum(n - 1, 0)
        r_q = pl.multiple_of(n * BLOCK, BLOCK)
        for c, (g, u) in enumerate(pairs):
            ks = slice(g * LANES, (g + 1) * LANES)
            kcat = jnp.concatenate([kd_ref[prev, :, ks], kd_ref[n, :, ks], kd_ref[nkb, :N_META, ks]], axis=0)
            h0 = 4 * g + 2 * u
            qp = q_ref[pl.ds(r_q, BLOCK), (2 * g + u) * LANES:(2 * g + u + 1) * LANES].astype(F32)
            qs = jnp.transpose(jnp.concatenate([jnp.where(lane < HEAD_DIM, qp, 0.0),
                                                jnp.where(lane >= HEAD_DIM, qp, 0.0)], axis=0)).astype(BF16)
            s = jnp.dot(kcat, qs, preferred_element_type=F32)
            s_scr[slot, c] = s + jnp.concatenate([bt_ref[first, h0], bt_ref[first, h0 + 1]], axis=1)

    def exponentials(slot):
        for c, (g, u) in enumerate(pairs):
            h0 = 4 * g + 2 * u
            s = s_scr[slot, c]
            sink = jnp.concatenate([sink_ref[h0:h0 + 1, :], sink_ref[h0 + 1:h0 + 2, :]], axis=1)
            m = jnp.maximum(jnp.max(s, axis=0, keepdims=True), sink)
            p = jnp.exp(s - m)
            p_scr[slot, c] = p.astype(BF16)
            inv_scr[slot, c] = jnp.broadcast_to(1.0 / (jnp.sum(p, axis=0, keepdims=True) + jnp.exp(sink - m)),
                                                inv_scr.shape[2:])

    def values(n, slot):
        prev = jnp.maximum(n - 1, 0)
        r_q = pl.multiple_of(n * BLOCK, BLOCK)
        for c, (g, u) in enumerate(pairs):
            vs = slice(g * HEAD_DIM, (g + 1) * HEAD_DIM)
            vcat = jnp.concatenate([vt_ref[prev, vs, :], vt_ref[n, vs, :]], axis=1)
            o = (jnp.dot(vcat, p_scr[slot, c, :2 * BLOCK], preferred_element_type=F32)
                 + jnp.dot(vt_ref[nkb, vs, :N_META], p_scr[slot, c, 2 * BLOCK:], preferred_element_type=F32)
                 ) * inv_scr[slot, c][0:1]
            ot = jnp.transpose(o)
            o_ref[pl.ds(r_q, BLOCK), (2 * g + u) * LANES:(2 * g + u + 1) * LANES] = (
                jnp.concatenate([ot[:BLOCK], ot[BLOCK:]], axis=1).astype(BF16))

    scores(0, 0)
    scores(1, 1)
    exponentials(0)

    def two_blocks(k, carry):
        n = 2 * k
        scores(n + 2, 0)
        exponentials(1)
        values(n, 0)
        scores(n + 3, 1)
        exponentials(0)
        values(n + 1, 1)
        return carry
    lax.fori_loop(0, nkb // 2 - 1, two_blocks, 0)
    exponentials(1)
    values(nkb - 2, 0)
    values(nkb - 1, 1)


def _swa_attention(sinks, qkv, km, vm, bt, batch, seq):
    nkb = seq // BLOCK
    assert nkb % 2 == 0
    sinkv = jnp.broadcast_to(sinks.reshape(N_SWA_HEADS, 1), (N_SWA_HEADS, LANES))
    return pl.pallas_call(
        _swa_kernel,
        out_shape=jax.ShapeDtypeStruct((batch * seq, N_SWA_HEADS * HEAD_DIM), BF16),
        grid=(batch,),
        in_specs=[
            pl.BlockSpec(sinkv.shape, lambda b: (0, 0)),
            pl.BlockSpec((seq, 512), lambda b: (b, C_SQ // 512)),
            pl.BlockSpec((seq, LANES), lambda b: (b, C_SK // LANES)),
            pl.BlockSpec((seq, LANES), lambda b: (b, C_SV // LANES)),
            pl.BlockSpec(km.shape, lambda b: (0, 0)),
            pl.BlockSpec(vm.shape, lambda b: (0, 0)),
            pl.BlockSpec(bt.shape, lambda b: (0, 0, 0, 0)),
        ],
        out_specs=pl.BlockSpec((seq, 512), lambda b: (b, 0)),
        scratch_shapes=[pltpu.VMEM((nkb + 1, BLOCK, 2 * LANES), BF16),
                        pltpu.VMEM((nkb + 1, LANES, BLOCK), BF16),
                        pltpu.VMEM((2, 4, 2 * BLOCK + N_META, 2 * BLOCK), F32),
                        pltpu.VMEM((2, 4, 2 * BLOCK + N_META, 2 * BLOCK), BF16),
                        pltpu.VMEM((2, 4, 8, 2 * BLOCK), F32)],
        compiler_params=_cparams(("parallel",)),
        name="swa_attention",
    )(sinkv, qkv, qkv, qkv, km, vm, bt)


def _outproj_kernel(x_ref, md_ref, ms_ref, wo_ref, g2_ref, wr_ref, br_ref,
                    h_ref, hb_ref, rt_ref, ti_ref, cnt_ref, c_ref, lg_ref):
    i = pl.program_id(0)

    @pl.when(i == 0)
    def _init():
        c_ref[...] = jnp.zeros(c_ref.shape, F32)
        lg_ref[...] = jnp.zeros(lg_ref.shape, F32)

    lg_prev = lg_ref[...]
    half = md_ref.shape[1]
    h = (x_ref[...]
         + jnp.dot(md_ref[...], wo_ref[:half, :], preferred_element_type=F32)
         + jnp.dot(ms_ref[...], wo_ref[half:, :], preferred_element_type=F32))
    h_ref[...] = h
    hn = h * lax.rsqrt(jnp.mean(h * h, axis=-1, keepdims=True) + EPS) * g2_ref[...]
    hb = hn.astype(BF16)
    hb_ref[...] = hb
    lg_ref[...] = jnp.dot(hb, wr_ref[...], preferred_element_type=F32) + br_ref[...]
    _route_tile(lg_prev, jnp.where(i > 0, 1.0, 0.0), rt_ref, ti_ref, c_ref)

    @pl.when(i == pl.num_programs(0) - 1)
    def _fin():
        cnt_ref[...] = c_ref[...]


def _route_tile(lg, live, rt_ref, ti_ref, c_ref):
    tm = lg.shape[0]
    lane_i = lax.broadcasted_iota(jnp.int32, lg.shape, 1)
    lane = lane_i.astype(F32)
    big = float(4 * LANES)
    is_g = (lane_i >= N_EXPERTS) & (lane_i < N_EXPERTS + N_GROUPS)
    glm = jnp.where(is_g, lg, -jnp.inf)
    gmax = jnp.max(glm, axis=1, keepdims=True)
    gidx = jnp.min(jnp.where(glm == gmax, lane, big), axis=1, keepdims=True) - N_EXPERTS
    gsum = jnp.sum(jnp.where(is_g, jnp.exp(lg - gmax), 0.0), axis=1, keepdims=True)
    g_w = 1.0 / gsum
    lane_grp = (lane_i >> 3).astype(F32)
    in_grp = (lane_i < N_EXPERTS) & (lane_grp == gidx)
    el = jnp.where(in_grp, lg, -jnp.inf)
    t1 = jnp.max(el, axis=1, keepdims=True)
    j1 = jnp.min(jnp.where(el == t1, lane, big), axis=1, keepdims=True)
    el2 = jnp.where(lane == j1, -jnp.inf, el)
    t2 = jnp.max(el2, axis=1, keepdims=True)
    j2 = jnp.min(jnp.where(el2 == t2, lane, big), axis=1, keepdims=True)
    e2 = jnp.exp(t2 - t1)
    den = 1.0 + e2
    gate1 = g_w / den
    gate2 = g_w * e2 / den

    o1 = lane == j1
    o2 = lane == j2
    onehot = jnp.where(o1 | o2, 1.0, 0.0).astype(BF16)
    rr = lax.broadcasted_iota(jnp.int32, (tm, tm), 0)
    cc = lax.broadcasted_iota(jnp.int32, (tm, tm), 1)
    lower = jnp.where(rr > cc, 1.0, 0.0).astype(BF16)
    pfx = jnp.dot(lower, onehot, preferred_element_type=F32)
    cnt_tile = jnp.sum(onehot.astype(F32), axis=0, keepdims=True)
    groups = jnp.floor((cnt_tile + (ROW_ALIGN - 1)) * (1.0 / ROW_ALIGN))
    er = lax.broadcasted_iota(jnp.int32, (LANES, LANES), 0)
    ec = lax.broadcasted_iota(jnp.int32, (LANES, LANES), 1)
    before = jnp.where(er < ec, 1.0, 0.0).astype(BF16)
    cbase = ROW_ALIGN * jnp.dot(jnp.broadcast_to(groups, (8, LANES)).astype(BF16), before,
                                preferred_element_type=F32)[0:1]
    at = pfx + cbase
    pos1 = jnp.sum(jnp.where(o1, at, 0.0), axis=1, keepdims=True)
    pos2 = jnp.sum(jnp.where(o2, at, 0.0), axis=1, keepdims=True)
    rt_ref[...] = jnp.where(lane_i == 0, gate1,
                            jnp.where(lane_i == 1, gate2,
                                      jnp.where(lane_i == 2, pos1,
                                                jnp.where(lane_i == 3, pos2, 0.0))))
    c_old = c_ref[...]
    c_ref[...] = c_old + groups * (ROW_ALIGN * live)
    row8 = lax.broadcasted_iota(jnp.int32, (8, LANES), 0)
    ti_ref[...] = jnp.where(row8 == 0, cnt_tile, jnp.where(row8 == 1, c_old, 0.0))


def _outproj(x2, mixd, mixs, wo, g2, wr, br):
    n, d = x2.shape
    nt = n // TM

    def proj_tile(i):
        return (jnp.minimum(i, nt - 1), 0)

    def route_tile(i):
        return (jnp.maximum(i - 1, 0), 0)

    return pl.pallas_call(
        _outproj_kernel,
        out_shape=(jax.ShapeDtypeStruct((n, d), F32),
                   jax.ShapeDtypeStruct((n, d), BF16),
                   jax.ShapeDtypeStruct((n, LANES), F32),
                   jax.ShapeDtypeStruct((nt * 8, LANES), F32),
                   jax.ShapeDtypeStruct((8, LANES), F32)),
        grid=(nt + 1,),
        in_specs=[
            pl.BlockSpec((TM, d), proj_tile),
            pl.BlockSpec((TM, mixd.shape[1]), proj_tile),
            pl.BlockSpec((TM, mixs.shape[1]), proj_tile),
            pl.BlockSpec(wo.shape, lambda i: (0, 0)),
            pl.BlockSpec(g2.shape, lambda i: (0, 0)),
            pl.BlockSpec(wr.shape, lambda i: (0, 0)),
            pl.BlockSpec(br.shape, lambda i: (0, 0)),
        ],
        out_specs=(pl.BlockSpec((TM, d), proj_tile),
                   pl.BlockSpec((TM, d), proj_tile),
                   pl.BlockSpec((TM, LANES), route_tile),
                   pl.BlockSpec((8, LANES), route_tile),
                   pl.BlockSpec((8, LANES), lambda i: (0, 0))),
        scratch_shapes=[pltpu.VMEM((8, LANES), F32), pltpu.VMEM((TM, LANES), F32)],
        compiler_params=_cparams(("arbitrary",)),
        name="outproj_router",
    )(x2, mixd, mixs, wo, g2, wr, br)


def _for_each_chunk(runs_ref, fn):
    big = CHUNK_ROWS[0]

    def per_expert(e, sorted_row, priority):
        start = runs_ref[0, 0, e]
        groups = runs_ref[0, 0, N_EXPERTS + e]
        whole = groups // (big // ROW_ALIGN)

        def per_chunk(c, carry):
            fn(pl.multiple_of(start + c * big, ROW_ALIGN), pl.multiple_of(sorted_row + c * big, ROW_ALIGN),
               big, priority)
            return carry
        lax.fori_loop(0, whole, per_chunk, 0)

        done = whole * big
        for rows in CHUNK_ROWS[1:]:
            has = (groups // (rows // ROW_ALIGN)) % 2

            @pl.when(has == 1)
            def _(done=done, rows=rows):
                fn(pl.multiple_of(start + done, ROW_ALIGN), pl.multiple_of(sorted_row + done, ROW_ALIGN),
                   rows, priority)
            done = done + has * rows
        return sorted_row + groups * ROW_ALIGN

    def expert_pair(e2, sorted_row):
        return per_expert(2 * e2 + 1, per_expert(2 * e2, sorted_row, 0), 1)
    lax.fori_loop(0, N_EXPERTS // 2, expert_pair, 0)


def _wait_chunks(runs_ref, make_copy):
    for k, rows in enumerate(CHUNK_ROWS):
        def body(c, carry, rows=rows):
            make_copy(rows).wait()
            return carry
        lax.fori_loop(0, runs_ref[0, 0, 2 * N_EXPERTS + k], body, 0)


def _dispatch_kernel(zf_ref, cur_ref, prv_ref, hb_ref, rt_ref, xs_ref, sbuf, zbuf, sem, zsem):
    i = pl.program_id(0)
    nt = pl.num_programs(0)
    slot = i % 2
    tm, d = hb_ref.shape

    def for_zero_blocks(kind, fn):
        def body(b, carry):
            @pl.when(zf_ref[b] == kind)
            def _():
                fn(pltpu.make_async_copy(zbuf, xs_ref.at[pl.ds(pl.multiple_of(b * EB, EB), EB)],
                                         zsem.at[kind - 1]))
            return carry
        lax.fori_loop(0, zf_ref.shape[0], body, 0)

    @pl.when(i == 0)
    def _():
        zbuf[...] = jnp.zeros(zbuf.shape, zbuf.dtype)
        for_zero_blocks(1, lambda c: c.start())
        for_zero_blocks(2, lambda c: c.start())
        for_zero_blocks(1, lambda c: c.wait())

    pos_t = jnp.transpose(rt_ref[...])
    srow = lax.broadcasted_iota(jnp.int32, (SROWS, tm), 0).astype(F32)
    sel = jnp.where(srow == pos_t[2:3, :], 1.0, jnp.where(srow == pos_t[3:4, :], 1.0, 0.0)).astype(BF16)
    srt = jnp.dot(sel, hb_ref[...], preferred_element_type=F32)
    bits = pltpu.bitcast(srt, jnp.uint32)
    sbuf[slot] = (bits[:, d // 2:] & jnp.uint32(0xFFFF0000)) | (bits[:, :d // 2] >> 16)

    def chunk_copy(run_row, sorted_row, rows, sl):
        return pltpu.make_async_copy(sbuf.at[sl, pl.ds(sorted_row, rows)], xs_ref.at[pl.ds(run_row, rows)],
                                     sem.at[sl])

    _for_each_chunk(cur_ref, lambda run_row, sorted_row, rows, priority:
                    chunk_copy(run_row, sorted_row, rows, slot).start(priority=priority))

    @pl.when(i > 0)
    def _():
        _wait_chunks(prv_ref, lambda rows: chunk_copy(0, 0, rows, 1 - slot))

    @pl.when(i == nt - 1)
    def _():
        _wait_chunks(cur_ref, lambda rows: chunk_copy(0, 0, rows, slot))
        for_zero_blocks(2, lambda c: c.wait())


def _dispatch(zero_blocks, runs, hb, rt, n_rows):
    n, d = hb.shape
    return pl.pallas_call(
        _dispatch_kernel,
        out_shape=jax.ShapeDtypeStruct((n_rows, d // 2), jnp.uint32),
        grid_spec=pltpu.PrefetchScalarGridSpec(
            num_scalar_prefetch=1,
            grid=(n // TM,),
            in_specs=[
                pl.BlockSpec((1, 1, LANES), lambda i, zf: (i, 0, 0), memory_space=pltpu.SMEM),
                pl.BlockSpec((1, 1, LANES), lambda i, zf: (jnp.maximum(i - 1, 0), 0, 0), memory_space=pltpu.SMEM),
                pl.BlockSpec((TM, d), lambda i, zf: (i, 0)),
                pl.BlockSpec((TM, LANES), lambda i, zf: (i, 0)),
            ],
            out_specs=pl.BlockSpec(memory_space=pl.ANY),
            scratch_shapes=[pltpu.VMEM((2, SROWS, d // 2), jnp.uint32), pltpu.VMEM((EB, d // 2), jnp.uint32),
                            pltpu.SemaphoreType.DMA((2,)), pltpu.SemaphoreType.DMA((2,))],
        ),
        compiler_params=_cparams(("arbitrary",)),
        name="dispatch",
    )(zero_blocks, runs, runs, hb, rt)


def _experts_kernel(be_ref, na_ref, nxt_ref, xs_ref, wg_hbm, wu_hbm, wd_hbm, ys_ref,
                    wgf, wuf, wdf, wgb, wub, wdb, sem):
    b = pl.program_id(0)

    def weight_copies(e):
        return (pltpu.make_async_copy(wg_hbm.at[e], wgf, sem.at[0]),
                pltpu.make_async_copy(wu_hbm.at[e], wuf, sem.at[1]),
                pltpu.make_async_copy(wd_hbm.at[e], wdf, sem.at[2]))

    @pl.when(b == 0)
    def _():
        for c in weight_copies(be_ref[0]):
            c.start()

    @pl.when(b < na_ref[0])
    def _():
        e = be_ref[b]
        changed = jnp.logical_or(b == 0, be_ref[jnp.maximum(b - 1, 0)] != e)

        @pl.when(changed)
        def _load():
            for c in weight_copies(e):
                c.wait()
            wgb[...] = wgf[...].astype(BF16)
            wub[...] = wuf[...].astype(BF16)
            wdb[...] = wdf[...].astype(BF16)
            nxt = nxt_ref[e]

            @pl.when(nxt >= 0)
            def _():
                for c in weight_copies(nxt):
                    c.start()

        w = xs_ref[...]
        x_lo = pltpu.bitcast(w << 16, F32).astype(BF16)
        x_hi = pltpu.bitcast(w & jnp.uint32(0xFFFF0000), F32).astype(BF16)
        dh = w.shape[1]
        g = (jnp.dot(x_lo, wgb[:dh, :], preferred_element_type=F32)
             + jnp.dot(x_hi, wgb[dh:, :], preferred_element_type=F32))
        u = (jnp.dot(x_lo, wub[:dh, :], preferred_element_type=F32)
             + jnp.dot(x_hi, wub[dh:, :], preferred_element_type=F32))
        hdn = g * (1.0 / (1.0 + jnp.exp(-g))) * u
        y = jnp.dot(hdn.astype(BF16), wdb[...], preferred_element_type=F32)
        bits = pltpu.bitcast(y.astype(BF16).astype(F32), jnp.uint32)
        ys_ref[...] = (bits[:, dh:] & jnp.uint32(0xFFFF0000)) | (bits[:, :dh] >> 16)

    @pl.when(b >= na_ref[0])
    def _():
        ys_ref[...] = jnp.zeros(ys_ref.shape, ys_ref.dtype)


def _experts(blk_e, n_act, nxt_e, xs, w_gate, w_up, w_down):
    p, dh = xs.shape
    d = 2 * dh
    de = w_gate.shape[2]

    def row_map(b, be, na, nx):
        return (jnp.minimum(b, na[0] - 1), 0)

    return pl.pallas_call(
        _experts_kernel,
        out_shape=jax.ShapeDtypeStruct((p, dh), jnp.uint32),
        grid_spec=pltpu.PrefetchScalarGridSpec(
            num_scalar_prefetch=3,
            grid=(p // EB,),
            in_specs=[
                pl.BlockSpec((EB, dh), row_map),
                pl.BlockSpec(memory_space=pl.ANY),
                pl.BlockSpec(memory_space=pl.ANY),
                pl.BlockSpec(memory_space=pl.ANY),
            ],
            out_specs=pl.BlockSpec((EB, dh), lambda b, be, na, nx: (b, 0)),
            scratch_shapes=[pltpu.VMEM((d, de), F32), pltpu.VMEM((d, de), F32), pltpu.VMEM((de, d), F32),
                            pltpu.VMEM((d, de), BF16), pltpu.VMEM((d, de), BF16), pltpu.VMEM((de, d), BF16),
                            pltpu.SemaphoreType.DMA((3,))],
        ),
        compiler_params=_cparams(("arbitrary",)),
        name="experts",
    )(blk_e, n_act, nxt_e, xs, w_gate, w_up, w_down)


def _combine_kernel(cur_ref, nxt_ref, ys_ref, h_ref, rt_ref, o_ref, ybuf, sem):
    i = pl.program_id(0)
    nt = pl.num_programs(0)
    slot = i % 2
    tm = h_ref.shape[0]

    def chunk_copy(run_row, sorted_row, rows, sl):
        return pltpu.make_async_copy(ys_ref.at[pl.ds(run_row, rows)], ybuf.at[sl, pl.ds(sorted_row, rows)],
                                     sem.at[sl])

    @pl.when(i == 0)
    def _():
        ybuf[...] = jnp.zeros(ybuf.shape, ybuf.dtype)
        _for_each_chunk(cur_ref, lambda run_row, sorted_row, rows, priority:
                        chunk_copy(run_row, sorted_row, rows, 0).start(priority=priority))

    @pl.when(i + 1 < nt)
    def _():
        _for_each_chunk(nxt_ref, lambda run_row, sorted_row, rows, priority:
                        chunk_copy(run_row, sorted_row, rows, 1 - slot).start(priority=priority))

    _wait_chunks(cur_ref, lambda rows: chunk_copy(0, 0, rows, slot))

    rt = rt_ref[...]
    w = ybuf[slot]
    dh = w.shape[1]
    y_lo = pltpu.bitcast(w << 16, F32).astype(BF16)
    y_hi = pltpu.bitcast(w & jnp.uint32(0xFFFF0000), F32).astype(BF16)
    col = lax.broadcasted_iota(jnp.int32, (tm, SROWS), 1).astype(F32)
    wsel = jnp.where(col == rt[:, 2:3], rt[:, 0:1], jnp.where(col == rt[:, 3:4], rt[:, 1:2], 0.0)).astype(BF16)
    for half, yb in ((slice(0, dh), y_lo), (slice(dh, 2 * dh), y_hi)):
        o_ref[:, half] = h_ref[:, half] + jnp.dot(wsel, yb, preferred_element_type=F32)


def _combine(runs, ys, h1, rt):
    n, d = h1.shape
    nt = n // TM
    return pl.pallas_call(
        _combine_kernel,
        out_shape=jax.ShapeDtypeStruct((n, d), F32),
        grid=(nt,),
        in_specs=[
            pl.BlockSpec((1, 1, LANES), lambda i: (i, 0, 0), memory_space=pltpu.SMEM),
            pl.BlockSpec((1, 1, LANES), lambda i: (jnp.minimum(i + 1, nt - 1), 0, 0), memory_space=pltpu.SMEM),
            pl.BlockSpec(memory_space=pl.ANY),
            pl.BlockSpec((TM, d), lambda i: (i, 0)),
            pl.BlockSpec((TM, LANES), lambda i: (i, 0)),
        ],
        out_specs=pl.BlockSpec((TM, d), lambda i: (i, 0)),
        scratch_shapes=[pltpu.VMEM((2, SROWS, d // 2), jnp.uint32), pltpu.SemaphoreType.DMA((2,))],
        compiler_params=_cparams(("arbitrary",)),
        name="combine",
    )(runs, runs, ys, h1, rt)


def kernel(x, meta_tokens, rel_bias, norm1_gain, w_in, diff_q_gain, diff_k_gain, lam_q1, lam_k1, lam_q2, lam_k2, diff_subln_gain, swa_q_gain, swa_k_gain, swa_sinks, w_out, norm2_gain, w_group, b_group, w_router, b_router, w_gate, w_up, w_down):
    batch, seq, d = x.shape
    depth = w_in.shape[0]
    n = batch * seq
    assert seq % TQ == 0 and n % TM == 0 and n % TP == 0 and d == 1024
    assert meta_tokens.shape[0] == N_META
    assert depth == 1, "the meta-token rows of the residual stream are not carried across layers"

    h = x.reshape(n, d)
    dblk, bm0, bt = _bias_tables(rel_bias, TQ)
    scale = HEAD_DIM ** -0.5
    bd = jnp.asarray(np.kron(np.eye(MXU_DIM // HEAD_DIM), np.full((HEAD_DIM, HEAD_DIM), 1.0 / HEAD_DIM)), BF16)
    ones = jnp.ones((HEAD_DIM,), F32)
    lower_pad = N_EXPERTS + N_GROUPS

    for layer in range(depth):
        lambda_init = 0.8 - 0.6 * math.exp(-0.3 * layer)
        w_cat = w_in[layer].astype(BF16)
        gain = jnp.concatenate([
            jnp.tile(diff_q_gain[layer] * (scale * LOG2E), 2 * N_DIFF_HEADS),
            jnp.tile(diff_k_gain[layer], 2 * N_DIFF_HEADS),
            jnp.tile(ones, 2 * N_DIFF_HEADS),
            jnp.tile(swa_q_gain[layer] * scale, N_SWA_HEADS),
            jnp.tile(swa_k_gain[layer], N_SWA_KV),
            jnp.tile(ones, N_SWA_KV)]).reshape(1, C_END).astype(F32)
        nmask = np.zeros((1, C_END), np.float32)
        nmask[:, C_DQ:C_DV] = 1.0
        nmask[:, C_SQ:C_SV] = 1.0
        nmask = jnp.asarray(nmask)
        g1 = norm1_gain[layer].reshape(1, d).astype(F32)

        qkv = _proj(h, g1, w_cat, bd, gain, nmask, TP)
        qkv_meta = _proj(meta_tokens.astype(F32), g1, w_cat, bd, gain, nmask, N_META)
        meta_pad = jnp.pad(qkv_meta, ((0, TQ - N_META), (0, 0)))

        lamv = jnp.pad(jnp.stack([lam_q1[layer], lam_k1[layer], lam_q2[layer], lam_k2[layer]]).astype(F32),
                       ((0, 4), (0, LANES - HEAD_DIM)))
        mixd = _diff_attention(qkv, meta_pad[:, C_DK:C_DV], meta_pad[:, C_DV:C_SQ], dblk, bm0, lamv,
                               diff_subln_gain[layer].reshape(1, LANES).astype(F32), batch, seq, lambda_init)
        mixs = _swa_attention(swa_sinks[layer].astype(F32), qkv, meta_pad[:BLOCK, C_SK:C_SV],
                              meta_pad[:BLOCK, C_SV:C_END], jnp.swapaxes(bt, -1, -2), batch, seq)

        wr = jnp.pad(jnp.concatenate([w_router[layer], w_group[layer]], axis=1),
                     ((0, 0), (0, LANES - lower_pad))).astype(BF16)
        br = jnp.pad(jnp.concatenate([b_router[layer], b_group[layer]]), (0, LANES - lower_pad)).reshape(1, LANES)
        h1, hb, rt, tinfo, cnt = _outproj(h, mixd, mixs, w_out[layer].astype(BF16),
                                          norm2_gain[layer].reshape(1, d).astype(F32), wr, br.astype(F32))

        nt = n // TM
        counts = cnt[0, :N_EXPERTS].astype(jnp.int32)
        nblk_e = (counts + EB - 1) // EB
        blk_end = jnp.cumsum(nblk_e)
        pstart = ((blk_end - nblk_e) * EB).astype(jnp.int32)
        n_blocks = -(-(2 * n + nt * N_EXPERTS * (ROW_ALIGN - 1) + N_EXPERTS * (EB - 1)) // EB)
        blk_ids = jnp.arange(n_blocks)
        blk_e = jnp.minimum(jnp.sum(blk_end[None, :] <= blk_ids[:, None], axis=1), N_EXPERTS - 1).astype(jnp.int32)
        n_act = blk_end[-1:].astype(jnp.int32)
        is_last = jnp.any((blk_end[None, :] == blk_ids[:, None] + 1) & (nblk_e[None, :] > 0), axis=1)
        zero_blocks = jnp.where(blk_ids >= n_act[0], 2, jnp.where(is_last, 1, 0)).astype(jnp.int32)
        ti = tinfo.reshape(nt, 8, LANES)
        run_len = ti[:, 0, :N_EXPERTS].astype(jnp.int32)
        run_start = pstart[None, :] + ti[:, 1, :N_EXPERTS].astype(jnp.int32)
        run_groups = (run_len + ROW_ALIGN - 1) // ROW_ALIGN
        n_copies = [jnp.sum(run_groups // (CHUNK_ROWS[0] // ROW_ALIGN), axis=1, keepdims=True)]
        n_copies += [jnp.sum((run_groups // (rows // ROW_ALIGN)) % 2, axis=1, keepdims=True)
                     for rows in CHUNK_ROWS[1:]]
        runs = jnp.concatenate([run_start, run_groups] + n_copies
                               + [jnp.zeros((nt, LANES - 2 * N_EXPERTS - len(CHUNK_ROWS)), jnp.int32)],
                               axis=1).reshape(nt, 1, LANES)

        xs = _dispatch(zero_blocks, runs, hb, rt, n_blocks * EB)
        own = jnp.where(nblk_e > 0, jnp.arange(N_EXPERTS), N_EXPERTS)
        later = jnp.concatenate([lax.cummin(own[::-1])[::-1][1:], jnp.full((1,), N_EXPERTS)])
        nxt_e = jnp.where(later < N_EXPERTS, later, -1).astype(jnp.int32)
        ys = _experts(blk_e, n_act, nxt_e, xs, w_gate[layer], w_up[layer], w_down[layer])
        h = _combine(runs, ys, h1, rt)
    return h.reshape(batch, seq, d)
```

```python
import functools
import math

import numpy as np
import jax
import jax.numpy as jnp
from jax import lax
from jax.experimental import pallas as pl
from jax.experimental.pallas import tpu as pltpu

F32 = jnp.float32
BF16 = jnp.bfloat16

HEAD_DIM = 64
N_DIFF_HEADS = 4
N_SWA_HEADS = 8
N_SWA_KV = 2
BLOCK = 128
N_META = 16
N_BUCKETS = 32
MAX_DISTANCE = 128
N_GROUPS = 4
EXPERTS_PER_GROUP = 8
N_EXPERTS = N_GROUPS * EXPERTS_PER_GROUP
EPS = 1e-6
NEG = -1e30
LOG2E = math.log2(math.e)

LANES = 128
MXU_DIM = 256
V7X_VMEM_BYTES = 64 * 1024 * 1024
VMEM_LIMIT = V7X_VMEM_BYTES * 3 // 4

TP = 1024
TM = 512
TQ = 256
ONES_ROWS = 16
EB = 512
ROW_ALIGN = 8
CHUNK_ROWS = (32, 16, 8)
SROWS = -(-(2 * TM + N_EXPERTS * (ROW_ALIGN - 1)) // MXU_DIM) * MXU_DIM

C_DQ, C_DK, C_DV, C_SQ, C_SK, C_SV, C_END = 0, 512, 1024, 1536, 2048, 2176, 2304
NORM_GROUPS = (0, 1, 2, 3, 6, 7, 8)


def _cparams(sem):
    return pltpu.CompilerParams(dimension_semantics=sem, vmem_limit_bytes=VMEM_LIMIT)


def _t5_bucket_np(dist):
    n = np.maximum(dist, 0)
    max_exact = N_BUCKETS // 2
    nf = np.maximum(n, 1).astype(np.float32)
    large = max_exact + (np.log(nf / np.float32(max_exact)) / np.float32(math.log(MAX_DISTANCE / max_exact))
                         * np.float32(N_BUCKETS - max_exact)).astype(np.int32)
    large = np.minimum(large, N_BUCKETS - 1)
    return np.where(n < max_exact, n, large)


def _bias_tables(rel_bias, tq):
    nd = 2 * BLOCK
    buckets = _t5_bucket_np(np.arange(nd))
    assert (buckets[MAX_DISTANCE:] == N_BUCKETS - 1).all()
    rb = rel_bias.astype(F32)
    r = np.arange(BLOCK)[:, None]
    c = np.arange(BLOCK)[None, :]
    d_own = r - c
    d_prev = BLOCK + r - c
    far = rb[N_BUCKETS - 1]

    def take(dist, heads):
        idx = jnp.asarray(buckets[np.clip(dist, 0, nd - 1)], jnp.int32)[None]
        out = jnp.zeros((heads.stop - heads.start,) + dist.shape, F32)
        for b in range(N_BUCKETS):
            out = jnp.where(idx == b, rb[b, heads].reshape((-1,) + (1,) * dist.ndim), out)
        return out

    hd = slice(0, N_DIFF_HEADS)
    far_d = far[hd][:, None, None]
    d0 = jnp.where(d_own[None] >= 0, take(d_own, hd) - far_d, NEG)
    d1 = take(d_prev, hd) - far_d
    dblk = jnp.stack([d0, d1], axis=1)
    rq = np.arange(tq)[:, None]
    cm = np.arange(LANES)[None, :]
    d_meta = (N_META + rq - cm)[:, :N_META]
    bm0 = jnp.pad(take(d_meta, hd) - far_d, ((0, 0), (0, 0), (0, LANES - N_META)), constant_values=NEG)

    hs = slice(N_DIFF_HEADS, N_DIFF_HEADS + N_SWA_HEADS)
    far_s = far[hs][:, None, None]
    d_meta_s = N_META + r - cm
    meta_first = jnp.where((cm < N_META)[None], take(d_meta_s, hs), NEG)
    meta_rest = jnp.where((cm < N_META)[None], jnp.broadcast_to(far_s, (N_SWA_HEADS, BLOCK, LANES)), NEG)
    prev_rest = jnp.where((c > r)[None], take(d_prev, hs), NEG)
    prev_first = jnp.full((N_SWA_HEADS, BLOCK, BLOCK), NEG, F32)
    own = jnp.where((d_own >= 0)[None], take(d_own, hs), NEG)
    bt = jnp.stack([jnp.concatenate([prev_first, own, meta_first[..., :N_META]], axis=-1),
                    jnp.concatenate([prev_rest, own, meta_rest[..., :N_META]], axis=-1)], axis=0)
    return dblk.astype(F32), bm0.astype(F32), bt.astype(F32)


def _proj_kernel(x_ref, g1_ref, w_ref, bd_ref, gain_ref, nmask_ref, o_ref):
    x = x_ref[...]
    a = x * lax.rsqrt(jnp.mean(x * x, axis=-1, keepdims=True) + EPS) * g1_ref[...]
    p = jnp.dot(a.astype(BF16), w_ref[...], preferred_element_type=F32)
    bd = bd_ref[...]
    for j in range(C_END // MXU_DIM):
        sl = slice(j * MXU_DIM, (j + 1) * MXU_DIM)
        pj = p[:, sl]
        if j in NORM_GROUPS:
            ms = jnp.dot((pj * pj).astype(BF16), bd, preferred_element_type=F32)
            pj = jnp.where(nmask_ref[:, sl] != 0.0, pj * lax.rsqrt(ms + EPS) * gain_ref[:, sl], pj)
        o_ref[:, sl] = pj.astype(BF16)


def _proj(x2, g1, w, bd, gain, nmask, tm):
    n = x2.shape[0]
    return pl.pallas_call(
        _proj_kernel,
        out_shape=jax.ShapeDtypeStruct((n, C_END), BF16),
        grid=(n // tm,),
        in_specs=[
            pl.BlockSpec((tm, x2.shape[1]), lambda i: (i, 0)),
            pl.BlockSpec(g1.shape, lambda i: (0, 0)),
            pl.BlockSpec(w.shape, lambda i: (0, 0)),
            pl.BlockSpec(bd.shape, lambda i: (0, 0)),
            pl.BlockSpec(gain.shape, lambda i: (0, 0)),
            pl.BlockSpec(nmask.shape, lambda i: (0, 0)),
        ],
        out_specs=pl.BlockSpec((tm, C_END), lambda i: (i, 0)),
        compiler_params=_cparams(("parallel",)),
        name="proj",
    )(x2, g1, w, bd, gain, nmask)


def _diff_kernel(qi_tab, t_tab, q_ref, k_ref, v_ref, km_ref, vm_ref, d_ref, bm0_ref, lamv_ref, gain_ref, o_ref,
                 bias_ref, mb_ref, qs_ref, kt_ref, vt_ref, s_buf, p_buf, a_buf, m_ref, acc_ref, *,
                 lambda_init, n_steps, n_far, n_near):
    tq = TQ
    nq = q_ref.shape[0] // tq
    nb = tq // BLOCK
    BIAS_LEFT, BIAS_DIAG = 0, 1

    d0 = d_ref[0, 0] * LOG2E
    d1 = d_ref[0, 1] * LOG2E
    zeros = jnp.zeros((BLOCK, BLOCK), F32)
    for a in range(nb):
        for b in range(nb):
            rs, cs = slice(a * BLOCK, (a + 1) * BLOCK), slice(b * BLOCK, (b + 1) * BLOCK)
            if a == b:
                blk = d0
            elif b == a + 1:
                blk = d1
            elif b > a:
                blk = zeros
            else:
                blk = jnp.full((BLOCK, BLOCK), NEG, F32)
            bias_ref[BIAS_DIAG, rs, cs] = blk
            bias_ref[BIAS_LEFT, rs, cs] = d1 if (b == 0 and a == nb - 1) else zeros
    mb_ref[0] = jnp.zeros((N_META, tq), F32)
    mb_ref[1] = bm0_ref[0, :N_META, :] * LOG2E

    lane = lax.broadcasted_iota(jnp.int32, (tq, LANES), 1)
    for i in range(nq):
        rows = slice(i * tq, (i + 1) * tq)
        q = q_ref[rows, :].astype(F32)
        qs_ref[i] = jnp.transpose(jnp.concatenate([jnp.where(lane < HEAD_DIM, q, 0.0),
                                                   jnp.where(lane >= HEAD_DIM, q, 0.0)], axis=0)).astype(BF16)
        vt_ref[i, :LANES, :] = jnp.transpose(v_ref[rows, :].astype(F32)).astype(BF16)
        kt_ref[i] = k_ref[rows, :]
    vt_ref[nq, :LANES, :] = jnp.transpose(vm_ref[...].astype(F32)).astype(BF16)
    kt_ref[nq] = km_ref[...]
    vt_ref[:, LANES:, :] = jnp.ones((nq + 1, ONES_ROWS, tq), BF16)
    acc_ref[...] = jnp.zeros(acc_ref.shape, F32)
    m_ref[...] = jnp.full(m_ref.shape, NEG, F32)
    lv = lamv_ref[...]
    lam = (jnp.exp(jnp.sum(lv[0:1] * lv[1:2], axis=-1, keepdims=True))
           - jnp.exp(jnp.sum(lv[2:3] * lv[3:4], axis=-1, keepdims=True)) + lambda_init)

    FAR, NEAR, META = 0, 1, 2

    def stage_a(n, slot, kind):
        qi, t = qi_tab[n], t_tab[n]
        if kind == META:
            s = jnp.dot(kt_ref[nq, :N_META, :], qs_ref[qi], preferred_element_type=F32)
            s_buf[slot, :N_META] = s + jnp.tile(mb_ref[jnp.where(qi == 0, 1, 0)], (1, 2))
            return
        s = jnp.dot(kt_ref[t - 1], qs_ref[qi], preferred_element_type=F32)
        if kind == NEAR:
            s = s + jnp.tile(bias_ref[jnp.where(t == qi + 1, BIAS_DIAG, BIAS_LEFT)], (1, 2))
        s_buf[slot] = s

    def stage_b(n, slot, kind):
        qi = qi_tab[n]
        rows = slice(0, N_META if kind == META else tq)
        s = s_buf[slot, rows]
        m_prev = m_ref[qi]
        m_new = jnp.maximum(m_prev, jnp.max(s, axis=0, keepdims=True))
        a_buf[slot] = jnp.exp2(m_prev - m_new)
        p_buf[slot, rows] = jnp.exp2(s - m_new[0:1]).astype(BF16)
        m_ref[qi] = m_new

    def stage_c(n, slot, kind):
        qi, t = qi_tab[n], t_tab[n]
        if kind == META:
            pv = jnp.dot(vt_ref[nq, :, :N_META], p_buf[slot, :N_META], preferred_element_type=F32)
        else:
            pv = jnp.dot(vt_ref[t - 1], p_buf[slot], preferred_element_type=F32)
        acc_ref[qi] = a_buf[slot][0:1] * acc_ref[qi] + pv

    def pipeline(base, count, kind):
        LEAD, SLOTS, unroll = 2, 3, 12
        assert unroll % SLOTS == 0
        if count <= 2 * LEAD:
            for j in range(count):
                stage_a(base + j, 0, kind)
                stage_b(base + j, 0, kind)
                stage_c(base + j, 0, kind)
            return
        for j in range(2 * LEAD):
            stage_a(base + j, j % SLOTS, kind)
            if j >= LEAD:
                stage_b(base + j - LEAD, (j - LEAD) % SLOTS, kind)

        def steps(n, count):
            for j in range(count):
                stage_a(base + n + j + 2 * LEAD, (j + 2 * LEAD) % SLOTS, kind)
                stage_b(base + n + j + LEAD, (j + LEAD) % SLOTS, kind)
                stage_c(base + n + j, j % SLOTS, kind)

        n_steady = count - 2 * LEAD
        n_blocks = n_steady // unroll

        def block(k, carry):
            steps(unroll * k, unroll)
            return carry
        lax.fori_loop(0, n_blocks, block, 0)
        steps(unroll * n_blocks, n_steady - unroll * n_blocks)
        for j in range(n_steady, count):
            if j + LEAD < count:
                stage_b(base + j + LEAD, (j + LEAD) % SLOTS, kind)
            stage_c(base + j, j % SLOTS, kind)

    pipeline(0, n_far, FAR)
    pipeline(n_far, n_near, NEAR)
    pipeline(n_far + n_near, n_steps - n_far - n_near, META)

    for i in range(nq):
        acc = acc_ref[i]
        o = acc[:LANES] * (1.0 / acc[LANES:LANES + 1])
        d = o[:, :tq] - lam * o[:, tq:]
        y = d * lax.rsqrt(jnp.mean(d * d, axis=0, keepdims=True) + EPS) * jnp.tile(gain_ref[...], (1, tq // LANES))
        o_ref[i * tq:(i + 1) * tq, :] = jnp.transpose(y * (1.0 - lambda_init)).astype(BF16)


def _diff_attention(qkv, km, vm, dblk, bm0, lamv, gain, batch, seq, lambda_init):
    nq = seq // TQ
    far = [(qi, t) for qi in range(nq) for t in range(1, qi)]
    near = [(qi, t) for qi in range(nq) for t in (qi, qi + 1) if t >= 1]
    meta = [(qi, 0) for qi in range(nq)]
    steps = far + near + meta
    qi_tab = jnp.asarray([s[0] for s in steps], jnp.int32)
    t_tab = jnp.asarray([s[1] for s in steps], jnp.int32)
    kern = functools.partial(_diff_kernel, lambda_init=lambda_init, n_steps=len(steps), n_far=len(far),
                             n_near=len(near))
    return pl.pallas_call(
        kern,
        out_shape=jax.ShapeDtypeStruct((batch * seq, N_DIFF_HEADS * LANES), BF16),
        grid_spec=pltpu.PrefetchScalarGridSpec(
            num_scalar_prefetch=2,
            grid=(batch, N_DIFF_HEADS),
            in_specs=[
                pl.BlockSpec((seq, LANES), lambda b, h, *_: (b, C_DQ // LANES + h)),
                pl.BlockSpec((seq, LANES), lambda b, h, *_: (b, C_DK // LANES + h)),
                pl.BlockSpec((seq, LANES), lambda b, h, *_: (b, C_DV // LANES + h)),
                pl.BlockSpec((TQ, LANES), lambda b, h, *_: (0, h)),
                pl.BlockSpec((TQ, LANES), lambda b, h, *_: (0, h)),
                pl.BlockSpec((1, 2, BLOCK, BLOCK), lambda b, h, *_: (h, 0, 0, 0)),
                pl.BlockSpec((1, LANES, TQ), lambda b, h, *_: (h, 0, 0)),
                pl.BlockSpec(lamv.shape, lambda b, h, *_: (0, 0)),
                pl.BlockSpec((LANES, LANES), lambda b, h, *_: (0, 0)),
            ],
            out_specs=pl.BlockSpec((seq, LANES), lambda b, h, *_: (b, h)),
            scratch_shapes=[
                pltpu.VMEM((2, TQ, TQ), F32),
                pltpu.VMEM((2, N_META, TQ), F32),
                pltpu.VMEM((nq, LANES, 2 * TQ), BF16),
                pltpu.VMEM((nq + 1, TQ, LANES), BF16),
                pltpu.VMEM((nq + 1, LANES + ONES_ROWS, TQ), BF16),
                pltpu.VMEM((3, TQ, 2 * TQ), F32),
                pltpu.VMEM((3, TQ, 2 * TQ), BF16),
                pltpu.VMEM((3, 8, 2 * TQ), F32),
                pltpu.VMEM((nq, 8, 2 * TQ), F32),
                pltpu.VMEM((nq, LANES + ONES_ROWS, 2 * TQ), F32),
            ],
        ),
        compiler_params=_cparams(("parallel", "parallel")),
        name="diff_attention",
    )(qi_tab, t_tab, qkv, qkv, qkv, km, vm, jnp.swapaxes(dblk, -1, -2), jnp.swapaxes(bm0, -1, -2), lamv,
      jnp.broadcast_to(gain.reshape(LANES, 1), (LANES, LANES)))


def _swa_kernel(sink_ref, q_ref, k_ref, v_ref, km_ref, vm_ref, bt_ref, o_ref, kd_ref, vt_ref,
                s_scr, p_scr, inv_scr):
    nkb = k_ref.shape[0] // BLOCK
    lane = lax.broadcasted_iota(jnp.int32, (BLOCK, LANES), 1)
    pairs = [(g, u) for g in range(N_SWA_KV) for u in range(2)]

    def both_halves(k):
        k0, k1 = k[:, :HEAD_DIM], k[:, HEAD_DIM:]
        return jnp.concatenate([k0, k0, k1, k1], axis=1)

    def prepare(j, carry):
        rows = pl.ds(pl.multiple_of(j * BLOCK, BLOCK), BLOCK)
        kd_ref[j] = both_halves(k_ref[rows, :])
        vt_ref[j] = jnp.transpose(v_ref[rows, :].astype(F32)).astype(BF16)
        return carry
    lax.fori_loop(0, nkb, prepare, 0)
    kd_ref[nkb] = both_halves(km_ref[...])
    vt_ref[nkb] = jnp.transpose(vm_ref[...].astype(F32)).astype(BF16)

    def scores(n, slot):
        first = jnp.where(n == 0, 0, 1)
        prev = jnp.maximum(n - 1, 0)
        r_q = pl.multiple_of(n * BLOCK, BLOCK)
        for c, (g, u) in enumerate(pairs):
            ks = slice(g * LANES, (g + 1) * LANES)
            kcat = jnp.concatenate([kd_ref[prev, :, ks], kd_ref[n, :, ks], kd_ref[nkb, :N_META, ks]], axis=0)
            h0 = 4 * g + 2 * u
            qp = q_ref[pl.ds(r_q, BLOCK), (2 * g + u) * LANES:(2 * g + u + 1) * LANES].astype(F32)
            qs = jnp.transpose(jnp.concatenate([jnp.where(lane < HEAD_DIM, qp, 0.0),
                                                jnp.where(lane >= HEAD_DIM, qp, 0.0)], axis=0)).astype(BF16)
            s = jnp.dot(kcat, qs, preferred_element_type=F32)
            s_scr[slot, c] = s + jnp.concatenate([bt_ref[first, h0], bt_ref[first, h0 + 1]], axis=1)

    def exponentials(slot):
        for c, (g, u) in enumerate(pairs):
            h0 = 4 * g + 2 * u
            s = s_scr[slot, c]
            sink = jnp.concatenate([sink_ref[h0:h0 + 1, :], sink_ref[h0 + 1:h0 + 2, :]], axis=1)
            m = jnp.maximum(jnp.max(s, axis=0, keepdims=True), sink)
            p = jnp.exp(s - m)
            p_scr[slot, c] = p.astype(BF16)
            inv_scr[slot, c] = jnp.broadcast_to(1.0 / (jnp.sum(p, axis=0, keepdims=True) + jnp.exp(sink - m)),
                                                inv_scr.shape[2:])

    def values(n, slot):
        prev = jnp.maximum(n - 1, 0)
        r_q = pl.multiple_of(n * BLOCK, BLOCK)
        for c, (g, u) in enumerate(pairs):
            vs = slice(g * HEAD_DIM, (g + 1) * HEAD_DIM)
            vcat = jnp.concatenate([vt_ref[prev, vs, :], vt_ref[n, vs, :]], axis=1)
            o = (jnp.dot(vcat, p_scr[slot, c, :2 * BLOCK], preferred_element_type=F32)
                 + jnp.dot(vt_ref[nkb, vs, :N_META], p_scr[slot, c, 2 * BLOCK:], preferred_element_type=F32)
                 ) * inv_scr[slot, c][0:1]
            ot = jnp.transpose(o)
            o_ref[pl.ds(r_q, BLOCK), (2 * g + u) * LANES:(2 * g + u + 1) * LANES] = (
                jnp.concatenate([ot[:BLOCK], ot[BLOCK:]], axis=1).astype(BF16))

    scores(0, 0)
    scores(1, 1)
    exponentials(0)

    def blocks(n, count):
        for j in range(count):
            scores(n + j + 2, j % 2)
            exponentials((j + 1) % 2)
            values(n + j, j % 2)

    unroll = 4
    n_steady = nkb - 2
    lax.fori_loop(0, n_steady // unroll, lambda k, carry: (blocks(unroll * k, unroll), carry)[1], 0)
    blocks(n_steady // unroll * unroll, n_steady % unroll)
    exponentials(1)
    values(nkb - 2, 0)
    values(nkb - 1, 1)


def _swa_attention(sinks, qkv, km, vm, bt, batch, seq):
    nkb = seq // BLOCK
    swa_q = N_SWA_HEADS * HEAD_DIM
    assert nkb % 2 == 0
    sinkv =jnp.broadcast_to(sinks.reshape(N_SWA_HEADS, 1), (N_SWA_HEADS, LANES))
    return pl.pallas_call(
        _swa_kernel,
        out_shape=jax.ShapeDtypeStruct((batch * seq, N_SWA_HEADS * HEAD_DIM), BF16),
        grid=(batch,),
        in_specs=[
            pl.BlockSpec(sinkv.shape, lambda b: (0, 0)),
            pl.BlockSpec((seq, swa_q), lambda b: (b, C_SQ // swa_q)),
            pl.BlockSpec((seq, LANES), lambda b: (b, C_SK // LANES)),
            pl.BlockSpec((seq, LANES), lambda b: (b, C_SV // LANES)),
            pl.BlockSpec(km.shape, lambda b: (0, 0)),
            pl.BlockSpec(vm.shape, lambda b: (0, 0)),
            pl.BlockSpec(bt.shape, lambda b: (0, 0, 0, 0)),
        ],
        out_specs=pl.BlockSpec((seq, swa_q), lambda b: (b, 0)),
        scratch_shapes=[pltpu.VMEM((nkb + 1, BLOCK, 2 * LANES), BF16),
                        pltpu.VMEM((nkb + 1, LANES, BLOCK), BF16),
                        pltpu.VMEM((2, 4, 2 * BLOCK + N_META, 2 * BLOCK), F32),
                        pltpu.VMEM((2, 4, 2 * BLOCK + N_META, 2 * BLOCK), BF16),
                        pltpu.VMEM((2, 4, 8, 2 * BLOCK), F32)],
        compiler_params=_cparams(("parallel",)),
        name="swa_attention",
    )(sinkv, qkv, qkv, qkv, km, vm, bt)


def _outproj_kernel(x_ref, md_ref, ms_ref, wo_ref, g2_ref, wr_ref, br_ref,
                    h_ref, hb_ref, rt_ref, ti_ref, cnt_ref, c_ref, lg_ref):
    i = pl.program_id(0)

    @pl.when(i == 0)
    def _init():
        c_ref[...] = jnp.zeros(c_ref.shape, F32)
        lg_ref[...] = jnp.zeros(lg_ref.shape, F32)

    lg_prev = lg_ref[...]
    half = md_ref.shape[1]
    h = (x_ref[...]
         + jnp.dot(md_ref[...], wo_ref[:half, :], preferred_element_type=F32)
         + jnp.dot(ms_ref[...], wo_ref[half:, :], preferred_element_type=F32))
    h_ref[...] = h
    hn = h * lax.rsqrt(jnp.mean(h * h, axis=-1, keepdims=True) + EPS) * g2_ref[...]
    hb = hn.astype(BF16)
    hb_ref[...] = hb
    lg_ref[...] = jnp.dot(hb, wr_ref[...], preferred_element_type=F32) + br_ref[...]
    _route_tile(lg_prev, jnp.where(i > 0, 1.0, 0.0), rt_ref, ti_ref, c_ref)

    @pl.when(i == pl.num_programs(0) - 1)
    def _fin():
        cnt_ref[...] = c_ref[...]


def _route_tile(lg, live, rt_ref, ti_ref, c_ref):
    tm = lg.shape[0]
    lane_i = lax.broadcasted_iota(jnp.int32, lg.shape, 1)
    lane = lane_i.astype(F32)
    big = float(4 * LANES)
    is_g = (lane_i >= N_EXPERTS) & (lane_i < N_EXPERTS + N_GROUPS)
    glm = jnp.where(is_g, lg, -jnp.inf)
    gmax = jnp.max(glm, axis=1, keepdims=True)
    gidx = jnp.min(jnp.where(glm == gmax, lane, big), axis=1, keepdims=True) - N_EXPERTS
    gsum = jnp.sum(jnp.where(is_g, jnp.exp(lg - gmax), 0.0), axis=1, keepdims=True)
    g_w = 1.0 / gsum
    lane_grp = (lane_i >> 3).astype(F32)
    in_grp = (lane_i < N_EXPERTS) & (lane_grp == gidx)
    el = jnp.where(in_grp, lg, -jnp.inf)
    t1 = jnp.max(el, axis=1, keepdims=True)
    j1 = jnp.min(jnp.where(el == t1, lane, big), axis=1, keepdims=True)
    el2 = jnp.where(lane == j1, -jnp.inf, el)
    t2 = jnp.max(el2, axis=1, keepdims=True)
    j2 = jnp.min(jnp.where(el2 == t2, lane, big), axis=1, keepdims=True)
    e2 = jnp.exp(t2 - t1)
    den = 1.0 + e2
    gate1 = g_w / den
    gate2 = g_w * e2 / den

    o1 = lane == j1
    o2 = lane == j2
    onehot = jnp.where(o1 | o2, 1.0, 0.0).astype(BF16)
    rr = lax.broadcasted_iota(jnp.int32, (tm, tm), 0)
    cc = lax.broadcasted_iota(jnp.int32, (tm, tm), 1)
    lower = jnp.where(rr > cc, 1.0, 0.0).astype(BF16)
    pfx = jnp.dot(lower, onehot, preferred_element_type=F32)
    cnt_tile = jnp.sum(onehot.astype(F32), axis=0, keepdims=True)
    groups = jnp.floor((cnt_tile + (ROW_ALIGN - 1)) * (1.0 / ROW_ALIGN))
    er = lax.broadcasted_iota(jnp.int32, (LANES, LANES), 0)
    ec = lax.broadcasted_iota(jnp.int32, (LANES, LANES), 1)
    before = jnp.where(er < ec, 1.0, 0.0).astype(BF16)
    cbase = ROW_ALIGN * jnp.dot(jnp.broadcast_to(groups, (8, LANES)).astype(BF16), before,
                                preferred_element_type=F32)[0:1]
    at = pfx + cbase
    pos1 = jnp.sum(jnp.where(o1, at, 0.0), axis=1, keepdims=True)
    pos2 = jnp.sum(jnp.where(o2, at, 0.0), axis=1, keepdims=True)
    rt_ref[...] = jnp.where(lane_i == 0, gate1,
                            jnp.where(lane_i == 1, gate2,
                                      jnp.where(lane_i == 2, pos1,
                                                jnp.where(lane_i == 3, pos2, 0.0))))
    c_old = c_ref[...]
    c_ref[...] = c_old + groups * (ROW_ALIGN * live)
    row8 = lax.broadcasted_iota(jnp.int32, (8, LANES), 0)
    ti_ref[...] = jnp.where(row8 == 0, cnt_tile, jnp.where(row8 == 1, c_old, 0.0))


def _outproj(x2, mixd, mixs, wo, g2, wr, br):
    n, d = x2.shape
    nt = n // TM

    def proj_tile(i):
        return (jnp.minimum(i, nt - 1), 0)

    def route_tile(i):
        return (jnp.maximum(i - 1, 0), 0)

    return pl.pallas_call(
        _outproj_kernel,
        out_shape=(jax.ShapeDtypeStruct((n, d), F32),
                   jax.ShapeDtypeStruct((n, d), BF16),
                   jax.ShapeDtypeStruct((n, LANES), F32),
                   jax.ShapeDtypeStruct((nt * 8, LANES), F32),
                   jax.ShapeDtypeStruct((8, LANES), F32)),
        grid=(nt + 1,),
        in_specs=[
            pl.BlockSpec((TM, d), proj_tile),
            pl.BlockSpec((TM, mixd.shape[1]), proj_tile),
            pl.BlockSpec((TM, mixs.shape[1]), proj_tile),
            pl.BlockSpec(wo.shape, lambda i: (0, 0)),
            pl.BlockSpec(g2.shape, lambda i: (0, 0)),
            pl.BlockSpec(wr.shape, lambda i: (0, 0)),
            pl.BlockSpec(br.shape, lambda i: (0, 0)),
        ],
        out_specs=(pl.BlockSpec((TM, d), proj_tile),
                   pl.BlockSpec((TM, d), proj_tile),
                   pl.BlockSpec((TM, LANES), route_tile),
                   pl.BlockSpec((8, LANES), route_tile),
                   pl.BlockSpec((8, LANES), lambda i: (0, 0))),
        scratch_shapes=[pltpu.VMEM((8, LANES), F32), pltpu.VMEM((TM, LANES), F32)],
        compiler_params=_cparams(("arbitrary",)),
        name="outproj_router",
    )(x2, mixd, mixs, wo, g2, wr, br)


def _for_each_chunk(runs_ref, fn):
    big = CHUNK_ROWS[0]

    def per_expert(e, sorted_row, priority):
        start = runs_ref[0, 0, e]
        groups = runs_ref[0, 0, N_EXPERTS + e]
        whole = groups // (big // ROW_ALIGN)

        def per_chunk(c, carry):
            fn(pl.multiple_of(start + c * big, ROW_ALIGN), pl.multiple_of(sorted_row + c * big, ROW_ALIGN),
               big, priority)
            return carry
        lax.fori_loop(0, whole, per_chunk, 0)

        done = whole * big
        for rows in CHUNK_ROWS[1:]:
            has = (groups // (rows // ROW_ALIGN)) % 2

            @pl.when(has == 1)
            def _(done=done, rows=rows):
                fn(pl.multiple_of(start + done, ROW_ALIGN), pl.multiple_of(sorted_row + done, ROW_ALIGN),
                   rows, priority)
            done = done + has * rows
        return sorted_row + groups * ROW_ALIGN

    def expert_pair(e2, sorted_row):
        return per_expert(2 * e2 + 1, per_expert(2 * e2, sorted_row, 0), 1)
    lax.fori_loop(0, N_EXPERTS // 2, expert_pair, 0)


def _wait_chunks(runs_ref, make_copy):
    for k, rows in enumerate(CHUNK_ROWS):
        def body(c, carry, rows=rows):
            make_copy(rows).wait()
            return carry
        lax.fori_loop(0, runs_ref[0, 0, 2 * N_EXPERTS + k], body, 0)


def _dispatch_kernel(zf_ref, cur_ref, prv_ref, hb_ref, rt_ref, xs_ref, sbuf, zbuf, sem, zsem):
    i = pl.program_id(0)
    nt = pl.num_programs(0)
    slot = i % 2
    tm, d = hb_ref.shape

    def for_zero_blocks(kind, fn):
        def body(b, carry):
            @pl.when(zf_ref[b] == kind)
            def _():
                fn(pltpu.make_async_copy(zbuf, xs_ref.at[pl.ds(pl.multiple_of(b * EB, EB), EB)],
                                         zsem.at[kind - 1]))
            return carry
        lax.fori_loop(0, zf_ref.shape[0], body, 0)

    @pl.when(i == 0)
    def _():
        zbuf[...] = jnp.zeros(zbuf.shape, zbuf.dtype)
        for_zero_blocks(1, lambda c: c.start())
        for_zero_blocks(2, lambda c: c.start())
        for_zero_blocks(1, lambda c: c.wait())

    pos_t = jnp.transpose(rt_ref[...])
    srow = lax.broadcasted_iota(jnp.int32, (SROWS, tm), 0).astype(F32)
    sel = jnp.where(srow == pos_t[2:3, :], 1.0, jnp.where(srow == pos_t[3:4, :], 1.0, 0.0)).astype(BF16)
    srt = jnp.dot(sel, hb_ref[...], preferred_element_type=F32)
    bits = pltpu.bitcast(srt, jnp.uint32)
    sbuf[slot] = (bits[:, d // 2:] & jnp.uint32(0xFFFF0000)) | (bits[:, :d // 2] >> 16)

    def chunk_copy(run_row, sorted_row, rows, sl):
        return pltpu.make_async_copy(sbuf.at[sl, pl.ds(sorted_row, rows)], xs_ref.at[pl.ds(run_row, rows)],
                                     sem.at[sl])

    _for_each_chunk(cur_ref, lambda run_row, sorted_row, rows, priority:
                    chunk_copy(run_row, sorted_row, rows, slot).start(priority=priority))

    @pl.when(i > 0)
    def _():
        _wait_chunks(prv_ref, lambda rows: chunk_copy(0, 0, rows, 1 - slot))

    @pl.when(i == nt - 1)
    def _():
        _wait_chunks(cur_ref, lambda rows: chunk_copy(0, 0, rows, slot))
        for_zero_blocks(2, lambda c: c.wait())


def _dispatch(zero_blocks, runs, hb, rt, n_rows):
    n, d = hb.shape
    return pl.pallas_call(
        _dispatch_kernel,
        out_shape=jax.ShapeDtypeStruct((n_rows, d // 2), jnp.uint32),
        grid_spec=pltpu.PrefetchScalarGridSpec(
            num_scalar_prefetch=1,
            grid=(n // TM,),
            in_specs=[
                pl.BlockSpec((1, 1, LANES), lambda i, zf: (i, 0, 0), memory_space=pltpu.SMEM),
                pl.BlockSpec((1, 1, LANES), lambda i, zf: (jnp.maximum(i - 1, 0), 0, 0), memory_space=pltpu.SMEM),
                pl.BlockSpec((TM, d), lambda i, zf: (i, 0)),
                pl.BlockSpec((TM, LANES), lambda i, zf: (i, 0)),
            ],
            out_specs=pl.BlockSpec(memory_space=pl.ANY),
            scratch_shapes=[pltpu.VMEM((2, SROWS, d // 2), jnp.uint32), pltpu.VMEM((EB, d // 2), jnp.uint32),
                            pltpu.SemaphoreType.DMA((2,)), pltpu.SemaphoreType.DMA((2,))],
        ),
        compiler_params=_cparams(("arbitrary",)),
        name="dispatch",
    )(zero_blocks, runs, runs, hb, rt)


def _experts_kernel(be_ref, na_ref, nxt_ref, xs_ref, wg_hbm, wu_hbm, wd_hbm, ys_ref,
                    wgf, wuf, wdf, wgb, wub, wdb, sem):
    b = pl.program_id(0)

    def weight_copies(e):
        return (pltpu.make_async_copy(wg_hbm.at[e], wgf, sem.at[0]),
                pltpu.make_async_copy(wu_hbm.at[e], wuf, sem.at[1]),
                pltpu.make_async_copy(wd_hbm.at[e], wdf, sem.at[2]))

    @pl.when(b == 0)
    def _():
        for c in weight_copies(be_ref[0]):
            c.start()

    @pl.when(b < na_ref[0])
    def _():
        e = be_ref[b]
        changed = jnp.logical_or(b == 0, be_ref[jnp.maximum(b - 1, 0)] != e)

        @pl.when(changed)
        def _load():
            for c in weight_copies(e):
                c.wait()
            wgb[...] = wgf[...].astype(BF16)
            wub[...] = wuf[...].astype(BF16)
            wdb[...] = wdf[...].astype(BF16)
            nxt = nxt_ref[e]

            @pl.when(nxt >= 0)
            def _():
                for c in weight_copies(nxt):
                    c.start()

        w = xs_ref[...]
        x_lo = pltpu.bitcast(w << 16, F32).astype(BF16)
        x_hi = pltpu.bitcast(w & jnp.uint32(0xFFFF0000), F32).astype(BF16)
        dh = w.shape[1]
        g = (jnp.dot(x_lo, wgb[:dh, :], preferred_element_type=F32)
             + jnp.dot(x_hi, wgb[dh:, :], preferred_element_type=F32))
        u = (jnp.dot(x_lo, wub[:dh, :], preferred_element_type=F32)
             + jnp.dot(x_hi, wub[dh:, :], preferred_element_type=F32))
        hdn = g * (1.0 / (1.0 + jnp.exp(-g))) * u
        y = jnp.dot(hdn.astype(BF16), wdb[...], preferred_element_type=F32)
        bits = pltpu.bitcast(y.astype(BF16).astype(F32), jnp.uint32)
        ys_ref[...] = (bits[:, dh:] & jnp.uint32(0xFFFF0000)) | (bits[:, :dh] >> 16)

    @pl.when(b >= na_ref[0])
    def _():
        ys_ref[...] = jnp.zeros(ys_ref.shape, ys_ref.dtype)


def _experts(blk_e, n_act, nxt_e, xs, w_gate, w_up, w_down):
    p, dh = xs.shape
    d = 2 * dh
    de = w_gate.shape[2]

    def row_map(b, be, na, nx):
        return (jnp.minimum(b, na[0] - 1), 0)

    return pl.pallas_call(
        _experts_kernel,
        out_shape=jax.ShapeDtypeStruct((p, dh), jnp.uint32),
        grid_spec=pltpu.PrefetchScalarGridSpec(
            num_scalar_prefetch=3,
            grid=(p // EB,),
            in_specs=[
                pl.BlockSpec((EB, dh), row_map),
                pl.BlockSpec(memory_space=pl.ANY),
                pl.BlockSpec(memory_space=pl.ANY),
                pl.BlockSpec(memory_space=pl.ANY),
            ],
            out_specs=pl.BlockSpec((EB, dh), lambda b, be, na, nx: (b, 0)),
            scratch_shapes=[pltpu.VMEM((d, de), F32), pltpu.VMEM((d, de), F32), pltpu.VMEM((de, d), F32),
                            pltpu.VMEM((d, de), BF16), pltpu.VMEM((d, de), BF16), pltpu.VMEM((de, d), BF16),
                            pltpu.SemaphoreType.DMA((3,))],
        ),
        compiler_params=_cparams(("arbitrary",)),
        name="experts",
    )(blk_e, n_act, nxt_e, xs, w_gate, w_up, w_down)


def _combine_kernel(cur_ref, nxt_ref, ys_ref, h_ref, rt_ref, o_ref, ybuf, sem):
    i = pl.program_id(0)
    nt = pl.num_programs(0)
    slot = i % 2
    tm = h_ref.shape[0]

    def chunk_copy(run_row, sorted_row, rows, sl):
        return pltpu.make_async_copy(ys_ref.at[pl.ds(run_row, rows)], ybuf.at[sl, pl.ds(sorted_row, rows)],
                                     sem.at[sl])

    @pl.when(i == 0)
    def _():
        ybuf[...] = jnp.zeros(ybuf.shape, ybuf.dtype)
        _for_each_chunk(cur_ref, lambda run_row, sorted_row, rows, priority:
                        chunk_copy(run_row, sorted_row, rows, 0).start(priority=priority))

    @pl.when(i + 1 < nt)
    def _():
        _for_each_chunk(nxt_ref, lambda run_row, sorted_row, rows, priority:
                        chunk_copy(run_row, sorted_row, rows, 1 - slot).start(priority=priority))

    _wait_chunks(cur_ref, lambda rows: chunk_copy(0, 0, rows, slot))

    rt = rt_ref[...]
    w = ybuf[slot]
    dh = w.shape[1]
    y_lo = pltpu.bitcast(w << 16, F32).astype(BF16)
    y_hi = pltpu.bitcast(w & jnp.uint32(0xFFFF0000), F32).astype(BF16)
    col = lax.broadcasted_iota(jnp.int32, (tm, SROWS), 1).astype(F32)
    wsel = jnp.where(col == rt[:, 2:3], rt[:, 0:1], jnp.where(col == rt[:, 3:4], rt[:, 1:2], 0.0)).astype(BF16)
    for half, yb in ((slice(0, dh), y_lo), (slice(dh, 2 * dh), y_hi)):
        o_ref[:, half] = h_ref[:, half] + jnp.dot(wsel, yb, preferred_element_type=F32)


def _combine(runs, ys, h1, rt):
    n, d = h1.shape
    nt = n // TM
    return pl.pallas_call(
        _combine_kernel,
        out_shape=jax.ShapeDtypeStruct((n, d), F32),
        grid=(nt,),
        in_specs=[
            pl.BlockSpec((1, 1, LANES), lambda i: (i, 0, 0), memory_space=pltpu.SMEM),
            pl.BlockSpec((1, 1, LANES), lambda i: (jnp.minimum(i + 1, nt - 1), 0, 0), memory_space=pltpu.SMEM),
            pl.BlockSpec(memory_space=pl.ANY),
            pl.BlockSpec((TM, d), lambda i: (i, 0)),
            pl.BlockSpec((TM, LANES), lambda i: (i, 0)),
        ],
        out_specs=pl.BlockSpec((TM, d), lambda i: (i, 0)),
        scratch_shapes=[pltpu.VMEM((2, SROWS, d // 2), jnp.uint32), pltpu.SemaphoreType.DMA((2,))],
        compiler_params=_cparams(("arbitrary",)),
        name="combine",
    )(runs, runs, ys, h1, rt)


def kernel(x, meta_tokens, rel_bias, norm1_gain, w_in, diff_q_gain, diff_k_gain, lam_q1, lam_k1, lam_q2, lam_k2, diff_subln_gain, swa_q_gain, swa_k_gain, swa_sinks, w_out, norm2_gain, w_group, b_group, w_router, b_router, w_gate, w_up, w_down):
    batch, seq, d = x.shape
    depth = w_in.shape[0]
    n = batch * seq
    assert seq % TQ == 0 and n % TM == 0 and n % TP == 0 and d == 1024
    assert meta_tokens.shape[0] == N_META
    assert depth == 1, "the meta-token rows of the residual stream are not carried across layers"

    h = x.reshape(n, d)
    dblk, bm0, bt = _bias_tables(rel_bias, TQ)
    scale = HEAD_DIM ** -0.5
    bd = jnp.asarray(np.kron(np.eye(MXU_DIM // HEAD_DIM), np.full((HEAD_DIM, HEAD_DIM), 1.0 / HEAD_DIM)), BF16)
    ones = jnp.ones((HEAD_DIM,), F32)
    lower_pad = N_EXPERTS + N_GROUPS

    for layer in range(depth):
        lambda_init = 0.8 - 0.6 * math.exp(-0.3 * layer)
        w_cat = w_in[layer].astype(BF16)
        gain = jnp.concatenate([
            jnp.tile(diff_q_gain[layer] * (scale * LOG2E), 2 * N_DIFF_HEADS),
            jnp.tile(diff_k_gain[layer], 2 * N_DIFF_HEADS),
            jnp.tile(ones, 2 * N_DIFF_HEADS),
            jnp.tile(swa_q_gain[layer] * scale, N_SWA_HEADS),
            jnp.tile(swa_k_gain[layer], N_SWA_KV),
            jnp.tile(ones, N_SWA_KV)]).reshape(1, C_END).astype(F32)
        nmask = np.zeros((1, C_END), np.float32)
        nmask[:, C_DQ:C_DV] = 1.0
        nmask[:, C_SQ:C_SV] = 1.0
        nmask = jnp.asarray(nmask)
        g1 = norm1_gain[layer].reshape(1, d).astype(F32)

        qkv = _proj(h, g1, w_cat, bd, gain, nmask, TP)
        qkv_meta = _proj(meta_tokens.astype(F32), g1, w_cat, bd, gain, nmask, N_META)
        meta_pad = jnp.pad(qkv_meta, ((0, TQ - N_META), (0, 0)))

        lamv = jnp.pad(jnp.stack([lam_q1[layer], lam_k1[layer], lam_q2[layer], lam_k2[layer]]).astype(F32),
                       ((0, 4), (0, LANES - HEAD_DIM)))
        mixd = _diff_attention(qkv, meta_pad[:, C_DK:C_DV], meta_pad[:, C_DV:C_SQ], dblk, bm0, lamv,
                               diff_subln_gain[layer].reshape(1, LANES).astype(F32), batch, seq, lambda_init)
        mixs = _swa_attention(swa_sinks[layer].astype(F32), qkv, meta_pad[:BLOCK, C_SK:C_SV],
                              meta_pad[:BLOCK, C_SV:C_END], jnp.swapaxes(bt, -1, -2), batch, seq)

        wr = jnp.pad(jnp.concatenate([w_router[layer], w_group[layer]], axis=1),
                     ((0, 0), (0, LANES - lower_pad))).astype(BF16)
        br = jnp.pad(jnp.concatenate([b_router[layer], b_group[layer]]), (0, LANES - lower_pad)).reshape(1, LANES)
        h1, hb, rt, tinfo, cnt = _outproj(h, mixd, mixs, w_out[layer].astype(BF16),
                                          norm2_gain[layer].reshape(1, d).astype(F32), wr, br.astype(F32))

        nt = n // TM
        counts = cnt[0, :N_EXPERTS].astype(jnp.int32)
        nblk_e = (counts + EB - 1) // EB
        blk_end = jnp.cumsum(nblk_e)
        pstart = ((blk_end - nblk_e) * EB).astype(jnp.int32)
        n_blocks = -(-(2 * n + nt * N_EXPERTS * (ROW_ALIGN - 1) + N_EXPERTS * (EB - 1)) // EB)
        blk_ids = jnp.arange(n_blocks)
        blk_e = jnp.minimum(jnp.sum(blk_end[None, :] <= blk_ids[:, None], axis=1), N_EXPERTS - 1).astype(jnp.int32)
        n_act = blk_end[-1:].astype(jnp.int32)
        is_last = jnp.any((blk_end[None, :] == blk_ids[:, None] + 1) & (nblk_e[None, :] > 0), axis=1)
        zero_blocks = jnp.where(blk_ids >= n_act[0], 2, jnp.where(is_last, 1, 0)).astype(jnp.int32)
        ti = tinfo.reshape(nt, 8, LANES)
        run_len = ti[:, 0, :N_EXPERTS].astype(jnp.int32)
        run_start = pstart[None, :] + ti[:, 1, :N_EXPERTS].astype(jnp.int32)
        run_groups = (run_len + ROW_ALIGN - 1) // ROW_ALIGN
        n_copies = [jnp.sum(run_groups // (CHUNK_ROWS[0] // ROW_ALIGN), axis=1, keepdims=True)]
        n_copies += [jnp.sum((run_groups // (rows // ROW_ALIGN)) % 2, axis=1, keepdims=True)
                     for rows in CHUNK_ROWS[1:]]
        runs = jnp.concatenate([run_start, run_groups] + n_copies
                               + [jnp.zeros((nt, LANES - 2 * N_EXPERTS - len(CHUNK_ROWS)), jnp.int32)],
                               axis=1).reshape(nt, 1, LANES)

        xs = _dispatch(zero_blocks, runs, hb, rt, n_blocks * EB)
        own = jnp.where(nblk_e > 0, jnp.arange(N_EXPERTS), N_EXPERTS)
        later = jnp.concatenate([lax.cummin(own[::-1])[::-1][1:], jnp.full((1,), N_EXPERTS)])
        nxt_e = jnp.where(later < N_EXPERTS, later, -1).astype(jnp.int32)
        ys = _experts(blk_e, n_act, nxt_e, xs, w_gate[layer], w_up[layer], w_down[layer])
        h = _combine(runs, ys, h1, rt)
    return h.reshape(batch, seq, d)
```

```python
import functools
import math

import numpy as np
import jax
import jax.numpy as jnp
from jax import lax
from jax.experimental import pallas as pl
from jax.experimental.pallas import tpu as pltpu

F32 = jnp.float32
BF16 = jnp.bfloat16

HEAD_DIM = 64
N_DIFF_HEADS = 4
N_SWA_HEADS = 8
N_SWA_KV = 2
BLOCK = 128
N_META = 16
N_BUCKETS = 32
MAX_DISTANCE = 128
N_GROUPS = 4
EXPERTS_PER_GROUP = 8
N_EXPERTS = N_GROUPS * EXPERTS_PER_GROUP
EPS = 1e-6
NEG = -1e30
LOG2E = math.log2(math.e)

LANES = 128
MXU_DIM = 256
V7X_VMEM_BYTES = 64 * 1024 * 1024
VMEM_LIMIT = V7X_VMEM_BYTES * 3 // 4

TP = 1024
TM = 512
TQ = 256
ONES_ROWS = 16
EB = 512
ROW_ALIGN = 8
CHUNK_ROWS = (32, 16, 8)
SROWS = -(-(2 * TM + N_EXPERTS * (ROW_ALIGN - 1)) // MXU_DIM) * MXU_DIM

C_DQ, C_DK, C_DV, C_SQ, C_SK, C_SV, C_END = 0, 512, 1024, 1536, 2048, 2176, 2304
NORM_GROUPS = (0, 1, 2, 3, 6, 7, 8)


def _cparams(sem):
    return pltpu.CompilerParams(dimension_semantics=sem, vmem_limit_bytes=VMEM_LIMIT)


def _t5_bucket_np(dist):
    n = np.maximum(dist, 0)
    max_exact = N_BUCKETS // 2
    nf = np.maximum(n, 1).astype(np.float32)
    large = max_exact + (np.log(nf / np.float32(max_exact)) / np.float32(math.log(MAX_DISTANCE / max_exact))
                         * np.float32(N_BUCKETS - max_exact)).astype(np.int32)
    large = np.minimum(large, N_BUCKETS - 1)
    return np.where(n < max_exact, n, large)


def _bias_tables(rel_bias, tq):
    nd = 2 * BLOCK
    buckets = _t5_bucket_np(np.arange(nd))
    assert (buckets[MAX_DISTANCE:] == N_BUCKETS - 1).all()
    rb = rel_bias.astype(F32)
    r = np.arange(BLOCK)[:, None]
    c = np.arange(BLOCK)[None, :]
    d_own = r - c
    d_prev = BLOCK + r - c
    far = rb[N_BUCKETS - 1]

    def take(dist, heads):
        idx = jnp.asarray(buckets[np.clip(dist, 0, nd - 1)], jnp.int32)[None]
        out = jnp.zeros((heads.stop - heads.start,) + dist.shape, F32)
        for b in range(N_BUCKETS):
            out = jnp.where(idx == b, rb[b, heads].reshape((-1,) + (1,) * dist.ndim), out)
        return out

    hd = slice(0, N_DIFF_HEADS)
    far_d = far[hd][:, None, None]
    d0 = jnp.where(d_own[None] >= 0, take(d_own, hd) - far_d, NEG)
    d1 = take(d_prev, hd) - far_d
    dblk = jnp.stack([d0, d1], axis=1)
    rq = np.arange(tq)[:, None]
    cm = np.arange(LANES)[None, :]
    d_meta = (N_META + rq - cm)[:, :N_META]
    bm0 = jnp.pad(take(d_meta, hd) - far_d, ((0, 0), (0, 0), (0, LANES - N_META)), constant_values=NEG)

    hs = slice(N_DIFF_HEADS, N_DIFF_HEADS + N_SWA_HEADS)
    far_s = far[hs][:, None, None]
    d_meta_s = N_META + r - cm
    meta_first = jnp.where((cm < N_META)[None], take(d_meta_s, hs), NEG)
    meta_rest = jnp.where((cm < N_META)[None], jnp.broadcast_to(far_s, (N_SWA_HEADS, BLOCK, LANES)), NEG)
    prev_rest = jnp.where((c > r)[None], take(d_prev, hs), NEG)
    prev_first = jnp.full((N_SWA_HEADS, BLOCK, BLOCK), NEG, F32)
    own = jnp.where((d_own >= 0)[None], take(d_own, hs), NEG)
    bt = jnp.stack([jnp.concatenate([prev_first, own, meta_first[..., :N_META]], axis=-1),
                    jnp.concatenate([prev_rest, own, meta_rest[..., :N_META]], axis=-1)], axis=0)
    return dblk.astype(F32), bm0.astype(F32), bt.astype(F32)


def _proj_kernel(x_ref, g1_ref, w_ref, bd_ref, gain_ref, nmask_ref, o_ref):
    x = x_ref[...]
    a = x * lax.rsqrt(jnp.mean(x * x, axis=-1, keepdims=True) + EPS) * g1_ref[...]
    p = jnp.dot(a.astype(BF16), w_ref[...], preferred_element_type=F32)
    bd = bd_ref[...]
    for j in range(C_END // MXU_DIM):
        sl = slice(j * MXU_DIM, (j + 1) * MXU_DIM)
        pj = p[:, sl]
        if j in NORM_GROUPS:
            ms = jnp.dot((pj * pj).astype(BF16), bd, preferred_element_type=F32)
            pj = jnp.where(nmask_ref[:, sl] != 0.0, pj * lax.rsqrt(ms + EPS) * gain_ref[:, sl], pj)
        o_ref[:, sl] = pj.astype(BF16)


def _proj(x2, g1, w, bd, gain, nmask, tm):
    n = x2.shape[0]
    return pl.pallas_call(
        _proj_kernel,
        out_shape=jax.ShapeDtypeStruct((n, C_END), BF16),
        grid=(n // tm,),
        in_specs=[
            pl.BlockSpec((tm, x2.shape[1]), lambda i: (i, 0)),
            pl.BlockSpec(g1.shape, lambda i: (0, 0)),
            pl.BlockSpec(w.shape, lambda i: (0, 0)),
            pl.BlockSpec(bd.shape, lambda i: (0, 0)),
            pl.BlockSpec(gain.shape, lambda i: (0, 0)),
            pl.BlockSpec(nmask.shape, lambda i: (0, 0)),
        ],
        out_specs=pl.BlockSpec((tm, C_END), lambda i: (i, 0)),
        compiler_params=_cparams(("parallel",)),
        name="proj",
    )(x2, g1, w, bd, gain, nmask)


def _diff_kernel(qi_tab, t_tab, q_ref, k_ref, v_ref, km_ref, vm_ref, d_ref, bm0_ref, lamv_ref, gain_ref, o_ref,
                 bias_ref, mb_ref, qs_ref, kt_ref, vt_ref, s_buf, p_buf, a_buf, m_ref, acc_ref, *,
                 lambda_init, n_steps, n_far, n_near):
    tq = TQ
    nq = q_ref.shape[0] // tq
    nb = tq // BLOCK
    BIAS_LEFT, BIAS_DIAG, BIAS_NONE = 0, 1, 2

    d0 = d_ref[0, 0] * LOG2E
    d1 = d_ref[0, 1] * LOG2E
    zeros = jnp.zeros((BLOCK, BLOCK), F32)
    for a in range(nb):
        for b in range(nb):
            rs, cs = slice(a * BLOCK, (a + 1) * BLOCK), slice(b * BLOCK, (b + 1) * BLOCK)
            if a == b:
                blk = d0
            elif b == a + 1:
                blk = d1
            elif b > a:
                blk = zeros
            else:
                blk = jnp.full((BLOCK, BLOCK), NEG, F32)
            bias_ref[BIAS_DIAG, rs, cs] = blk
            bias_ref[BIAS_LEFT, rs, cs] = d1 if (b == 0 and a == nb - 1) else zeros
    bias_ref[BIAS_NONE] = jnp.zeros((tq, tq), F32)
    mb_ref[0] = jnp.zeros((N_META, tq), F32)
    mb_ref[1] = bm0_ref[0, :N_META, :] * LOG2E

    lane = lax.broadcasted_iota(jnp.int32, (tq, LANES), 1)
    for i in range(nq):
        rows = slice(i * tq, (i + 1) * tq)
        q = q_ref[rows, :].astype(F32)
        qs_ref[i] = jnp.transpose(jnp.concatenate([jnp.where(lane < HEAD_DIM, q, 0.0),
                                                   jnp.where(lane >= HEAD_DIM, q, 0.0)], axis=0)).astype(BF16)
        vt_ref[i, :LANES, :] = jnp.transpose(v_ref[rows, :].astype(F32)).astype(BF16)
        kt_ref[i] = k_ref[rows, :]
    vt_ref[nq, :LANES, :] = jnp.transpose(vm_ref[...].astype(F32)).astype(BF16)
    kt_ref[nq] = km_ref[...]
    vt_ref[:, LANES:, :] = jnp.ones((nq + 1, ONES_ROWS, tq), BF16)
    acc_ref[...] = jnp.zeros(acc_ref.shape, F32)
    m_ref[...] = jnp.full(m_ref.shape, NEG, F32)
    lv = lamv_ref[...]
    lam = (jnp.exp(jnp.sum(lv[0:1] * lv[1:2], axis=-1, keepdims=True))
           - jnp.exp(jnp.sum(lv[2:3] * lv[3:4], axis=-1, keepdims=True)) + lambda_init)

    FAR, NEAR, META = 0, 1, 2

    def stage_a(n, slot, kind):
        qi, t = qi_tab[n], t_tab[n]
        if kind == META:
            s = jnp.dot(kt_ref[nq, :N_META, :], qs_ref[qi], preferred_element_type=F32)
            s_buf[slot, :N_META] = s + jnp.tile(mb_ref[jnp.where(qi == 0, 1, 0)], (1, 2))
            return
        s = jnp.dot(kt_ref[t - 1], qs_ref[qi], preferred_element_type=F32)
        if kind == NEAR:
            which = jnp.where(t == qi + 1, BIAS_DIAG, jnp.where(t == qi, BIAS_LEFT, BIAS_NONE))
            s = s + jnp.tile(bias_ref[which], (1, 2))
        s_buf[slot] = s

    def stage_b(n, slot, kind):
        qi = qi_tab[n]
        rows = slice(0, N_META if kind == META else tq)
        s = s_buf[slot, rows]
        m_prev = m_ref[qi]
        m_new = jnp.maximum(m_prev, jnp.max(s, axis=0, keepdims=True))
        a_buf[slot] = jnp.exp2(m_prev - m_new)
        p_buf[slot, rows] = jnp.exp2(s - m_new[0:1]).astype(BF16)
        m_ref[qi] = m_new

    def stage_c(n, slot, kind):
        qi, t = qi_tab[n], t_tab[n]
        if kind == META:
            pv = jnp.dot(vt_ref[nq, :, :N_META], p_buf[slot, :N_META], preferred_element_type=F32)
        else:
            pv = jnp.dot(vt_ref[t - 1], p_buf[slot], preferred_element_type=F32)
        acc_ref[qi] = a_buf[slot][0:1] * acc_ref[qi] + pv

    LEAD, SLOTS, UNROLL = 2, 3, 12
    assert UNROLL % SLOTS == 0

    def pipeline(base, count, kind, biased_from=None):
        def kind_a(j):
            return kind if biased_from is None or j < biased_from else NEAR

        if count <= 2 * LEAD:
            for j in range(count):
                stage_a(base + j, 0, kind_a(j))
                stage_b(base + j, 0, kind)
                stage_c(base + j, 0, kind)
            return
        for j in range(2 * LEAD):
            stage_a(base + j, j % SLOTS, kind_a(j))
            if j >= LEAD:
                stage_b(base + j - LEAD, (j - LEAD) % SLOTS, kind)

        def steps(n, first, count):
            for j in range(count):
                stage_a(base + n + j + 2 * LEAD, (j + 2 * LEAD) % SLOTS, kind_a(first + j + 2 * LEAD))
                stage_b(base + n + j + LEAD, (j + LEAD) % SLOTS, kind)
                stage_c(base + n + j, j % SLOTS, kind)

        n_steady = count - 2 * LEAD
        n_blocks = n_steady // UNROLL
        switch = n_blocks if biased_from is None else (biased_from - 2 * LEAD) // UNROLL
        assert biased_from is None or (biased_from - 2 * LEAD) % UNROLL == 0
        for lo, hi in ((0, min(switch, n_blocks)), (min(switch, n_blocks), n_blocks)):
            lax.fori_loop(lo, hi, lambda k, carry, lo=lo: (steps(UNROLL * k, UNROLL * lo, UNROLL), carry)[1], 0)
        steps(UNROLL * n_blocks, UNROLL * n_blocks, n_steady - UNROLL * n_blocks)
        for j in range(n_steady, count):
            if j + LEAD < count:
                stage_b(base + j + LEAD, (j + LEAD) % SLOTS, kind)
            stage_c(base + j, j % SLOTS, kind)

    if n_far >= 2 * LEAD:
        pipeline(0, n_far + n_near, FAR, 2 * LEAD + UNROLL * ((n_far - 2 * LEAD) // UNROLL))
    else:
        pipeline(0, n_far + n_near, NEAR)
    pipeline(n_far + n_near, n_steps - n_far - n_near, META)

    for i in range(nq):
        acc = acc_ref[i]
        o = acc[:LANES] * (1.0 / acc[LANES:LANES + 1])
        d = o[:, :tq] - lam * o[:, tq:]
        y = d * lax.rsqrt(jnp.mean(d * d, axis=0, keepdims=True) + EPS) * jnp.tile(gain_ref[...], (1, tq // LANES))
        o_ref[i * tq:(i + 1) * tq, :] = jnp.transpose(y * (1.0 - lambda_init)).astype(BF16)


def _diff_attention(qkv, km, vm, dblk, bm0, lamv, gain, batch, seq, lambda_init):
    nq = seq // TQ
    far = [(qi, t) for qi in range(nq) for t in range(1, qi)]
    near = [(qi, t) for qi in range(nq) for t in (qi, qi + 1) if t >= 1]
    meta = [(qi, 0) for qi in range(nq)]
    steps = far + near + meta
    qi_tab = jnp.asarray([s[0] for s in steps], jnp.int32)
    t_tab = jnp.asarray([s[1] for s in steps], jnp.int32)
    kern = functools.partial(_diff_kernel, lambda_init=lambda_init, n_steps=len(steps), n_far=len(far),
                             n_near=len(near))
    return pl.pallas_call(
        kern,
        out_shape=jax.ShapeDtypeStruct((batch * seq, N_DIFF_HEADS * LANES), BF16),
        grid_spec=pltpu.PrefetchScalarGridSpec(
            num_scalar_prefetch=2,
            grid=(batch, N_DIFF_HEADS),
            in_specs=[
                pl.BlockSpec((seq, LANES), lambda b, h, *_: (b, C_DQ // LANES + h)),
                pl.BlockSpec((seq, LANES), lambda b, h, *_: (b, C_DK // LANES + h)),
                pl.BlockSpec((seq, LANES), lambda b, h, *_: (b, C_DV // LANES + h)),
                pl.BlockSpec((TQ, LANES), lambda b, h, *_: (0, h)),
                pl.BlockSpec((TQ, LANES), lambda b, h, *_: (0, h)),
                pl.BlockSpec((1, 2, BLOCK, BLOCK), lambda b, h, *_: (h, 0, 0, 0)),
                pl.BlockSpec((1, LANES, TQ), lambda b, h, *_: (h, 0, 0)),
                pl.BlockSpec(lamv.shape, lambda b, h, *_: (0, 0)),
                pl.BlockSpec((LANES, LANES), lambda b, h, *_: (0, 0)),
            ],
            out_specs=pl.BlockSpec((seq, LANES), lambda b, h, *_: (b, h)),
            scratch_shapes=[
                pltpu.VMEM((3, TQ, TQ), F32),
                pltpu.VMEM((2, N_META, TQ), F32),
                pltpu.VMEM((nq, LANES, 2 * TQ), BF16),
                pltpu.VMEM((nq + 1, TQ, LANES), BF16),
                pltpu.VMEM((nq + 1, LANES + ONES_ROWS, TQ), BF16),
                pltpu.VMEM((3, TQ, 2 * TQ), F32),
                pltpu.VMEM((3, TQ, 2 * TQ), BF16),
                pltpu.VMEM((3, 8, 2 * TQ), F32),
                pltpu.VMEM((nq, 8, 2 * TQ), F32),
                pltpu.VMEM((nq, LANES + ONES_ROWS, 2 * TQ), F32),
            ],
        ),
        compiler_params=_cparams(("parallel", "parallel")),
        name="diff_attention",
    )(qi_tab, t_tab, qkv, qkv, qkv, km, vm, jnp.swapaxes(dblk, -1, -2), jnp.swapaxes(bm0, -1, -2), lamv,
      jnp.broadcast_to(gain.reshape(LANES, 1), (LANES, LANES)))


def _swa_kernel(sink_ref, q_ref, k_ref, v_ref, km_ref, vm_ref, bt_ref, o_ref, kd_ref, vt_ref,
                s_scr, p_scr, inv_scr):
    nkb = k_ref.shape[0] // BLOCK
    lane = lax.broadcasted_iota(jnp.int32, (BLOCK, LANES), 1)
    pairs = [(g, u) for g in range(N_SWA_KV) for u in range(2)]

    def both_halves(k):
        k0, k1 = k[:, :HEAD_DIM], k[:, HEAD_DIM:]
        return jnp.concatenate([k0, k0, k1, k1], axis=1)

    def prepare(j, carry):
        rows = pl.ds(pl.multiple_of(j * BLOCK, BLOCK), BLOCK)
        kd_ref[j] = both_halves(k_ref[rows, :])
        vt_ref[j] = jnp.transpose(v_ref[rows, :].astype(F32)).astype(BF16)
        return carry
    lax.fori_loop(0, nkb, prepare, 0)
    kd_ref[nkb] = both_halves(km_ref[...])
    vt_ref[nkb] = jnp.transpose(vm_ref[...].astype(F32)).astype(BF16)

    def scores(n, slot):
        first = jnp.where(n == 0, 0, 1)
        prev = jnp.maximum(n - 1, 0)
        r_q = pl.multiple_of(n * BLOCK, BLOCK)
        for c, (g, u) in enumerate(pairs):
            ks = slice(g * LANES, (g + 1) * LANES)
            kcat = jnp.concatenate([kd_ref[prev, :, ks], kd_ref[n, :, ks], kd_ref[nkb, :N_META, ks]], axis=0)
            h0 = 4 * g + 2 * u
            qp = q_ref[pl.ds(r_q, BLOCK), (2 * g + u) * LANES:(2 * g + u + 1) * LANES].astype(F32)
            qs = jnp.transpose(jnp.concatenate([jnp.where(lane < HEAD_DIM, qp, 0.0),
                                                jnp.where(lane >= HEAD_DIM, qp, 0.0)], axis=0)).astype(BF16)
            s = jnp.dot(kcat, qs, preferred_element_type=F32)
            s_scr[slot, c] = s + jnp.concatenate([bt_ref[first, h0], bt_ref[first, h0 + 1]], axis=1)

    def exponentials(slot):
        for c, (g, u) in enumerate(pairs):
            h0 = 4 * g + 2 * u
            s = s_scr[slot, c]
            sink = jnp.concatenate([sink_ref[h0:h0 + 1, :], sink_ref[h0 + 1:h0 + 2, :]], axis=1)
            m = jnp.maximum(jnp.max(s, axis=0, keepdims=True), sink)
            p = jnp.exp(s - m)
            p_scr[slot, c] = p.astype(BF16)
            inv_scr[slot, c] = jnp.broadcast_to(1.0 / (jnp.sum(p, axis=0, keepdims=True) + jnp.exp(sink - m)),
                                                inv_scr.shape[2:])

    def values(n, slot):
        prev = jnp.maximum(n - 1, 0)
        r_q = pl.multiple_of(n * BLOCK, BLOCK)
        for c, (g, u) in enumerate(pairs):
            vs = slice(g * HEAD_DIM, (g + 1) * HEAD_DIM)
            vcat = jnp.concatenate([vt_ref[prev, vs, :], vt_ref[n, vs, :]], axis=1)
            o = (jnp.dot(vcat, p_scr[slot, c, :2 * BLOCK], preferred_element_type=F32)
                 + jnp.dot(vt_ref[nkb, vs, :N_META], p_scr[slot, c, 2 * BLOCK:], preferred_element_type=F32)
                 ) * inv_scr[slot, c][0:1]
            ot = jnp.transpose(o)
            o_ref[pl.ds(r_q, BLOCK), (2 * g + u) * LANES:(2 * g + u + 1) * LANES] = (
                jnp.concatenate([ot[:BLOCK], ot[BLOCK:]], axis=1).astype(BF16))

    scores(0, 0)
    scores(1, 1)
    exponentials(0)

    def blocks(n, count):
        for j in range(count):
            scores(n + j + 2, j % 2)
            exponentials((j + 1) % 2)
            values(n + j, j % 2)

    unroll = 4
    n_steady = nkb - 2
    lax.fori_loop(0, n_steady // unroll, lambda k, carry: (blocks(unroll * k, unroll), carry)[1], 0)
    blocks(n_steady // unroll * unroll, n_steady % unroll)
    exponentials(1)
    values(nkb - 2, 0)
    values(nkb - 1, 1)


def _swa_attention(sinks, qkv, km, vm, bt, batch, seq):
    nkb = seq // BLOCK
    swa_q = N_SWA_HEADS * HEAD_DIM
    assert nkb % 2 == 0
    sinkv =jnp.broadcast_to(sinks.reshape(N_SWA_HEADS, 1), (N_SWA_HEADS, LANES))
    return pl.pallas_call(
        _swa_kernel,
        out_shape=jax.ShapeDtypeStruct((batch * seq, N_SWA_HEADS * HEAD_DIM), BF16),
        grid=(batch,),
        in_specs=[
            pl.BlockSpec(sinkv.shape, lambda b: (0, 0)),
            pl.BlockSpec((seq, swa_q), lambda b: (b, C_SQ // swa_q)),
            pl.BlockSpec((seq, LANES), lambda b: (b, C_SK // LANES)),
            pl.BlockSpec((seq, LANES), lambda b: (b, C_SV // LANES)),
            pl.BlockSpec(km.shape, lambda b: (0, 0)),
            pl.BlockSpec(vm.shape, lambda b: (0, 0)),
            pl.BlockSpec(bt.shape, lambda b: (0, 0, 0, 0)),
        ],
        out_specs=pl.BlockSpec((seq, swa_q), lambda b: (b, 0)),
        scratch_shapes=[pltpu.VMEM((nkb + 1, BLOCK, 2 * LANES), BF16),
                        pltpu.VMEM((nkb + 1, LANES, BLOCK), BF16),
                        pltpu.VMEM((2, 4, 2 * BLOCK + N_META, 2 * BLOCK), F32),
                        pltpu.VMEM((2, 4, 2 * BLOCK + N_META, 2 * BLOCK), BF16),
                        pltpu.VMEM((2, 4, 8, 2 * BLOCK), F32)],
        compiler_params=_cparams(("parallel",)),
        name="swa_attention",
    )(sinkv, qkv, qkv, qkv, km, vm, bt)


def _outproj_kernel(x_ref, md_ref, ms_ref, wo_ref, g2_ref, wr_ref, br_ref,
                    h_ref, hb_ref, rt_ref, ti_ref, cnt_ref, c_ref, lg_ref):
    i = pl.program_id(0)

    @pl.when(i == 0)
    def _init():
        c_ref[...] = jnp.zeros(c_ref.shape, F32)
        lg_ref[...] = jnp.zeros(lg_ref.shape, F32)

    lg_prev = lg_ref[...]
    half = md_ref.shape[1]
    h = (x_ref[...]
         + jnp.dot(md_ref[...], wo_ref[:half, :], preferred_element_type=F32)
         + jnp.dot(ms_ref[...], wo_ref[half:, :], preferred_element_type=F32))
    h_ref[...] = h
    hn = h * lax.rsqrt(jnp.mean(h * h, axis=-1, keepdims=True) + EPS) * g2_ref[...]
    hb = hn.astype(BF16)
    hb_ref[...] = hb
    lg_ref[...] = jnp.dot(hb, wr_ref[...], preferred_element_type=F32) + br_ref[...]
    _route_tile(lg_prev, jnp.where(i > 0, 1.0, 0.0), rt_ref, ti_ref, c_ref)

    @pl.when(i == pl.num_programs(0) - 1)
    def _fin():
        cnt_ref[...] = c_ref[...]


def _route_tile(lg, live, rt_ref, ti_ref, c_ref):
    tm = lg.shape[0]
    lane_i = lax.broadcasted_iota(jnp.int32, lg.shape, 1)
    lane = lane_i.astype(F32)
    big = float(4 * LANES)
    is_g = (lane_i >= N_EXPERTS) & (lane_i < N_EXPERTS + N_GROUPS)
    glm = jnp.where(is_g, lg, -jnp.inf)
    gmax = jnp.max(glm, axis=1, keepdims=True)
    gidx = jnp.min(jnp.where(glm == gmax, lane, big), axis=1, keepdims=True) - N_EXPERTS
    gsum = jnp.sum(jnp.where(is_g, jnp.exp(lg - gmax), 0.0), axis=1, keepdims=True)
    g_w = 1.0 / gsum
    lane_grp = (lane_i >> 3).astype(F32)
    in_grp = (lane_i < N_EXPERTS) & (lane_grp == gidx)
    el = jnp.where(in_grp, lg, -jnp.inf)
    t1 = jnp.max(el, axis=1, keepdims=True)
    j1 = jnp.min(jnp.where(el == t1, lane, big), axis=1, keepdims=True)
    el2 = jnp.where(lane == j1, -jnp.inf, el)
    t2 = jnp.max(el2, axis=1, keepdims=True)
    j2 = jnp.min(jnp.where(el2 == t2, lane, big), axis=1, keepdims=True)
    e2 = jnp.exp(t2 - t1)
    den = 1.0 + e2
    gate1 = g_w / den
    gate2 = g_w * e2 / den

    o1 = lane == j1
    o2 = lane == j2
    onehot = jnp.where(o1 | o2, 1.0, 0.0).astype(BF16)
    rr = lax.broadcasted_iota(jnp.int32, (tm, tm), 0)
    cc = lax.broadcasted_iota(jnp.int32, (tm, tm), 1)
    lower = jnp.where(rr > cc, 1.0, 0.0).astype(BF16)
    pfx = jnp.dot(lower, onehot, preferred_element_type=F32)
    cnt_tile = jnp.sum(onehot.astype(F32), axis=0, keepdims=True)
    groups = jnp.floor((cnt_tile + (ROW_ALIGN - 1)) * (1.0 / ROW_ALIGN))
    er = lax.broadcasted_iota(jnp.int32, (LANES, LANES), 0)
    ec = lax.broadcasted_iota(jnp.int32, (LANES, LANES), 1)
    before = jnp.where(er < ec, 1.0, 0.0).astype(BF16)
    cbase = ROW_ALIGN * jnp.dot(jnp.broadcast_to(groups, (8, LANES)).astype(BF16), before,
                                preferred_element_type=F32)[0:1]
    at = pfx + cbase
    pos1 = jnp.sum(jnp.where(o1, at, 0.0), axis=1, keepdims=True)
    pos2 = jnp.sum(jnp.where(o2, at, 0.0), axis=1, keepdims=True)
    rt_ref[...] = jnp.where(lane_i == 0, gate1,
                            jnp.where(lane_i == 1, gate2,
                                      jnp.where(lane_i == 2, pos1,
                                                jnp.where(lane_i == 3, pos2, 0.0))))
    c_old = c_ref[...]
    c_ref[...] = c_old + groups * (ROW_ALIGN * live)
    row8 = lax.broadcasted_iota(jnp.int32, (8, LANES), 0)
    ti_ref[...] = jnp.where(row8 == 0, cnt_tile, jnp.where(row8 == 1, c_old, 0.0))


def _outproj(x2, mixd, mixs, wo, g2, wr, br):
    n, d = x2.shape
    nt = n // TM

    def proj_tile(i):
        return (jnp.minimum(i, nt - 1), 0)

    def route_tile(i):
        return (jnp.maximum(i - 1, 0), 0)

    return pl.pallas_call(
        _outproj_kernel,
        out_shape=(jax.ShapeDtypeStruct((n, d), F32),
                   jax.ShapeDtypeStruct((n, d), BF16),
                   jax.ShapeDtypeStruct((n, LANES), F32),
                   jax.ShapeDtypeStruct((nt * 8, LANES), F32),
                   jax.ShapeDtypeStruct((8, LANES), F32)),
        grid=(nt + 1,),
        in_specs=[
            pl.BlockSpec((TM, d), proj_tile),
            pl.BlockSpec((TM, mixd.shape[1]), proj_tile),
            pl.BlockSpec((TM, mixs.shape[1]), proj_tile),
            pl.BlockSpec(wo.shape, lambda i: (0, 0)),
            pl.BlockSpec(g2.shape, lambda i: (0, 0)),
            pl.BlockSpec(wr.shape, lambda i: (0, 0)),
            pl.BlockSpec(br.shape, lambda i: (0, 0)),
        ],
        out_specs=(pl.BlockSpec((TM, d), proj_tile),
                   pl.BlockSpec((TM, d), proj_tile),
                   pl.BlockSpec((TM, LANES), route_tile),
                   pl.BlockSpec((8, LANES), route_tile),
                   pl.BlockSpec((8, LANES), lambda i: (0, 0))),
        scratch_shapes=[pltpu.VMEM((8, LANES), F32), pltpu.VMEM((TM, LANES), F32)],
        compiler_params=_cparams(("arbitrary",)),
        name="outproj_router",
    )(x2, mixd, mixs, wo, g2, wr, br)


def _for_each_chunk(runs_ref, fn):
    big = CHUNK_ROWS[0]

    def per_expert(e, sorted_row, priority):
        start = runs_ref[0, 0, e]
        groups = runs_ref[0, 0, N_EXPERTS + e]
        whole = groups // (big // ROW_ALIGN)

        def per_chunk(c, carry):
            fn(pl.multiple_of(start + c * big, ROW_ALIGN), pl.multiple_of(sorted_row + c * big, ROW_ALIGN),
               big, priority)
            return carry
        lax.fori_loop(0, whole, per_chunk, 0)

        done = whole * big
        for rows in CHUNK_ROWS[1:]:
            has = (groups // (rows // ROW_ALIGN)) % 2

            @pl.when(has == 1)
            def _(done=done, rows=rows):
                fn(pl.multiple_of(start + done, ROW_ALIGN), pl.multiple_of(sorted_row + done, ROW_ALIGN),
                   rows, priority)
            done = done + has * rows
        return sorted_row + groups * ROW_ALIGN

    def expert_pair(e2, sorted_row):
        return per_expert(2 * e2 + 1, per_expert(2 * e2, sorted_row, 0), 1)
    lax.fori_loop(0, N_EXPERTS // 2, expert_pair, 0)


def _wait_chunks(runs_ref, make_copy):
    for k, rows in enumerate(CHUNK_ROWS):
        def body(c, carry, rows=rows):
            make_copy(rows).wait()
            return carry
        lax.fori_loop(0, runs_ref[0, 0, 2 * N_EXPERTS + k], body, 0)


def _dispatch_kernel(zf_ref, cur_ref, prv_ref, hb_ref, rt_ref, xs_ref, sbuf, zbuf, sem, zsem):
    i = pl.program_id(0)
    nt = pl.num_programs(0)
    slot = i % 2
    tm, d = hb_ref.shape

    def for_zero_blocks(kind, fn):
        def body(b, carry):
            @pl.when(zf_ref[b] == kind)
            def _():
                fn(pltpu.make_async_copy(zbuf, xs_ref.at[pl.ds(pl.multiple_of(b * EB, EB), EB)],
                                         zsem.at[kind - 1]))
            return carry
        lax.fori_loop(0, zf_ref.shape[0], body, 0)

    @pl.when(i == 0)
    def _():
        zbuf[...] = jnp.zeros(zbuf.shape, zbuf.dtype)
        for_zero_blocks(1, lambda c: c.start())
        for_zero_blocks(2, lambda c: c.start())
        for_zero_blocks(1, lambda c: c.wait())

    pos_t = jnp.transpose(rt_ref[...])
    srow = lax.broadcasted_iota(jnp.int32, (SROWS, tm), 0).astype(F32)
    sel = jnp.where(srow == pos_t[2:3, :], 1.0, jnp.where(srow == pos_t[3:4, :], 1.0, 0.0)).astype(BF16)
    srt = jnp.dot(sel, hb_ref[...], preferred_element_type=F32)
    bits = pltpu.bitcast(srt, jnp.uint32)
    sbuf[slot] = (bits[:, d // 2:] & jnp.uint32(0xFFFF0000)) | (bits[:, :d // 2] >> 16)

    def chunk_copy(run_row, sorted_row, rows, sl):
        return pltpu.make_async_copy(sbuf.at[sl, pl.ds(sorted_row, rows)], xs_ref.at[pl.ds(run_row, rows)],
                                     sem.at[sl])

    _for_each_chunk(cur_ref, lambda run_row, sorted_row, rows, priority:
                    chunk_copy(run_row, sorted_row, rows, slot).start(priority=priority))

    @pl.when(i > 0)
    def _():
        _wait_chunks(prv_ref, lambda rows: chunk_copy(0, 0, rows, 1 - slot))

    @pl.when(i == nt - 1)
    def _():
        _wait_chunks(cur_ref, lambda rows: chunk_copy(0, 0, rows, slot))
        for_zero_blocks(2, lambda c: c.wait())


def _dispatch(zero_blocks, runs, hb, rt, n_rows):
    n, d = hb.shape
    return pl.pallas_call(
        _dispatch_kernel,
        out_shape=jax.ShapeDtypeStruct((n_rows, d // 2), jnp.uint32),
        grid_spec=pltpu.PrefetchScalarGridSpec(
            num_scalar_prefetch=1,
            grid=(n // TM,),
            in_specs=[
                pl.BlockSpec((1, 1, LANES), lambda i, zf: (i, 0, 0), memory_space=pltpu.SMEM),
                pl.BlockSpec((1, 1, LANES), lambda i, zf: (jnp.maximum(i - 1, 0), 0, 0), memory_space=pltpu.SMEM),
                pl.BlockSpec((TM, d), lambda i, zf: (i, 0)),
                pl.BlockSpec((TM, LANES), lambda i, zf: (i, 0)),
            ],
            out_specs=pl.BlockSpec(memory_space=pl.ANY),
            scratch_shapes=[pltpu.VMEM((2, SROWS, d // 2), jnp.uint32), pltpu.VMEM((EB, d // 2), jnp.uint32),
                            pltpu.SemaphoreType.DMA((2,)), pltpu.SemaphoreType.DMA((2,))],
        ),
        compiler_params=_cparams(("arbitrary",)),
        name="dispatch",
    )(zero_blocks, runs, runs, hb, rt)


def _experts_kernel(be_ref, na_ref, nxt_ref, xs_ref, wg_hbm, wu_hbm, wd_hbm, ys_ref,
                    wgf, wuf, wdf, wgb, wub, wdb, sem):
    b = pl.program_id(0)

    def weight_copies(e):
        return (pltpu.make_async_copy(wg_hbm.at[e], wgf, sem.at[0]),
                pltpu.make_async_copy(wu_hbm.at[e], wuf, sem.at[1]),
                pltpu.make_async_copy(wd_hbm.at[e], wdf, sem.at[2]))

    @pl.when(b == 0)
    def _():
        for c in weight_copies(be_ref[0]):
            c.start()

    @pl.when(b < na_ref[0])
    def _():
        e = be_ref[b]
        changed = jnp.logical_or(b == 0, be_ref[jnp.maximum(b - 1, 0)] != e)

        @pl.when(changed)
        def _load():
            for c in weight_copies(e):
                c.wait()
            wgb[...] = wgf[...].astype(BF16)
            wub[...] = wuf[...].astype(BF16)
            wdb[...] = wdf[...].astype(BF16)
            nxt = nxt_ref[e]

            @pl.when(nxt >= 0)
            def _():
                for c in weight_copies(nxt):
                    c.start()

        w = xs_ref[...]
        x_lo = pltpu.bitcast(w << 16, F32).astype(BF16)
        x_hi = pltpu.bitcast(w & jnp.uint32(0xFFFF0000), F32).astype(BF16)
        dh = w.shape[1]
        g = (jnp.dot(x_lo, wgb[:dh, :], preferred_element_type=F32)
             + jnp.dot(x_hi, wgb[dh:, :], preferred_element_type=F32))
        u = (jnp.dot(x_lo, wub[:dh, :], preferred_element_type=F32)
             + jnp.dot(x_hi, wub[dh:, :], preferred_element_type=F32))
        hdn = g * (1.0 / (1.0 + jnp.exp(-g))) * u
        y = jnp.dot(hdn.astype(BF16), wdb[...], preferred_element_type=F32)
        bits = pltpu.bitcast(y.astype(BF16).astype(F32), jnp.uint32)
        ys_ref[...] = (bits[:, dh:] & jnp.uint32(0xFFFF0000)) | (bits[:, :dh] >> 16)

    @pl.when(b >= na_ref[0])
    def _():
        ys_ref[...] = jnp.zeros(ys_ref.shape, ys_ref.dtype)


def _experts(blk_e, n_act, nxt_e, xs, w_gate, w_up, w_down):
    p, dh = xs.shape
    d = 2 * dh
    de = w_gate.shape[2]

    def row_map(b, be, na, nx):
        return (jnp.minimum(b, na[0] - 1), 0)

    return pl.pallas_call(
        _experts_kernel,
        out_shape=jax.ShapeDtypeStruct((p, dh), jnp.uint32),
        grid_spec=pltpu.PrefetchScalarGridSpec(
            num_scalar_prefetch=3,
            grid=(p // EB,),
            in_specs=[
                pl.BlockSpec((EB, dh), row_map),
                pl.BlockSpec(memory_space=pl.ANY),
                pl.BlockSpec(memory_space=pl.ANY),
                pl.BlockSpec(memory_space=pl.ANY),
            ],
            out_specs=pl.BlockSpec((EB, dh), lambda b, be, na, nx: (b, 0)),
            scratch_shapes=[pltpu.VMEM((d, de), F32), pltpu.VMEM((d, de), F32), pltpu.VMEM((de, d), F32),
                            pltpu.VMEM((d, de), BF16), pltpu.VMEM((d, de), BF16), pltpu.VMEM((de, d), BF16),
                            pltpu.SemaphoreType.DMA((3,))],
        ),
        compiler_params=_cparams(("arbitrary",)),
        name="experts",
    )(blk_e, n_act, nxt_e, xs, w_gate, w_up, w_down)


def _combine_kernel(cur_ref, nxt_ref, ys_ref, h_ref, rt_ref, o_ref, ybuf, sem):
    i = pl.program_id(0)
    nt = pl.num_programs(0)
    slot = i % 2
    tm = h_ref.shape[0]

    def chunk_copy(run_row, sorted_row, rows, sl):
        return pltpu.make_async_copy(ys_ref.at[pl.ds(run_row, rows)], ybuf.at[sl, pl.ds(sorted_row, rows)],
                                     sem.at[sl])

    @pl.when(i == 0)
    def _():
        ybuf[...] = jnp.zeros(ybuf.shape, ybuf.dtype)
        _for_each_chunk(cur_ref, lambda run_row, sorted_row, rows, priority:
                        chunk_copy(run_row, sorted_row, rows, 0).start(priority=priority))

    @pl.when(i + 1 < nt)
    def _():
        _for_each_chunk(nxt_ref, lambda run_row, sorted_row, rows, priority:
                        chunk_copy(run_row, sorted_row, rows, 1 - slot).start(priority=priority))

    _wait_chunks(cur_ref, lambda rows: chunk_copy(0, 0, rows, slot))

    rt = rt_ref[...]
    w = ybuf[slot]
    dh = w.shape[1]
    y_lo = pltpu.bitcast(w << 16, F32).astype(BF16)
    y_hi = pltpu.bitcast(w & jnp.uint32(0xFFFF0000), F32).astype(BF16)
    col = lax.broadcasted_iota(jnp.int32, (tm, SROWS), 1).astype(F32)
    wsel = jnp.where(col == rt[:, 2:3], rt[:, 0:1], jnp.where(col == rt[:, 3:4], rt[:, 1:2], 0.0)).astype(BF16)
    for half, yb in ((slice(0, dh), y_lo), (slice(dh, 2 * dh), y_hi)):
        o_ref[:, half] = h_ref[:, half] + jnp.dot(wsel, yb, preferred_element_type=F32)


def _combine(runs, ys, h1, rt):
    n, d = h1.shape
    nt = n // TM
    return pl.pallas_call(
        _combine_kernel,
        out_shape=jax.ShapeDtypeStruct((n, d), F32),
        grid=(nt,),
        in_specs=[
            pl.BlockSpec((1, 1, LANES), lambda i: (i, 0, 0), memory_space=pltpu.SMEM),
            pl.BlockSpec((1, 1, LANES), lambda i: (jnp.minimum(i + 1, nt - 1), 0, 0), memory_space=pltpu.SMEM),
            pl.BlockSpec(memory_space=pl.ANY),
            pl.BlockSpec((TM, d), lambda i: (i, 0)),
            pl.BlockSpec((TM, LANES), lambda i: (i, 0)),
        ],
        out_specs=pl.BlockSpec((TM, d), lambda i: (i, 0)),
        scratch_shapes=[pltpu.VMEM((2, SROWS, d // 2), jnp.uint32), pltpu.SemaphoreType.DMA((2,))],
        compiler_params=_cparams(("arbitrary",)),
        name="combine",
    )(runs, runs, ys, h1, rt)


def kernel(x, meta_tokens, rel_bias, norm1_gain, w_in, diff_q_gain, diff_k_gain, lam_q1, lam_k1, lam_q2, lam_k2, diff_subln_gain, swa_q_gain, swa_k_gain, swa_sinks, w_out, norm2_gain, w_group, b_group, w_router, b_router, w_gate, w_up, w_down):
    batch, seq, d = x.shape
    depth = w_in.shape[0]
    n = batch * seq
    assert seq % TQ == 0 and n % TM == 0 and n % TP == 0 and d == 1024
    assert meta_tokens.shape[0] == N_META
    assert depth == 1, "the meta-token rows of the residual stream are not carried across layers"

    h = x.reshape(n, d)
    dblk, bm0, bt = _bias_tables(rel_bias, TQ)
    scale = HEAD_DIM ** -0.5
    bd = jnp.asarray(np.kron(np.eye(MXU_DIM // HEAD_DIM), np.full((HEAD_DIM, HEAD_DIM), 1.0 / HEAD_DIM)), BF16)
    ones = jnp.ones((HEAD_DIM,), F32)
    lower_pad = N_EXPERTS + N_GROUPS

    for layer in range(depth):
        lambda_init = 0.8 - 0.6 * math.exp(-0.3 * layer)
        w_cat = w_in[layer].astype(BF16)
        gain = jnp.concatenate([
            jnp.tile(diff_q_gain[layer] * (scale * LOG2E), 2 * N_DIFF_HEADS),
            jnp.tile(diff_k_gain[layer], 2 * N_DIFF_HEADS),
            jnp.tile(ones, 2 * N_DIFF_HEADS),
            jnp.tile(swa_q_gain[layer] * scale, N_SWA_HEADS),
            jnp.tile(swa_k_gain[layer], N_SWA_KV),
            jnp.tile(ones, N_SWA_KV)]).reshape(1, C_END).astype(F32)
        nmask = np.zeros((1, C_END), np.float32)
        nmask[:, C_DQ:C_DV] = 1.0
        nmask[:, C_SQ:C_SV] = 1.0
        nmask = jnp.asarray(nmask)
        g1 = norm1_gain[layer].reshape(1, d).astype(F32)

        qkv = _proj(h, g1, w_cat, bd, gain, nmask, TP)
        qkv_meta = _proj(meta_tokens.astype(F32), g1, w_cat, bd, gain, nmask, N_META)
        meta_pad = jnp.pad(qkv_meta, ((0, TQ - N_META), (0, 0)))

        lamv = jnp.pad(jnp.stack([lam_q1[layer], lam_k1[layer], lam_q2[layer], lam_k2[layer]]).astype(F32),
                       ((0, 4), (0, LANES - HEAD_DIM)))
        mixd = _diff_attention(qkv, meta_pad[:, C_DK:C_DV], meta_pad[:, C_DV:C_SQ], dblk, bm0, lamv,
                               diff_subln_gain[layer].reshape(1, LANES).astype(F32), batch, seq, lambda_init)
        mixs = _swa_attention(swa_sinks[layer].astype(F32), qkv, meta_pad[:BLOCK, C_SK:C_SV],
                              meta_pad[:BLOCK, C_SV:C_END], jnp.swapaxes(bt, -1, -2), batch, seq)

        wr = jnp.pad(jnp.concatenate([w_router[layer], w_group[layer]], axis=1),
                     ((0, 0), (0, LANES - lower_pad))).astype(BF16)
        br = jnp.pad(jnp.concatenate([b_router[layer], b_group[layer]]), (0, LANES - lower_pad)).reshape(1, LANES)
        h1, hb, rt, tinfo, cnt = _outproj(h, mixd, mixs, w_out[layer].astype(BF16),
                                          norm2_gain[layer].reshape(1, d).astype(F32), wr, br.astype(F32))

        nt = n // TM
        counts = cnt[0, :N_EXPERTS].astype(jnp.int32)
        nblk_e = (counts + EB - 1) // EB
        blk_end = jnp.cumsum(nblk_e)
        pstart = ((blk_end - nblk_e) * EB).astype(jnp.int32)
        n_blocks = -(-(2 * n + nt * N_EXPERTS * (ROW_ALIGN - 1) + N_EXPERTS * (EB - 1)) // EB)
        blk_ids = jnp.arange(n_blocks)
        blk_e = jnp.minimum(jnp.sum(blk_end[None, :] <= blk_ids[:, None], axis=1), N_EXPERTS - 1).astype(jnp.int32)
        n_act = blk_end[-1:].astype(jnp.int32)
        is_last = jnp.any((blk_end[None, :] == blk_ids[:, None] + 1) & (nblk_e[None, :] > 0), axis=1)
        zero_blocks = jnp.where(blk_ids >= n_act[0], 2, jnp.where(is_last, 1, 0)).astype(jnp.int32)
        ti = tinfo.reshape(nt, 8, LANES)
        run_len = ti[:, 0, :N_EXPERTS].astype(jnp.int32)
        run_start = pstart[None, :] + ti[:, 1, :N_EXPERTS].astype(jnp.int32)
        run_groups = (run_len + ROW_ALIGN - 1) // ROW_ALIGN
        n_copies = [jnp.sum(run_groups // (CHUNK_ROWS[0] // ROW_ALIGN), axis=1, keepdims=True)]
        n_copies += [jnp.sum((run_groups // (rows // ROW_ALIGN)) % 2, axis=1, keepdims=True)
                     for rows in CHUNK_ROWS[1:]]
        runs = jnp.concatenate([run_start, run_groups] + n_copies
                               + [jnp.zeros((nt, LANES - 2 * N_EXPERTS - len(CHUNK_ROWS)), jnp.int32)],
                               axis=1).reshape(nt, 1, LANES)

        xs = _dispatch(zero_blocks, runs, hb, rt, n_blocks * EB)
        own = jnp.where(nblk_e > 0, jnp.arange(N_EXPERTS), N_EXPERTS)
        later = jnp.concatenate([lax.cummin(own[::-1])[::-1][1:], jnp.full((1,), N_EXPERTS)])
        nxt_e = jnp.where(later < N_EXPERTS, later, -1).astype(jnp.int32)
        ys = _experts(blk_e, n_act, nxt_e, xs, w_gate[layer], w_up[layer], w_down[layer])
        h = _combine(runs, ys, h1, rt)
    return h.reshape(batch, seq, d)
```

```python
import functools
import math

import numpy as np
import jax
import jax.numpy as jnp
from jax import lax
from jax.experimental import pallas as pl
from jax.experimental.pallas import tpu as pltpu

F32 = jnp.float32
BF16 = jnp.bfloat16

HEAD_DIM = 64
N_DIFF_HEADS = 4
N_SWA_HEADS = 8
N_SWA_KV = 2
BLOCK = 128
N_META = 16
N_BUCKETS = 32
MAX_DISTANCE = 128
N_GROUPS = 4
EXPERTS_PER_GROUP = 8
N_EXPERTS = N_GROUPS * EXPERTS_PER_GROUP
EPS = 1e-6
NEG = -1e30
LOG2E = math.log2(math.e)

LANES = 128
MXU_DIM = 256
V7X_VMEM_BYTES = 64 * 1024 * 1024
VMEM_LIMIT = V7X_VMEM_BYTES * 3 // 4

TP = 1024
TM = 512
TQ = 256
ONES_ROWS = 16
EB = 512
ROW_ALIGN = 8
CHUNK_ROWS = (32, 16, 8)
SROWS = -(-(2 * TM + N_EXPERTS * (ROW_ALIGN - 1)) // MXU_DIM) * MXU_DIM
SORTED_RANGE = 2048
CHUNK_SLOTS = (SROWS // CHUNK_ROWS[0],) + (N_EXPERTS,) * (len(CHUNK_ROWS) - 1)
CHUNK_OFFSETS = tuple(sum(CHUNK_SLOTS[:k]) for k in range(len(CHUNK_ROWS)))
CHUNK_COUNTS = sum(CHUNK_SLOTS)
assert SROWS <= SORTED_RANGE and CHUNK_COUNTS + len(CHUNK_ROWS) <= LANES

C_DQ, C_DK, C_DV, C_SQ, C_SK, C_SV, C_END = 0, 512, 1024, 1536, 2048, 2176, 2304
NORM_GROUPS = (0, 1, 2, 3, 6, 7, 8)


def _cparams(sem):
    return pltpu.CompilerParams(dimension_semantics=sem, vmem_limit_bytes=VMEM_LIMIT)


def _t5_bucket_np(dist):
    n = np.maximum(dist, 0)
    max_exact = N_BUCKETS // 2
    nf = np.maximum(n, 1).astype(np.float32)
    large = max_exact + (np.log(nf / np.float32(max_exact)) / np.float32(math.log(MAX_DISTANCE / max_exact))
                         * np.float32(N_BUCKETS - max_exact)).astype(np.int32)
    large = np.minimum(large, N_BUCKETS - 1)
    return np.where(n < max_exact, n, large)


def _bias_tables(rel_bias, tq):
    nd = 2 * BLOCK
    buckets = _t5_bucket_np(np.arange(nd))
    assert (buckets[MAX_DISTANCE:] == N_BUCKETS - 1).all()
    rb = rel_bias.astype(F32)
    r = np.arange(BLOCK)[:, None]
    c = np.arange(BLOCK)[None, :]
    d_own = r - c
    d_prev = BLOCK + r - c
    far = rb[N_BUCKETS - 1]

    def take(dist, heads):
        idx = jnp.asarray(buckets[np.clip(dist, 0, nd - 1)], jnp.int32)[None]
        out = jnp.zeros((heads.stop - heads.start,) + dist.shape, F32)
        for b in range(N_BUCKETS):
            out = jnp.where(idx == b, rb[b, heads].reshape((-1,) + (1,) * dist.ndim), out)
        return out

    hd = slice(0, N_DIFF_HEADS)
    far_d = far[hd][:, None, None]
    d0 = jnp.where(d_own[None] >= 0, take(d_own, hd) - far_d, NEG)
    d1 = take(d_prev, hd) - far_d
    dblk = jnp.stack([d0, d1], axis=1)
    rq = np.arange(tq)[:, None]
    cm = np.arange(LANES)[None, :]
    d_meta = (N_META + rq - cm)[:, :N_META]
    bm0 = jnp.pad(take(d_meta, hd) - far_d, ((0, 0), (0, 0), (0, LANES - N_META)), constant_values=NEG)

    hs = slice(N_DIFF_HEADS, N_DIFF_HEADS + N_SWA_HEADS)
    far_s = far[hs][:, None, None]
    d_meta_s = N_META + r - cm
    meta_first = jnp.where((cm < N_META)[None], take(d_meta_s, hs), NEG)
    meta_rest = jnp.where((cm < N_META)[None], jnp.broadcast_to(far_s, (N_SWA_HEADS, BLOCK, LANES)), NEG)
    prev_rest = jnp.where((c > r)[None], take(d_prev, hs), NEG)
    prev_first = jnp.full((N_SWA_HEADS, BLOCK, BLOCK), NEG, F32)
    own = jnp.where((d_own >= 0)[None], take(d_own, hs), NEG)
    bt = jnp.stack([jnp.concatenate([prev_first, own, meta_first[..., :N_META]], axis=-1),
                    jnp.concatenate([prev_rest, own, meta_rest[..., :N_META]], axis=-1)], axis=0)
    return dblk.astype(F32), bm0.astype(F32), bt.astype(F32)


def _proj_kernel(x_ref, g1_ref, w_ref, bd_ref, gain_ref, nmask_ref, o_ref):
    x = x_ref[...]
    a = x * lax.rsqrt(jnp.mean(x * x, axis=-1, keepdims=True) + EPS) * g1_ref[...]
    p = jnp.dot(a.astype(BF16), w_ref[...], preferred_element_type=F32)
    bd = bd_ref[...]
    for j in range(C_END // MXU_DIM):
        sl = slice(j * MXU_DIM, (j + 1) * MXU_DIM)
        pj = p[:, sl]
        if j in NORM_GROUPS:
            ms = jnp.dot((pj * pj).astype(BF16), bd, preferred_element_type=F32)
            pj = jnp.where(nmask_ref[:, sl] != 0.0, pj * lax.rsqrt(ms + EPS) * gain_ref[:, sl], pj)
        o_ref[:, sl] = pj.astype(BF16)


def _proj(x2, g1, w, bd, gain, nmask, tm):
    n = x2.shape[0]
    return pl.pallas_call(
        _proj_kernel,
        out_shape=jax.ShapeDtypeStruct((n, C_END), BF16),
        grid=(n // tm,),
        in_specs=[
            pl.BlockSpec((tm, x2.shape[1]), lambda i: (i, 0)),
            pl.BlockSpec(g1.shape, lambda i: (0, 0)),
            pl.BlockSpec(w.shape, lambda i: (0, 0)),
            pl.BlockSpec(bd.shape, lambda i: (0, 0)),
            pl.BlockSpec(gain.shape, lambda i: (0, 0)),
            pl.BlockSpec(nmask.shape, lambda i: (0, 0)),
        ],
        out_specs=pl.BlockSpec((tm, C_END), lambda i: (i, 0)),
        compiler_params=_cparams(("parallel",)),
        name="proj",
    )(x2, g1, w, bd, gain, nmask)


def _diff_kernel(qi_tab, t_tab, q_ref, k_ref, v_ref, km_ref, vm_ref, d_ref, bm0_ref, lamv_ref, gain_ref, o_ref,
                 bias_ref, mb_ref, qs_ref, kt_ref, vt_ref, s_buf, p_buf, a_buf, m_ref, acc_ref, *,
                 lambda_init, n_steps, n_far, n_near):
    tq = TQ
    nq = q_ref.shape[0] // tq
    nb = tq // BLOCK
    BIAS_LEFT, BIAS_DIAG, BIAS_NONE = 0, 1, 2

    d0 = d_ref[0, 0] * LOG2E
    d1 = d_ref[0, 1] * LOG2E
    zeros = jnp.zeros((BLOCK, BLOCK), F32)
    for a in range(nb):
        for b in range(nb):
            rs, cs = slice(a * BLOCK, (a + 1) * BLOCK), slice(b * BLOCK, (b + 1) * BLOCK)
            if a == b:
                blk = d0
            elif b == a + 1:
                blk = d1
            elif b > a:
                blk = zeros
            else:
                blk = jnp.full((BLOCK, BLOCK), NEG, F32)
            bias_ref[BIAS_DIAG, rs, cs] = blk
            bias_ref[BIAS_LEFT, rs, cs] = d1 if (b == 0 and a == nb - 1) else zeros
    bias_ref[BIAS_NONE] = jnp.zeros((tq, tq), F32)
    mb_ref[0] = jnp.zeros((N_META, tq), F32)
    mb_ref[1] = bm0_ref[0, :N_META, :] * LOG2E

    lane = lax.broadcasted_iota(jnp.int32, (tq, LANES), 1)
    for i in range(nq):
        rows = slice(i * tq, (i + 1) * tq)
        q = q_ref[rows, :].astype(F32)
        qs_ref[i] = jnp.transpose(jnp.concatenate([jnp.where(lane < HEAD_DIM, q, 0.0),
                                                   jnp.where(lane >= HEAD_DIM, q, 0.0)], axis=0)).astype(BF16)
        vt_ref[i, :LANES, :] = jnp.transpose(v_ref[rows, :].astype(F32)).astype(BF16)
        kt_ref[i] = k_ref[rows, :]
    vt_ref[nq, :LANES, :] = jnp.transpose(vm_ref[...].astype(F32)).astype(BF16)
    kt_ref[nq] = km_ref[...]
    vt_ref[:, LANES:, :] = jnp.ones((nq + 1, ONES_ROWS, tq), BF16)
    acc_ref[...] = jnp.zeros(acc_ref.shape, F32)
    m_ref[...] = jnp.full(m_ref.shape, NEG, F32)
    lv = lamv_ref[...]
    lam = (jnp.exp(jnp.sum(lv[0:1] * lv[1:2], axis=-1, keepdims=True))
           - jnp.exp(jnp.sum(lv[2:3] * lv[3:4], axis=-1, keepdims=True)) + lambda_init)

    FAR, NEAR, META = 0, 1, 2

    def stage_a(n, slot, kind):
        qi, t = qi_tab[n], t_tab[n]
        if kind == META:
            s = jnp.dot(kt_ref[nq, :N_META, :], qs_ref[qi], preferred_element_type=F32)
            s_buf[slot, :N_META] = s + jnp.tile(mb_ref[jnp.where(qi == 0, 1, 0)], (1, 2))
            return
        s = jnp.dot(kt_ref[t - 1], qs_ref[qi], preferred_element_type=F32)
        if kind == NEAR:
            which = jnp.where(t == qi + 1, BIAS_DIAG, jnp.where(t == qi, BIAS_LEFT, BIAS_NONE))
            s = s + jnp.tile(bias_ref[which], (1, 2))
        s_buf[slot] = s

    def stage_b(n, slot, kind):
        qi = qi_tab[n]
        rows = slice(0, N_META if kind == META else tq)
        s = s_buf[slot, rows]
        m_prev = m_ref[qi]
        m_new = jnp.maximum(m_prev, jnp.max(s, axis=0, keepdims=True))
        a_buf[slot] = jnp.exp2(m_prev - m_new)
        p_buf[slot, rows] = jnp.exp2(s - m_new[0:1]).astype(BF16)
        m_ref[qi] = m_new

    def stage_c(n, slot, kind):
        qi, t = qi_tab[n], t_tab[n]
        if kind == META:
            pv = jnp.dot(vt_ref[nq, :, :N_META], p_buf[slot, :N_META], preferred_element_type=F32)
        else:
            pv = jnp.dot(vt_ref[t - 1], p_buf[slot], preferred_element_type=F32)
        acc_ref[qi] = a_buf[slot][0:1] * acc_ref[qi] + pv

    LEAD, SLOTS, UNROLL = 2, 3, 12
    assert UNROLL % SLOTS == 0

    def pipeline(base, count, kind, biased_from=None):
        def kind_a(j):
            return kind if biased_from is None or j < biased_from else NEAR

        if count <= 2 * LEAD:
            for j in range(count):
                stage_a(base + j, 0, kind_a(j))
                stage_b(base + j, 0, kind)
                stage_c(base + j, 0, kind)
            return
        for j in range(2 * LEAD):
            stage_a(base + j, j % SLOTS, kind_a(j))
            if j >= LEAD:
                stage_b(base + j - LEAD, (j - LEAD) % SLOTS, kind)

        def steps(n, first, count):
            for j in range(count):
                stage_a(base + n + j + 2 * LEAD, (j + 2 * LEAD) % SLOTS, kind_a(first + j + 2 * LEAD))
                stage_b(base + n + j + LEAD, (j + LEAD) % SLOTS, kind)
                stage_c(base + n + j, j % SLOTS, kind)

        n_steady = count - 2 * LEAD
        n_blocks = n_steady // UNROLL
        switch = n_blocks if biased_from is None else (biased_from - 2 * LEAD) // UNROLL
        assert biased_from is None or (biased_from - 2 * LEAD) % UNROLL == 0
        for lo, hi in ((0, min(switch, n_blocks)), (min(switch, n_blocks), n_blocks)):
            lax.fori_loop(lo, hi, lambda k, carry, lo=lo: (steps(UNROLL * k, UNROLL * lo, UNROLL), carry)[1], 0)
        steps(UNROLL * n_blocks, UNROLL * n_blocks, n_steady - UNROLL * n_blocks)
        for j in range(n_steady, count):
            if j + LEAD < count:
                stage_b(base + j + LEAD, (j + LEAD) % SLOTS, kind)
            stage_c(base + j, j % SLOTS, kind)

    if n_far >= 2 * LEAD:
        pipeline(0, n_far + n_near, FAR, 2 * LEAD + UNROLL * ((n_far - 2 * LEAD) // UNROLL))
    else:
        pipeline(0, n_far + n_near, NEAR)
    pipeline(n_far + n_near, n_steps - n_far - n_near, META)

    for i in range(nq):
        acc = acc_ref[i]
        o = acc[:LANES] * (1.0 / acc[LANES:LANES + 1])
        d = o[:, :tq] - lam * o[:, tq:]
        y = d * lax.rsqrt(jnp.mean(d * d, axis=0, keepdims=True) + EPS) * jnp.tile(gain_ref[...], (1, tq // LANES))
        o_ref[i * tq:(i + 1) * tq, :] = jnp.transpose(y * (1.0 - lambda_init)).astype(BF16)


def _diff_attention(qkv, km, vm, dblk, bm0, lamv, gain, batch, seq, lambda_init):
    nq = seq // TQ
    far = [(qi, t) for qi in range(nq) for t in range(1, qi)]
    near = [(qi, t) for qi in range(nq) for t in (qi, qi + 1) if t >= 1]
    meta = [(qi, 0) for qi in range(nq)]
    steps = far + near + meta
    qi_tab = jnp.asarray([s[0] for s in steps], jnp.int32)
    t_tab = jnp.asarray([s[1] for s in steps], jnp.int32)
    kern = functools.partial(_diff_kernel, lambda_init=lambda_init, n_steps=len(steps), n_far=len(far),
                             n_near=len(near))
    return pl.pallas_call(
        kern,
        out_shape=jax.ShapeDtypeStruct((batch * seq, N_DIFF_HEADS * LANES), BF16),
        grid_spec=pltpu.PrefetchScalarGridSpec(
            num_scalar_prefetch=2,
            grid=(batch, N_DIFF_HEADS),
            in_specs=[
                pl.BlockSpec((seq, LANES), lambda b, h, *_: (b, C_DQ // LANES + h)),
                pl.BlockSpec((seq, LANES), lambda b, h, *_: (b, C_DK // LANES + h)),
                pl.BlockSpec((seq, LANES), lambda b, h, *_: (b, C_DV // LANES + h)),
                pl.BlockSpec((TQ, LANES), lambda b, h, *_: (0, h)),
                pl.BlockSpec((TQ, LANES), lambda b, h, *_: (0, h)),
                pl.BlockSpec((1, 2, BLOCK, BLOCK), lambda b, h, *_: (h, 0, 0, 0)),
                pl.BlockSpec((1, LANES, TQ), lambda b, h, *_: (h, 0, 0)),
                pl.BlockSpec(lamv.shape, lambda b, h, *_: (0, 0)),
                pl.BlockSpec((LANES, LANES), lambda b, h, *_: (0, 0)),
            ],
            out_specs=pl.BlockSpec((seq, LANES), lambda b, h, *_: (b, h)),
            scratch_shapes=[
                pltpu.VMEM((3, TQ, TQ), F32),
                pltpu.VMEM((2, N_META, TQ), F32),
                pltpu.VMEM((nq, LANES, 2 * TQ), BF16),
                pltpu.VMEM((nq + 1, TQ, LANES), BF16),
                pltpu.VMEM((nq + 1, LANES + ONES_ROWS, TQ), BF16),
                pltpu.VMEM((3, TQ, 2 * TQ), F32),
                pltpu.VMEM((3, TQ, 2 * TQ), BF16),
                pltpu.VMEM((3, 8, 2 * TQ), F32),
                pltpu.VMEM((nq, 8, 2 * TQ), F32),
                pltpu.VMEM((nq, LANES + ONES_ROWS, 2 * TQ), F32),
            ],
        ),
        compiler_params=_cparams(("parallel", "parallel")),
        name="diff_attention",
    )(qi_tab, t_tab, qkv, qkv, qkv, km, vm, jnp.swapaxes(dblk, -1, -2), jnp.swapaxes(bm0, -1, -2), lamv,
      jnp.broadcast_to(gain.reshape(LANES, 1), (LANES, LANES)))


def _swa_kernel(sink_ref, q_ref, k_ref, v_ref, km_ref, vm_ref, bt_ref, o_ref, kd_ref, vt_ref,
                s_scr, p_scr, inv_scr):
    nkb = k_ref.shape[0] // BLOCK
    lane = lax.broadcasted_iota(jnp.int32, (BLOCK, LANES), 1)
    pairs = [(g, u) for g in range(N_SWA_KV) for u in range(2)]

    def both_halves(k):
        k0, k1 = k[:, :HEAD_DIM], k[:, HEAD_DIM:]
        return jnp.concatenate([k0, k0, k1, k1], axis=1)

    def prepare(j, carry):
        rows = pl.ds(pl.multiple_of(j * BLOCK, BLOCK), BLOCK)
        kd_ref[j] = both_halves(k_ref[rows, :])
        vt_ref[j] = jnp.transpose(v_ref[rows, :].astype(F32)).astype(BF16)
        return carry
    lax.fori_loop(0, nkb, prepare, 0)
    kd_ref[nkb] = both_halves(km_ref[...])
    vt_ref[nkb] = jnp.transpose(vm_ref[...].astype(F32)).astype(BF16)

    def scores(n, slot):
        first = jnp.where(n == 0, 0, 1)
        prev = jnp.maximum(n - 1, 0)
        r_q = pl.multiple_of(n * BLOCK, BLOCK)
        for c, (g, u) in enumerate(pairs):
            ks = slice(g * LANES, (g + 1) * LANES)
            kcat = jnp.concatenate([kd_ref[prev, :, ks], kd_ref[n, :, ks], kd_ref[nkb, :N_META, ks]], axis=0)
            h0 = 4 * g + 2 * u
            qp = q_ref[pl.ds(r_q, BLOCK), (2 * g + u) * LANES:(2 * g + u + 1) * LANES].astype(F32)
            qs = jnp.transpose(jnp.concatenate([jnp.where(lane < HEAD_DIM, qp, 0.0),
                                                jnp.where(lane >= HEAD_DIM, qp, 0.0)], axis=0)).astype(BF16)
            s = jnp.dot(kcat, qs, preferred_element_type=F32)
            s_scr[slot, c] = s + jnp.concatenate([bt_ref[first, h0], bt_ref[first, h0 + 1]], axis=1)

    def exponentials(slot):
        for c, (g, u) in enumerate(pairs):
            h0 = 4 * g + 2 * u
            s = s_scr[slot, c]
            sink = jnp.concatenate([sink_ref[h0:h0 + 1, :], sink_ref[h0 + 1:h0 + 2, :]], axis=1)
            m = jnp.maximum(jnp.max(s, axis=0, keepdims=True), sink)
            p = jnp.exp(s - m)
            p_scr[slot, c] = p.astype(BF16)
            inv_scr[slot, c] = jnp.broadcast_to(1.0 / (jnp.sum(p, axis=0, keepdims=True) + jnp.exp(sink - m)),
                                                inv_scr.shape[2:])

    def values(n, slot):
        prev = jnp.maximum(n - 1, 0)
        r_q = pl.multiple_of(n * BLOCK, BLOCK)
        for c, (g, u) in enumerate(pairs):
            vs = slice(g * HEAD_DIM, (g + 1) * HEAD_DIM)
            vcat = jnp.concatenate([vt_ref[prev, vs, :], vt_ref[n, vs, :]], axis=1)
            o = (jnp.dot(vcat, p_scr[slot, c, :2 * BLOCK], preferred_element_type=F32)
                 + jnp.dot(vt_ref[nkb, vs, :N_META], p_scr[slot, c, 2 * BLOCK:], preferred_element_type=F32)
                 ) * inv_scr[slot, c][0:1]
            ot = jnp.transpose(o)
            o_ref[pl.ds(r_q, BLOCK), (2 * g + u) * LANES:(2 * g + u + 1) * LANES] = (
                jnp.concatenate([ot[:BLOCK], ot[BLOCK:]], axis=1).astype(BF16))

    scores(0, 0)
    scores(1, 1)
    exponentials(0)

    def blocks(n, count):
        for j in range(count):
            scores(n + j + 2, j % 2)
            exponentials((j + 1) % 2)
            values(n + j, j % 2)

    unroll = 4
    n_steady = nkb - 2
    lax.fori_loop(0, n_steady // unroll, lambda k, carry: (blocks(unroll * k, unroll), carry)[1], 0)
    blocks(n_steady // unroll * unroll, n_steady % unroll)
    exponentials(1)
    values(nkb - 2, 0)
    values(nkb - 1, 1)


def _swa_attention(sinks, qkv, km, vm, bt, batch, seq):
    nkb = seq // BLOCK
    swa_q = N_SWA_HEADS * HEAD_DIM
    assert nkb % 2 == 0
    sinkv =jnp.broadcast_to(sinks.reshape(N_SWA_HEADS, 1), (N_SWA_HEADS, LANES))
    return pl.pallas_call(
        _swa_kernel,
        out_shape=jax.ShapeDtypeStruct((batch * seq, N_SWA_HEADS * HEAD_DIM), BF16),
        grid=(batch,),
        in_specs=[
            pl.BlockSpec(sinkv.shape, lambda b: (0, 0)),
            pl.BlockSpec((seq, swa_q), lambda b: (b, C_SQ // swa_q)),
            pl.BlockSpec((seq, LANES), lambda b: (b, C_SK // LANES)),
            pl.BlockSpec((seq, LANES), lambda b: (b, C_SV // LANES)),
            pl.BlockSpec(km.shape, lambda b: (0, 0)),
            pl.BlockSpec(vm.shape, lambda b: (0, 0)),
            pl.BlockSpec(bt.shape, lambda b: (0, 0, 0, 0)),
        ],
        out_specs=pl.BlockSpec((seq, swa_q), lambda b: (b, 0)),
        scratch_shapes=[pltpu.VMEM((nkb + 1, BLOCK, 2 * LANES), BF16),
                        pltpu.VMEM((nkb + 1, LANES, BLOCK), BF16),
                        pltpu.VMEM((2, 4, 2 * BLOCK + N_META, 2 * BLOCK), F32),
                        pltpu.VMEM((2, 4, 2 * BLOCK + N_META, 2 * BLOCK), BF16),
                        pltpu.VMEM((2, 4, 8, 2 * BLOCK), F32)],
        compiler_params=_cparams(("parallel",)),
        name="swa_attention",
    )(sinkv, qkv, qkv, qkv, km, vm, bt)


def _outproj_kernel(x_ref, md_ref, ms_ref, wo_ref, g2_ref, wr_ref, br_ref,
                    h_ref, hb_ref, rt_ref, ti_ref, cnt_ref, c_ref, lg_ref):
    i = pl.program_id(0)

    @pl.when(i == 0)
    def _init():
        c_ref[...] = jnp.zeros(c_ref.shape, F32)
        lg_ref[...] = jnp.zeros(lg_ref.shape, F32)

    lg_prev = lg_ref[...]
    half = md_ref.shape[1]
    h = (x_ref[...]
         + jnp.dot(md_ref[...], wo_ref[:half, :], preferred_element_type=F32)
         + jnp.dot(ms_ref[...], wo_ref[half:, :], preferred_element_type=F32))
    h_ref[...] = h
    hn = h * lax.rsqrt(jnp.mean(h * h, axis=-1, keepdims=True) + EPS) * g2_ref[...]
    hb = hn.astype(BF16)
    hb_ref[...] = hb
    lg_ref[...] = jnp.dot(hb, wr_ref[...], preferred_element_type=F32) + br_ref[...]
    _route_tile(lg_prev, jnp.where(i > 0, 1.0, 0.0), rt_ref, ti_ref, c_ref)

    @pl.when(i == pl.num_programs(0) - 1)
    def _fin():
        cnt_ref[...] = c_ref[...]


def _route_tile(lg, live, rt_ref, ti_ref, c_ref):
    tm = lg.shape[0]
    lane_i = lax.broadcasted_iota(jnp.int32, lg.shape, 1)
    lane = lane_i.astype(F32)
    big = float(4 * LANES)
    is_g = (lane_i >= N_EXPERTS) & (lane_i < N_EXPERTS + N_GROUPS)
    glm = jnp.where(is_g, lg, -jnp.inf)
    gmax = jnp.max(glm, axis=1, keepdims=True)
    gidx = jnp.min(jnp.where(glm == gmax, lane, big), axis=1, keepdims=True) - N_EXPERTS
    gsum = jnp.sum(jnp.where(is_g, jnp.exp(lg - gmax), 0.0), axis=1, keepdims=True)
    g_w = 1.0 / gsum
    lane_grp = (lane_i >> 3).astype(F32)
    in_grp = (lane_i < N_EXPERTS) & (lane_grp == gidx)
    el = jnp.where(in_grp, lg, -jnp.inf)
    t1 = jnp.max(el, axis=1, keepdims=True)
    j1 = jnp.min(jnp.where(el == t1, lane, big), axis=1, keepdims=True)
    el2 = jnp.where(lane == j1, -jnp.inf, el)
    t2 = jnp.max(el2, axis=1, keepdims=True)
    j2 = jnp.min(jnp.where(el2 == t2, lane, big), axis=1, keepdims=True)
    e2 = jnp.exp(t2 - t1)
    den = 1.0 + e2
    gate1 = g_w / den
    gate2 = g_w * e2 / den

    o1 = lane == j1
    o2 = lane == j2
    onehot = jnp.where(o1 | o2, 1.0, 0.0).astype(BF16)
    rr = lax.broadcasted_iota(jnp.int32, (tm, tm), 0)
    cc = lax.broadcasted_iota(jnp.int32, (tm, tm), 1)
    lower = jnp.where(rr > cc, 1.0, 0.0).astype(BF16)
    pfx = jnp.dot(lower, onehot, preferred_element_type=F32)
    cnt_tile = jnp.sum(onehot.astype(F32), axis=0, keepdims=True)
    groups = jnp.floor((cnt_tile + (ROW_ALIGN - 1)) * (1.0 / ROW_ALIGN))
    er = lax.broadcasted_iota(jnp.int32, (LANES, LANES), 0)
    ec = lax.broadcasted_iota(jnp.int32, (LANES, LANES), 1)
    before = jnp.where(er < ec, 1.0, 0.0).astype(BF16)
    cbase = ROW_ALIGN * jnp.dot(jnp.broadcast_to(groups, (8, LANES)).astype(BF16), before,
                                preferred_element_type=F32)[0:1]
    at = pfx + cbase
    pos1 = jnp.sum(jnp.where(o1, at, 0.0), axis=1, keepdims=True)
    pos2 = jnp.sum(jnp.where(o2, at, 0.0), axis=1, keepdims=True)
    rt_ref[...] = jnp.where(lane_i == 0, gate1,
                            jnp.where(lane_i == 1, gate2,
                                      jnp.where(lane_i == 2, pos1,
                                                jnp.where(lane_i == 3, pos2, 0.0))))
    c_old = c_ref[...]
    c_ref[...] = c_old + groups * (ROW_ALIGN * live)
    row8 = lax.broadcasted_iota(jnp.int32, (8, LANES), 0)
    ti_ref[...] = jnp.where(row8 == 0, cnt_tile, jnp.where(row8 == 1, c_old, 0.0))


def _outproj(x2, mixd, mixs, wo, g2, wr, br):
    n, d = x2.shape
    nt = n // TM

    def proj_tile(i):
        return (jnp.minimum(i, nt - 1), 0)

    def route_tile(i):
        return (jnp.maximum(i - 1, 0), 0)

    return pl.pallas_call(
        _outproj_kernel,
        out_shape=(jax.ShapeDtypeStruct((n, d), F32),
                   jax.ShapeDtypeStruct((n, d), BF16),
                   jax.ShapeDtypeStruct((n, LANES), F32),
                   jax.ShapeDtypeStruct((nt * 8, LANES), F32),
                   jax.ShapeDtypeStruct((8, LANES), F32)),
        grid=(nt + 1,),
        in_specs=[
            pl.BlockSpec((TM, d), proj_tile),
            pl.BlockSpec((TM, mixd.shape[1]), proj_tile),
            pl.BlockSpec((TM, mixs.shape[1]), proj_tile),
            pl.BlockSpec(wo.shape, lambda i: (0, 0)),
            pl.BlockSpec(g2.shape, lambda i: (0, 0)),
            pl.BlockSpec(wr.shape, lambda i: (0, 0)),
            pl.BlockSpec(br.shape, lambda i: (0, 0)),
        ],
        out_specs=(pl.BlockSpec((TM, d), proj_tile),
                   pl.BlockSpec((TM, d), proj_tile),
                   pl.BlockSpec((TM, LANES), route_tile),
                   pl.BlockSpec((8, LANES), route_tile),
                   pl.BlockSpec((8, LANES), lambda i: (0, 0))),
        scratch_shapes=[pltpu.VMEM((8, LANES), F32), pltpu.VMEM((TM, LANES), F32)],
        compiler_params=_cparams(("arbitrary",)),
        name="outproj_router",
    )(x2, mixd, mixs, wo, g2, wr, br)


def _chunk_table(run_start, run_groups):
    nt = run_start.shape[0]
    sorted_start = (jnp.cumsum(run_groups, axis=1) - run_groups) * ROW_ALIGN
    word = run_start * SORTED_RANGE + sorted_start
    done = jnp.zeros_like(run_groups)
    segments, counts = [], []
    for k, rows in enumerate(CHUNK_ROWS):
        per_run = run_groups // (rows // ROW_ALIGN)
        if k > 0:
            per_run = per_run % 2
        last = jnp.cumsum(per_run, axis=1)[:, None, :]
        first = last - per_run[:, None, :]
        slot = jnp.arange(CHUNK_SLOTS[k], dtype=jnp.int32)[None, :, None]
        value = word[:, None, :] + (done[:, None, :] + (slot - first) * rows) * (SORTED_RANGE + 1)
        segments.append(jnp.sum(jnp.where((first <= slot) & (slot < last), value, 0), axis=2))
        counts.append(last[:, 0, -1:])
        done = done + per_run * rows
    pad = jnp.zeros((nt, LANES - CHUNK_COUNTS - len(CHUNK_ROWS)), jnp.int32)
    return jnp.concatenate(segments + counts + [pad], axis=1).astype(jnp.int32).reshape(nt, 1, LANES)


def _for_each_chunk(chunks_ref, fn):
    def issue(k, lo, hi, priority):
        def body(c, carry):
            word = chunks_ref[0, 0, CHUNK_OFFSETS[k] + c]
            fn(pl.multiple_of(word >> (SORTED_RANGE.bit_length() - 1), ROW_ALIGN),
               pl.multiple_of(word & (SORTED_RANGE - 1), ROW_ALIGN), CHUNK_ROWS[k], priority)
            return carry
        lax.fori_loop(lo, hi, body, 0)

    n_big = chunks_ref[0, 0, CHUNK_COUNTS]
    issue(0, 0, n_big // 2, 0)
    issue(0, n_big // 2, n_big, 1)
    for k in range(1, len(CHUNK_ROWS)):
        issue(k, 0, chunks_ref[0, 0, CHUNK_COUNTS + k], (k - 1) % 2)


def _wait_chunks(chunks_ref, make_copy):
    for k, rows in enumerate(CHUNK_ROWS):
        def body(c, carry, rows=rows):
            make_copy(rows).wait()
            return carry
        lax.fori_loop(0, chunks_ref[0, 0, CHUNK_COUNTS + k], body, 0)


def _dispatch_kernel(zf_ref, cur_ref, prv_ref, hb_ref, rt_ref, xs_ref, sbuf, zbuf, sem, zsem):
    i = pl.program_id(0)
    nt = pl.num_programs(0)
    slot = i % 2
    tm, d = hb_ref.shape

    def for_zero_blocks(kind, fn):
        def body(b, carry):
            @pl.when(zf_ref[b] == kind)
            def _():
                fn(pltpu.make_async_copy(zbuf, xs_ref.at[pl.ds(pl.multiple_of(b * EB, EB), EB)],
                                         zsem.at[kind - 1]))
            return carry
        lax.fori_loop(0, zf_ref.shape[0], body, 0)

    @pl.when(i == 0)
    def _():
        zbuf[...] = jnp.zeros(zbuf.shape, zbuf.dtype)
        for_zero_blocks(1, lambda c: c.start())
        for_zero_blocks(2, lambda c: c.start())
        for_zero_blocks(1, lambda c: c.wait())

    pos_t = jnp.transpose(rt_ref[...])
    srow = lax.broadcasted_iota(jnp.int32, (SROWS, tm), 0).astype(F32)
    sel = jnp.where(srow == pos_t[2:3, :], 1.0, jnp.where(srow == pos_t[3:4, :], 1.0, 0.0)).astype(BF16)
    srt = jnp.dot(sel, hb_ref[...], preferred_element_type=F32)
    bits = pltpu.bitcast(srt, jnp.uint32)
    sbuf[slot] = (bits[:, d // 2:] & jnp.uint32(0xFFFF0000)) | (bits[:, :d // 2] >> 16)

    def chunk_copy(run_row, sorted_row, rows, sl):
        return pltpu.make_async_copy(sbuf.at[sl, pl.ds(sorted_row, rows)], xs_ref.at[pl.ds(run_row, rows)],
                                     sem.at[sl])

    _for_each_chunk(cur_ref, lambda run_row, sorted_row, rows, priority:
                    chunk_copy(run_row, sorted_row, rows, slot).start(priority=priority))

    @pl.when(i > 0)
    def _():
        _wait_chunks(prv_ref, lambda rows: chunk_copy(0, 0, rows, 1 - slot))

    @pl.when(i == nt - 1)
    def _():
        _wait_chunks(cur_ref, lambda rows: chunk_copy(0, 0, rows, slot))
        for_zero_blocks(2, lambda c: c.wait())


def _dispatch(zero_blocks, runs, hb, rt, n_rows):
    n, d = hb.shape
    return pl.pallas_call(
        _dispatch_kernel,
        out_shape=jax.ShapeDtypeStruct((n_rows, d // 2), jnp.uint32),
        grid_spec=pltpu.PrefetchScalarGridSpec(
            num_scalar_prefetch=1,
            grid=(n // TM,),
            in_specs=[
                pl.BlockSpec((1, 1, LANES), lambda i, zf: (i, 0, 0), memory_space=pltpu.SMEM),
                pl.BlockSpec((1, 1, LANES), lambda i, zf: (jnp.maximum(i - 1, 0), 0, 0), memory_space=pltpu.SMEM),
                pl.BlockSpec((TM, d), lambda i, zf: (i, 0)),
                pl.BlockSpec((TM, LANES), lambda i, zf: (i, 0)),
            ],
            out_specs=pl.BlockSpec(memory_space=pl.ANY),
            scratch_shapes=[pltpu.VMEM((2, SROWS, d // 2), jnp.uint32), pltpu.VMEM((EB, d // 2), jnp.uint32),
                            pltpu.SemaphoreType.DMA((2,)), pltpu.SemaphoreType.DMA((2,))],
        ),
        compiler_params=_cparams(("arbitrary",)),
        name="dispatch",
    )(zero_blocks, runs, runs, hb, rt)


def _experts_kernel(be_ref, na_ref, nxt_ref, xs_ref, wg_hbm, wu_hbm, wd_hbm, ys_ref,
                    wgf, wuf, wdf, wgb, wub, wdb, sem):
    b = pl.program_id(0)

    def weight_copies(e):
        return (pltpu.make_async_copy(wg_hbm.at[e], wgf, sem.at[0]),
                pltpu.make_async_copy(wu_hbm.at[e], wuf, sem.at[1]),
                pltpu.make_async_copy(wd_hbm.at[e], wdf, sem.at[2]))

    @pl.when(b == 0)
    def _():
        for c in weight_copies(be_ref[0]):
            c.start()

    @pl.when(b < na_ref[0])
    def _():
        e = be_ref[b]
        changed = jnp.logical_or(b == 0, be_ref[jnp.maximum(b - 1, 0)] != e)

        @pl.when(changed)
        def _load():
            for c in weight_copies(e):
                c.wait()
            wgb[...] = wgf[...].astype(BF16)
            wub[...] = wuf[...].astype(BF16)
            wdb[...] = wdf[...].astype(BF16)
            nxt = nxt_ref[e]

            @pl.when(nxt >= 0)
            def _():
                for c in weight_copies(nxt):
                    c.start()

        w = xs_ref[...]
        x_lo = pltpu.bitcast(w << 16, F32).astype(BF16)
        x_hi = pltpu.bitcast(w & jnp.uint32(0xFFFF0000), F32).astype(BF16)
        dh = w.shape[1]
        g = (jnp.dot(x_lo, wgb[:dh, :], preferred_element_type=F32)
             + jnp.dot(x_hi, wgb[dh:, :], preferred_element_type=F32))
        u = (jnp.dot(x_lo, wub[:dh, :], preferred_element_type=F32)
             + jnp.dot(x_hi, wub[dh:, :], preferred_element_type=F32))
        hdn = g * (1.0 / (1.0 + jnp.exp(-g))) * u
        y = jnp.dot(hdn.astype(BF16), wdb[...], preferred_element_type=F32)
        bits = pltpu.bitcast(y.astype(BF16).astype(F32), jnp.uint32)
        ys_ref[...] = (bits[:, dh:] & jnp.uint32(0xFFFF0000)) | (bits[:, :dh] >> 16)

    @pl.when(b >= na_ref[0])
    def _():
        ys_ref[...] = jnp.zeros(ys_ref.shape, ys_ref.dtype)


def _experts(blk_e, n_act, nxt_e, xs, w_gate, w_up, w_down):
    p, dh = xs.shape
    d = 2 * dh
    de = w_gate.shape[2]

    def row_map(b, be, na, nx):
        return (jnp.minimum(b, na[0] - 1), 0)

    return pl.pallas_call(
        _experts_kernel,
        out_shape=jax.ShapeDtypeStruct((p, dh), jnp.uint32),
        grid_spec=pltpu.PrefetchScalarGridSpec(
            num_scalar_prefetch=3,
            grid=(p // EB,),
            in_specs=[
                pl.BlockSpec((EB, dh), row_map),
                pl.BlockSpec(memory_space=pl.ANY),
                pl.BlockSpec(memory_space=pl.ANY),
                pl.BlockSpec(memory_space=pl.ANY),
            ],
            out_specs=pl.BlockSpec((EB, dh), lambda b, be, na, nx: (b, 0)),
            scratch_shapes=[pltpu.VMEM((d, de), F32), pltpu.VMEM((d, de), F32), pltpu.VMEM((de, d), F32),
                            pltpu.VMEM((d, de), BF16), pltpu.VMEM((d, de), BF16), pltpu.VMEM((de, d), BF16),
                            pltpu.SemaphoreType.DMA((3,))],
        ),
        compiler_params=_cparams(("arbitrary",)),
        name="experts",
    )(blk_e, n_act, nxt_e, xs, w_gate, w_up, w_down)


def _combine_kernel(cur_ref, nxt_ref, ys_ref, h_ref, rt_ref, o_ref, ybuf, sem):
    i = pl.program_id(0)
    nt = pl.num_programs(0)
    slot = i % 2
    tm = h_ref.shape[0]

    def chunk_copy(run_row, sorted_row, rows, sl):
        return pltpu.make_async_copy(ys_ref.at[pl.ds(run_row, rows)], ybuf.at[sl, pl.ds(sorted_row, rows)],
                                     sem.at[sl])

    @pl.when(i == 0)
    def _():
        ybuf[...] = jnp.zeros(ybuf.shape, ybuf.dtype)
        _for_each_chunk(cur_ref, lambda run_row, sorted_row, rows, priority:
                        chunk_copy(run_row, sorted_row, rows, 0).start(priority=priority))

    @pl.when(i + 1 < nt)
    def _():
        _for_each_chunk(nxt_ref, lambda run_row, sorted_row, rows, priority:
                        chunk_copy(run_row, sorted_row, rows, 1 - slot).start(priority=priority))

    _wait_chunks(cur_ref, lambda rows: chunk_copy(0, 0, rows, slot))

    rt = rt_ref[...]
    w = ybuf[slot]
    dh = w.shape[1]
    y_lo = pltpu.bitcast(w << 16, F32).astype(BF16)
    y_hi = pltpu.bitcast(w & jnp.uint32(0xFFFF0000), F32).astype(BF16)
    col = lax.broadcasted_iota(jnp.int32, (tm, SROWS), 1).astype(F32)
    wsel = jnp.where(col == rt[:, 2:3], rt[:, 0:1], jnp.where(col == rt[:, 3:4], rt[:, 1:2], 0.0)).astype(BF16)
    for half, yb in ((slice(0, dh), y_lo), (slice(dh, 2 * dh), y_hi)):
        o_ref[:, half] = h_ref[:, half] + jnp.dot(wsel, yb, preferred_element_type=F32)


def _combine(runs, ys, h1, rt):
    n, d = h1.shape
    nt = n // TM
    return pl.pallas_call(
        _combine_kernel,
        out_shape=jax.ShapeDtypeStruct((n, d), F32),
        grid=(nt,),
        in_specs=[
            pl.BlockSpec((1, 1, LANES), lambda i: (i, 0, 0), memory_space=pltpu.SMEM),
            pl.BlockSpec((1, 1, LANES), lambda i: (jnp.minimum(i + 1, nt - 1), 0, 0), memory_space=pltpu.SMEM),
            pl.BlockSpec(memory_space=pl.ANY),
            pl.BlockSpec((TM, d), lambda i: (i, 0)),
            pl.BlockSpec((TM, LANES), lambda i: (i, 0)),
        ],
        out_specs=pl.BlockSpec((TM, d), lambda i: (i, 0)),
        scratch_shapes=[pltpu.VMEM((2, SROWS, d // 2), jnp.uint32), pltpu.SemaphoreType.DMA((2,))],
        compiler_params=_cparams(("arbitrary",)),
        name="combine",
    )(runs, runs, ys, h1, rt)


def kernel(x, meta_tokens, rel_bias, norm1_gain, w_in, diff_q_gain, diff_k_gain, lam_q1, lam_k1, lam_q2, lam_k2, diff_subln_gain, swa_q_gain, swa_k_gain, swa_sinks, w_out, norm2_gain, w_group, b_group, w_router, b_router, w_gate, w_up, w_down):
    batch, seq, d = x.shape
    depth = w_in.shape[0]
    n = batch * seq
    assert seq % TQ == 0 and n % TM == 0 and n % TP == 0 and d == 1024
    assert meta_tokens.shape[0] == N_META
    assert depth == 1, "the meta-token rows of the residual stream are not carried across layers"

    h = x.reshape(n, d)
    dblk, bm0, bt = _bias_tables(rel_bias, TQ)
    scale = HEAD_DIM ** -0.5
    bd = jnp.asarray(np.kron(np.eye(MXU_DIM // HEAD_DIM), np.full((HEAD_DIM, HEAD_DIM), 1.0 / HEAD_DIM)), BF16)
    ones = jnp.ones((HEAD_DIM,), F32)
    lower_pad = N_EXPERTS + N_GROUPS

    for layer in range(depth):
        lambda_init = 0.8 - 0.6 * math.exp(-0.3 * layer)
        w_cat = w_in[layer].astype(BF16)
        gain = jnp.concatenate([
            jnp.tile(diff_q_gain[layer] * (scale * LOG2E), 2 * N_DIFF_HEADS),
            jnp.tile(diff_k_gain[layer], 2 * N_DIFF_HEADS),
            jnp.tile(ones, 2 * N_DIFF_HEADS),
            jnp.tile(swa_q_gain[layer] * scale, N_SWA_HEADS),
            jnp.tile(swa_k_gain[layer], N_SWA_KV),
            jnp.tile(ones, N_SWA_KV)]).reshape(1, C_END).astype(F32)
        nmask = np.zeros((1, C_END), np.float32)
        nmask[:, C_DQ:C_DV] = 1.0
        nmask[:, C_SQ:C_SV] = 1.0
        nmask = jnp.asarray(nmask)
        g1 = norm1_gain[layer].reshape(1, d).astype(F32)

        qkv = _proj(h, g1, w_cat, bd, gain, nmask, TP)
        qkv_meta = _proj(meta_tokens.astype(F32), g1, w_cat, bd, gain, nmask, N_META)
        meta_pad = jnp.pad(qkv_meta, ((0, TQ - N_META), (0, 0)))

        lamv = jnp.pad(jnp.stack([lam_q1[layer], lam_k1[layer], lam_q2[layer], lam_k2[layer]]).astype(F32),
                       ((0, 4), (0, LANES - HEAD_DIM)))
        mixd = _diff_attention(qkv, meta_pad[:, C_DK:C_DV], meta_pad[:, C_DV:C_SQ], dblk, bm0, lamv,
                               diff_subln_gain[layer].reshape(1, LANES).astype(F32), batch, seq, lambda_init)
        mixs = _swa_attention(swa_sinks[layer].astype(F32), qkv, meta_pad[:BLOCK, C_SK:C_SV],
                              meta_pad[:BLOCK, C_SV:C_END], jnp.swapaxes(bt, -1, -2), batch, seq)

        wr = jnp.pad(jnp.concatenate([w_router[layer], w_group[layer]], axis=1),
                     ((0, 0), (0, LANES - lower_pad))).astype(BF16)
        br = jnp.pad(jnp.concatenate([b_router[layer], b_group[layer]]), (0, LANES - lower_pad)).reshape(1, LANES)
        h1, hb, rt, tinfo, cnt = _outproj(h, mixd, mixs, w_out[layer].astype(BF16),
                                          norm2_gain[layer].reshape(1, d).astype(F32), wr, br.astype(F32))

        nt = n // TM
        counts = cnt[0, :N_EXPERTS].astype(jnp.int32)
        nblk_e = (counts + EB - 1) // EB
        blk_end = jnp.cumsum(nblk_e)
        pstart = ((blk_end - nblk_e) * EB).astype(jnp.int32)
        n_blocks = -(-(2 * n + nt * N_EXPERTS * (ROW_ALIGN - 1) + N_EXPERTS * (EB - 1)) // EB)
        blk_ids = jnp.arange(n_blocks)
        blk_e = jnp.minimum(jnp.sum(blk_end[None, :] <= blk_ids[:, None], axis=1), N_EXPERTS - 1).astype(jnp.int32)
        n_act = blk_end[-1:].astype(jnp.int32)
        is_last = jnp.any((blk_end[None, :] == blk_ids[:, None] + 1) & (nblk_e[None, :] > 0), axis=1)
        zero_blocks = jnp.where(blk_ids >= n_act[0], 2, jnp.where(is_last, 1, 0)).astype(jnp.int32)
        ti = tinfo.reshape(nt, 8, LANES)
        run_len = ti[:, 0, :N_EXPERTS].astype(jnp.int32)
        run_start = pstart[None, :] + ti[:, 1, :N_EXPERTS].astype(jnp.int32)
        run_groups = (run_len + ROW_ALIGN - 1) // ROW_ALIGN
        assert n_blocks * EB * SORTED_RANGE < 2 ** 31
        runs = _chunk_table(run_start, run_groups)

        xs = _dispatch(zero_blocks, runs, hb, rt, n_blocks * EB)
        own = jnp.where(nblk_e > 0, jnp.arange(N_EXPERTS), N_EXPERTS)
        later = jnp.concatenate([lax.cummin(own[::-1])[::-1][1:], jnp.full((1,), N_EXPERTS)])
        nxt_e = jnp.where(later < N_EXPERTS, later, -1).astype(jnp.int32)
        ys = _experts(blk_e, n_act, nxt_e, xs, w_gate[layer], w_up[layer], w_down[layer])
        h = _combine(runs, ys, h1, rt)
    return h.reshape(batch, seq, d)
```

```python
import functools
import math

import numpy as np
import jax
import jax.numpy as jnp
from jax import lax
from jax.experimental import pallas as pl
from jax.experimental.pallas import tpu as pltpu

F32 = jnp.float32
BF16 = jnp.bfloat16

HEAD_DIM = 64
N_DIFF_HEADS = 4
N_SWA_HEADS = 8
N_SWA_KV = 2
BLOCK = 128
N_META = 16
N_BUCKETS = 32
MAX_DISTANCE = 128
N_GROUPS = 4
EXPERTS_PER_GROUP = 8
N_EXPERTS = N_GROUPS * EXPERTS_PER_GROUP
EPS = 1e-6
NEG = -1e30
LOG2E = math.log2(math.e)

LANES = 128
MXU_DIM = 256
V7X_VMEM_BYTES = 64 * 1024 * 1024
VMEM_LIMIT = V7X_VMEM_BYTES * 3 // 4

TP = 1024
TM = 512
TQ = 256
ONES_ROWS = 16
EB = 512
ROW_ALIGN = 8
CHUNK_ROWS = (32, 16, 8)
SROWS = -(-(2 * TM + N_EXPERTS * (ROW_ALIGN - 1)) // MXU_DIM) * MXU_DIM
SORTED_RANGE = 2048
CHUNK_SLOTS = (SROWS // CHUNK_ROWS[0],) + (N_EXPERTS,) * (len(CHUNK_ROWS) - 1)
CHUNK_OFFSETS = tuple(sum(CHUNK_SLOTS[:k]) for k in range(len(CHUNK_ROWS)))
CHUNK_COUNTS = sum(CHUNK_SLOTS)
assert SROWS <= SORTED_RANGE and CHUNK_COUNTS + len(CHUNK_ROWS) <= LANES

C_DQ, C_DK, C_DV, C_SQ, C_SK, C_SV, C_END = 0, 512, 1024, 1536, 2048, 2176, 2304
NORM_GROUPS = (0, 1, 2, 3, 6, 7, 8)


def _cparams(sem):
    return pltpu.CompilerParams(dimension_semantics=sem, vmem_limit_bytes=VMEM_LIMIT)


def _t5_bucket_np(dist):
    n = np.maximum(dist, 0)
    max_exact = N_BUCKETS // 2
    nf = np.maximum(n, 1).astype(np.float32)
    large = max_exact + (np.log(nf / np.float32(max_exact)) / np.float32(math.log(MAX_DISTANCE / max_exact))
                         * np.float32(N_BUCKETS - max_exact)).astype(np.int32)
    large = np.minimum(large, N_BUCKETS - 1)
    return np.where(n < max_exact, n, large)


def _bias_tables(rel_bias, tq):
    nd = 2 * BLOCK
    buckets = _t5_bucket_np(np.arange(nd))
    assert (buckets[MAX_DISTANCE:] == N_BUCKETS - 1).all()
    rb = rel_bias.astype(F32)
    r = np.arange(BLOCK)[:, None]
    c = np.arange(BLOCK)[None, :]
    d_own = r - c
    d_prev = BLOCK + r - c
    far = rb[N_BUCKETS - 1]

    def take(dist, heads):
        idx = jnp.asarray(buckets[np.clip(dist, 0, nd - 1)], jnp.int32)[None]
        out = jnp.zeros((heads.stop - heads.start,) + dist.shape, F32)
        for b in range(N_BUCKETS):
            out = jnp.where(idx == b, rb[b, heads].reshape((-1,) + (1,) * dist.ndim), out)
        return out

    hd = slice(0, N_DIFF_HEADS)
    far_d = far[hd][:, None, None]
    d0 = jnp.where(d_own[None] >= 0, take(d_own, hd) - far_d, NEG)
    d1 = take(d_prev, hd) - far_d
    dblk = jnp.stack([d0, d1], axis=1)
    rq = np.arange(tq)[:, None]
    cm = np.arange(LANES)[None, :]
    d_meta = (N_META + rq - cm)[:, :N_META]
    bm0 = jnp.pad(take(d_meta, hd) - far_d, ((0, 0), (0, 0), (0, LANES - N_META)), constant_values=NEG)

    hs = slice(N_DIFF_HEADS, N_DIFF_HEADS + N_SWA_HEADS)
    far_s = far[hs][:, None, None]
    d_meta_s = N_META + r - cm
    meta_first = jnp.where((cm < N_META)[None], take(d_meta_s, hs), NEG)
    meta_rest = jnp.where((cm < N_META)[None], jnp.broadcast_to(far_s, (N_SWA_HEADS, BLOCK, LANES)), NEG)
    prev_rest = jnp.where((c > r)[None], take(d_prev, hs), NEG)
    prev_first = jnp.full((N_SWA_HEADS, BLOCK, BLOCK), NEG, F32)
    own = jnp.where((d_own >= 0)[None], take(d_own, hs), NEG)
    bt = jnp.stack([jnp.concatenate([prev_first, own, meta_first[..., :N_META]], axis=-1),
                    jnp.concatenate([prev_rest, own, meta_rest[..., :N_META]], axis=-1)], axis=0)
    return dblk.astype(F32), bm0.astype(F32), bt.astype(F32)


def _proj_kernel(x_ref, g1_ref, w_ref, bd_ref, gain_ref, nmask_ref, o_ref):
    x = x_ref[...]
    a = x * lax.rsqrt(jnp.mean(x * x, axis=-1, keepdims=True) + EPS) * g1_ref[...]
    p = jnp.dot(a.astype(BF16), w_ref[...], preferred_element_type=F32)
    bd = bd_ref[...]
    for j in range(C_END // MXU_DIM):
        sl = slice(j * MXU_DIM, (j + 1) * MXU_DIM)
        pj = p[:, sl]
        if j in NORM_GROUPS:
            ms = jnp.dot((pj * pj).astype(BF16), bd, preferred_element_type=F32)
            pj = jnp.where(nmask_ref[:, sl] != 0.0, pj * lax.rsqrt(ms + EPS) * gain_ref[:, sl], pj)
        o_ref[:, sl] = pj.astype(BF16)


def _proj(x2, g1, w, bd, gain, nmask, tm):
    n = x2.shape[0]
    return pl.pallas_call(
        _proj_kernel,
        out_shape=jax.ShapeDtypeStruct((n, C_END), BF16),
        grid=(n // tm,),
        in_specs=[
            pl.BlockSpec((tm, x2.shape[1]), lambda i: (i, 0)),
            pl.BlockSpec(g1.shape, lambda i: (0, 0)),
            pl.BlockSpec(w.shape, lambda i: (0, 0)),
            pl.BlockSpec(bd.shape, lambda i: (0, 0)),
            pl.BlockSpec(gain.shape, lambda i: (0, 0)),
            pl.BlockSpec(nmask.shape, lambda i: (0, 0)),
        ],
        out_specs=pl.BlockSpec((tm, C_END), lambda i: (i, 0)),
        compiler_params=_cparams(("parallel",)),
        name="proj",
    )(x2, g1, w, bd, gain, nmask)


def _diff_kernel(qi_tab, t_tab, q_ref, k_ref, v_ref, km_ref, vm_ref, d_ref, bm0_ref, lamv_ref, gain_ref, o_ref,
                 bias_ref, mb_ref, qs_ref, kt_ref, vt_ref, s_buf, p_buf, a_buf, m_ref, acc_ref, *,
                 lambda_init, n_steps, n_far, n_near):
    tq = TQ
    nq = q_ref.shape[0] // tq
    nb = tq // BLOCK
    BIAS_LEFT, BIAS_DIAG, BIAS_NONE = 0, 1, 2

    d0 = d_ref[0, 0] * LOG2E
    d1 = d_ref[0, 1] * LOG2E
    zeros = jnp.zeros((BLOCK, BLOCK), F32)
    for a in range(nb):
        for b in range(nb):
            rs, cs = slice(a * BLOCK, (a + 1) * BLOCK), slice(b * BLOCK, (b + 1) * BLOCK)
            if a == b:
                blk = d0
            elif b == a + 1:
                blk = d1
            elif b > a:
                blk = zeros
            else:
                blk = jnp.full((BLOCK, BLOCK), NEG, F32)
            bias_ref[BIAS_DIAG, rs, cs] = blk
            bias_ref[BIAS_LEFT, rs, cs] = d1 if (b == 0 and a == nb - 1) else zeros
    bias_ref[BIAS_NONE] = jnp.zeros((tq, tq), F32)
    mb_ref[0] = jnp.zeros((N_META, tq), F32)
    mb_ref[1] = bm0_ref[0, :N_META, :] * LOG2E

    lane = lax.broadcasted_iota(jnp.int32, (tq, LANES), 1)
    for i in range(nq):
        rows = slice(i * tq, (i + 1) * tq)
        q = q_ref[rows, :].astype(F32)
        qs_ref[i] = jnp.transpose(jnp.concatenate([jnp.where(lane < HEAD_DIM, q, 0.0),
                                                   jnp.where(lane >= HEAD_DIM, q, 0.0)], axis=0)).astype(BF16)
        vt_ref[i, :LANES, :] = jnp.transpose(v_ref[rows, :].astype(F32)).astype(BF16)
        kt_ref[i] = k_ref[rows, :]
    vt_ref[nq, :LANES, :] = jnp.transpose(vm_ref[...].astype(F32)).astype(BF16)
    kt_ref[nq] = km_ref[...]
    vt_ref[:, LANES:, :] = jnp.ones((nq + 1, ONES_ROWS, tq), BF16)
    acc_ref[...] = jnp.zeros(acc_ref.shape, F32)
    m_ref[...] = jnp.full(m_ref.shape, NEG, F32)
    lv = lamv_ref[...]
    lam = (jnp.exp(jnp.sum(lv[0:1] * lv[1:2], axis=-1, keepdims=True))
           - jnp.exp(jnp.sum(lv[2:3] * lv[3:4], axis=-1, keepdims=True)) + lambda_init)

    FAR, NEAR, META = 0, 1, 2

    def stage_a(n, slot, kind):
        qi, t = qi_tab[n], t_tab[n]
        if kind == META:
            s = jnp.dot(kt_ref[nq, :N_META, :], qs_ref[qi], preferred_element_type=F32)
            s_buf[slot, :N_META] = s + jnp.tile(mb_ref[jnp.where(qi == 0, 1, 0)], (1, 2))
            return
        s = jnp.dot(kt_ref[t - 1], qs_ref[qi], preferred_element_type=F32)
        if kind == NEAR:
            which = jnp.where(t == qi + 1, BIAS_DIAG, jnp.where(t == qi, BIAS_LEFT, BIAS_NONE))
            s = s + jnp.tile(bias_ref[which], (1, 2))
        s_buf[slot] = s

    def stage_b(n, slot, kind):
        qi = qi_tab[n]
        rows = slice(0, N_META if kind == META else tq)
        s = s_buf[slot, rows]
        m_prev = m_ref[qi]
        m_new = jnp.maximum(m_prev, jnp.max(s, axis=0, keepdims=True))
        a_buf[slot] = jnp.exp2(m_prev - m_new)
        p_buf[slot, rows] = jnp.exp2(s - m_new[0:1]).astype(BF16)
        m_ref[qi] = m_new

    def stage_c(n, slot, kind):
        qi, t = qi_tab[n], t_tab[n]
        if kind == META:
            pv = jnp.dot(vt_ref[nq, :, :N_META], p_buf[slot, :N_META], preferred_element_type=F32)
        else:
            pv = jnp.dot(vt_ref[t - 1], p_buf[slot], preferred_element_type=F32)
        acc_ref[qi] = a_buf[slot][0:1] * acc_ref[qi] + pv

    LEAD, SLOTS, UNROLL = 2, 3, 12
    assert UNROLL % SLOTS == 0

    def pipeline(base, count, kind, biased_from=None):
        def kind_a(j):
            return kind if biased_from is None or j < biased_from else NEAR

        if count <= 2 * LEAD:
            for j in range(count):
                stage_a(base + j, 0, kind_a(j))
                stage_b(base + j, 0, kind)
                stage_c(base + j, 0, kind)
            return
        for j in range(2 * LEAD):
            stage_a(base + j, j % SLOTS, kind_a(j))
            if j >= LEAD:
                stage_b(base + j - LEAD, (j - LEAD) % SLOTS, kind)

        def steps(n, first, count):
            for j in range(count):
                stage_a(base + n + j + 2 * LEAD, (j + 2 * LEAD) % SLOTS, kind_a(first + j + 2 * LEAD))
                stage_b(base + n + j + LEAD, (j + LEAD) % SLOTS, kind)
                stage_c(base + n + j, j % SLOTS, kind)

        n_steady = count - 2 * LEAD
        n_blocks = n_steady // UNROLL
        switch = n_blocks if biased_from is None else (biased_from - 2 * LEAD) // UNROLL
        assert biased_from is None or (biased_from - 2 * LEAD) % UNROLL == 0
        for lo, hi in ((0, min(switch, n_blocks)), (min(switch, n_blocks), n_blocks)):
            lax.fori_loop(lo, hi, lambda k, carry, lo=lo: (steps(UNROLL * k, UNROLL * lo, UNROLL), carry)[1], 0)
        steps(UNROLL * n_blocks, UNROLL * n_blocks, n_steady - UNROLL * n_blocks)
        for j in range(n_steady, count):
            if j + LEAD < count:
                stage_b(base + j + LEAD, (j + LEAD) % SLOTS, kind)
            stage_c(base + j, j % SLOTS, kind)

    if n_far >= 2 * LEAD:
        pipeline(0, n_far + n_near, FAR, 2 * LEAD + UNROLL * ((n_far - 2 * LEAD) // UNROLL))
    else:
        pipeline(0, n_far + n_near, NEAR)
    pipeline(n_far + n_near, n_steps - n_far - n_near, META)

    for i in range(nq):
        acc = acc_ref[i]
        o = acc[:LANES] * (1.0 / acc[LANES:LANES + 1])
        d = o[:, :tq] - lam * o[:, tq:]
        y = d * lax.rsqrt(jnp.mean(d * d, axis=0, keepdims=True) + EPS) * jnp.tile(gain_ref[...], (1, tq // LANES))
        o_ref[i * tq:(i + 1) * tq, :] = jnp.transpose(y * (1.0 - lambda_init)).astype(BF16)


def _diff_attention(qkv, km, vm, dblk, bm0, lamv, gain, batch, seq, lambda_init):
    nq = seq // TQ
    far = [(qi, t) for qi in range(nq) for t in range(1, qi)]
    near = [(qi, t) for qi in range(nq) for t in (qi, qi + 1) if t >= 1]
    meta = [(qi, 0) for qi in range(nq)]
    steps = far + near + meta
    qi_tab = jnp.asarray([s[0] for s in steps], jnp.int32)
    t_tab = jnp.asarray([s[1] for s in steps], jnp.int32)
    kern = functools.partial(_diff_kernel, lambda_init=lambda_init, n_steps=len(steps), n_far=len(far),
                             n_near=len(near))
    return pl.pallas_call(
        kern,
        out_shape=jax.ShapeDtypeStruct((batch * seq, N_DIFF_HEADS * LANES), BF16),
        grid_spec=pltpu.PrefetchScalarGridSpec(
            num_scalar_prefetch=2,
            grid=(batch, N_DIFF_HEADS),
            in_specs=[
                pl.BlockSpec((seq, LANES), lambda b, h, *_: (b, C_DQ // LANES + h)),
                pl.BlockSpec((seq, LANES), lambda b, h, *_: (b, C_DK // LANES + h)),
                pl.BlockSpec((seq, LANES), lambda b, h, *_: (b, C_DV // LANES + h)),
                pl.BlockSpec((TQ, LANES), lambda b, h, *_: (0, h)),
                pl.BlockSpec((TQ, LANES), lambda b, h, *_: (0, h)),
                pl.BlockSpec((1, 2, BLOCK, BLOCK), lambda b, h, *_: (h, 0, 0, 0)),
                pl.BlockSpec((1, LANES, TQ), lambda b, h, *_: (h, 0, 0)),
                pl.BlockSpec(lamv.shape, lambda b, h, *_: (0, 0)),
                pl.BlockSpec((LANES, LANES), lambda b, h, *_: (0, 0)),
            ],
            out_specs=pl.BlockSpec((seq, LANES), lambda b, h, *_: (b, h)),
            scratch_shapes=[
                pltpu.VMEM((3, TQ, TQ), F32),
                pltpu.VMEM((2, N_META, TQ), F32),
                pltpu.VMEM((nq, LANES, 2 * TQ), BF16),
                pltpu.VMEM((nq + 1, TQ, LANES), BF16),
                pltpu.VMEM((nq + 1, LANES + ONES_ROWS, TQ), BF16),
                pltpu.VMEM((3, TQ, 2 * TQ), F32),
                pltpu.VMEM((3, TQ, 2 * TQ), BF16),
                pltpu.VMEM((3, 8, 2 * TQ), F32),
                pltpu.VMEM((nq, 8, 2 * TQ), F32),
                pltpu.VMEM((nq, LANES + ONES_ROWS, 2 * TQ), F32),
            ],
        ),
        compiler_params=_cparams(("parallel", "parallel")),
        name="diff_attention",
    )(qi_tab, t_tab, qkv, qkv, qkv, km, vm, jnp.swapaxes(dblk, -1, -2), jnp.swapaxes(bm0, -1, -2), lamv,
      jnp.broadcast_to(gain.reshape(LANES, 1), (LANES, LANES)))


def _swa_kernel(sink_ref, q_ref, k_ref, v_ref, km_ref, vm_ref, bt_ref, o_ref, kd_ref, vt_ref,
                s_scr, p_scr, inv_scr):
    nkb = k_ref.shape[0] // BLOCK
    lane = lax.broadcasted_iota(jnp.int32, (BLOCK, LANES), 1)
    pairs = [(g, u) for g in range(N_SWA_KV) for u in range(2)]

    def both_halves(k):
        k0, k1 = k[:, :HEAD_DIM], k[:, HEAD_DIM:]
        return jnp.concatenate([k0, k0, k1, k1], axis=1)

    def prepare(j, carry):
        rows = pl.ds(pl.multiple_of(j * BLOCK, BLOCK), BLOCK)
        kd_ref[j] = both_halves(k_ref[rows, :])
        vt_ref[j] = jnp.transpose(v_ref[rows, :].astype(F32)).astype(BF16)
        return carry
    lax.fori_loop(0, nkb, prepare, 0)
    kd_ref[nkb] = both_halves(km_ref[...])
    vt_ref[nkb] = jnp.transpose(vm_ref[...].astype(F32)).astype(BF16)

    def scores(n, slot):
        first = jnp.where(n == 0, 0, 1)
        prev = jnp.maximum(n - 1, 0)
        r_q = pl.multiple_of(n * BLOCK, BLOCK)
        for c, (g, u) in enumerate(pairs):
            ks = slice(g * LANES, (g + 1) * LANES)
            kcat = jnp.concatenate([kd_ref[prev, :, ks], kd_ref[n, :, ks], kd_ref[nkb, :N_META, ks]], axis=0)
            h0 = 4 * g + 2 * u
            qp = q_ref[pl.ds(r_q, BLOCK), (2 * g + u) * LANES:(2 * g + u + 1) * LANES].astype(F32)
            qs = jnp.transpose(jnp.concatenate([jnp.where(lane < HEAD_DIM, qp, 0.0),
                                                jnp.where(lane >= HEAD_DIM, qp, 0.0)], axis=0)).astype(BF16)
            s = jnp.dot(kcat, qs, preferred_element_type=F32)
            s_scr[slot, c] = s + jnp.concatenate([bt_ref[first, h0], bt_ref[first, h0 + 1]], axis=1)

    def exponentials(slot):
        for c, (g, u) in enumerate(pairs):
            h0 = 4 * g + 2 * u
            s = s_scr[slot, c]
            sink = jnp.concatenate([sink_ref[h0:h0 + 1, :], sink_ref[h0 + 1:h0 + 2, :]], axis=1)
            m = jnp.maximum(jnp.max(s, axis=0, keepdims=True), sink)
            p = jnp.exp(s - m)
            p_scr[slot, c] = p.astype(BF16)
            inv_scr[slot, c] = jnp.broadcast_to(1.0 / (jnp.sum(p, axis=0, keepdims=True) + jnp.exp(sink - m)),
                                                inv_scr.shape[2:])

    def values(n, slot):
        prev = jnp.maximum(n - 1, 0)
        r_q = pl.multiple_of(n * BLOCK, BLOCK)
        for c, (g, u) in enumerate(pairs):
            vs = slice(g * HEAD_DIM, (g + 1) * HEAD_DIM)
            vcat = jnp.concatenate([vt_ref[prev, vs, :], vt_ref[n, vs, :]], axis=1)
            o = (jnp.dot(vcat, p_scr[slot, c, :2 * BLOCK], preferred_element_type=F32)
                 + jnp.dot(vt_ref[nkb, vs, :N_META], p_scr[slot, c, 2 * BLOCK:], preferred_element_type=F32)
                 ) * inv_scr[slot, c][0:1]
            ot = jnp.transpose(o)
            o_ref[pl.ds(r_q, BLOCK), (2 * g + u) * LANES:(2 * g + u + 1) * LANES] = (
                jnp.concatenate([ot[:BLOCK], ot[BLOCK:]], axis=1).astype(BF16))

    scores(0, 0)
    scores(1, 1)
    exponentials(0)

    def blocks(n, count):
        for j in range(count):
            scores(n + j + 2, j % 2)
            exponentials((j + 1) % 2)
            values(n + j, j % 2)

    unroll = 4
    n_steady = nkb - 2
    lax.fori_loop(0, n_steady // unroll, lambda k, carry: (blocks(unroll * k, unroll), carry)[1], 0)
    blocks(n_steady // unroll * unroll, n_steady % unroll)
    exponentials(1)
    values(nkb - 2, 0)
    values(nkb - 1, 1)


def _swa_attention(sinks, qkv, km, vm, bt, batch, seq):
    nkb = seq // BLOCK
    swa_q = N_SWA_HEADS * HEAD_DIM
    assert nkb % 2 == 0
    sinkv =jnp.broadcast_to(sinks.reshape(N_SWA_HEADS, 1), (N_SWA_HEADS, LANES))
    return pl.pallas_call(
        _swa_kernel,
        out_shape=jax.ShapeDtypeStruct((batch * seq, N_SWA_HEADS * HEAD_DIM), BF16),
        grid=(batch,),
        in_specs=[
            pl.BlockSpec(sinkv.shape, lambda b: (0, 0)),
            pl.BlockSpec((seq, swa_q), lambda b: (b, C_SQ // swa_q)),
            pl.BlockSpec((seq, LANES), lambda b: (b, C_SK // LANES)),
            pl.BlockSpec((seq, LANES), lambda b: (b, C_SV // LANES)),
            pl.BlockSpec(km.shape, lambda b: (0, 0)),
            pl.BlockSpec(vm.shape, lambda b: (0, 0)),
            pl.BlockSpec(bt.shape, lambda b: (0, 0, 0, 0)),
        ],
        out_specs=pl.BlockSpec((seq, swa_q), lambda b: (b, 0)),
        scratch_shapes=[pltpu.VMEM((nkb + 1, BLOCK, 2 * LANES), BF16),
                        pltpu.VMEM((nkb + 1, LANES, BLOCK), BF16),
                        pltpu.VMEM((2, 4, 2 * BLOCK + N_META, 2 * BLOCK), F32),
                        pltpu.VMEM((2, 4, 2 * BLOCK + N_META, 2 * BLOCK), BF16),
                        pltpu.VMEM((2, 4, 8, 2 * BLOCK), F32)],
        compiler_params=_cparams(("parallel",)),
        name="swa_attention",
    )(sinkv, qkv, qkv, qkv, km, vm, bt)


def _outproj_kernel(x_ref, md_ref, ms_ref, wo_ref, g2_ref, wr_ref, br_ref,
                    h_ref, hb_ref, rt_ref, ti_ref, cnt_ref, c_ref, lg_ref):
    i = pl.program_id(0)

    @pl.when(i == 0)
    def _init():
        c_ref[...] = jnp.zeros(c_ref.shape, F32)
        lg_ref[...] = jnp.zeros(lg_ref.shape, F32)

    lg_prev = lg_ref[...]
    half = md_ref.shape[1]
    h = (x_ref[...]
         + jnp.dot(md_ref[...], wo_ref[:half, :], preferred_element_type=F32)
         + jnp.dot(ms_ref[...], wo_ref[half:, :], preferred_element_type=F32))
    h_ref[...] = h
    hn = h * lax.rsqrt(jnp.mean(h * h, axis=-1, keepdims=True) + EPS) * g2_ref[...]
    hb = hn.astype(BF16)
    hb_ref[...] = hb
    lg_ref[...] = jnp.dot(hb, wr_ref[...], preferred_element_type=F32) + br_ref[...]
    _route_tile(lg_prev, jnp.where(i > 0, 1.0, 0.0), rt_ref, ti_ref, c_ref)

    @pl.when(i == pl.num_programs(0) - 1)
    def _fin():
        cnt_ref[...] = c_ref[...]


def _route_tile(lg, live, rt_ref, ti_ref, c_ref):
    tm = lg.shape[0]
    lane_i = lax.broadcasted_iota(jnp.int32, lg.shape, 1)
    lane = lane_i.astype(F32)
    big = float(4 * LANES)
    is_g = (lane_i >= N_EXPERTS) & (lane_i < N_EXPERTS + N_GROUPS)
    glm = jnp.where(is_g, lg, -jnp.inf)
    gmax = jnp.max(glm, axis=1, keepdims=True)
    gidx = jnp.min(jnp.where(glm == gmax, lane, big), axis=1, keepdims=True) - N_EXPERTS
    gsum = jnp.sum(jnp.where(is_g, jnp.exp(lg - gmax), 0.0), axis=1, keepdims=True)
    g_w = 1.0 / gsum
    lane_grp = (lane_i >> 3).astype(F32)
    in_grp = (lane_i < N_EXPERTS) & (lane_grp == gidx)
    el = jnp.where(in_grp, lg, -jnp.inf)
    t1 = jnp.max(el, axis=1, keepdims=True)
    j1 = jnp.min(jnp.where(el == t1, lane, big), axis=1, keepdims=True)
    el2 = jnp.where(lane == j1, -jnp.inf, el)
    t2 = jnp.max(el2, axis=1, keepdims=True)
    j2 = jnp.min(jnp.where(el2 == t2, lane, big), axis=1, keepdims=True)
    e2 = jnp.exp(t2 - t1)
    den = 1.0 + e2
    gate1 = g_w / den
    gate2 = g_w * e2 / den

    o1 = lane == j1
    o2 = lane == j2
    onehot = jnp.where(o1 | o2, 1.0, 0.0).astype(BF16)
    rr = lax.broadcasted_iota(jnp.int32, (tm, tm), 0)
    cc = lax.broadcasted_iota(jnp.int32, (tm, tm), 1)
    lower = jnp.where(rr > cc, 1.0, 0.0).astype(BF16)
    pfx = jnp.dot(lower, onehot, preferred_element_type=F32)
    cnt_tile = jnp.sum(onehot.astype(F32), axis=0, keepdims=True)
    groups = jnp.floor((cnt_tile + (ROW_ALIGN - 1)) * (1.0 / ROW_ALIGN))
    er = lax.broadcasted_iota(jnp.int32, (LANES, LANES), 0)
    ec = lax.broadcasted_iota(jnp.int32, (LANES, LANES), 1)
    before = jnp.where(er < ec, 1.0, 0.0).astype(BF16)
    cbase = ROW_ALIGN * jnp.dot(jnp.broadcast_to(groups, (8, LANES)).astype(BF16), before,
                                preferred_element_type=F32)[0:1]
    at = pfx + cbase
    pos1 = jnp.sum(jnp.where(o1, at, 0.0), axis=1, keepdims=True)
    pos2 = jnp.sum(jnp.where(o2, at, 0.0), axis=1, keepdims=True)
    rt_ref[...] = jnp.where(lane_i == 0, gate1,
                            jnp.where(lane_i == 1, gate2,
                                      jnp.where(lane_i == 2, pos1,
                                                jnp.where(lane_i == 3, pos2, 0.0))))
    c_old = c_ref[...]
    c_ref[...] = c_old + groups * (ROW_ALIGN * live)
    row8 = lax.broadcasted_iota(jnp.int32, (8, LANES), 0)
    ti_ref[...] = jnp.where(row8 == 0, cnt_tile, jnp.where(row8 == 1, c_old, 0.0))


def _outproj(x2, mixd, mixs, wo, g2, wr, br):
    n, d = x2.shape
    nt = n // TM

    def proj_tile(i):
        return (jnp.minimum(i, nt - 1), 0)

    def route_tile(i):
        return (jnp.maximum(i - 1, 0), 0)

    return pl.pallas_call(
        _outproj_kernel,
        out_shape=(jax.ShapeDtypeStruct((n, d), F32),
                   jax.ShapeDtypeStruct((n, d), BF16),
                   jax.ShapeDtypeStruct((n, LANES), F32),
                   jax.ShapeDtypeStruct((nt * 8, LANES), F32),
                   jax.ShapeDtypeStruct((8, LANES), F32)),
        grid=(nt + 1,),
        in_specs=[
            pl.BlockSpec((TM, d), proj_tile),
            pl.BlockSpec((TM, mixd.shape[1]), proj_tile),
            pl.BlockSpec((TM, mixs.shape[1]), proj_tile),
            pl.BlockSpec(wo.shape, lambda i: (0, 0)),
            pl.BlockSpec(g2.shape, lambda i: (0, 0)),
            pl.BlockSpec(wr.shape, lambda i: (0, 0)),
            pl.BlockSpec(br.shape, lambda i: (0, 0)),
        ],
        out_specs=(pl.BlockSpec((TM, d), proj_tile),
                   pl.BlockSpec((TM, d), proj_tile),
                   pl.BlockSpec((TM, LANES), route_tile),
                   pl.BlockSpec((8, LANES), route_tile),
                   pl.BlockSpec((8, LANES), lambda i: (0, 0))),
        scratch_shapes=[pltpu.VMEM((8, LANES), F32), pltpu.VMEM((TM, LANES), F32)],
        compiler_params=_cparams(("arbitrary",)),
        name="outproj_router",
    )(x2, mixd, mixs, wo, g2, wr, br)


def _chunk_table(run_start, run_groups):
    nt = run_start.shape[0]
    sorted_start = (jnp.cumsum(run_groups, axis=1) - run_groups) * ROW_ALIGN
    word = run_start * SORTED_RANGE + sorted_start
    done = jnp.zeros_like(run_groups)
    segments, counts = [], []
    for k, rows in enumerate(CHUNK_ROWS):
        per_run = run_groups // (rows // ROW_ALIGN)
        if k > 0:
            per_run = per_run % 2
        last = jnp.cumsum(per_run, axis=1)[:, None, :]
        first = last - per_run[:, None, :]
        slot = jnp.arange(CHUNK_SLOTS[k], dtype=jnp.int32)[None, :, None]
        value = word[:, None, :] + (done[:, None, :] + (slot - first) * rows) * (SORTED_RANGE + 1)
        segments.append(jnp.sum(jnp.where((first <= slot) & (slot < last), value, 0), axis=2))
        counts.append(last[:, 0, -1:])
        done = done + per_run * rows
    pad = jnp.zeros((nt, LANES - CHUNK_COUNTS - len(CHUNK_ROWS)), jnp.int32)
    return jnp.concatenate(segments + counts + [pad], axis=1).astype(jnp.int32).reshape(nt, 1, LANES)


def _for_each_chunk(chunks_ref, fn):
    def issue(k, c, priority):
        word = chunks_ref[0, 0, CHUNK_OFFSETS[k] + c]
        fn(pl.multiple_of(word >> (SORTED_RANGE.bit_length() - 1), ROW_ALIGN),
           pl.multiple_of(word & (SORTED_RANGE - 1), ROW_ALIGN), CHUNK_ROWS[k], priority)

    for k in range(len(CHUNK_ROWS)):
        _in_pairs(chunks_ref[0, 0, CHUNK_COUNTS + k], lambda c, second, k=k: issue(k, c, second))


def _in_pairs(count, fn):
    half = count // 2

    def body(c, carry):
        fn(c, 0)
        fn(half + c, 1)
        return carry
    lax.fori_loop(0, half, body, 0)

    @pl.when(count % 2 == 1)
    def _():
        fn(count - 1, 0)


def _wait_chunks(chunks_ref, make_copy):
    for k, rows in enumerate(CHUNK_ROWS):
        _in_pairs(chunks_ref[0, 0, CHUNK_COUNTS + k], lambda c, second, rows=rows: make_copy(rows).wait())


def _dispatch_kernel(zf_ref, cur_ref, prv_ref, hb_ref, rt_ref, xs_ref, sbuf, zbuf, sem, zsem):
    i = pl.program_id(0)
    nt = pl.num_programs(0)
    slot = i % 2
    tm, d = hb_ref.shape

    def for_zero_blocks(kind, fn):
        def body(b, carry):
            @pl.when(zf_ref[b] == kind)
            def _():
                fn(pltpu.make_async_copy(zbuf, xs_ref.at[pl.ds(pl.multiple_of(b * EB, EB), EB)],
                                         zsem.at[kind - 1]))
            return carry
        lax.fori_loop(0, zf_ref.shape[0], body, 0)

    @pl.when(i == 0)
    def _():
        zbuf[...] = jnp.zeros(zbuf.shape, zbuf.dtype)
        for_zero_blocks(1, lambda c: c.start())
        for_zero_blocks(2, lambda c: c.start())
        for_zero_blocks(1, lambda c: c.wait())

    pos_t = jnp.transpose(rt_ref[...])
    srow = lax.broadcasted_iota(jnp.int32, (SROWS, tm), 0).astype(F32)
    sel = jnp.where(srow == pos_t[2:3, :], 1.0, jnp.where(srow == pos_t[3:4, :], 1.0, 0.0)).astype(BF16)
    srt = jnp.dot(sel, hb_ref[...], preferred_element_type=F32)
    bits = pltpu.bitcast(srt, jnp.uint32)
    sbuf[slot] = (bits[:, d // 2:] & jnp.uint32(0xFFFF0000)) | (bits[:, :d // 2] >> 16)

    def chunk_copy(run_row, sorted_row, rows, sl):
        return pltpu.make_async_copy(sbuf.at[sl, pl.ds(sorted_row, rows)], xs_ref.at[pl.ds(run_row, rows)],
                                     sem.at[sl])

    _for_each_chunk(cur_ref, lambda run_row, sorted_row, rows, priority:
                    chunk_copy(run_row, sorted_row, rows, slot).start(priority=priority))

    @pl.when(i > 0)
    def _():
        _wait_chunks(prv_ref, lambda rows: chunk_copy(0, 0, rows, 1 - slot))

    @pl.when(i == nt - 1)
    def _():
        _wait_chunks(cur_ref, lambda rows: chunk_copy(0, 0, rows, slot))
        for_zero_blocks(2, lambda c: c.wait())


def _dispatch(zero_blocks, runs, hb, rt, n_rows):
    n, d = hb.shape
    return pl.pallas_call(
        _dispatch_kernel,
        out_shape=jax.ShapeDtypeStruct((n_rows, d // 2), jnp.uint32),
        grid_spec=pltpu.PrefetchScalarGridSpec(
            num_scalar_prefetch=1,
            grid=(n // TM,),
            in_specs=[
                pl.BlockSpec((1, 1, LANES), lambda i, zf: (i, 0, 0), memory_space=pltpu.SMEM),
                pl.BlockSpec((1, 1, LANES), lambda i, zf: (jnp.maximum(i - 1, 0), 0, 0), memory_space=pltpu.SMEM),
                pl.BlockSpec((TM, d), lambda i, zf: (i, 0)),
                pl.BlockSpec((TM, LANES), lambda i, zf: (i, 0)),
            ],
            out_specs=pl.BlockSpec(memory_space=pl.ANY),
            scratch_shapes=[pltpu.VMEM((2, SROWS, d // 2), jnp.uint32), pltpu.VMEM((EB, d // 2), jnp.uint32),
                            pltpu.SemaphoreType.DMA((2,)), pltpu.SemaphoreType.DMA((2,))],
        ),
        compiler_params=_cparams(("arbitrary",)),
        name="dispatch",
    )(zero_blocks, runs, runs, hb, rt)


def _experts_kernel(be_ref, na_ref, nxt_ref, xs_ref, wg_hbm, wu_hbm, wd_hbm, ys_ref,
                    wgf, wuf, wdf, wgb, wub, wdb, sem):
    b = pl.program_id(0)

    def weight_copies(e):
        return (pltpu.make_async_copy(wg_hbm.at[e], wgf, sem.at[0]),
                pltpu.make_async_copy(wu_hbm.at[e], wuf, sem.at[1]),
                pltpu.make_async_copy(wd_hbm.at[e], wdf, sem.at[2]))

    @pl.when(b == 0)
    def _():
        for c in weight_copies(be_ref[0]):
            c.start()

    @pl.when(b < na_ref[0])
    def _():
        e = be_ref[b]
        changed = jnp.logical_or(b == 0, be_ref[jnp.maximum(b - 1, 0)] != e)

        @pl.when(changed)
        def _load():
            for c in weight_copies(e):
                c.wait()
            wgb[...] = wgf[...].astype(BF16)
            wub[...] = wuf[...].astype(BF16)
            wdb[...] = wdf[...].astype(BF16)
            nxt = nxt_ref[e]

            @pl.when(nxt >= 0)
            def _():
                for c in weight_copies(nxt):
                    c.start()

        w = xs_ref[...]
        x_lo = pltpu.bitcast(w << 16, F32).astype(BF16)
        x_hi = pltpu.bitcast(w & jnp.uint32(0xFFFF0000), F32).astype(BF16)
        dh = w.shape[1]
        g = (jnp.dot(x_lo, wgb[:dh, :], preferred_element_type=F32)
             + jnp.dot(x_hi, wgb[dh:, :], preferred_element_type=F32))
        u = (jnp.dot(x_lo, wub[:dh, :], preferred_element_type=F32)
             + jnp.dot(x_hi, wub[dh:, :], preferred_element_type=F32))
        hdn = g * (1.0 / (1.0 + jnp.exp(-g))) * u
        y = jnp.dot(hdn.astype(BF16), wdb[...], preferred_element_type=F32)
        bits = pltpu.bitcast(y.astype(BF16).astype(F32), jnp.uint32)
        ys_ref[...] = (bits[:, dh:] & jnp.uint32(0xFFFF0000)) | (bits[:, :dh] >> 16)

    @pl.when(b >= na_ref[0])
    def _():
        ys_ref[...] = jnp.zeros(ys_ref.shape, ys_ref.dtype)


def _experts(blk_e, n_act, nxt_e, xs, w_gate, w_up, w_down):
    p, dh = xs.shape
    d = 2 * dh
    de = w_gate.shape[2]

    def row_map(b, be, na, nx):
        return (jnp.minimum(b, na[0] - 1), 0)

    return pl.pallas_call(
        _experts_kernel,
        out_shape=jax.ShapeDtypeStruct((p, dh), jnp.uint32),
        grid_spec=pltpu.PrefetchScalarGridSpec(
            num_scalar_prefetch=3,
            grid=(p // EB,),
            in_specs=[
                pl.BlockSpec((EB, dh), row_map),
                pl.BlockSpec(memory_space=pl.ANY),
                pl.BlockSpec(memory_space=pl.ANY),
                pl.BlockSpec(memory_space=pl.ANY),
            ],
            out_specs=pl.BlockSpec((EB, dh), lambda b, be, na, nx: (b, 0)),
            scratch_shapes=[pltpu.VMEM((d, de), F32), pltpu.VMEM((d, de), F32), pltpu.VMEM((de, d), F32),
                            pltpu.VMEM((d, de), BF16), pltpu.VMEM((d, de), BF16), pltpu.VMEM((de, d), BF16),
                            pltpu.SemaphoreType.DMA((3,))],
        ),
        compiler_params=_cparams(("arbitrary",)),
        name="experts",
    )(blk_e, n_act, nxt_e, xs, w_gate, w_up, w_down)


def _combine_kernel(cur_ref, nxt_ref, ys_ref, h_ref, rt_ref, o_ref, ybuf, sem):
    i = pl.program_id(0)
    nt = pl.num_programs(0)
    slot = i % 2
    tm = h_ref.shape[0]

    def chunk_copy(run_row, sorted_row, rows, sl):
        return pltpu.make_async_copy(ys_ref.at[pl.ds(run_row, rows)], ybuf.at[sl, pl.ds(sorted_row, rows)],
                                     sem.at[sl])

    @pl.when(i == 0)
    def _():
        ybuf[...] = jnp.zeros(ybuf.shape, ybuf.dtype)
        _for_each_chunk(cur_ref, lambda run_row, sorted_row, rows, priority:
                        chunk_copy(run_row, sorted_row, rows, 0).start(priority=priority))

    @pl.when(i + 1 < nt)
    def _():
        _for_each_chunk(nxt_ref, lambda run_row, sorted_row, rows, priority:
                        chunk_copy(run_row, sorted_row, rows, 1 - slot).start(priority=priority))

    _wait_chunks(cur_ref, lambda rows: chunk_copy(0, 0, rows, slot))

    rt = rt_ref[...]
    w = ybuf[slot]
    dh = w.shape[1]
    y_lo = pltpu.bitcast(w << 16, F32).astype(BF16)
    y_hi = pltpu.bitcast(w & jnp.uint32(0xFFFF0000), F32).astype(BF16)
    col = lax.broadcasted_iota(jnp.int32, (tm, SROWS), 1).astype(F32)
    wsel = jnp.where(col == rt[:, 2:3], rt[:, 0:1], jnp.where(col == rt[:, 3:4], rt[:, 1:2], 0.0)).astype(BF16)
    for half, yb in ((slice(0, dh), y_lo), (slice(dh, 2 * dh), y_hi)):
        o_ref[:, half] = h_ref[:, half] + jnp.dot(wsel, yb, preferred_element_type=F32)


def _combine(runs, ys, h1, rt):
    n, d = h1.shape
    nt = n // TM
    return pl.pallas_call(
        _combine_kernel,
        out_shape=jax.ShapeDtypeStruct((n, d), F32),
        grid=(nt,),
        in_specs=[
            pl.BlockSpec((1, 1, LANES), lambda i: (i, 0, 0), memory_space=pltpu.SMEM),
            pl.BlockSpec((1, 1, LANES), lambda i: (jnp.minimum(i + 1, nt - 1), 0, 0), memory_space=pltpu.SMEM),
            pl.BlockSpec(memory_space=pl.ANY),
            pl.BlockSpec((TM, d), lambda i: (i, 0)),
            pl.BlockSpec((TM, LANES), lambda i: (i, 0)),
        ],
        out_specs=pl.BlockSpec((TM, d), lambda i: (i, 0)),
        scratch_shapes=[pltpu.VMEM((2, SROWS, d // 2), jnp.uint32), pltpu.SemaphoreType.DMA((2,))],
        compiler_params=_cparams(("arbitrary",)),
        name="combine",
    )(runs, runs, ys, h1, rt)


def kernel(x, meta_tokens, rel_bias, norm1_gain, w_in, diff_q_gain, diff_k_gain, lam_q1, lam_k1, lam_q2, lam_k2, diff_subln_gain, swa_q_gain, swa_k_gain, swa_sinks, w_out, norm2_gain, w_group, b_group, w_router, b_router, w_gate, w_up, w_down):
    batch, seq, d = x.shape
    depth = w_in.shape[0]
    n = batch * seq
    assert seq % TQ == 0 and n % TM == 0 and n % TP == 0 and d == 1024
    assert meta_tokens.shape[0] == N_META
    assert depth == 1, "the meta-token rows of the residual stream are not carried across layers"

    h = x.reshape(n, d)
    dblk, bm0, bt = _bias_tables(rel_bias, TQ)
    scale = HEAD_DIM ** -0.5
    bd = jnp.asarray(np.kron(np.eye(MXU_DIM // HEAD_DIM), np.full((HEAD_DIM, HEAD_DIM), 1.0 / HEAD_DIM)), BF16)
    ones = jnp.ones((HEAD_DIM,), F32)
    lower_pad = N_EXPERTS + N_GROUPS

    for layer in range(depth):
        lambda_init = 0.8 - 0.6 * math.exp(-0.3 * layer)
        w_cat = w_in[layer].astype(BF16)
        gain = jnp.concatenate([
            jnp.tile(diff_q_gain[layer] * (scale * LOG2E), 2 * N_DIFF_HEADS),
            jnp.tile(diff_k_gain[layer], 2 * N_DIFF_HEADS),
            jnp.tile(ones, 2 * N_DIFF_HEADS),
            jnp.tile(swa_q_gain[layer] * scale, N_SWA_HEADS),
            jnp.tile(swa_k_gain[layer], N_SWA_KV),
            jnp.tile(ones, N_SWA_KV)]).reshape(1, C_END).astype(F32)
        nmask = np.zeros((1, C_END), np.float32)
        nmask[:, C_DQ:C_DV] = 1.0
        nmask[:, C_SQ:C_SV] = 1.0
        nmask = jnp.asarray(nmask)
        g1 = norm1_gain[layer].reshape(1, d).astype(F32)

        qkv = _proj(h, g1, w_cat, bd, gain, nmask, TP)
        qkv_meta = _proj(meta_tokens.astype(F32), g1, w_cat, bd, gain, nmask, N_META)
        meta_pad = jnp.pad(qkv_meta, ((0, TQ - N_META), (0, 0)))

        lamv = jnp.pad(jnp.stack([lam_q1[layer], lam_k1[layer], lam_q2[layer], lam_k2[layer]]).astype(F32),
                       ((0, 4), (0, LANES - HEAD_DIM)))
        mixd = _diff_attention(qkv, meta_pad[:, C_DK:C_DV], meta_pad[:, C_DV:C_SQ], dblk, bm0, lamv,
                               diff_subln_gain[layer].reshape(1, LANES).astype(F32), batch, seq, lambda_init)
        mixs = _swa_attention(swa_sinks[layer].astype(F32), qkv, meta_pad[:BLOCK, C_SK:C_SV],
                              meta_pad[:BLOCK, C_SV:C_END], jnp.swapaxes(bt, -1, -2), batch, seq)

        wr = jnp.pad(jnp.concatenate([w_router[layer], w_group[layer]], axis=1),
                     ((0, 0), (0, LANES - lower_pad))).astype(BF16)
        br = jnp.pad(jnp.concatenate([b_router[layer], b_group[layer]]), (0, LANES - lower_pad)).reshape(1, LANES)
        h1, hb, rt, tinfo, cnt = _outproj(h, mixd, mixs, w_out[layer].astype(BF16),
                                          norm2_gain[layer].reshape(1, d).astype(F32), wr, br.astype(F32))

        nt = n // TM
        counts = cnt[0, :N_EXPERTS].astype(jnp.int32)
        nblk_e = (counts + EB - 1) // EB
        blk_end = jnp.cumsum(nblk_e)
        pstart = ((blk_end - nblk_e) * EB).astype(jnp.int32)
        n_blocks = -(-(2 * n + nt * N_EXPERTS * (ROW_ALIGN - 1) + N_EXPERTS * (EB - 1)) // EB)
        blk_ids = jnp.arange(n_blocks)
        blk_e = jnp.minimum(jnp.sum(blk_end[None, :] <= blk_ids[:, None], axis=1), N_EXPERTS - 1).astype(jnp.int32)
        n_act = blk_end[-1:].astype(jnp.int32)
        is_last = jnp.any((blk_end[None, :] == blk_ids[:, None] + 1) & (nblk_e[None, :] > 0), axis=1)
        zero_blocks = jnp.where(blk_ids >= n_act[0], 2, jnp.where(is_last, 1, 0)).astype(jnp.int32)
        ti = tinfo.reshape(nt, 8, LANES)
        run_len = ti[:, 0, :N_EXPERTS].astype(jnp.int32)
        run_start = pstart[None, :] + ti[:, 1, :N_EXPERTS].astype(jnp.int32)
        run_groups = (run_len + ROW_ALIGN - 1) // ROW_ALIGN
        assert n_blocks * EB * SORTED_RANGE < 2 ** 31
        runs = _chunk_table(run_start, run_groups)

        xs = _dispatch(zero_blocks, runs, hb, rt, n_blocks * EB)
        own = jnp.where(nblk_e > 0, jnp.arange(N_EXPERTS), N_EXPERTS)
        later = jnp.concatenate([lax.cummin(own[::-1])[::-1][1:], jnp.full((1,), N_EXPERTS)])
        nxt_e = jnp.where(later < N_EXPERTS, later, -1).astype(jnp.int32)
        ys = _experts(blk_e, n_act, nxt_e, xs, w_gate[layer], w_up[layer], w_down[layer])
        h = _combine(runs, ys, h1, rt)
    return h.reshape(batch, seq, d)
```

```python
import functools
import math

import numpy as np
import jax
import jax.numpy as jnp
from jax import lax
from jax.experimental import pallas as pl
from jax.experimental.pallas import tpu as pltpu

F32 = jnp.float32
BF16 = jnp.bfloat16

HEAD_DIM = 64
N_DIFF_HEADS = 4
N_SWA_HEADS = 8
N_SWA_KV = 2
BLOCK = 128
N_META = 16
N_BUCKETS = 32
MAX_DISTANCE = 128
N_GROUPS = 4
EXPERTS_PER_GROUP = 8
N_EXPERTS = N_GROUPS * EXPERTS_PER_GROUP
EPS = 1e-6
NEG = -1e30
LOG2E = math.log2(math.e)

LANES = 128
MXU_DIM = 256
V7X_VMEM_BYTES = 64 * 1024 * 1024
VMEM_LIMIT = V7X_VMEM_BYTES * 3 // 4

TP = 1024
TM = 512
TQ = 256
ONES_ROWS = 16
EB = 512
EXPERT_PART = EB // 4
ROW_ALIGN = 8
CHUNK_ROWS = (32, 16, 8)
SROWS = -(-(2 * TM + N_EXPERTS * (ROW_ALIGN - 1)) // MXU_DIM) * MXU_DIM
SORTED_RANGE = 2048
CHUNK_SLOTS = (SROWS // CHUNK_ROWS[0],) + (N_EXPERTS,) * (len(CHUNK_ROWS) - 1)
CHUNK_OFFSETS = tuple(sum(CHUNK_SLOTS[:k]) for k in range(len(CHUNK_ROWS)))
CHUNK_COUNTS = sum(CHUNK_SLOTS)
assert SROWS <= SORTED_RANGE and CHUNK_COUNTS + len(CHUNK_ROWS) <= LANES

C_DQ, C_DK, C_DV, C_SQ, C_SK, C_SV, C_END = 0, 512, 1024, 1536, 2048, 2176, 2304
NORM_GROUPS = (0, 1, 2, 3, 6, 7, 8)


def _cparams(sem):
    return pltpu.CompilerParams(dimension_semantics=sem, vmem_limit_bytes=VMEM_LIMIT)


def _t5_bucket_np(dist):
    n = np.maximum(dist, 0)
    max_exact = N_BUCKETS // 2
    nf = np.maximum(n, 1).astype(np.float32)
    large = max_exact + (np.log(nf / np.float32(max_exact)) / np.float32(math.log(MAX_DISTANCE / max_exact))
                         * np.float32(N_BUCKETS - max_exact)).astype(np.int32)
    large = np.minimum(large, N_BUCKETS - 1)
    return np.where(n < max_exact, n, large)


def _bias_tables(rel_bias, tq):
    nd = 2 * BLOCK
    buckets = _t5_bucket_np(np.arange(nd))
    assert (buckets[MAX_DISTANCE:] == N_BUCKETS - 1).all()
    rb = rel_bias.astype(F32)
    r = np.arange(BLOCK)[:, None]
    c = np.arange(BLOCK)[None, :]
    d_own = r - c
    d_prev = BLOCK + r - c
    far = rb[N_BUCKETS - 1]

    def take(dist, heads):
        idx = jnp.asarray(buckets[np.clip(dist, 0, nd - 1)], jnp.int32)[None]
        out = jnp.zeros((heads.stop - heads.start,) + dist.shape, F32)
        for b in range(N_BUCKETS):
            out = jnp.where(idx == b, rb[b, heads].reshape((-1,) + (1,) * dist.ndim), out)
        return out

    hd = slice(0, N_DIFF_HEADS)
    far_d = far[hd][:, None, None]
    d0 = jnp.where(d_own[None] >= 0, take(d_own, hd) - far_d, NEG)
    d1 = take(d_prev, hd) - far_d
    dblk = jnp.stack([d0, d1], axis=1)
    rq = np.arange(tq)[:, None]
    cm = np.arange(LANES)[None, :]
    d_meta = (N_META + rq - cm)[:, :N_META]
    bm0 = jnp.pad(take(d_meta, hd) - far_d, ((0, 0), (0, 0), (0, LANES - N_META)), constant_values=NEG)

    hs = slice(N_DIFF_HEADS, N_DIFF_HEADS + N_SWA_HEADS)
    far_s = far[hs][:, None, None]
    d_meta_s = N_META + r - cm
    meta_first = jnp.where((cm < N_META)[None], take(d_meta_s, hs), NEG)
    meta_rest = jnp.where((cm < N_META)[None], jnp.broadcast_to(far_s, (N_SWA_HEADS, BLOCK, LANES)), NEG)
    prev_rest = jnp.where((c > r)[None], take(d_prev, hs), NEG)
    prev_first = jnp.full((N_SWA_HEADS, BLOCK, BLOCK), NEG, F32)
    own = jnp.where((d_own >= 0)[None], take(d_own, hs), NEG)
    bt = jnp.stack([jnp.concatenate([prev_first, own, meta_first[..., :N_META]], axis=-1),
                    jnp.concatenate([prev_rest, own, meta_rest[..., :N_META]], axis=-1)], axis=0)
    return dblk.astype(F32), bm0.astype(F32), bt.astype(F32)


def _proj_kernel(x_ref, g1_ref, w_ref, bd_ref, gain_ref, nmask_ref, o_ref):
    x = x_ref[...]
    a = x * lax.rsqrt(jnp.mean(x * x, axis=-1, keepdims=True) + EPS) * g1_ref[...]
    p = jnp.dot(a.astype(BF16), w_ref[...], preferred_element_type=F32)
    bd = bd_ref[...]
    for j in range(C_END // MXU_DIM):
        sl = slice(j * MXU_DIM, (j + 1) * MXU_DIM)
        pj = p[:, sl]
        if j in NORM_GROUPS:
            ms = jnp.dot((pj * pj).astype(BF16), bd, preferred_element_type=F32)
            pj = jnp.where(nmask_ref[:, sl] != 0.0, pj * lax.rsqrt(ms + EPS) * gain_ref[:, sl], pj)
        o_ref[:, sl] = pj.astype(BF16)


def _proj(x2, g1, w, bd, gain, nmask, tm):
    n = x2.shape[0]
    return pl.pallas_call(
        _proj_kernel,
        out_shape=jax.ShapeDtypeStruct((n, C_END), BF16),
        grid=(n // tm,),
        in_specs=[
            pl.BlockSpec((tm, x2.shape[1]), lambda i: (i, 0)),
            pl.BlockSpec(g1.shape, lambda i: (0, 0)),
            pl.BlockSpec(w.shape, lambda i: (0, 0)),
            pl.BlockSpec(bd.shape, lambda i: (0, 0)),
            pl.BlockSpec(gain.shape, lambda i: (0, 0)),
            pl.BlockSpec(nmask.shape, lambda i: (0, 0)),
        ],
        out_specs=pl.BlockSpec((tm, C_END), lambda i: (i, 0)),
        compiler_params=_cparams(("parallel",)),
        name="proj",
    )(x2, g1, w, bd, gain, nmask)


def _diff_kernel(qi_tab, t_tab, q_ref, k_ref, v_ref, km_ref, vm_ref, d_ref, bm0_ref, lamv_ref, gain_ref, o_ref,
                 bias_ref, mb_ref, qs_ref, kt_ref, vt_ref, s_buf, p_buf, a_buf, m_ref, acc_ref, *,
                 lambda_init, n_steps, n_far, n_near):
    tq = TQ
    nq = q_ref.shape[0] // tq
    nb = tq // BLOCK
    BIAS_LEFT, BIAS_DIAG, BIAS_NONE = 0, 1, 2

    d0 = d_ref[0, 0] * LOG2E
    d1 = d_ref[0, 1] * LOG2E
    zeros = jnp.zeros((BLOCK, BLOCK), F32)
    for a in range(nb):
        for b in range(nb):
            rs, cs = slice(a * BLOCK, (a + 1) * BLOCK), slice(b * BLOCK, (b + 1) * BLOCK)
            if a == b:
                blk = d0
            elif b == a + 1:
                blk = d1
            elif b > a:
                blk = zeros
            else:
                blk = jnp.full((BLOCK, BLOCK), NEG, F32)
            bias_ref[BIAS_DIAG, rs, cs] = blk
            bias_ref[BIAS_LEFT, rs, cs] = d1 if (b == 0 and a == nb - 1) else zeros
    bias_ref[BIAS_NONE] = jnp.zeros((tq, tq), F32)
    mb_ref[0] = jnp.zeros((N_META, tq), F32)
    mb_ref[1] = bm0_ref[0, :N_META, :] * LOG2E

    lane = lax.broadcasted_iota(jnp.int32, (tq, LANES), 1)
    for i in range(nq):
        rows = slice(i * tq, (i + 1) * tq)
        q = q_ref[rows, :].astype(F32)
        qs_ref[i] = jnp.transpose(jnp.concatenate([jnp.where(lane < HEAD_DIM, q, 0.0),
                                                   jnp.where(lane >= HEAD_DIM, q, 0.0)], axis=0)).astype(BF16)
        vt_ref[i, :LANES, :] = jnp.transpose(v_ref[rows, :].astype(F32)).astype(BF16)
        kt_ref[i] = k_ref[rows, :]
    vt_ref[nq, :LANES, :] = jnp.transpose(vm_ref[...].astype(F32)).astype(BF16)
    kt_ref[nq] = km_ref[...]
    vt_ref[:, LANES:, :] = jnp.ones((nq + 1, ONES_ROWS, tq), BF16)
    acc_ref[...] = jnp.zeros(acc_ref.shape, F32)
    m_ref[...] = jnp.full(m_ref.shape, NEG, F32)
    lv = lamv_ref[...]
    lam = (jnp.exp(jnp.sum(lv[0:1] * lv[1:2], axis=-1, keepdims=True))
           - jnp.exp(jnp.sum(lv[2:3] * lv[3:4], axis=-1, keepdims=True)) + lambda_init)

    FAR, NEAR, META = 0, 1, 2

    def stage_a(n, slot, kind):
        qi, t = qi_tab[n], t_tab[n]
        if kind == META:
            s = jnp.dot(kt_ref[nq, :N_META, :], qs_ref[qi], preferred_element_type=F32)
            s_buf[slot, :N_META] = s + jnp.tile(mb_ref[jnp.where(qi == 0, 1, 0)], (1, 2))
            return
        s = jnp.dot(kt_ref[t - 1], qs_ref[qi], preferred_element_type=F32)
        if kind == NEAR:
            which = jnp.where(t == qi + 1, BIAS_DIAG, jnp.where(t == qi, BIAS_LEFT, BIAS_NONE))
            s = s + jnp.tile(bias_ref[which], (1, 2))
        s_buf[slot] = s

    def stage_b(n, slot, kind):
        qi = qi_tab[n]
        rows = slice(0, N_META if kind == META else tq)
        s = s_buf[slot, rows]
        m_prev = m_ref[qi]
        m_new = jnp.maximum(m_prev, jnp.max(s, axis=0, keepdims=True))
        a_buf[slot] = jnp.exp2(m_prev - m_new)
        p_buf[slot, rows] = jnp.exp2(s - m_new[0:1]).astype(BF16)
        m_ref[qi] = m_new

    def stage_c(n, slot, kind):
        qi, t = qi_tab[n], t_tab[n]
        if kind == META:
            pv = jnp.dot(vt_ref[nq, :, :N_META], p_buf[slot, :N_META], preferred_element_type=F32)
        else:
            pv = jnp.dot(vt_ref[t - 1], p_buf[slot], preferred_element_type=F32)
        acc_ref[qi] = a_buf[slot][0:1] * acc_ref[qi] + pv

    LEAD, SLOTS, UNROLL = 2, 3, 12
    assert UNROLL % SLOTS == 0

    def pipeline(base, count, kind, biased_from=None):
        def kind_a(j):
            return kind if biased_from is None or j < biased_from else NEAR

        if count <= 2 * LEAD:
            for j in range(count):
                stage_a(base + j, 0, kind_a(j))
                stage_b(base + j, 0, kind)
                stage_c(base + j, 0, kind)
            return
        for j in range(2 * LEAD):
            stage_a(base + j, j % SLOTS, kind_a(j))
            if j >= LEAD:
                stage_b(base + j - LEAD, (j - LEAD) % SLOTS, kind)

        def steps(n, first, count):
            for j in range(count):
                stage_a(base + n + j + 2 * LEAD, (j + 2 * LEAD) % SLOTS, kind_a(first + j + 2 * LEAD))
                stage_b(base + n + j + LEAD, (j + LEAD) % SLOTS, kind)
                stage_c(base + n + j, j % SLOTS, kind)

        n_steady = count - 2 * LEAD
        n_blocks = n_steady // UNROLL
        switch = n_blocks if biased_from is None else (biased_from - 2 * LEAD) // UNROLL
        assert biased_from is None or (biased_from - 2 * LEAD) % UNROLL == 0
        for lo, hi in ((0, min(switch, n_blocks)), (min(switch, n_blocks), n_blocks)):
            lax.fori_loop(lo, hi, lambda k, carry, lo=lo: (steps(UNROLL * k, UNROLL * lo, UNROLL), carry)[1], 0)
        steps(UNROLL * n_blocks, UNROLL * n_blocks, n_steady - UNROLL * n_blocks)
        for j in range(n_steady, count):
            if j + LEAD < count:
                stage_b(base + j + LEAD, (j + LEAD) % SLOTS, kind)
            stage_c(base + j, j % SLOTS, kind)

    if n_far >= 2 * LEAD:
        pipeline(0, n_far + n_near, FAR, 2 * LEAD + UNROLL * ((n_far - 2 * LEAD) // UNROLL))
    else:
        pipeline(0, n_far + n_near, NEAR)
    pipeline(n_far + n_near, n_steps - n_far - n_near, META)

    for i in range(nq):
        acc = acc_ref[i]
        o = acc[:LANES] * (1.0 / acc[LANES:LANES + 1])
        d = o[:, :tq] - lam * o[:, tq:]
        y = d * lax.rsqrt(jnp.mean(d * d, axis=0, keepdims=True) + EPS) * jnp.tile(gain_ref[...], (1, tq // LANES))
        o_ref[i * tq:(i + 1) * tq, :] = jnp.transpose(y * (1.0 - lambda_init)).astype(BF16)


def _diff_attention(qkv, km, vm, dblk, bm0, lamv, gain, batch, seq, lambda_init):
    nq = seq // TQ
    far = [(qi, t) for qi in range(nq) for t in range(1, qi)]
    near = [(qi, t) for qi in range(nq) for t in (qi, qi + 1) if t >= 1]
    meta = [(qi, 0) for qi in range(nq)]
    steps = far + near + meta
    qi_tab = jnp.asarray([s[0] for s in steps], jnp.int32)
    t_tab = jnp.asarray([s[1] for s in steps], jnp.int32)
    kern = functools.partial(_diff_kernel, lambda_init=lambda_init, n_steps=len(steps), n_far=len(far),
                             n_near=len(near))
    return pl.pallas_call(
        kern,
        out_shape=jax.ShapeDtypeStruct((batch * seq, N_DIFF_HEADS * LANES), BF16),
        grid_spec=pltpu.PrefetchScalarGridSpec(
            num_scalar_prefetch=2,
            grid=(batch, N_DIFF_HEADS),
            in_specs=[
                pl.BlockSpec((seq, LANES), lambda b, h, *_: (b, C_DQ // LANES + h)),
                pl.BlockSpec((seq, LANES), lambda b, h, *_: (b, C_DK // LANES + h)),
                pl.BlockSpec((seq, LANES), lambda b, h, *_: (b, C_DV // LANES + h)),
                pl.BlockSpec((TQ, LANES), lambda b, h, *_: (0, h)),
                pl.BlockSpec((TQ, LANES), lambda b, h, *_: (0, h)),
                pl.BlockSpec((1, 2, BLOCK, BLOCK), lambda b, h, *_: (h, 0, 0, 0)),
                pl.BlockSpec((1, LANES, TQ), lambda b, h, *_: (h, 0, 0)),
                pl.BlockSpec(lamv.shape, lambda b, h, *_: (0, 0)),
                pl.BlockSpec((LANES, LANES), lambda b, h, *_: (0, 0)),
            ],
            out_specs=pl.BlockSpec((seq, LANES), lambda b, h, *_: (b, h)),
            scratch_shapes=[
                pltpu.VMEM((3, TQ, TQ), F32),
                pltpu.VMEM((2, N_META, TQ), F32),
                pltpu.VMEM((nq, LANES, 2 * TQ), BF16),
                pltpu.VMEM((nq + 1, TQ, LANES), BF16),
                pltpu.VMEM((nq + 1, LANES + ONES_ROWS, TQ), BF16),
                pltpu.VMEM((3, TQ, 2 * TQ), F32),
                pltpu.VMEM((3, TQ, 2 * TQ), BF16),
                pltpu.VMEM((3, 8, 2 * TQ), F32),
                pltpu.VMEM((nq, 8, 2 * TQ), F32),
                pltpu.VMEM((nq, LANES + ONES_ROWS, 2 * TQ), F32),
            ],
        ),
        compiler_params=_cparams(("parallel", "parallel")),
        name="diff_attention",
    )(qi_tab, t_tab, qkv, qkv, qkv, km, vm, jnp.swapaxes(dblk, -1, -2), jnp.swapaxes(bm0, -1, -2), lamv,
      jnp.broadcast_to(gain.reshape(LANES, 1), (LANES, LANES)))


def _swa_kernel(sink_ref, q_ref, k_ref, v_ref, km_ref, vm_ref, bt_ref, o_ref, kd_ref, vt_ref,
                s_scr, p_scr, inv_scr):
    nkb = k_ref.shape[0] // BLOCK
    lane = lax.broadcasted_iota(jnp.int32, (BLOCK, LANES), 1)
    pairs = [(g, u) for g in range(N_SWA_KV) for u in range(2)]

    def both_halves(k):
        k0, k1 = k[:, :HEAD_DIM], k[:, HEAD_DIM:]
        return jnp.concatenate([k0, k0, k1, k1], axis=1)

    def prepare(j, carry):
        rows = pl.ds(pl.multiple_of(j * BLOCK, BLOCK), BLOCK)
        kd_ref[j] = both_halves(k_ref[rows, :])
        vt_ref[j] = jnp.transpose(v_ref[rows, :].astype(F32)).astype(BF16)
        return carry
    lax.fori_loop(0, nkb, prepare, 0)
    kd_ref[nkb] = both_halves(km_ref[...])
    vt_ref[nkb] = jnp.transpose(vm_ref[...].astype(F32)).astype(BF16)

    def scores(n, slot):
        first = jnp.where(n == 0, 0, 1)
        prev = jnp.maximum(n - 1, 0)
        r_q = pl.multiple_of(n * BLOCK, BLOCK)
        for c, (g, u) in enumerate(pairs):
            ks = slice(g * LANES, (g + 1) * LANES)
            kcat = jnp.concatenate([kd_ref[prev, :, ks], kd_ref[n, :, ks], kd_ref[nkb, :N_META, ks]], axis=0)
            h0 = 4 * g + 2 * u
            qp = q_ref[pl.ds(r_q, BLOCK), (2 * g + u) * LANES:(2 * g + u + 1) * LANES].astype(F32)
            qs = jnp.transpose(jnp.concatenate([jnp.where(lane < HEAD_DIM, qp, 0.0),
                                                jnp.where(lane >= HEAD_DIM, qp, 0.0)], axis=0)).astype(BF16)
            s = jnp.dot(kcat, qs, preferred_element_type=F32)
            s_scr[slot, c] = s + jnp.concatenate([bt_ref[first, h0], bt_ref[first, h0 + 1]], axis=1)

    def exponentials(slot):
        for c, (g, u) in enumerate(pairs):
            h0 = 4 * g + 2 * u
            s = s_scr[slot, c]
            sink = jnp.concatenate([sink_ref[h0:h0 + 1, :], sink_ref[h0 + 1:h0 + 2, :]], axis=1)
            m = jnp.maximum(jnp.max(s, axis=0, keepdims=True), sink)
            p = jnp.exp(s - m)
            p_scr[slot, c] = p.astype(BF16)
            inv_scr[slot, c] = jnp.broadcast_to(1.0 / (jnp.sum(p, axis=0, keepdims=True) + jnp.exp(sink - m)),
                                                inv_scr.shape[2:])

    def values(n, slot):
        prev = jnp.maximum(n - 1, 0)
        r_q = pl.multiple_of(n * BLOCK, BLOCK)
        for c, (g, u) in enumerate(pairs):
            vs = slice(g * HEAD_DIM, (g + 1) * HEAD_DIM)
            vcat = jnp.concatenate([vt_ref[prev, vs, :], vt_ref[n, vs, :]], axis=1)
            o = (jnp.dot(vcat, p_scr[slot, c, :2 * BLOCK], preferred_element_type=F32)
                 + jnp.dot(vt_ref[nkb, vs, :N_META], p_scr[slot, c, 2 * BLOCK:], preferred_element_type=F32)
                 ) * inv_scr[slot, c][0:1]
            ot = jnp.transpose(o)
            o_ref[pl.ds(r_q, BLOCK), (2 * g + u) * LANES:(2 * g + u + 1) * LANES] = (
                jnp.concatenate([ot[:BLOCK], ot[BLOCK:]], axis=1).astype(BF16))

    scores(0, 0)
    scores(1, 1)
    exponentials(0)

    def blocks(n, count):
        for j in range(count):
            scores(n + j + 2, j % 2)
            exponentials((j + 1) % 2)
            values(n + j, j % 2)

    unroll = 4
    n_steady = nkb - 2
    lax.fori_loop(0, n_steady // unroll, lambda k, carry: (blocks(unroll * k, unroll), carry)[1], 0)
    blocks(n_steady // unroll * unroll, n_steady % unroll)
    exponentials(1)
    values(nkb - 2, 0)
    values(nkb - 1, 1)


def _swa_attention(sinks, qkv, km, vm, bt, batch, seq):
    nkb = seq // BLOCK
    swa_q = N_SWA_HEADS * HEAD_DIM
    assert nkb % 2 == 0
    sinkv =jnp.broadcast_to(sinks.reshape(N_SWA_HEADS, 1), (N_SWA_HEADS, LANES))
    return pl.pallas_call(
        _swa_kernel,
        out_shape=jax.ShapeDtypeStruct((batch * seq, N_SWA_HEADS * HEAD_DIM), BF16),
        grid=(batch,),
        in_specs=[
            pl.BlockSpec(sinkv.shape, lambda b: (0, 0)),
            pl.BlockSpec((seq, swa_q), lambda b: (b, C_SQ // swa_q)),
            pl.BlockSpec((seq, LANES), lambda b: (b, C_SK // LANES)),
            pl.BlockSpec((seq, LANES), lambda b: (b, C_SV // LANES)),
            pl.BlockSpec(km.shape, lambda b: (0, 0)),
            pl.BlockSpec(vm.shape, lambda b: (0, 0)),
            pl.BlockSpec(bt.shape, lambda b: (0, 0, 0, 0)),
        ],
        out_specs=pl.BlockSpec((seq, swa_q), lambda b: (b, 0)),
        scratch_shapes=[pltpu.VMEM((nkb + 1, BLOCK, 2 * LANES), BF16),
                        pltpu.VMEM((nkb + 1, LANES, BLOCK), BF16),
                        pltpu.VMEM((2, 4, 2 * BLOCK + N_META, 2 * BLOCK), F32),
                        pltpu.VMEM((2, 4, 2 * BLOCK + N_META, 2 * BLOCK), BF16),
                        pltpu.VMEM((2, 4, 8, 2 * BLOCK), F32)],
        compiler_params=_cparams(("parallel",)),
        name="swa_attention",
    )(sinkv, qkv, qkv, qkv, km, vm, bt)


def _outproj_kernel(x_ref, md_ref, ms_ref, wo_ref, g2_ref, wr_ref, br_ref,
                    h_ref, hb_ref, rt_ref, ti_ref, cnt_ref, c_ref, lg_ref):
    i = pl.program_id(0)

    @pl.when(i == 0)
    def _init():
        c_ref[...] = jnp.zeros(c_ref.shape, F32)
        lg_ref[...] = jnp.zeros(lg_ref.shape, F32)

    lg_prev = lg_ref[...]
    half = md_ref.shape[1]
    h = (x_ref[...]
         + jnp.dot(md_ref[...], wo_ref[:half, :], preferred_element_type=F32)
         + jnp.dot(ms_ref[...], wo_ref[half:, :], preferred_element_type=F32))
    h_ref[...] = h
    hn = h * lax.rsqrt(jnp.mean(h * h, axis=-1, keepdims=True) + EPS) * g2_ref[...]
    hb = hn.astype(BF16)
    hb_ref[...] = hb
    lg_ref[...] = jnp.dot(hb, wr_ref[...], preferred_element_type=F32) + br_ref[...]
    _route_tile(lg_prev, jnp.where(i > 0, 1.0, 0.0), rt_ref, ti_ref, c_ref)

    @pl.when(i == pl.num_programs(0) - 1)
    def _fin():
        cnt_ref[...] = c_ref[...]


def _route_tile(lg, live, rt_ref, ti_ref, c_ref):
    tm = lg.shape[0]
    lane_i = lax.broadcasted_iota(jnp.int32, lg.shape, 1)
    lane = lane_i.astype(F32)
    big = float(4 * LANES)
    is_g = (lane_i >= N_EXPERTS) & (lane_i < N_EXPERTS + N_GROUPS)
    glm = jnp.where(is_g, lg, -jnp.inf)
    gmax = jnp.max(glm, axis=1, keepdims=True)
    gidx = jnp.min(jnp.where(glm == gmax, lane, big), axis=1, keepdims=True) - N_EXPERTS
    gsum = jnp.sum(jnp.where(is_g, jnp.exp(lg - gmax), 0.0), axis=1, keepdims=True)
    g_w = 1.0 / gsum
    lane_grp = (lane_i >> 3).astype(F32)
    in_grp = (lane_i < N_EXPERTS) & (lane_grp == gidx)
    el = jnp.where(in_grp, lg, -jnp.inf)
    t1 = jnp.max(el, axis=1, keepdims=True)
    j1 = jnp.min(jnp.where(el == t1, lane, big), axis=1, keepdims=True)
    el2 = jnp.where(lane == j1, -jnp.inf, el)
    t2 = jnp.max(el2, axis=1, keepdims=True)
    j2 = jnp.min(jnp.where(el2 == t2, lane, big), axis=1, keepdims=True)
    e2 = jnp.exp(t2 - t1)
    den = 1.0 + e2
    gate1 = g_w / den
    gate2 = g_w * e2 / den

    o1 = lane == j1
    o2 = lane == j2
    onehot = jnp.where(o1 | o2, 1.0, 0.0).astype(BF16)
    rr = lax.broadcasted_iota(jnp.int32, (tm, tm), 0)
    cc = lax.broadcasted_iota(jnp.int32, (tm, tm), 1)
    lower = jnp.where(rr > cc, 1.0, 0.0).astype(BF16)
    pfx = jnp.dot(lower, onehot, preferred_element_type=F32)
    cnt_tile = jnp.sum(onehot.astype(F32), axis=0, keepdims=True)
    groups = jnp.floor((cnt_tile + (ROW_ALIGN - 1)) * (1.0 / ROW_ALIGN))
    er = lax.broadcasted_iota(jnp.int32, (LANES, LANES), 0)
    ec = lax.broadcasted_iota(jnp.int32, (LANES, LANES), 1)
    before = jnp.where(er < ec, 1.0, 0.0).astype(BF16)
    cbase = ROW_ALIGN * jnp.dot(jnp.broadcast_to(groups, (8, LANES)).astype(BF16), before,
                                preferred_element_type=F32)[0:1]
    at = pfx + cbase
    pos1 = jnp.sum(jnp.where(o1, at, 0.0), axis=1, keepdims=True)
    pos2 = jnp.sum(jnp.where(o2, at, 0.0), axis=1, keepdims=True)
    rt_ref[...] = jnp.where(lane_i == 0, gate1,
                            jnp.where(lane_i == 1, gate2,
                                      jnp.where(lane_i == 2, pos1,
                                                jnp.where(lane_i == 3, pos2, 0.0))))
    c_old = c_ref[...]
    c_ref[...] = c_old + groups * (ROW_ALIGN * live)
    row8 = lax.broadcasted_iota(jnp.int32, (8, LANES), 0)
    ti_ref[...] = jnp.where(row8 == 0, cnt_tile, jnp.where(row8 == 1, c_old, 0.0))


def _outproj(x2, mixd, mixs, wo, g2, wr, br):
    n, d = x2.shape
    nt = n // TM

    def proj_tile(i):
        return (jnp.minimum(i, nt - 1), 0)

    def route_tile(i):
        return (jnp.maximum(i - 1, 0), 0)

    return pl.pallas_call(
        _outproj_kernel,
        out_shape=(jax.ShapeDtypeStruct((n, d), F32),
                   jax.ShapeDtypeStruct((n, d), BF16),
                   jax.ShapeDtypeStruct((n, LANES), F32),
                   jax.ShapeDtypeStruct((nt * 8, LANES), F32),
                   jax.ShapeDtypeStruct((8, LANES), F32)),
        grid=(nt + 1,),
        in_specs=[
            pl.BlockSpec((TM, d), proj_tile),
            pl.BlockSpec((TM, mixd.shape[1]), proj_tile),
            pl.BlockSpec((TM, mixs.shape[1]), proj_tile),
            pl.BlockSpec(wo.shape, lambda i: (0, 0)),
            pl.BlockSpec(g2.shape, lambda i: (0, 0)),
            pl.BlockSpec(wr.shape, lambda i: (0, 0)),
            pl.BlockSpec(br.shape, lambda i: (0, 0)),
        ],
        out_specs=(pl.BlockSpec((TM, d), proj_tile),
                   pl.BlockSpec((TM, d), proj_tile),
                   pl.BlockSpec((TM, LANES), route_tile),
                   pl.BlockSpec((8, LANES), route_tile),
                   pl.BlockSpec((8, LANES), lambda i: (0, 0))),
        scratch_shapes=[pltpu.VMEM((8, LANES), F32), pltpu.VMEM((TM, LANES), F32)],
        compiler_params=_cparams(("arbitrary",)),
        name="outproj_router",
    )(x2, mixd, mixs, wo, g2, wr, br)


def _chunk_table(run_start, run_groups):
    nt = run_start.shape[0]
    sorted_start = (jnp.cumsum(run_groups, axis=1) - run_groups) * ROW_ALIGN
    word = run_start * SORTED_RANGE + sorted_start
    done = jnp.zeros_like(run_groups)
    segments, counts = [], []
    for k, rows in enumerate(CHUNK_ROWS):
        per_run = run_groups // (rows // ROW_ALIGN)
        if k > 0:
            per_run = per_run % 2
        last = jnp.cumsum(per_run, axis=1)[:, None, :]
        first = last - per_run[:, None, :]
        slot = jnp.arange(CHUNK_SLOTS[k], dtype=jnp.int32)[None, :, None]
        value = word[:, None, :] + (done[:, None, :] + (slot - first) * rows) * (SORTED_RANGE + 1)
        segments.append(jnp.sum(jnp.where((first <= slot) & (slot < last), value, 0), axis=2))
        counts.append(last[:, 0, -1:])
        done = done + per_run * rows
    pad = jnp.zeros((nt, LANES - CHUNK_COUNTS - len(CHUNK_ROWS)), jnp.int32)
    return jnp.concatenate(segments + counts + [pad], axis=1).astype(jnp.int32).reshape(nt, 1, LANES)


def _for_each_chunk(chunks_ref, fn):
    def issue(k, c, priority):
        word = chunks_ref[0, 0, CHUNK_OFFSETS[k] + c]
        fn(pl.multiple_of(word >> (SORTED_RANGE.bit_length() - 1), ROW_ALIGN),
           pl.multiple_of(word & (SORTED_RANGE - 1), ROW_ALIGN), CHUNK_ROWS[k], priority)

    for k in range(len(CHUNK_ROWS)):
        _in_pairs(chunks_ref[0, 0, CHUNK_COUNTS + k], lambda c, second, k=k: issue(k, c, second))


def _in_pairs(count, fn):
    half = count // 2

    def body(c, carry):
        fn(c, 0)
        fn(half + c, 1)
        return carry
    lax.fori_loop(0, half, body, 0)

    @pl.when(count % 2 == 1)
    def _():
        fn(count - 1, 0)


def _wait_chunks(chunks_ref, make_copy):
    for k, rows in enumerate(CHUNK_ROWS):
        _in_pairs(chunks_ref[0, 0, CHUNK_COUNTS + k], lambda c, second, rows=rows: make_copy(rows).wait())


def _dispatch_kernel(zf_ref, cur_ref, prv_ref, hb_ref, rt_ref, xs_ref, sbuf, zbuf, sem, zsem):
    i = pl.program_id(0)
    nt = pl.num_programs(0)
    slot = i % 2
    tm, d = hb_ref.shape

    def for_zero_blocks(kind, fn):
        def body(b, carry):
            @pl.when(zf_ref[b] == kind)
            def _():
                fn(pltpu.make_async_copy(zbuf, xs_ref.at[pl.ds(pl.multiple_of(b * EB, EB), EB)],
                                         zsem.at[kind - 1]))
            return carry
        lax.fori_loop(0, zf_ref.shape[0], body, 0)

    @pl.when(i == 0)
    def _():
        zbuf[...] = jnp.zeros(zbuf.shape, zbuf.dtype)
        for_zero_blocks(1, lambda c: c.start())
        for_zero_blocks(2, lambda c: c.start())
        for_zero_blocks(1, lambda c: c.wait())

    pos_t = jnp.transpose(rt_ref[...])
    srow = lax.broadcasted_iota(jnp.int32, (SROWS, tm), 0).astype(F32)
    sel = jnp.where(srow == pos_t[2:3, :], 1.0, jnp.where(srow == pos_t[3:4, :], 1.0, 0.0)).astype(BF16)
    srt = jnp.dot(sel, hb_ref[...], preferred_element_type=F32)
    bits = pltpu.bitcast(srt, jnp.uint32)
    sbuf[slot] = (bits[:, d // 2:] & jnp.uint32(0xFFFF0000)) | (bits[:, :d // 2] >> 16)

    def chunk_copy(run_row, sorted_row, rows, sl):
        return pltpu.make_async_copy(sbuf.at[sl, pl.ds(sorted_row, rows)], xs_ref.at[pl.ds(run_row, rows)],
                                     sem.at[sl])

    _for_each_chunk(cur_ref, lambda run_row, sorted_row, rows, priority:
                    chunk_copy(run_row, sorted_row, rows, slot).start(priority=priority))

    @pl.when(i > 0)
    def _():
        _wait_chunks(prv_ref, lambda rows: chunk_copy(0, 0, rows, 1 - slot))

    @pl.when(i == nt - 1)
    def _():
        _wait_chunks(cur_ref, lambda rows: chunk_copy(0, 0, rows, slot))
        for_zero_blocks(2, lambda c: c.wait())


def _dispatch(zero_blocks, runs, hb, rt, n_rows):
    n, d = hb.shape
    return pl.pallas_call(
        _dispatch_kernel,
        out_shape=jax.ShapeDtypeStruct((n_rows, d // 2), jnp.uint32),
        grid_spec=pltpu.PrefetchScalarGridSpec(
            num_scalar_prefetch=1,
            grid=(n // TM,),
            in_specs=[
                pl.BlockSpec((1, 1, LANES), lambda i, zf: (i, 0, 0), memory_space=pltpu.SMEM),
                pl.BlockSpec((1, 1, LANES), lambda i, zf: (jnp.maximum(i - 1, 0), 0, 0), memory_space=pltpu.SMEM),
                pl.BlockSpec((TM, d), lambda i, zf: (i, 0)),
                pl.BlockSpec((TM, LANES), lambda i, zf: (i, 0)),
            ],
            out_specs=pl.BlockSpec(memory_space=pl.ANY),
            scratch_shapes=[pltpu.VMEM((2, SROWS, d // 2), jnp.uint32), pltpu.VMEM((EB, d // 2), jnp.uint32),
                            pltpu.SemaphoreType.DMA((2,)), pltpu.SemaphoreType.DMA((2,))],
        ),
        compiler_params=_cparams(("arbitrary",)),
        name="dispatch",
    )(zero_blocks, runs, runs, hb, rt)


def _experts_kernel(be_ref, na_ref, nxt_ref, rows_ref, xs_ref, wg_hbm, wu_hbm, wd_hbm, ys_ref,
                    wgf, wuf, wdf, wgb, wub, wdb, sem):
    b = pl.program_id(0)

    def weight_copies(e):
        return (pltpu.make_async_copy(wg_hbm.at[e], wgf, sem.at[0]),
                pltpu.make_async_copy(wu_hbm.at[e], wuf, sem.at[1]),
                pltpu.make_async_copy(wd_hbm.at[e], wdf, sem.at[2]))

    @pl.when(b == 0)
    def _():
        for c in weight_copies(be_ref[0]):
            c.start()

    @pl.when(b < na_ref[0])
    def _():
        e = be_ref[b]
        changed = jnp.logical_or(b == 0, be_ref[jnp.maximum(b - 1, 0)] != e)

        @pl.when(changed)
        def _load():
            for c in weight_copies(e):
                c.wait()
            wgb[...] = wgf[...].astype(BF16)
            wub[...] = wuf[...].astype(BF16)
            wdb[...] = wdf[...].astype(BF16)
            nxt = nxt_ref[e]

            @pl.when(nxt >= 0)
            def _():
                for c in weight_copies(nxt):
                    c.start()

    def compute(rows):
        w = xs_ref[rows, :]
        x_lo = pltpu.bitcast(w << 16, F32).astype(BF16)
        x_hi = pltpu.bitcast(w & jnp.uint32(0xFFFF0000), F32).astype(BF16)
        dh = w.shape[1]
        g = (jnp.dot(x_lo, wgb[:dh, :], preferred_element_type=F32)
             + jnp.dot(x_hi, wgb[dh:, :], preferred_element_type=F32))
        u = (jnp.dot(x_lo, wub[:dh, :], preferred_element_type=F32)
             + jnp.dot(x_hi, wub[dh:, :], preferred_element_type=F32))
        hdn = g * (1.0 / (1.0 + jnp.exp(-g))) * u
        y = jnp.dot(hdn.astype(BF16), wdb[...], preferred_element_type=F32)
        bits = pltpu.bitcast(y.astype(BF16).astype(F32), jnp.uint32)
        ys_ref[rows, :] = (bits[:, dh:] & jnp.uint32(0xFFFF0000)) | (bits[:, :dh] >> 16)

    n_rows = rows_ref[b]

    part = EXPERT_PART
    whole = n_rows > 3 * part
    for lo, size, cond in ((0, 4 * part, whole),
                           (0, 2 * part, jnp.logical_and(n_rows > part, n_rows <= 3 * part)),
                           (0, part, jnp.logical_and(n_rows > 0, n_rows <= part)),
                           (2 * part, part, jnp.logical_and(n_rows > 2 * part, n_rows <= 3 * part))):
        @pl.when(cond)
        def _(lo=lo, size=size):
            compute(pl.ds(lo, size))

    for k in range(4):
        @pl.when(n_rows <= k * part)
        def _(k=k):
            ys_ref[pl.ds(k * part, part), :] = jnp.zeros((part, ys_ref.shape[1]), ys_ref.dtype)


def _experts(blk_e, n_act, nxt_e, blk_rows, xs, w_gate, w_up, w_down):
    p, dh = xs.shape
    d = 2 * dh
    de = w_gate.shape[2]

    def row_map(b, be, na, nx, br):
        return (jnp.minimum(b, na[0] - 1), 0)

    return pl.pallas_call(
        _experts_kernel,
        out_shape=jax.ShapeDtypeStruct((p, dh), jnp.uint32),
        grid_spec=pltpu.PrefetchScalarGridSpec(
            num_scalar_prefetch=4,
            grid=(p // EB,),
            in_specs=[
                pl.BlockSpec((EB, dh), row_map),
                pl.BlockSpec(memory_space=pl.ANY),
                pl.BlockSpec(memory_space=pl.ANY),
                pl.BlockSpec(memory_space=pl.ANY),
            ],
            out_specs=pl.BlockSpec((EB, dh), lambda b, be, na, nx, br: (b, 0)),
            scratch_shapes=[pltpu.VMEM((d, de), F32), pltpu.VMEM((d, de), F32), pltpu.VMEM((de, d), F32),
                            pltpu.VMEM((d, de), BF16), pltpu.VMEM((d, de), BF16), pltpu.VMEM((de, d), BF16),
                            pltpu.SemaphoreType.DMA((3,))],
        ),
        compiler_params=_cparams(("arbitrary",)),
        name="experts",
    )(blk_e, n_act, nxt_e, blk_rows, xs, w_gate, w_up, w_down)


def _combine_kernel(cur_ref, nxt_ref, ys_ref, h_ref, rt_ref, o_ref, ybuf, sem):
    i = pl.program_id(0)
    nt = pl.num_programs(0)
    slot = i % 2
    tm = h_ref.shape[0]

    def chunk_copy(run_row, sorted_row, rows, sl):
        return pltpu.make_async_copy(ys_ref.at[pl.ds(run_row, rows)], ybuf.at[sl, pl.ds(sorted_row, rows)],
                                     sem.at[sl])

    @pl.when(i == 0)
    def _():
        ybuf[...] = jnp.zeros(ybuf.shape, ybuf.dtype)
        _for_each_chunk(cur_ref, lambda run_row, sorted_row, rows, priority:
                        chunk_copy(run_row, sorted_row, rows, 0).start(priority=priority))

    @pl.when(i + 1 < nt)
    def _():
        _for_each_chunk(nxt_ref, lambda run_row, sorted_row, rows, priority:
                        chunk_copy(run_row, sorted_row, rows, 1 - slot).start(priority=priority))

    _wait_chunks(cur_ref, lambda rows: chunk_copy(0, 0, rows, slot))

    rt = rt_ref[...]
    w = ybuf[slot]
    dh = w.shape[1]
    y_lo = pltpu.bitcast(w << 16, F32).astype(BF16)
    y_hi = pltpu.bitcast(w & jnp.uint32(0xFFFF0000), F32).astype(BF16)
    col = lax.broadcasted_iota(jnp.int32, (tm, SROWS), 1).astype(F32)
    wsel = jnp.where(col == rt[:, 2:3], rt[:, 0:1], jnp.where(col == rt[:, 3:4], rt[:, 1:2], 0.0)).astype(BF16)
    for half, yb in ((slice(0, dh), y_lo), (slice(dh, 2 * dh), y_hi)):
        o_ref[:, half] = h_ref[:, half] + jnp.dot(wsel, yb, preferred_element_type=F32)


def _combine(runs, ys, h1, rt):
    n, d = h1.shape
    nt = n // TM
    return pl.pallas_call(
        _combine_kernel,
        out_shape=jax.ShapeDtypeStruct((n, d), F32),
        grid=(nt,),
        in_specs=[
            pl.BlockSpec((1, 1, LANES), lambda i: (i, 0, 0), memory_space=pltpu.SMEM),
            pl.BlockSpec((1, 1, LANES), lambda i: (jnp.minimum(i + 1, nt - 1), 0, 0), memory_space=pltpu.SMEM),
            pl.BlockSpec(memory_space=pl.ANY),
            pl.BlockSpec((TM, d), lambda i: (i, 0)),
            pl.BlockSpec((TM, LANES), lambda i: (i, 0)),
        ],
        out_specs=pl.BlockSpec((TM, d), lambda i: (i, 0)),
        scratch_shapes=[pltpu.VMEM((2, SROWS, d // 2), jnp.uint32), pltpu.SemaphoreType.DMA((2,))],
        compiler_params=_cparams(("arbitrary",)),
        name="combine",
    )(runs, runs, ys, h1, rt)


def kernel(x, meta_tokens, rel_bias, norm1_gain, w_in, diff_q_gain, diff_k_gain, lam_q1, lam_k1, lam_q2, lam_k2, diff_subln_gain, swa_q_gain, swa_k_gain, swa_sinks, w_out, norm2_gain, w_group, b_group, w_router, b_router, w_gate, w_up, w_down):
    batch, seq, d = x.shape
    depth = w_in.shape[0]
    n = batch * seq
    assert seq % TQ == 0 and n % TM == 0 and n % TP == 0 and d == 1024
    assert meta_tokens.shape[0] == N_META
    assert depth == 1, "the meta-token rows of the residual stream are not carried across layers"

    h = x.reshape(n, d)
    dblk, bm0, bt = _bias_tables(rel_bias, TQ)
    scale = HEAD_DIM ** -0.5
    bd = jnp.asarray(np.kron(np.eye(MXU_DIM // HEAD_DIM), np.full((HEAD_DIM, HEAD_DIM), 1.0 / HEAD_DIM)), BF16)
    ones = jnp.ones((HEAD_DIM,), F32)
    lower_pad = N_EXPERTS + N_GROUPS

    for layer in range(depth):
        lambda_init = 0.8 - 0.6 * math.exp(-0.3 * layer)
        w_cat = w_in[layer].astype(BF16)
        gain = jnp.concatenate([
            jnp.tile(diff_q_gain[layer] * (scale * LOG2E), 2 * N_DIFF_HEADS),
            jnp.tile(diff_k_gain[layer], 2 * N_DIFF_HEADS),
            jnp.tile(ones, 2 * N_DIFF_HEADS),
            jnp.tile(swa_q_gain[layer] * scale, N_SWA_HEADS),
            jnp.tile(swa_k_gain[layer], N_SWA_KV),
            jnp.tile(ones, N_SWA_KV)]).reshape(1, C_END).astype(F32)
        nmask = np.zeros((1, C_END), np.float32)
        nmask[:, C_DQ:C_DV] = 1.0
        nmask[:, C_SQ:C_SV] = 1.0
        nmask = jnp.asarray(nmask)
        g1 = norm1_gain[layer].reshape(1, d).astype(F32)

        qkv = _proj(h, g1, w_cat, bd, gain, nmask, TP)
        qkv_meta = _proj(meta_tokens.astype(F32), g1, w_cat, bd, gain, nmask, N_META)
        meta_pad = jnp.pad(qkv_meta, ((0, TQ - N_META), (0, 0)))

        lamv = jnp.pad(jnp.stack([lam_q1[layer], lam_k1[layer], lam_q2[layer], lam_k2[layer]]).astype(F32),
                       ((0, 4), (0, LANES - HEAD_DIM)))
        mixd = _diff_attention(qkv, meta_pad[:, C_DK:C_DV], meta_pad[:, C_DV:C_SQ], dblk, bm0, lamv,
                               diff_subln_gain[layer].reshape(1, LANES).astype(F32), batch, seq, lambda_init)
        mixs = _swa_attention(swa_sinks[layer].astype(F32), qkv, meta_pad[:BLOCK, C_SK:C_SV],
                              meta_pad[:BLOCK, C_SV:C_END], jnp.swapaxes(bt, -1, -2), batch, seq)

        wr = jnp.pad(jnp.concatenate([w_router[layer], w_group[layer]], axis=1),
                     ((0, 0), (0, LANES - lower_pad))).astype(BF16)
        br = jnp.pad(jnp.concatenate([b_router[layer], b_group[layer]]), (0, LANES - lower_pad)).reshape(1, LANES)
        h1, hb, rt, tinfo, cnt = _outproj(h, mixd, mixs, w_out[layer].astype(BF16),
                                          norm2_gain[layer].reshape(1, d).astype(F32), wr, br.astype(F32))

        nt = n // TM
        counts = cnt[0, :N_EXPERTS].astype(jnp.int32)
        nblk_e = (counts + EB - 1) // EB
        blk_end = jnp.cumsum(nblk_e)
        pstart = ((blk_end - nblk_e) * EB).astype(jnp.int32)
        n_blocks = -(-(2 * n + nt * N_EXPERTS * (ROW_ALIGN - 1) + N_EXPERTS * (EB - 1)) // EB)
        blk_ids = jnp.arange(n_blocks)
        blk_e = jnp.minimum(jnp.sum(blk_end[None, :] <= blk_ids[:, None], axis=1), N_EXPERTS - 1).astype(jnp.int32)
        n_act = blk_end[-1:].astype(jnp.int32)
        is_last = jnp.any((blk_end[None, :] == blk_ids[:, None] + 1) & (nblk_e[None, :] > 0), axis=1)
        zero_blocks = jnp.where(blk_ids >= n_act[0], 2, jnp.where(is_last, 1, 0)).astype(jnp.int32)
        ti = tinfo.reshape(nt, 8, LANES)
        run_len = ti[:, 0, :N_EXPERTS].astype(jnp.int32)
        run_start = pstart[None, :] + ti[:, 1, :N_EXPERTS].astype(jnp.int32)
        run_groups = (run_len + ROW_ALIGN - 1) // ROW_ALIGN
        assert n_blocks * EB * SORTED_RANGE < 2 ** 31
        runs = _chunk_table(run_start, run_groups)

        xs = _dispatch(zero_blocks, runs, hb, rt, n_blocks * EB)
        own = jnp.where(nblk_e > 0, jnp.arange(N_EXPERTS), N_EXPERTS)
        later = jnp.concatenate([lax.cummin(own[::-1])[::-1][1:], jnp.full((1,), N_EXPERTS)])
        nxt_e = jnp.where(later < N_EXPERTS, later, -1).astype(jnp.int32)
        blk_first = blk_end - nblk_e
        owned = (blk_first[None, :] <= blk_ids[:, None]) & (blk_ids[:, None] < blk_end[None, :])
        blk_rows = jnp.sum(jnp.where(owned, jnp.minimum(counts[None, :] - (blk_ids[:, None] - blk_first[None, :]) * EB,
                                                        EB), 0), axis=1).astype(jnp.int32)
        ys = _experts(blk_e, n_act, nxt_e, blk_rows, xs, w_gate[layer], w_up[layer], w_down[layer])
        h = _combine(runs, ys, h1, rt)
    return h.reshape(batch, seq, d)
```

```python
import functools
import math

import numpy as np
import jax
import jax.numpy as jnp
from jax import lax
from jax.experimental import pallas as pl
from jax.experimental.pallas import tpu as pltpu

F32 = jnp.float32
BF16 = jnp.bfloat16

HEAD_DIM = 64
N_DIFF_HEADS = 4
N_SWA_HEADS = 8
N_SWA_KV = 2
BLOCK = 128
N_META = 16
N_BUCKETS = 32
MAX_DISTANCE = 128
N_GROUPS = 4
EXPERTS_PER_GROUP = 8
N_EXPERTS = N_GROUPS * EXPERTS_PER_GROUP
EPS = 1e-6
NEG = -1e30
LOG2E = math.log2(math.e)

LANES = 128
MXU_DIM = 256
V7X_VMEM_BYTES = 64 * 1024 * 1024
VMEM_LIMIT = V7X_VMEM_BYTES * 3 // 4

TP = 1024
TM = 512
TQ = 256
ONES_ROWS = 16
EB = 512
WEIGHT_SLICES = 4
EXPERT_PART = EB // 4
ROW_ALIGN = 8
CHUNK_ROWS = (32, 16, 8)
SROWS = -(-(2 * TM + N_EXPERTS * (ROW_ALIGN - 1)) // MXU_DIM) * MXU_DIM
SORTED_RANGE = 2048
CHUNK_SLOTS = (SROWS // CHUNK_ROWS[0],) + (N_EXPERTS,) * (len(CHUNK_ROWS) - 1)
CHUNK_OFFSETS = tuple(sum(CHUNK_SLOTS[:k]) for k in range(len(CHUNK_ROWS)))
CHUNK_COUNTS = sum(CHUNK_SLOTS)
assert SROWS <= SORTED_RANGE and CHUNK_COUNTS + len(CHUNK_ROWS) <= LANES

C_DQ, C_DK, C_DV, C_SQ, C_SK, C_SV, C_END = 0, 512, 1024, 1536, 2048, 2176, 2304
NORM_GROUPS = (0, 1, 2, 3, 6, 7, 8)


def _cparams(sem):
    return pltpu.CompilerParams(dimension_semantics=sem, vmem_limit_bytes=VMEM_LIMIT)


def _t5_bucket_np(dist):
    n = np.maximum(dist, 0)
    max_exact = N_BUCKETS // 2
    nf = np.maximum(n, 1).astype(np.float32)
    large = max_exact + (np.log(nf / np.float32(max_exact)) / np.float32(math.log(MAX_DISTANCE / max_exact))
                         * np.float32(N_BUCKETS - max_exact)).astype(np.int32)
    large = np.minimum(large, N_BUCKETS - 1)
    return np.where(n < max_exact, n, large)


def _bias_tables(rel_bias, tq):
    nd = 2 * BLOCK
    buckets = _t5_bucket_np(np.arange(nd))
    assert (buckets[MAX_DISTANCE:] == N_BUCKETS - 1).all()
    rb = rel_bias.astype(F32)
    r = np.arange(BLOCK)[:, None]
    c = np.arange(BLOCK)[None, :]
    d_own = r - c
    d_prev = BLOCK + r - c
    far = rb[N_BUCKETS - 1]

    def take(dist, heads):
        idx = jnp.asarray(buckets[np.clip(dist, 0, nd - 1)], jnp.int32)[None]
        out = jnp.zeros((heads.stop - heads.start,) + dist.shape, F32)
        for b in range(N_BUCKETS):
            out = jnp.where(idx == b, rb[b, heads].reshape((-1,) + (1,) * dist.ndim), out)
        return out

    hd = slice(0, N_DIFF_HEADS)
    far_d = far[hd][:, None, None]
    d0 = jnp.where(d_own[None] >= 0, take(d_own, hd) - far_d, NEG)
    d1 = take(d_prev, hd) - far_d
    dblk = jnp.stack([d0, d1], axis=1)
    rq = np.arange(tq)[:, None]
    cm = np.arange(LANES)[None, :]
    d_meta = (N_META + rq - cm)[:, :N_META]
    bm0 = jnp.pad(take(d_meta, hd) - far_d, ((0, 0), (0, 0), (0, LANES - N_META)), constant_values=NEG)

    hs = slice(N_DIFF_HEADS, N_DIFF_HEADS + N_SWA_HEADS)
    far_s = far[hs][:, None, None]
    d_meta_s = N_META + r - cm
    meta_first = jnp.where((cm < N_META)[None], take(d_meta_s, hs), NEG)
    meta_rest = jnp.where((cm < N_META)[None], jnp.broadcast_to(far_s, (N_SWA_HEADS, BLOCK, LANES)), NEG)
    prev_rest = jnp.where((c > r)[None], take(d_prev, hs), NEG)
    prev_first = jnp.full((N_SWA_HEADS, BLOCK, BLOCK), NEG, F32)
    own = jnp.where((d_own >= 0)[None], take(d_own, hs), NEG)
    bt = jnp.stack([jnp.concatenate([prev_first, own, meta_first[..., :N_META]], axis=-1),
                    jnp.concatenate([prev_rest, own, meta_rest[..., :N_META]], axis=-1)], axis=0)
    return dblk.astype(F32), bm0.astype(F32), bt.astype(F32)


def _proj_kernel(x_ref, g1_ref, w_ref, bd_ref, gain_ref, nmask_ref, o_ref):
    x = x_ref[...]
    a = x * lax.rsqrt(jnp.mean(x * x, axis=-1, keepdims=True) + EPS) * g1_ref[...]
    p = jnp.dot(a.astype(BF16), w_ref[...], preferred_element_type=F32)
    bd = bd_ref[...]
    for j in range(C_END // MXU_DIM):
        sl = slice(j * MXU_DIM, (j + 1) * MXU_DIM)
        pj = p[:, sl]
        if j in NORM_GROUPS:
            ms = jnp.dot((pj * pj).astype(BF16), bd, preferred_element_type=F32)
            pj = jnp.where(nmask_ref[:, sl] != 0.0, pj * lax.rsqrt(ms + EPS) * gain_ref[:, sl], pj)
        o_ref[:, sl] = pj.astype(BF16)


def _proj(x2, g1, w, bd, gain, nmask, tm):
    n = x2.shape[0]
    return pl.pallas_call(
        _proj_kernel,
        out_shape=jax.ShapeDtypeStruct((n, C_END), BF16),
        grid=(n // tm,),
        in_specs=[
            pl.BlockSpec((tm, x2.shape[1]), lambda i: (i, 0)),
            pl.BlockSpec(g1.shape, lambda i: (0, 0)),
            pl.BlockSpec(w.shape, lambda i: (0, 0)),
            pl.BlockSpec(bd.shape, lambda i: (0, 0)),
            pl.BlockSpec(gain.shape, lambda i: (0, 0)),
            pl.BlockSpec(nmask.shape, lambda i: (0, 0)),
        ],
        out_specs=pl.BlockSpec((tm, C_END), lambda i: (i, 0)),
        compiler_params=_cparams(("parallel",)),
        name="proj",
    )(x2, g1, w, bd, gain, nmask)


def _diff_kernel(qi_tab, t_tab, q_ref, k_ref, v_ref, km_ref, vm_ref, d_ref, bm0_ref, lamv_ref, gain_ref, o_ref,
                 bias_ref, mb_ref, qs_ref, kt_ref, vt_ref, s_buf, p_buf, a_buf, m_ref, acc_ref, *,
                 lambda_init, n_steps, n_far, n_near):
    tq = TQ
    nq = q_ref.shape[0] // tq
    nb = tq // BLOCK
    BIAS_LEFT, BIAS_DIAG, BIAS_NONE = 0, 1, 2

    d0 = d_ref[0, 0] * LOG2E
    d1 = d_ref[0, 1] * LOG2E
    zeros = jnp.zeros((BLOCK, BLOCK), F32)
    for a in range(nb):
        for b in range(nb):
            rs, cs = slice(a * BLOCK, (a + 1) * BLOCK), slice(b * BLOCK, (b + 1) * BLOCK)
            if a == b:
                blk = d0
            elif b == a + 1:
                blk = d1
            elif b > a:
                blk = zeros
            else:
                blk = jnp.full((BLOCK, BLOCK), NEG, F32)
            bias_ref[BIAS_DIAG, rs, cs] = blk
            bias_ref[BIAS_LEFT, rs, cs] = d1 if (b == 0 and a == nb - 1) else zeros
    bias_ref[BIAS_NONE] = jnp.zeros((tq, tq), F32)
    mb_ref[0] = jnp.zeros((N_META, tq), F32)
    mb_ref[1] = bm0_ref[0, :N_META, :] * LOG2E

    lane = lax.broadcasted_iota(jnp.int32, (tq, LANES), 1)
    for i in range(nq):
        rows = slice(i * tq, (i + 1) * tq)
        q = q_ref[rows, :].astype(F32)
        qs_ref[i] = jnp.transpose(jnp.concatenate([jnp.where(lane < HEAD_DIM, q, 0.0),
                                                   jnp.where(lane >= HEAD_DIM, q, 0.0)], axis=0)).astype(BF16)
        vt_ref[i, :LANES, :] = jnp.transpose(v_ref[rows, :].astype(F32)).astype(BF16)
        kt_ref[i] = k_ref[rows, :]
    vt_ref[nq, :LANES, :] = jnp.transpose(vm_ref[...].astype(F32)).astype(BF16)
    kt_ref[nq] = km_ref[...]
    vt_ref[:, LANES:, :] = jnp.ones((nq + 1, ONES_ROWS, tq), BF16)
    acc_ref[...] = jnp.zeros(acc_ref.shape, F32)
    m_ref[...] = jnp.full(m_ref.shape, NEG, F32)
    lv = lamv_ref[...]
    lam = (jnp.exp(jnp.sum(lv[0:1] * lv[1:2], axis=-1, keepdims=True))
           - jnp.exp(jnp.sum(lv[2:3] * lv[3:4], axis=-1, keepdims=True)) + lambda_init)

    FAR, NEAR, META = 0, 1, 2

    def stage_a(n, slot, kind):
        qi, t = qi_tab[n], t_tab[n]
        if kind == META:
            s = jnp.dot(kt_ref[nq, :N_META, :], qs_ref[qi], preferred_element_type=F32)
            s_buf[slot, :N_META] = s + jnp.tile(mb_ref[jnp.where(qi == 0, 1, 0)], (1, 2))
            return
        s = jnp.dot(kt_ref[t - 1], qs_ref[qi], preferred_element_type=F32)
        if kind == NEAR:
            which = jnp.where(t == qi + 1, BIAS_DIAG, jnp.where(t == qi, BIAS_LEFT, BIAS_NONE))
            s = s + jnp.tile(bias_ref[which], (1, 2))
        s_buf[slot] = s

    def stage_b(n, slot, kind):
        qi = qi_tab[n]
        rows = slice(0, N_META if kind == META else tq)
        s = s_buf[slot, rows]
        m_prev = m_ref[qi]
        m_new = jnp.maximum(m_prev, jnp.max(s, axis=0, keepdims=True))
        a_buf[slot] = jnp.exp2(m_prev - m_new)
        p_buf[slot, rows] = jnp.exp2(s - m_new[0:1]).astype(BF16)
        m_ref[qi] = m_new

    def stage_c(n, slot, kind):
        qi, t = qi_tab[n], t_tab[n]
        if kind == META:
            pv = jnp.dot(vt_ref[nq, :, :N_META], p_buf[slot, :N_META], preferred_element_type=F32)
        else:
            pv = jnp.dot(vt_ref[t - 1], p_buf[slot], preferred_element_type=F32)
        acc_ref[qi] = a_buf[slot][0:1] * acc_ref[qi] + pv

    LEAD, SLOTS, UNROLL = 2, 3, 12
    assert UNROLL % SLOTS == 0

    def pipeline(base, count, kind, biased_from=None):
        def kind_a(j):
            return kind if biased_from is None or j < biased_from else NEAR

        if count <= 2 * LEAD:
            for j in range(count):
                stage_a(base + j, 0, kind_a(j))
                stage_b(base + j, 0, kind)
                stage_c(base + j, 0, kind)
            return
        for j in range(2 * LEAD):
            stage_a(base + j, j % SLOTS, kind_a(j))
            if j >= LEAD:
                stage_b(base + j - LEAD, (j - LEAD) % SLOTS, kind)

        def steps(n, first, count):
            for j in range(count):
                stage_a(base + n + j + 2 * LEAD, (j + 2 * LEAD) % SLOTS, kind_a(first + j + 2 * LEAD))
                stage_b(base + n + j + LEAD, (j + LEAD) % SLOTS, kind)
                stage_c(base + n + j, j % SLOTS, kind)

        n_steady = count - 2 * LEAD
        n_blocks = n_steady // UNROLL
        switch = n_blocks if biased_from is None else (biased_from - 2 * LEAD) // UNROLL
        assert biased_from is None or (biased_from - 2 * LEAD) % UNROLL == 0
        for lo, hi in ((0, min(switch, n_blocks)), (min(switch, n_blocks), n_blocks)):
            lax.fori_loop(lo, hi, lambda k, carry, lo=lo: (steps(UNROLL * k, UNROLL * lo, UNROLL), carry)[1], 0)
        steps(UNROLL * n_blocks, UNROLL * n_blocks, n_steady - UNROLL * n_blocks)
        for j in range(n_steady, count):
            if j + LEAD < count:
                stage_b(base + j + LEAD, (j + LEAD) % SLOTS, kind)
            stage_c(base + j, j % SLOTS, kind)

    if n_far >= 2 * LEAD:
        pipeline(0, n_far + n_near, FAR, 2 * LEAD + UNROLL * ((n_far - 2 * LEAD) // UNROLL))
    else:
        pipeline(0, n_far + n_near, NEAR)
    pipeline(n_far + n_near, n_steps - n_far - n_near, META)

    for i in range(nq):
        acc = acc_ref[i]
        o = acc[:LANES] * (1.0 / acc[LANES:LANES + 1])
        d = o[:, :tq] - lam * o[:, tq:]
        y = d * lax.rsqrt(jnp.mean(d * d, axis=0, keepdims=True) + EPS) * jnp.tile(gain_ref[...], (1, tq // LANES))
        o_ref[i * tq:(i + 1) * tq, :] = jnp.transpose(y * (1.0 - lambda_init)).astype(BF16)


def _diff_attention(qkv, km, vm, dblk, bm0, lamv, gain, batch, seq, lambda_init):
    nq = seq // TQ
    far = [(qi, t) for qi in range(nq) for t in range(1, qi)]
    near = [(qi, t) for qi in range(nq) for t in (qi, qi + 1) if t >= 1]
    meta = [(qi, 0) for qi in range(nq)]
    steps = far + near + meta
    qi_tab = jnp.asarray([s[0] for s in steps], jnp.int32)
    t_tab = jnp.asarray([s[1] for s in steps], jnp.int32)
    kern = functools.partial(_diff_kernel, lambda_init=lambda_init, n_steps=len(steps), n_far=len(far),
                             n_near=len(near))
    return pl.pallas_call(
        kern,
        out_shape=jax.ShapeDtypeStruct((batch * seq, N_DIFF_HEADS * LANES), BF16),
        grid_spec=pltpu.PrefetchScalarGridSpec(
            num_scalar_prefetch=2,
            grid=(batch, N_DIFF_HEADS),
            in_specs=[
                pl.BlockSpec((seq, LANES), lambda b, h, *_: (b, C_DQ // LANES + h)),
                pl.BlockSpec((seq, LANES), lambda b, h, *_: (b, C_DK // LANES + h)),
                pl.BlockSpec((seq, LANES), lambda b, h, *_: (b, C_DV // LANES + h)),
                pl.BlockSpec((TQ, LANES), lambda b, h, *_: (0, h)),
                pl.BlockSpec((TQ, LANES), lambda b, h, *_: (0, h)),
                pl.BlockSpec((1, 2, BLOCK, BLOCK), lambda b, h, *_: (h, 0, 0, 0)),
                pl.BlockSpec((1, LANES, TQ), lambda b, h, *_: (h, 0, 0)),
                pl.BlockSpec(lamv.shape, lambda b, h, *_: (0, 0)),
                pl.BlockSpec((LANES, LANES), lambda b, h, *_: (0, 0)),
            ],
            out_specs=pl.BlockSpec((seq, LANES), lambda b, h, *_: (b, h)),
            scratch_shapes=[
                pltpu.VMEM((3, TQ, TQ), F32),
                pltpu.VMEM((2, N_META, TQ), F32),
                pltpu.VMEM((nq, LANES, 2 * TQ), BF16),
                pltpu.VMEM((nq + 1, TQ, LANES), BF16),
                pltpu.VMEM((nq + 1, LANES + ONES_ROWS, TQ), BF16),
                pltpu.VMEM((3, TQ, 2 * TQ), F32),
                pltpu.VMEM((3, TQ, 2 * TQ), BF16),
                pltpu.VMEM((3, 8, 2 * TQ), F32),
                pltpu.VMEM((nq, 8, 2 * TQ), F32),
                pltpu.VMEM((nq, LANES + ONES_ROWS, 2 * TQ), F32),
            ],
        ),
        compiler_params=_cparams(("parallel", "parallel")),
        name="diff_attention",
    )(qi_tab, t_tab, qkv, qkv, qkv, km, vm, jnp.swapaxes(dblk, -1, -2), jnp.swapaxes(bm0, -1, -2), lamv,
      jnp.broadcast_to(gain.reshape(LANES, 1), (LANES, LANES)))


def _swa_kernel(sink_ref, q_ref, k_ref, v_ref, km_ref, vm_ref, bt_ref, o_ref, kd_ref, vt_ref,
                s_scr, p_scr, inv_scr):
    nkb = k_ref.shape[0] // BLOCK
    lane = lax.broadcasted_iota(jnp.int32, (BLOCK, LANES), 1)
    pairs = [(g, u) for g in range(N_SWA_KV) for u in range(2)]

    def both_halves(k):
        k0, k1 = k[:, :HEAD_DIM], k[:, HEAD_DIM:]
        return jnp.concatenate([k0, k0, k1, k1], axis=1)

    def prepare(j, carry):
        rows = pl.ds(pl.multiple_of(j * BLOCK, BLOCK), BLOCK)
        kd_ref[j] = both_halves(k_ref[rows, :])
        vt_ref[j] = jnp.transpose(v_ref[rows, :].astype(F32)).astype(BF16)
        return carry
    lax.fori_loop(0, nkb, prepare, 0)
    kd_ref[nkb] = both_halves(km_ref[...])
    vt_ref[nkb] = jnp.transpose(vm_ref[...].astype(F32)).astype(BF16)

    def scores(n, slot):
        first = jnp.where(n == 0, 0, 1)
        prev = jnp.maximum(n - 1, 0)
        r_q = pl.multiple_of(n * BLOCK, BLOCK)
        for c, (g, u) in enumerate(pairs):
            ks = slice(g * LANES, (g + 1) * LANES)
            kcat = jnp.concatenate([kd_ref[prev, :, ks], kd_ref[n, :, ks], kd_ref[nkb, :N_META, ks]], axis=0)
            h0 = 4 * g + 2 * u
            qp = q_ref[pl.ds(r_q, BLOCK), (2 * g + u) * LANES:(2 * g + u + 1) * LANES].astype(F32)
            qs = jnp.transpose(jnp.concatenate([jnp.where(lane < HEAD_DIM, qp, 0.0),
                                                jnp.where(lane >= HEAD_DIM, qp, 0.0)], axis=0)).astype(BF16)
            s = jnp.dot(kcat, qs, preferred_element_type=F32)
            s_scr[slot, c] = s + jnp.concatenate([bt_ref[first, h0], bt_ref[first, h0 + 1]], axis=1)

    def exponentials(slot):
        for c, (g, u) in enumerate(pairs):
            h0 = 4 * g + 2 * u
            s = s_scr[slot, c]
            sink = jnp.concatenate([sink_ref[h0:h0 + 1, :], sink_ref[h0 + 1:h0 + 2, :]], axis=1)
            m = jnp.maximum(jnp.max(s, axis=0, keepdims=True), sink)
            p = jnp.exp(s - m)
            p_scr[slot, c] = p.astype(BF16)
            inv_scr[slot, c] = jnp.broadcast_to(1.0 / (jnp.sum(p, axis=0, keepdims=True) + jnp.exp(sink - m)),
                                                inv_scr.shape[2:])

    def values(n, slot):
        prev = jnp.maximum(n - 1, 0)
        r_q = pl.multiple_of(n * BLOCK, BLOCK)
        for c, (g, u) in enumerate(pairs):
            vs = slice(g * HEAD_DIM, (g + 1) * HEAD_DIM)
            vcat = jnp.concatenate([vt_ref[prev, vs, :], vt_ref[n, vs, :]], axis=1)
            o = (jnp.dot(vcat, p_scr[slot, c, :2 * BLOCK], preferred_element_type=F32)
                 + jnp.dot(vt_ref[nkb, vs, :N_META], p_scr[slot, c, 2 * BLOCK:], preferred_element_type=F32)
                 ) * inv_scr[slot, c][0:1]
            ot = jnp.transpose(o)
            o_ref[pl.ds(r_q, BLOCK), (2 * g + u) * LANES:(2 * g + u + 1) * LANES] = (
                jnp.concatenate([ot[:BLOCK], ot[BLOCK:]], axis=1).astype(BF16))

    scores(0, 0)
    scores(1, 1)
    exponentials(0)

    def blocks(n, count):
        for j in range(count):
            scores(n + j + 2, j % 2)
            exponentials((j + 1) % 2)
            values(n + j, j % 2)

    unroll = 4
    n_steady = nkb - 2
    lax.fori_loop(0, n_steady // unroll, lambda k, carry: (blocks(unroll * k, unroll), carry)[1], 0)
    blocks(n_steady // unroll * unroll, n_steady % unroll)
    exponentials(1)
    values(nkb - 2, 0)
    values(nkb - 1, 1)


def _swa_attention(sinks, qkv, km, vm, bt, batch, seq):
    nkb = seq // BLOCK
    swa_q = N_SWA_HEADS * HEAD_DIM
    assert nkb % 2 == 0
    sinkv =jnp.broadcast_to(sinks.reshape(N_SWA_HEADS, 1), (N_SWA_HEADS, LANES))
    return pl.pallas_call(
        _swa_kernel,
        out_shape=jax.ShapeDtypeStruct((batch * seq, N_SWA_HEADS * HEAD_DIM), BF16),
        grid=(batch,),
        in_specs=[
            pl.BlockSpec(sinkv.shape, lambda b: (0, 0)),
            pl.BlockSpec((seq, swa_q), lambda b: (b, C_SQ // swa_q)),
            pl.BlockSpec((seq, LANES), lambda b: (b, C_SK // LANES)),
            pl.BlockSpec((seq, LANES), lambda b: (b, C_SV // LANES)),
            pl.BlockSpec(km.shape, lambda b: (0, 0)),
            pl.BlockSpec(vm.shape, lambda b: (0, 0)),
            pl.BlockSpec(bt.shape, lambda b: (0, 0, 0, 0)),
        ],
        out_specs=pl.BlockSpec((seq, swa_q), lambda b: (b, 0)),
        scratch_shapes=[pltpu.VMEM((nkb + 1, BLOCK, 2 * LANES), BF16),
                        pltpu.VMEM((nkb + 1, LANES, BLOCK), BF16),
                        pltpu.VMEM((2, 4, 2 * BLOCK + N_META, 2 * BLOCK), F32),
                        pltpu.VMEM((2, 4, 2 * BLOCK + N_META, 2 * BLOCK), BF16),
                        pltpu.VMEM((2, 4, 8, 2 * BLOCK), F32)],
        compiler_params=_cparams(("parallel",)),
        name="swa_attention",
    )(sinkv, qkv, qkv, qkv, km, vm, bt)


def _outproj_kernel(x_ref, md_ref, ms_ref, wo_ref, g2_ref, wr_ref, br_ref,
                    h_ref, hb_ref, rt_ref, ti_ref, cnt_ref, c_ref, lg_ref):
    i = pl.program_id(0)

    @pl.when(i == 0)
    def _init():
        c_ref[...] = jnp.zeros(c_ref.shape, F32)
        lg_ref[...] = jnp.zeros(lg_ref.shape, F32)

    lg_prev = lg_ref[...]
    half = md_ref.shape[1]
    h = (x_ref[...]
         + jnp.dot(md_ref[...], wo_ref[:half, :], preferred_element_type=F32)
         + jnp.dot(ms_ref[...], wo_ref[half:, :], preferred_element_type=F32))
    h_ref[...] = h
    hn = h * lax.rsqrt(jnp.mean(h * h, axis=-1, keepdims=True) + EPS) * g2_ref[...]
    hb = hn.astype(BF16)
    hb_ref[...] = hb
    lg_ref[...] = jnp.dot(hb, wr_ref[...], preferred_element_type=F32) + br_ref[...]
    _route_tile(lg_prev, jnp.where(i > 0, 1.0, 0.0), rt_ref, ti_ref, c_ref)

    @pl.when(i == pl.num_programs(0) - 1)
    def _fin():
        cnt_ref[...] = c_ref[...]


def _route_tile(lg, live, rt_ref, ti_ref, c_ref):
    tm = lg.shape[0]
    lane_i = lax.broadcasted_iota(jnp.int32, lg.shape, 1)
    lane = lane_i.astype(F32)
    big = float(4 * LANES)
    is_g = (lane_i >= N_EXPERTS) & (lane_i < N_EXPERTS + N_GROUPS)
    glm = jnp.where(is_g, lg, -jnp.inf)
    gmax = jnp.max(glm, axis=1, keepdims=True)
    gidx = jnp.min(jnp.where(glm == gmax, lane, big), axis=1, keepdims=True) - N_EXPERTS
    gsum = jnp.sum(jnp.where(is_g, jnp.exp(lg - gmax), 0.0), axis=1, keepdims=True)
    g_w = 1.0 / gsum
    lane_grp = (lane_i >> 3).astype(F32)
    in_grp = (lane_i < N_EXPERTS) & (lane_grp == gidx)
    el = jnp.where(in_grp, lg, -jnp.inf)
    t1 = jnp.max(el, axis=1, keepdims=True)
    j1 = jnp.min(jnp.where(el == t1, lane, big), axis=1, keepdims=True)
    el2 = jnp.where(lane == j1, -jnp.inf, el)
    t2 = jnp.max(el2, axis=1, keepdims=True)
    j2 = jnp.min(jnp.where(el2 == t2, lane, big), axis=1, keepdims=True)
    e2 = jnp.exp(t2 - t1)
    den = 1.0 + e2
    gate1 = g_w / den
    gate2 = g_w * e2 / den

    o1 = lane == j1
    o2 = lane == j2
    onehot = jnp.where(o1 | o2, 1.0, 0.0).astype(BF16)
    rr = lax.broadcasted_iota(jnp.int32, (tm, tm), 0)
    cc = lax.broadcasted_iota(jnp.int32, (tm, tm), 1)
    lower = jnp.where(rr > cc, 1.0, 0.0).astype(BF16)
    pfx = jnp.dot(lower, onehot, preferred_element_type=F32)
    cnt_tile = jnp.sum(onehot.astype(F32), axis=0, keepdims=True)
    groups = jnp.floor((cnt_tile + (ROW_ALIGN - 1)) * (1.0 / ROW_ALIGN))
    er = lax.broadcasted_iota(jnp.int32, (LANES, LANES), 0)
    ec = lax.broadcasted_iota(jnp.int32, (LANES, LANES), 1)
    before = jnp.where(er < ec, 1.0, 0.0).astype(BF16)
    cbase = ROW_ALIGN * jnp.dot(jnp.broadcast_to(groups, (8, LANES)).astype(BF16), before,
                                preferred_element_type=F32)[0:1]
    at = pfx + cbase
    pos1 = jnp.sum(jnp.where(o1, at, 0.0), axis=1, keepdims=True)
    pos2 = jnp.sum(jnp.where(o2, at, 0.0), axis=1, keepdims=True)
    rt_ref[...] = jnp.where(lane_i == 0, gate1,
                            jnp.where(lane_i == 1, gate2,
                                      jnp.where(lane_i == 2, pos1,
                                                jnp.where(lane_i == 3, pos2, 0.0))))
    c_old = c_ref[...]
    c_ref[...] = c_old + groups * (ROW_ALIGN * live)
    row8 = lax.broadcasted_iota(jnp.int32, (8, LANES), 0)
    ti_ref[...] = jnp.where(row8 == 0, cnt_tile, jnp.where(row8 == 1, c_old, 0.0))


def _outproj(x2, mixd, mixs, wo, g2, wr, br):
    n, d = x2.shape
    nt = n // TM

    def proj_tile(i):
        return (jnp.minimum(i, nt - 1), 0)

    def route_tile(i):
        return (jnp.maximum(i - 1, 0), 0)

    return pl.pallas_call(
        _outproj_kernel,
        out_shape=(jax.ShapeDtypeStruct((n, d), F32),
                   jax.ShapeDtypeStruct((n, d), BF16),
                   jax.ShapeDtypeStruct((n, LANES), F32),
                   jax.ShapeDtypeStruct((nt * 8, LANES), F32),
                   jax.ShapeDtypeStruct((8, LANES), F32)),
        grid=(nt + 1,),
        in_specs=[
            pl.BlockSpec((TM, d), proj_tile),
            pl.BlockSpec((TM, mixd.shape[1]), proj_tile),
            pl.BlockSpec((TM, mixs.shape[1]), proj_tile),
            pl.BlockSpec(wo.shape, lambda i: (0, 0)),
            pl.BlockSpec(g2.shape, lambda i: (0, 0)),
            pl.BlockSpec(wr.shape, lambda i: (0, 0)),
            pl.BlockSpec(br.shape, lambda i: (0, 0)),
        ],
        out_specs=(pl.BlockSpec((TM, d), proj_tile),
                   pl.BlockSpec((TM, d), proj_tile),
                   pl.BlockSpec((TM, LANES), route_tile),
                   pl.BlockSpec((8, LANES), route_tile),
                   pl.BlockSpec((8, LANES), lambda i: (0, 0))),
        scratch_shapes=[pltpu.VMEM((8, LANES), F32), pltpu.VMEM((TM, LANES), F32)],
        compiler_params=_cparams(("arbitrary",)),
        name="outproj_router",
    )(x2, mixd, mixs, wo, g2, wr, br)


def _chunk_table(run_start, run_groups):
    nt = run_start.shape[0]
    sorted_start = (jnp.cumsum(run_groups, axis=1) - run_groups) * ROW_ALIGN
    word = run_start * SORTED_RANGE + sorted_start
    done = jnp.zeros_like(run_groups)
    segments, counts = [], []
    for k, rows in enumerate(CHUNK_ROWS):
        per_run = run_groups // (rows // ROW_ALIGN)
        if k > 0:
            per_run = per_run % 2
        last = jnp.cumsum(per_run, axis=1)[:, None, :]
        first = last - per_run[:, None, :]
        slot = jnp.arange(CHUNK_SLOTS[k], dtype=jnp.int32)[None, :, None]
        value = word[:, None, :] + (done[:, None, :] + (slot - first) * rows) * (SORTED_RANGE + 1)
        segments.append(jnp.sum(jnp.where((first <= slot) & (slot < last), value, 0), axis=2))
        counts.append(last[:, 0, -1:])
        done = done + per_run * rows
    pad = jnp.zeros((nt, LANES - CHUNK_COUNTS - len(CHUNK_ROWS)), jnp.int32)
    return jnp.concatenate(segments + counts + [pad], axis=1).astype(jnp.int32).reshape(nt, 1, LANES)


def _for_each_chunk(chunks_ref, fn):
    def issue(k, c, priority):
        word = chunks_ref[0, 0, CHUNK_OFFSETS[k] + c]
        fn(pl.multiple_of(word >> (SORTED_RANGE.bit_length() - 1), ROW_ALIGN),
           pl.multiple_of(word & (SORTED_RANGE - 1), ROW_ALIGN), CHUNK_ROWS[k], priority)

    for k in range(len(CHUNK_ROWS)):
        _in_pairs(chunks_ref[0, 0, CHUNK_COUNTS + k], lambda c, second, k=k: issue(k, c, second))


def _in_pairs(count, fn):
    half = count // 2

    def body(c, carry):
        fn(c, 0)
        fn(half + c, 1)
        return carry
    lax.fori_loop(0, half, body, 0)

    @pl.when(count % 2 == 1)
    def _():
        fn(count - 1, 0)


def _wait_chunks(chunks_ref, make_copy):
    for k, rows in enumerate(CHUNK_ROWS):
        _in_pairs(chunks_ref[0, 0, CHUNK_COUNTS + k], lambda c, second, rows=rows: make_copy(rows).wait())


def _dispatch_kernel(zf_ref, cur_ref, prv_ref, hb_ref, rt_ref, xs_ref, sbuf, zbuf, sem, zsem):
    i = pl.program_id(0)
    nt = pl.num_programs(0)
    slot = i % 2
    tm, d = hb_ref.shape

    def for_zero_blocks(kind, fn):
        def body(b, carry):
            @pl.when(zf_ref[b] == kind)
            def _():
                fn(pltpu.make_async_copy(zbuf, xs_ref.at[pl.ds(pl.multiple_of(b * EB, EB), EB)],
                                         zsem.at[kind - 1]))
            return carry
        lax.fori_loop(0, zf_ref.shape[0], body, 0)

    @pl.when(i == 0)
    def _():
        zbuf[...] = jnp.zeros(zbuf.shape, zbuf.dtype)
        for_zero_blocks(1, lambda c: c.start())
        for_zero_blocks(2, lambda c: c.start())
        for_zero_blocks(1, lambda c: c.wait())

    pos_t = jnp.transpose(rt_ref[...])
    srow = lax.broadcasted_iota(jnp.int32, (SROWS, tm), 0).astype(F32)
    sel = jnp.where(srow == pos_t[2:3, :], 1.0, jnp.where(srow == pos_t[3:4, :], 1.0, 0.0)).astype(BF16)
    srt = jnp.dot(sel, hb_ref[...], preferred_element_type=F32)
    bits = pltpu.bitcast(srt, jnp.uint32)
    sbuf[slot] = (bits[:, d // 2:] & jnp.uint32(0xFFFF0000)) | (bits[:, :d // 2] >> 16)

    def chunk_copy(run_row, sorted_row, rows, sl):
        return pltpu.make_async_copy(sbuf.at[sl, pl.ds(sorted_row, rows)], xs_ref.at[pl.ds(run_row, rows)],
                                     sem.at[sl])

    _for_each_chunk(cur_ref, lambda run_row, sorted_row, rows, priority:
                    chunk_copy(run_row, sorted_row, rows, slot).start(priority=priority))

    @pl.when(i > 0)
    def _():
        _wait_chunks(prv_ref, lambda rows: chunk_copy(0, 0, rows, 1 - slot))

    @pl.when(i == nt - 1)
    def _():
        _wait_chunks(cur_ref, lambda rows: chunk_copy(0, 0, rows, slot))
        for_zero_blocks(2, lambda c: c.wait())


def _dispatch(zero_blocks, runs, hb, rt, n_rows):
    n, d = hb.shape
    return pl.pallas_call(
        _dispatch_kernel,
        out_shape=jax.ShapeDtypeStruct((n_rows, d // 2), jnp.uint32),
        grid_spec=pltpu.PrefetchScalarGridSpec(
            num_scalar_prefetch=1,
            grid=(n // TM,),
            in_specs=[
                pl.BlockSpec((1, 1, LANES), lambda i, zf: (i, 0, 0), memory_space=pltpu.SMEM),
                pl.BlockSpec((1, 1, LANES), lambda i, zf: (jnp.maximum(i - 1, 0), 0, 0), memory_space=pltpu.SMEM),
                pl.BlockSpec((TM, d), lambda i, zf: (i, 0)),
                pl.BlockSpec((TM, LANES), lambda i, zf: (i, 0)),
            ],
            out_specs=pl.BlockSpec(memory_space=pl.ANY),
            scratch_shapes=[pltpu.VMEM((2, SROWS, d // 2), jnp.uint32), pltpu.VMEM((EB, d // 2), jnp.uint32),
                            pltpu.SemaphoreType.DMA((2,)), pltpu.SemaphoreType.DMA((2,))],
        ),
        compiler_params=_cparams(("arbitrary",)),
        name="dispatch",
    )(zero_blocks, runs, runs, hb, rt)


def _experts_kernel(be_ref, na_ref, nxt_ref, rows_ref, xs_ref, wg_hbm, wu_hbm, wd_hbm, ys_ref,
                    wgf, wuf, wdf, wgb, wub, wdb, sem):
    b = pl.program_id(0)

    def weight_copies(e):
        copies = []
        for k, (src, dst) in enumerate(((wg_hbm, wgf), (wu_hbm, wuf), (wd_hbm, wdf))):
            rows = dst.shape[0] // WEIGHT_SLICES
            copies += [pltpu.make_async_copy(src.at[e, pl.ds(j * rows, rows)], dst.at[pl.ds(j * rows, rows)],
                                             sem.at[k]) for j in range(WEIGHT_SLICES)]
        return copies

    def start_weights(e):
        for j, c in enumerate(weight_copies(e)):
            c.start(priority=j % 2)

    @pl.when(b == 0)
    def _():
        start_weights(be_ref[0])

    @pl.when(b < na_ref[0])
    def _():
        e = be_ref[b]
        changed = jnp.logical_or(b == 0, be_ref[jnp.maximum(b - 1, 0)] != e)

        @pl.when(changed)
        def _load():
            for c in weight_copies(e):
                c.wait()
            wgb[...] = wgf[...].astype(BF16)
            wub[...] = wuf[...].astype(BF16)
            wdb[...] = wdf[...].astype(BF16)
            nxt = nxt_ref[e]

            @pl.when(nxt >= 0)
            def _():
                start_weights(nxt)

    def compute(rows):
        w = xs_ref[rows, :]
        x_lo = pltpu.bitcast(w << 16, F32).astype(BF16)
        x_hi = pltpu.bitcast(w & jnp.uint32(0xFFFF0000), F32).astype(BF16)
        dh = w.shape[1]
        g = (jnp.dot(x_lo, wgb[:dh, :], preferred_element_type=F32)
             + jnp.dot(x_hi, wgb[dh:, :], preferred_element_type=F32))
        u = (jnp.dot(x_lo, wub[:dh, :], preferred_element_type=F32)
             + jnp.dot(x_hi, wub[dh:, :], preferred_element_type=F32))
        hdn = g * (1.0 / (1.0 + jnp.exp(-g))) * u
        y = jnp.dot(hdn.astype(BF16), wdb[...], preferred_element_type=F32)
        bits = pltpu.bitcast(y.astype(BF16).astype(F32), jnp.uint32)
        ys_ref[rows, :] = (bits[:, dh:] & jnp.uint32(0xFFFF0000)) | (bits[:, :dh] >> 16)

    n_rows = rows_ref[b]

    part = EXPERT_PART
    whole = n_rows > 3 * part
    for lo, size, cond in ((0, 4 * part, whole),
                           (0, 2 * part, jnp.logical_and(n_rows > part, n_rows <= 3 * part)),
                           (0, part, jnp.logical_and(n_rows > 0, n_rows <= part)),
                           (2 * part, part, jnp.logical_and(n_rows > 2 * part, n_rows <= 3 * part))):
        @pl.when(cond)
        def _(lo=lo, size=size):
            compute(pl.ds(lo, size))

    for k in range(4):
        @pl.when(n_rows <= k * part)
        def _(k=k):
            ys_ref[pl.ds(k * part, part), :] = jnp.zeros((part, ys_ref.shape[1]), ys_ref.dtype)


def _experts(blk_e, n_act, nxt_e, blk_rows, xs, w_gate, w_up, w_down):
    p, dh = xs.shape
    d = 2 * dh
    de = w_gate.shape[2]

    def row_map(b, be, na, nx, br):
        return (jnp.minimum(b, na[0] - 1), 0)

    return pl.pallas_call(
        _experts_kernel,
        out_shape=jax.ShapeDtypeStruct((p, dh), jnp.uint32),
        grid_spec=pltpu.PrefetchScalarGridSpec(
            num_scalar_prefetch=4,
            grid=(p // EB,),
            in_specs=[
                pl.BlockSpec((EB, dh), row_map),
                pl.BlockSpec(memory_space=pl.ANY),
                pl.BlockSpec(memory_space=pl.ANY),
                pl.BlockSpec(memory_space=pl.ANY),
            ],
            out_specs=pl.BlockSpec((EB, dh), lambda b, be, na, nx, br: (b, 0)),
            scratch_shapes=[pltpu.VMEM((d, de), F32), pltpu.VMEM((d, de), F32), pltpu.VMEM((de, d), F32),
                            pltpu.VMEM((d, de), BF16), pltpu.VMEM((d, de), BF16), pltpu.VMEM((de, d), BF16),
                            pltpu.SemaphoreType.DMA((3,))],
        ),
        compiler_params=_cparams(("arbitrary",)),
        name="experts",
    )(blk_e, n_act, nxt_e, blk_rows, xs, w_gate, w_up, w_down)


def _combine_kernel(cur_ref, nxt_ref, ys_ref, h_ref, rt_ref, o_ref, ybuf, sem):
    i = pl.program_id(0)
    nt = pl.num_programs(0)
    slot = i % 2
    tm = h_ref.shape[0]

    def chunk_copy(run_row, sorted_row, rows, sl):
        return pltpu.make_async_copy(ys_ref.at[pl.ds(run_row, rows)], ybuf.at[sl, pl.ds(sorted_row, rows)],
                                     sem.at[sl])

    @pl.when(i == 0)
    def _():
        ybuf[...] = jnp.zeros(ybuf.shape, ybuf.dtype)
        _for_each_chunk(cur_ref, lambda run_row, sorted_row, rows, priority:
                        chunk_copy(run_row, sorted_row, rows, 0).start(priority=priority))

    @pl.when(i + 1 < nt)
    def _():
        _for_each_chunk(nxt_ref, lambda run_row, sorted_row, rows, priority:
                        chunk_copy(run_row, sorted_row, rows, 1 - slot).start(priority=priority))

    _wait_chunks(cur_ref, lambda rows: chunk_copy(0, 0, rows, slot))

    rt = rt_ref[...]
    w = ybuf[slot]
    dh = w.shape[1]
    y_lo = pltpu.bitcast(w << 16, F32).astype(BF16)
    y_hi = pltpu.bitcast(w & jnp.uint32(0xFFFF0000), F32).astype(BF16)
    col = lax.broadcasted_iota(jnp.int32, (tm, SROWS), 1).astype(F32)
    wsel = jnp.where(col == rt[:, 2:3], rt[:, 0:1], jnp.where(col == rt[:, 3:4], rt[:, 1:2], 0.0)).astype(BF16)
    for half, yb in ((slice(0, dh), y_lo), (slice(dh, 2 * dh), y_hi)):
        o_ref[:, half] = h_ref[:, half] + jnp.dot(wsel, yb, preferred_element_type=F32)


def _combine(runs, ys, h1, rt):
    n, d = h1.shape
    nt = n // TM
    return pl.pallas_call(
        _combine_kernel,
        out_shape=jax.ShapeDtypeStruct((n, d), F32),
        grid=(nt,),
        in_specs=[
            pl.BlockSpec((1, 1, LANES), lambda i: (i, 0, 0), memory_space=pltpu.SMEM),
            pl.BlockSpec((1, 1, LANES), lambda i: (jnp.minimum(i + 1, nt - 1), 0, 0), memory_space=pltpu.SMEM),
            pl.BlockSpec(memory_space=pl.ANY),
            pl.BlockSpec((TM, d), lambda i: (i, 0)),
            pl.BlockSpec((TM, LANES), lambda i: (i, 0)),
        ],
        out_specs=pl.BlockSpec((TM, d), lambda i: (i, 0)),
        scratch_shapes=[pltpu.VMEM((2, SROWS, d // 2), jnp.uint32), pltpu.SemaphoreType.DMA((2,))],
        compiler_params=_cparams(("arbitrary",)),
        name="combine",
    )(runs, runs, ys, h1, rt)


def kernel(x, meta_tokens, rel_bias, norm1_gain, w_in, diff_q_gain, diff_k_gain, lam_q1, lam_k1, lam_q2, lam_k2, diff_subln_gain, swa_q_gain, swa_k_gain, swa_sinks, w_out, norm2_gain, w_group, b_group, w_router, b_router, w_gate, w_up, w_down):
    batch, seq, d = x.shape
    depth = w_in.shape[0]
    n = batch * seq
    assert seq % TQ == 0 and n % TM == 0 and n % TP == 0 and d == 1024
    assert meta_tokens.shape[0] == N_META
    assert depth == 1, "the meta-token rows of the residual stream are not carried across layers"

    h = x.reshape(n, d)
    dblk, bm0, bt = _bias_tables(rel_bias, TQ)
    scale = HEAD_DIM ** -0.5
    bd = jnp.asarray(np.kron(np.eye(MXU_DIM // HEAD_DIM), np.full((HEAD_DIM, HEAD_DIM), 1.0 / HEAD_DIM)), BF16)
    ones = jnp.ones((HEAD_DIM,), F32)
    lower_pad = N_EXPERTS + N_GROUPS

    for layer in range(depth):
        lambda_init = 0.8 - 0.6 * math.exp(-0.3 * layer)
        w_cat = w_in[layer].astype(BF16)
        gain = jnp.concatenate([
            jnp.tile(diff_q_gain[layer] * (scale * LOG2E), 2 * N_DIFF_HEADS),
            jnp.tile(diff_k_gain[layer], 2 * N_DIFF_HEADS),
            jnp.tile(ones, 2 * N_DIFF_HEADS),
            jnp.tile(swa_q_gain[layer] * scale, N_SWA_HEADS),
            jnp.tile(swa_k_gain[layer], N_SWA_KV),
            jnp.tile(ones, N_SWA_KV)]).reshape(1, C_END).astype(F32)
        nmask = np.zeros((1, C_END), np.float32)
        nmask[:, C_DQ:C_DV] = 1.0
        nmask[:, C_SQ:C_SV] = 1.0
        nmask = jnp.asarray(nmask)
        g1 = norm1_gain[layer].reshape(1, d).astype(F32)

        qkv = _proj(h, g1, w_cat, bd, gain, nmask, TP)
        qkv_meta = _proj(meta_tokens.astype(F32), g1, w_cat, bd, gain, nmask, N_META)
        meta_pad = jnp.pad(qkv_meta, ((0, TQ - N_META), (0, 0)))

        lamv = jnp.pad(jnp.stack([lam_q1[layer], lam_k1[layer], lam_q2[layer], lam_k2[layer]]).astype(F32),
                       ((0, 4), (0, LANES - HEAD_DIM)))
        mixd = _diff_attention(qkv, meta_pad[:, C_DK:C_DV], meta_pad[:, C_DV:C_SQ], dblk, bm0, lamv,
                               diff_subln_gain[layer].reshape(1, LANES).astype(F32), batch, seq, lambda_init)
        mixs = _swa_attention(swa_sinks[layer].astype(F32), qkv, meta_pad[:BLOCK, C_SK:C_SV],
                              meta_pad[:BLOCK, C_SV:C_END], jnp.swapaxes(bt, -1, -2), batch, seq)

        wr = jnp.pad(jnp.concatenate([w_router[layer], w_group[layer]], axis=1),
                     ((0, 0), (0, LANES - lower_pad))).astype(BF16)
        br = jnp.pad(jnp.concatenate([b_router[layer], b_group[layer]]), (0, LANES - lower_pad)).reshape(1, LANES)
        h1, hb, rt, tinfo, cnt = _outproj(h, mixd, mixs, w_out[layer].astype(BF16),
                                          norm2_gain[layer].reshape(1, d).astype(F32), wr, br.astype(F32))

        nt = n // TM
        counts = cnt[0, :N_EXPERTS].astype(jnp.int32)
        nblk_e = (counts + EB - 1) // EB
        blk_end = jnp.cumsum(nblk_e)
        pstart = ((blk_end - nblk_e) * EB).astype(jnp.int32)
        n_blocks = -(-(2 * n + nt * N_EXPERTS * (ROW_ALIGN - 1) + N_EXPERTS * (EB - 1)) // EB)
        blk_ids = jnp.arange(n_blocks)
        blk_e = jnp.minimum(jnp.sum(blk_end[None, :] <= blk_ids[:, None], axis=1), N_EXPERTS - 1).astype(jnp.int32)
        n_act = blk_end[-1:].astype(jnp.int32)
        is_last = jnp.any((blk_end[None, :] == blk_ids[:, None] + 1) & (nblk_e[None, :] > 0), axis=1)
        zero_blocks = jnp.where(blk_ids >= n_act[0], 2, jnp.where(is_last, 1, 0)).astype(jnp.int32)
        ti = tinfo.reshape(nt, 8, LANES)
        run_len = ti[:, 0, :N_EXPERTS].astype(jnp.int32)
        run_start = pstart[None, :] + ti[:, 1, :N_EXPERTS].astype(jnp.int32)
        run_groups = (run_len + ROW_ALIGN - 1) // ROW_ALIGN
        assert n_blocks * EB * SORTED_RANGE < 2 ** 31
        runs = _chunk_table(run_start, run_groups)

        xs = _dispatch(zero_blocks, runs, hb, rt, n_blocks * EB)
        own = jnp.where(nblk_e > 0, jnp.arange(N_EXPERTS), N_EXPERTS)
        later = jnp.concatenate([lax.cummin(own[::-1])[::-1][1:], jnp.full((1,), N_EXPERTS)])
        nxt_e = jnp.where(later < N_EXPERTS, later, -1).astype(jnp.int32)
        blk_first = blk_end - nblk_e
        owned = (blk_first[None, :] <= blk_ids[:, None]) & (blk_ids[:, None] < blk_end[None, :])
        blk_rows = jnp.sum(jnp.where(owned, jnp.minimum(counts[None, :] - (blk_ids[:, None] - blk_first[None, :]) * EB,
                                                        EB), 0), axis=1).astype(jnp.int32)
        ys = _experts(blk_e, n_act, nxt_e, blk_rows, xs, w_gate[layer], w_up[layer], w_down[layer])
        h = _combine(runs, ys, h1, rt)
    return h.reshape(batch, seq, d)
```

```python
import functools
import math

import numpy as np
import jax
import jax.numpy as jnp
from jax import lax
from jax.experimental import pallas as pl
from jax.experimental.pallas import tpu as pltpu

F32 = jnp.float32
BF16 = jnp.bfloat16

HEAD_DIM = 64
N_DIFF_HEADS = 4
N_SWA_HEADS = 8
N_SWA_KV = 2
BLOCK = 128
N_META = 16
N_BUCKETS = 32
MAX_DISTANCE = 128
N_GROUPS = 4
EXPERTS_PER_GROUP = 8
N_EXPERTS = N_GROUPS * EXPERTS_PER_GROUP
EPS = 1e-6
NEG = -1e30
LOG2E = math.log2(math.e)

LANES = 128
MXU_DIM = 256
V7X_VMEM_BYTES = 64 * 1024 * 1024
VMEM_LIMIT = V7X_VMEM_BYTES * 3 // 4

TP = 1024
TM = 512
TQ = 256
ONES_ROWS = 16
EB = 512
X_BUFFERS = 3
EXPERT_PART = EB // 4
ROW_ALIGN = 8
CHUNK_ROWS = (32, 16, 8)
SROWS = -(-(2 * TM + N_EXPERTS * (ROW_ALIGN - 1)) // MXU_DIM) * MXU_DIM
SORTED_RANGE = 2048
CHUNK_SLOTS = (SROWS // CHUNK_ROWS[0],) + (N_EXPERTS,) * (len(CHUNK_ROWS) - 1)
CHUNK_OFFSETS = tuple(sum(CHUNK_SLOTS[:k]) for k in range(len(CHUNK_ROWS)))
CHUNK_COUNTS = sum(CHUNK_SLOTS)
assert SROWS <= SORTED_RANGE and CHUNK_COUNTS + len(CHUNK_ROWS) <= LANES

C_DQ, C_DK, C_DV, C_SQ, C_SK, C_SV, C_END = 0, 512, 1024, 1536, 2048, 2176, 2304
NORM_GROUPS = (0, 1, 2, 3, 6, 7, 8)


def _cparams(sem):
    return pltpu.CompilerParams(dimension_semantics=sem, vmem_limit_bytes=VMEM_LIMIT)


def _t5_bucket_np(dist):
    n = np.maximum(dist, 0)
    max_exact = N_BUCKETS // 2
    nf = np.maximum(n, 1).astype(np.float32)
    large = max_exact + (np.log(nf / np.float32(max_exact)) / np.float32(math.log(MAX_DISTANCE / max_exact))
                         * np.float32(N_BUCKETS - max_exact)).astype(np.int32)
    large = np.minimum(large, N_BUCKETS - 1)
    return np.where(n < max_exact, n, large)


def _bias_tables(rel_bias, tq):
    nd = 2 * BLOCK
    buckets = _t5_bucket_np(np.arange(nd))
    assert (buckets[MAX_DISTANCE:] == N_BUCKETS - 1).all()
    rb = rel_bias.astype(F32)
    r = np.arange(BLOCK)[:, None]
    c = np.arange(BLOCK)[None, :]
    d_own = r - c
    d_prev = BLOCK + r - c
    far = rb[N_BUCKETS - 1]

    def take(dist, heads):
        idx = jnp.asarray(buckets[np.clip(dist, 0, nd - 1)], jnp.int32)[None]
        out = jnp.zeros((heads.stop - heads.start,) + dist.shape, F32)
        for b in range(N_BUCKETS):
            out = jnp.where(idx == b, rb[b, heads].reshape((-1,) + (1,) * dist.ndim), out)
        return out

    hd = slice(0, N_DIFF_HEADS)
    far_d = far[hd][:, None, None]
    d0 = jnp.where(d_own[None] >= 0, take(d_own, hd) - far_d, NEG)
    d1 = take(d_prev, hd) - far_d
    dblk = jnp.stack([d0, d1], axis=1)
    rq = np.arange(tq)[:, None]
    cm = np.arange(LANES)[None, :]
    d_meta = (N_META + rq - cm)[:, :N_META]
    bm0 = jnp.pad(take(d_meta, hd) - far_d, ((0, 0), (0, 0), (0, LANES - N_META)), constant_values=NEG)

    hs = slice(N_DIFF_HEADS, N_DIFF_HEADS + N_SWA_HEADS)
    far_s = far[hs][:, None, None]
    d_meta_s = N_META + r - cm
    meta_first = jnp.where((cm < N_META)[None], take(d_meta_s, hs), NEG)
    meta_rest = jnp.where((cm < N_META)[None], jnp.broadcast_to(far_s, (N_SWA_HEADS, BLOCK, LANES)), NEG)
    prev_rest = jnp.where((c > r)[None], take(d_prev, hs), NEG)
    prev_first = jnp.full((N_SWA_HEADS, BLOCK, BLOCK), NEG, F32)
    own = jnp.where((d_own >= 0)[None], take(d_own, hs), NEG)
    bt = jnp.stack([jnp.concatenate([prev_first, own, meta_first[..., :N_META]], axis=-1),
                    jnp.concatenate([prev_rest, own, meta_rest[..., :N_META]], axis=-1)], axis=0)
    return dblk.astype(F32), bm0.astype(F32), bt.astype(F32)


def _proj_kernel(x_ref, g1_ref, w_ref, bd_ref, gain_ref, nmask_ref, o_ref):
    x = x_ref[...]
    a = x * lax.rsqrt(jnp.mean(x * x, axis=-1, keepdims=True) + EPS) * g1_ref[...]
    p = jnp.dot(a.astype(BF16), w_ref[...], preferred_element_type=F32)
    bd = bd_ref[...]
    for j in range(C_END // MXU_DIM):
        sl = slice(j * MXU_DIM, (j + 1) * MXU_DIM)
        pj = p[:, sl]
        if j in NORM_GROUPS:
            ms = jnp.dot((pj * pj).astype(BF16), bd, preferred_element_type=F32)
            pj = jnp.where(nmask_ref[:, sl] != 0.0, pj * lax.rsqrt(ms + EPS) * gain_ref[:, sl], pj)
        o_ref[:, sl] = pj.astype(BF16)


def _proj(x2, g1, w, bd, gain, nmask, tm):
    n = x2.shape[0]
    return pl.pallas_call(
        _proj_kernel,
        out_shape=jax.ShapeDtypeStruct((n, C_END), BF16),
        grid=(n // tm,),
        in_specs=[
            pl.BlockSpec((tm, x2.shape[1]), lambda i: (i, 0)),
            pl.BlockSpec(g1.shape, lambda i: (0, 0)),
            pl.BlockSpec(w.shape, lambda i: (0, 0)),
            pl.BlockSpec(bd.shape, lambda i: (0, 0)),
            pl.BlockSpec(gain.shape, lambda i: (0, 0)),
            pl.BlockSpec(nmask.shape, lambda i: (0, 0)),
        ],
        out_specs=pl.BlockSpec((tm, C_END), lambda i: (i, 0)),
        compiler_params=_cparams(("parallel",)),
        name="proj",
    )(x2, g1, w, bd, gain, nmask)


def _diff_kernel(qi_tab, t_tab, q_ref, k_ref, v_ref, km_ref, vm_ref, d_ref, bm0_ref, lamv_ref, gain_ref, o_ref,
                 bias_ref, mb_ref, qs_ref, kt_ref, vt_ref, s_buf, p_buf, a_buf, m_ref, acc_ref, *,
                 lambda_init, n_steps, n_far, n_near):
    tq = TQ
    nq = q_ref.shape[0] // tq
    nb = tq // BLOCK
    BIAS_LEFT, BIAS_DIAG, BIAS_NONE = 0, 1, 2

    d0 = d_ref[0, 0] * LOG2E
    d1 = d_ref[0, 1] * LOG2E
    zeros = jnp.zeros((BLOCK, BLOCK), F32)
    for a in range(nb):
        for b in range(nb):
            rs, cs = slice(a * BLOCK, (a + 1) * BLOCK), slice(b * BLOCK, (b + 1) * BLOCK)
            if a == b:
                blk = d0
            elif b == a + 1:
                blk = d1
            elif b > a:
                blk = zeros
            else:
                blk = jnp.full((BLOCK, BLOCK), NEG, F32)
            bias_ref[BIAS_DIAG, rs, cs] = blk
            bias_ref[BIAS_LEFT, rs, cs] = d1 if (b == 0 and a == nb - 1) else zeros
    bias_ref[BIAS_NONE] = jnp.zeros((tq, tq), F32)
    mb_ref[0] = jnp.zeros((N_META, tq), F32)
    mb_ref[1] = bm0_ref[0, :N_META, :] * LOG2E

    lane = lax.broadcasted_iota(jnp.int32, (tq, LANES), 1)
    for i in range(nq):
        rows = slice(i * tq, (i + 1) * tq)
        q = q_ref[rows, :].astype(F32)
        qs_ref[i] = jnp.transpose(jnp.concatenate([jnp.where(lane < HEAD_DIM, q, 0.0),
                                                   jnp.where(lane >= HEAD_DIM, q, 0.0)], axis=0)).astype(BF16)
        vt_ref[i, :LANES, :] = jnp.transpose(v_ref[rows, :].astype(F32)).astype(BF16)
        kt_ref[i] = k_ref[rows, :]
    vt_ref[nq, :LANES, :] = jnp.transpose(vm_ref[...].astype(F32)).astype(BF16)
    kt_ref[nq] = km_ref[...]
    vt_ref[:, LANES:, :] = jnp.ones((nq + 1, ONES_ROWS, tq), BF16)
    acc_ref[...] = jnp.zeros(acc_ref.shape, F32)
    m_ref[...] = jnp.full(m_ref.shape, NEG, F32)
    lv = lamv_ref[...]
    lam = (jnp.exp(jnp.sum(lv[0:1] * lv[1:2], axis=-1, keepdims=True))
           - jnp.exp(jnp.sum(lv[2:3] * lv[3:4], axis=-1, keepdims=True)) + lambda_init)

    FAR, NEAR, META = 0, 1, 2

    def stage_a(n, slot, kind):
        qi, t = qi_tab[n], t_tab[n]
        if kind == META:
            s = jnp.dot(kt_ref[nq, :N_META, :], qs_ref[qi], preferred_element_type=F32)
            s_buf[slot, :N_META] = s + jnp.tile(mb_ref[jnp.where(qi == 0, 1, 0)], (1, 2))
            return
        s = jnp.dot(kt_ref[t - 1], qs_ref[qi], preferred_element_type=F32)
        if kind == NEAR:
            which = jnp.where(t == qi + 1, BIAS_DIAG, jnp.where(t == qi, BIAS_LEFT, BIAS_NONE))
            s = s + jnp.tile(bias_ref[which], (1, 2))
        s_buf[slot] = s

    def stage_b(n, slot, kind):
        qi = qi_tab[n]
        rows = slice(0, N_META if kind == META else tq)
        s = s_buf[slot, rows]
        m_prev = m_ref[qi]
        m_new = jnp.maximum(m_prev, jnp.max(s, axis=0, keepdims=True))
        a_buf[slot] = jnp.exp2(m_prev - m_new)
        p_buf[slot, rows] = jnp.exp2(s - m_new[0:1]).astype(BF16)
        m_ref[qi] = m_new

    def stage_c(n, slot, kind):
        qi, t = qi_tab[n], t_tab[n]
        if kind == META:
            pv = jnp.dot(vt_ref[nq, :, :N_META], p_buf[slot, :N_META], preferred_element_type=F32)
        else:
            pv = jnp.dot(vt_ref[t - 1], p_buf[slot], preferred_element_type=F32)
        acc_ref[qi] = a_buf[slot][0:1] * acc_ref[qi] + pv

    LEAD, SLOTS, UNROLL = 2, 3, 12
    assert UNROLL % SLOTS == 0

    def pipeline(base, count, kind, biased_from=None):
        def kind_a(j):
            return kind if biased_from is None or j < biased_from else NEAR

        if count <= 2 * LEAD:
            for j in range(count):
                stage_a(base + j, 0, kind_a(j))
                stage_b(base + j, 0, kind)
                stage_c(base + j, 0, kind)
            return
        for j in range(2 * LEAD):
            stage_a(base + j, j % SLOTS, kind_a(j))
            if j >= LEAD:
                stage_b(base + j - LEAD, (j - LEAD) % SLOTS, kind)

        def steps(n, first, count):
            for j in range(count):
                stage_a(base + n + j + 2 * LEAD, (j + 2 * LEAD) % SLOTS, kind_a(first + j + 2 * LEAD))
                stage_b(base + n + j + LEAD, (j + LEAD) % SLOTS, kind)
                stage_c(base + n + j, j % SLOTS, kind)

        n_steady = count - 2 * LEAD
        n_blocks = n_steady // UNROLL
        switch = n_blocks if biased_from is None else (biased_from - 2 * LEAD) // UNROLL
        assert biased_from is None or (biased_from - 2 * LEAD) % UNROLL == 0
        for lo, hi in ((0, min(switch, n_blocks)), (min(switch, n_blocks), n_blocks)):
            lax.fori_loop(lo, hi, lambda k, carry, lo=lo: (steps(UNROLL * k, UNROLL * lo, UNROLL), carry)[1], 0)
        steps(UNROLL * n_blocks, UNROLL * n_blocks, n_steady - UNROLL * n_blocks)
        for j in range(n_steady, count):
            if j + LEAD < count:
                stage_b(base + j + LEAD, (j + LEAD) % SLOTS, kind)
            stage_c(base + j, j % SLOTS, kind)

    if n_far >= 2 * LEAD:
        pipeline(0, n_far + n_near, FAR, 2 * LEAD + UNROLL * ((n_far - 2 * LEAD) // UNROLL))
    else:
        pipeline(0, n_far + n_near, NEAR)
    pipeline(n_far + n_near, n_steps - n_far - n_near, META)

    for i in range(nq):
        acc = acc_ref[i]
        o = acc[:LANES] * (1.0 / acc[LANES:LANES + 1])
        d = o[:, :tq] - lam * o[:, tq:]
        y = d * lax.rsqrt(jnp.mean(d * d, axis=0, keepdims=True) + EPS) * jnp.tile(gain_ref[...], (1, tq // LANES))
        o_ref[i * tq:(i + 1) * tq, :] = jnp.transpose(y * (1.0 - lambda_init)).astype(BF16)


def _diff_attention(qkv, km, vm, dblk, bm0, lamv, gain, batch, seq, lambda_init):
    nq = seq // TQ
    far = [(qi, t) for qi in range(nq) for t in range(1, qi)]
    near = [(qi, t) for qi in range(nq) for t in (qi, qi + 1) if t >= 1]
    meta = [(qi, 0) for qi in range(nq)]
    steps = far + near + meta
    qi_tab = jnp.asarray([s[0] for s in steps], jnp.int32)
    t_tab = jnp.asarray([s[1] for s in steps], jnp.int32)
    kern = functools.partial(_diff_kernel, lambda_init=lambda_init, n_steps=len(steps), n_far=len(far),
                             n_near=len(near))
    return pl.pallas_call(
        kern,
        out_shape=jax.ShapeDtypeStruct((batch * seq, N_DIFF_HEADS * LANES), BF16),
        grid_spec=pltpu.PrefetchScalarGridSpec(
            num_scalar_prefetch=2,
            grid=(batch, N_DIFF_HEADS),
            in_specs=[
                pl.BlockSpec((seq, LANES), lambda b, h, *_: (b, C_DQ // LANES + h)),
                pl.BlockSpec((seq, LANES), lambda b, h, *_: (b, C_DK // LANES + h)),
                pl.BlockSpec((seq, LANES), lambda b, h, *_: (b, C_DV // LANES + h)),
                pl.BlockSpec((TQ, LANES), lambda b, h, *_: (0, h)),
                pl.BlockSpec((TQ, LANES), lambda b, h, *_: (0, h)),
                pl.BlockSpec((1, 2, BLOCK, BLOCK), lambda b, h, *_: (h, 0, 0, 0)),
                pl.BlockSpec((1, LANES, TQ), lambda b, h, *_: (h, 0, 0)),
                pl.BlockSpec(lamv.shape, lambda b, h, *_: (0, 0)),
                pl.BlockSpec((LANES, LANES), lambda b, h, *_: (0, 0)),
            ],
            out_specs=pl.BlockSpec((seq, LANES), lambda b, h, *_: (b, h)),
            scratch_shapes=[
                pltpu.VMEM((3, TQ, TQ), F32),
                pltpu.VMEM((2, N_META, TQ), F32),
                pltpu.VMEM((nq, LANES, 2 * TQ), BF16),
                pltpu.VMEM((nq + 1, TQ, LANES), BF16),
                pltpu.VMEM((nq + 1, LANES + ONES_ROWS, TQ), BF16),
                pltpu.VMEM((3, TQ, 2 * TQ), F32),
                pltpu.VMEM((3, TQ, 2 * TQ), BF16),
                pltpu.VMEM((3, 8, 2 * TQ), F32),
                pltpu.VMEM((nq, 8, 2 * TQ), F32),
                pltpu.VMEM((nq, LANES + ONES_ROWS, 2 * TQ), F32),
            ],
        ),
        compiler_params=_cparams(("parallel", "parallel")),
        name="diff_attention",
    )(qi_tab, t_tab, qkv, qkv, qkv, km, vm, jnp.swapaxes(dblk, -1, -2), jnp.swapaxes(bm0, -1, -2), lamv,
      jnp.broadcast_to(gain.reshape(LANES, 1), (LANES, LANES)))


def _swa_kernel(sink_ref, q_ref, k_ref, v_ref, km_ref, vm_ref, bt_ref, o_ref, kd_ref, vt_ref,
                s_scr, p_scr, inv_scr):
    nkb = k_ref.shape[0] // BLOCK
    lane = lax.broadcasted_iota(jnp.int32, (BLOCK, LANES), 1)
    pairs = [(g, u) for g in range(N_SWA_KV) for u in range(2)]

    def both_halves(k):
        k0, k1 = k[:, :HEAD_DIM], k[:, HEAD_DIM:]
        return jnp.concatenate([k0, k0, k1, k1], axis=1)

    def prepare(j, carry):
        rows = pl.ds(pl.multiple_of(j * BLOCK, BLOCK), BLOCK)
        kd_ref[j] = both_halves(k_ref[rows, :])
        vt_ref[j] = jnp.transpose(v_ref[rows, :].astype(F32)).astype(BF16)
        return carry
    lax.fori_loop(0, nkb, prepare, 0)
    kd_ref[nkb] = both_halves(km_ref[...])
    vt_ref[nkb] = jnp.transpose(vm_ref[...].astype(F32)).astype(BF16)

    def scores(n, slot):
        first = jnp.where(n == 0, 0, 1)
        prev = jnp.maximum(n - 1, 0)
        r_q = pl.multiple_of(n * BLOCK, BLOCK)
        for c, (g, u) in enumerate(pairs):
            ks = slice(g * LANES, (g + 1) * LANES)
            kcat = jnp.concatenate([kd_ref[prev, :, ks], kd_ref[n, :, ks], kd_ref[nkb, :N_META, ks]], axis=0)
            h0 = 4 * g + 2 * u
            qp = q_ref[pl.ds(r_q, BLOCK), (2 * g + u) * LANES:(2 * g + u + 1) * LANES].astype(F32)
            qs = jnp.transpose(jnp.concatenate([jnp.where(lane < HEAD_DIM, qp, 0.0),
                                                jnp.where(lane >= HEAD_DIM, qp, 0.0)], axis=0)).astype(BF16)
            s = jnp.dot(kcat, qs, preferred_element_type=F32)
            s_scr[slot, c] = s + jnp.concatenate([bt_ref[first, h0], bt_ref[first, h0 + 1]], axis=1)

    def exponentials(slot):
        for c, (g, u) in enumerate(pairs):
            h0 = 4 * g + 2 * u
            s = s_scr[slot, c]
            sink = jnp.concatenate([sink_ref[h0:h0 + 1, :], sink_ref[h0 + 1:h0 + 2, :]], axis=1)
            m = jnp.maximum(jnp.max(s, axis=0, keepdims=True), sink)
            p = jnp.exp(s - m)
            p_scr[slot, c] = p.astype(BF16)
            inv_scr[slot, c] = jnp.broadcast_to(1.0 / (jnp.sum(p, axis=0, keepdims=True) + jnp.exp(sink - m)),
                                                inv_scr.shape[2:])

    def values(n, slot):
        prev = jnp.maximum(n - 1, 0)
        r_q = pl.multiple_of(n * BLOCK, BLOCK)
        for c, (g, u) in enumerate(pairs):
            vs = slice(g * HEAD_DIM, (g + 1) * HEAD_DIM)
            vcat = jnp.concatenate([vt_ref[prev, vs, :], vt_ref[n, vs, :]], axis=1)
            o = (jnp.dot(vcat, p_scr[slot, c, :2 * BLOCK], preferred_element_type=F32)
                 + jnp.dot(vt_ref[nkb, vs, :N_META], p_scr[slot, c, 2 * BLOCK:], preferred_element_type=F32)
                 ) * inv_scr[slot, c][0:1]
            ot = jnp.transpose(o)
            o_ref[pl.ds(r_q, BLOCK), (2 * g + u) * LANES:(2 * g + u + 1) * LANES] = (
                jnp.concatenate([ot[:BLOCK], ot[BLOCK:]], axis=1).astype(BF16))

    scores(0, 0)
    scores(1, 1)
    exponentials(0)

    def blocks(n, count):
        for j in range(count):
            scores(n + j + 2, j % 2)
            exponentials((j + 1) % 2)
            values(n + j, j % 2)

    unroll = 4
    n_steady = nkb - 2
    lax.fori_loop(0, n_steady // unroll, lambda k, carry: (blocks(unroll * k, unroll), carry)[1], 0)
    blocks(n_steady // unroll * unroll, n_steady % unroll)
    exponentials(1)
    values(nkb - 2, 0)
    values(nkb - 1, 1)


def _swa_attention(sinks, qkv, km, vm, bt, batch, seq):
    nkb = seq // BLOCK
    swa_q = N_SWA_HEADS * HEAD_DIM
    assert nkb % 2 == 0
    sinkv =jnp.broadcast_to(sinks.reshape(N_SWA_HEADS, 1), (N_SWA_HEADS, LANES))
    return pl.pallas_call(
        _swa_kernel,
        out_shape=jax.ShapeDtypeStruct((batch * seq, N_SWA_HEADS * HEAD_DIM), BF16),
        grid=(batch,),
        in_specs=[
            pl.BlockSpec(sinkv.shape, lambda b: (0, 0)),
            pl.BlockSpec((seq, swa_q), lambda b: (b, C_SQ // swa_q)),
            pl.BlockSpec((seq, LANES), lambda b: (b, C_SK // LANES)),
            pl.BlockSpec((seq, LANES), lambda b: (b, C_SV // LANES)),
            pl.BlockSpec(km.shape, lambda b: (0, 0)),
            pl.BlockSpec(vm.shape, lambda b: (0, 0)),
            pl.BlockSpec(bt.shape, lambda b: (0, 0, 0, 0)),
        ],
        out_specs=pl.BlockSpec((seq, swa_q), lambda b: (b, 0)),
        scratch_shapes=[pltpu.VMEM((nkb + 1, BLOCK, 2 * LANES), BF16),
                        pltpu.VMEM((nkb + 1, LANES, BLOCK), BF16),
                        pltpu.VMEM((2, 4, 2 * BLOCK + N_META, 2 * BLOCK), F32),
                        pltpu.VMEM((2, 4, 2 * BLOCK + N_META, 2 * BLOCK), BF16),
                        pltpu.VMEM((2, 4, 8, 2 * BLOCK), F32)],
        compiler_params=_cparams(("parallel",)),
        name="swa_attention",
    )(sinkv, qkv, qkv, qkv, km, vm, bt)


def _outproj_kernel(x_ref, md_ref, ms_ref, wo_ref, g2_ref, wr_ref, br_ref,
                    h_ref, hb_ref, rt_ref, ti_ref, cnt_ref, c_ref, lg_ref):
    i = pl.program_id(0)

    @pl.when(i == 0)
    def _init():
        c_ref[...] = jnp.zeros(c_ref.shape, F32)
        lg_ref[...] = jnp.zeros(lg_ref.shape, F32)

    lg_prev = lg_ref[...]
    half = md_ref.shape[1]
    h = (x_ref[...]
         + jnp.dot(md_ref[...], wo_ref[:half, :], preferred_element_type=F32)
         + jnp.dot(ms_ref[...], wo_ref[half:, :], preferred_element_type=F32))
    h_ref[...] = h
    hn = h * lax.rsqrt(jnp.mean(h * h, axis=-1, keepdims=True) + EPS) * g2_ref[...]
    hb = hn.astype(BF16)
    hb_ref[...] = hb
    lg_ref[...] = jnp.dot(hb, wr_ref[...], preferred_element_type=F32) + br_ref[...]
    _route_tile(lg_prev, jnp.where(i > 0, 1.0, 0.0), rt_ref, ti_ref, c_ref)

    @pl.when(i == pl.num_programs(0) - 1)
    def _fin():
        cnt_ref[...] = c_ref[...]


def _route_tile(lg, live, rt_ref, ti_ref, c_ref):
    tm = lg.shape[0]
    lane_i = lax.broadcasted_iota(jnp.int32, lg.shape, 1)
    lane = lane_i.astype(F32)
    big = float(4 * LANES)
    is_g = (lane_i >= N_EXPERTS) & (lane_i < N_EXPERTS + N_GROUPS)
    glm = jnp.where(is_g, lg, -jnp.inf)
    gmax = jnp.max(glm, axis=1, keepdims=True)
    gidx = jnp.min(jnp.where(glm == gmax, lane, big), axis=1, keepdims=True) - N_EXPERTS
    gsum = jnp.sum(jnp.where(is_g, jnp.exp(lg - gmax), 0.0), axis=1, keepdims=True)
    g_w = 1.0 / gsum
    lane_grp = (lane_i >> 3).astype(F32)
    in_grp = (lane_i < N_EXPERTS) & (lane_grp == gidx)
    el = jnp.where(in_grp, lg, -jnp.inf)
    t1 = jnp.max(el, axis=1, keepdims=True)
    j1 = jnp.min(jnp.where(el == t1, lane, big), axis=1, keepdims=True)
    el2 = jnp.where(lane == j1, -jnp.inf, el)
    t2 = jnp.max(el2, axis=1, keepdims=True)
    j2 = jnp.min(jnp.where(el2 == t2, lane, big), axis=1, keepdims=True)
    e2 = jnp.exp(t2 - t1)
    den = 1.0 + e2
    gate1 = g_w / den
    gate2 = g_w * e2 / den

    o1 = lane == j1
    o2 = lane == j2
    onehot = jnp.where(o1 | o2, 1.0, 0.0).astype(BF16)
    rr = lax.broadcasted_iota(jnp.int32, (tm, tm), 0)
    cc = lax.broadcasted_iota(jnp.int32, (tm, tm), 1)
    lower = jnp.where(rr > cc, 1.0, 0.0).astype(BF16)
    pfx = jnp.dot(lower, onehot, preferred_element_type=F32)
    cnt_tile = jnp.sum(onehot.astype(F32), axis=0, keepdims=True)
    groups = jnp.floor((cnt_tile + (ROW_ALIGN - 1)) * (1.0 / ROW_ALIGN))
    er = lax.broadcasted_iota(jnp.int32, (LANES, LANES), 0)
    ec = lax.broadcasted_iota(jnp.int32, (LANES, LANES), 1)
    before = jnp.where(er < ec, 1.0, 0.0).astype(BF16)
    cbase = ROW_ALIGN * jnp.dot(jnp.broadcast_to(groups, (8, LANES)).astype(BF16), before,
                                preferred_element_type=F32)[0:1]
    at = pfx + cbase
    pos1 = jnp.sum(jnp.where(o1, at, 0.0), axis=1, keepdims=True)
    pos2 = jnp.sum(jnp.where(o2, at, 0.0), axis=1, keepdims=True)
    rt_ref[...] = jnp.where(lane_i == 0, gate1,
                            jnp.where(lane_i == 1, gate2,
                                      jnp.where(lane_i == 2, pos1,
                                                jnp.where(lane_i == 3, pos2, 0.0))))
    c_old = c_ref[...]
    c_ref[...] = c_old + groups * (ROW_ALIGN * live)
    row8 = lax.broadcasted_iota(jnp.int32, (8, LANES), 0)
    ti_ref[...] = jnp.where(row8 == 0, cnt_tile, jnp.where(row8 == 1, c_old, 0.0))


def _outproj(x2, mixd, mixs, wo, g2, wr, br):
    n, d = x2.shape
    nt = n // TM

    def proj_tile(i):
        return (jnp.minimum(i, nt - 1), 0)

    def route_tile(i):
        return (jnp.maximum(i - 1, 0), 0)

    return pl.pallas_call(
        _outproj_kernel,
        out_shape=(jax.ShapeDtypeStruct((n, d), F32),
                   jax.ShapeDtypeStruct((n, d), BF16),
                   jax.ShapeDtypeStruct((n, LANES), F32),
                   jax.ShapeDtypeStruct((nt * 8, LANES), F32),
                   jax.ShapeDtypeStruct((8, LANES), F32)),
        grid=(nt + 1,),
        in_specs=[
            pl.BlockSpec((TM, d), proj_tile),
            pl.BlockSpec((TM, mixd.shape[1]), proj_tile),
            pl.BlockSpec((TM, mixs.shape[1]), proj_tile),
            pl.BlockSpec(wo.shape, lambda i: (0, 0)),
            pl.BlockSpec(g2.shape, lambda i: (0, 0)),
            pl.BlockSpec(wr.shape, lambda i: (0, 0)),
            pl.BlockSpec(br.shape, lambda i: (0, 0)),
        ],
        out_specs=(pl.BlockSpec((TM, d), proj_tile),
                   pl.BlockSpec((TM, d), proj_tile),
                   pl.BlockSpec((TM, LANES), route_tile),
                   pl.BlockSpec((8, LANES), route_tile),
                   pl.BlockSpec((8, LANES), lambda i: (0, 0))),
        scratch_shapes=[pltpu.VMEM((8, LANES), F32), pltpu.VMEM((TM, LANES), F32)],
        compiler_params=_cparams(("arbitrary",)),
        name="outproj_router",
    )(x2, mixd, mixs, wo, g2, wr, br)


def _chunk_table(run_start, run_groups):
    nt = run_start.shape[0]
    sorted_start = (jnp.cumsum(run_groups, axis=1) - run_groups) * ROW_ALIGN
    word = run_start * SORTED_RANGE + sorted_start
    done = jnp.zeros_like(run_groups)
    segments, counts = [], []
    for k, rows in enumerate(CHUNK_ROWS):
        per_run = run_groups // (rows // ROW_ALIGN)
        if k > 0:
            per_run = per_run % 2
        last = jnp.cumsum(per_run, axis=1)[:, None, :]
        first = last - per_run[:, None, :]
        slot = jnp.arange(CHUNK_SLOTS[k], dtype=jnp.int32)[None, :, None]
        value = word[:, None, :] + (done[:, None, :] + (slot - first) * rows) * (SORTED_RANGE + 1)
        segments.append(jnp.sum(jnp.where((first <= slot) & (slot < last), value, 0), axis=2))
        counts.append(last[:, 0, -1:])
        done = done + per_run * rows
    pad = jnp.zeros((nt, LANES - CHUNK_COUNTS - len(CHUNK_ROWS)), jnp.int32)
    return jnp.concatenate(segments + counts + [pad], axis=1).astype(jnp.int32).reshape(nt, 1, LANES)


def _for_each_chunk(chunks_ref, fn):
    def issue(k, c, priority):
        word = chunks_ref[0, 0, CHUNK_OFFSETS[k] + c]
        fn(pl.multiple_of(word >> (SORTED_RANGE.bit_length() - 1), ROW_ALIGN),
           pl.multiple_of(word & (SORTED_RANGE - 1), ROW_ALIGN), CHUNK_ROWS[k], priority)

    for k in range(len(CHUNK_ROWS)):
        _in_pairs(chunks_ref[0, 0, CHUNK_COUNTS + k], lambda c, second, k=k: issue(k, c, second))


def _in_pairs(count, fn):
    half = count // 2

    def body(c, carry):
        fn(c, 0)
        fn(half + c, 1)
        return carry
    lax.fori_loop(0, half, body, 0)

    @pl.when(count % 2 == 1)
    def _():
        fn(count - 1, 0)


def _wait_chunks(chunks_ref, make_copy):
    for k, rows in enumerate(CHUNK_ROWS):
        _in_pairs(chunks_ref[0, 0, CHUNK_COUNTS + k], lambda c, second, rows=rows: make_copy(rows).wait())


def _dispatch_kernel(zf_ref, cur_ref, prv_ref, hb_ref, rt_ref, xs_ref, sbuf, zbuf, sem, zsem):
    i = pl.program_id(0)
    nt = pl.num_programs(0)
    slot = i % 2
    tm, d = hb_ref.shape

    def for_zero_blocks(kind, fn):
        def body(b, carry):
            @pl.when(zf_ref[b] == kind)
            def _():
                fn(pltpu.make_async_copy(zbuf, xs_ref.at[pl.ds(pl.multiple_of(b * EB, EB), EB)],
                                         zsem.at[kind - 1]))
            return carry
        lax.fori_loop(0, zf_ref.shape[0], body, 0)

    @pl.when(i == 0)
    def _():
        zbuf[...] = jnp.zeros(zbuf.shape, zbuf.dtype)
        for_zero_blocks(1, lambda c: c.start())
        for_zero_blocks(2, lambda c: c.start())
        for_zero_blocks(1, lambda c: c.wait())

    pos_t = jnp.transpose(rt_ref[...])
    srow = lax.broadcasted_iota(jnp.int32, (SROWS, tm), 0).astype(F32)
    sel = jnp.where(srow == pos_t[2:3, :], 1.0, jnp.where(srow == pos_t[3:4, :], 1.0, 0.0)).astype(BF16)
    srt = jnp.dot(sel, hb_ref[...], preferred_element_type=F32)
    bits = pltpu.bitcast(srt, jnp.uint32)
    sbuf[slot] = (bits[:, d // 2:] & jnp.uint32(0xFFFF0000)) | (bits[:, :d // 2] >> 16)

    def chunk_copy(run_row, sorted_row, rows, sl):
        return pltpu.make_async_copy(sbuf.at[sl, pl.ds(sorted_row, rows)], xs_ref.at[pl.ds(run_row, rows)],
                                     sem.at[sl])

    _for_each_chunk(cur_ref, lambda run_row, sorted_row, rows, priority:
                    chunk_copy(run_row, sorted_row, rows, slot).start(priority=priority))

    @pl.when(i > 0)
    def _():
        _wait_chunks(prv_ref, lambda rows: chunk_copy(0, 0, rows, 1 - slot))

    @pl.when(i == nt - 1)
    def _():
        _wait_chunks(cur_ref, lambda rows: chunk_copy(0, 0, rows, slot))
        for_zero_blocks(2, lambda c: c.wait())


def _dispatch(zero_blocks, runs, hb, rt, n_rows):
    n, d = hb.shape
    return pl.pallas_call(
        _dispatch_kernel,
        out_shape=jax.ShapeDtypeStruct((n_rows, d // 2), jnp.uint32),
        grid_spec=pltpu.PrefetchScalarGridSpec(
            num_scalar_prefetch=1,
            grid=(n // TM,),
            in_specs=[
                pl.BlockSpec((1, 1, LANES), lambda i, zf: (i, 0, 0), memory_space=pltpu.SMEM),
                pl.BlockSpec((1, 1, LANES), lambda i, zf: (jnp.maximum(i - 1, 0), 0, 0), memory_space=pltpu.SMEM),
                pl.BlockSpec((TM, d), lambda i, zf: (i, 0)),
                pl.BlockSpec((TM, LANES), lambda i, zf: (i, 0)),
            ],
            out_specs=pl.BlockSpec(memory_space=pl.ANY),
            scratch_shapes=[pltpu.VMEM((2, SROWS, d // 2), jnp.uint32), pltpu.VMEM((EB, d // 2), jnp.uint32),
                            pltpu.SemaphoreType.DMA((2,)), pltpu.SemaphoreType.DMA((2,))],
        ),
        compiler_params=_cparams(("arbitrary",)),
        name="dispatch",
    )(zero_blocks, runs, runs, hb, rt)


def _experts_kernel(be_ref, na_ref, nxt_ref, rows_ref, xs_hbm, wg_hbm, wu_hbm, wd_hbm, ys_ref,
                    xbuf, wgf, wuf, wdf, wgb, wub, wdb, sem, xsem):
    b = pl.program_id(0)
    n_act = na_ref[0]
    slot = b % X_BUFFERS

    def x_copy(blk):
        s = blk % X_BUFFERS
        return pltpu.make_async_copy(xs_hbm.at[pl.ds(pl.multiple_of(blk * EB, EB), EB)], xbuf.at[s], xsem.at[s])

    @pl.when(b == 0)
    def _():
        for blk in range(X_BUFFERS - 1):
            @pl.when(blk < n_act)
            def _(blk=blk):
                x_copy(blk).start()

    @pl.when(b + X_BUFFERS - 1 < n_act)
    def _():
        x_copy(b + X_BUFFERS - 1).start()

    def weight_copies(e):
        return (pltpu.make_async_copy(wg_hbm.at[e], wgf, sem.at[0]),
                pltpu.make_async_copy(wu_hbm.at[e], wuf, sem.at[1]),
                pltpu.make_async_copy(wd_hbm.at[e], wdf, sem.at[2]))

    @pl.when(b == 0)
    def _():
        for c in weight_copies(be_ref[0]):
            c.start()

    @pl.when(b < na_ref[0])
    def _():
        e = be_ref[b]
        changed = jnp.logical_or(b == 0, be_ref[jnp.maximum(b - 1, 0)] != e)

        @pl.when(changed)
        def _load():
            for c in weight_copies(e):
                c.wait()
            wgb[...] = wgf[...].astype(BF16)
            wub[...] = wuf[...].astype(BF16)
            wdb[...] = wdf[...].astype(BF16)
            nxt = nxt_ref[e]

            @pl.when(nxt >= 0)
            def _():
                for c in weight_copies(nxt):
                    c.start()

        x_copy(b).wait()

    def compute(rows):
        w = xbuf[slot, rows, :]
        x_lo = pltpu.bitcast(w << 16, F32).astype(BF16)
        x_hi = pltpu.bitcast(w & jnp.uint32(0xFFFF0000), F32).astype(BF16)
        dh = w.shape[1]
        g = (jnp.dot(x_lo, wgb[:dh, :], preferred_element_type=F32)
             + jnp.dot(x_hi, wgb[dh:, :], preferred_element_type=F32))
        u = (jnp.dot(x_lo, wub[:dh, :], preferred_element_type=F32)
             + jnp.dot(x_hi, wub[dh:, :], preferred_element_type=F32))
        hdn = g * (1.0 / (1.0 + jnp.exp(-g))) * u
        y = jnp.dot(hdn.astype(BF16), wdb[...], preferred_element_type=F32)
        bits = pltpu.bitcast(y.astype(BF16).astype(F32), jnp.uint32)
        ys_ref[rows, :] = (bits[:, dh:] & jnp.uint32(0xFFFF0000)) | (bits[:, :dh] >> 16)

    n_rows = rows_ref[b]

    part = EXPERT_PART
    whole = n_rows > 3 * part
    for lo, size, cond in ((0, 4 * part, whole),
                           (0, 2 * part, jnp.logical_and(n_rows > part, n_rows <= 3 * part)),
                           (0, part, jnp.logical_and(n_rows > 0, n_rows <= part)),
                           (2 * part, part, jnp.logical_and(n_rows > 2 * part, n_rows <= 3 * part))):
        @pl.when(cond)
        def _(lo=lo, size=size):
            compute(pl.ds(lo, size))

    for k in range(4):
        @pl.when(n_rows <= k * part)
        def _(k=k):
            ys_ref[pl.ds(k * part, part), :] = jnp.zeros((part, ys_ref.shape[1]), ys_ref.dtype)


def _experts(blk_e, n_act, nxt_e, blk_rows, xs, w_gate, w_up, w_down):
    p, dh = xs.shape
    d = 2 * dh
    de = w_gate.shape[2]

    return pl.pallas_call(
        _experts_kernel,
        out_shape=jax.ShapeDtypeStruct((p, dh), jnp.uint32),
        grid_spec=pltpu.PrefetchScalarGridSpec(
            num_scalar_prefetch=4,
            grid=(p // EB,),
            in_specs=[
                pl.BlockSpec(memory_space=pl.ANY),
                pl.BlockSpec(memory_space=pl.ANY),
                pl.BlockSpec(memory_space=pl.ANY),
                pl.BlockSpec(memory_space=pl.ANY),
            ],
            out_specs=pl.BlockSpec((EB, dh), lambda b, be, na, nx, br: (b, 0)),
            scratch_shapes=[pltpu.VMEM((X_BUFFERS, EB, dh), jnp.uint32),
                            pltpu.VMEM((d, de), F32), pltpu.VMEM((d, de), F32), pltpu.VMEM((de, d), F32),
                            pltpu.VMEM((d, de), BF16), pltpu.VMEM((d, de), BF16), pltpu.VMEM((de, d), BF16),
                            pltpu.SemaphoreType.DMA((3,)), pltpu.SemaphoreType.DMA((X_BUFFERS,))],
        ),
        compiler_params=_cparams(("arbitrary",)),
        name="experts",
    )(blk_e, n_act, nxt_e, blk_rows, xs, w_gate, w_up, w_down)


def _combine_kernel(cur_ref, nxt_ref, ys_ref, h_ref, rt_ref, o_ref, ybuf, sem):
    i = pl.program_id(0)
    nt = pl.num_programs(0)
    slot = i % 2
    tm = h_ref.shape[0]

    def chunk_copy(run_row, sorted_row, rows, sl):
        return pltpu.make_async_copy(ys_ref.at[pl.ds(run_row, rows)], ybuf.at[sl, pl.ds(sorted_row, rows)],
                                     sem.at[sl])

    @pl.when(i == 0)
    def _():
        ybuf[...] = jnp.zeros(ybuf.shape, ybuf.dtype)
        _for_each_chunk(cur_ref, lambda run_row, sorted_row, rows, priority:
                        chunk_copy(run_row, sorted_row, rows, 0).start(priority=priority))

    @pl.when(i + 1 < nt)
    def _():
        _for_each_chunk(nxt_ref, lambda run_row, sorted_row, rows, priority:
                        chunk_copy(run_row, sorted_row, rows, 1 - slot).start(priority=priority))

    _wait_chunks(cur_ref, lambda rows: chunk_copy(0, 0, rows, slot))

    rt = rt_ref[...]
    w = ybuf[slot]
    dh = w.shape[1]
    y_lo = pltpu.bitcast(w << 16, F32).astype(BF16)
    y_hi = pltpu.bitcast(w & jnp.uint32(0xFFFF0000), F32).astype(BF16)
    col = lax.broadcasted_iota(jnp.int32, (tm, SROWS), 1).astype(F32)
    wsel = jnp.where(col == rt[:, 2:3], rt[:, 0:1], jnp.where(col == rt[:, 3:4], rt[:, 1:2], 0.0)).astype(BF16)
    for half, yb in ((slice(0, dh), y_lo), (slice(dh, 2 * dh), y_hi)):
        o_ref[:, half] = h_ref[:, half] + jnp.dot(wsel, yb, preferred_element_type=F32)


def _combine(runs, ys, h1, rt):
    n, d = h1.shape
    nt = n // TM
    return pl.pallas_call(
        _combine_kernel,
        out_shape=jax.ShapeDtypeStruct((n, d), F32),
        grid=(nt,),
        in_specs=[
            pl.BlockSpec((1, 1, LANES), lambda i: (i, 0, 0), memory_space=pltpu.SMEM),
            pl.BlockSpec((1, 1, LANES), lambda i: (jnp.minimum(i + 1, nt - 1), 0, 0), memory_space=pltpu.SMEM),
            pl.BlockSpec(memory_space=pl.ANY),
            pl.BlockSpec((TM, d), lambda i: (i, 0)),
            pl.BlockSpec((TM, LANES), lambda i: (i, 0)),
        ],
        out_specs=pl.BlockSpec((TM, d), lambda i: (i, 0)),
        scratch_shapes=[pltpu.VMEM((2, SROWS, d // 2), jnp.uint32), pltpu.SemaphoreType.DMA((2,))],
        compiler_params=_cparams(("arbitrary",)),
        name="combine",
    )(runs, runs, ys, h1, rt)


def kernel(x, meta_tokens, rel_bias, norm1_gain, w_in, diff_q_gain, diff_k_gain, lam_q1, lam_k1, lam_q2, lam_k2, diff_subln_gain, swa_q_gain, swa_k_gain, swa_sinks, w_out, norm2_gain, w_group, b_group, w_router, b_router, w_gate, w_up, w_down):
    batch, seq, d = x.shape
    depth = w_in.shape[0]
    n = batch * seq
    assert seq % TQ == 0 and n % TM == 0 and n % TP == 0 and d == 1024
    assert meta_tokens.shape[0] == N_META
    assert depth == 1, "the meta-token rows of the residual stream are not carried across layers"

    h = x.reshape(n, d)
    dblk, bm0, bt = _bias_tables(rel_bias, TQ)
    scale = HEAD_DIM ** -0.5
    bd = jnp.asarray(np.kron(np.eye(MXU_DIM // HEAD_DIM), np.full((HEAD_DIM, HEAD_DIM), 1.0 / HEAD_DIM)), BF16)
    ones = jnp.ones((HEAD_DIM,), F32)
    lower_pad = N_EXPERTS + N_GROUPS

    for layer in range(depth):
        lambda_init = 0.8 - 0.6 * math.exp(-0.3 * layer)
        w_cat = w_in[layer].astype(BF16)
        gain = jnp.concatenate([
            jnp.tile(diff_q_gain[layer] * (scale * LOG2E), 2 * N_DIFF_HEADS),
            jnp.tile(diff_k_gain[layer], 2 * N_DIFF_HEADS),
            jnp.tile(ones, 2 * N_DIFF_HEADS),
            jnp.tile(swa_q_gain[layer] * scale, N_SWA_HEADS),
            jnp.tile(swa_k_gain[layer], N_SWA_KV),
            jnp.tile(ones, N_SWA_KV)]).reshape(1, C_END).astype(F32)
        nmask = np.zeros((1, C_END), np.float32)
        nmask[:, C_DQ:C_DV] = 1.0
        nmask[:, C_SQ:C_SV] = 1.0
        nmask = jnp.asarray(nmask)
        g1 = norm1_gain[layer].reshape(1, d).astype(F32)

        qkv = _proj(h, g1, w_cat, bd, gain, nmask, TP)
        qkv_meta = _proj(meta_tokens.astype(F32), g1, w_cat, bd, gain, nmask, N_META)
        meta_pad = jnp.pad(qkv_meta, ((0, TQ - N_META), (0, 0)))

        lamv = jnp.pad(jnp.stack([lam_q1[layer], lam_k1[layer], lam_q2[layer], lam_k2[layer]]).astype(F32),
                       ((0, 4), (0, LANES - HEAD_DIM)))
        mixd = _diff_attention(qkv, meta_pad[:, C_DK:C_DV], meta_pad[:, C_DV:C_SQ], dblk, bm0, lamv,
                               diff_subln_gain[layer].reshape(1, LANES).astype(F32), batch, seq, lambda_init)
        mixs = _swa_attention(swa_sinks[layer].astype(F32), qkv, meta_pad[:BLOCK, C_SK:C_SV],
                              meta_pad[:BLOCK, C_SV:C_END], jnp.swapaxes(bt, -1, -2), batch, seq)

        wr = jnp.pad(jnp.concatenate([w_router[layer], w_group[layer]], axis=1),
                     ((0, 0), (0, LANES - lower_pad))).astype(BF16)
        br = jnp.pad(jnp.concatenate([b_router[layer], b_group[layer]]), (0, LANES - lower_pad)).reshape(1, LANES)
        h1, hb, rt, tinfo, cnt = _outproj(h, mixd, mixs, w_out[layer].astype(BF16),
                                          norm2_gain[layer].reshape(1, d).astype(F32), wr, br.astype(F32))

        nt = n // TM
        counts = cnt[0, :N_EXPERTS].astype(jnp.int32)
        nblk_e = (counts + EB - 1) // EB
        blk_end = jnp.cumsum(nblk_e)
        pstart = ((blk_end - nblk_e) * EB).astype(jnp.int32)
        n_blocks = -(-(2 * n + nt * N_EXPERTS * (ROW_ALIGN - 1) + N_EXPERTS * (EB - 1)) // EB)
        blk_ids = jnp.arange(n_blocks)
        blk_e = jnp.minimum(jnp.sum(blk_end[None, :] <= blk_ids[:, None], axis=1), N_EXPERTS - 1).astype(jnp.int32)
        n_act = blk_end[-1:].astype(jnp.int32)
        is_last = jnp.any((blk_end[None, :] == blk_ids[:, None] + 1) & (nblk_e[None, :] > 0), axis=1)
        zero_blocks = jnp.where(blk_ids >= n_act[0], 2, jnp.where(is_last, 1, 0)).astype(jnp.int32)
        ti = tinfo.reshape(nt, 8, LANES)
        run_len = ti[:, 0, :N_EXPERTS].astype(jnp.int32)
        run_start = pstart[None, :] + ti[:, 1, :N_EXPERTS].astype(jnp.int32)
        run_groups = (run_len + ROW_ALIGN - 1) // ROW_ALIGN
        assert n_blocks * EB * SORTED_RANGE < 2 ** 31
        runs = _chunk_table(run_start, run_groups)

        xs = _dispatch(zero_blocks, runs, hb, rt, n_blocks * EB)
        own = jnp.where(nblk_e > 0, jnp.arange(N_EXPERTS), N_EXPERTS)
        later = jnp.concatenate([lax.cummin(own[::-1])[::-1][1:], jnp.full((1,), N_EXPERTS)])
        nxt_e = jnp.where(later < N_EXPERTS, later, -1).astype(jnp.int32)
        blk_first = blk_end - nblk_e
        owned = (blk_first[None, :] <= blk_ids[:, None]) & (blk_ids[:, None] < blk_end[None, :])
        blk_rows = jnp.sum(jnp.where(owned, jnp.minimum(counts[None, :] - (blk_ids[:, None] - blk_first[None, :]) * EB,
                                                        EB), 0), axis=1).astype(jnp.int32)
        ys = _experts(blk_e, n_act, nxt_e, blk_rows, xs, w_gate[layer], w_up[layer], w_down[layer])
        h = _combine(runs, ys, h1, rt)
    return h.reshape(batch, seq, d)
```

```python
import functools
import math

import numpy as np
import jax
import jax.numpy as jnp
from jax import lax
from jax.experimental import pallas as pl
from jax.experimental.pallas import tpu as pltpu

F32 = jnp.float32
BF16 = jnp.bfloat16

HEAD_DIM = 64
N_DIFF_HEADS = 4
N_SWA_HEADS = 8
N_SWA_KV = 2
BLOCK = 128
N_META = 16
N_BUCKETS = 32
MAX_DISTANCE = 128
N_GROUPS = 4
EXPERTS_PER_GROUP = 8
N_EXPERTS = N_GROUPS * EXPERTS_PER_GROUP
EPS = 1e-6
NEG = -1e30
LOG2E = math.log2(math.e)

LANES = 128
MXU_DIM = 256
V7X_VMEM_BYTES = 64 * 1024 * 1024
VMEM_LIMIT = V7X_VMEM_BYTES * 3 // 4

TP = 1024
TM = 512
TQ = 256
ONES_ROWS = 16
EB = 512
X_BUFFERS = 4
EXPERT_PART = EB // 4
ROW_ALIGN = 8
CHUNK_ROWS = (32, 16, 8)
SROWS = -(-(2 * TM + N_EXPERTS * (ROW_ALIGN - 1)) // MXU_DIM) * MXU_DIM
SORTED_RANGE = 2048
CHUNK_SLOTS = (SROWS // CHUNK_ROWS[0],) + (N_EXPERTS,) * (len(CHUNK_ROWS) - 1)
CHUNK_OFFSETS = tuple(sum(CHUNK_SLOTS[:k]) for k in range(len(CHUNK_ROWS)))
CHUNK_COUNTS = sum(CHUNK_SLOTS)
assert SROWS <= SORTED_RANGE and CHUNK_COUNTS + len(CHUNK_ROWS) <= LANES

C_DQ, C_DK, C_DV, C_SQ, C_SK, C_SV, C_END = 0, 512, 1024, 1536, 2048, 2176, 2304
NORM_GROUPS = (0, 1, 2, 3, 6, 7, 8)


def _cparams(sem):
    return pltpu.CompilerParams(dimension_semantics=sem, vmem_limit_bytes=VMEM_LIMIT)


def _t5_bucket_np(dist):
    n = np.maximum(dist, 0)
    max_exact = N_BUCKETS // 2
    nf = np.maximum(n, 1).astype(np.float32)
    large = max_exact + (np.log(nf / np.float32(max_exact)) / np.float32(math.log(MAX_DISTANCE / max_exact))
                         * np.float32(N_BUCKETS - max_exact)).astype(np.int32)
    large = np.minimum(large, N_BUCKETS - 1)
    return np.where(n < max_exact, n, large)


def _bias_tables(rel_bias, tq):
    nd = 2 * BLOCK
    buckets = _t5_bucket_np(np.arange(nd))
    assert (buckets[MAX_DISTANCE:] == N_BUCKETS - 1).all()
    rb = rel_bias.astype(F32)
    r = np.arange(BLOCK)[:, None]
    c = np.arange(BLOCK)[None, :]
    d_own = r - c
    d_prev = BLOCK + r - c
    far = rb[N_BUCKETS - 1]

    def take(dist, heads):
        idx = jnp.asarray(buckets[np.clip(dist, 0, nd - 1)], jnp.int32)[None]
        out = jnp.zeros((heads.stop - heads.start,) + dist.shape, F32)
        for b in range(N_BUCKETS):
            out = jnp.where(idx == b, rb[b, heads].reshape((-1,) + (1,) * dist.ndim), out)
        return out

    hd = slice(0, N_DIFF_HEADS)
    far_d = far[hd][:, None, None]
    d0 = jnp.where(d_own[None] >= 0, take(d_own, hd) - far_d, NEG)
    d1 = take(d_prev, hd) - far_d
    dblk = jnp.stack([d0, d1], axis=1)
    rq = np.arange(tq)[:, None]
    cm = np.arange(LANES)[None, :]
    d_meta = (N_META + rq - cm)[:, :N_META]
    bm0 = jnp.pad(take(d_meta, hd) - far_d, ((0, 0), (0, 0), (0, LANES - N_META)), constant_values=NEG)

    hs = slice(N_DIFF_HEADS, N_DIFF_HEADS + N_SWA_HEADS)
    far_s = far[hs][:, None, None]
    d_meta_s = N_META + r - cm
    meta_first = jnp.where((cm < N_META)[None], take(d_meta_s, hs), NEG)
    meta_rest = jnp.where((cm < N_META)[None], jnp.broadcast_to(far_s, (N_SWA_HEADS, BLOCK, LANES)), NEG)
    prev_rest = jnp.where((c > r)[None], take(d_prev, hs), NEG)
    prev_first = jnp.full((N_SWA_HEADS, BLOCK, BLOCK), NEG, F32)
    own = jnp.where((d_own >= 0)[None], take(d_own, hs), NEG)
    bt = jnp.stack([jnp.concatenate([prev_first, own, meta_first[..., :N_META]], axis=-1),
                    jnp.concatenate([prev_rest, own, meta_rest[..., :N_META]], axis=-1)], axis=0)
    return dblk.astype(F32), bm0.astype(F32), bt.astype(F32)


def _proj_kernel(x_ref, g1_ref, w_ref, bd_ref, gain_ref, nmask_ref, o_ref):
    x = x_ref[...]
    a = x * lax.rsqrt(jnp.mean(x * x, axis=-1, keepdims=True) + EPS) * g1_ref[...]
    p = jnp.dot(a.astype(BF16), w_ref[...], preferred_element_type=F32)
    bd = bd_ref[...]
    for j in range(C_END // MXU_DIM):
        sl = slice(j * MXU_DIM, (j + 1) * MXU_DIM)
        pj = p[:, sl]
        if j in NORM_GROUPS:
            ms = jnp.dot((pj * pj).astype(BF16), bd, preferred_element_type=F32)
            pj = jnp.where(nmask_ref[:, sl] != 0.0, pj * lax.rsqrt(ms + EPS) * gain_ref[:, sl], pj)
        o_ref[:, sl] = pj.astype(BF16)


def _proj(x2, g1, w, bd, gain, nmask, tm):
    n = x2.shape[0]
    return pl.pallas_call(
        _proj_kernel,
        out_shape=jax.ShapeDtypeStruct((n, C_END), BF16),
        grid=(n // tm,),
        in_specs=[
            pl.BlockSpec((tm, x2.shape[1]), lambda i: (i, 0)),
            pl.BlockSpec(g1.shape, lambda i: (0, 0)),
            pl.BlockSpec(w.shape, lambda i: (0, 0)),
            pl.BlockSpec(bd.shape, lambda i: (0, 0)),
            pl.BlockSpec(gain.shape, lambda i: (0, 0)),
            pl.BlockSpec(nmask.shape, lambda i: (0, 0)),
        ],
        out_specs=pl.BlockSpec((tm, C_END), lambda i: (i, 0)),
        compiler_params=_cparams(("parallel",)),
        name="proj",
    )(x2, g1, w, bd, gain, nmask)


def _diff_kernel(qi_tab, t_tab, q_ref, k_ref, v_ref, km_ref, vm_ref, d_ref, bm0_ref, lamv_ref, gain_ref, o_ref,
                 bias_ref, mb_ref, qs_ref, kt_ref, vt_ref, s_buf, p_buf, a_buf, m_ref, acc_ref, *,
                 lambda_init, n_steps, n_far, n_near):
    tq = TQ
    nq = q_ref.shape[0] // tq
    nb = tq // BLOCK
    BIAS_LEFT, BIAS_DIAG, BIAS_NONE = 0, 1, 2

    d0 = d_ref[0, 0] * LOG2E
    d1 = d_ref[0, 1] * LOG2E
    zeros = jnp.zeros((BLOCK, BLOCK), F32)
    for a in range(nb):
        for b in range(nb):
            rs, cs = slice(a * BLOCK, (a + 1) * BLOCK), slice(b * BLOCK, (b + 1) * BLOCK)
            if a == b:
                blk = d0
            elif b == a + 1:
                blk = d1
            elif b > a:
                blk = zeros
            else:
                blk = jnp.full((BLOCK, BLOCK), NEG, F32)
            bias_ref[BIAS_DIAG, rs, cs] = blk
            bias_ref[BIAS_LEFT, rs, cs] = d1 if (b == 0 and a == nb - 1) else zeros
    bias_ref[BIAS_NONE] = jnp.zeros((tq, tq), F32)
    mb_ref[0] = jnp.zeros((N_META, tq), F32)
    mb_ref[1] = bm0_ref[0, :N_META, :] * LOG2E

    lane = lax.broadcasted_iota(jnp.int32, (tq, LANES), 1)
    for i in range(nq):
        rows = slice(i * tq, (i + 1) * tq)
        q = q_ref[rows, :].astype(F32)
        qs_ref[i] = jnp.transpose(jnp.concatenate([jnp.where(lane < HEAD_DIM, q, 0.0),
                                                   jnp.where(lane >= HEAD_DIM, q, 0.0)], axis=0)).astype(BF16)
        vt_ref[i, :LANES, :] = jnp.transpose(v_ref[rows, :].astype(F32)).astype(BF16)
        kt_ref[i] = k_ref[rows, :]
    vt_ref[nq, :LANES, :] = jnp.transpose(vm_ref[...].astype(F32)).astype(BF16)
    kt_ref[nq] = km_ref[...]
    vt_ref[:, LANES:, :] = jnp.ones((nq + 1, ONES_ROWS, tq), BF16)
    acc_ref[...] = jnp.zeros(acc_ref.shape, F32)
    m_ref[...] = jnp.full(m_ref.shape, NEG, F32)
    lv = lamv_ref[...]
    lam = (jnp.exp(jnp.sum(lv[0:1] * lv[1:2], axis=-1, keepdims=True))
           - jnp.exp(jnp.sum(lv[2:3] * lv[3:4], axis=-1, keepdims=True)) + lambda_init)

    FAR, NEAR, META = 0, 1, 2

    def stage_a(n, slot, kind):
        qi, t = qi_tab[n], t_tab[n]
        if kind == META:
            s = jnp.dot(kt_ref[nq, :N_META, :], qs_ref[qi], preferred_element_type=F32)
            s_buf[slot, :N_META] = s + jnp.tile(mb_ref[jnp.where(qi == 0, 1, 0)], (1, 2))
            return
        s = jnp.dot(kt_ref[t - 1], qs_ref[qi], preferred_element_type=F32)
        if kind == NEAR:
            which = jnp.where(t == qi + 1, BIAS_DIAG, jnp.where(t == qi, BIAS_LEFT, BIAS_NONE))
            s = s + jnp.tile(bias_ref[which], (1, 2))
        s_buf[slot] = s

    def stage_b(n, slot, kind):
        qi = qi_tab[n]
        rows = slice(0, N_META if kind == META else tq)
        s = s_buf[slot, rows]
        m_prev = m_ref[qi]
        m_new = jnp.maximum(m_prev, jnp.max(s, axis=0, keepdims=True))
        a_buf[slot] = jnp.exp2(m_prev - m_new)
        p_buf[slot, rows] = jnp.exp2(s - m_new[0:1]).astype(BF16)
        m_ref[qi] = m_new

    def stage_c(n, slot, kind):
        qi, t = qi_tab[n], t_tab[n]
        if kind == META:
            pv = jnp.dot(vt_ref[nq, :, :N_META], p_buf[slot, :N_META], preferred_element_type=F32)
        else:
            pv = jnp.dot(vt_ref[t - 1], p_buf[slot], preferred_element_type=F32)
        acc_ref[qi] = a_buf[slot][0:1] * acc_ref[qi] + pv

    LEAD, SLOTS, UNROLL = 2, 3, 12
    assert UNROLL % SLOTS == 0

    def pipeline(base, count, kind, biased_from=None):
        def kind_a(j):
            return kind if biased_from is None or j < biased_from else NEAR

        if count <= 2 * LEAD:
            for j in range(count):
                stage_a(base + j, 0, kind_a(j))
                stage_b(base + j, 0, kind)
                stage_c(base + j, 0, kind)
            return
        for j in range(2 * LEAD):
            stage_a(base + j, j % SLOTS, kind_a(j))
            if j >= LEAD:
                stage_b(base + j - LEAD, (j - LEAD) % SLOTS, kind)

        def steps(n, first, count):
            for j in range(count):
                stage_a(base + n + j + 2 * LEAD, (j + 2 * LEAD) % SLOTS, kind_a(first + j + 2 * LEAD))
                stage_b(base + n + j + LEAD, (j + LEAD) % SLOTS, kind)
                stage_c(base + n + j, j % SLOTS, kind)

        n_steady = count - 2 * LEAD
        n_blocks = n_steady // UNROLL
        switch = n_blocks if biased_from is None else (biased_from - 2 * LEAD) // UNROLL
        assert biased_from is None or (biased_from - 2 * LEAD) % UNROLL == 0
        for lo, hi in ((0, min(switch, n_blocks)), (min(switch, n_blocks), n_blocks)):
            lax.fori_loop(lo, hi, lambda k, carry, lo=lo: (steps(UNROLL * k, UNROLL * lo, UNROLL), carry)[1], 0)
        steps(UNROLL * n_blocks, UNROLL * n_blocks, n_steady - UNROLL * n_blocks)
        for j in range(n_steady, count):
            if j + LEAD < count:
                stage_b(base + j + LEAD, (j + LEAD) % SLOTS, kind)
            stage_c(base + j, j % SLOTS, kind)

    if n_far >= 2 * LEAD:
        pipeline(0, n_far + n_near, FAR, 2 * LEAD + UNROLL * ((n_far - 2 * LEAD) // UNROLL))
    else:
        pipeline(0, n_far + n_near, NEAR)
    pipeline(n_far + n_near, n_steps - n_far - n_near, META)

    for i in range(nq):
        acc = acc_ref[i]
        o = acc[:LANES] * (1.0 / acc[LANES:LANES + 1])
        d = o[:, :tq] - lam * o[:, tq:]
        y = d * lax.rsqrt(jnp.mean(d * d, axis=0, keepdims=True) + EPS) * jnp.tile(gain_ref[...], (1, tq // LANES))
        o_ref[i * tq:(i + 1) * tq, :] = jnp.transpose(y * (1.0 - lambda_init)).astype(BF16)


def _diff_attention(qkv, km, vm, dblk, bm0, lamv, gain, batch, seq, lambda_init):
    nq = seq // TQ
    far = [(qi, t) for qi in range(nq) for t in range(1, qi)]
    near = [(qi, t) for qi in range(nq) for t in (qi, qi + 1) if t >= 1]
    meta = [(qi, 0) for qi in range(nq)]
    steps = far + near + meta
    qi_tab = jnp.asarray([s[0] for s in steps], jnp.int32)
    t_tab = jnp.asarray([s[1] for s in steps], jnp.int32)
    kern = functools.partial(_diff_kernel, lambda_init=lambda_init, n_steps=len(steps), n_far=len(far),
                             n_near=len(near))
    return pl.pallas_call(
        kern,
        out_shape=jax.ShapeDtypeStruct((batch * seq, N_DIFF_HEADS * LANES), BF16),
        grid_spec=pltpu.PrefetchScalarGridSpec(
            num_scalar_prefetch=2,
            grid=(batch, N_DIFF_HEADS),
            in_specs=[
                pl.BlockSpec((seq, LANES), lambda b, h, *_: (b, C_DQ // LANES + h)),
                pl.BlockSpec((seq, LANES), lambda b, h, *_: (b, C_DK // LANES + h)),
                pl.BlockSpec((seq, LANES), lambda b, h, *_: (b, C_DV // LANES + h)),
                pl.BlockSpec((TQ, LANES), lambda b, h, *_: (0, h)),
                pl.BlockSpec((TQ, LANES), lambda b, h, *_: (0, h)),
                pl.BlockSpec((1, 2, BLOCK, BLOCK), lambda b, h, *_: (h, 0, 0, 0)),
                pl.BlockSpec((1, LANES, TQ), lambda b, h, *_: (h, 0, 0)),
                pl.BlockSpec(lamv.shape, lambda b, h, *_: (0, 0)),
                pl.BlockSpec((LANES, LANES), lambda b, h, *_: (0, 0)),
            ],
            out_specs=pl.BlockSpec((seq, LANES), lambda b, h, *_: (b, h)),
            scratch_shapes=[
                pltpu.VMEM((3, TQ, TQ), F32),
                pltpu.VMEM((2, N_META, TQ), F32),
                pltpu.VMEM((nq, LANES, 2 * TQ), BF16),
                pltpu.VMEM((nq + 1, TQ, LANES), BF16),
                pltpu.VMEM((nq + 1, LANES + ONES_ROWS, TQ), BF16),
                pltpu.VMEM((3, TQ, 2 * TQ), F32),
                pltpu.VMEM((3, TQ, 2 * TQ), BF16),
                pltpu.VMEM((3, 8, 2 * TQ), F32),
                pltpu.VMEM((nq, 8, 2 * TQ), F32),
                pltpu.VMEM((nq, LANES + ONES_ROWS, 2 * TQ), F32),
            ],
        ),
        compiler_params=_cparams(("parallel", "parallel")),
        name="diff_attention",
    )(qi_tab, t_tab, qkv, qkv, qkv, km, vm, jnp.swapaxes(dblk, -1, -2), jnp.swapaxes(bm0, -1, -2), lamv,
      jnp.broadcast_to(gain.reshape(LANES, 1), (LANES, LANES)))


def _swa_kernel(sink_ref, q_ref, k_ref, v_ref, km_ref, vm_ref, bt_ref, o_ref, kd_ref, vt_ref,
                s_scr, p_scr, inv_scr):
    nkb = k_ref.shape[0] // BLOCK
    lane = lax.broadcasted_iota(jnp.int32, (BLOCK, LANES), 1)
    pairs = [(g, u) for g in range(N_SWA_KV) for u in range(2)]

    def both_halves(k):
        k0, k1 = k[:, :HEAD_DIM], k[:, HEAD_DIM:]
        return jnp.concatenate([k0, k0, k1, k1], axis=1)

    def prepare(j, carry):
        rows = pl.ds(pl.multiple_of(j * BLOCK, BLOCK), BLOCK)
        kd_ref[j] = both_halves(k_ref[rows, :])
        vt_ref[j] = jnp.transpose(v_ref[rows, :].astype(F32)).astype(BF16)
        return carry
    lax.fori_loop(0, nkb, prepare, 0)
    kd_ref[nkb] = both_halves(km_ref[...])
    vt_ref[nkb] = jnp.transpose(vm_ref[...].astype(F32)).astype(BF16)

    def scores(n, slot):
        first = jnp.where(n == 0, 0, 1)
        prev = jnp.maximum(n - 1, 0)
        r_q = pl.multiple_of(n * BLOCK, BLOCK)
        for c, (g, u) in enumerate(pairs):
            ks = slice(g * LANES, (g + 1) * LANES)
            kcat = jnp.concatenate([kd_ref[prev, :, ks], kd_ref[n, :, ks], kd_ref[nkb, :N_META, ks]], axis=0)
            h0 = 4 * g + 2 * u
            qp = q_ref[pl.ds(r_q, BLOCK), (2 * g + u) * LANES:(2 * g + u + 1) * LANES].astype(F32)
            qs = jnp.transpose(jnp.concatenate([jnp.where(lane < HEAD_DIM, qp, 0.0),
                                                jnp.where(lane >= HEAD_DIM, qp, 0.0)], axis=0)).astype(BF16)
            s = jnp.dot(kcat, qs, preferred_element_type=F32)
            s_scr[slot, c] = s + jnp.concatenate([bt_ref[first, h0], bt_ref[first, h0 + 1]], axis=1)

    def exponentials(slot):
        for c, (g, u) in enumerate(pairs):
            h0 = 4 * g + 2 * u
            s = s_scr[slot, c]
            sink = jnp.concatenate([sink_ref[h0:h0 + 1, :], sink_ref[h0 + 1:h0 + 2, :]], axis=1)
            m = jnp.maximum(jnp.max(s, axis=0, keepdims=True), sink)
            p = jnp.exp(s - m)
            p_scr[slot, c] = p.astype(BF16)
            inv_scr[slot, c] = jnp.broadcast_to(1.0 / (jnp.sum(p, axis=0, keepdims=True) + jnp.exp(sink - m)),
                                                inv_scr.shape[2:])

    def values(n, slot):
        prev = jnp.maximum(n - 1, 0)
        r_q = pl.multiple_of(n * BLOCK, BLOCK)
        for c, (g, u) in enumerate(pairs):
            vs = slice(g * HEAD_DIM, (g + 1) * HEAD_DIM)
            vcat = jnp.concatenate([vt_ref[prev, vs, :], vt_ref[n, vs, :]], axis=1)
            o = (jnp.dot(vcat, p_scr[slot, c, :2 * BLOCK], preferred_element_type=F32)
                 + jnp.dot(vt_ref[nkb, vs, :N_META], p_scr[slot, c, 2 * BLOCK:], preferred_element_type=F32)
                 ) * inv_scr[slot, c][0:1]
            ot = jnp.transpose(o)
            o_ref[pl.ds(r_q, BLOCK), (2 * g + u) * LANES:(2 * g + u + 1) * LANES] = (
                jnp.concatenate([ot[:BLOCK], ot[BLOCK:]], axis=1).astype(BF16))

    scores(0, 0)
    scores(1, 1)
    exponentials(0)

    def blocks(n, count):
        for j in range(count):
            scores(n + j + 2, j % 2)
            exponentials((j + 1) % 2)
            values(n + j, j % 2)

    unroll = 4
    n_steady = nkb - 2
    lax.fori_loop(0, n_steady // unroll, lambda k, carry: (blocks(unroll * k, unroll), carry)[1], 0)
    blocks(n_steady // unroll * unroll, n_steady % unroll)
    exponentials(1)
    values(nkb - 2, 0)
    values(nkb - 1, 1)


def _swa_attention(sinks, qkv, km, vm, bt, batch, seq):
    nkb = seq // BLOCK
    swa_q = N_SWA_HEADS * HEAD_DIM
    assert nkb % 2 == 0
    sinkv =jnp.broadcast_to(sinks.reshape(N_SWA_HEADS, 1), (N_SWA_HEADS, LANES))
    return pl.pallas_call(
        _swa_kernel,
        out_shape=jax.ShapeDtypeStruct((batch * seq, N_SWA_HEADS * HEAD_DIM), BF16),
        grid=(batch,),
        in_specs=[
            pl.BlockSpec(sinkv.shape, lambda b: (0, 0)),
            pl.BlockSpec((seq, swa_q), lambda b: (b, C_SQ // swa_q)),
            pl.BlockSpec((seq, LANES), lambda b: (b, C_SK // LANES)),
            pl.BlockSpec((seq, LANES), lambda b: (b, C_SV // LANES)),
            pl.BlockSpec(km.shape, lambda b: (0, 0)),
            pl.BlockSpec(vm.shape, lambda b: (0, 0)),
            pl.BlockSpec(bt.shape, lambda b: (0, 0, 0, 0)),
        ],
        out_specs=pl.BlockSpec((seq, swa_q), lambda b: (b, 0)),
        scratch_shapes=[pltpu.VMEM((nkb + 1, BLOCK, 2 * LANES), BF16),
                        pltpu.VMEM((nkb + 1, LANES, BLOCK), BF16),
                        pltpu.VMEM((2, 4, 2 * BLOCK + N_META, 2 * BLOCK), F32),
                        pltpu.VMEM((2, 4, 2 * BLOCK + N_META, 2 * BLOCK), BF16),
                        pltpu.VMEM((2, 4, 8, 2 * BLOCK), F32)],
        compiler_params=_cparams(("parallel",)),
        name="swa_attention",
    )(sinkv, qkv, qkv, qkv, km, vm, bt)


def _outproj_kernel(x_ref, md_ref, ms_ref, wo_ref, g2_ref, wr_ref, br_ref,
                    h_ref, hb_ref, rt_ref, ti_ref, cnt_ref, c_ref, lg_ref):
    i = pl.program_id(0)

    @pl.when(i == 0)
    def _init():
        c_ref[...] = jnp.zeros(c_ref.shape, F32)
        lg_ref[...] = jnp.zeros(lg_ref.shape, F32)

    lg_prev = lg_ref[...]
    half = md_ref.shape[1]
    h = (x_ref[...]
         + jnp.dot(md_ref[...], wo_ref[:half, :], preferred_element_type=F32)
         + jnp.dot(ms_ref[...], wo_ref[half:, :], preferred_element_type=F32))
    h_ref[...] = h
    hn = h * lax.rsqrt(jnp.mean(h * h, axis=-1, keepdims=True) + EPS) * g2_ref[...]
    hb = hn.astype(BF16)
    hb_ref[...] = hb
    lg_ref[...] = jnp.dot(hb, wr_ref[...], preferred_element_type=F32) + br_ref[...]
    _route_tile(lg_prev, jnp.where(i > 0, 1.0, 0.0), rt_ref, ti_ref, c_ref)

    @pl.when(i == pl.num_programs(0) - 1)
    def _fin():
        cnt_ref[...] = c_ref[...]


def _route_tile(lg, live, rt_ref, ti_ref, c_ref):
    tm = lg.shape[0]
    lane_i = lax.broadcasted_iota(jnp.int32, lg.shape, 1)
    lane = lane_i.astype(F32)
    big = float(4 * LANES)
    is_g = (lane_i >= N_EXPERTS) & (lane_i < N_EXPERTS + N_GROUPS)
    glm = jnp.where(is_g, lg, -jnp.inf)
    gmax = jnp.max(glm, axis=1, keepdims=True)
    gidx = jnp.min(jnp.where(glm == gmax, lane, big), axis=1, keepdims=True) - N_EXPERTS
    gsum = jnp.sum(jnp.where(is_g, jnp.exp(lg - gmax), 0.0), axis=1, keepdims=True)
    g_w = 1.0 / gsum
    lane_grp = (lane_i >> 3).astype(F32)
    in_grp = (lane_i < N_EXPERTS) & (lane_grp == gidx)
    el = jnp.where(in_grp, lg, -jnp.inf)
    t1 = jnp.max(el, axis=1, keepdims=True)
    j1 = jnp.min(jnp.where(el == t1, lane, big), axis=1, keepdims=True)
    el2 = jnp.where(lane == j1, -jnp.inf, el)
    t2 = jnp.max(el2, axis=1, keepdims=True)
    j2 = jnp.min(jnp.where(el2 == t2, lane, big), axis=1, keepdims=True)
    e2 = jnp.exp(t2 - t1)
    den = 1.0 + e2
    gate1 = g_w / den
    gate2 = g_w * e2 / den

    o1 = lane == j1
    o2 = lane == j2
    onehot = jnp.where(o1 | o2, 1.0, 0.0).astype(BF16)
    rr = lax.broadcasted_iota(jnp.int32, (tm, tm), 0)
    cc = lax.broadcasted_iota(jnp.int32, (tm, tm), 1)
    lower = jnp.where(rr > cc, 1.0, 0.0).astype(BF16)
    pfx = jnp.dot(lower, onehot, preferred_element_type=F32)
    cnt_tile = jnp.sum(onehot.astype(F32), axis=0, keepdims=True)
    groups = jnp.floor((cnt_tile + (ROW_ALIGN - 1)) * (1.0 / ROW_ALIGN))
    er = lax.broadcasted_iota(jnp.int32, (LANES, LANES), 0)
    ec = lax.broadcasted_iota(jnp.int32, (LANES, LANES), 1)
    before = jnp.where(er < ec, 1.0, 0.0).astype(BF16)
    cbase = ROW_ALIGN * jnp.dot(jnp.broadcast_to(groups, (8, LANES)).astype(BF16), before,
                                preferred_element_type=F32)[0:1]
    at = pfx + cbase
    pos1 = jnp.sum(jnp.where(o1, at, 0.0), axis=1, keepdims=True)
    pos2 = jnp.sum(jnp.where(o2, at, 0.0), axis=1, keepdims=True)
    rt_ref[...] = jnp.where(lane_i == 0, gate1,
                            jnp.where(lane_i == 1, gate2,
                                      jnp.where(lane_i == 2, pos1,
                                                jnp.where(lane_i == 3, pos2, 0.0))))
    c_old = c_ref[...]
    c_ref[...] = c_old + groups * (ROW_ALIGN * live)
    row8 = lax.broadcasted_iota(jnp.int32, (8, LANES), 0)
    ti_ref[...] = jnp.where(row8 == 0, cnt_tile, jnp.where(row8 == 1, c_old, 0.0))


def _outproj(x2, mixd, mixs, wo, g2, wr, br):
    n, d = x2.shape
    nt = n // TM

    def proj_tile(i):
        return (jnp.minimum(i, nt - 1), 0)

    def route_tile(i):
        return (jnp.maximum(i - 1, 0), 0)

    return pl.pallas_call(
        _outproj_kernel,
        out_shape=(jax.ShapeDtypeStruct((n, d), F32),
                   jax.ShapeDtypeStruct((n, d), BF16),
                   jax.ShapeDtypeStruct((n, LANES), F32),
                   jax.ShapeDtypeStruct((nt * 8, LANES), F32),
                   jax.ShapeDtypeStruct((8, LANES), F32)),
        grid=(nt + 1,),
        in_specs=[
            pl.BlockSpec((TM, d), proj_tile),
            pl.BlockSpec((TM, mixd.shape[1]), proj_tile),
            pl.BlockSpec((TM, mixs.shape[1]), proj_tile),
            pl.BlockSpec(wo.shape, lambda i: (0, 0)),
            pl.BlockSpec(g2.shape, lambda i: (0, 0)),
            pl.BlockSpec(wr.shape, lambda i: (0, 0)),
            pl.BlockSpec(br.shape, lambda i: (0, 0)),
        ],
        out_specs=(pl.BlockSpec((TM, d), proj_tile),
                   pl.BlockSpec((TM, d), proj_tile),
                   pl.BlockSpec((TM, LANES), route_tile),
                   pl.BlockSpec((8, LANES), route_tile),
                   pl.BlockSpec((8, LANES), lambda i: (0, 0))),
        scratch_shapes=[pltpu.VMEM((8, LANES), F32), pltpu.VMEM((TM, LANES), F32)],
        compiler_params=_cparams(("arbitrary",)),
        name="outproj_router",
    )(x2, mixd, mixs, wo, g2, wr, br)


def _chunk_table(run_start, run_groups):
    nt = run_start.shape[0]
    sorted_start = (jnp.cumsum(run_groups, axis=1) - run_groups) * ROW_ALIGN
    word = run_start * SORTED_RANGE + sorted_start
    done = jnp.zeros_like(run_groups)
    segments, counts = [], []
    for k, rows in enumerate(CHUNK_ROWS):
        per_run = run_groups // (rows // ROW_ALIGN)
        if k > 0:
            per_run = per_run % 2
        last = jnp.cumsum(per_run, axis=1)[:, None, :]
        first = last - per_run[:, None, :]
        slot = jnp.arange(CHUNK_SLOTS[k], dtype=jnp.int32)[None, :, None]
        value = word[:, None, :] + (done[:, None, :] + (slot - first) * rows) * (SORTED_RANGE + 1)
        segments.append(jnp.sum(jnp.where((first <= slot) & (slot < last), value, 0), axis=2))
        counts.append(last[:, 0, -1:])
        done = done + per_run * rows
    pad = jnp.zeros((nt, LANES - CHUNK_COUNTS - len(CHUNK_ROWS)), jnp.int32)
    return jnp.concatenate(segments + counts + [pad], axis=1).astype(jnp.int32).reshape(nt, 1, LANES)


def _for_each_chunk(chunks_ref, fn):
    def issue(k, c, priority):
        word = chunks_ref[0, 0, CHUNK_OFFSETS[k] + c]
        fn(pl.multiple_of(word >> (SORTED_RANGE.bit_length() - 1), ROW_ALIGN),
           pl.multiple_of(word & (SORTED_RANGE - 1), ROW_ALIGN), CHUNK_ROWS[k], priority)

    for k in range(len(CHUNK_ROWS)):
        _in_pairs(chunks_ref[0, 0, CHUNK_COUNTS + k], lambda c, second, k=k: issue(k, c, second))


def _in_pairs(count, fn):
    half = count // 2

    def body(c, carry):
        fn(c, 0)
        fn(half + c, 1)
        return carry
    lax.fori_loop(0, half, body, 0)

    @pl.when(count % 2 == 1)
    def _():
        fn(count - 1, 0)


def _wait_chunks(chunks_ref, make_copy):
    for k, rows in enumerate(CHUNK_ROWS):
        _in_pairs(chunks_ref[0, 0, CHUNK_COUNTS + k], lambda c, second, rows=rows: make_copy(rows).wait())


def _dispatch_kernel(zf_ref, cur_ref, prv_ref, hb_ref, rt_ref, xs_ref, sbuf, zbuf, sem, zsem):
    i = pl.program_id(0)
    nt = pl.num_programs(0)
    slot = i % 2
    tm, d = hb_ref.shape

    def for_zero_blocks(kind, fn):
        def body(b, carry):
            @pl.when(zf_ref[b] == kind)
            def _():
                fn(pltpu.make_async_copy(zbuf, xs_ref.at[pl.ds(pl.multiple_of(b * EB, EB), EB)],
                                         zsem.at[kind - 1]))
            return carry
        lax.fori_loop(0, zf_ref.shape[0], body, 0)

    @pl.when(i == 0)
    def _():
        zbuf[...] = jnp.zeros(zbuf.shape, zbuf.dtype)
        for_zero_blocks(1, lambda c: c.start())
        for_zero_blocks(2, lambda c: c.start())
        for_zero_blocks(1, lambda c: c.wait())

    pos_t = jnp.transpose(rt_ref[...])
    srow = lax.broadcasted_iota(jnp.int32, (SROWS, tm), 0).astype(F32)
    sel = jnp.where(srow == pos_t[2:3, :], 1.0, jnp.where(srow == pos_t[3:4, :], 1.0, 0.0)).astype(BF16)
    srt = jnp.dot(sel, hb_ref[...], preferred_element_type=F32)
    bits = pltpu.bitcast(srt, jnp.uint32)
    sbuf[slot] = (bits[:, d // 2:] & jnp.uint32(0xFFFF0000)) | (bits[:, :d // 2] >> 16)

    def chunk_copy(run_row, sorted_row, rows, sl):
        return pltpu.make_async_copy(sbuf.at[sl, pl.ds(sorted_row, rows)], xs_ref.at[pl.ds(run_row, rows)],
                                     sem.at[sl])

    _for_each_chunk(cur_ref, lambda run_row, sorted_row, rows, priority:
                    chunk_copy(run_row, sorted_row, rows, slot).start(priority=priority))

    @pl.when(i > 0)
    def _():
        _wait_chunks(prv_ref, lambda rows: chunk_copy(0, 0, rows, 1 - slot))

    @pl.when(i == nt - 1)
    def _():
        _wait_chunks(cur_ref, lambda rows: chunk_copy(0, 0, rows, slot))
        for_zero_blocks(2, lambda c: c.wait())


def _dispatch(zero_blocks, runs, hb, rt, n_rows):
    n, d = hb.shape
    return pl.pallas_call(
        _dispatch_kernel,
        out_shape=jax.ShapeDtypeStruct((n_rows, d // 2), jnp.uint32),
        grid_spec=pltpu.PrefetchScalarGridSpec(
            num_scalar_prefetch=1,
            grid=(n // TM,),
            in_specs=[
                pl.BlockSpec((1, 1, LANES), lambda i, zf: (i, 0, 0), memory_space=pltpu.SMEM),
                pl.BlockSpec((1, 1, LANES), lambda i, zf: (jnp.maximum(i - 1, 0), 0, 0), memory_space=pltpu.SMEM),
                pl.BlockSpec((TM, d), lambda i, zf: (i, 0)),
                pl.BlockSpec((TM, LANES), lambda i, zf: (i, 0)),
            ],
            out_specs=pl.BlockSpec(memory_space=pl.ANY),
            scratch_shapes=[pltpu.VMEM((2, SROWS, d // 2), jnp.uint32), pltpu.VMEM((EB, d // 2), jnp.uint32),
                            pltpu.SemaphoreType.DMA((2,)), pltpu.SemaphoreType.DMA((2,))],
        ),
        compiler_params=_cparams(("arbitrary",)),
        name="dispatch",
    )(zero_blocks, runs, runs, hb, rt)


def _experts_kernel(be_ref, na_ref, nxt_ref, rows_ref, xs_hbm, wg_hbm, wu_hbm, wd_hbm, ys_ref,
                    xbuf, wgf, wuf, wdf, wgb, wub, wdb, sem, xsem):
    b = pl.program_id(0)
    n_act = na_ref[0]
    slot = b % X_BUFFERS

    def x_copy(blk):
        s = blk % X_BUFFERS
        return pltpu.make_async_copy(xs_hbm.at[pl.ds(pl.multiple_of(blk * EB, EB), EB)], xbuf.at[s], xsem.at[s])

    @pl.when(b == 0)
    def _():
        for blk in range(X_BUFFERS - 1):
            @pl.when(blk < n_act)
            def _(blk=blk):
                x_copy(blk).start()

    @pl.when(b + X_BUFFERS - 1 < n_act)
    def _():
        x_copy(b + X_BUFFERS - 1).start()

    def weight_copies(e):
        return (pltpu.make_async_copy(wg_hbm.at[e], wgf, sem.at[0]),
                pltpu.make_async_copy(wu_hbm.at[e], wuf, sem.at[1]),
                pltpu.make_async_copy(wd_hbm.at[e], wdf, sem.at[2]))

    @pl.when(b == 0)
    def _():
        for c in weight_copies(be_ref[0]):
            c.start()

    @pl.when(b < na_ref[0])
    def _():
        e = be_ref[b]
        changed = jnp.logical_or(b == 0, be_ref[jnp.maximum(b - 1, 0)] != e)

        @pl.when(changed)
        def _load():
            for c in weight_copies(e):
                c.wait()
            wgb[...] = wgf[...].astype(BF16)
            wub[...] = wuf[...].astype(BF16)
            wdb[...] = wdf[...].astype(BF16)
            nxt = nxt_ref[e]

            @pl.when(nxt >= 0)
            def _():
                for c in weight_copies(nxt):
                    c.start()

        x_copy(b).wait()

    def compute(rows):
        w = xbuf[slot, rows, :]
        x_lo = pltpu.bitcast(w << 16, F32).astype(BF16)
        x_hi = pltpu.bitcast(w & jnp.uint32(0xFFFF0000), F32).astype(BF16)
        dh = w.shape[1]
        g = (jnp.dot(x_lo, wgb[:dh, :], preferred_element_type=F32)
             + jnp.dot(x_hi, wgb[dh:, :], preferred_element_type=F32))
        u = (jnp.dot(x_lo, wub[:dh, :], preferred_element_type=F32)
             + jnp.dot(x_hi, wub[dh:, :], preferred_element_type=F32))
        hdn = g * (1.0 / (1.0 + jnp.exp(-g))) * u
        y = jnp.dot(hdn.astype(BF16), wdb[...], preferred_element_type=F32)
        bits = pltpu.bitcast(y.astype(BF16).astype(F32), jnp.uint32)
        ys_ref[rows, :] = (bits[:, dh:] & jnp.uint32(0xFFFF0000)) | (bits[:, :dh] >> 16)

    n_rows = rows_ref[b]

    part = EXPERT_PART
    whole = n_rows > 3 * part
    for lo, size, cond in ((0, 4 * part, whole),
                           (0, 2 * part, jnp.logical_and(n_rows > part, n_rows <= 3 * part)),
                           (0, part, jnp.logical_and(n_rows > 0, n_rows <= part)),
                           (2 * part, part, jnp.logical_and(n_rows > 2 * part, n_rows <= 3 * part))):
        @pl.when(cond)
        def _(lo=lo, size=size):
            compute(pl.ds(lo, size))

    for k in range(4):
        @pl.when(n_rows <= k * part)
        def _(k=k):
            ys_ref[pl.ds(k * part, part), :] = jnp.zeros((part, ys_ref.shape[1]), ys_ref.dtype)


def _experts(blk_e, n_act, nxt_e, blk_rows, xs, w_gate, w_up, w_down):
    p, dh = xs.shape
    d = 2 * dh
    de = w_gate.shape[2]

    return pl.pallas_call(
        _experts_kernel,
        out_shape=jax.ShapeDtypeStruct((p, dh), jnp.uint32),
        grid_spec=pltpu.PrefetchScalarGridSpec(
            num_scalar_prefetch=4,
            grid=(p // EB,),
            in_specs=[
                pl.BlockSpec(memory_space=pl.ANY),
                pl.BlockSpec(memory_space=pl.ANY),
                pl.BlockSpec(memory_space=pl.ANY),
                pl.BlockSpec(memory_space=pl.ANY),
            ],
            out_specs=pl.BlockSpec((EB, dh), lambda b, be, na, nx, br: (b, 0)),
            scratch_shapes=[pltpu.VMEM((X_BUFFERS, EB, dh), jnp.uint32),
                            pltpu.VMEM((d, de), F32), pltpu.VMEM((d, de), F32), pltpu.VMEM((de, d), F32),
                            pltpu.VMEM((d, de), BF16), pltpu.VMEM((d, de), BF16), pltpu.VMEM((de, d), BF16),
                            pltpu.SemaphoreType.DMA((3,)), pltpu.SemaphoreType.DMA((X_BUFFERS,))],
        ),
        compiler_params=_cparams(("arbitrary",)),
        name="experts",
    )(blk_e, n_act, nxt_e, blk_rows, xs, w_gate, w_up, w_down)


def _combine_kernel(cur_ref, nxt_ref, ys_ref, h_ref, rt_ref, o_ref, ybuf, sem):
    i = pl.program_id(0)
    nt = pl.num_programs(0)
    slot = i % 2
    tm = h_ref.shape[0]

    def chunk_copy(run_row, sorted_row, rows, sl):
        return pltpu.make_async_copy(ys_ref.at[pl.ds(run_row, rows)], ybuf.at[sl, pl.ds(sorted_row, rows)],
                                     sem.at[sl])

    @pl.when(i == 0)
    def _():
        ybuf[...] = jnp.zeros(ybuf.shape, ybuf.dtype)
        _for_each_chunk(cur_ref, lambda run_row, sorted_row, rows, priority:
                        chunk_copy(run_row, sorted_row, rows, 0).start(priority=priority))

    @pl.when(i + 1 < nt)
    def _():
        _for_each_chunk(nxt_ref, lambda run_row, sorted_row, rows, priority:
                        chunk_copy(run_row, sorted_row, rows, 1 - slot).start(priority=priority))

    _wait_chunks(cur_ref, lambda rows: chunk_copy(0, 0, rows, slot))

    rt = rt_ref[...]
    w = ybuf[slot]
    dh = w.shape[1]
    y_lo = pltpu.bitcast(w << 16, F32).astype(BF16)
    y_hi = pltpu.bitcast(w & jnp.uint32(0xFFFF0000), F32).astype(BF16)
    col = lax.broadcasted_iota(jnp.int32, (tm, SROWS), 1).astype(F32)
    wsel = jnp.where(col == rt[:, 2:3], rt[:, 0:1], jnp.where(col == rt[:, 3:4], rt[:, 1:2], 0.0)).astype(BF16)
    for half, yb in ((slice(0, dh), y_lo), (slice(dh, 2 * dh), y_hi)):
        o_ref[:, half] = h_ref[:, half] + jnp.dot(wsel, yb, preferred_element_type=F32)


def _combine(runs, ys, h1, rt):
    n, d = h1.shape
    nt = n // TM
    return pl.pallas_call(
        _combine_kernel,
        out_shape=jax.ShapeDtypeStruct((n, d), F32),
        grid=(nt,),
        in_specs=[
            pl.BlockSpec((1, 1, LANES), lambda i: (i, 0, 0), memory_space=pltpu.SMEM),
            pl.BlockSpec((1, 1, LANES), lambda i: (jnp.minimum(i + 1, nt - 1), 0, 0), memory_space=pltpu.SMEM),
            pl.BlockSpec(memory_space=pl.ANY),
            pl.BlockSpec((TM, d), lambda i: (i, 0)),
            pl.BlockSpec((TM, LANES), lambda i: (i, 0)),
        ],
        out_specs=pl.BlockSpec((TM, d), lambda i: (i, 0)),
        scratch_shapes=[pltpu.VMEM((2, SROWS, d // 2), jnp.uint32), pltpu.SemaphoreType.DMA((2,))],
        compiler_params=_cparams(("arbitrary",)),
        name="combine",
    )(runs, runs, ys, h1, rt)


def kernel(x, meta_tokens, rel_bias, norm1_gain, w_in, diff_q_gain, diff_k_gain, lam_q1, lam_k1, lam_q2, lam_k2, diff_subln_gain, swa_q_gain, swa_k_gain, swa_sinks, w_out, norm2_gain, w_group, b_group, w_router, b_router, w_gate, w_up, w_down):
    batch, seq, d = x.shape
    depth = w_in.shape[0]
    n = batch * seq
    assert seq % TQ == 0 and n % TM == 0 and n % TP == 0 and d == 1024
    assert meta_tokens.shape[0] == N_META
    assert depth == 1, "the meta-token rows of the residual stream are not carried across layers"

    h = x.reshape(n, d)
    dblk, bm0, bt = _bias_tables(rel_bias, TQ)
    scale = HEAD_DIM ** -0.5
    bd = jnp.asarray(np.kron(np.eye(MXU_DIM // HEAD_DIM), np.full((HEAD_DIM, HEAD_DIM), 1.0 / HEAD_DIM)), BF16)
    ones = jnp.ones((HEAD_DIM,), F32)
    lower_pad = N_EXPERTS + N_GROUPS

    for layer in range(depth):
        lambda_init = 0.8 - 0.6 * math.exp(-0.3 * layer)
        w_cat = w_in[layer].astype(BF16)
        gain = jnp.concatenate([
            jnp.tile(diff_q_gain[layer] * (scale * LOG2E), 2 * N_DIFF_HEADS),
            jnp.tile(diff_k_gain[layer], 2 * N_DIFF_HEADS),
            jnp.tile(ones, 2 * N_DIFF_HEADS),
            jnp.tile(swa_q_gain[layer] * scale, N_SWA_HEADS),
            jnp.tile(swa_k_gain[layer], N_SWA_KV),
            jnp.tile(ones, N_SWA_KV)]).reshape(1, C_END).astype(F32)
        nmask = np.zeros((1, C_END), np.float32)
        nmask[:, C_DQ:C_DV] = 1.0
        nmask[:, C_SQ:C_SV] = 1.0
        nmask = jnp.asarray(nmask)
        g1 = norm1_gain[layer].reshape(1, d).astype(F32)

        qkv = _proj(h, g1, w_cat, bd, gain, nmask, TP)
        qkv_meta = _proj(meta_tokens.astype(F32), g1, w_cat, bd, gain, nmask, N_META)
        meta_pad = jnp.pad(qkv_meta, ((0, TQ - N_META), (0, 0)))

        lamv = jnp.pad(jnp.stack([lam_q1[layer], lam_k1[layer], lam_q2[layer], lam_k2[layer]]).astype(F32),
                       ((0, 4), (0, LANES - HEAD_DIM)))
        mixd = _diff_attention(qkv, meta_pad[:, C_DK:C_DV], meta_pad[:, C_DV:C_SQ], dblk, bm0, lamv,
                               diff_subln_gain[layer].reshape(1, LANES).astype(F32), batch, seq, lambda_init)
        mixs = _swa_attention(swa_sinks[layer].astype(F32), qkv, meta_pad[:BLOCK, C_SK:C_SV],
                              meta_pad[:BLOCK, C_SV:C_END], jnp.swapaxes(bt, -1, -2), batch, seq)

        wr = jnp.pad(jnp.concatenate([w_router[layer], w_group[layer]], axis=1),
                     ((0, 0), (0, LANES - lower_pad))).astype(BF16)
        br = jnp.pad(jnp.concatenate([b_router[layer], b_group[layer]]), (0, LANES - lower_pad)).reshape(1, LANES)
        h1, hb, rt, tinfo, cnt = _outproj(h, mixd, mixs, w_out[layer].astype(BF16),
                                          norm2_gain[layer].reshape(1, d).astype(F32), wr, br.astype(F32))

        nt = n // TM
        counts = cnt[0, :N_EXPERTS].astype(jnp.int32)
        nblk_e = (counts + EB - 1) // EB
        blk_end = jnp.cumsum(nblk_e)
        pstart = ((blk_end - nblk_e) * EB).astype(jnp.int32)
        n_blocks = -(-(2 * n + nt * N_EXPERTS * (ROW_ALIGN - 1) + N_EXPERTS * (EB - 1)) // EB)
        blk_ids = jnp.arange(n_blocks)
        blk_e = jnp.minimum(jnp.sum(blk_end[None, :] <= blk_ids[:, None], axis=1), N_EXPERTS - 1).astype(jnp.int32)
        n_act = blk_end[-1:].astype(jnp.int32)
        is_last = jnp.any((blk_end[None, :] == blk_ids[:, None] + 1) & (nblk_e[None, :] > 0), axis=1)
        zero_blocks = jnp.where(blk_ids >= n_act[0], 2, jnp.where(is_last, 1, 0)).astype(jnp.int32)
        ti = tinfo.reshape(nt, 8, LANES)
        run_len = ti[:, 0, :N_EXPERTS].astype(jnp.int32)
        run_start = pstart[None, :] + ti[:, 1, :N_EXPERTS].astype(jnp.int32)
        run_groups = (run_len + ROW_ALIGN - 1) // ROW_ALIGN
        assert n_blocks * EB * SORTED_RANGE < 2 ** 31
        runs = _chunk_table(run_start, run_groups)

        xs = _dispatch(zero_blocks, runs, hb, rt, n_blocks * EB)
        own = jnp.where(nblk_e > 0, jnp.arange(N_EXPERTS), N_EXPERTS)
        later = jnp.concatenate([lax.cummin(own[::-1])[::-1][1:], jnp.full((1,), N_EXPERTS)])
        nxt_e = jnp.where(later < N_EXPERTS, later, -1).astype(jnp.int32)
        blk_first = blk_end - nblk_e
        owned = (blk_first[None, :] <= blk_ids[:, None]) & (blk_ids[:, None] < blk_end[None, :])
        blk_rows = jnp.sum(jnp.where(owned, jnp.minimum(counts[None, :] - (blk_ids[:, None] - blk_first[None, :]) * EB,
                                                        EB), 0), axis=1).astype(jnp.int32)
        ys = _experts(blk_e, n_act, nxt_e, blk_rows, xs, w_gate[layer], w_up[layer], w_down[layer])
        h = _combine(runs, ys, h1, rt)
    return h.reshape(batch, seq, d)
```
